```python
import jax, jax.numpy as jnp
from jax import lax
import numpy as np

D_MODEL = 1024
BATCH = 8
SEQ = 8192
DEPTH = 1

HEAD_DIM = 64
ATTN_Q_HEADS = 8
ATTN_KV_HEADS = 4
ATTN_GROUP = ATTN_Q_HEADS // ATTN_KV_HEADS
DILATED_PATTERNS = ((128, 1), (512, 4), (2048, 16))
ATTN_BLOCK = 128
ATTN_DIM = ATTN_Q_HEADS * HEAD_DIM
KV_DIM = ATTN_KV_HEADS * HEAD_DIM
SSM_HEADS = 16
SSM_HEAD_DIM = 64
SSM_INNER = SSM_HEADS * SSM_HEAD_DIM
SSM_GROUPS = 2
SSM_STATE = 128
SSM_CONV = 4
SSM_CHUNK = 128
BC_DIM = SSM_GROUPS * SSM_STATE
MIX_DIM = ATTN_DIM + SSM_INNER
IN_PROJ_DIM = ATTN_DIM + 2 * KV_DIM + 2 * SSM_INNER + 2 * BC_DIM + SSM_HEADS
D_FF = 2816
FFN_CONV = 3
PLE_DIM = 256
EPS = 1e-6

kernel_name = 'hybrid_dilated_attn_ssd_convffn_ple'


def rms_norm(x, g):
    xf = x.astype(jnp.float32)
    y = xf * lax.rsqrt(jnp.mean(xf * xf, axis=-1, keepdims=True) + EPS)
    return (y * g.astype(jnp.float32)).astype(x.dtype)


def causal_depthwise_conv(u, w):
    k_width, chans = w.shape
    return lax.conv_general_dilated(
        u, w[:, None, :].astype(u.dtype), window_strides=(1,), padding=[(k_width - 1, 0)],
        dimension_numbers=('NWC', 'WIO', 'NWC'), feature_group_count=chans)


def _dilated_pattern(qh, kh, vh, window, dilation):
    bsz, seq = qh.shape[:2]
    steps = window // dilation
    span = dilation * ATTN_BLOCK
    s_pad = -(-seq // span) * span
    nblk = s_pad // span

    def to_sub(t):
        t = jnp.pad(t, [(0, 0), (0, s_pad - seq)] + [(0, 0)] * (t.ndim - 2))
        t = t.reshape(bsz, s_pad // dilation, dilation, *t.shape[2:])
        t = jnp.moveaxis(t, 2, 1)
        return t.reshape(bsz, dilation, nblk, ATTN_BLOCK, *t.shape[3:])

    def from_sub(t):
        t = t.reshape(bsz, dilation, nblk * ATTN_BLOCK, *t.shape[4:])
        t = jnp.moveaxis(t, 1, 2).reshape(bsz, s_pad, *t.shape[3:])
        return t[:, :seq]

    qs, ks, vs = to_sub(qh), to_sub(kh), to_sub(vh)
    pad_blk = [(0, 0), (0, 0), (1, 0), (0, 0), (0, 0), (0, 0)]
    kk = jnp.concatenate([jnp.pad(ks, pad_blk)[:, :, :-1], ks], axis=3)
    vv = jnp.concatenate([jnp.pad(vs, pad_blk)[:, :, :-1], vs], axis=3)
    s = jnp.einsum('brnikge,brnjke->brnkgij', qs, kk)
    qi = jnp.arange(ATTN_BLOCK)[:, None]
    kj = jnp.arange(2 * ATTN_BLOCK)[None, :]
    delta = ATTN_BLOCK + qi - kj
    band = (delta >= 0) & (delta <= steps)
    valid = (jnp.arange(nblk)[:, None, None] > 0) | (kj[None] >= ATTN_BLOCK)
    mask = band[None] & valid
    s = jnp.where(mask[:, None, None], s, -jnp.inf)
    m = jnp.max(s, axis=-1)
    pexp = jnp.exp(s - m[..., None])
    l = jnp.sum(pexp, axis=-1)
    o = jnp.einsum('brnkgij,brnjke->brnikge', pexp, vv)
    m = jnp.moveaxis(m, -1, 3)
    l = jnp.moveaxis(l, -1, 3)
    return from_sub(o), from_sub(m), from_sub(l)


def dilated_attention(q, k, v, q_g, k_g):
    bsz, seq, _ = q.shape
    qh = rms_norm(q.reshape(bsz, seq, ATTN_KV_HEADS, ATTN_GROUP, HEAD_DIM), q_g).astype(jnp.float32)
    qh = qh * (HEAD_DIM ** -0.5)
    kh = rms_norm(k.reshape(bsz, seq, ATTN_KV_HEADS, HEAD_DIM), k_g).astype(jnp.float32)
    vh = v.reshape(bsz, seq, ATTN_KV_HEADS, HEAD_DIM).astype(jnp.float32)
    res = [_dilated_pattern(qh, kh, vh, w, d) for (w, d) in DILATED_PATTERNS]
    m_max = jnp.max(jnp.stack([r[1] for r in res]), axis=0)
    wts = [jnp.exp(r[1] - m_max) for r in res]
    num = sum(wi[..., None] * r[0] for wi, r in zip(wts, res))
    den = sum(wi * r[2] for wi, r in zip(wts, res))
    out = num / den[..., None]
    return out.reshape(bsz, seq, ATTN_DIM).astype(q.dtype)


def ssd_chunked(xdt, a, bm, cm):
    bsz, seq, nh, hp = xdt.shape
    nc, ln, ng, ns = seq // SSM_CHUNK, SSM_CHUNK, SSM_GROUPS, SSM_STATE
    ne = nh // ng
    xc = xdt.reshape(bsz, nc, ln, ng, ne, hp)
    ac = a.reshape(bsz, nc, ln, ng, ne)
    bc = bm.reshape(bsz, nc, ln, ng, ns)
    cc = cm.reshape(bsz, nc, ln, ng, ns)
    acum = jnp.cumsum(ac, axis=2)
    causal = jnp.tril(jnp.ones((ln, ln), dtype=bool))
    seg = acum[:, :, :, None] - acum[:, :, None]
    decay = jnp.exp(jnp.where(causal[:, :, None, None], seg, -jnp.inf))
    cb = jnp.einsum('bclgn,bcsgn->bclsg', cc, bc)
    y_diag = jnp.einsum('bclsge,bcsgep->bclgep', cb[..., None] * decay, xc)
    decay_to_end = jnp.exp(acum[:, :, -1:] - acum)
    chunk_states = jnp.einsum('bclgn,bclgep->bcgepn', bc, xc * decay_to_end[..., None])
    chunk_decay = jnp.exp(acum[:, :, -1])

    def step(h, inp):
        st, dec = inp
        return h * dec[..., None, None] + st, h

    h0 = jnp.zeros((bsz, ng, ne, hp, ns), jnp.float32)
    _, h_in = lax.scan(step, h0, (jnp.moveaxis(chunk_states, 1, 0), jnp.moveaxis(chunk_decay, 1, 0)))
    h_in = jnp.moveaxis(h_in, 0, 1)
    y_off = jnp.einsum('bclgn,bcgepn->bclgep', cc, h_in) * jnp.exp(acum)[..., None]
    return (y_diag + y_off).reshape(bsz, seq, nh, hp)


def ssd_mixer(z, xbc, dt_raw, conv_w, conv_b, dt_bias, a_log, d_skip, norm_g):
    bsz, seq, _ = z.shape
    xbc = jax.nn.silu(causal_depthwise_conv(xbc, conv_w) + conv_b)
    xs, bm, cm = jnp.split(xbc, [SSM_INNER, SSM_INNER + BC_DIM], axis=-1)
    xs = xs.reshape(bsz, seq, SSM_HEADS, SSM_HEAD_DIM).astype(jnp.float32)
    bm = bm.reshape(bsz, seq, SSM_GROUPS, SSM_STATE).astype(jnp.float32)
    cm = cm.reshape(bsz, seq, SSM_GROUPS, SSM_STATE).astype(jnp.float32)
    dt = jax.nn.softplus(dt_raw.astype(jnp.float32) + dt_bias.astype(jnp.float32))
    a = -jnp.exp(a_log.astype(jnp.float32))
    y = ssd_chunked(xs * dt[..., None], dt * a, bm, cm) + d_skip.astype(jnp.float32)[:, None] * xs
    y = y.reshape(bsz, seq, SSM_INNER) * jax.nn.silu(z.astype(jnp.float32))
    return rms_norm(y, norm_g).astype(z.dtype)


def conv_ffn(x, norm_g, w_up, conv_w, conv_b, w_down):
    u = rms_norm(x, norm_g) @ w_up
    u = causal_depthwise_conv(u, conv_w) + conv_b
    gate, val = jnp.split(u, 2, axis=-1)
    return (jax.nn.silu(gate) * val) @ w_down


def _fwd_setup_inputs(seed: int = 0) -> dict:
    key = jax.random.key(seed)
    ks = jax.random.split(key, 24)
    f32 = jnp.float32

    def nrm(k, shape, fan_in):
        return jax.random.normal(k, shape, f32) * (fan_in ** -0.5)

    def gain(k, shape):
        return 1.0 + 0.02 * jax.random.normal(k, shape, f32)

    dt0 = jnp.exp(jax.random.uniform(ks[8], (DEPTH, SSM_HEADS), f32) * (np.log(0.1) - np.log(0.001)) + np.log(0.001))
    return {
        'x': jax.random.normal(ks[0], (BATCH, SEQ, D_MODEL), f32),
        'p': jax.random.normal(ks[1], (DEPTH, BATCH, SEQ, PLE_DIM), f32),
        'attn_norm_g': gain(ks[2], (DEPTH, D_MODEL)),
        'w_in': nrm(ks[3], (DEPTH, D_MODEL, IN_PROJ_DIM), D_MODEL),
        'q_norm_g': gain(ks[4], (DEPTH, HEAD_DIM)),
        'k_norm_g': gain(ks[5], (DEPTH, HEAD_DIM)),
        'ssm_conv_w': nrm(ks[6], (DEPTH, SSM_CONV, SSM_INNER + 2 * BC_DIM), SSM_CONV),
        'ssm_conv_b': 0.02 * jax.random.normal(ks[7], (DEPTH, SSM_INNER + 2 * BC_DIM), f32),
        'dt_bias': dt0 + jnp.log(-jnp.expm1(-dt0)),
        'a_log': jnp.log(jax.random.uniform(ks[9], (DEPTH, SSM_HEADS), f32, 1.0, 16.0)),
        'd_skip': 1.0 + 0.1 * jax.random.normal(ks[10], (DEPTH, SSM_HEADS), f32),
        'ssm_norm_g': gain(ks[11], (DEPTH, SSM_INNER)),
        'w_out': nrm(ks[12], (DEPTH, MIX_DIM, D_MODEL), MIX_DIM),
        'ffn_norm_g': gain(ks[13], (DEPTH, D_MODEL)),
        'w_up': nrm(ks[14], (DEPTH, D_MODEL, 2 * D_FF), D_MODEL),
        'ffn_conv_w': nrm(ks[15], (DEPTH, FFN_CONV, 2 * D_FF), FFN_CONV),
        'ffn_conv_b': 0.02 * jax.random.normal(ks[16], (DEPTH, 2 * D_FF), f32),
        'w_down': nrm(ks[17], (DEPTH, D_FF, D_MODEL), D_FF),
        'ple_norm_g': gain(ks[18], (DEPTH, D_MODEL)),
        'w_ple_gate': nrm(ks[19], (DEPTH, D_MODEL, D_MODEL), D_MODEL),
        'w_ple_proj': nrm(ks[20], (DEPTH, PLE_DIM, D_MODEL), PLE_DIM),
    }


def _fwd_reference(x, p, attn_norm_g, w_in, q_norm_g, k_norm_g, ssm_conv_w, ssm_conv_b, dt_bias, a_log,
              d_skip, ssm_norm_g, w_out, ffn_norm_g, w_up, ffn_conv_w, ffn_conv_b, w_down,
              ple_norm_g, w_ple_gate, w_ple_proj):
    splits = [ATTN_DIM, ATTN_DIM + KV_DIM, ATTN_DIM + 2 * KV_DIM, ATTN_DIM + 2 * KV_DIM + SSM_INNER,
              ATTN_DIM + 2 * KV_DIM + 2 * SSM_INNER + 2 * BC_DIM]
    for i in range(DEPTH):
        h = rms_norm(x, attn_norm_g[i])
        proj = h @ w_in[i]
        q, k, v, z, xbc, dt_raw = jnp.split(proj, splits, axis=-1)
        attn_out = dilated_attention(q, k, v, q_norm_g[i], k_norm_g[i])
        ssm_out = ssd_mixer(z, xbc, dt_raw, ssm_conv_w[i], ssm_conv_b[i], dt_bias[i],
                            a_log[i], d_skip[i], ssm_norm_g[i])
        x = x + jnp.concatenate([attn_out, ssm_out], axis=-1) @ w_out[i]
        x = x + conv_ffn(x, ffn_norm_g[i], w_up[i], ffn_conv_w[i], ffn_conv_b[i], w_down[i])
        gate = jax.nn.sigmoid(rms_norm(x, ple_norm_g[i]) @ w_ple_gate[i])
        x = x + gate * (p[i] @ w_ple_proj[i])
    return x


import jax as _jax
import jax.numpy as _jnp

TWIN_FORMAT = 'train_step'
FWD_PARAMS = ['x', 'p', 'attn_norm_g', 'w_in', 'q_norm_g', 'k_norm_g', 'ssm_conv_w', 'ssm_conv_b', 'dt_bias', 'a_log', 'd_skip', 'ssm_norm_g', 'w_out', 'ffn_norm_g', 'w_up', 'ffn_conv_w', 'ffn_conv_b', 'w_down', 'ple_norm_g', 'w_ple_gate', 'w_ple_proj']
TWIN_WEIGHTS = ['attn_norm_g', 'w_in', 'q_norm_g', 'k_norm_g', 'ssm_conv_w', 'ssm_conv_b', 'dt_bias', 'a_log', 'd_skip', 'ssm_norm_g', 'w_out', 'ffn_norm_g', 'w_up', 'ffn_conv_w', 'ffn_conv_b', 'w_down', 'ple_norm_g', 'w_ple_gate', 'w_ple_proj']
TWIN_DIFF_INPUT = 'x'
TWIN_INPUTS = ['x', 'p', 'attn_norm_g', 'w_in', 'q_norm_g', 'k_norm_g', 'ssm_conv_w', 'ssm_conv_b', 'dt_bias', 'a_log', 'd_skip', 'ssm_norm_g', 'w_out', 'ffn_norm_g', 'w_up', 'ffn_conv_w', 'ffn_conv_b', 'w_down', 'ple_norm_g', 'w_ple_gate', 'w_ple_proj', 'loss_target', 'm_attn_norm_g', 'm_w_in', 'm_q_norm_g', 'm_k_norm_g', 'm_ssm_conv_w', 'm_ssm_conv_b', 'm_dt_bias', 'm_a_log', 'm_d_skip', 'm_ssm_norm_g', 'm_w_out', 'm_ffn_norm_g', 'm_w_up', 'm_ffn_conv_w', 'm_ffn_conv_b', 'm_w_down', 'm_ple_norm_g', 'm_w_ple_gate', 'm_w_ple_proj', 'v_attn_norm_g', 'v_w_in', 'v_q_norm_g', 'v_k_norm_g', 'v_ssm_conv_w', 'v_ssm_conv_b', 'v_dt_bias', 'v_a_log', 'v_d_skip', 'v_ssm_norm_g', 'v_w_out', 'v_ffn_norm_g', 'v_w_up', 'v_ffn_conv_w', 'v_ffn_conv_b', 'v_w_down', 'v_ple_norm_g', 'v_w_ple_gate', 'v_w_ple_proj']
TWIN_OUTPUTS = ['loss', 'grad_x', 'grad_attn_norm_g', 'grad_w_in', 'grad_q_norm_g', 'grad_k_norm_g', 'grad_ssm_conv_w', 'grad_ssm_conv_b', 'grad_dt_bias', 'grad_a_log', 'grad_d_skip', 'grad_ssm_norm_g', 'grad_w_out', 'grad_ffn_norm_g', 'grad_w_up', 'grad_ffn_conv_w', 'grad_ffn_conv_b', 'grad_w_down', 'grad_ple_norm_g', 'grad_w_ple_gate', 'grad_w_ple_proj', 'delta_attn_norm_g', 'delta_w_in', 'delta_q_norm_g', 'delta_k_norm_g', 'delta_ssm_conv_w', 'delta_ssm_conv_b', 'delta_dt_bias', 'delta_a_log', 'delta_d_skip', 'delta_ssm_norm_g', 'delta_w_out', 'delta_ffn_norm_g', 'delta_w_up', 'delta_ffn_conv_w', 'delta_ffn_conv_b', 'delta_w_down', 'delta_ple_norm_g', 'delta_w_ple_gate', 'delta_w_ple_proj', 'new_m_attn_norm_g', 'new_m_w_in', 'new_m_q_norm_g', 'new_m_k_norm_g', 'new_m_ssm_conv_w', 'new_m_ssm_conv_b', 'new_m_dt_bias', 'new_m_a_log', 'new_m_d_skip', 'new_m_ssm_norm_g', 'new_m_w_out', 'new_m_ffn_norm_g', 'new_m_w_up', 'new_m_ffn_conv_w', 'new_m_ffn_conv_b', 'new_m_w_down', 'new_m_ple_norm_g', 'new_m_w_ple_gate', 'new_m_w_ple_proj', 'new_v_attn_norm_g', 'new_v_w_in', 'new_v_q_norm_g', 'new_v_k_norm_g', 'new_v_ssm_conv_w', 'new_v_ssm_conv_b', 'new_v_dt_bias', 'new_v_a_log', 'new_v_d_skip', 'new_v_ssm_norm_g', 'new_v_w_out', 'new_v_ffn_norm_g', 'new_v_w_up', 'new_v_ffn_conv_w', 'new_v_ffn_conv_b', 'new_v_w_down', 'new_v_ple_norm_g', 'new_v_w_ple_gate', 'new_v_w_ple_proj']
TWIN_LEAF_KINDS = {'loss': 'loss', 'grad_x': 'grad_x', 'grad_attn_norm_g': 'grad_w', 'grad_w_in': 'grad_w', 'grad_q_norm_g': 'grad_w', 'grad_k_norm_g': 'grad_w', 'grad_ssm_conv_w': 'grad_w', 'grad_ssm_conv_b': 'grad_w', 'grad_dt_bias': 'grad_w', 'grad_a_log': 'grad_w', 'grad_d_skip': 'grad_w', 'grad_ssm_norm_g': 'grad_w', 'grad_w_out': 'grad_w', 'grad_ffn_norm_g': 'grad_w', 'grad_w_up': 'grad_w', 'grad_ffn_conv_w': 'grad_w', 'grad_ffn_conv_b': 'grad_w', 'grad_w_down': 'grad_w', 'grad_ple_norm_g': 'grad_w', 'grad_w_ple_gate': 'grad_w', 'grad_w_ple_proj': 'grad_w', 'delta_attn_norm_g': 'delta_w', 'delta_w_in': 'delta_w', 'delta_q_norm_g': 'delta_w', 'delta_k_norm_g': 'delta_w', 'delta_ssm_conv_w': 'delta_w', 'delta_ssm_conv_b': 'delta_w', 'delta_dt_bias': 'delta_w', 'delta_a_log': 'delta_w', 'delta_d_skip': 'delta_w', 'delta_ssm_norm_g': 'delta_w', 'delta_w_out': 'delta_w', 'delta_ffn_norm_g': 'delta_w', 'delta_w_up': 'delta_w', 'delta_ffn_conv_w': 'delta_w', 'delta_ffn_conv_b': 'delta_w', 'delta_w_down': 'delta_w', 'delta_ple_norm_g': 'delta_w', 'delta_w_ple_gate': 'delta_w', 'delta_w_ple_proj': 'delta_w', 'new_m_attn_norm_g': 'new_m', 'new_m_w_in': 'new_m', 'new_m_q_norm_g': 'new_m', 'new_m_k_norm_g': 'new_m', 'new_m_ssm_conv_w': 'new_m', 'new_m_ssm_conv_b': 'new_m', 'new_m_dt_bias': 'new_m', 'new_m_a_log': 'new_m', 'new_m_d_skip': 'new_m', 'new_m_ssm_norm_g': 'new_m', 'new_m_w_out': 'new_m', 'new_m_ffn_norm_g': 'new_m', 'new_m_w_up': 'new_m', 'new_m_ffn_conv_w': 'new_m', 'new_m_ffn_conv_b': 'new_m', 'new_m_w_down': 'new_m', 'new_m_ple_norm_g': 'new_m', 'new_m_w_ple_gate': 'new_m', 'new_m_w_ple_proj': 'new_m', 'new_v_attn_norm_g': 'new_v', 'new_v_w_in': 'new_v', 'new_v_q_norm_g': 'new_v', 'new_v_k_norm_g': 'new_v', 'new_v_ssm_conv_w': 'new_v', 'new_v_ssm_conv_b': 'new_v', 'new_v_dt_bias': 'new_v', 'new_v_a_log': 'new_v', 'new_v_d_skip': 'new_v', 'new_v_ssm_norm_g': 'new_v', 'new_v_w_out': 'new_v', 'new_v_ffn_norm_g': 'new_v', 'new_v_w_up': 'new_v', 'new_v_ffn_conv_w': 'new_v', 'new_v_ffn_conv_b': 'new_v', 'new_v_w_down': 'new_v', 'new_v_ple_norm_g': 'new_v', 'new_v_w_ple_gate': 'new_v', 'new_v_w_ple_proj': 'new_v'}


def _forward(args):
    return _fwd_reference(*[args[k] for k in FWD_PARAMS])


def _output_shape():
    def fwd():
        inp = _fwd_setup_inputs(0)
        return _fwd_reference(*[inp[k] for k in FWD_PARAMS])
    out = _jax.eval_shape(fwd)
    return out.shape, out.dtype

N_MICROBATCH = 1
ADAM_LR = 0.001
ADAM_B1 = 0.9
ADAM_B2 = 0.999
ADAM_EPS = 1e-08
ADAM_WD = 0.01
ADAM_STEP = 10
PER_EXAMPLE_BATCH_AXIS = {'x': 0, 'p': 1, 'loss_target': 0}
SHARED_INPUTS = []
_WEIGHT_DTYPES = {'attn_norm_g': _jnp.float32, 'w_in': _jnp.float32, 'q_norm_g': _jnp.float32, 'k_norm_g': _jnp.float32, 'ssm_conv_w': _jnp.float32, 'ssm_conv_b': _jnp.float32, 'dt_bias': _jnp.float32, 'a_log': _jnp.float32, 'd_skip': _jnp.float32, 'ssm_norm_g': _jnp.float32, 'w_out': _jnp.float32, 'ffn_norm_g': _jnp.float32, 'w_up': _jnp.float32, 'ffn_conv_w': _jnp.float32, 'ffn_conv_b': _jnp.float32, 'w_down': _jnp.float32, 'ple_norm_g': _jnp.float32, 'w_ple_gate': _jnp.float32, 'w_ple_proj': _jnp.float32}
MOMENT_SCALE = {'attn_norm_g': 1.118310e+00, 'w_in': 5.099481e-01, 'q_norm_g': 2.344894e+00, 'k_norm_g': 2.332499e+00, 'ssm_conv_w': 1.908591e+00, 'ssm_conv_b': 6.570828e+00, 'dt_bias': 3.479151e+00, 'a_log': 1.154163e+01, 'd_skip': 2.232605e+01, 'ssm_norm_g': 6.300172e+01, 'w_out': 3.760238e+00, 'ffn_norm_g': 5.374446e+01, 'w_up': 1.183496e+00, 'ffn_conv_w': 7.806241e+00, 'ffn_conv_b': 7.041670e+00, 'w_down': 7.730726e-01, 'ple_norm_g': 2.087715e+00, 'w_ple_gate': 6.733966e-01, 'w_ple_proj': 8.781108e-01}


def _to_microbatches(a, axis):
    t = _jnp.moveaxis(a, axis, 0)
    t = t.reshape((N_MICROBATCH, t.shape[0] // N_MICROBATCH) + t.shape[1:])
    return _jnp.moveaxis(t, 1, axis + 1)


def setup_inputs(seed: int = 0) -> dict:
    inp = _fwd_setup_inputs(seed)
    key = _jax.random.fold_in(_jax.random.key(seed), 7919)
    shape, _ = _output_shape()
    out = dict(inp)
    out["loss_target"] = _jax.random.normal(_jax.random.fold_in(key, 0), shape, _jnp.float32)
    for i, name in enumerate(TWIN_WEIGHTS):
        w = inp[name].astype(_jnp.float32)
        if MOMENT_SCALE is None:
            s = _jnp.sqrt(_jnp.mean(_jnp.square(w)) + 1e-30)
        else:
            s = MOMENT_SCALE[name]
        km, kv = _jax.random.split(_jax.random.fold_in(key, i + 1))
        out[name] = w
        out["m_" + name] = s * _jax.random.normal(km, w.shape, _jnp.float32)
        out["v_" + name] = (s * s) * _jax.random.uniform(kv, w.shape, _jnp.float32, 0.5, 1.5)
    if N_MICROBATCH > 1:
        for name, axis in PER_EXAMPLE_BATCH_AXIS.items():
            out[name] = _to_microbatches(out[name], axis)
    return {'x': out['x'], 'p': out['p'], 'attn_norm_g': out['attn_norm_g'], 'w_in': out['w_in'], 'q_norm_g': out['q_norm_g'], 'k_norm_g': out['k_norm_g'], 'ssm_conv_w': out['ssm_conv_w'], 'ssm_conv_b': out['ssm_conv_b'], 'dt_bias': out['dt_bias'], 'a_log': out['a_log'], 'd_skip': out['d_skip'], 'ssm_norm_g': out['ssm_norm_g'], 'w_out': out['w_out'], 'ffn_norm_g': out['ffn_norm_g'], 'w_up': out['w_up'], 'ffn_conv_w': out['ffn_conv_w'], 'ffn_conv_b': out['ffn_conv_b'], 'w_down': out['w_down'], 'ple_norm_g': out['ple_norm_g'], 'w_ple_gate': out['w_ple_gate'], 'w_ple_proj': out['w_ple_proj'], 'loss_target': out['loss_target'], 'm_attn_norm_g': out['m_attn_norm_g'], 'm_w_in': out['m_w_in'], 'm_q_norm_g': out['m_q_norm_g'], 'm_k_norm_g': out['m_k_norm_g'], 'm_ssm_conv_w': out['m_ssm_conv_w'], 'm_ssm_conv_b': out['m_ssm_conv_b'], 'm_dt_bias': out['m_dt_bias'], 'm_a_log': out['m_a_log'], 'm_d_skip': out['m_d_skip'], 'm_ssm_norm_g': out['m_ssm_norm_g'], 'm_w_out': out['m_w_out'], 'm_ffn_norm_g': out['m_ffn_norm_g'], 'm_w_up': out['m_w_up'], 'm_ffn_conv_w': out['m_ffn_conv_w'], 'm_ffn_conv_b': out['m_ffn_conv_b'], 'm_w_down': out['m_w_down'], 'm_ple_norm_g': out['m_ple_norm_g'], 'm_w_ple_gate': out['m_w_ple_gate'], 'm_w_ple_proj': out['m_w_ple_proj'], 'v_attn_norm_g': out['v_attn_norm_g'], 'v_w_in': out['v_w_in'], 'v_q_norm_g': out['v_q_norm_g'], 'v_k_norm_g': out['v_k_norm_g'], 'v_ssm_conv_w': out['v_ssm_conv_w'], 'v_ssm_conv_b': out['v_ssm_conv_b'], 'v_dt_bias': out['v_dt_bias'], 'v_a_log': out['v_a_log'], 'v_d_skip': out['v_d_skip'], 'v_ssm_norm_g': out['v_ssm_norm_g'], 'v_w_out': out['v_w_out'], 'v_ffn_norm_g': out['v_ffn_norm_g'], 'v_w_up': out['v_w_up'], 'v_ffn_conv_w': out['v_ffn_conv_w'], 'v_ffn_conv_b': out['v_ffn_conv_b'], 'v_w_down': out['v_w_down'], 'v_ple_norm_g': out['v_ple_norm_g'], 'v_w_ple_gate': out['v_w_ple_gate'], 'v_w_ple_proj': out['v_w_ple_proj']}


def _loss(weights, diff, rest, loss_target):
    with _jax.named_scope("forward"):
        args = {**rest, TWIN_DIFF_INPUT: diff, **{k: w.astype(_WEIGHT_DTYPES[k]) for k, w in weights.items()}}
        y = _forward(args)
    with _jax.named_scope("loss_head"):
        err = _jnp.square(y.astype(_jnp.float32) - loss_target)
        return 0.5 * _jnp.sum(_jnp.mean(err, axis=-1)) if err.ndim else 0.5 * err


def _adamw(w, g, m, v):
    m = ADAM_B1 * m + (1.0 - ADAM_B1) * g
    v = ADAM_B2 * v + (1.0 - ADAM_B2) * _jnp.square(g)
    m_hat = m / (1.0 - ADAM_B1 ** ADAM_STEP)
    v_hat = v / (1.0 - ADAM_B2 ** ADAM_STEP)
    delta = -ADAM_LR * (m_hat / (_jnp.sqrt(v_hat) + ADAM_EPS) + ADAM_WD * w)
    return delta, m, v


def reference(x, p, attn_norm_g, w_in, q_norm_g, k_norm_g, ssm_conv_w, ssm_conv_b, dt_bias, a_log, d_skip, ssm_norm_g, w_out, ffn_norm_g, w_up, ffn_conv_w, ffn_conv_b, w_down, ple_norm_g, w_ple_gate, w_ple_proj, loss_target, m_attn_norm_g, m_w_in, m_q_norm_g, m_k_norm_g, m_ssm_conv_w, m_ssm_conv_b, m_dt_bias, m_a_log, m_d_skip, m_ssm_norm_g, m_w_out, m_ffn_norm_g, m_w_up, m_ffn_conv_w, m_ffn_conv_b, m_w_down, m_ple_norm_g, m_w_ple_gate, m_w_ple_proj, v_attn_norm_g, v_w_in, v_q_norm_g, v_k_norm_g, v_ssm_conv_w, v_ssm_conv_b, v_dt_bias, v_a_log, v_d_skip, v_ssm_norm_g, v_w_out, v_ffn_norm_g, v_w_up, v_ffn_conv_w, v_ffn_conv_b, v_w_down, v_ple_norm_g, v_w_ple_gate, v_w_ple_proj):
    given = dict(x=x, p=p, attn_norm_g=attn_norm_g, w_in=w_in, q_norm_g=q_norm_g, k_norm_g=k_norm_g, ssm_conv_w=ssm_conv_w, ssm_conv_b=ssm_conv_b, dt_bias=dt_bias, a_log=a_log, d_skip=d_skip, ssm_norm_g=ssm_norm_g, w_out=w_out, ffn_norm_g=ffn_norm_g, w_up=w_up, ffn_conv_w=ffn_conv_w, ffn_conv_b=ffn_conv_b, w_down=w_down, ple_norm_g=ple_norm_g, w_ple_gate=w_ple_gate, w_ple_proj=w_ple_proj, loss_target=loss_target, m_attn_norm_g=m_attn_norm_g, m_w_in=m_w_in, m_q_norm_g=m_q_norm_g, m_k_norm_g=m_k_norm_g, m_ssm_conv_w=m_ssm_conv_w, m_ssm_conv_b=m_ssm_conv_b, m_dt_bias=m_dt_bias, m_a_log=m_a_log, m_d_skip=m_d_skip, m_ssm_norm_g=m_ssm_norm_g, m_w_out=m_w_out, m_ffn_norm_g=m_ffn_norm_g, m_w_up=m_w_up, m_ffn_conv_w=m_ffn_conv_w, m_ffn_conv_b=m_ffn_conv_b, m_w_down=m_w_down, m_ple_norm_g=m_ple_norm_g, m_w_ple_gate=m_w_ple_gate, m_w_ple_proj=m_w_ple_proj, v_attn_norm_g=v_attn_norm_g, v_w_in=v_w_in, v_q_norm_g=v_q_norm_g, v_k_norm_g=v_k_norm_g, v_ssm_conv_w=v_ssm_conv_w, v_ssm_conv_b=v_ssm_conv_b, v_dt_bias=v_dt_bias, v_a_log=v_a_log, v_d_skip=v_d_skip, v_ssm_norm_g=v_ssm_norm_g, v_w_out=v_w_out, v_ffn_norm_g=v_ffn_norm_g, v_w_up=v_w_up, v_ffn_conv_w=v_ffn_conv_w, v_ffn_conv_b=v_ffn_conv_b, v_w_down=v_w_down, v_ple_norm_g=v_ple_norm_g, v_w_ple_gate=v_w_ple_gate, v_w_ple_proj=v_w_ple_proj)
    weights = {n: given[n] for n in TWIN_WEIGHTS}
    shared = {n: given[n] for n in SHARED_INPUTS}
    per_example = {n: given[n] for n in ['x', 'p']}
    grad_fn = _jax.value_and_grad(_loss, argnums=(0, 1))

    def one_microbatch(ex, loss_target):
        ex = dict(ex)
        diff = ex.pop(TWIN_DIFF_INPUT)
        return grad_fn(weights, diff, {**shared, **ex}, loss_target)

    if N_MICROBATCH == 1:
        loss, (grad_w, grad_x) = one_microbatch(per_example, given["loss_target"])
    else:
        def body(carry, xs):
            loss_sum, grad_sum = carry
            l_k, (gw_k, gx_k) = one_microbatch(xs[0], xs[1])
            with _jax.named_scope("update"):
                return (loss_sum + l_k, _jax.tree.map(_jnp.add, grad_sum, gw_k)), gx_k

        init = (_jnp.zeros((), _jnp.float32), _jax.tree.map(_jnp.zeros_like, weights))
        (loss, grad_w), grad_x = _jax.lax.scan(body, init, (per_example, given["loss_target"]))
    with _jax.named_scope("update"):
        delta_w, new_m, new_v = {}, {}, {}
        for n in TWIN_WEIGHTS:
            delta_w[n], new_m[n], new_v[n] = _adamw(weights[n], grad_w[n], given["m_" + n], given["v_" + n])
    return (loss, grad_x, *[grad_w[n] for n in TWIN_WEIGHTS], *[delta_w[n] for n in TWIN_WEIGHTS],
            *[new_m[n] for n in TWIN_WEIGHTS], *[new_v[n] for n in TWIN_WEIGHTS])
```

```python
import functools

import numpy as np
import jax
import jax.numpy as jnp
from jax import lax
from jax.experimental import pallas as pl
from jax.experimental.pallas import tpu as pltpu

F32 = jnp.float32
BF16 = jnp.bfloat16
SDS = jax.ShapeDtypeStruct
EPS = 1e-6
N_DEV = 8
D_MODEL = 1024
HEAD_DIM = 64
ATTN_DIM = 512
KV_DIM = 256
SSM_INNER = 1024
SSM_HEADS = 16
BC_DIM = 256
XBC_DIM = SSM_INNER + 2 * BC_DIM
MIX_DIM = ATTN_DIM + SSM_INNER
IN_PROJ = 3600
IN_PROJ_PAD = 3840
D_FF = 2816
PLE_DIM = 256
CHUNK = 128
SUPER = 2048
DILATIONS = (1, 4, 16)
VMEM_LIMIT = 56 * 1024 * 1024
ADAM_LR, ADAM_B1, ADAM_B2, ADAM_EPS, ADAM_WD, ADAM_STEP = 0.001, 0.9, 0.999, 1e-08, 0.01, 10

NT = (((1,), (1,)), ((), ()))
TN = (((0,), (0,)), ((), ()))
HI = lax.Precision.HIGHEST


def _params(*sem):
    return pltpu.CompilerParams(dimension_semantics=sem if sem else None, vmem_limit_bytes=VMEM_LIMIT)


def _dot(a, b, dims=None):
    if dims is None:
        return jnp.dot(a, b, preferred_element_type=F32)
    return lax.dot_general(a, b, dims, preferred_element_type=F32)


def _hdot(a, b):
    return jnp.dot(a, b, precision=HI, preferred_element_type=F32)


def _sigmoid(x):
    return 1.0 / (1.0 + jnp.exp(-x))


def _shift_down(x, halo8, s):
    if s == 0:
        return x
    n = x.shape[0]
    row = lax.broadcasted_iota(jnp.int32, x.shape, 0)
    return jnp.where(row < s, jnp.tile(pltpu.roll(halo8, s, 0), (n // 8, 1)), pltpu.roll(x, s, 0))


def _shift_up(x, halo8, s):
    if s == 0:
        return x
    n = x.shape[0]
    row = lax.broadcasted_iota(jnp.int32, x.shape, 0)
    return jnp.where(row >= n - s, jnp.tile(pltpu.roll(halo8, 8 - s, 0), (n // 8, 1)), pltpu.roll(x, n - s, 0))


def _norm_matmul(x, g, wt, tm, tn, name):
    t, k = x.shape
    n = wt.shape[0]

    def body(x_ref, g_ref, w_ref, o_ref, h_ref):
        @pl.when(pl.program_id(1) == 0)
        def _():
            xv = x_ref[...]
            r = lax.rsqrt(jnp.mean(xv * xv, axis=-1, keepdims=True) + EPS)
            h_ref[...] = (xv * r * g_ref[...]).astype(BF16)
        o_ref[...] = _dot(h_ref[...], w_ref[...], NT)

    return pl.pallas_call(
        body, name=name, grid=(t // tm, n // tn),
        in_specs=[pl.BlockSpec((tm, k), lambda i, j: (i, 0)), pl.BlockSpec((1, k), lambda i, j: (0, 0)),
                  pl.BlockSpec((tn, k), lambda i, j: (j, 0))],
        out_specs=[pl.BlockSpec((tm, tn), lambda i, j: (i, j)), pl.BlockSpec((tm, k), lambda i, j: (i, 0))],
        out_shape=[SDS((t, n), F32), SDS((t, k), BF16)],
        compiler_params=_params("parallel", "arbitrary"))(x, g, wt)


def _a_spec(a, lead, tm):
    if lead is None:
        return pl.BlockSpec((tm, a.shape[-1]), lambda i: (i, 0))
    return pl.BlockSpec((None, tm, a.shape[-1]), lambda i, _l=lead: (_l, i, 0))


def _mm_resid(pairs, res, dims, tm, out_dtype, name):
    t = pairs[0][0].shape[-2]
    n = pairs[0][2].shape[1] if dims is None else pairs[0][2].shape[0]
    np_ = len(pairs)

    def body(*refs):
        o_ref = refs[-1]
        acc = refs[2 * np_][...] if res is not None else None
        for q in range(np_):
            d = _dot(refs[q][...].astype(BF16), refs[np_ + q][...], dims)
            acc = d if acc is None else acc + d
        o_ref[...] = acc.astype(out_dtype)

    in_specs = [_a_spec(a, lead, tm) for a, lead, _ in pairs]
    in_specs += [pl.BlockSpec(b.shape, lambda i: (0, 0)) for _, _, b in pairs]
    args = [a for a, _, _ in pairs] + [b for _, _, b in pairs]
    if res is not None:
        in_specs.append(pl.BlockSpec((tm, n), lambda i: (i, 0)))
        args.append(res)
    return pl.pallas_call(
        body, name=name, grid=(t // tm,), in_specs=in_specs,
        out_specs=pl.BlockSpec((tm, n), lambda i: (i, 0)), out_shape=SDS((t, n), out_dtype),
        compiler_params=_params("parallel"))(*args)


def _mm_normbwd(pairs, x, g, dres, tm, name):
    t, k = x.shape
    np_ = len(pairs)

    def body(*refs):
        x_ref, g_ref, dres_ref, dx_ref, dg_ref = refs[2 * np_:]
        dh = None
        for q in range(np_):
            d = _dot(refs[q][...], refs[np_ + q][...])
            dh = d if dh is None else dh + d
        xv = x_ref[...]
        r = lax.rsqrt(jnp.mean(xv * xv, axis=-1, keepdims=True) + EPS)
        xh = xv * r

        @pl.when(pl.program_id(0) == 0)
        def _():
            dg_ref[...] = jnp.zeros_like(dg_ref)
        dg_ref[...] += jnp.sum(dh * xh, axis=0, keepdims=True)
        gd = dh * g_ref[...]
        dx_ref[...] = dres_ref[...] + r * (gd - xh * jnp.mean(gd * xh, axis=-1, keepdims=True))

    in_specs = [_a_spec(a, lead, tm) for a, lead, _ in pairs]
    in_specs += [pl.BlockSpec(b.shape, lambda i: (0, 0)) for _, _, b in pairs]
    in_specs += [pl.BlockSpec((tm, k), lambda i: (i, 0)), pl.BlockSpec((1, k), lambda i: (0, 0)),
                 pl.BlockSpec((tm, k), lambda i: (i, 0))]
    args = [a for a, _, _ in pairs] + [b for _, _, b in pairs] + [x, g, dres]
    return pl.pallas_call(
        body, name=name, grid=(t // tm,), in_specs=in_specs,
        out_specs=[pl.BlockSpec((tm, k), lambda i: (i, 0)), pl.BlockSpec((1, k), lambda i: (0, 0))],
        out_shape=[SDS((t, k), F32), SDS((1, k), F32)],
        compiler_params=_params("arbitrary"))(*args)


def _wgrad(a, a_lead, b, name, tk=512):
    t, m = a.shape[-2:]
    n = b.shape[1]
    tm = m if m <= 1024 else 1408
    assert m % tm == 0

    def body(a_ref, b_ref, o_ref):
        @pl.when(pl.program_id(1) == 0)
        def _():
            o_ref[...] = jnp.zeros_like(o_ref)
        o_ref[...] += _dot(a_ref[...].astype(BF16), b_ref[...].astype(BF16), TN)

    if a_lead is None:
        a_spec = pl.BlockSpec((tk, tm), lambda mi, ki: (ki, mi))
    else:
        a_spec = pl.BlockSpec((None, tk, tm), lambda mi, ki, _l=a_lead: (_l, ki, mi))
    return pl.pallas_call(
        body, name=name, grid=(m // tm, t // tk),
        in_specs=[a_spec, pl.BlockSpec((tk, n), lambda mi, ki: (ki, 0))],
        out_specs=pl.BlockSpec((tm, n), lambda mi, ki: (mi, 0)), out_shape=SDS((m, n), F32),
        compiler_params=_params("parallel", "arbitrary"))(a, b)


def _head_consts():
    iq = np.arange(ATTN_DIM)
    ik = np.arange(KV_DIM)
    ones_q = (iq[:, None] // HEAD_DIM == iq[None, :] // HEAD_DIM).astype(np.float32)
    ones_k = (ik[:, None] // HEAD_DIM == ik[None, :] // HEAD_DIM).astype(np.float32)
    dup = (ik[:, None] == (HEAD_DIM * (iq[None, :] // 128) + iq[None, :] % HEAD_DIM)).astype(np.float32)
    return jnp.asarray(ones_q), jnp.asarray(ones_k), jnp.asarray(dup, BF16), jnp.asarray(dup.T)


def _attn_prep(proj, gq, gk, ones_q, ones_k, dup, tm=512):
    t = proj.shape[0]

    def body(p_ref, gq_ref, gk_ref, oq_ref, ok_ref, dup_ref, qn_ref, kd_ref, vd_ref):
        q = p_ref[:, 0:ATTN_DIM]
        k = p_ref[:, ATTN_DIM:ATTN_DIM + KV_DIM]
        v = p_ref[:, ATTN_DIM + KV_DIM:]
        rq = lax.rsqrt(_hdot(q * q, oq_ref[...]) * (1.0 / HEAD_DIM) + EPS)
        qn_ref[...] = (q * rq * gq_ref[...]) * (HEAD_DIM ** -0.5)
        rk = lax.rsqrt(_hdot(k * k, ok_ref[...]) * (1.0 / HEAD_DIM) + EPS)
        kn = k * rk * gk_ref[...]
        kd_ref[...] = _dot(kn.astype(BF16), dup_ref[...])
        vd_ref[...] = _dot(v.astype(BF16), dup_ref[...])

    full = lambda a: pl.BlockSpec(a.shape, lambda i: (0, 0))
    o_spec = pl.BlockSpec((tm, ATTN_DIM), lambda i: (i, 0))
    return pl.pallas_call(
        body, name="attn_prep", grid=(t // tm,),
        in_specs=[pl.BlockSpec((tm, 1024), lambda i: (i, 0)), full(gq), full(gk), full(ones_q), full(ones_k), full(dup)],
        out_specs=[o_spec, o_spec, o_spec], out_shape=[SDS((t, ATTN_DIM), F32)] * 3,
        compiler_params=_params("parallel"))(proj, gq, gk, ones_q, ones_k, dup)


def _attn_prep_bwd(proj, dqn, dkc, dkp, dvc, dvp, gq, gk, ones_q, ones_k, dup_t, tm=512):
    t = proj.shape[0]
    nblk = t // tm
    off = SUPER // tm

    def body(p_ref, dqn_ref, dkc_ref, dkp_ref, dvc_ref, dvp_ref, gq_ref, gk_ref, oq_ref, ok_ref, dt_ref,
             o_ref, dgq_ref, dgk_ref):
        i = pl.program_id(0)
        has_next = (i + off < nblk).astype(F32)
        q = p_ref[:, 0:ATTN_DIM]
        k = p_ref[:, ATTN_DIM:ATTN_DIM + KV_DIM]
        dkn = _hdot(dkc_ref[...] + has_next * dkp_ref[...], dt_ref[...])
        dv = _hdot(dvc_ref[...] + has_next * dvp_ref[...], dt_ref[...])

        @pl.when(i == 0)
        def _():
            dgq_ref[...] = jnp.zeros_like(dgq_ref)
            dgk_ref[...] = jnp.zeros_like(dgk_ref)

        rq = lax.rsqrt(_hdot(q * q, oq_ref[...]) * (1.0 / HEAD_DIM) + EPS)
        xh = q * rq
        dy = dqn_ref[...] * (HEAD_DIM ** -0.5)
        dgq_ref[...] += jnp.sum(dy * xh, axis=0, keepdims=True)
        gd = dy * gq_ref[...]
        dq = rq * (gd - xh * (_hdot(gd * xh, oq_ref[...]) * (1.0 / HEAD_DIM)))
        rk = lax.rsqrt(_hdot(k * k, ok_ref[...]) * (1.0 / HEAD_DIM) + EPS)
        kh = k * rk
        dgk_ref[...] += jnp.sum(dkn * kh, axis=0, keepdims=True)
        gdk = dkn * gk_ref[...]
        dk = rk * (gdk - kh * (_hdot(gdk * kh, ok_ref[...]) * (1.0 / HEAD_DIM)))
        o_ref[:, 0:ATTN_DIM] = dq.astype(BF16)
        o_ref[:, ATTN_DIM:ATTN_DIM + KV_DIM] = dk.astype(BF16)
        o_ref[:, ATTN_DIM + KV_DIM:] = dv.astype(BF16)

    full = lambda a: pl.BlockSpec(a.shape, lambda i: (0, 0))
    cur = pl.BlockSpec((tm, ATTN_DIM), lambda i: (i, 0))
    nxt = pl.BlockSpec((tm, ATTN_DIM), lambda i: (jnp.minimum(i + off, nblk - 1), 0))
    return pl.pallas_call(
        body, name="attn_prep_bwd", grid=(nblk,),
        in_specs=[pl.BlockSpec((tm, 1024), lambda i: (i, 0)), cur, cur, nxt, cur, nxt,
                  full(gq), full(gk), full(ones_q), full(ones_k), full(dup_t)],
        out_specs=[pl.BlockSpec((tm, 1024), lambda i: (i, 0)), pl.BlockSpec((1, ATTN_DIM), lambda i: (0, 0)),
                   pl.BlockSpec((1, KV_DIM), lambda i: (0, 0))],
        out_shape=[SDS((t, 1024), BF16), SDS((1, ATTN_DIM), F32), SDS((1, KV_DIM), F32)],
        compiler_params=_params("arbitrary"))(proj, dqn, dkc, dkp, dvc, dvp, gq, gk, ones_q, ones_k, dup_t)


def _tile_masks():
    qi = lax.broadcasted_iota(jnp.int32, (2 * CHUNK, 2 * CHUNK), 0) & (CHUNK - 1)
    kj = lax.broadcasted_iota(jnp.int32, (2 * CHUNK, 2 * CHUNK), 1)
    delta = CHUNK + qi - kj
    band = (delta >= 0) & (delta <= CHUNK)
    return band, kj


def _deinterleave(dst, src, n_rows, d):
    per = n_rows // d
    for r in range(d):
        dst[r * per:(r + 1) * per, :] = src[pl.ds(r, per, stride=d), :]


def _attn_specs(t):
    blk = lambda f: pl.BlockSpec((SUPER, 128), f)
    cur = blk(lambda h, s: (s, h))
    prev = blk(lambda h, s: (jnp.maximum(s - 1, 0), h))
    return cur, prev


def _attn_fwd(qn, kd, vd):
    t = qn.shape[0]
    cur, prev = _attn_specs(t)

    def body(q_ref, kp_ref, kc_ref, vp_ref, vc_ref, o_ref, lse_ref, kk, vv, qd, kdd, vdd, po, pm, pll, acc, mm, ll):
        s = pl.program_id(1)
        kk[0:SUPER, :] = kp_ref[...]
        kk[SUPER:, :] = kc_ref[...]
        vv[0:SUPER, :] = vp_ref[...]
        vv[SUPER:, :] = vc_ref[...]
        m0 = lax.broadcasted_iota(jnp.int32, (CHUNK, 128), 1) < HEAD_DIM
        band, kj = _tile_masks()
        for d in DILATIONS:
            lq = SUPER // d
            if d == 1:
                qs_ref, ks_ref, vs_ref = q_ref, kk, vv
            else:
                _deinterleave(qd, q_ref, SUPER, d)
                _deinterleave(kdd, kk, 2 * SUPER, d)
                _deinterleave(vdd, vv, 2 * SUPER, d)
                qs_ref, ks_ref, vs_ref = qd, kdd, vdd

            def tile(ti, carry):
                r = ti // (lq // CHUNK)
                nb = ti % (lq // CHUNK)
                qoff = pl.multiple_of(ti * CHUNK, CHUNK)
                koff = pl.multiple_of(r * 2 * lq + lq + (nb - 1) * CHUNK, CHUNK)
                qt = qs_ref[pl.ds(qoff, CHUNK), :]
                qs = jnp.concatenate([jnp.where(m0, qt, 0.0), jnp.where(m0, 0.0, qt)], axis=0).astype(BF16)
                kt = ks_ref[pl.ds(koff, 2 * CHUNK), :].astype(BF16)
                vt = vs_ref[pl.ds(koff, 2 * CHUNK), :].astype(BF16)
                sc = _dot(qs, kt, NT)
                ok = band & (kj >= jnp.where((s > 0) | (nb > 0), 0, CHUNK))
                sc = jnp.where(ok, sc, -jnp.inf)
                mt = jnp.max(sc, axis=-1, keepdims=True)
                p = jnp.exp(sc - mt)
                lt = jnp.sum(p, axis=-1, keepdims=True)
                ot = _dot(p.astype(BF16), vt)
                po[pl.ds(qoff, CHUNK), :] = jnp.where(m0, ot[:CHUNK], ot[CHUNK:])
                pm[pl.ds(qoff, CHUNK), :] = jnp.where(m0, mt[:CHUNK], mt[CHUNK:])
                pll[pl.ds(qoff, CHUNK), :] = jnp.where(m0, lt[:CHUNK], lt[CHUNK:])
                return carry

            lax.fori_loop(0, SUPER // CHUNK, tile, 0)
            if d == 1:
                acc[...] = po[...]
                mm[...] = pm[...]
                ll[...] = pll[...]
            else:
                for r in range(d):
                    rows = pl.ds(r, lq, stride=d)
                    seg = slice(r * lq, (r + 1) * lq)
                    m_old, m_new = mm[rows, :], pm[seg, :]
                    m_all = jnp.maximum(m_old, m_new)
                    a, b = jnp.exp(m_old - m_all), jnp.exp(m_new - m_all)
                    acc[rows, :] = acc[rows, :] * a + po[seg, :] * b
                    ll[rows, :] = ll[rows, :] * a + pll[seg, :] * b
                    mm[rows, :] = m_all
        o_ref[...] = acc[...] / ll[...]
        lse_ref[...] = mm[...] + jnp.log(ll[...])

    big = pltpu.VMEM((2 * SUPER, 128), F32)
    one = pltpu.VMEM((SUPER, 128), F32)
    return pl.pallas_call(
        body, name="attn_fwd", grid=(4, t // SUPER),
        in_specs=[cur, prev, cur, prev, cur], out_specs=[cur, cur],
        out_shape=[SDS((t, ATTN_DIM), F32)] * 2,
        scratch_shapes=[big, big, one, big, big, one, one, one, one, one, one],
        compiler_params=_params("parallel", "arbitrary"))(qn, kd, kd, vd, vd)


def _attn_bwd(qn, kd, vd, out, lse, dout):
    t = qn.shape[0]
    cur, prev = _attn_specs(t)

    def body(q_ref, kp_ref, kc_ref, vp_ref, vc_ref, o_ref, lse_ref, do_ref,
             dq_ref, dkc_ref, dkp_ref, dvc_ref, dvp_ref,
             kk, vv, dkk, dvv, qd, od, ld, dod, kdd, vdd, dkd, dvd, pdq):
        s = pl.program_id(1)
        kk[0:SUPER, :] = kp_ref[...]
        kk[SUPER:, :] = kc_ref[...]
        vv[0:SUPER, :] = vp_ref[...]
        vv[SUPER:, :] = vc_ref[...]
        dkk[...] = jnp.zeros_like(dkk)
        dvv[...] = jnp.zeros_like(dvv)
        dq_ref[...] = jnp.zeros_like(dq_ref)
        m0 = lax.broadcasted_iota(jnp.int32, (CHUNK, 128), 1) < HEAD_DIM
        band, kj = _tile_masks()
        ninf = -jnp.inf
        for d in DILATIONS:
            lq = SUPER // d
            if d == 1:
                qs_ref, os_ref, ls_ref, dos_ref, ks_ref, vs_ref, dks_ref, dvs_ref = (
                    q_ref, o_ref, lse_ref, do_ref, kk, vv, dkk, dvv)
            else:
                _deinterleave(qd, q_ref, SUPER, d)
                _deinterleave(od, o_ref, SUPER, d)
                _deinterleave(ld, lse_ref, SUPER, d)
                _deinterleave(dod, do_ref, SUPER, d)
                _deinterleave(kdd, kk, 2 * SUPER, d)
                _deinterleave(vdd, vv, 2 * SUPER, d)
                dkd[...] = jnp.zeros_like(dkd)
                dvd[...] = jnp.zeros_like(dvd)
                qs_ref, os_ref, ls_ref, dos_ref, ks_ref, vs_ref, dks_ref, dvs_ref = (
                    qd, od, ld, dod, kdd, vdd, dkd, dvd)

            def tile(ti, carry):
                r = ti // (lq // CHUNK)
                nb = ti % (lq // CHUNK)
                qoff = pl.multiple_of(ti * CHUNK, CHUNK)
                koff = pl.multiple_of(r * 2 * lq + lq + (nb - 1) * CHUNK, CHUNK)
                qrows = pl.ds(qoff, CHUNK)
                krows = pl.ds(koff, 2 * CHUNK)
                qt, ot, lt, dot_ = qs_ref[qrows, :], os_ref[qrows, :], ls_ref[qrows, :], dos_ref[qrows, :]
                qs = jnp.concatenate([jnp.where(m0, qt, 0.0), jnp.where(m0, 0.0, qt)], axis=0).astype(BF16)
                dos = jnp.concatenate([jnp.where(m0, dot_, 0.0), jnp.where(m0, 0.0, dot_)], axis=0).astype(BF16)
                lse_rows = jnp.concatenate([jnp.max(jnp.where(m0, lt, ninf), axis=-1, keepdims=True),
                                            jnp.max(jnp.where(m0, ninf, lt), axis=-1, keepdims=True)], axis=0)
                prod = dot_ * ot
                dl_rows = jnp.concatenate([jnp.sum(jnp.where(m0, prod, 0.0), axis=-1, keepdims=True),
                                           jnp.sum(jnp.where(m0, 0.0, prod), axis=-1, keepdims=True)], axis=0)
                kt = ks_ref[krows, :].astype(BF16)
                vt = vs_ref[krows, :].astype(BF16)
                sc = _dot(qs, kt, NT)
                ok = band & (kj >= jnp.where((s > 0) | (nb > 0), 0, CHUNK))
                p = jnp.exp(jnp.where(ok, sc, ninf) - lse_rows)
                dp = _dot(dos, vt, NT)
                ds = p * (dp - dl_rows)
                dqs = _dot(ds.astype(BF16), kt)
                pdq[qrows, :] = jnp.where(m0, dqs[:CHUNK], dqs[CHUNK:])
                dks_ref[krows, :] += _dot(ds.T.astype(BF16), qs)
                dvs_ref[krows, :] += _dot(p.T.astype(BF16), dos)
                return carry

            lax.fori_loop(0, SUPER // CHUNK, tile, 0)
            if d == 1:
                dq_ref[...] += pdq[...]
            else:
                for r in range(d):
                    dq_ref[pl.ds(r, lq, stride=d), :] += pdq[r * lq:(r + 1) * lq, :]
                    dkk[pl.ds(r, 2 * lq, stride=d), :] += dkd[r * 2 * lq:(r + 1) * 2 * lq, :]
                    dvv[pl.ds(r, 2 * lq, stride=d), :] += dvd[r * 2 * lq:(r + 1) * 2 * lq, :]
        dkp_ref[...] = dkk[0:SUPER, :]
        dkc_ref[...] = dkk[SUPER:, :]
        dvp_ref[...] = dvv[0:SUPER, :]
        dvc_ref[...] = dvv[SUPER:, :]

    big = pltpu.VMEM((2 * SUPER, 128), F32)
    one = pltpu.VMEM((SUPER, 128), F32)
    return pl.pallas_call(
        body, name="attn_bwd", grid=(4, t // SUPER),
        in_specs=[cur, prev, cur, prev, cur, cur, cur, cur], out_specs=[cur] * 5,
        out_shape=[SDS((t, ATTN_DIM), F32)] * 5,
        scratch_shapes=[big, big, big, big, one, one, one, one, big, big, big, big, one],
        compiler_params=_params("parallel", "arbitrary"))(qn, kd, kd, vd, vd, out, lse, dout)


def _ssd_consts():
    tri = np.tril(np.ones((CHUNK, CHUNK), np.float32))
    expand = np.zeros((128, SSM_INNER), np.float32)
    for h in range(SSM_HEADS):
        expand[h, h * HEAD_DIM:(h + 1) * HEAD_DIM] = 1.0
    return jnp.asarray(tri), jnp.asarray(tri.T), jnp.asarray(expand), jnp.asarray(expand.T)


def _conv4(x, halo, w_ref, b_ref):
    acc = b_ref[...] + w_ref[3:4, :] * x
    for k in range(3):
        acc = acc + w_ref[k:k + 1, :] * _shift_down(x, halo, 3 - k)
    return acc


def _softplus(x):
    return jnp.maximum(x, 0.0) + jnp.log(1.0 + jnp.exp(-jnp.abs(x)))


def _ssd_common(xs_ref, bc_ref, dt_ref, hx_ref, hb_ref, cwx_ref, cbx_ref, cwb_ref, cbb_ref, dtb_ref, alog_ref,
                tri_ref, exp_ref, first):
    keep = 1.0 - first.astype(F32)
    hx = hx_ref[...] * keep
    hb = hb_ref[...] * keep
    pre_x = _conv4(xs_ref[...], hx, cwx_ref, cbx_ref)
    pre_b = _conv4(bc_ref[...], hb, cwb_ref, cbb_ref)
    xa = pre_x * _sigmoid(pre_x)
    ba = pre_b * _sigmoid(pre_b)
    dtv = _softplus(dt_ref[...] + dtb_ref[...])
    a_neg = -jnp.exp(alog_ref[...])
    acum = _hdot(tri_ref[...], dtv * a_neg)
    lam = jnp.exp(acum)
    gam = jnp.exp(acum[CHUNK - 1:CHUNK, :] - acum)
    dt_e = _hdot(dtv, exp_ref[...])
    lam_e = _hdot(lam, exp_ref[...])
    gam_e = _hdot(gam, exp_ref[...])
    return dict(hx=hx, hb=hb, pre_x=pre_x, pre_b=pre_b, xa=xa, ba=ba, dtv=dtv, a_neg=a_neg, acum=acum,
                dt_e=dt_e, lam_e=lam_e, gam_e=gam_e, xdt=xa * dt_e)


def _decay(acum_t, h, transposed):
    rb = jnp.broadcast_to(acum_t[h:h + 1, :], (CHUNK, CHUNK))
    ri = lax.broadcasted_iota(jnp.int32, (CHUNK, CHUNK), 0)
    ci = lax.broadcasted_iota(jnp.int32, (CHUNK, CHUNK), 1)
    if transposed:
        return jnp.exp(jnp.where(ci >= ri, rb - rb.T, -jnp.inf))
    return jnp.exp(jnp.where(ri >= ci, rb.T - rb, -jnp.inf))


def _ssd_specs(t, rev):
    nc = t // CHUNK
    ch = (lambda c: nc - 1 - c) if rev else (lambda c: c)
    col = lambda w, j: pl.BlockSpec((CHUNK, w), lambda c: (ch(c), j))
    halo = lambda w, j: pl.BlockSpec((8, w), lambda c: (jnp.maximum(ch(c) * (CHUNK // 8) - 1, 0), j))
    return nc, ch, col, halo


def _ssd_fwd(proj, cwx, cbx, cwb, cbb, dtb, alog, dsk_e, norm_g, tri, expand):
    t = proj.shape[0]
    nc, _, col, halo = _ssd_specs(t, False)

    def body(z_ref, xs_ref, bc_ref, dt_ref, hx_ref, hb_ref, cwx_ref, cbx_ref, cwb_ref, cbb_ref, dtb_ref, alog_ref,
             dsk_ref, g_ref, tri_ref, exp_ref, y_ref, hs_ref, o_ref, state):
        c = pl.program_id(0)

        @pl.when(c == 0)
        def _():
            state[...] = jnp.zeros_like(state)

        v = _ssd_common(xs_ref, bc_ref, dt_ref, hx_ref, hb_ref, cwx_ref, cbx_ref, cwb_ref, cbb_ref, dtb_ref,
                        alog_ref, tri_ref, exp_ref, c == 0)
        acum_t = v["acum"].T
        xdt, ba = v["xdt"], v["ba"]
        h_in = state[...]
        hs_ref[0] = h_in
        xg = xdt * v["gam_e"]
        m0 = lax.broadcasted_iota(jnp.int32, (CHUNK, 128), 1) < HEAD_DIM
        for g in range(2):
            bg = ba[:, g * 128:(g + 1) * 128].astype(BF16)
            cg = ba[:, 256 + g * 128:256 + (g + 1) * 128].astype(BF16)
            gl = slice(g * 512, (g + 1) * 512)
            cb = _dot(cg, bg, NT)
            y_off = _dot(cg, h_in[:, gl].astype(BF16)) * v["lam_e"][:, gl]
            s_new = _dot(bg.T, xg[:, gl].astype(BF16))
            state[:, gl] = h_in[:, gl] * v["lam_e"][CHUNK - 1:CHUNK, gl] + s_new
            for j in range(4):
                h0 = 8 * g + 2 * j
                ln = slice(g * 512 + j * 128, g * 512 + (j + 1) * 128)
                xp = xdt[:, ln].astype(BF16)
                y0 = _dot((cb * _decay(acum_t, h0, False)).astype(BF16), xp)
                y1 = _dot((cb * _decay(acum_t, h0 + 1, False)).astype(BF16), xp)
                y_ref[:, ln] = jnp.where(m0, y0, y1) + y_off[:, j * 128:(j + 1) * 128]
        z = z_ref[...]
        yg = (y_ref[...] + dsk_ref[...] * v["xa"]) * (z * _sigmoid(z))
        r = lax.rsqrt(jnp.mean(yg * yg, axis=-1, keepdims=True) + EPS)
        o_ref[...] = (yg * r * g_ref[...]).astype(BF16)

    full = lambda a: pl.BlockSpec(a.shape, lambda c: (0,) * a.ndim)
    return pl.pallas_call(
        body, name="ssd_fwd", grid=(nc,),
        in_specs=[col(1024, 1), col(1024, 2), col(512, 6), col(128, 28), halo(1024, 2), halo(512, 6),
                  full(cwx), full(cbx), full(cwb), full(cbb), full(dtb), full(alog), full(dsk_e), full(norm_g),
                  full(tri), full(expand)],
        out_specs=[pl.BlockSpec((CHUNK, SSM_INNER), lambda c: (c, 0)),
                   pl.BlockSpec((1, 128, SSM_INNER), lambda c: (c, 0, 0)),
                   pl.BlockSpec((CHUNK, SSM_INNER), lambda c: (c, 0))],
        out_shape=[SDS((t, SSM_INNER), F32), SDS((nc, 128, SSM_INNER), F32), SDS((t, SSM_INNER), BF16)],
        scratch_shapes=[pltpu.VMEM((128, SSM_INNER), F32)],
        compiler_params=_params("arbitrary"))(proj, proj, proj, proj, proj, proj, cwx, cbx, cwb, cbb, dtb, alog,
                                              dsk_e, norm_g, tri, expand)


def _ssd_bwd(proj, y_ssd, hs, dout, cwx, cbx, cwb, cbb, dtb, alog, dsk_e, norm_g, tri, triu, expand, expand_t):
    t = proj.shape[0]
    nc, ch, col, halo = _ssd_specs(t, True)

    def body(z_ref, xs_ref, bc_ref, dt_ref, hx_ref, hb_ref, y_ref, hin_ref, do_ref,
             cwx_ref, cbx_ref, cwb_ref, cbb_ref, dtb_ref, alog_ref, dsk_ref, g_ref, tri_ref, triu_ref, exp_ref, expt_ref,
             dz_ref, dxs_ref, dbc_ref, ddt_ref, dg_ref, ddsk_ref, dalog_ref, ddtb_ref, dcwx_ref, dcbx_ref, dcwb_ref,
             dcbb_ref, gstate, nx_x, nx_b, dact_b):
        step = pl.program_id(0)
        c = nc - 1 - step

        @pl.when(step == 0)
        def _():
            gstate[...] = jnp.zeros_like(gstate)
            nx_x[...] = jnp.zeros_like(nx_x)
            nx_b[...] = jnp.zeros_like(nx_b)
            for ref in (dg_ref, ddsk_ref, dalog_ref, ddtb_ref, dcwx_ref, dcbx_ref, dcwb_ref, dcbb_ref):
                ref[...] = jnp.zeros_like(ref)

        v = _ssd_common(xs_ref, bc_ref, dt_ref, hx_ref, hb_ref, cwx_ref, cbx_ref, cwb_ref, cbb_ref, dtb_ref,
                        alog_ref, tri_ref, exp_ref, c == 0)
        acum_t = v["acum"].T
        xa, ba, xdt, dtv = v["xa"], v["ba"], v["xdt"], v["dtv"]
        lam_e, gam_e, dt_e = v["lam_e"], v["gam_e"], v["dt_e"]
        z = z_ref[...]
        y = y_ref[...]
        sz = _sigmoid(z)
        zs = z * sz
        y_tot = y + dsk_ref[...] * xa
        yg = y_tot * zs
        r = lax.rsqrt(jnp.mean(yg * yg, axis=-1, keepdims=True) + EPS)
        yh = yg * r
        do = do_ref[...]
        dg_ref[...] += jnp.sum(do * yh, axis=0, keepdims=True)
        gd = do * g_ref[...]
        dyg = r * (gd - yh * jnp.mean(gd * yh, axis=-1, keepdims=True))
        dz_ref[...] = (dyg * y_tot * (sz * (1.0 + z * (1.0 - sz)))).astype(BF16)
        dy = dyg * zs
        ddsk_ref[...] += jnp.sum(dy * xa, axis=0, keepdims=True)
        g_out = gstate[...]
        h_in = hin_ref[0]
        lam_dy = lam_e * dy
        gam_x = gam_e * xdt
        m0 = lax.broadcasted_iota(jnp.int32, (CHUNK, 128), 1) < HEAD_DIM
        lane = lax.broadcasted_iota(jnp.int32, (CHUNK, 128), 1)
        below = (lax.broadcasted_iota(jnp.int32, (CHUNK, CHUNK), 0) >
                 lax.broadcasted_iota(jnp.int32, (CHUNK, CHUNK), 1))
        da_in = jnp.zeros((CHUNK, 128), F32)
        off_y, off_x = [], []
        for g in range(2):
            bg = ba[:, g * 128:(g + 1) * 128].astype(BF16)
            cg = ba[:, 256 + g * 128:256 + (g + 1) * 128].astype(BF16)
            gl = slice(g * 512, (g + 1) * 512)
            gg = g_out[:, gl].astype(BF16)
            bc_t = _dot(bg, cg, NT)
            cb = _dot(cg, bg, NT)
            dxdt_off = _dot(bg, gg) * gam_e[:, gl]
            off_x.append(xdt[:, gl] * dxdt_off)
            off_y.append(dy[:, gl] * (_dot(cg, h_in[:, gl].astype(BF16)) * lam_e[:, gl]))
            q_sum = jnp.zeros((CHUNK, CHUNK), F32)
            for j in range(4):
                h0 = 8 * g + 2 * j
                ln = slice(g * 512 + j * 128, g * 512 + (j + 1) * 128)
                dyp = dy[:, ln]
                dyb = dyp.astype(BF16)
                xpb = xdt[:, ln].astype(BF16)
                d0 = _dot((bc_t * _decay(acum_t, h0, True)).astype(BF16), dyb)
                d1 = _dot((bc_t * _decay(acum_t, h0 + 1, True)).astype(BF16), dyb)
                dxs_ref[:, ln] = jnp.where(m0, d0, d1) + dxdt_off[:, j * 128:(j + 1) * 128]
                for hh, dym in ((h0, jnp.where(m0, dyp, 0.0)), (h0 + 1, jnp.where(m0, 0.0, dyp))):
                    qd = _dot(dym.astype(BF16), xpb, NT) * _decay(acum_t, hh, False)
                    q_sum = q_sum + qd
                    reach = jnp.where(below, _hdot(triu_ref[...], qd * cb), 0.0)
                    da_in = jnp.where(lane == hh, jnp.sum(reach, axis=-1, keepdims=True), da_in)
            gstate[:, gl] = g_out[:, gl] * lam_e[CHUNK - 1:CHUNK, gl] + _dot(cg.T, lam_dy[:, gl].astype(BF16))
            qb = q_sum.astype(BF16)
            dact_b[:, 256 + g * 128:256 + (g + 1) * 128] = (
                _dot(qb, bg) + _dot(lam_dy[:, gl].astype(BF16), h_in[:, gl].astype(BF16), NT))
            dact_b[:, g * 128:(g + 1) * 128] = _dot(qb.T, cg) + _dot(gam_x[:, gl].astype(BF16), gg, NT)
        dxdt = dxs_ref[...]
        seg_y = _hdot(jnp.concatenate(off_y, axis=1), expt_ref[...])
        seg_x = _hdot(jnp.concatenate(off_x, axis=1), expt_ref[...])
        e_col = jnp.sum(g_out * h_in * lam_e[CHUNK - 1:CHUNK, :], axis=0, keepdims=True)
        e_seg = _hdot(jnp.broadcast_to(e_col, (8, SSM_INNER)), expt_ref[...])[0:1, :]
        da = da_in + _hdot(triu_ref[...], seg_y) + (_hdot(tri_ref[...], seg_x) - seg_x) + e_seg
        a_neg = v["a_neg"]
        ddtv = da * a_neg + _hdot(dxdt * xa, expt_ref[...])
        dalog_ref[...] += jnp.sum(da * dtv, axis=0, keepdims=True) * a_neg
        lane16 = lax.broadcasted_iota(jnp.int32, (CHUNK, 128), 1) < SSM_HEADS
        draw = jnp.where(lane16, ddtv * _sigmoid(dt_ref[...] + dtb_ref[...]), 0.0)
        ddtb_ref[...] += jnp.sum(draw, axis=0, keepdims=True)
        ddt_ref[...] = draw.astype(BF16)
        dxa = dxdt * dt_e + dy * dsk_ref[...]
        for (dact, pre, x_ref, halo_v, nx, cw_ref, dcw_ref, dcb_ref, dx_ref) in (
                (dxa, v["pre_x"], xs_ref, v["hx"], nx_x, cwx_ref, dcwx_ref, dcbx_ref, dxs_ref),
                (dact_b[...], v["pre_b"], bc_ref, v["hb"], nx_b, cwb_ref, dcwb_ref, dcbb_ref, dbc_ref)):
            sp = _sigmoid(pre)
            dpre = dact * (sp * (1.0 + pre * (1.0 - sp)))
            dcb_ref[...] += jnp.sum(dpre, axis=0, keepdims=True)
            xv = x_ref[...]
            nxt = nx[...]
            dx = cw_ref[3:4, :] * dpre
            dcw_ref[3:4, :] += jnp.sum(dpre * xv, axis=0, keepdims=True)
            for k in range(3):
                dcw_ref[k:k + 1, :] += jnp.sum(dpre * _shift_down(xv, halo_v, 3 - k), axis=0, keepdims=True)
                dx = dx + cw_ref[k:k + 1, :] * _shift_up(dpre, nxt, 3 - k)
            nx[...] = dpre[0:8, :]
            dx_ref[...] = dx.astype(dx_ref.dtype)

    full = lambda a: pl.BlockSpec(a.shape, lambda c: (0,) * a.ndim)
    rowblk = lambda w: pl.BlockSpec((CHUNK, w), lambda c: (ch(c), 0))
    acc = lambda a, b: pl.BlockSpec((a, b), lambda c: (0, 0))
    return pl.pallas_call(
        body, name="ssd_bwd", grid=(nc,),
        in_specs=[col(1024, 1), col(1024, 2), col(512, 6), col(128, 28), halo(1024, 2), halo(512, 6),
                  rowblk(SSM_INNER),
                  pl.BlockSpec((1, 128, SSM_INNER), lambda c: (ch(c), 0, 0)),
                  rowblk(SSM_INNER),
                  full(cwx), full(cbx), full(cwb), full(cbb), full(dtb), full(alog), full(dsk_e), full(norm_g),
                  full(tri), full(triu), full(expand), full(expand_t)],
        out_specs=[rowblk(SSM_INNER), rowblk(SSM_INNER), rowblk(512), rowblk(128),
                   acc(1, 1024), acc(1, 1024), acc(1, 128), acc(1, 128), acc(4, 1024), acc(1, 1024), acc(4, 512),
                   acc(1, 512)],
        out_shape=[SDS((t, SSM_INNER), BF16), SDS((t, SSM_INNER), F32), SDS((t, 512), BF16), SDS((t, 128), BF16),
                   SDS((1, 1024), F32), SDS((1, 1024), F32), SDS((1, 128), F32), SDS((1, 128), F32),
                   SDS((4, 1024), F32), SDS((1, 1024), F32), SDS((4, 512), F32), SDS((1, 512), F32)],
        scratch_shapes=[pltpu.VMEM((128, SSM_INNER), F32), pltpu.VMEM((8, 1024), F32), pltpu.VMEM((8, 512), F32),
                        pltpu.VMEM((CHUNK, 512), F32)],
        compiler_params=_params("arbitrary"))(proj, proj, proj, proj, proj, proj, y_ssd, hs, dout,
                                              cwx, cbx, cwb, cbb, dtb, alog, dsk_e, norm_g, tri, triu, expand,
                                              expand_t)


def _conv3(x, halo, w_ref, b_ref, part):
    acc = b_ref[part] + w_ref[2, part] * x
    for k in range(2):
        acc = acc + w_ref[k, part] * _shift_down(x, halo, 2 - k)
    return acc


def _ffn_specs(t, tm, tn, order):
    nj = D_FF // tn
    ij = (lambda a, b: (b, a)) if order == "ji" else (lambda a, b: (a, b))

    def spec(shape, f):
        return pl.BlockSpec(shape, lambda a, b: f(*ij(a, b)))
    u_g = spec((tm, tn), lambda i, j: (i, j))
    u_v = spec((tm, tn), lambda i, j: (i, j + nj))
    h_g = spec((8, tn), lambda i, j: (jnp.maximum(i * (tm // 8) - 1, 0), j))
    h_v = spec((8, tn), lambda i, j: (jnp.maximum(i * (tm // 8) - 1, 0), j + nj))
    w = spec((3, 2, 1, tn), lambda i, j: (0, 0, 0, j))
    b = spec((2, 1, tn), lambda i, j: (0, 0, j))
    return nj, spec, u_g, u_v, h_g, h_v, w, b


def _ffn_act(u, cw, cb, tm=512, tn=1408):
    t = u.shape[0]
    nj, spec, u_g, u_v, h_g, h_v, w, b = _ffn_specs(t, tm, tn, "ij")

    def body(ug_ref, uv_ref, hg_ref, hv_ref, w_ref, b_ref, f_ref):
        keep = (pl.program_id(0) > 0).astype(F32)
        gate = _conv3(ug_ref[...], hg_ref[...] * keep, w_ref, b_ref, 0)
        val = _conv3(uv_ref[...], hv_ref[...] * keep, w_ref, b_ref, 1)
        f_ref[...] = (gate * _sigmoid(gate) * val).astype(BF16)

    return pl.pallas_call(
        body, name="ffn_act", grid=(t // tm, nj), in_specs=[u_g, u_v, h_g, h_v, w, b],
        out_specs=spec((tm, tn), lambda i, j: (i, j)), out_shape=SDS((t, D_FF), BF16),
        compiler_params=_params("parallel", "parallel"))(u, u, u, u, cw, cb)


def _ffn_act_bwd(dx2, w_down, u, cw, cb, tm=512, tn=1408):
    t = u.shape[0]
    nj, spec, u_g, u_v, h_g, h_v, w, b = _ffn_specs(t, tm, tn, "ji")

    def body(dx_ref, wd_ref, ug_ref, uv_ref, hg_ref, hv_ref, w_ref, b_ref, duc_ref, dcw_ref, dcb_ref):
        i = pl.program_id(1)
        first = i == 0
        df = _dot(dx_ref[...].astype(BF16), wd_ref[...], NT)
        ug, uv = ug_ref[...], uv_ref[...]
        keep = 1.0 - first.astype(F32)
        hg, hv = hg_ref[...] * keep, hv_ref[...] * keep
        gate = _conv3(ug, hg, w_ref, b_ref, 0)
        val = _conv3(uv, hv, w_ref, b_ref, 1)
        sg = _sigmoid(gate)
        dgate = df * val * (sg * (1.0 + gate * (1.0 - sg)))
        dval = df * (gate * sg)
        duc_ref[0] = dgate
        duc_ref[1] = dval

        @pl.when(first)
        def _():
            dcw_ref[...] = jnp.zeros_like(dcw_ref)
            dcb_ref[...] = jnp.zeros_like(dcb_ref)

        for part, (du, uu, hh) in enumerate(((dgate, ug, hg), (dval, uv, hv))):
            dcb_ref[part] += jnp.sum(du, axis=0, keepdims=True)
            for k in range(3):
                dcw_ref[k, part] += jnp.sum(du * _shift_down(uu, hh, 2 - k), axis=0, keepdims=True)

    return pl.pallas_call(
        body, name="ffn_act_bwd", grid=(nj, t // tm),
        in_specs=[spec((tm, D_MODEL), lambda i, j: (i, 0)), spec((tn, D_MODEL), lambda i, j: (j, 0)),
                  u_g, u_v, h_g, h_v, w, b],
        out_specs=[pl.BlockSpec((2, tm, tn), lambda j, i: (0, i, j)), w, b],
        out_shape=[SDS((2, t, D_FF), F32), SDS((3, 2, 1, D_FF), F32), SDS((2, 1, D_FF), F32)],
        compiler_params=_params("parallel", "arbitrary"))(dx2, w_down, u, u, u, u, cw, cb)


def _ffn_conv_t(duc, cw, tm=512, tn=1408):
    t = duc.shape[1]
    nj = D_FF // tn
    nrow8 = t // 8

    def body(d_ref, n_ref, w_ref, o_ref):
        keep = (pl.program_id(0) < pl.num_programs(0) - 1).astype(F32)
        for part in range(2):
            dv = d_ref[part]
            nxt = n_ref[part] * keep
            acc = w_ref[2, part] * dv
            for k in range(2):
                acc = acc + w_ref[k, part] * _shift_up(dv, nxt, 2 - k)
            o_ref[part] = acc.astype(BF16)

    return pl.pallas_call(
        body, name="ffn_conv_t", grid=(t // tm, nj),
        in_specs=[pl.BlockSpec((2, tm, tn), lambda i, j: (0, i, j)),
                  pl.BlockSpec((2, 8, tn), lambda i, j: (0, jnp.minimum((i + 1) * (tm // 8), nrow8 - 1), j)),
                  pl.BlockSpec((3, 2, 1, tn), lambda i, j: (0, 0, 0, j))],
        out_specs=pl.BlockSpec((2, tm, tn), lambda i, j: (0, i, j)), out_shape=SDS((2, t, D_FF), BF16),
        compiler_params=_params("parallel", "parallel"))(duc, duc, cw)


def _ple_loss(x2, g, w_gate, p, w_proj_t, target, tm=256):
    t = x2.shape[0]

    def body(x_ref, g_ref, wg_ref, p_ref, wp_ref, tg_ref, dx_ref, dpre_ref, dpp_ref, h_ref, loss_ref, dg_ref):
        i = pl.program_id(0)
        xv = x_ref[...]
        r = lax.rsqrt(jnp.mean(xv * xv, axis=-1, keepdims=True) + EPS)
        xh = xv * r
        h = (xh * g_ref[...]).astype(BF16)
        h_ref[...] = h
        gate = _sigmoid(_dot(h, wg_ref[...]))
        pp = _dot(p_ref[...].astype(BF16), wp_ref[...], NT)
        err = (xv + gate * pp) - tg_ref[...]

        @pl.when(i == 0)
        def _():
            loss_ref[...] = jnp.zeros_like(loss_ref)
            dg_ref[...] = jnp.zeros_like(dg_ref)

        loss_ref[...] += 0.5 * jnp.sum(jnp.mean(err * err, axis=-1, keepdims=True), axis=0, keepdims=True)
        dy = err * (1.0 / D_MODEL)
        dpre = (dy * pp * gate * (1.0 - gate)).astype(BF16)
        dpre_ref[...] = dpre
        dpp_ref[...] = (dy * gate).astype(BF16)
        dh = _dot(dpre, wg_ref[...], NT)
        dg_ref[...] += jnp.sum(dh * xh, axis=0, keepdims=True)
        gd = dh * g_ref[...]
        dx_ref[...] = dy + r * (gd - xh * jnp.mean(gd * xh, axis=-1, keepdims=True))

    row = lambda w: pl.BlockSpec((tm, w), lambda i: (i, 0))
    full = lambda a: pl.BlockSpec(a.shape, lambda i: (0, 0))
    return pl.pallas_call(
        body, name="ple_loss", grid=(t // tm,),
        in_specs=[row(D_MODEL), full(g), full(w_gate), row(PLE_DIM), full(w_proj_t), row(D_MODEL)],
        out_specs=[row(D_MODEL), row(D_MODEL), row(D_MODEL), row(D_MODEL),
                   pl.BlockSpec((1, 128), lambda i: (0, 0)), pl.BlockSpec((1, D_MODEL), lambda i: (0, 0))],
        out_shape=[SDS((t, D_MODEL), F32), SDS((t, D_MODEL), BF16), SDS((t, D_MODEL), BF16), SDS((t, D_MODEL), BF16),
                   SDS((1, 128), F32), SDS((1, D_MODEL), F32)],
        compiler_params=_params("arbitrary"))(x2, g, w_gate, p, w_proj_t, target)


def _exchange(scatter, gather, name):
    arrays = list(scatter) + list(gather)
    n_a, n_s = len(arrays), len(scatter)

    def body(*refs):
        src, dst = refs[:n_a], refs[n_a:2 * n_a]
        send_sems, recv_sems, local_sems = refs[2 * n_a:]
        x, y, c = lax.axis_index("x"), lax.axis_index("y"), lax.axis_index("c")
        me = 4 * x + 2 * y + c

        def src_of(a, slot):
            return src[a].at[slot] if a < n_s else src[a]

        local = [pltpu.make_async_copy(src_of(a, me), dst[a].at[me], local_sems.at[a]) for a in range(n_a)]
        for cp in local:
            cp.start()
        sends, peers = [], []
        for k in range(1, N_DEV):
            px = 1 - x if k & 4 else x
            py = 1 - y if k & 2 else y
            pc = 1 - c if k & 1 else c
            peer = 4 * px + 2 * py + pc
            peers.append(peer)
            for a in range(n_a):
                cp = pltpu.make_async_remote_copy(
                    src_ref=src_of(a, peer), dst_ref=dst[a].at[me], send_sem=send_sems.at[a, k - 1],
                    recv_sem=recv_sems.at[a, k - 1], device_id=(px, py, pc), device_id_type=pl.DeviceIdType.MESH)
                cp.start()
                sends.append(cp)
        for k in range(1, N_DEV):
            peer = peers[k - 1]
            for a in range(n_a):
                pltpu.make_async_remote_copy(
                    src_ref=src_of(a, peer), dst_ref=dst[a].at[peer], send_sem=send_sems.at[a, k - 1],
                    recv_sem=recv_sems.at[a, k - 1], device_id=(x, y, c),
                    device_id_type=pl.DeviceIdType.MESH).wait_recv()
        for cp in sends:
            cp.wait_send()
        for cp in local:
            cp.wait()

    out_shape = [SDS(a.shape, a.dtype) for a in scatter] + [SDS((N_DEV,) + a.shape, a.dtype) for a in gather]
    hbm = pl.BlockSpec(memory_space=pl.ANY)
    return pl.pallas_call(
        body, name=name, in_specs=[hbm] * n_a, out_specs=[hbm] * n_a, out_shape=out_shape,
        scratch_shapes=[pltpu.SemaphoreType.DMA((n_a, N_DEV - 1)), pltpu.SemaphoreType.DMA((n_a, N_DEV - 1)),
                        pltpu.SemaphoreType.DMA((n_a,))],
        )(*arrays)


def _reduce8(a, tr, name):
    _, rows, cols = a.shape

    def body(a_ref, o_ref):
        acc = a_ref[0]
        for j in range(1, N_DEV):
            acc = acc + a_ref[j]
        o_ref[...] = acc

    return pl.pallas_call(
        body, name=name, grid=(rows // tr,),
        in_specs=[pl.BlockSpec((N_DEV, tr, cols), lambda i: (0, i, 0))],
        out_specs=pl.BlockSpec((tr, cols), lambda i: (i, 0)), out_shape=SDS((rows, cols), F32),
        compiler_params=_params("parallel"))(a)


def _adamw(w, g, m, v, name, tr=None):
    rows, cols = w.shape
    tr = rows if tr is None else tr
    c1 = 1.0 - ADAM_B1 ** ADAM_STEP
    c2 = 1.0 - ADAM_B2 ** ADAM_STEP

    def body(w_ref, g_ref, m_ref, v_ref, d_ref, mo_ref, vo_ref):
        gv = g_ref[...]
        mn = ADAM_B1 * m_ref[...] + (1.0 - ADAM_B1) * gv
        vn = ADAM_B2 * v_ref[...] + (1.0 - ADAM_B2) * (gv * gv)
        mo_ref[...] = mn
        vo_ref[...] = vn
        d_ref[...] = -ADAM_LR * ((mn / c1) / (jnp.sqrt(vn / c2) + ADAM_EPS) + ADAM_WD * w_ref[...])

    blk = pl.BlockSpec((tr, cols), lambda i: (i, 0))
    return pl.pallas_call(
        body, name=name, grid=(rows // tr,), in_specs=[blk] * 4, out_specs=[blk] * 3,
        out_shape=[SDS((rows, cols), F32)] * 3, compiler_params=_params("parallel"))(w, g, m, v)


def _pad_rows(a, rows):
    return jnp.pad(a, ((0, rows - a.shape[0]),) + ((0, 0),) * (a.ndim - 1))


def _local_step(x, p, target, sm, wts):
    t = x.shape[0]
    ones_q, ones_k, dup, dup_t = _head_consts()
    tri, triu, expand, expand_t = _ssd_consts()
    w_in_t, w_out, w_up_t, w_down, w_gate, w_proj_t = (wts[k] for k in ("in_t", "out", "up_t", "down", "gate", "proj_t"))
    cwx, cwb = wts["ssm_cw"][:, :SSM_INNER], wts["ssm_cw"][:, SSM_INNER:]
    cbx, cbb = sm["ssm_conv_b"][:, :SSM_INNER], sm["ssm_conv_b"][:, SSM_INNER:]
    pad128 = lambda a: jnp.pad(a, ((0, 0), (0, 128 - a.shape[1])))
    dtb, alog = pad128(sm["dt_bias"]), pad128(sm["a_log"])
    dsk_e = jnp.repeat(sm["d_skip"], HEAD_DIM, axis=1)
    gq = jnp.tile(sm["q_norm_g"], (1, ATTN_DIM // HEAD_DIM))
    gk = jnp.tile(sm["k_norm_g"], (1, KV_DIM // HEAD_DIM))
    ffn_cw = wts["ffn_cw"].reshape(3, 2, 1, D_FF)
    ffn_cb = sm["ffn_conv_b"].reshape(2, 1, D_FF)

    proj, h1 = _norm_matmul(x, sm["attn_norm_g"], w_in_t, 512, 768, "in_proj")
    qn, kd, vd = _attn_prep(proj, gq, gk, ones_q, ones_k, dup)
    attn_out, lse = _attn_fwd(qn, kd, vd)
    y_ssd, hs, ssm_out = _ssd_fwd(proj, cwx, cbx, cwb, cbb, dtb, alog, dsk_e, sm["ssm_norm_g"], tri, expand)
    x1 = _mm_resid([(attn_out, None, w_out[:ATTN_DIM]), (ssm_out, None, w_out[ATTN_DIM:])], x, None, 512, F32,
                   "out_proj")
    u, h2 = _norm_matmul(x1, sm["ffn_norm_g"], w_up_t, 512, 1408, "up_proj")
    f = _ffn_act(u, ffn_cw, ffn_cb)
    x2 = _mm_resid([(f, None, w_down)], x1, None, 512, F32, "down_proj")
    dx2, dpre, dpp, h3, loss, dg_ple = _ple_loss(x2, sm["ple_norm_g"], w_gate, p, w_proj_t, target)

    g_gate = _wgrad(h3, None, dpre, "wg_gate")
    g_proj_t = _wgrad(dpp, None, p, "wg_proj")
    g_down = _wgrad(f, None, dx2, "wg_down")
    duc, d_ffn_cw, d_ffn_cb = _ffn_act_bwd(dx2, w_down, u, ffn_cw, ffn_cb)
    du = _ffn_conv_t(duc, ffn_cw)
    dx1, dg_ffn = _mm_normbwd([(du, 0, w_up_t[:D_FF]), (du, 1, w_up_t[D_FF:])], x1, sm["ffn_norm_g"], dx2, 256,
                              "up_proj_bwd")
    g_up_t = jnp.concatenate([_wgrad(du, 0, h2, "wg_up_gate"), _wgrad(du, 1, h2, "wg_up_val")], axis=0)
    g_out = jnp.concatenate([_wgrad(attn_out, None, dx1, "wg_out_attn"), _wgrad(ssm_out, None, dx1, "wg_out_ssm")],
                            axis=0)
    d_attn = _mm_resid([(dx1, None, w_out[:ATTN_DIM])], None, NT, 512, F32, "out_proj_bwd_attn")
    d_ssm = _mm_resid([(dx1, None, w_out[ATTN_DIM:])], None, NT, 512, F32, "out_proj_bwd_ssm")
    (dz, dxs, dbc, ddt, dg_ssm, d_dsk_e, d_alog, d_dtb, d_cwx, d_cbx, d_cwb, d_cbb) = _ssd_bwd(
        proj, y_ssd, hs, d_ssm, cwx, cbx, cwb, cbb, dtb, alog, dsk_e, sm["ssm_norm_g"], tri, triu, expand, expand_t)
    dxs = dxs.astype(BF16)
    dqn, dkc, dkp, dvc, dvp = _attn_bwd(qn, kd, vd, attn_out, lse, d_attn)
    dqkv, dgq, dgk = _attn_prep_bwd(proj, dqn, dkc, dkp, dvc, dvp, gq, gk, ones_q, ones_k, dup_t)
    pieces = [(dqkv, 0, 1024), (dz, 1024, 2048), (dxs, 2048, 3072), (dbc, 3072, 3584), (ddt, 3584, 3712)]
    grad_x, dg_attn = _mm_normbwd([(a, None, w_in_t[lo:hi]) for a, lo, hi in pieces], x, sm["attn_norm_g"], dx1, 256,
                                  "in_proj_bwd")
    g_in_t = jnp.concatenate([_wgrad(a, None, h1, "wg_in_%d" % lo) for a, lo, _ in pieces], axis=0)[:IN_PROJ]

    big = dict(in_t=g_in_t, out=g_out, up_t=g_up_t, down=g_down, gate=g_gate, proj_t=g_proj_t)
    small = dict(
        attn_norm_g=dg_attn, q_norm_g=dgq.reshape(-1, HEAD_DIM).sum(0, keepdims=True),
        k_norm_g=dgk.reshape(-1, HEAD_DIM).sum(0, keepdims=True),
        ssm_conv_w=jnp.concatenate([d_cwx, d_cwb], axis=1), ssm_conv_b=jnp.concatenate([d_cbx, d_cbb], axis=1),
        dt_bias=d_dtb[:, :SSM_HEADS], a_log=d_alog[:, :SSM_HEADS],
        d_skip=d_dsk_e.reshape(SSM_HEADS, HEAD_DIM).sum(1)[None, :], ssm_norm_g=dg_ssm, ffn_norm_g=dg_ffn,
        ffn_conv_w=d_ffn_cw.reshape(3, 2 * D_FF), ffn_conv_b=d_ffn_cb.reshape(1, 2 * D_FF), ple_norm_g=dg_ple)
    return loss[0, 0], grad_x, big, small


_BIG = (("in_t", 450, 512), ("out", 192, 192), ("up_t", 704, 704), ("down", 352, 352), ("gate", 128, 128),
        ("proj_t", 32, 32))
_BIG_ROWS = sum(r for _, _, r in _BIG)
_SMALL = (("attn_norm_g", 1024), ("q_norm_g", 64), ("k_norm_g", 64), ("ssm_conv_w", 4 * XBC_DIM),
          ("ssm_conv_b", XBC_DIM), ("dt_bias", 16), ("a_log", 16), ("d_skip", 16), ("ssm_norm_g", 1024),
          ("ffn_norm_g", 1024), ("ffn_conv_w", 3 * 2 * D_FF), ("ffn_conv_b", 2 * D_FF), ("ple_norm_g", 1024))
_SMALL_ROWS = 34
_SHARD_SMALL = (("attn_norm_g", 1024), ("q_norm_g", 64), ("k_norm_g", 64), ("ssm_conv_w", 4 * XBC_DIM // N_DEV),
                ("ssm_conv_b", XBC_DIM), ("dt_bias", 16), ("a_log", 16), ("d_skip", 16), ("ssm_norm_g", 1024),
                ("ffn_norm_g", 1024), ("ffn_conv_w", 3 * 2 * D_FF // N_DEV), ("ffn_conv_b", 2 * D_FF),
                ("ple_norm_g", 1024))
_SHARD_SMALL_ROWS = 14


def _pack_flat(parts, order, rows):
    flat = jnp.concatenate([parts[name].reshape(-1) for name, _ in order])
    return jnp.pad(flat, (0, rows * 1024 - flat.shape[0])).reshape(rows, 1024)


def _unpack_flat(packed, order):
    flat, out, pos = packed.reshape(-1), {}, 0
    for name, size in order:
        out[name] = flat[pos:pos + size]
        pos += size
    return out


def kernel(x, p, attn_norm_g, w_in, q_norm_g, k_norm_g, ssm_conv_w, ssm_conv_b, dt_bias, a_log, d_skip, ssm_norm_g, w_out, ffn_norm_g, w_up, ffn_conv_w, ffn_conv_b, w_down, ple_norm_g, w_ple_gate, w_ple_proj, loss_target, m_attn_norm_g, m_w_in, m_q_norm_g, m_k_norm_g, m_ssm_conv_w, m_ssm_conv_b, m_dt_bias, m_a_log, m_d_skip, m_ssm_norm_g, m_w_out, m_ffn_norm_g, m_w_up, m_ffn_conv_w, m_ffn_conv_b, m_w_down, m_ple_norm_g, m_w_ple_gate, m_w_ple_proj, v_attn_norm_g, v_w_in, v_q_norm_g, v_k_norm_g, v_ssm_conv_w, v_ssm_conv_b, v_dt_bias, v_a_log, v_d_skip, v_ssm_norm_g, v_w_out, v_ffn_norm_g, v_w_up, v_ffn_conv_w, v_ffn_conv_b, v_w_down, v_ple_norm_g, v_w_ple_gate, v_w_ple_proj):
    names = ("attn_norm_g", "w_in", "q_norm_g", "k_norm_g", "ssm_conv_w", "ssm_conv_b", "dt_bias", "a_log", "d_skip",
             "ssm_norm_g", "w_out", "ffn_norm_g", "w_up", "ffn_conv_w", "ffn_conv_b", "w_down", "ple_norm_g",
             "w_ple_gate", "w_ple_proj")
    w = dict(zip(names, (attn_norm_g, w_in, q_norm_g, k_norm_g, ssm_conv_w, ssm_conv_b, dt_bias, a_log, d_skip,
                         ssm_norm_g, w_out, ffn_norm_g, w_up, ffn_conv_w, ffn_conv_b, w_down, ple_norm_g, w_ple_gate,
                         w_ple_proj)))
    m = dict(zip(names, (m_attn_norm_g, m_w_in, m_q_norm_g, m_k_norm_g, m_ssm_conv_w, m_ssm_conv_b, m_dt_bias,
                         m_a_log, m_d_skip, m_ssm_norm_g, m_w_out, m_ffn_norm_g, m_w_up, m_ffn_conv_w, m_ffn_conv_b,
                         m_w_down, m_ple_norm_g, m_w_ple_gate, m_w_ple_proj)))
    v = dict(zip(names, (v_attn_norm_g, v_w_in, v_q_norm_g, v_k_norm_g, v_ssm_conv_w, v_ssm_conv_b, v_dt_bias,
                         v_a_log, v_d_skip, v_ssm_norm_g, v_w_out, v_ffn_norm_g, v_w_up, v_ffn_conv_w, v_ffn_conv_b,
                         v_w_down, v_ple_norm_g, v_w_ple_gate, v_w_ple_proj)))
    w, m, v = ({k: a[0] for k, a in d.items()} for d in (w, m, v))
    me = 4 * lax.axis_index("x") + 2 * lax.axis_index("y") + lax.axis_index("c")

    mine = dict(in_t=w["w_in"].T, out=w["w_out"], up_t=w["w_up"].T, down=w["w_down"], gate=w["w_ple_gate"],
                proj_t=w["w_ple_proj"].T.reshape(32, 1024))
    pack = jnp.concatenate([_pad_rows(mine[k].astype(BF16), r) for k, _, r in _BIG], axis=0)
    conv_pack = jnp.pad(jnp.concatenate([w["ssm_conv_w"].reshape(-1), w["ffn_conv_w"].reshape(-1)]),
                        (0, 3072 - 2880)).reshape(8, 384)
    all_w, all_conv = _exchange([], [pack, conv_pack], "gather_weights")
    wts, pos = {}, 0
    for k, r, rp in _BIG:
        wts[k] = all_w[:, pos:pos + r].reshape(N_DEV * r, 1024)
        pos += rp
    wts["in_t"] = _pad_rows(wts["in_t"], IN_PROJ_PAD)
    wts["proj_t"] = wts["proj_t"].reshape(D_MODEL, PLE_DIM)
    conv_flat = all_conv.reshape(N_DEV, 3072)
    wts["ssm_cw"] = conv_flat[:, :768].reshape(N_DEV, 4, XBC_DIM // N_DEV).transpose(1, 0, 2).reshape(4, XBC_DIM)
    wts["ffn_cw"] = conv_flat[:, 768:2880].reshape(N_DEV, 3, 2 * D_FF // N_DEV).transpose(1, 0, 2).reshape(3, 2 * D_FF)
    sm = {k: w[k].reshape(1, -1) for k, _ in _SMALL if k not in ("ssm_conv_w", "ffn_conv_w")}

    loss, grad_x, big, small = _local_step(x[0], p[0, 0], loss_target[0], sm, wts)
    loss = lax.psum(loss, ("x", "y", "c"))

    big["proj_t"] = big["proj_t"].reshape(N_DEV * 32, 1024)
    send = jnp.concatenate([_pad_rows(big[k].reshape(N_DEV, r, 1024).transpose(1, 0, 2), rp).transpose(1, 0, 2)
                            for k, r, rp in _BIG], axis=1)
    small_pack = _pack_flat(small, _SMALL, _SMALL_ROWS)
    got_big, got_small = _exchange([send], [small_pack], "exchange_grads")
    g_big = _reduce8(got_big, 240, "reduce_big")
    g_small = _unpack_flat(_reduce8(got_small, _SMALL_ROWS, "reduce_small"), _SMALL)

    grads, pos = {}, 0
    for k, r, rp in _BIG:
        grads[k] = g_big[pos:pos + r]
        pos += rp
    gw = {"w_in": grads["in_t"].T, "w_out": grads["out"], "w_up": grads["up_t"].T, "w_down": grads["down"],
          "w_ple_gate": grads["gate"], "w_ple_proj": grads["proj_t"].reshape(128, PLE_DIM).T}
    for k, size in _SMALL:
        gw[k] = g_small[k].reshape(w[k].shape) if k not in ("ssm_conv_w", "ffn_conv_w") else None
    n_ssm, n_ffn = XBC_DIM // N_DEV, 2 * D_FF // N_DEV
    gw["ssm_conv_w"] = lax.dynamic_slice(g_small["ssm_conv_w"].reshape(4, XBC_DIM), (0, me * n_ssm), (4, n_ssm))
    gw["ffn_conv_w"] = lax.dynamic_slice(g_small["ffn_conv_w"].reshape(3, 2 * D_FF), (0, me * n_ffn), (3, n_ffn))

    delta, new_m, new_v = {}, {}, {}
    for k, tr in (("w_in", 256), ("w_out", None), ("w_up", 256), ("w_down", None), ("w_ple_gate", None),
                  ("w_ple_proj", None)):
        delta[k], new_m[k], new_v[k] = _adamw(w[k], gw[k], m[k], v[k], "adamw_" + k, tr)
    packs = [_pack_flat(d, _SHARD_SMALL, _SHARD_SMALL_ROWS) for d in (w, gw, m, v)]
    for d, packed in zip((delta, new_m, new_v), _adamw(*packs, "adamw_small")):
        for k, a in _unpack_flat(packed, _SHARD_SMALL).items():
            d[k] = a.reshape(w[k].shape)

    outs = [loss, grad_x[None]]
    for d in (gw, delta, new_m, new_v):
        outs += [d[k][None] for k in names]
    return tuple(outs)
```

```python
import functools

import numpy as np
import jax
import jax.numpy as jnp
from jax import lax
from jax.experimental import pallas as pl
from jax.experimental.pallas import tpu as pltpu

F32 = jnp.float32
BF16 = jnp.bfloat16
SDS = jax.ShapeDtypeStruct
EPS = 1e-6
N_DEV = 8
D_MODEL = 1024
HEAD_DIM = 64
ATTN_DIM = 512
KV_DIM = 256
SSM_INNER = 1024
SSM_HEADS = 16
BC_DIM = 256
XBC_DIM = SSM_INNER + 2 * BC_DIM
MIX_DIM = ATTN_DIM + SSM_INNER
IN_PROJ = 3600
IN_PROJ_PAD = 3840
D_FF = 2816
PLE_DIM = 256
CHUNK = 128
SUPER = 2048
DILATIONS = (1, 4, 16)
VMEM_LIMIT = 56 * 1024 * 1024
ADAM_LR, ADAM_B1, ADAM_B2, ADAM_EPS, ADAM_WD, ADAM_STEP = 0.001, 0.9, 0.999, 1e-08, 0.01, 10

NT = (((1,), (1,)), ((), ()))
TN = (((0,), (0,)), ((), ()))
HI = lax.Precision.HIGHEST


def _params(*sem):
    return pltpu.CompilerParams(dimension_semantics=sem if sem else None, vmem_limit_bytes=VMEM_LIMIT)


def _dot(a, b, dims=None):
    if dims is None:
        return jnp.dot(a, b, preferred_element_type=F32)
    return lax.dot_general(a, b, dims, preferred_element_type=F32)


def _hdot(a, b):
    return jnp.dot(a, b, precision=HI, preferred_element_type=F32)


def _sigmoid(x):
    return 1.0 / (1.0 + jnp.exp(-x))


def _shift_down(x, halo8, s):
    if s == 0:
        return x
    n = x.shape[0]
    row = lax.broadcasted_iota(jnp.int32, x.shape, 0)
    return jnp.where(row < s, jnp.tile(pltpu.roll(halo8, s, 0), (n // 8, 1)), pltpu.roll(x, s, 0))


def _shift_up(x, halo8, s):
    if s == 0:
        return x
    n = x.shape[0]
    row = lax.broadcasted_iota(jnp.int32, x.shape, 0)
    return jnp.where(row >= n - s, jnp.tile(pltpu.roll(halo8, 8 - s, 0), (n // 8, 1)), pltpu.roll(x, n - s, 0))


def _norm_matmul(x, g, wt, tm, tn, name):
    t, k = x.shape
    n = wt.shape[0]

    def body(x_ref, g_ref, w_ref, o_ref, h_ref):
        @pl.when(pl.program_id(1) == 0)
        def _():
            xv = x_ref[...]
            r = lax.rsqrt(jnp.mean(xv * xv, axis=-1, keepdims=True) + EPS)
            h_ref[...] = (xv * r * g_ref[...]).astype(BF16)
        o_ref[...] = _dot(h_ref[...], w_ref[...], NT)

    return pl.pallas_call(
        body, name=name, grid=(t // tm, n // tn),
        in_specs=[pl.BlockSpec((tm, k), lambda i, j: (i, 0)), pl.BlockSpec((1, k), lambda i, j: (0, 0)),
                  pl.BlockSpec((tn, k), lambda i, j: (j, 0))],
        out_specs=[pl.BlockSpec((tm, tn), lambda i, j: (i, j)), pl.BlockSpec((tm, k), lambda i, j: (i, 0))],
        out_shape=[SDS((t, n), F32), SDS((t, k), BF16)],
        compiler_params=_params("parallel", "arbitrary"))(x, g, wt)


def _a_spec(a, lead, tm):
    if lead is None:
        return pl.BlockSpec((tm, a.shape[-1]), lambda i: (i, 0))
    return pl.BlockSpec((None, tm, a.shape[-1]), lambda i, _l=lead: (_l, i, 0))


def _mm_resid(pairs, res, dims, tm, out_dtype, name):
    t = pairs[0][0].shape[-2]
    n = pairs[0][2].shape[1] if dims is None else pairs[0][2].shape[0]
    np_ = len(pairs)

    def body(*refs):
        o_ref = refs[-1]
        acc = refs[2 * np_][...] if res is not None else None
        for q in range(np_):
            d = _dot(refs[q][...].astype(BF16), refs[np_ + q][...], dims)
            acc = d if acc is None else acc + d
        o_ref[...] = acc.astype(out_dtype)

    in_specs = [_a_spec(a, lead, tm) for a, lead, _ in pairs]
    in_specs += [pl.BlockSpec(b.shape, lambda i: (0, 0)) for _, _, b in pairs]
    args = [a for a, _, _ in pairs] + [b for _, _, b in pairs]
    if res is not None:
        in_specs.append(pl.BlockSpec((tm, n), lambda i: (i, 0)))
        args.append(res)
    return pl.pallas_call(
        body, name=name, grid=(t // tm,), in_specs=in_specs,
        out_specs=pl.BlockSpec((tm, n), lambda i: (i, 0)), out_shape=SDS((t, n), out_dtype),
        compiler_params=_params("parallel"))(*args)


def _mm_normbwd(pairs, x, g, dres, tm, name):
    t, k = x.shape
    np_ = len(pairs)

    def body(*refs):
        x_ref, g_ref, dres_ref, dx_ref, dg_ref = refs[2 * np_:]
        dh = None
        for q in range(np_):
            d = _dot(refs[q][...], refs[np_ + q][...])
            dh = d if dh is None else dh + d
        xv = x_ref[...]
        r = lax.rsqrt(jnp.mean(xv * xv, axis=-1, keepdims=True) + EPS)
        xh = xv * r

        @pl.when(pl.program_id(0) == 0)
        def _():
            dg_ref[...] = jnp.zeros_like(dg_ref)
        dg_ref[...] += jnp.sum(dh * xh, axis=0, keepdims=True)
        gd = dh * g_ref[...]
        dx_ref[...] = dres_ref[...] + r * (gd - xh * jnp.mean(gd * xh, axis=-1, keepdims=True))

    in_specs = [_a_spec(a, lead, tm) for a, lead, _ in pairs]
    in_specs += [pl.BlockSpec(b.shape, lambda i: (0, 0)) for _, _, b in pairs]
    in_specs += [pl.BlockSpec((tm, k), lambda i: (i, 0)), pl.BlockSpec((1, k), lambda i: (0, 0)),
                 pl.BlockSpec((tm, k), lambda i: (i, 0))]
    args = [a for a, _, _ in pairs] + [b for _, _, b in pairs] + [x, g, dres]
    return pl.pallas_call(
        body, name=name, grid=(t // tm,), in_specs=in_specs,
        out_specs=[pl.BlockSpec((tm, k), lambda i: (i, 0)), pl.BlockSpec((1, k), lambda i: (0, 0))],
        out_shape=[SDS((t, k), F32), SDS((1, k), F32)],
        compiler_params=_params("arbitrary"))(*args)


def _wgrad(a, a_lead, b, name, tk=512):
    t, m = a.shape[-2:]
    n = b.shape[1]
    tm = m if m <= 1024 else 1408
    assert m % tm == 0

    def body(a_ref, b_ref, o_ref):
        @pl.when(pl.program_id(1) == 0)
        def _():
            o_ref[...] = jnp.zeros_like(o_ref)
        o_ref[...] += _dot(a_ref[...].astype(BF16), b_ref[...].astype(BF16), TN)

    if a_lead is None:
        a_spec = pl.BlockSpec((tk, tm), lambda mi, ki: (ki, mi))
    else:
        a_spec = pl.BlockSpec((None, tk, tm), lambda mi, ki, _l=a_lead: (_l, ki, mi))
    return pl.pallas_call(
        body, name=name, grid=(m // tm, t // tk),
        in_specs=[a_spec, pl.BlockSpec((tk, n), lambda mi, ki: (ki, 0))],
        out_specs=pl.BlockSpec((tm, n), lambda mi, ki: (mi, 0)), out_shape=SDS((m, n), F32),
        compiler_params=_params("parallel", "arbitrary"))(a, b)


def _head_consts():
    iq = np.arange(ATTN_DIM)
    ik = np.arange(KV_DIM)
    ones_q = (iq[:, None] // HEAD_DIM == iq[None, :] // HEAD_DIM).astype(np.float32)
    ones_k = (ik[:, None] // HEAD_DIM == ik[None, :] // HEAD_DIM).astype(np.float32)
    dup = (ik[:, None] == (HEAD_DIM * (iq[None, :] // 128) + iq[None, :] % HEAD_DIM)).astype(np.float32)
    return jnp.asarray(ones_q), jnp.asarray(ones_k), jnp.asarray(dup, BF16), jnp.asarray(dup.T)


def _attn_prep(proj, gq, gk, ones_q, ones_k, dup, tm=512):
    t = proj.shape[0]

    def body(p_ref, gq_ref, gk_ref, oq_ref, ok_ref, dup_ref, qn_ref, kd_ref, vd_ref):
        q = p_ref[:, 0:ATTN_DIM]
        k = p_ref[:, ATTN_DIM:ATTN_DIM + KV_DIM]
        v = p_ref[:, ATTN_DIM + KV_DIM:]
        rq = lax.rsqrt(_hdot(q * q, oq_ref[...]) * (1.0 / HEAD_DIM) + EPS)
        qn_ref[...] = (q * rq * gq_ref[...]) * (HEAD_DIM ** -0.5)
        rk = lax.rsqrt(_hdot(k * k, ok_ref[...]) * (1.0 / HEAD_DIM) + EPS)
        kn = k * rk * gk_ref[...]
        kd_ref[...] = _dot(kn.astype(BF16), dup_ref[...])
        vd_ref[...] = _dot(v.astype(BF16), dup_ref[...])

    full = lambda a: pl.BlockSpec(a.shape, lambda i: (0, 0))
    o_spec = pl.BlockSpec((tm, ATTN_DIM), lambda i: (i, 0))
    return pl.pallas_call(
        body, name="attn_prep", grid=(t // tm,),
        in_specs=[pl.BlockSpec((tm, 1024), lambda i: (i, 0)), full(gq), full(gk), full(ones_q), full(ones_k), full(dup)],
        out_specs=[o_spec, o_spec, o_spec], out_shape=[SDS((t, ATTN_DIM), F32)] * 3,
        compiler_params=_params("parallel"))(proj, gq, gk, ones_q, ones_k, dup)


def _attn_prep_bwd(proj, dqn, dkc, dkp, dvc, dvp, gq, gk, ones_q, ones_k, dup_t, tm=512):
    t = proj.shape[0]
    nblk = t // tm
    off = SUPER // tm

    def body(p_ref, dqn_ref, dkc_ref, dkp_ref, dvc_ref, dvp_ref, gq_ref, gk_ref, oq_ref, ok_ref, dt_ref,
             o_ref, dgq_ref, dgk_ref):
        i = pl.program_id(0)
        has_next = (i + off < nblk).astype(F32)
        q = p_ref[:, 0:ATTN_DIM]
        k = p_ref[:, ATTN_DIM:ATTN_DIM + KV_DIM]
        dkn = _hdot(dkc_ref[...] + has_next * dkp_ref[...], dt_ref[...])
        dv = _hdot(dvc_ref[...] + has_next * dvp_ref[...], dt_ref[...])

        @pl.when(i == 0)
        def _():
            dgq_ref[...] = jnp.zeros_like(dgq_ref)
            dgk_ref[...] = jnp.zeros_like(dgk_ref)

        rq = lax.rsqrt(_hdot(q * q, oq_ref[...]) * (1.0 / HEAD_DIM) + EPS)
        xh = q * rq
        dy = dqn_ref[...] * (HEAD_DIM ** -0.5)
        dgq_ref[...] += jnp.sum(dy * xh, axis=0, keepdims=True)
        gd = dy * gq_ref[...]
        dq = rq * (gd - xh * (_hdot(gd * xh, oq_ref[...]) * (1.0 / HEAD_DIM)))
        rk = lax.rsqrt(_hdot(k * k, ok_ref[...]) * (1.0 / HEAD_DIM) + EPS)
        kh = k * rk
        dgk_ref[...] += jnp.sum(dkn * kh, axis=0, keepdims=True)
        gdk = dkn * gk_ref[...]
        dk = rk * (gdk - kh * (_hdot(gdk * kh, ok_ref[...]) * (1.0 / HEAD_DIM)))
        o_ref[:, 0:ATTN_DIM] = dq.astype(BF16)
        o_ref[:, ATTN_DIM:ATTN_DIM + KV_DIM] = dk.astype(BF16)
        o_ref[:, ATTN_DIM + KV_DIM:] = dv.astype(BF16)

    full = lambda a: pl.BlockSpec(a.shape, lambda i: (0, 0))
    cur = pl.BlockSpec((tm, ATTN_DIM), lambda i: (i, 0))
    nxt = pl.BlockSpec((tm, ATTN_DIM), lambda i: (jnp.minimum(i + off, nblk - 1), 0))
    return pl.pallas_call(
        body, name="attn_prep_bwd", grid=(nblk,),
        in_specs=[pl.BlockSpec((tm, 1024), lambda i: (i, 0)), cur, cur, nxt, cur, nxt,
                  full(gq), full(gk), full(ones_q), full(ones_k), full(dup_t)],
        out_specs=[pl.BlockSpec((tm, 1024), lambda i: (i, 0)), pl.BlockSpec((1, ATTN_DIM), lambda i: (0, 0)),
                   pl.BlockSpec((1, KV_DIM), lambda i: (0, 0))],
        out_shape=[SDS((t, 1024), BF16), SDS((1, ATTN_DIM), F32), SDS((1, KV_DIM), F32)],
        compiler_params=_params("arbitrary"))(proj, dqn, dkc, dkp, dvc, dvp, gq, gk, ones_q, ones_k, dup_t)


def _tile_masks():
    qi = lax.broadcasted_iota(jnp.int32, (2 * CHUNK, 2 * CHUNK), 0) & (CHUNK - 1)
    kj = lax.broadcasted_iota(jnp.int32, (2 * CHUNK, 2 * CHUNK), 1)
    delta = CHUNK + qi - kj
    band = (delta >= 0) & (delta <= CHUNK)
    return band, kj


def _deinterleave(dst, src, n_rows, d):
    per = n_rows // d
    for r in range(d):
        dst[r * per:(r + 1) * per, :] = src[pl.ds(r, per, stride=d), :]


def _attn_specs(t):
    blk = lambda f: pl.BlockSpec((SUPER, 128), f)
    cur = blk(lambda h, s: (s, h))
    prev = blk(lambda h, s: (jnp.maximum(s - 1, 0), h))
    return cur, prev


def _attn_fwd(qn, kd, vd):
    t = qn.shape[0]
    cur, prev = _attn_specs(t)

    def body(q_ref, kp_ref, kc_ref, vp_ref, vc_ref, o_ref, lse_ref, kk, vv, qd, kdd, vdd, po, pm, pll, acc, mm, ll):
        s = pl.program_id(1)
        kk[0:SUPER, :] = kp_ref[...]
        kk[SUPER:, :] = kc_ref[...]
        vv[0:SUPER, :] = vp_ref[...]
        vv[SUPER:, :] = vc_ref[...]
        m0 = lax.broadcasted_iota(jnp.int32, (CHUNK, 128), 1) < HEAD_DIM
        band, kj = _tile_masks()
        for d in DILATIONS:
            lq = SUPER // d
            if d == 1:
                qs_ref, ks_ref, vs_ref = q_ref, kk, vv
            else:
                _deinterleave(qd, q_ref, SUPER, d)
                _deinterleave(kdd, kk, 2 * SUPER, d)
                _deinterleave(vdd, vv, 2 * SUPER, d)
                qs_ref, ks_ref, vs_ref = qd, kdd, vdd

            def tile(ti, carry):
                r = ti // (lq // CHUNK)
                nb = ti % (lq // CHUNK)
                qoff = pl.multiple_of(ti * CHUNK, CHUNK)
                koff = pl.multiple_of(r * 2 * lq + lq + (nb - 1) * CHUNK, CHUNK)
                qt = qs_ref[pl.ds(qoff, CHUNK), :]
                qs = jnp.concatenate([jnp.where(m0, qt, 0.0), jnp.where(m0, 0.0, qt)], axis=0).astype(BF16)
                kt = ks_ref[pl.ds(koff, 2 * CHUNK), :].astype(BF16)
                vt = vs_ref[pl.ds(koff, 2 * CHUNK), :].astype(BF16)
                sc = _dot(qs, kt, NT)
                ok = band & (kj >= jnp.where((s > 0) | (nb > 0), 0, CHUNK))
                sc = jnp.where(ok, sc, -jnp.inf)
                mt = jnp.max(sc, axis=-1, keepdims=True)
                p = jnp.exp(sc - mt)
                lt = jnp.sum(p, axis=-1, keepdims=True)
                ot = _dot(p.astype(BF16), vt)
                po[pl.ds(qoff, CHUNK), :] = jnp.where(m0, ot[:CHUNK], ot[CHUNK:])
                pm[pl.ds(qoff, CHUNK), :] = jnp.where(m0, mt[:CHUNK], mt[CHUNK:])
                pll[pl.ds(qoff, CHUNK), :] = jnp.where(m0, lt[:CHUNK], lt[CHUNK:])
                return carry

            lax.fori_loop(0, SUPER // CHUNK, tile, 0)
            if d == 1:
                acc[...] = po[...]
                mm[...] = pm[...]
                ll[...] = pll[...]
            else:
                for r in range(d):
                    rows = pl.ds(r, lq, stride=d)
                    seg = slice(r * lq, (r + 1) * lq)
                    m_old, m_new = mm[rows, :], pm[seg, :]
                    m_all = jnp.maximum(m_old, m_new)
                    a, b = jnp.exp(m_old - m_all), jnp.exp(m_new - m_all)
                    acc[rows, :] = acc[rows, :] * a + po[seg, :] * b
                    ll[rows, :] = ll[rows, :] * a + pll[seg, :] * b
                    mm[rows, :] = m_all
        o_ref[...] = acc[...] / ll[...]
        lse_ref[...] = mm[...] + jnp.log(ll[...])

    big = pltpu.VMEM((2 * SUPER, 128), F32)
    one = pltpu.VMEM((SUPER, 128), F32)
    return pl.pallas_call(
        body, name="attn_fwd", grid=(4, t // SUPER),
        in_specs=[cur, prev, cur, prev, cur], out_specs=[cur, cur],
        out_shape=[SDS((t, ATTN_DIM), F32)] * 2,
        scratch_shapes=[big, big, one, big, big, one, one, one, one, one, one],
        compiler_params=_params("parallel", "arbitrary"))(qn, kd, kd, vd, vd)


def _attn_bwd(qn, kd, vd, out, lse, dout):
    t = qn.shape[0]
    cur, prev = _attn_specs(t)

    def body(q_ref, kp_ref, kc_ref, vp_ref, vc_ref, o_ref, lse_ref, do_ref,
             dq_ref, dkc_ref, dkp_ref, dvc_ref, dvp_ref,
             kk, vv, dkk, dvv, qd, od, ld, dod, kdd, vdd, dkd, dvd, pdq):
        s = pl.program_id(1)
        kk[0:SUPER, :] = kp_ref[...]
        kk[SUPER:, :] = kc_ref[...]
        vv[0:SUPER, :] = vp_ref[...]
        vv[SUPER:, :] = vc_ref[...]
        dkk[...] = jnp.zeros_like(dkk)
        dvv[...] = jnp.zeros_like(dvv)
        dq_ref[...] = jnp.zeros_like(dq_ref)
        m0 = lax.broadcasted_iota(jnp.int32, (CHUNK, 128), 1) < HEAD_DIM
        band, kj = _tile_masks()
        ninf = -jnp.inf
        for d in DILATIONS:
            lq = SUPER // d
            if d == 1:
                qs_ref, os_ref, ls_ref, dos_ref, ks_ref, vs_ref, dks_ref, dvs_ref = (
                    q_ref, o_ref, lse_ref, do_ref, kk, vv, dkk, dvv)
            else:
                _deinterleave(qd, q_ref, SUPER, d)
                _deinterleave(od, o_ref, SUPER, d)
                _deinterleave(ld, lse_ref, SUPER, d)
                _deinterleave(dod, do_ref, SUPER, d)
                _deinterleave(kdd, kk, 2 * SUPER, d)
                _deinterleave(vdd, vv, 2 * SUPER, d)
                dkd[...] = jnp.zeros_like(dkd)
                dvd[...] = jnp.zeros_like(dvd)
                qs_ref, os_ref, ls_ref, dos_ref, ks_ref, vs_ref, dks_ref, dvs_ref = (
                    qd, od, ld, dod, kdd, vdd, dkd, dvd)

            def tile(ti, carry):
                r = ti // (lq // CHUNK)
                nb = ti % (lq // CHUNK)
                qoff = pl.multiple_of(ti * CHUNK, CHUNK)
                koff = pl.multiple_of(r * 2 * lq + lq + (nb - 1) * CHUNK, CHUNK)
                qrows = pl.ds(qoff, CHUNK)
                krows = pl.ds(koff, 2 * CHUNK)
                qt, ot, lt, dot_ = qs_ref[qrows, :], os_ref[qrows, :], ls_ref[qrows, :], dos_ref[qrows, :]
                qs = jnp.concatenate([jnp.where(m0, qt, 0.0), jnp.where(m0, 0.0, qt)], axis=0).astype(BF16)
                dos = jnp.concatenate([jnp.where(m0, dot_, 0.0), jnp.where(m0, 0.0, dot_)], axis=0).astype(BF16)
                lse_rows = jnp.concatenate([jnp.max(jnp.where(m0, lt, ninf), axis=-1, keepdims=True),
                                            jnp.max(jnp.where(m0, ninf, lt), axis=-1, keepdims=True)], axis=0)
                prod = dot_ * ot
                dl_rows = jnp.concatenate([jnp.sum(jnp.where(m0, prod, 0.0), axis=-1, keepdims=True),
                                           jnp.sum(jnp.where(m0, 0.0, prod), axis=-1, keepdims=True)], axis=0)
                kt = ks_ref[krows, :].astype(BF16)
                vt = vs_ref[krows, :].astype(BF16)
                sc = _dot(qs, kt, NT)
                ok = band & (kj >= jnp.where((s > 0) | (nb > 0), 0, CHUNK))
                p = jnp.exp(jnp.where(ok, sc, ninf) - lse_rows)
                dp = _dot(dos, vt, NT)
                ds = p * (dp - dl_rows)
                dqs = _dot(ds.astype(BF16), kt)
                pdq[qrows, :] = jnp.where(m0, dqs[:CHUNK], dqs[CHUNK:])
                dks_ref[krows, :] += _dot(ds.T.astype(BF16), qs)
                dvs_ref[krows, :] += _dot(p.T.astype(BF16), dos)
                return carry

            lax.fori_loop(0, SUPER // CHUNK, tile, 0)
            if d == 1:
                dq_ref[...] += pdq[...]
            else:
                for r in range(d):
                    dq_ref[pl.ds(r, lq, stride=d), :] += pdq[r * lq:(r + 1) * lq, :]
                    dkk[pl.ds(r, 2 * lq, stride=d), :] += dkd[r * 2 * lq:(r + 1) * 2 * lq, :]
                    dvv[pl.ds(r, 2 * lq, stride=d), :] += dvd[r * 2 * lq:(r + 1) * 2 * lq, :]
        dkp_ref[...] = dkk[0:SUPER, :]
        dkc_ref[...] = dkk[SUPER:, :]
        dvp_ref[...] = dvv[0:SUPER, :]
        dvc_ref[...] = dvv[SUPER:, :]

    big = pltpu.VMEM((2 * SUPER, 128), F32)
    one = pltpu.VMEM((SUPER, 128), F32)
    return pl.pallas_call(
        body, name="attn_bwd", grid=(4, t // SUPER),
        in_specs=[cur, prev, cur, prev, cur, cur, cur, cur], out_specs=[cur] * 5,
        out_shape=[SDS((t, ATTN_DIM), F32)] * 5,
        scratch_shapes=[big, big, big, big, one, one, one, one, big, big, big, big, one],
        compiler_params=_params("parallel", "arbitrary"))(qn, kd, kd, vd, vd, out, lse, dout)


def _ssd_consts():
    tri = np.tril(np.ones((CHUNK, CHUNK), np.float32))
    expand = np.zeros((128, SSM_INNER), np.float32)
    for h in range(SSM_HEADS):
        expand[h, h * HEAD_DIM:(h + 1) * HEAD_DIM] = 1.0
    return jnp.asarray(tri), jnp.asarray(tri.T), jnp.asarray(expand), jnp.asarray(expand.T)


def _conv4(x, halo, w_ref, b_ref):
    acc = b_ref[...] + w_ref[3:4, :] * x
    for k in range(3):
        acc = acc + w_ref[k:k + 1, :] * _shift_down(x, halo, 3 - k)
    return acc


def _softplus(x):
    return jnp.maximum(x, 0.0) + jnp.log(1.0 + jnp.exp(-jnp.abs(x)))


def _ssd_common(xs_ref, bc_ref, dt_ref, hx_ref, hb_ref, cwx_ref, cbx_ref, cwb_ref, cbb_ref, dtb_ref, alog_ref,
                tri_ref, exp_ref, first):
    keep = 1.0 - first.astype(F32)
    hx = hx_ref[...] * keep
    hb = hb_ref[...] * keep
    pre_x = _conv4(xs_ref[...], hx, cwx_ref, cbx_ref)
    pre_b = _conv4(bc_ref[...], hb, cwb_ref, cbb_ref)
    xa = pre_x * _sigmoid(pre_x)
    ba = pre_b * _sigmoid(pre_b)
    dtv = _softplus(dt_ref[...] + dtb_ref[...])
    a_neg = -jnp.exp(alog_ref[...])
    acum = _hdot(tri_ref[...], dtv * a_neg)
    lam = jnp.exp(acum)
    gam = jnp.exp(acum[CHUNK - 1:CHUNK, :] - acum)
    dt_e = _hdot(dtv, exp_ref[...])
    lam_e = _hdot(lam, exp_ref[...])
    gam_e = _hdot(gam, exp_ref[...])
    return dict(hx=hx, hb=hb, pre_x=pre_x, pre_b=pre_b, xa=xa, ba=ba, dtv=dtv, a_neg=a_neg, acum=acum,
                dt_e=dt_e, lam_e=lam_e, gam_e=gam_e, xdt=xa * dt_e)


def _decay(acum_t, h, transposed):
    rb = jnp.broadcast_to(acum_t[h:h + 1, :], (CHUNK, CHUNK))
    ri = lax.broadcasted_iota(jnp.int32, (CHUNK, CHUNK), 0)
    ci = lax.broadcasted_iota(jnp.int32, (CHUNK, CHUNK), 1)
    if transposed:
        return jnp.exp(jnp.where(ci >= ri, rb - rb.T, -jnp.inf))
    return jnp.exp(jnp.where(ri >= ci, rb.T - rb, -jnp.inf))


def _ssd_specs(t, rev):
    nc = t // CHUNK
    ch = (lambda c: nc - 1 - c) if rev else (lambda c: c)
    col = lambda w, j: pl.BlockSpec((CHUNK, w), lambda c: (ch(c), j))
    halo = lambda w, j: pl.BlockSpec((8, w), lambda c: (jnp.maximum(ch(c) * (CHUNK // 8) - 1, 0), j))
    return nc, ch, col, halo


def _ssd_fwd(proj, cwx, cbx, cwb, cbb, dtb, alog, dsk_e, norm_g, tri, expand):
    t = proj.shape[0]
    nc, _, col, halo = _ssd_specs(t, False)

    def body(z_ref, xs_ref, bc_ref, dt_ref, hx_ref, hb_ref, cwx_ref, cbx_ref, cwb_ref, cbb_ref, dtb_ref, alog_ref,
             dsk_ref, g_ref, tri_ref, exp_ref, y_ref, hs_ref, o_ref, state):
        c = pl.program_id(0)

        @pl.when(c == 0)
        def _():
            state[...] = jnp.zeros_like(state)

        v = _ssd_common(xs_ref, bc_ref, dt_ref, hx_ref, hb_ref, cwx_ref, cbx_ref, cwb_ref, cbb_ref, dtb_ref,
                        alog_ref, tri_ref, exp_ref, c == 0)
        acum_t = v["acum"].T
        xdt, ba = v["xdt"], v["ba"]
        h_in = state[...]
        hs_ref[0] = h_in
        xg = xdt * v["gam_e"]
        m0 = lax.broadcasted_iota(jnp.int32, (CHUNK, 128), 1) < HEAD_DIM
        for g in range(2):
            bg = ba[:, g * 128:(g + 1) * 128].astype(BF16)
            cg = ba[:, 256 + g * 128:256 + (g + 1) * 128].astype(BF16)
            gl = slice(g * 512, (g + 1) * 512)
            cb = _dot(cg, bg, NT)
            y_off = _dot(cg, h_in[:, gl].astype(BF16)) * v["lam_e"][:, gl]
            s_new = _dot(bg.T, xg[:, gl].astype(BF16))
            state[:, gl] = h_in[:, gl] * v["lam_e"][CHUNK - 1:CHUNK, gl] + s_new
            for j in range(4):
                h0 = 8 * g + 2 * j
                ln = slice(g * 512 + j * 128, g * 512 + (j + 1) * 128)
                xp = xdt[:, ln].astype(BF16)
                y0 = _dot((cb * _decay(acum_t, h0, False)).astype(BF16), xp)
                y1 = _dot((cb * _decay(acum_t, h0 + 1, False)).astype(BF16), xp)
                y_ref[:, ln] = jnp.where(m0, y0, y1) + y_off[:, j * 128:(j + 1) * 128]
        z = z_ref[...]
        yg = (y_ref[...] + dsk_ref[...] * v["xa"]) * (z * _sigmoid(z))
        r = lax.rsqrt(jnp.mean(yg * yg, axis=-1, keepdims=True) + EPS)
        o_ref[...] = (yg * r * g_ref[...]).astype(BF16)

    full = lambda a: pl.BlockSpec(a.shape, lambda c: (0,) * a.ndim)
    return pl.pallas_call(
        body, name="ssd_fwd", grid=(nc,),
        in_specs=[col(1024, 1), col(1024, 2), col(512, 6), col(128, 28), halo(1024, 2), halo(512, 6),
                  full(cwx), full(cbx), full(cwb), full(cbb), full(dtb), full(alog), full(dsk_e), full(norm_g),
                  full(tri), full(expand)],
        out_specs=[pl.BlockSpec((CHUNK, SSM_INNER), lambda c: (c, 0)),
                   pl.BlockSpec((1, 128, SSM_INNER), lambda c: (c, 0, 0)),
                   pl.BlockSpec((CHUNK, SSM_INNER), lambda c: (c, 0))],
        out_shape=[SDS((t, SSM_INNER), F32), SDS((nc, 128, SSM_INNER), F32), SDS((t, SSM_INNER), BF16)],
        scratch_shapes=[pltpu.VMEM((128, SSM_INNER), F32)],
        compiler_params=_params("arbitrary"))(proj, proj, proj, proj, proj, proj, cwx, cbx, cwb, cbb, dtb, alog,
                                              dsk_e, norm_g, tri, expand)


def _ssd_bwd(proj, y_ssd, hs, dout, cwx, cbx, cwb, cbb, dtb, alog, dsk_e, norm_g, tri, triu, expand, expand_t):
    t = proj.shape[0]
    nc, ch, col, halo = _ssd_specs(t, True)

    def body(z_ref, xs_ref, bc_ref, dt_ref, hx_ref, hb_ref, y_ref, hin_ref, do_ref,
             cwx_ref, cbx_ref, cwb_ref, cbb_ref, dtb_ref, alog_ref, dsk_ref, g_ref, tri_ref, triu_ref, exp_ref, expt_ref,
             dz_ref, dxs_ref, dbc_ref, ddt_ref, dg_ref, ddsk_ref, dalog_ref, ddtb_ref, dcwx_ref, dcbx_ref, dcwb_ref,
             dcbb_ref, gstate, nx_x, nx_b, dact_b):
        step = pl.program_id(0)
        c = nc - 1 - step

        @pl.when(step == 0)
        def _():
            gstate[...] = jnp.zeros_like(gstate)
            nx_x[...] = jnp.zeros_like(nx_x)
            nx_b[...] = jnp.zeros_like(nx_b)
            for ref in (dg_ref, ddsk_ref, dalog_ref, ddtb_ref, dcwx_ref, dcbx_ref, dcwb_ref, dcbb_ref):
                ref[...] = jnp.zeros_like(ref)

        v = _ssd_common(xs_ref, bc_ref, dt_ref, hx_ref, hb_ref, cwx_ref, cbx_ref, cwb_ref, cbb_ref, dtb_ref,
                        alog_ref, tri_ref, exp_ref, c == 0)
        acum_t = v["acum"].T
        xa, ba, xdt, dtv = v["xa"], v["ba"], v["xdt"], v["dtv"]
        lam_e, gam_e, dt_e = v["lam_e"], v["gam_e"], v["dt_e"]
        z = z_ref[...]
        y = y_ref[...]
        sz = _sigmoid(z)
        zs = z * sz
        y_tot = y + dsk_ref[...] * xa
        yg = y_tot * zs
        r = lax.rsqrt(jnp.mean(yg * yg, axis=-1, keepdims=True) + EPS)
        yh = yg * r
        do = do_ref[...]
        dg_ref[...] += jnp.sum(do * yh, axis=0, keepdims=True)
        gd = do * g_ref[...]
        dyg = r * (gd - yh * jnp.mean(gd * yh, axis=-1, keepdims=True))
        dz_ref[...] = (dyg * y_tot * (sz * (1.0 + z * (1.0 - sz)))).astype(BF16)
        dy = dyg * zs
        ddsk_ref[...] += jnp.sum(dy * xa, axis=0, keepdims=True)
        g_out = gstate[...]
        h_in = hin_ref[0]
        lam_dy = lam_e * dy
        gam_x = gam_e * xdt
        m0 = lax.broadcasted_iota(jnp.int32, (CHUNK, 128), 1) < HEAD_DIM
        lane = lax.broadcasted_iota(jnp.int32, (CHUNK, 128), 1)
        below = (lax.broadcasted_iota(jnp.int32, (CHUNK, CHUNK), 0) >
                 lax.broadcasted_iota(jnp.int32, (CHUNK, CHUNK), 1))
        da_in = jnp.zeros((CHUNK, 128), F32)
        off_y, off_x = [], []
        for g in range(2):
            bg = ba[:, g * 128:(g + 1) * 128].astype(BF16)
            cg = ba[:, 256 + g * 128:256 + (g + 1) * 128].astype(BF16)
            gl = slice(g * 512, (g + 1) * 512)
            gg = g_out[:, gl].astype(BF16)
            bc_t = _dot(bg, cg, NT)
            cb = _dot(cg, bg, NT)
            dxdt_off = _dot(bg, gg) * gam_e[:, gl]
            off_x.append(xdt[:, gl] * dxdt_off)
            off_y.append(dy[:, gl] * (_dot(cg, h_in[:, gl].astype(BF16)) * lam_e[:, gl]))
            q_sum = jnp.zeros((CHUNK, CHUNK), F32)
            for j in range(4):
                h0 = 8 * g + 2 * j
                ln = slice(g * 512 + j * 128, g * 512 + (j + 1) * 128)
                dyp = dy[:, ln]
                dyb = dyp.astype(BF16)
                xpb = xdt[:, ln].astype(BF16)
                d0 = _dot((bc_t * _decay(acum_t, h0, True)).astype(BF16), dyb)
                d1 = _dot((bc_t * _decay(acum_t, h0 + 1, True)).astype(BF16), dyb)
                dxs_ref[:, ln] = jnp.where(m0, d0, d1) + dxdt_off[:, j * 128:(j + 1) * 128]
                for hh, dym in ((h0, jnp.where(m0, dyp, 0.0)), (h0 + 1, jnp.where(m0, 0.0, dyp))):
                    qd = _dot(dym.astype(BF16), xpb, NT) * _decay(acum_t, hh, False)
                    q_sum = q_sum + qd
                    reach = jnp.where(below, _hdot(triu_ref[...], qd * cb), 0.0)
                    da_in = jnp.where(lane == hh, jnp.sum(reach, axis=-1, keepdims=True), da_in)
            gstate[:, gl] = g_out[:, gl] * lam_e[CHUNK - 1:CHUNK, gl] + _dot(cg.T, lam_dy[:, gl].astype(BF16))
            qb = q_sum.astype(BF16)
            dact_b[:, 256 + g * 128:256 + (g + 1) * 128] = (
                _dot(qb, bg) + _dot(lam_dy[:, gl].astype(BF16), h_in[:, gl].astype(BF16), NT))
            dact_b[:, g * 128:(g + 1) * 128] = _dot(qb.T, cg) + _dot(gam_x[:, gl].astype(BF16), gg, NT)
        dxdt = dxs_ref[...]
        seg_y = _hdot(jnp.concatenate(off_y, axis=1), expt_ref[...])
        seg_x = _hdot(jnp.concatenate(off_x, axis=1), expt_ref[...])
        e_col = jnp.sum(g_out * h_in * lam_e[CHUNK - 1:CHUNK, :], axis=0, keepdims=True)
        e_seg = _hdot(jnp.broadcast_to(e_col, (8, SSM_INNER)), expt_ref[...])[0:1, :]
        da = da_in + _hdot(triu_ref[...], seg_y) + (_hdot(tri_ref[...], seg_x) - seg_x) + e_seg
        a_neg = v["a_neg"]
        ddtv = da * a_neg + _hdot(dxdt * xa, expt_ref[...])
        dalog_ref[...] += jnp.sum(da * dtv, axis=0, keepdims=True) * a_neg
        lane16 = lax.broadcasted_iota(jnp.int32, (CHUNK, 128), 1) < SSM_HEADS
        draw = jnp.where(lane16, ddtv * _sigmoid(dt_ref[...] + dtb_ref[...]), 0.0)
        ddtb_ref[...] += jnp.sum(draw, axis=0, keepdims=True)
        ddt_ref[...] = draw.astype(BF16)
        dxa = dxdt * dt_e + dy * dsk_ref[...]
        for (dact, pre, x_ref, halo_v, nx, cw_ref, dcw_ref, dcb_ref, dx_ref) in (
                (dxa, v["pre_x"], xs_ref, v["hx"], nx_x, cwx_ref, dcwx_ref, dcbx_ref, dxs_ref),
                (dact_b[...], v["pre_b"], bc_ref, v["hb"], nx_b, cwb_ref, dcwb_ref, dcbb_ref, dbc_ref)):
            sp = _sigmoid(pre)
            dpre = dact * (sp * (1.0 + pre * (1.0 - sp)))
            dcb_ref[...] += jnp.sum(dpre, axis=0, keepdims=True)
            xv = x_ref[...]
            nxt = nx[...]
            dx = cw_ref[3:4, :] * dpre
            dcw_ref[3:4, :] += jnp.sum(dpre * xv, axis=0, keepdims=True)
            for k in range(3):
                dcw_ref[k:k + 1, :] += jnp.sum(dpre * _shift_down(xv, halo_v, 3 - k), axis=0, keepdims=True)
                dx = dx + cw_ref[k:k + 1, :] * _shift_up(dpre, nxt, 3 - k)
            nx[...] = dpre[0:8, :]
            dx_ref[...] = dx.astype(dx_ref.dtype)

    full = lambda a: pl.BlockSpec(a.shape, lambda c: (0,) * a.ndim)
    rowblk = lambda w: pl.BlockSpec((CHUNK, w), lambda c: (ch(c), 0))
    acc = lambda a, b: pl.BlockSpec((a, b), lambda c: (0, 0))
    return pl.pallas_call(
        body, name="ssd_bwd", grid=(nc,),
        in_specs=[col(1024, 1), col(1024, 2), col(512, 6), col(128, 28), halo(1024, 2), halo(512, 6),
                  rowblk(SSM_INNER),
                  pl.BlockSpec((1, 128, SSM_INNER), lambda c: (ch(c), 0, 0)),
                  rowblk(SSM_INNER),
                  full(cwx), full(cbx), full(cwb), full(cbb), full(dtb), full(alog), full(dsk_e), full(norm_g),
                  full(tri), full(triu), full(expand), full(expand_t)],
        out_specs=[rowblk(SSM_INNER), rowblk(SSM_INNER), rowblk(512), rowblk(128),
                   acc(1, 1024), acc(1, 1024), acc(1, 128), acc(1, 128), acc(4, 1024), acc(1, 1024), acc(4, 512),
                   acc(1, 512)],
        out_shape=[SDS((t, SSM_INNER), BF16), SDS((t, SSM_INNER), F32), SDS((t, 512), BF16), SDS((t, 128), BF16),
                   SDS((1, 1024), F32), SDS((1, 1024), F32), SDS((1, 128), F32), SDS((1, 128), F32),
                   SDS((4, 1024), F32), SDS((1, 1024), F32), SDS((4, 512), F32), SDS((1, 512), F32)],
        scratch_shapes=[pltpu.VMEM((128, SSM_INNER), F32), pltpu.VMEM((8, 1024), F32), pltpu.VMEM((8, 512), F32),
                        pltpu.VMEM((CHUNK, 512), F32)],
        compiler_params=_params("arbitrary"))(proj, proj, proj, proj, proj, proj, y_ssd, hs, dout,
                                              cwx, cbx, cwb, cbb, dtb, alog, dsk_e, norm_g, tri, triu, expand,
                                              expand_t)


def _conv3(x, halo, w_ref, b_ref, part):
    acc = b_ref[part] + w_ref[2, part] * x
    for k in range(2):
        acc = acc + w_ref[k, part] * _shift_down(x, halo, 2 - k)
    return acc


def _ffn_specs(t, tm, tn, order):
    nj = D_FF // tn
    ij = (lambda a, b: (b, a)) if order == "ji" else (lambda a, b: (a, b))

    def spec(shape, f):
        return pl.BlockSpec(shape, lambda a, b: f(*ij(a, b)))
    u_g = spec((tm, tn), lambda i, j: (i, j))
    u_v = spec((tm, tn), lambda i, j: (i, j + nj))
    h_g = spec((8, tn), lambda i, j: (jnp.maximum(i * (tm // 8) - 1, 0), j))
    h_v = spec((8, tn), lambda i, j: (jnp.maximum(i * (tm // 8) - 1, 0), j + nj))
    w = spec((3, 2, 1, tn), lambda i, j: (0, 0, 0, j))
    b = spec((2, 1, tn), lambda i, j: (0, 0, j))
    return nj, spec, u_g, u_v, h_g, h_v, w, b


def _ffn_act(u, cw, cb, tm=512, tn=1408):
    t = u.shape[0]
    nj, spec, u_g, u_v, h_g, h_v, w, b = _ffn_specs(t, tm, tn, "ij")

    def body(ug_ref, uv_ref, hg_ref, hv_ref, w_ref, b_ref, f_ref):
        keep = (pl.program_id(0) > 0).astype(F32)
        gate = _conv3(ug_ref[...], hg_ref[...] * keep, w_ref, b_ref, 0)
        val = _conv3(uv_ref[...], hv_ref[...] * keep, w_ref, b_ref, 1)
        f_ref[...] = (gate * _sigmoid(gate) * val).astype(BF16)

    return pl.pallas_call(
        body, name="ffn_act", grid=(t // tm, nj), in_specs=[u_g, u_v, h_g, h_v, w, b],
        out_specs=spec((tm, tn), lambda i, j: (i, j)), out_shape=SDS((t, D_FF), BF16),
        compiler_params=_params("parallel", "parallel"))(u, u, u, u, cw, cb)


def _ffn_act_bwd(dx2, w_down, u, cw, cb, tm=512, tn=1408):
    t = u.shape[0]
    nj, spec, u_g, u_v, h_g, h_v, w, b = _ffn_specs(t, tm, tn, "ji")

    def body(dx_ref, wd_ref, ug_ref, uv_ref, hg_ref, hv_ref, w_ref, b_ref, duc_ref, dcw_ref, dcb_ref):
        i = pl.program_id(1)
        first = i == 0
        df = _dot(dx_ref[...].astype(BF16), wd_ref[...], NT)
        ug, uv = ug_ref[...], uv_ref[...]
        keep = 1.0 - first.astype(F32)
        hg, hv = hg_ref[...] * keep, hv_ref[...] * keep
        gate = _conv3(ug, hg, w_ref, b_ref, 0)
        val = _conv3(uv, hv, w_ref, b_ref, 1)
        sg = _sigmoid(gate)
        dgate = df * val * (sg * (1.0 + gate * (1.0 - sg)))
        dval = df * (gate * sg)
        duc_ref[0] = dgate
        duc_ref[1] = dval

        @pl.when(first)
        def _():
            dcw_ref[...] = jnp.zeros_like(dcw_ref)
            dcb_ref[...] = jnp.zeros_like(dcb_ref)

        for part, (du, uu, hh) in enumerate(((dgate, ug, hg), (dval, uv, hv))):
            dcb_ref[part] += jnp.sum(du, axis=0, keepdims=True)
            for k in range(3):
                dcw_ref[k, part] += jnp.sum(du * _shift_down(uu, hh, 2 - k), axis=0, keepdims=True)

    return pl.pallas_call(
        body, name="ffn_act_bwd", grid=(nj, t // tm),
        in_specs=[spec((tm, D_MODEL), lambda i, j: (i, 0)), spec((tn, D_MODEL), lambda i, j: (j, 0)),
                  u_g, u_v, h_g, h_v, w, b],
        out_specs=[pl.BlockSpec((2, tm, tn), lambda j, i: (0, i, j)), w, b],
        out_shape=[SDS((2, t, D_FF), F32), SDS((3, 2, 1, D_FF), F32), SDS((2, 1, D_FF), F32)],
        compiler_params=_params("parallel", "arbitrary"))(dx2, w_down, u, u, u, u, cw, cb)


def _ffn_conv_t(duc, cw, tm=512, tn=1408):
    t = duc.shape[1]
    nj = D_FF // tn
    nrow8 = t // 8

    def body(d_ref, n_ref, w_ref, o_ref):
        keep = (pl.program_id(0) < pl.num_programs(0) - 1).astype(F32)
        for part in range(2):
            dv = d_ref[part]
            nxt = n_ref[part] * keep
            acc = w_ref[2, part] * dv
            for k in range(2):
                acc = acc + w_ref[k, part] * _shift_up(dv, nxt, 2 - k)
            o_ref[part] = acc.astype(BF16)

    return pl.pallas_call(
        body, name="ffn_conv_t", grid=(t // tm, nj),
        in_specs=[pl.BlockSpec((2, tm, tn), lambda i, j: (0, i, j)),
                  pl.BlockSpec((2, 8, tn), lambda i, j: (0, jnp.minimum((i + 1) * (tm // 8), nrow8 - 1), j)),
                  pl.BlockSpec((3, 2, 1, tn), lambda i, j: (0, 0, 0, j))],
        out_specs=pl.BlockSpec((2, tm, tn), lambda i, j: (0, i, j)), out_shape=SDS((2, t, D_FF), BF16),
        compiler_params=_params("parallel", "parallel"))(duc, duc, cw)


def _ple_loss(x2, g, w_gate, p, w_proj_t, target, tm=256):
    t = x2.shape[0]

    def body(x_ref, g_ref, wg_ref, p_ref, wp_ref, tg_ref, dx_ref, dpre_ref, dpp_ref, h_ref, loss_ref, dg_ref):
        i = pl.program_id(0)
        xv = x_ref[...]
        r = lax.rsqrt(jnp.mean(xv * xv, axis=-1, keepdims=True) + EPS)
        xh = xv * r
        h = (xh * g_ref[...]).astype(BF16)
        h_ref[...] = h
        gate = _sigmoid(_dot(h, wg_ref[...]))
        pp = _dot(p_ref[...].astype(BF16), wp_ref[...], NT)
        err = (xv + gate * pp) - tg_ref[...]

        @pl.when(i == 0)
        def _():
            loss_ref[...] = jnp.zeros_like(loss_ref)
            dg_ref[...] = jnp.zeros_like(dg_ref)

        loss_ref[...] += 0.5 * jnp.sum(jnp.mean(err * err, axis=-1, keepdims=True), axis=0, keepdims=True)
        dy = err * (1.0 / D_MODEL)
        dpre = (dy * pp * gate * (1.0 - gate)).astype(BF16)
        dpre_ref[...] = dpre
        dpp_ref[...] = (dy * gate).astype(BF16)
        dh = _dot(dpre, wg_ref[...], NT)
        dg_ref[...] += jnp.sum(dh * xh, axis=0, keepdims=True)
        gd = dh * g_ref[...]
        dx_ref[...] = dy + r * (gd - xh * jnp.mean(gd * xh, axis=-1, keepdims=True))

    row = lambda w: pl.BlockSpec((tm, w), lambda i: (i, 0))
    full = lambda a: pl.BlockSpec(a.shape, lambda i: (0, 0))
    return pl.pallas_call(
        body, name="ple_loss", grid=(t // tm,),
        in_specs=[row(D_MODEL), full(g), full(w_gate), row(PLE_DIM), full(w_proj_t), row(D_MODEL)],
        out_specs=[row(D_MODEL), row(D_MODEL), row(D_MODEL), row(D_MODEL),
                   pl.BlockSpec((1, 128), lambda i: (0, 0)), pl.BlockSpec((1, D_MODEL), lambda i: (0, 0))],
        out_shape=[SDS((t, D_MODEL), F32), SDS((t, D_MODEL), BF16), SDS((t, D_MODEL), BF16), SDS((t, D_MODEL), BF16),
                   SDS((1, 128), F32), SDS((1, D_MODEL), F32)],
        compiler_params=_params("arbitrary"))(x2, g, w_gate, p, w_proj_t, target)


def _exchange(scatter, gather, name):
    arrays = list(scatter) + list(gather)
    n_a, n_s = len(arrays), len(scatter)

    def body(*refs):
        src, dst = refs[:n_a], refs[n_a:2 * n_a]
        send_sems, recv_sems, local_sems = refs[2 * n_a:]
        x, y, c = lax.axis_index("x"), lax.axis_index("y"), lax.axis_index("c")
        me = 4 * x + 2 * y + c

        def src_of(a, slot):
            return src[a].at[slot] if a < n_s else src[a]

        local = [pltpu.make_async_copy(src_of(a, me), dst[a].at[me], local_sems.at[a]) for a in range(n_a)]
        for cp in local:
            cp.start()
        sends, peers = [], []
        for k in range(1, N_DEV):
            px = 1 - x if k & 4 else x
            py = 1 - y if k & 2 else y
            pc = 1 - c if k & 1 else c
            peer = 4 * px + 2 * py + pc
            peers.append(peer)
            for a in range(n_a):
                cp = pltpu.make_async_remote_copy(
                    src_ref=src_of(a, peer), dst_ref=dst[a].at[me], send_sem=send_sems.at[a, k - 1],
                    recv_sem=recv_sems.at[a, k - 1], device_id=(px, py, pc), device_id_type=pl.DeviceIdType.MESH)
                cp.start()
                sends.append(cp)
        for k in range(1, N_DEV):
            peer = peers[k - 1]
            for a in range(n_a):
                pltpu.make_async_remote_copy(
                    src_ref=src_of(a, peer), dst_ref=dst[a].at[peer], send_sem=send_sems.at[a, k - 1],
                    recv_sem=recv_sems.at[a, k - 1], device_id=(x, y, c),
                    device_id_type=pl.DeviceIdType.MESH).wait_recv()
        for cp in sends:
            cp.wait_send()
        for cp in local:
            cp.wait()

    out_shape = [SDS(a.shape, a.dtype) for a in scatter] + [SDS((N_DEV,) + a.shape, a.dtype) for a in gather]
    hbm = pl.BlockSpec(memory_space=pl.ANY)
    return pl.pallas_call(
        body, name=name, in_specs=[hbm] * n_a, out_specs=[hbm] * n_a, out_shape=out_shape,
        scratch_shapes=[pltpu.SemaphoreType.DMA((n_a, N_DEV - 1)), pltpu.SemaphoreType.DMA((n_a, N_DEV - 1)),
                        pltpu.SemaphoreType.DMA((n_a,))],
        )(*arrays)


def _peer(k):
    x, y, c = lax.axis_index("x"), lax.axis_index("y"), lax.axis_index("c")
    px = 1 - x if k & 4 else x
    py = 1 - y if k & 2 else y
    pc = 1 - c if k & 1 else c
    return (px, py, pc), 4 * px + 2 * py + pc


_HBM = pl.BlockSpec(memory_space=pltpu.HBM)
_SEM = pl.BlockSpec(memory_space=pltpu.SEMAPHORE)


def _split_copies(src, land, send_sems, recv_sems, scatter, arrivals):
    _, me = _peer(0)
    out = []
    for k in range(1, N_DEV):
        coords, peer = _peer(k)
        for a in range(len(src)):
            sem = a * (N_DEV - 1) + k - 1
            if scatter[a]:
                s, d = src[a].at[peer], land[a].at[k]
            else:
                s, d = src[a], land[a].at[peer if arrivals else me]
            out.append(pltpu.make_async_remote_copy(
                src_ref=s, dst_ref=d, send_sem=send_sems.at[sem], recv_sem=recv_sems.at[sem], device_id=coords,
                device_id_type=pl.DeviceIdType.MESH))
    return out


def _exchange_start(srcs, lands, scatter, name):
    n = len(srcs)

    def body(*refs):
        src, land = refs[:n], refs[n:2 * n]
        send_sems, recv_sems = refs[2 * n], refs[2 * n + 1]
        token = refs[-1]
        for cp in _split_copies(src, land, send_sems, recv_sems, scatter, False):
            cp.start()
        token[...] = jnp.zeros_like(token)

    hbm_shape = lambda a: pltpu.HBM(a.shape, a.dtype)
    sem_shape = pltpu.SemaphoreType.DMA((n * (N_DEV - 1),))
    outs = pl.pallas_call(
        body, name=name,
        out_shape=(sem_shape, sem_shape, *[hbm_shape(a) for a in srcs], *[hbm_shape(a) for a in lands],
                   SDS((8, 128), F32)),
        in_specs=[_HBM] * (2 * n), out_specs=(_SEM, _SEM, *[_HBM] * (2 * n), pl.BlockSpec(memory_space=pltpu.VMEM)),
        input_output_aliases={a: 2 + a for a in range(2 * n)},
        compiler_params=pltpu.CompilerParams(has_side_effects=pltpu.SideEffectType.DATAFLOW_SIDE_EFFECTING),
    )(*[pltpu.with_memory_space_constraint(a, pltpu.HBM) for a in list(srcs) + list(lands)])
    return outs[0], outs[1], outs[2:2 + n], outs[2 + n:2 + 2 * n], outs[-1]


def _exchange_wait(send_sems, recv_sems, srcs, lands, scatter, after, name):
    n = len(srcs)

    def body(*refs):
        src, land = refs[:n], refs[n:2 * n]
        for cp in _split_copies(src, land, refs[2 * n], refs[2 * n + 1], scatter, False):
            cp.wait_send()
        for cp in _split_copies(src, land, refs[2 * n], refs[2 * n + 1], scatter, True):
            cp.wait_recv()

    hbm_shape = lambda a: pltpu.HBM(a.shape, a.dtype)
    outs = pl.pallas_call(
        body, name=name, out_shape=tuple(hbm_shape(a) for a in list(srcs) + list(lands)),
        in_specs=[_HBM] * (2 * n) + [_SEM, _SEM, pl.BlockSpec(memory_space=pl.ANY)], out_specs=(_HBM,) * (2 * n),
        input_output_aliases={a: a for a in range(2 * n)},
        compiler_params=pltpu.CompilerParams(has_side_effects=pltpu.SideEffectType.DATAFLOW_SIDE_EFFECTING),
    )(*srcs, *lands, send_sems, recv_sems, after)
    return outs[:n], outs[n:]


def _reduce8(a, tr, name):
    _, rows, cols = a.shape

    def body(a_ref, o_ref):
        acc = a_ref[0]
        for j in range(1, N_DEV):
            acc = acc + a_ref[j]
        o_ref[...] = acc

    return pl.pallas_call(
        body, name=name, grid=(rows // tr,),
        in_specs=[pl.BlockSpec((N_DEV, tr, cols), lambda i: (0, i, 0))],
        out_specs=pl.BlockSpec((tr, cols), lambda i: (i, 0)), out_shape=SDS((rows, cols), F32),
        compiler_params=_params("parallel"))(a)


def _reduce_landed(own, land, name, tc=256):
    rows, cols = own.shape

    def body(own_ref, land_ref, o_ref):
        acc = own_ref[...]
        for k in range(1, N_DEV):
            acc = acc + land_ref[k]
        o_ref[...] = acc

    return pl.pallas_call(
        body, name=name, grid=(cols // tc,),
        in_specs=[pl.BlockSpec((rows, tc), lambda j: (0, j)), pl.BlockSpec((N_DEV, rows, tc), lambda j: (0, 0, j))],
        out_specs=pl.BlockSpec((rows, tc), lambda j: (0, j)), out_shape=SDS((rows, cols), F32),
        compiler_params=_params("parallel"))(own, land)


def _adamw(w, g, m, v, name, tr=None):
    rows, cols = w.shape
    tr = rows if tr is None else tr
    c1 = 1.0 - ADAM_B1 ** ADAM_STEP
    c2 = 1.0 - ADAM_B2 ** ADAM_STEP

    def body(w_ref, g_ref, m_ref, v_ref, d_ref, mo_ref, vo_ref):
        gv = g_ref[...]
        mn = ADAM_B1 * m_ref[...] + (1.0 - ADAM_B1) * gv
        vn = ADAM_B2 * v_ref[...] + (1.0 - ADAM_B2) * (gv * gv)
        mo_ref[...] = mn
        vo_ref[...] = vn
        d_ref[...] = -ADAM_LR * ((mn / c1) / (jnp.sqrt(vn / c2) + ADAM_EPS) + ADAM_WD * w_ref[...])

    blk = pl.BlockSpec((tr, cols), lambda i: (i, 0))
    return pl.pallas_call(
        body, name=name, grid=(rows // tr,), in_specs=[blk] * 4, out_specs=[blk] * 3,
        out_shape=[SDS((rows, cols), F32)] * 3, compiler_params=_params("parallel"))(w, g, m, v)


def _pad_rows(a, rows):
    return jnp.pad(a, ((0, rows - a.shape[0]),) + ((0, 0),) * (a.ndim - 1))


def _local_step(x, p, target, sm, wts, fetch_rest, send, tok):
    ones_q, ones_k, dup, dup_t = _head_consts()
    tri, triu, expand, expand_t = _ssd_consts()
    w_in_t = wts["in_t"]
    cwx, cwb = wts["ssm_cw"][:, :SSM_INNER], wts["ssm_cw"][:, SSM_INNER:]
    cbx, cbb = sm["ssm_conv_b"][:, :SSM_INNER], sm["ssm_conv_b"][:, SSM_INNER:]
    pad128 = lambda a: jnp.pad(a, ((0, 0), (0, 128 - a.shape[1])))
    dtb, alog = pad128(sm["dt_bias"]), pad128(sm["a_log"])
    dsk_e = jnp.repeat(sm["d_skip"], HEAD_DIM, axis=1)
    gq = jnp.tile(sm["q_norm_g"], (1, ATTN_DIM // HEAD_DIM))
    gk = jnp.tile(sm["k_norm_g"], (1, KV_DIM // HEAD_DIM))
    ffn_cw = wts["ffn_cw"].reshape(3, 2, 1, D_FF)
    ffn_cb = sm["ffn_conv_b"].reshape(2, 1, D_FF)

    proj, h1 = _norm_matmul(x, sm["attn_norm_g"] + tok, w_in_t, 512, 768, "in_proj")
    qn, kd, vd = _attn_prep(proj, gq, gk, ones_q, ones_k, dup)
    attn_out, lse = _attn_fwd(qn, kd, vd)
    y_ssd, hs, ssm_out = _ssd_fwd(proj, cwx, cbx, cwb, cbb, dtb, alog, dsk_e, sm["ssm_norm_g"], tri, expand)
    rest = fetch_rest(ssm_out)
    w_out, w_up_t, w_down, w_gate, w_proj_t = (rest[k] for k in ("out", "up_t", "down", "gate", "proj_t"))
    x1 = _mm_resid([(attn_out, None, w_out[:ATTN_DIM]), (ssm_out, None, w_out[ATTN_DIM:])], x, None, 512, F32,
                   "out_proj")
    u, h2 = _norm_matmul(x1, sm["ffn_norm_g"], w_up_t, 512, 1408, "up_proj")
    f = _ffn_act(u, ffn_cw, ffn_cb)
    x2 = _mm_resid([(f, None, w_down)], x1, None, 512, F32, "down_proj")
    dx2, dpre, dpp, h3, loss, dg_ple = _ple_loss(x2, sm["ple_norm_g"], w_gate, p, w_proj_t, target)

    g_gate = _wgrad(h3, None, dpre, "wg_gate")
    g_proj_t = _wgrad(dpp, None, p, "wg_proj")
    g_down = _wgrad(f, None, dx2, "wg_down")
    duc, d_ffn_cw, d_ffn_cb = _ffn_act_bwd(dx2, w_down, u, ffn_cw, ffn_cb)
    du = _ffn_conv_t(duc, ffn_cw)
    dx1, dg_ffn = _mm_normbwd([(du, 0, w_up_t[:D_FF]), (du, 1, w_up_t[D_FF:])], x1, sm["ffn_norm_g"], dx2, 256,
                              "up_proj_bwd")
    g_up_t = jnp.concatenate([_wgrad(du, 0, h2, "wg_up_gate"), _wgrad(du, 1, h2, "wg_up_val")], axis=0)
    tok = send(dict(gate=g_gate, proj_t=g_proj_t, down=g_down, up_t=g_up_t)).astype(BF16)
    d_attn = _mm_resid([(dx1, None, w_out[:ATTN_DIM] + tok)], None, NT, 512, F32, "out_proj_bwd_attn")
    d_ssm = _mm_resid([(dx1, None, w_out[ATTN_DIM:] + tok)], None, NT, 512, F32, "out_proj_bwd_ssm")
    g_out = jnp.concatenate([_wgrad(attn_out, None, dx1, "wg_out_attn"), _wgrad(ssm_out, None, dx1, "wg_out_ssm")],
                            axis=0)
    tok = send(dict(out=g_out))
    (dz, dxs, dbc, ddt, dg_ssm, d_dsk_e, d_alog, d_dtb, d_cwx, d_cbx, d_cwb, d_cbb) = _ssd_bwd(
        proj, y_ssd, hs, d_ssm, cwx, cbx, cwb, cbb, dtb + tok, alog, dsk_e, sm["ssm_norm_g"], tri, triu, expand,
        expand_t)
    dxs = dxs.astype(BF16)
    dqn, dkc, dkp, dvc, dvp = _attn_bwd(qn, kd, vd, attn_out, lse, d_attn)
    dqkv, dgq, dgk = _attn_prep_bwd(proj, dqn, dkc, dkp, dvc, dvp, gq + tok, gk, ones_q, ones_k, dup_t)
    pieces = [(dqkv, 0, 1024), (dz, 1024, 2048), (dxs, 2048, 3072), (dbc, 3072, 3584), (ddt, 3584, 3712)]
    g_in_t = jnp.concatenate([_wgrad(a, None, h1, "wg_in_%d" % lo) for a, lo, _ in pieces], axis=0)[:IN_PROJ]
    tok = send(dict(in_t=g_in_t))
    grad_x, dg_attn = _mm_normbwd([(a, None, w_in_t[lo:hi]) for a, lo, hi in pieces], x, sm["attn_norm_g"] + tok, dx1,
                                  256, "in_proj_bwd")

    small = dict(
        attn_norm_g=dg_attn, q_norm_g=dgq.reshape(-1, HEAD_DIM).sum(0, keepdims=True),
        k_norm_g=dgk.reshape(-1, HEAD_DIM).sum(0, keepdims=True),
        ssm_conv_w=jnp.concatenate([d_cwx, d_cwb], axis=1), ssm_conv_b=jnp.concatenate([d_cbx, d_cbb], axis=1),
        dt_bias=d_dtb[:, :SSM_HEADS], a_log=d_alog[:, :SSM_HEADS],
        d_skip=d_dsk_e.reshape(SSM_HEADS, HEAD_DIM).sum(1)[None, :], ssm_norm_g=dg_ssm, ffn_norm_g=dg_ffn,
        ffn_conv_w=d_ffn_cw.reshape(3, 2 * D_FF), ffn_conv_b=d_ffn_cb.reshape(1, 2 * D_FF), ple_norm_g=dg_ple)
    return loss[0, 0], grad_x, small


_SMALL = (("attn_norm_g", 1024), ("q_norm_g", 64), ("k_norm_g", 64), ("ssm_conv_w", 4 * XBC_DIM),
          ("ssm_conv_b", XBC_DIM), ("dt_bias", 16), ("a_log", 16), ("d_skip", 16), ("ssm_norm_g", 1024),
          ("ffn_norm_g", 1024), ("ffn_conv_w", 3 * 2 * D_FF), ("ffn_conv_b", 2 * D_FF), ("ple_norm_g", 1024))
_SMALL_ROWS = 34
_SHARD_SMALL = (("attn_norm_g", 1024), ("q_norm_g", 64), ("k_norm_g", 64), ("ssm_conv_w", 4 * XBC_DIM // N_DEV),
                ("ssm_conv_b", XBC_DIM), ("dt_bias", 16), ("a_log", 16), ("d_skip", 16), ("ssm_norm_g", 1024),
                ("ffn_norm_g", 1024), ("ffn_conv_w", 3 * 2 * D_FF // N_DEV), ("ffn_conv_b", 2 * D_FF),
                ("ple_norm_g", 1024))
_SHARD_SMALL_ROWS = 14


def _pack_flat(parts, order, rows):
    flat = jnp.concatenate([parts[name].reshape(-1) for name, _ in order])
    return jnp.pad(flat, (0, rows * 1024 - flat.shape[0])).reshape(rows, 1024)


def _unpack_flat(packed, order):
    flat, out, pos = packed.reshape(-1), {}, 0
    for name, size in order:
        out[name] = flat[pos:pos + size]
        pos += size
    return out


def kernel(x, p, attn_norm_g, w_in, q_norm_g, k_norm_g, ssm_conv_w, ssm_conv_b, dt_bias, a_log, d_skip, ssm_norm_g, w_out, ffn_norm_g, w_up, ffn_conv_w, ffn_conv_b, w_down, ple_norm_g, w_ple_gate, w_ple_proj, loss_target, m_attn_norm_g, m_w_in, m_q_norm_g, m_k_norm_g, m_ssm_conv_w, m_ssm_conv_b, m_dt_bias, m_a_log, m_d_skip, m_ssm_norm_g, m_w_out, m_ffn_norm_g, m_w_up, m_ffn_conv_w, m_ffn_conv_b, m_w_down, m_ple_norm_g, m_w_ple_gate, m_w_ple_proj, v_attn_norm_g, v_w_in, v_q_norm_g, v_k_norm_g, v_ssm_conv_w, v_ssm_conv_b, v_dt_bias, v_a_log, v_d_skip, v_ssm_norm_g, v_w_out, v_ffn_norm_g, v_w_up, v_ffn_conv_w, v_ffn_conv_b, v_w_down, v_ple_norm_g, v_w_ple_gate, v_w_ple_proj):
    names = ("attn_norm_g", "w_in", "q_norm_g", "k_norm_g", "ssm_conv_w", "ssm_conv_b", "dt_bias", "a_log", "d_skip",
             "ssm_norm_g", "w_out", "ffn_norm_g", "w_up", "ffn_conv_w", "ffn_conv_b", "w_down", "ple_norm_g",
             "w_ple_gate", "w_ple_proj")
    w = dict(zip(names, (attn_norm_g, w_in, q_norm_g, k_norm_g, ssm_conv_w, ssm_conv_b, dt_bias, a_log, d_skip,
                         ssm_norm_g, w_out, ffn_norm_g, w_up, ffn_conv_w, ffn_conv_b, w_down, ple_norm_g, w_ple_gate,
                         w_ple_proj)))
    m = dict(zip(names, (m_attn_norm_g, m_w_in, m_q_norm_g, m_k_norm_g, m_ssm_conv_w, m_ssm_conv_b, m_dt_bias,
                         m_a_log, m_d_skip, m_ssm_norm_g, m_w_out, m_ffn_norm_g, m_w_up, m_ffn_conv_w, m_ffn_conv_b,
                         m_w_down, m_ple_norm_g, m_w_ple_gate, m_w_ple_proj)))
    v = dict(zip(names, (v_attn_norm_g, v_w_in, v_q_norm_g, v_k_norm_g, v_ssm_conv_w, v_ssm_conv_b, v_dt_bias,
                         v_a_log, v_d_skip, v_ssm_norm_g, v_w_out, v_ffn_norm_g, v_w_up, v_ffn_conv_w, v_ffn_conv_b,
                         v_w_down, v_ple_norm_g, v_w_ple_gate, v_w_ple_proj)))
    w, m, v = ({k: a[0] for k, a in d.items()} for d in (w, m, v))
    me = 4 * lax.axis_index("x") + 2 * lax.axis_index("y") + lax.axis_index("c")

    mine = dict(in_t=w["w_in"].T, out=w["w_out"], up_t=w["w_up"].T, down=w["w_down"], gate=w["w_ple_gate"],
                proj_t=w["w_ple_proj"].T)
    mine = {k: a.astype(BF16) for k, a in mine.items()}
    conv_pack = jnp.pad(jnp.concatenate([w["ssm_conv_w"].reshape(-1), w["ffn_conv_w"].reshape(-1)]),
                        (0, 3072 - 2880)).reshape(8, 384)
    all_in, all_conv = _exchange([], [mine["in_t"], conv_pack], "gather_first")
    later = ("out", "up_t", "down", "gate", "proj_t")
    zones = [lax.dynamic_update_slice(lax.empty((N_DEV,) + mine[k].shape, BF16), mine[k][None], (me, 0, 0))
             for k in later]
    rest_state = _exchange_start([mine[k] for k in later], zones, [False] * len(later), "gather_rest_start")

    def fetch_rest(after):
        _, landed = _exchange_wait(*rest_state[:4], [False] * len(later), after, "gather_rest_wait")
        return {k: a.reshape(N_DEV * a.shape[1], a.shape[2]) for k, a in zip(later, landed)}

    wts = dict(in_t=_pad_rows(all_in.reshape(IN_PROJ, D_MODEL), IN_PROJ_PAD))
    conv_flat = all_conv.reshape(N_DEV, 3072)
    wts["ssm_cw"] = conv_flat[:, :768].reshape(N_DEV, 4, XBC_DIM // N_DEV).transpose(1, 0, 2).reshape(4, XBC_DIM)
    wts["ffn_cw"] = conv_flat[:, 768:2880].reshape(N_DEV, 3, 2 * D_FF // N_DEV).transpose(1, 0, 2).reshape(3, 2 * D_FF)
    sm = {k: w[k].reshape(1, -1) for k, _ in _SMALL if k not in ("ssm_conv_w", "ffn_conv_w")}

    in_flight = []

    def send(grads):
        keys = sorted(grads)
        srcs = [grads[k].reshape(N_DEV, grads[k].shape[0] // N_DEV, grads[k].shape[1]) for k in keys]
        state = _exchange_start(srcs, [lax.empty(a.shape, F32) for a in srcs], [True] * len(keys),
                                "send_" + "_".join(keys))
        in_flight.append((keys, state))
        return state[4][0:1, 0:1]

    loss, grad_x, small = _local_step(x[0], p[0, 0], loss_target[0], sm, wts, fetch_rest, send,
                                      rest_state[4][0:1, 0:1])
    loss = lax.psum(loss, ("x", "y", "c"))

    (got_small,) = _exchange([], [_pack_flat(small, _SMALL, _SMALL_ROWS)], "gather_small_grads")
    g_small = _unpack_flat(_reduce8(got_small, _SMALL_ROWS, "reduce_small"), _SMALL)
    grads = {}
    for keys, state in in_flight:
        sent, landed = _exchange_wait(*state[:4], [True] * len(keys), grad_x, "wait_" + "_".join(keys))
        for k, own, land in zip(keys, sent, landed):
            grads[k] = _reduce_landed(lax.dynamic_index_in_dim(own, me, 0, keepdims=False), land, "reduce_" + k)
    gw = {"w_in": grads["in_t"].T, "w_out": grads["out"], "w_up": grads["up_t"].T, "w_down": grads["down"],
          "w_ple_gate": grads["gate"], "w_ple_proj": grads["proj_t"].T}
    for k, size in _SMALL:
        gw[k] = g_small[k].reshape(w[k].shape) if k not in ("ssm_conv_w", "ffn_conv_w") else None
    n_ssm, n_ffn = XBC_DIM // N_DEV, 2 * D_FF // N_DEV
    gw["ssm_conv_w"] = lax.dynamic_slice(g_small["ssm_conv_w"].reshape(4, XBC_DIM), (0, me * n_ssm), (4, n_ssm))
    gw["ffn_conv_w"] = lax.dynamic_slice(g_small["ffn_conv_w"].reshape(3, 2 * D_FF), (0, me * n_ffn), (3, n_ffn))

    delta, new_m, new_v = {}, {}, {}
    for k, tr in (("w_in", 256), ("w_out", None), ("w_up", 256), ("w_down", None), ("w_ple_gate", None),
                  ("w_ple_proj", None)):
        delta[k], new_m[k], new_v[k] = _adamw(w[k], gw[k], m[k], v[k], "adamw_" + k, tr)
    packs = [_pack_flat(d, _SHARD_SMALL, _SHARD_SMALL_ROWS) for d in (w, gw, m, v)]
    for d, packed in zip((delta, new_m, new_v), _adamw(*packs, "adamw_small")):
        for k, a in _unpack_flat(packed, _SHARD_SMALL).items():
            d[k] = a.reshape(w[k].shape)

    outs = [loss, grad_x[None]]
    for d in (gw, delta, new_m, new_v):
        outs += [d[k][None] for k in names]
    return tuple(outs)
```

```python
import functools

import numpy as np
import jax
import jax.numpy as jnp
from jax import lax
from jax.experimental import pallas as pl
from jax.experimental.pallas import tpu as pltpu

F32 = jnp.float32
BF16 = jnp.bfloat16
SDS = jax.ShapeDtypeStruct
EPS = 1e-6
N_DEV = 8
D_MODEL = 1024
HEAD_DIM = 64
ATTN_DIM = 512
KV_DIM = 256
SSM_INNER = 1024
SSM_HEADS = 16
BC_DIM = 256
XBC_DIM = SSM_INNER + 2 * BC_DIM
MIX_DIM = ATTN_DIM + SSM_INNER
IN_PROJ = 3600
IN_PROJ_PAD = 3840
D_FF = 2816
PLE_DIM = 256
CHUNK = 128
SUPER = 2048
DILATIONS = (1, 4, 16)
TILE_UNROLL = 4
VMEM_LIMIT = 56 * 1024 * 1024
ADAM_LR, ADAM_B1, ADAM_B2, ADAM_EPS, ADAM_WD, ADAM_STEP = 0.001, 0.9, 0.999, 1e-08, 0.01, 10

NT = (((1,), (1,)), ((), ()))
TN = (((0,), (0,)), ((), ()))


def _params(*sem):
    return pltpu.CompilerParams(dimension_semantics=sem if sem else None, vmem_limit_bytes=VMEM_LIMIT)


def _dot(a, b, dims=None):
    if dims is None:
        return jnp.dot(a, b, preferred_element_type=F32)
    return lax.dot_general(a, b, dims, preferred_element_type=F32)


def _hdot(a, b, parts=2):
    a_exact = a.dtype == BF16
    x = b if a_exact else a
    acc = None
    for _ in range(parts):
        piece = x.astype(BF16)
        x = x - piece.astype(F32)
        d = _dot(a, piece) if a_exact else _dot(piece, b)
        acc = d if acc is None else acc + d
    return acc


def _sigmoid(x):
    return 1.0 / (1.0 + jnp.exp(-x))


def _shift_down(x, halo8, s):
    if s == 0:
        return x
    n = x.shape[0]
    row = lax.broadcasted_iota(jnp.int32, x.shape, 0)
    return jnp.where(row < s, jnp.tile(pltpu.roll(halo8, s, 0), (n // 8, 1)), pltpu.roll(x, s, 0))


def _shift_up(x, halo8, s):
    if s == 0:
        return x
    n = x.shape[0]
    row = lax.broadcasted_iota(jnp.int32, x.shape, 0)
    return jnp.where(row >= n - s, jnp.tile(pltpu.roll(halo8, 8 - s, 0), (n // 8, 1)), pltpu.roll(x, n - s, 0))


def _norm_matmul(x, g, wt, tm, tn, name):
    t, k = x.shape
    n = wt.shape[0]

    def body(x_ref, g_ref, w_ref, o_ref, h_ref):
        @pl.when(pl.program_id(1) == 0)
        def _():
            xv = x_ref[...]
            r = lax.rsqrt(jnp.mean(xv * xv, axis=-1, keepdims=True) + EPS)
            h_ref[...] = (xv * r * g_ref[...]).astype(BF16)
        o_ref[...] = _dot(h_ref[...], w_ref[...], NT)

    return pl.pallas_call(
        body, name=name, grid=(t // tm, n // tn),
        in_specs=[pl.BlockSpec((tm, k), lambda i, j: (i, 0)), pl.BlockSpec((1, k), lambda i, j: (0, 0)),
                  pl.BlockSpec((tn, k), lambda i, j: (j, 0))],
        out_specs=[pl.BlockSpec((tm, tn), lambda i, j: (i, j)), pl.BlockSpec((tm, k), lambda i, j: (i, 0))],
        out_shape=[SDS((t, n), F32), SDS((t, k), BF16)],
        compiler_params=_params("parallel", "arbitrary"))(x, g, wt)


def _a_spec(a, lead, tm):
    if lead is None:
        return pl.BlockSpec((tm, a.shape[-1]), lambda i: (i, 0))
    return pl.BlockSpec((None, tm, a.shape[-1]), lambda i, _l=lead: (_l, i, 0))


def _mm_resid(pairs, res, dims, tm, out_dtype, name):
    t = pairs[0][0].shape[-2]
    n = pairs[0][2].shape[1] if dims is None else pairs[0][2].shape[0]
    np_ = len(pairs)

    def body(*refs):
        o_ref = refs[-1]
        acc = refs[2 * np_][...] if res is not None else None
        for q in range(np_):
            d = _dot(refs[q][...].astype(BF16), refs[np_ + q][...], dims)
            acc = d if acc is None else acc + d
        o_ref[...] = acc.astype(out_dtype)

    in_specs = [_a_spec(a, lead, tm) for a, lead, _ in pairs]
    in_specs += [pl.BlockSpec(b.shape, lambda i: (0, 0)) for _, _, b in pairs]
    args = [a for a, _, _ in pairs] + [b for _, _, b in pairs]
    if res is not None:
        in_specs.append(pl.BlockSpec((tm, n), lambda i: (i, 0)))
        args.append(res)
    return pl.pallas_call(
        body, name=name, grid=(t // tm,), in_specs=in_specs,
        out_specs=pl.BlockSpec((tm, n), lambda i: (i, 0)), out_shape=SDS((t, n), out_dtype),
        compiler_params=_params("parallel"))(*args)


def _mm_normbwd(pairs, x, g, dres, tm, name):
    t, k = x.shape
    np_ = len(pairs)

    def body(*refs):
        x_ref, g_ref, dres_ref, dx_ref, dg_ref = refs[2 * np_:]
        dh = None
        for q in range(np_):
            d = _dot(refs[q][...], refs[np_ + q][...])
            dh = d if dh is None else dh + d
        xv = x_ref[...]
        r = lax.rsqrt(jnp.mean(xv * xv, axis=-1, keepdims=True) + EPS)
        xh = xv * r

        @pl.when(pl.program_id(0) == 0)
        def _():
            dg_ref[...] = jnp.zeros_like(dg_ref)
        dg_ref[...] += jnp.sum(dh * xh, axis=0, keepdims=True)
        gd = dh * g_ref[...]
        dx_ref[...] = dres_ref[...] + r * (gd - xh * jnp.mean(gd * xh, axis=-1, keepdims=True))

    in_specs = [_a_spec(a, lead, tm) for a, lead, _ in pairs]
    in_specs += [pl.BlockSpec(b.shape, lambda i: (0, 0)) for _, _, b in pairs]
    in_specs += [pl.BlockSpec((tm, k), lambda i: (i, 0)), pl.BlockSpec((1, k), lambda i: (0, 0)),
                 pl.BlockSpec((tm, k), lambda i: (i, 0))]
    args = [a for a, _, _ in pairs] + [b for _, _, b in pairs] + [x, g, dres]
    return pl.pallas_call(
        body, name=name, grid=(t // tm,), in_specs=in_specs,
        out_specs=[pl.BlockSpec((tm, k), lambda i: (i, 0)), pl.BlockSpec((1, k), lambda i: (0, 0))],
        out_shape=[SDS((t, k), F32), SDS((1, k), F32)],
        compiler_params=_params("arbitrary"))(*args)


def _wgrad(a, a_lead, b, name, tk=512):
    t, m = a.shape[-2:]
    n = b.shape[1]
    tm = m if m <= 1024 else 1408
    assert m % tm == 0

    def body(a_ref, b_ref, o_ref):
        @pl.when(pl.program_id(1) == 0)
        def _():
            o_ref[...] = jnp.zeros_like(o_ref)
        o_ref[...] += _dot(a_ref[...].astype(BF16), b_ref[...].astype(BF16), TN)

    if a_lead is None:
        a_spec = pl.BlockSpec((tk, tm), lambda mi, ki: (ki, mi))
    else:
        a_spec = pl.BlockSpec((None, tk, tm), lambda mi, ki, _l=a_lead: (_l, ki, mi))
    return pl.pallas_call(
        body, name=name, grid=(m // tm, t // tk),
        in_specs=[a_spec, pl.BlockSpec((tk, n), lambda mi, ki: (ki, 0))],
        out_specs=pl.BlockSpec((tm, n), lambda mi, ki: (mi, 0)), out_shape=SDS((m, n), F32),
        compiler_params=_params("parallel", "arbitrary"))(a, b)


def _head_consts():
    iq = np.arange(ATTN_DIM)
    ik = np.arange(KV_DIM)
    ones_q = (iq[:, None] // HEAD_DIM == iq[None, :] // HEAD_DIM).astype(np.float32)
    ones_k = (ik[:, None] // HEAD_DIM == ik[None, :] // HEAD_DIM).astype(np.float32)
    dup = (ik[:, None] == (HEAD_DIM * (iq[None, :] // 128) + iq[None, :] % HEAD_DIM)).astype(np.float32)
    return jnp.asarray(ones_q, BF16), jnp.asarray(ones_k, BF16), jnp.asarray(dup, BF16), jnp.asarray(dup.T, BF16)


def _attn_prep(proj, gq, gk, ones_q, ones_k, dup, tm=512):
    t = proj.shape[0]

    def body(p_ref, gq_ref, gk_ref, oq_ref, ok_ref, dup_ref, qn_ref, kd_ref, vd_ref):
        q = p_ref[:, 0:ATTN_DIM]
        k = p_ref[:, ATTN_DIM:ATTN_DIM + KV_DIM]
        v = p_ref[:, ATTN_DIM + KV_DIM:]
        rq = lax.rsqrt(_hdot(q * q, oq_ref[...]) * (1.0 / HEAD_DIM) + EPS)
        qn_ref[...] = (q * rq * gq_ref[...]) * (HEAD_DIM ** -0.5)
        rk = lax.rsqrt(_hdot(k * k, ok_ref[...]) * (1.0 / HEAD_DIM) + EPS)
        kn = k * rk * gk_ref[...]
        kd_ref[...] = _dot(kn.astype(BF16), dup_ref[...])
        vd_ref[...] = _dot(v.astype(BF16), dup_ref[...])

    full = lambda a: pl.BlockSpec(a.shape, lambda i: (0, 0))
    o_spec = pl.BlockSpec((tm, ATTN_DIM), lambda i: (i, 0))
    return pl.pallas_call(
        body, name="attn_prep", grid=(t // tm,),
        in_specs=[pl.BlockSpec((tm, 1024), lambda i: (i, 0)), full(gq), full(gk), full(ones_q), full(ones_k), full(dup)],
        out_specs=[o_spec, o_spec, o_spec], out_shape=[SDS((t, ATTN_DIM), F32)] * 3,
        compiler_params=_params("parallel"))(proj, gq, gk, ones_q, ones_k, dup)


def _attn_prep_bwd(proj, dqn, dkc, dkp, dvc, dvp, gq, gk, ones_q, ones_k, dup_t, tm=512):
    t = proj.shape[0]
    nblk = t // tm
    off = SUPER // tm

    def body(p_ref, dqn_ref, dkc_ref, dkp_ref, dvc_ref, dvp_ref, gq_ref, gk_ref, oq_ref, ok_ref, dt_ref,
             o_ref, dgq_ref, dgk_ref):
        i = pl.program_id(0)
        has_next = (i + off < nblk).astype(F32)
        q = p_ref[:, 0:ATTN_DIM]
        k = p_ref[:, ATTN_DIM:ATTN_DIM + KV_DIM]
        dkn = _hdot(dkc_ref[...] + has_next * dkp_ref[...], dt_ref[...])
        dv = _hdot(dvc_ref[...] + has_next * dvp_ref[...], dt_ref[...])

        @pl.when(i == 0)
        def _():
            dgq_ref[...] = jnp.zeros_like(dgq_ref)
            dgk_ref[...] = jnp.zeros_like(dgk_ref)

        rq = lax.rsqrt(_hdot(q * q, oq_ref[...]) * (1.0 / HEAD_DIM) + EPS)
        xh = q * rq
        dy = dqn_ref[...] * (HEAD_DIM ** -0.5)
        dgq_ref[...] += jnp.sum(dy * xh, axis=0, keepdims=True)
        gd = dy * gq_ref[...]
        dq = rq * (gd - xh * (_hdot(gd * xh, oq_ref[...]) * (1.0 / HEAD_DIM)))
        rk = lax.rsqrt(_hdot(k * k, ok_ref[...]) * (1.0 / HEAD_DIM) + EPS)
        kh = k * rk
        dgk_ref[...] += jnp.sum(dkn * kh, axis=0, keepdims=True)
        gdk = dkn * gk_ref[...]
        dk = rk * (gdk - kh * (_hdot(gdk * kh, ok_ref[...]) * (1.0 / HEAD_DIM)))
        o_ref[:, 0:ATTN_DIM] = dq.astype(BF16)
        o_ref[:, ATTN_DIM:ATTN_DIM + KV_DIM] = dk.astype(BF16)
        o_ref[:, ATTN_DIM + KV_DIM:] = dv.astype(BF16)

    full = lambda a: pl.BlockSpec(a.shape, lambda i: (0, 0))
    cur = pl.BlockSpec((tm, ATTN_DIM), lambda i: (i, 0))
    nxt = pl.BlockSpec((tm, ATTN_DIM), lambda i: (jnp.minimum(i + off, nblk - 1), 0))
    return pl.pallas_call(
        body, name="attn_prep_bwd", grid=(nblk,),
        in_specs=[pl.BlockSpec((tm, 1024), lambda i: (i, 0)), cur, cur, nxt, cur, nxt,
                  full(gq), full(gk), full(ones_q), full(ones_k), full(dup_t)],
        out_specs=[pl.BlockSpec((tm, 1024), lambda i: (i, 0)), pl.BlockSpec((1, ATTN_DIM), lambda i: (0, 0)),
                   pl.BlockSpec((1, KV_DIM), lambda i: (0, 0))],
        out_shape=[SDS((t, 1024), BF16), SDS((1, ATTN_DIM), F32), SDS((1, KV_DIM), F32)],
        compiler_params=_params("arbitrary"))(proj, dqn, dkc, dkp, dvc, dvp, gq, gk, ones_q, ones_k, dup_t)


def _tile_masks():
    qi = lax.broadcasted_iota(jnp.int32, (2 * CHUNK, 2 * CHUNK), 0) & (CHUNK - 1)
    kj = lax.broadcasted_iota(jnp.int32, (2 * CHUNK, 2 * CHUNK), 1)
    delta = CHUNK + qi - kj
    band = (delta >= 0) & (delta <= CHUNK)
    return band, kj


def _deinterleave(dst, src, n_rows, d):
    per = n_rows // d
    for r in range(d):
        dst[r * per:(r + 1) * per, :] = src[pl.ds(r, per, stride=d), :]


def _attn_specs(t):
    blk = lambda f: pl.BlockSpec((SUPER, 128), f)
    cur = blk(lambda h, s: (s, h))
    prev = blk(lambda h, s: (jnp.maximum(s - 1, 0), h))
    return cur, prev


def _attn_fwd(qn, kd, vd):
    t = qn.shape[0]
    cur, prev = _attn_specs(t)

    def body(q_ref, kp_ref, kc_ref, vp_ref, vc_ref, o_ref, lse_ref, kk, vv, qd, kdd, vdd, po, pm, pll, acc, mm, ll):
        s = pl.program_id(1)
        kk[0:SUPER, :] = kp_ref[...]
        kk[SUPER:, :] = kc_ref[...]
        vv[0:SUPER, :] = vp_ref[...]
        vv[SUPER:, :] = vc_ref[...]
        m0 = lax.broadcasted_iota(jnp.int32, (CHUNK, 128), 1) < HEAD_DIM
        band, kj = _tile_masks()
        for d in DILATIONS:
            lq = SUPER // d
            if d == 1:
                qs_ref, ks_ref, vs_ref = q_ref, kk, vv
            else:
                _deinterleave(qd, q_ref, SUPER, d)
                _deinterleave(kdd, kk, 2 * SUPER, d)
                _deinterleave(vdd, vv, 2 * SUPER, d)
                qs_ref, ks_ref, vs_ref = qd, kdd, vdd

            def tile(ti, carry):
                r = ti // (lq // CHUNK)
                nb = ti % (lq // CHUNK)
                qoff = pl.multiple_of(ti * CHUNK, CHUNK)
                koff = pl.multiple_of(r * 2 * lq + lq + (nb - 1) * CHUNK, CHUNK)
                qt = qs_ref[pl.ds(qoff, CHUNK), :]
                qs = jnp.concatenate([jnp.where(m0, qt, 0.0), jnp.where(m0, 0.0, qt)], axis=0).astype(BF16)
                kt = ks_ref[pl.ds(koff, 2 * CHUNK), :].astype(BF16)
                vt = vs_ref[pl.ds(koff, 2 * CHUNK), :].astype(BF16)
                sc = _dot(qs, kt, NT)
                ok = band & (kj >= jnp.where((s > 0) | (nb > 0), 0, CHUNK))
                sc = jnp.where(ok, sc, -jnp.inf)
                mt = jnp.max(sc, axis=-1, keepdims=True)
                p = jnp.exp(sc - mt)
                lt = jnp.sum(p, axis=-1, keepdims=True)
                ot = _dot(p.astype(BF16), vt)
                po[pl.ds(qoff, CHUNK), :] = jnp.where(m0, ot[:CHUNK], ot[CHUNK:])
                pm[pl.ds(qoff, CHUNK), :] = jnp.where(m0, mt[:CHUNK], mt[CHUNK:])
                pll[pl.ds(qoff, CHUNK), :] = jnp.where(m0, lt[:CHUNK], lt[CHUNK:])
                return carry

            lax.fori_loop(0, SUPER // CHUNK, tile, 0, unroll=TILE_UNROLL)
            if d == 1:
                acc[...] = po[...]
                mm[...] = pm[...]
                ll[...] = pll[...]
            else:
                for r in range(d):
                    rows = pl.ds(r, lq, stride=d)
                    seg = slice(r * lq, (r + 1) * lq)
                    m_old, m_new = mm[rows, :], pm[seg, :]
                    m_all = jnp.maximum(m_old, m_new)
                    a, b = jnp.exp(m_old - m_all), jnp.exp(m_new - m_all)
                    acc[rows, :] = acc[rows, :] * a + po[seg, :] * b
                    ll[rows, :] = ll[rows, :] * a + pll[seg, :] * b
                    mm[rows, :] = m_all
        o_ref[...] = acc[...] / ll[...]
        lse_ref[...] = mm[...] + jnp.log(ll[...])

    big = pltpu.VMEM((2 * SUPER, 128), F32)
    one = pltpu.VMEM((SUPER, 128), F32)
    return pl.pallas_call(
        body, name="attn_fwd", grid=(4, t // SUPER),
        in_specs=[cur, prev, cur, prev, cur], out_specs=[cur, cur],
        out_shape=[SDS((t, ATTN_DIM), F32)] * 2,
        scratch_shapes=[big, big, one, big, big, one, one, one, one, one, one],
        compiler_params=_params("parallel", "arbitrary"))(qn, kd, kd, vd, vd)


def _attn_bwd(qn, kd, vd, out, lse, dout):
    t = qn.shape[0]
    cur, prev = _attn_specs(t)

    def body(q_ref, kp_ref, kc_ref, vp_ref, vc_ref, o_ref, lse_ref, do_ref,
             dq_ref, dkc_ref, dkp_ref, dvc_ref, dvp_ref,
             kk, vv, dkk, dvv, qd, od, ld, dod, kdd, vdd, dkd, dvd, pdq):
        s = pl.program_id(1)
        kk[0:SUPER, :] = kp_ref[...]
        kk[SUPER:, :] = kc_ref[...]
        vv[0:SUPER, :] = vp_ref[...]
        vv[SUPER:, :] = vc_ref[...]
        dkk[...] = jnp.zeros_like(dkk)
        dvv[...] = jnp.zeros_like(dvv)
        dq_ref[...] = jnp.zeros_like(dq_ref)
        m0 = lax.broadcasted_iota(jnp.int32, (CHUNK, 128), 1) < HEAD_DIM
        band, kj = _tile_masks()
        ninf = -jnp.inf
        for d in DILATIONS:
            lq = SUPER // d
            if d == 1:
                qs_ref, os_ref, ls_ref, dos_ref, ks_ref, vs_ref, dks_ref, dvs_ref = (
                    q_ref, o_ref, lse_ref, do_ref, kk, vv, dkk, dvv)
            else:
                _deinterleave(qd, q_ref, SUPER, d)
                _deinterleave(od, o_ref, SUPER, d)
                _deinterleave(ld, lse_ref, SUPER, d)
                _deinterleave(dod, do_ref, SUPER, d)
                _deinterleave(kdd, kk, 2 * SUPER, d)
                _deinterleave(vdd, vv, 2 * SUPER, d)
                dkd[...] = jnp.zeros_like(dkd)
                dvd[...] = jnp.zeros_like(dvd)
                qs_ref, os_ref, ls_ref, dos_ref, ks_ref, vs_ref, dks_ref, dvs_ref = (
                    qd, od, ld, dod, kdd, vdd, dkd, dvd)

            def tile(ti, carry):
                r = ti // (lq // CHUNK)
                nb = ti % (lq // CHUNK)
                qoff = pl.multiple_of(ti * CHUNK, CHUNK)
                koff = pl.multiple_of(r * 2 * lq + lq + (nb - 1) * CHUNK, CHUNK)
                qrows = pl.ds(qoff, CHUNK)
                krows = pl.ds(koff, 2 * CHUNK)
                qt, ot, lt, dot_ = qs_ref[qrows, :], os_ref[qrows, :], ls_ref[qrows, :], dos_ref[qrows, :]
                qs = jnp.concatenate([jnp.where(m0, qt, 0.0), jnp.where(m0, 0.0, qt)], axis=0).astype(BF16)
                dos = jnp.concatenate([jnp.where(m0, dot_, 0.0), jnp.where(m0, 0.0, dot_)], axis=0).astype(BF16)
                lse_rows = jnp.concatenate([jnp.max(jnp.where(m0, lt, ninf), axis=-1, keepdims=True),
                                            jnp.max(jnp.where(m0, ninf, lt), axis=-1, keepdims=True)], axis=0)
                prod = dot_ * ot
                dl_rows = jnp.concatenate([jnp.sum(jnp.where(m0, prod, 0.0), axis=-1, keepdims=True),
                                           jnp.sum(jnp.where(m0, 0.0, prod), axis=-1, keepdims=True)], axis=0)
                kt = ks_ref[krows, :].astype(BF16)
                vt = vs_ref[krows, :].astype(BF16)
                sc = _dot(qs, kt, NT)
                ok = band & (kj >= jnp.where((s > 0) | (nb > 0), 0, CHUNK))
                p = jnp.exp(jnp.where(ok, sc, ninf) - lse_rows)
                dp = _dot(dos, vt, NT)
                ds = p * (dp - dl_rows)
                dqs = _dot(ds.astype(BF16), kt)
                pdq[qrows, :] = jnp.where(m0, dqs[:CHUNK], dqs[CHUNK:])
                dks_ref[krows, :] += _dot(ds.astype(BF16), qs, TN)
                dvs_ref[krows, :] += _dot(p.astype(BF16), dos, TN)
                return carry

            lax.fori_loop(0, SUPER // CHUNK, tile, 0, unroll=TILE_UNROLL)
            if d == 1:
                dq_ref[...] += pdq[...]
            else:
                for r in range(d):
                    dq_ref[pl.ds(r, lq, stride=d), :] += pdq[r * lq:(r + 1) * lq, :]
                    dkk[pl.ds(r, 2 * lq, stride=d), :] += dkd[r * 2 * lq:(r + 1) * 2 * lq, :]
                    dvv[pl.ds(r, 2 * lq, stride=d), :] += dvd[r * 2 * lq:(r + 1) * 2 * lq, :]
        dkp_ref[...] = dkk[0:SUPER, :]
        dkc_ref[...] = dkk[SUPER:, :]
        dvp_ref[...] = dvv[0:SUPER, :]
        dvc_ref[...] = dvv[SUPER:, :]

    big = pltpu.VMEM((2 * SUPER, 128), F32)
    one = pltpu.VMEM((SUPER, 128), F32)
    return pl.pallas_call(
        body, name="attn_bwd", grid=(4, t // SUPER),
        in_specs=[cur, prev, cur, prev, cur, cur, cur, cur], out_specs=[cur] * 5,
        out_shape=[SDS((t, ATTN_DIM), F32)] * 5,
        scratch_shapes=[big, big, big, big, one, one, one, one, big, big, big, big, one],
        compiler_params=_params("parallel", "arbitrary"))(qn, kd, kd, vd, vd, out, lse, dout)


def _ssd_consts():
    tri = np.tril(np.ones((CHUNK, CHUNK), np.float32))
    expand = np.zeros((128, SSM_INNER), np.float32)
    for h in range(SSM_HEADS):
        expand[h, h * HEAD_DIM:(h + 1) * HEAD_DIM] = 1.0
    return jnp.asarray(tri, BF16), jnp.asarray(tri.T, BF16), jnp.asarray(expand, BF16), jnp.asarray(expand.T, BF16)


def _conv4(x, halo, w_ref, b_ref):
    acc = b_ref[...] + w_ref[3:4, :] * x
    for k in range(3):
        acc = acc + w_ref[k:k + 1, :] * _shift_down(x, halo, 3 - k)
    return acc


def _softplus(x):
    return jnp.maximum(x, 0.0) + jnp.log(1.0 + jnp.exp(-jnp.abs(x)))


def _ssd_common(xs_ref, bc_ref, dt_ref, hx_ref, hb_ref, cwx_ref, cbx_ref, cwb_ref, cbb_ref, dtb_ref, alog_ref,
                tri_ref, exp_ref, first):
    keep = 1.0 - first.astype(F32)
    hx = hx_ref[...] * keep
    hb = hb_ref[...] * keep
    pre_x = _conv4(xs_ref[...], hx, cwx_ref, cbx_ref)
    pre_b = _conv4(bc_ref[...], hb, cwb_ref, cbb_ref)
    xa = pre_x * _sigmoid(pre_x)
    ba = pre_b * _sigmoid(pre_b)
    dtv = _softplus(dt_ref[...] + dtb_ref[...])
    a_neg = -jnp.exp(alog_ref[...])
    acum = _hdot(tri_ref[...], dtv * a_neg, parts=3)
    lam = jnp.exp(acum)
    gam = jnp.exp(acum[CHUNK - 1:CHUNK, :] - acum)
    dt_e = _hdot(dtv, exp_ref[...])
    lam_e = _hdot(lam, exp_ref[...])
    gam_e = _hdot(gam, exp_ref[...])
    return dict(hx=hx, hb=hb, pre_x=pre_x, pre_b=pre_b, xa=xa, ba=ba, dtv=dtv, a_neg=a_neg, acum=acum,
                dt_e=dt_e, lam_e=lam_e, gam_e=gam_e, xdt=xa * dt_e)


def _decay(acum_t, h, transposed):
    rb = jnp.broadcast_to(acum_t[h:h + 1, :], (CHUNK, CHUNK))
    ri = lax.broadcasted_iota(jnp.int32, (CHUNK, CHUNK), 0)
    ci = lax.broadcasted_iota(jnp.int32, (CHUNK, CHUNK), 1)
    if transposed:
        return jnp.exp(jnp.where(ci >= ri, rb - rb.T, -jnp.inf))
    return jnp.exp(jnp.where(ri >= ci, rb.T - rb, -jnp.inf))


def _ssd_specs(t, rev):
    nc = t // CHUNK
    ch = (lambda c: nc - 1 - c) if rev else (lambda c: c)
    col = lambda w, j: pl.BlockSpec((CHUNK, w), lambda c: (ch(c), j))
    halo = lambda w, j: pl.BlockSpec((8, w), lambda c: (jnp.maximum(ch(c) * (CHUNK // 8) - 1, 0), j))
    return nc, ch, col, halo


def _ssd_fwd(proj, cwx, cbx, cwb, cbb, dtb, alog, dsk_e, norm_g, tri, expand):
    t = proj.shape[0]
    nc, _, col, halo = _ssd_specs(t, False)

    def body(z_ref, xs_ref, bc_ref, dt_ref, hx_ref, hb_ref, cwx_ref, cbx_ref, cwb_ref, cbb_ref, dtb_ref, alog_ref,
             dsk_ref, g_ref, tri_ref, exp_ref, y_ref, hs_ref, o_ref, state):
        c = pl.program_id(0)

        @pl.when(c == 0)
        def _():
            state[...] = jnp.zeros_like(state)

        v = _ssd_common(xs_ref, bc_ref, dt_ref, hx_ref, hb_ref, cwx_ref, cbx_ref, cwb_ref, cbb_ref, dtb_ref,
                        alog_ref, tri_ref, exp_ref, c == 0)
        acum_t = v["acum"].T
        xdt, ba = v["xdt"], v["ba"]
        h_in = state[...]
        hs_ref[0] = h_in
        xg = xdt * v["gam_e"]
        m0 = lax.broadcasted_iota(jnp.int32, (CHUNK, 128), 1) < HEAD_DIM
        for g in range(2):
            bg = ba[:, g * 128:(g + 1) * 128].astype(BF16)
            cg = ba[:, 256 + g * 128:256 + (g + 1) * 128].astype(BF16)
            gl = slice(g * 512, (g + 1) * 512)
            cb = _dot(cg, bg, NT)
            y_off = _dot(cg, h_in[:, gl].astype(BF16)) * v["lam_e"][:, gl]
            s_new = _dot(bg.T, xg[:, gl].astype(BF16))
            state[:, gl] = h_in[:, gl] * v["lam_e"][CHUNK - 1:CHUNK, gl] + s_new
            for j in range(4):
                h0 = 8 * g + 2 * j
                ln = slice(g * 512 + j * 128, g * 512 + (j + 1) * 128)
                xp = xdt[:, ln].astype(BF16)
                y0 = _dot((cb * _decay(acum_t, h0, False)).astype(BF16), xp)
                y1 = _dot((cb * _decay(acum_t, h0 + 1, False)).astype(BF16), xp)
                y_ref[:, ln] = jnp.where(m0, y0, y1) + y_off[:, j * 128:(j + 1) * 128]
        z = z_ref[...]
        yg = (y_ref[...] + dsk_ref[...] * v["xa"]) * (z * _sigmoid(z))
        r = lax.rsqrt(jnp.mean(yg * yg, axis=-1, keepdims=True) + EPS)
        o_ref[...] = (yg * r * g_ref[...]).astype(BF16)

    full = lambda a: pl.BlockSpec(a.shape, lambda c: (0,) * a.ndim)
    return pl.pallas_call(
        body, name="ssd_fwd", grid=(nc,),
        in_specs=[col(1024, 1), col(1024, 2), col(512, 6), col(128, 28), halo(1024, 2), halo(512, 6),
                  full(cwx), full(cbx), full(cwb), full(cbb), full(dtb), full(alog), full(dsk_e), full(norm_g),
                  full(tri), full(expand)],
        out_specs=[pl.BlockSpec((CHUNK, SSM_INNER), lambda c: (c, 0)),
                   pl.BlockSpec((1, 128, SSM_INNER), lambda c: (c, 0, 0)),
                   pl.BlockSpec((CHUNK, SSM_INNER), lambda c: (c, 0))],
        out_shape=[SDS((t, SSM_INNER), F32), SDS((nc, 128, SSM_INNER), F32), SDS((t, SSM_INNER), BF16)],
        scratch_shapes=[pltpu.VMEM((128, SSM_INNER), F32)],
        compiler_params=_params("arbitrary"))(proj, proj, proj, proj, proj, proj, cwx, cbx, cwb, cbb, dtb, alog,
                                              dsk_e, norm_g, tri, expand)


def _ssd_bwd(proj, y_ssd, hs, dout, cwx, cbx, cwb, cbb, dtb, alog, dsk_e, norm_g, tri, triu, expand, expand_t):
    t = proj.shape[0]
    nc, ch, col, halo = _ssd_specs(t, True)

    def body(z_ref, xs_ref, bc_ref, dt_ref, hx_ref, hb_ref, y_ref, hin_ref, do_ref,
             cwx_ref, cbx_ref, cwb_ref, cbb_ref, dtb_ref, alog_ref, dsk_ref, g_ref, tri_ref, triu_ref, exp_ref, expt_ref,
             dz_ref, dxs_ref, dbc_ref, ddt_ref, dg_ref, ddsk_ref, dalog_ref, ddtb_ref, dcwx_ref, dcbx_ref, dcwb_ref,
             dcbb_ref, gstate, nx_x, nx_b, dact_b):
        step = pl.program_id(0)
        c = nc - 1 - step

        @pl.when(step == 0)
        def _():
            gstate[...] = jnp.zeros_like(gstate)
            nx_x[...] = jnp.zeros_like(nx_x)
            nx_b[...] = jnp.zeros_like(nx_b)
            for ref in (dg_ref, ddsk_ref, dalog_ref, ddtb_ref, dcwx_ref, dcbx_ref, dcwb_ref, dcbb_ref):
                ref[...] = jnp.zeros_like(ref)

        v = _ssd_common(xs_ref, bc_ref, dt_ref, hx_ref, hb_ref, cwx_ref, cbx_ref, cwb_ref, cbb_ref, dtb_ref,
                        alog_ref, tri_ref, exp_ref, c == 0)
        acum_t = v["acum"].T
        xa, ba, xdt, dtv = v["xa"], v["ba"], v["xdt"], v["dtv"]
        lam_e, gam_e, dt_e = v["lam_e"], v["gam_e"], v["dt_e"]
        z = z_ref[...]
        y = y_ref[...]
        sz = _sigmoid(z)
        zs = z * sz
        y_tot = y + dsk_ref[...] * xa
        yg = y_tot * zs
        r = lax.rsqrt(jnp.mean(yg * yg, axis=-1, keepdims=True) + EPS)
        yh = yg * r
        do = do_ref[...]
        dg_ref[...] += jnp.sum(do * yh, axis=0, keepdims=True)
        gd = do * g_ref[...]
        dyg = r * (gd - yh * jnp.mean(gd * yh, axis=-1, keepdims=True))
        dz_ref[...] = (dyg * y_tot * (sz * (1.0 + z * (1.0 - sz)))).astype(BF16)
        dy = dyg * zs
        ddsk_ref[...] += jnp.sum(dy * xa, axis=0, keepdims=True)
        g_out = gstate[...]
        h_in = hin_ref[0]
        lam_dy = lam_e * dy
        gam_x = gam_e * xdt
        m0 = lax.broadcasted_iota(jnp.int32, (CHUNK, 128), 1) < HEAD_DIM
        lane = lax.broadcasted_iota(jnp.int32, (CHUNK, 128), 1)
        below = (lax.broadcasted_iota(jnp.int32, (CHUNK, CHUNK), 0) >
                 lax.broadcasted_iota(jnp.int32, (CHUNK, CHUNK), 1))
        da_in = jnp.zeros((CHUNK, 128), F32)
        off_y, off_x = [], []
        for g in range(2):
            bg = ba[:, g * 128:(g + 1) * 128].astype(BF16)
            cg = ba[:, 256 + g * 128:256 + (g + 1) * 128].astype(BF16)
            gl = slice(g * 512, (g + 1) * 512)
            gg = g_out[:, gl].astype(BF16)
            bc_t = _dot(bg, cg, NT)
            cb = _dot(cg, bg, NT)
            dxdt_off = _dot(bg, gg) * gam_e[:, gl]
            off_x.append(xdt[:, gl] * dxdt_off)
            off_y.append(dy[:, gl] * (_dot(cg, h_in[:, gl].astype(BF16)) * lam_e[:, gl]))
            q_sum = jnp.zeros((CHUNK, CHUNK), F32)
            for j in range(4):
                h0 = 8 * g + 2 * j
                ln = slice(g * 512 + j * 128, g * 512 + (j + 1) * 128)
                dyp = dy[:, ln]
                dyb = dyp.astype(BF16)
                xpb = xdt[:, ln].astype(BF16)
                d0 = _dot((bc_t * _decay(acum_t, h0, True)).astype(BF16), dyb)
                d1 = _dot((bc_t * _decay(acum_t, h0 + 1, True)).astype(BF16), dyb)
                dxs_ref[:, ln] = jnp.where(m0, d0, d1) + dxdt_off[:, j * 128:(j + 1) * 128]
                for hh, dym in ((h0, jnp.where(m0, dyp, 0.0)), (h0 + 1, jnp.where(m0, 0.0, dyp))):
                    qd = _dot(dym.astype(BF16), xpb, NT) * _decay(acum_t, hh, False)
                    q_sum = q_sum + qd
                    reach = jnp.where(below, _hdot(triu_ref[...], qd * cb), 0.0)
                    da_in = jnp.where(lane == hh, jnp.sum(reach, axis=-1, keepdims=True), da_in)
            gstate[:, gl] = g_out[:, gl] * lam_e[CHUNK - 1:CHUNK, gl] + _dot(cg.T, lam_dy[:, gl].astype(BF16))
            qb = q_sum.astype(BF16)
            dact_b[:, 256 + g * 128:256 + (g + 1) * 128] = (
                _dot(qb, bg) + _dot(lam_dy[:, gl].astype(BF16), h_in[:, gl].astype(BF16), NT))
            dact_b[:, g * 128:(g + 1) * 128] = _dot(qb.T, cg) + _dot(gam_x[:, gl].astype(BF16), gg, NT)
        dxdt = dxs_ref[...]
        seg_y = _hdot(jnp.concatenate(off_y, axis=1), expt_ref[...])
        seg_x = _hdot(jnp.concatenate(off_x, axis=1), expt_ref[...])
        e_col = jnp.sum(g_out * h_in * lam_e[CHUNK - 1:CHUNK, :], axis=0, keepdims=True)
        e_seg = _hdot(jnp.broadcast_to(e_col, (8, SSM_INNER)), expt_ref[...])[0:1, :]
        da = da_in + _hdot(triu_ref[...], seg_y) + (_hdot(tri_ref[...], seg_x) - seg_x) + e_seg
        a_neg = v["a_neg"]
        ddtv = da * a_neg + _hdot(dxdt * xa, expt_ref[...])
        dalog_ref[...] += jnp.sum(da * dtv, axis=0, keepdims=True) * a_neg
        lane16 = lax.broadcasted_iota(jnp.int32, (CHUNK, 128), 1) < SSM_HEADS
        draw = jnp.where(lane16, ddtv * _sigmoid(dt_ref[...] + dtb_ref[...]), 0.0)
        ddtb_ref[...] += jnp.sum(draw, axis=0, keepdims=True)
        ddt_ref[...] = draw.astype(BF16)
        dxa = dxdt * dt_e + dy * dsk_ref[...]
        for (dact, pre, x_ref, halo_v, nx, cw_ref, dcw_ref, dcb_ref, dx_ref) in (
                (dxa, v["pre_x"], xs_ref, v["hx"], nx_x, cwx_ref, dcwx_ref, dcbx_ref, dxs_ref),
                (dact_b[...], v["pre_b"], bc_ref, v["hb"], nx_b, cwb_ref, dcwb_ref, dcbb_ref, dbc_ref)):
            sp = _sigmoid(pre)
            dpre = dact * (sp * (1.0 + pre * (1.0 - sp)))
            dcb_ref[...] += jnp.sum(dpre, axis=0, keepdims=True)
            xv = x_ref[...]
            nxt = nx[...]
            dx = cw_ref[3:4, :] * dpre
            dcw_ref[3:4, :] += jnp.sum(dpre * xv, axis=0, keepdims=True)
            for k in range(3):
                dcw_ref[k:k + 1, :] += jnp.sum(dpre * _shift_down(xv, halo_v, 3 - k), axis=0, keepdims=True)
                dx = dx + cw_ref[k:k + 1, :] * _shift_up(dpre, nxt, 3 - k)
            nx[...] = dpre[0:8, :]
            dx_ref[...] = dx.astype(dx_ref.dtype)

    full = lambda a: pl.BlockSpec(a.shape, lambda c: (0,) * a.ndim)
    rowblk = lambda w: pl.BlockSpec((CHUNK, w), lambda c: (ch(c), 0))
    acc = lambda a, b: pl.BlockSpec((a, b), lambda c: (0, 0))
    return pl.pallas_call(
        body, name="ssd_bwd", grid=(nc,),
        in_specs=[col(1024, 1), col(1024, 2), col(512, 6), col(128, 28), halo(1024, 2), halo(512, 6),
                  rowblk(SSM_INNER),
                  pl.BlockSpec((1, 128, SSM_INNER), lambda c: (ch(c), 0, 0)),
                  rowblk(SSM_INNER),
                  full(cwx), full(cbx), full(cwb), full(cbb), full(dtb), full(alog), full(dsk_e), full(norm_g),
                  full(tri), full(triu), full(expand), full(expand_t)],
        out_specs=[rowblk(SSM_INNER), rowblk(SSM_INNER), rowblk(512), rowblk(128),
                   acc(1, 1024), acc(1, 1024), acc(1, 128), acc(1, 128), acc(4, 1024), acc(1, 1024), acc(4, 512),
                   acc(1, 512)],
        out_shape=[SDS((t, SSM_INNER), BF16), SDS((t, SSM_INNER), F32), SDS((t, 512), BF16), SDS((t, 128), BF16),
                   SDS((1, 1024), F32), SDS((1, 1024), F32), SDS((1, 128), F32), SDS((1, 128), F32),
                   SDS((4, 1024), F32), SDS((1, 1024), F32), SDS((4, 512), F32), SDS((1, 512), F32)],
        scratch_shapes=[pltpu.VMEM((128, SSM_INNER), F32), pltpu.VMEM((8, 1024), F32), pltpu.VMEM((8, 512), F32),
                        pltpu.VMEM((CHUNK, 512), F32)],
        compiler_params=_params("arbitrary"))(proj, proj, proj, proj, proj, proj, y_ssd, hs, dout,
                                              cwx, cbx, cwb, cbb, dtb, alog, dsk_e, norm_g, tri, triu, expand,
                                              expand_t)


def _conv3(x, halo, w_ref, b_ref, part):
    acc = b_ref[part] + w_ref[2, part] * x
    for k in range(2):
        acc = acc + w_ref[k, part] * _shift_down(x, halo, 2 - k)
    return acc


def _ffn_specs(t, tm, tn, order):
    nj = D_FF // tn
    ij = (lambda a, b: (b, a)) if order == "ji" else (lambda a, b: (a, b))

    def spec(shape, f):
        return pl.BlockSpec(shape, lambda a, b: f(*ij(a, b)))
    u_g = spec((tm, tn), lambda i, j: (i, j))
    u_v = spec((tm, tn), lambda i, j: (i, j + nj))
    h_g = spec((8, tn), lambda i, j: (jnp.maximum(i * (tm // 8) - 1, 0), j))
    h_v = spec((8, tn), lambda i, j: (jnp.maximum(i * (tm // 8) - 1, 0), j + nj))
    w = spec((3, 2, 1, tn), lambda i, j: (0, 0, 0, j))
    b = spec((2, 1, tn), lambda i, j: (0, 0, j))
    return nj, spec, u_g, u_v, h_g, h_v, w, b


def _ffn_act(u, cw, cb, tm=512, tn=1408):
    t = u.shape[0]
    nj, spec, u_g, u_v, h_g, h_v, w, b = _ffn_specs(t, tm, tn, "ij")

    def body(ug_ref, uv_ref, hg_ref, hv_ref, w_ref, b_ref, f_ref):
        keep = (pl.program_id(0) > 0).astype(F32)
        gate = _conv3(ug_ref[...], hg_ref[...] * keep, w_ref, b_ref, 0)
        val = _conv3(uv_ref[...], hv_ref[...] * keep, w_ref, b_ref, 1)
        f_ref[...] = (gate * _sigmoid(gate) * val).astype(BF16)

    return pl.pallas_call(
        body, name="ffn_act", grid=(t // tm, nj), in_specs=[u_g, u_v, h_g, h_v, w, b],
        out_specs=spec((tm, tn), lambda i, j: (i, j)), out_shape=SDS((t, D_FF), BF16),
        compiler_params=_params("parallel", "parallel"))(u, u, u, u, cw, cb)


def _ffn_act_bwd(dx2, w_down, u, cw, cb, tm=512, tn=1408):
    t = u.shape[0]
    nj, spec, u_g, u_v, h_g, h_v, w, b = _ffn_specs(t, tm, tn, "ji")

    def body(dx_ref, wd_ref, ug_ref, uv_ref, hg_ref, hv_ref, w_ref, b_ref, duc_ref, dcw_ref, dcb_ref):
        i = pl.program_id(1)
        first = i == 0
        df = _dot(dx_ref[...].astype(BF16), wd_ref[...], NT)
        ug, uv = ug_ref[...], uv_ref[...]
        keep = 1.0 - first.astype(F32)
        hg, hv = hg_ref[...] * keep, hv_ref[...] * keep
        gate = _conv3(ug, hg, w_ref, b_ref, 0)
        val = _conv3(uv, hv, w_ref, b_ref, 1)
        sg = _sigmoid(gate)
        dgate = df * val * (sg * (1.0 + gate * (1.0 - sg)))
        dval = df * (gate * sg)
        duc_ref[0] = dgate
        duc_ref[1] = dval

        @pl.when(first)
        def _():
            dcw_ref[...] = jnp.zeros_like(dcw_ref)
            dcb_ref[...] = jnp.zeros_like(dcb_ref)

        for part, (du, uu, hh) in enumerate(((dgate, ug, hg), (dval, uv, hv))):
            dcb_ref[part] += jnp.sum(du, axis=0, keepdims=True)
            for k in range(3):
                dcw_ref[k, part] += jnp.sum(du * _shift_down(uu, hh, 2 - k), axis=0, keepdims=True)

    return pl.pallas_call(
        body, name="ffn_act_bwd", grid=(nj, t // tm),
        in_specs=[spec((tm, D_MODEL), lambda i, j: (i, 0)), spec((tn, D_MODEL), lambda i, j: (j, 0)),
                  u_g, u_v, h_g, h_v, w, b],
        out_specs=[pl.BlockSpec((2, tm, tn), lambda j, i: (0, i, j)), w, b],
        out_shape=[SDS((2, t, D_FF), F32), SDS((3, 2, 1, D_FF), F32), SDS((2, 1, D_FF), F32)],
        compiler_params=_params("parallel", "arbitrary"))(dx2, w_down, u, u, u, u, cw, cb)


def _ffn_conv_t(duc, cw, tm=512, tn=1408):
    t = duc.shape[1]
    nj = D_FF // tn
    nrow8 = t // 8

    def body(d_ref, n_ref, w_ref, o_ref):
        keep = (pl.program_id(0) < pl.num_programs(0) - 1).astype(F32)
        for part in range(2):
            dv = d_ref[part]
            nxt = n_ref[part] * keep
            acc = w_ref[2, part] * dv
            for k in range(2):
                acc = acc + w_ref[k, part] * _shift_up(dv, nxt, 2 - k)
            o_ref[part] = acc.astype(BF16)

    return pl.pallas_call(
        body, name="ffn_conv_t", grid=(t // tm, nj),
        in_specs=[pl.BlockSpec((2, tm, tn), lambda i, j: (0, i, j)),
                  pl.BlockSpec((2, 8, tn), lambda i, j: (0, jnp.minimum((i + 1) * (tm // 8), nrow8 - 1), j)),
                  pl.BlockSpec((3, 2, 1, tn), lambda i, j: (0, 0, 0, j))],
        out_specs=pl.BlockSpec((2, tm, tn), lambda i, j: (0, i, j)), out_shape=SDS((2, t, D_FF), BF16),
        compiler_params=_params("parallel", "parallel"))(duc, duc, cw)


def _ple_loss(x2, g, w_gate, p, w_proj_t, target, tm=256):
    t = x2.shape[0]

    def body(x_ref, g_ref, wg_ref, p_ref, wp_ref, tg_ref, dx_ref, dpre_ref, dpp_ref, h_ref, loss_ref, dg_ref):
        i = pl.program_id(0)
        xv = x_ref[...]
        r = lax.rsqrt(jnp.mean(xv * xv, axis=-1, keepdims=True) + EPS)
        xh = xv * r
        h = (xh * g_ref[...]).astype(BF16)
        h_ref[...] = h
        gate = _sigmoid(_dot(h, wg_ref[...]))
        pp = _dot(p_ref[...].astype(BF16), wp_ref[...], NT)
        err = (xv + gate * pp) - tg_ref[...]

        @pl.when(i == 0)
        def _():
            loss_ref[...] = jnp.zeros_like(loss_ref)
            dg_ref[...] = jnp.zeros_like(dg_ref)

        loss_ref[...] += 0.5 * jnp.sum(jnp.mean(err * err, axis=-1, keepdims=True), axis=0, keepdims=True)
        dy = err * (1.0 / D_MODEL)
        dpre = (dy * pp * gate * (1.0 - gate)).astype(BF16)
        dpre_ref[...] = dpre
        dpp_ref[...] = (dy * gate).astype(BF16)
        dh = _dot(dpre, wg_ref[...], NT)
        dg_ref[...] += jnp.sum(dh * xh, axis=0, keepdims=True)
        gd = dh * g_ref[...]
        dx_ref[...] = dy + r * (gd - xh * jnp.mean(gd * xh, axis=-1, keepdims=True))

    row = lambda w: pl.BlockSpec((tm, w), lambda i: (i, 0))
    full = lambda a: pl.BlockSpec(a.shape, lambda i: (0, 0))
    return pl.pallas_call(
        body, name="ple_loss", grid=(t // tm,),
        in_specs=[row(D_MODEL), full(g), full(w_gate), row(PLE_DIM), full(w_proj_t), row(D_MODEL)],
        out_specs=[row(D_MODEL), row(D_MODEL), row(D_MODEL), row(D_MODEL),
                   pl.BlockSpec((1, 128), lambda i: (0, 0)), pl.BlockSpec((1, D_MODEL), lambda i: (0, 0))],
        out_shape=[SDS((t, D_MODEL), F32), SDS((t, D_MODEL), BF16), SDS((t, D_MODEL), BF16), SDS((t, D_MODEL), BF16),
                   SDS((1, 128), F32), SDS((1, D_MODEL), F32)],
        compiler_params=_params("arbitrary"))(x2, g, w_gate, p, w_proj_t, target)


def _exchange(scatter, gather, name):
    arrays = list(scatter) + list(gather)
    n_a, n_s = len(arrays), len(scatter)

    def body(*refs):
        src, dst = refs[:n_a], refs[n_a:2 * n_a]
        send_sems, recv_sems, local_sems = refs[2 * n_a:]
        x, y, c = lax.axis_index("x"), lax.axis_index("y"), lax.axis_index("c")
        me = 4 * x + 2 * y + c

        def src_of(a, slot):
            return src[a].at[slot] if a < n_s else src[a]

        local = [pltpu.make_async_copy(src_of(a, me), dst[a].at[me], local_sems.at[a]) for a in range(n_a)]
        for cp in local:
            cp.start()
        sends, peers = [], []
        for k in range(1, N_DEV):
            px = 1 - x if k & 4 else x
            py = 1 - y if k & 2 else y
            pc = 1 - c if k & 1 else c
            peer = 4 * px + 2 * py + pc
            peers.append(peer)
            for a in range(n_a):
                cp = pltpu.make_async_remote_copy(
                    src_ref=src_of(a, peer), dst_ref=dst[a].at[me], send_sem=send_sems.at[a, k - 1],
                    recv_sem=recv_sems.at[a, k - 1], device_id=(px, py, pc), device_id_type=pl.DeviceIdType.MESH)
                cp.start()
                sends.append(cp)
        for k in range(1, N_DEV):
            peer = peers[k - 1]
            for a in range(n_a):
                pltpu.make_async_remote_copy(
                    src_ref=src_of(a, peer), dst_ref=dst[a].at[peer], send_sem=send_sems.at[a, k - 1],
                    recv_sem=recv_sems.at[a, k - 1], device_id=(x, y, c),
                    device_id_type=pl.DeviceIdType.MESH).wait_recv()
        for cp in sends:
            cp.wait_send()
        for cp in local:
            cp.wait()

    out_shape = [SDS(a.shape, a.dtype) for a in scatter] + [SDS((N_DEV,) + a.shape, a.dtype) for a in gather]
    hbm = pl.BlockSpec(memory_space=pl.ANY)
    return pl.pallas_call(
        body, name=name, in_specs=[hbm] * n_a, out_specs=[hbm] * n_a, out_shape=out_shape,
        scratch_shapes=[pltpu.SemaphoreType.DMA((n_a, N_DEV - 1)), pltpu.SemaphoreType.DMA((n_a, N_DEV - 1)),
                        pltpu.SemaphoreType.DMA((n_a,))],
        )(*arrays)


def _peer(k):
    x, y, c = lax.axis_index("x"), lax.axis_index("y"), lax.axis_index("c")
    px = 1 - x if k & 4 else x
    py = 1 - y if k & 2 else y
    pc = 1 - c if k & 1 else c
    return (px, py, pc), 4 * px + 2 * py + pc


_HBM = pl.BlockSpec(memory_space=pltpu.HBM)
_SEM = pl.BlockSpec(memory_space=pltpu.SEMAPHORE)


def _split_copies(src, land, send_sems, recv_sems, scatter, arrivals):
    _, me = _peer(0)
    out = []
    for k in range(1, N_DEV):
        coords, peer = _peer(k)
        for a in range(len(src)):
            sem = a * (N_DEV - 1) + k - 1
            if scatter[a]:
                s, d = src[a].at[peer], land[a].at[k]
            else:
                s, d = src[a], land[a].at[peer if arrivals else me]
            out.append(pltpu.make_async_remote_copy(
                src_ref=s, dst_ref=d, send_sem=send_sems.at[sem], recv_sem=recv_sems.at[sem], device_id=coords,
                device_id_type=pl.DeviceIdType.MESH))
    return out


def _exchange_start(srcs, lands, scatter, name):
    n = len(srcs)

    def body(*refs):
        src, land = refs[:n], refs[n:2 * n]
        send_sems, recv_sems = refs[2 * n], refs[2 * n + 1]
        token = refs[-1]
        for cp in _split_copies(src, land, send_sems, recv_sems, scatter, False):
            cp.start()
        token[...] = jnp.zeros_like(token)

    hbm_shape = lambda a: pltpu.HBM(a.shape, a.dtype)
    sem_shape = pltpu.SemaphoreType.DMA((n * (N_DEV - 1),))
    outs = pl.pallas_call(
        body, name=name,
        out_shape=(sem_shape, sem_shape, *[hbm_shape(a) for a in srcs], *[hbm_shape(a) for a in lands],
                   SDS((8, 128), F32)),
        in_specs=[_HBM] * (2 * n), out_specs=(_SEM, _SEM, *[_HBM] * (2 * n), pl.BlockSpec(memory_space=pltpu.VMEM)),
        input_output_aliases={a: 2 + a for a in range(2 * n)},
        compiler_params=pltpu.CompilerParams(has_side_effects=pltpu.SideEffectType.DATAFLOW_SIDE_EFFECTING),
    )(*[pltpu.with_memory_space_constraint(a, pltpu.HBM) for a in list(srcs) + list(lands)])
    return outs[0], outs[1], outs[2:2 + n], outs[2 + n:2 + 2 * n], outs[-1]


def _exchange_wait(send_sems, recv_sems, srcs, lands, scatter, after, name):
    n = len(srcs)

    def body(*refs):
        src, land = refs[:n], refs[n:2 * n]
        for cp in _split_copies(src, land, refs[2 * n], refs[2 * n + 1], scatter, False):
            cp.wait_send()
        for cp in _split_copies(src, land, refs[2 * n], refs[2 * n + 1], scatter, True):
            cp.wait_recv()

    hbm_shape = lambda a: pltpu.HBM(a.shape, a.dtype)
    outs = pl.pallas_call(
        body, name=name, out_shape=tuple(hbm_shape(a) for a in list(srcs) + list(lands)),
        in_specs=[_HBM] * (2 * n) + [_SEM, _SEM, pl.BlockSpec(memory_space=pl.ANY)], out_specs=(_HBM,) * (2 * n),
        input_output_aliases={a: a for a in range(2 * n)},
        compiler_params=pltpu.CompilerParams(has_side_effects=pltpu.SideEffectType.DATAFLOW_SIDE_EFFECTING),
    )(*srcs, *lands, send_sems, recv_sems, after)
    return outs[:n], outs[n:]


def _reduce8(a, tr, name):
    _, rows, cols = a.shape

    def body(a_ref, o_ref):
        acc = a_ref[0]
        for j in range(1, N_DEV):
            acc = acc + a_ref[j]
        o_ref[...] = acc

    return pl.pallas_call(
        body, name=name, grid=(rows // tr,),
        in_specs=[pl.BlockSpec((N_DEV, tr, cols), lambda i: (0, i, 0))],
        out_specs=pl.BlockSpec((tr, cols), lambda i: (i, 0)), out_shape=SDS((rows, cols), F32),
        compiler_params=_params("parallel"))(a)


def _reduce_landed(own, land, name, tc=256):
    rows, cols = own.shape

    def body(own_ref, land_ref, o_ref):
        acc = own_ref[...]
        for k in range(1, N_DEV):
            acc = acc + land_ref[k]
        o_ref[...] = acc

    return pl.pallas_call(
        body, name=name, grid=(cols // tc,),
        in_specs=[pl.BlockSpec((rows, tc), lambda j: (0, j)), pl.BlockSpec((N_DEV, rows, tc), lambda j: (0, 0, j))],
        out_specs=pl.BlockSpec((rows, tc), lambda j: (0, j)), out_shape=SDS((rows, cols), F32),
        compiler_params=_params("parallel"))(own, land)


def _adamw(w, g, m, v, name, tr=None):
    rows, cols = w.shape
    tr = rows if tr is None else tr
    c1 = 1.0 - ADAM_B1 ** ADAM_STEP
    c2 = 1.0 - ADAM_B2 ** ADAM_STEP

    def body(w_ref, g_ref, m_ref, v_ref, d_ref, mo_ref, vo_ref):
        gv = g_ref[...]
        mn = ADAM_B1 * m_ref[...] + (1.0 - ADAM_B1) * gv
        vn = ADAM_B2 * v_ref[...] + (1.0 - ADAM_B2) * (gv * gv)
        mo_ref[...] = mn
        vo_ref[...] = vn
        d_ref[...] = -ADAM_LR * ((mn / c1) / (jnp.sqrt(vn / c2) + ADAM_EPS) + ADAM_WD * w_ref[...])

    blk = pl.BlockSpec((tr, cols), lambda i: (i, 0))
    return pl.pallas_call(
        body, name=name, grid=(rows // tr,), in_specs=[blk] * 4, out_specs=[blk] * 3,
        out_shape=[SDS((rows, cols), F32)] * 3, compiler_params=_params("parallel"))(w, g, m, v)


def _pad_rows(a, rows):
    return jnp.pad(a, ((0, rows - a.shape[0]),) + ((0, 0),) * (a.ndim - 1))


def _local_step(x, p, target, sm, wts, fetch_rest, send, tok):
    ones_q, ones_k, dup, dup_t = _head_consts()
    tri, triu, expand, expand_t = _ssd_consts()
    w_in_t = wts["in_t"]
    cwx, cwb = wts["ssm_cw"][:, :SSM_INNER], wts["ssm_cw"][:, SSM_INNER:]
    cbx, cbb = sm["ssm_conv_b"][:, :SSM_INNER], sm["ssm_conv_b"][:, SSM_INNER:]
    pad128 = lambda a: jnp.pad(a, ((0, 0), (0, 128 - a.shape[1])))
    dtb, alog = pad128(sm["dt_bias"]), pad128(sm["a_log"])
    dsk_e = jnp.repeat(sm["d_skip"], HEAD_DIM, axis=1)
    gq = jnp.tile(sm["q_norm_g"], (1, ATTN_DIM // HEAD_DIM))
    gk = jnp.tile(sm["k_norm_g"], (1, KV_DIM // HEAD_DIM))
    ffn_cw = wts["ffn_cw"].reshape(3, 2, 1, D_FF)
    ffn_cb = sm["ffn_conv_b"].reshape(2, 1, D_FF)

    proj, h1 = _norm_matmul(x, sm["attn_norm_g"] + tok, w_in_t, 512, 768, "in_proj")
    qn, kd, vd = _attn_prep(proj, gq, gk, ones_q, ones_k, dup)
    attn_out, lse = _attn_fwd(qn, kd, vd)
    y_ssd, hs, ssm_out = _ssd_fwd(proj, cwx, cbx, cwb, cbb, dtb, alog, dsk_e, sm["ssm_norm_g"], tri, expand)
    rest = fetch_rest(ssm_out)
    w_out, w_up_t, w_down, w_gate, w_proj_t = (rest[k] for k in ("out", "up_t", "down", "gate", "proj_t"))
    x1 = _mm_resid([(attn_out, None, w_out[:ATTN_DIM]), (ssm_out, None, w_out[ATTN_DIM:])], x, None, 512, F32,
                   "out_proj")
    u, h2 = _norm_matmul(x1, sm["ffn_norm_g"], w_up_t, 512, 1408, "up_proj")
    f = _ffn_act(u, ffn_cw, ffn_cb)
    x2 = _mm_resid([(f, None, w_down)], x1, None, 512, F32, "down_proj")
    dx2, dpre, dpp, h3, loss, dg_ple = _ple_loss(x2, sm["ple_norm_g"], w_gate, p, w_proj_t, target)

    g_gate = _wgrad(h3, None, dpre, "wg_gate")
    g_proj_t = _wgrad(dpp, None, p, "wg_proj")
    g_down = _wgrad(f, None, dx2, "wg_down")
    duc, d_ffn_cw, d_ffn_cb = _ffn_act_bwd(dx2, w_down, u, ffn_cw, ffn_cb)
    du = _ffn_conv_t(duc, ffn_cw)
    dx1, dg_ffn = _mm_normbwd([(du, 0, w_up_t[:D_FF]), (du, 1, w_up_t[D_FF:])], x1, sm["ffn_norm_g"], dx2, 256,
                              "up_proj_bwd")
    g_up_t = jnp.concatenate([_wgrad(du, 0, h2, "wg_up_gate"), _wgrad(du, 1, h2, "wg_up_val")], axis=0)
    tok = send(dict(gate=g_gate, proj_t=g_proj_t, down=g_down, up_t=g_up_t)).astype(BF16)
    d_attn = _mm_resid([(dx1, None, w_out[:ATTN_DIM] + tok)], None, NT, 512, F32, "out_proj_bwd_attn")
    d_ssm = _mm_resid([(dx1, None, w_out[ATTN_DIM:] + tok)], None, NT, 512, F32, "out_proj_bwd_ssm")
    g_out = jnp.concatenate([_wgrad(attn_out, None, dx1, "wg_out_attn"), _wgrad(ssm_out, None, dx1, "wg_out_ssm")],
                            axis=0)
    tok = send(dict(out=g_out))
    (dz, dxs, dbc, ddt, dg_ssm, d_dsk_e, d_alog, d_dtb, d_cwx, d_cbx, d_cwb, d_cbb) = _ssd_bwd(
        proj, y_ssd, hs, d_ssm, cwx, cbx, cwb, cbb, dtb + tok, alog, dsk_e, sm["ssm_norm_g"], tri, triu, expand,
        expand_t)
    dxs = dxs.astype(BF16)
    dqn, dkc, dkp, dvc, dvp = _attn_bwd(qn, kd, vd, attn_out, lse, d_attn)
    dqkv, dgq, dgk = _attn_prep_bwd(proj, dqn, dkc, dkp, dvc, dvp, gq + tok, gk, ones_q, ones_k, dup_t)
    pieces = [(dqkv, 0, 1024), (dz, 1024, 2048), (dxs, 2048, 3072), (dbc, 3072, 3584), (ddt, 3584, 3712)]
    g_in_t = jnp.concatenate([_wgrad(a, None, h1, "wg_in_%d" % lo) for a, lo, _ in pieces], axis=0)[:IN_PROJ]
    tok = send(dict(in_t=g_in_t))
    grad_x, dg_attn = _mm_normbwd([(a, None, w_in_t[lo:hi]) for a, lo, hi in pieces], x, sm["attn_norm_g"] + tok, dx1,
                                  256, "in_proj_bwd")

    small = dict(
        attn_norm_g=dg_attn, q_norm_g=dgq.reshape(-1, HEAD_DIM).sum(0, keepdims=True),
        k_norm_g=dgk.reshape(-1, HEAD_DIM).sum(0, keepdims=True),
        ssm_conv_w=jnp.concatenate([d_cwx, d_cwb], axis=1), ssm_conv_b=jnp.concatenate([d_cbx, d_cbb], axis=1),
        dt_bias=d_dtb[:, :SSM_HEADS], a_log=d_alog[:, :SSM_HEADS],
        d_skip=d_dsk_e.reshape(SSM_HEADS, HEAD_DIM).sum(1)[None, :], ssm_norm_g=dg_ssm, ffn_norm_g=dg_ffn,
        ffn_conv_w=d_ffn_cw.reshape(3, 2 * D_FF), ffn_conv_b=d_ffn_cb.reshape(1, 2 * D_FF), ple_norm_g=dg_ple)
    return loss[0, 0], grad_x, small


_SMALL = (("attn_norm_g", 1024), ("q_norm_g", 64), ("k_norm_g", 64), ("ssm_conv_w", 4 * XBC_DIM),
          ("ssm_conv_b", XBC_DIM), ("dt_bias", 16), ("a_log", 16), ("d_skip", 16), ("ssm_norm_g", 1024),
          ("ffn_norm_g", 1024), ("ffn_conv_w", 3 * 2 * D_FF), ("ffn_conv_b", 2 * D_FF), ("ple_norm_g", 1024))
_SMALL_ROWS = 34
_SHARD_SMALL = (("attn_norm_g", 1024), ("q_norm_g", 64), ("k_norm_g", 64), ("ssm_conv_w", 4 * XBC_DIM // N_DEV),
                ("ssm_conv_b", XBC_DIM), ("dt_bias", 16), ("a_log", 16), ("d_skip", 16), ("ssm_norm_g", 1024),
                ("ffn_norm_g", 1024), ("ffn_conv_w", 3 * 2 * D_FF // N_DEV), ("ffn_conv_b", 2 * D_FF),
                ("ple_norm_g", 1024))
_SHARD_SMALL_ROWS = 14


def _pack_flat(parts, order, rows):
    flat = jnp.concatenate([parts[name].reshape(-1) for name, _ in order])
    return jnp.pad(flat, (0, rows * 1024 - flat.shape[0])).reshape(rows, 1024)


def _unpack_flat(packed, order):
    flat, out, pos = packed.reshape(-1), {}, 0
    for name, size in order:
        out[name] = flat[pos:pos + size]
        pos += size
    return out


def kernel(x, p, attn_norm_g, w_in, q_norm_g, k_norm_g, ssm_conv_w, ssm_conv_b, dt_bias, a_log, d_skip, ssm_norm_g, w_out, ffn_norm_g, w_up, ffn_conv_w, ffn_conv_b, w_down, ple_norm_g, w_ple_gate, w_ple_proj, loss_target, m_attn_norm_g, m_w_in, m_q_norm_g, m_k_norm_g, m_ssm_conv_w, m_ssm_conv_b, m_dt_bias, m_a_log, m_d_skip, m_ssm_norm_g, m_w_out, m_ffn_norm_g, m_w_up, m_ffn_conv_w, m_ffn_conv_b, m_w_down, m_ple_norm_g, m_w_ple_gate, m_w_ple_proj, v_attn_norm_g, v_w_in, v_q_norm_g, v_k_norm_g, v_ssm_conv_w, v_ssm_conv_b, v_dt_bias, v_a_log, v_d_skip, v_ssm_norm_g, v_w_out, v_ffn_norm_g, v_w_up, v_ffn_conv_w, v_ffn_conv_b, v_w_down, v_ple_norm_g, v_w_ple_gate, v_w_ple_proj):
    names = ("attn_norm_g", "w_in", "q_norm_g", "k_norm_g", "ssm_conv_w", "ssm_conv_b", "dt_bias", "a_log", "d_skip",
             "ssm_norm_g", "w_out", "ffn_norm_g", "w_up", "ffn_conv_w", "ffn_conv_b", "w_down", "ple_norm_g",
             "w_ple_gate", "w_ple_proj")
    w = dict(zip(names, (attn_norm_g, w_in, q_norm_g, k_norm_g, ssm_conv_w, ssm_conv_b, dt_bias, a_log, d_skip,
                         ssm_norm_g, w_out, ffn_norm_g, w_up, ffn_conv_w, ffn_conv_b, w_down, ple_norm_g, w_ple_gate,
                         w_ple_proj)))
    m = dict(zip(names, (m_attn_norm_g, m_w_in, m_q_norm_g, m_k_norm_g, m_ssm_conv_w, m_ssm_conv_b, m_dt_bias,
                         m_a_log, m_d_skip, m_ssm_norm_g, m_w_out, m_ffn_norm_g, m_w_up, m_ffn_conv_w, m_ffn_conv_b,
                         m_w_down, m_ple_norm_g, m_w_ple_gate, m_w_ple_proj)))
    v = dict(zip(names, (v_attn_norm_g, v_w_in, v_q_norm_g, v_k_norm_g, v_ssm_conv_w, v_ssm_conv_b, v_dt_bias,
                         v_a_log, v_d_skip, v_ssm_norm_g, v_w_out, v_ffn_norm_g, v_w_up, v_ffn_conv_w, v_ffn_conv_b,
                         v_w_down, v_ple_norm_g, v_w_ple_gate, v_w_ple_proj)))
    w, m, v = ({k: a[0] for k, a in d.items()} for d in (w, m, v))
    me = 4 * lax.axis_index("x") + 2 * lax.axis_index("y") + lax.axis_index("c")

    mine = dict(in_t=w["w_in"].T, out=w["w_out"], up_t=w["w_up"].T, down=w["w_down"], gate=w["w_ple_gate"],
                proj_t=w["w_ple_proj"].T)
    mine = {k: a.astype(BF16) for k, a in mine.items()}
    conv_pack = jnp.pad(jnp.concatenate([w["ssm_conv_w"].reshape(-1), w["ffn_conv_w"].reshape(-1)]),
                        (0, 3072 - 2880)).reshape(8, 384)
    all_in, all_conv = _exchange([], [mine["in_t"], conv_pack], "gather_first")
    later = ("out", "up_t", "down", "gate", "proj_t")
    zones = [lax.dynamic_update_slice(lax.empty((N_DEV,) + mine[k].shape, BF16), mine[k][None], (me, 0, 0))
             for k in later]
    zones, all_in, all_conv = lax.optimization_barrier((zones, all_in, all_conv))
    rest_state = _exchange_start([mine[k] for k in later], zones, [False] * len(later), "gather_rest_start")

    def fetch_rest(after):
        _, landed = _exchange_wait(*rest_state[:4], [False] * len(later), after, "gather_rest_wait")
        return {k: a.reshape(N_DEV * a.shape[1], a.shape[2]) for k, a in zip(later, landed)}

    wts = dict(in_t=_pad_rows(all_in.reshape(IN_PROJ, D_MODEL), IN_PROJ_PAD))
    conv_flat = all_conv.reshape(N_DEV, 3072)
    wts["ssm_cw"] = conv_flat[:, :768].reshape(N_DEV, 4, XBC_DIM // N_DEV).transpose(1, 0, 2).reshape(4, XBC_DIM)
    wts["ffn_cw"] = conv_flat[:, 768:2880].reshape(N_DEV, 3, 2 * D_FF // N_DEV).transpose(1, 0, 2).reshape(3, 2 * D_FF)
    sm = {k: w[k].reshape(1, -1) for k, _ in _SMALL if k not in ("ssm_conv_w", "ffn_conv_w")}

    in_flight = []

    def send(grads):
        keys = sorted(grads)
        srcs = [grads[k].reshape(N_DEV, grads[k].shape[0] // N_DEV, grads[k].shape[1]) for k in keys]
        state = _exchange_start(srcs, [lax.empty(a.shape, F32) for a in srcs], [True] * len(keys),
                                "send_" + "_".join(keys))
        in_flight.append((keys, state))
        return state[4][0:1, 0:1]

    loss, grad_x, small = _local_step(x[0], p[0, 0], loss_target[0], sm, wts, fetch_rest, send,
                                      rest_state[4][0:1, 0:1])
    loss = lax.psum(loss, ("x", "y", "c"))

    (got_small,) = _exchange([], [_pack_flat(small, _SMALL, _SMALL_ROWS)], "gather_small_grads")
    g_small = _unpack_flat(_reduce8(got_small, _SMALL_ROWS, "reduce_small"), _SMALL)
    grads = {}
    for keys, state in in_flight:
        sent, landed = _exchange_wait(*state[:4], [True] * len(keys), grad_x, "wait_" + "_".join(keys))
        for k, own, land in zip(keys, sent, landed):
            grads[k] = _reduce_landed(lax.dynamic_index_in_dim(own, me, 0, keepdims=False), land, "reduce_" + k)
    gw = {"w_in": grads["in_t"].T, "w_out": grads["out"], "w_up": grads["up_t"].T, "w_down": grads["down"],
          "w_ple_gate": grads["gate"], "w_ple_proj": grads["proj_t"].T}
    for k, size in _SMALL:
        gw[k] = g_small[k].reshape(w[k].shape) if k not in ("ssm_conv_w", "ffn_conv_w") else None
    n_ssm, n_ffn = XBC_DIM // N_DEV, 2 * D_FF // N_DEV
    gw["ssm_conv_w"] = lax.dynamic_slice(g_small["ssm_conv_w"].reshape(4, XBC_DIM), (0, me * n_ssm), (4, n_ssm))
    gw["ffn_conv_w"] = lax.dynamic_slice(g_small["ffn_conv_w"].reshape(3, 2 * D_FF), (0, me * n_ffn), (3, n_ffn))

    delta, new_m, new_v = {}, {}, {}
    for k, tr in (("w_in", 256), ("w_out", None), ("w_up", 256), ("w_down", None), ("w_ple_gate", None),
                  ("w_ple_proj", None)):
        delta[k], new_m[k], new_v[k] = _adamw(w[k], gw[k], m[k], v[k], "adamw_" + k, tr)
    packs = [_pack_flat(d, _SHARD_SMALL, _SHARD_SMALL_ROWS) for d in (w, gw, m, v)]
    for d, packed in zip((delta, new_m, new_v), _adamw(*packs, "adamw_small")):
        for k, a in _unpack_flat(packed, _SHARD_SMALL).items():
            d[k] = a.reshape(w[k].shape)

    outs = [loss, grad_x[None]]
    for d in (gw, delta, new_m, new_v):
        outs += [d[k][None] for k in names]
    return tuple(outs)
```

```python
import functools

import numpy as np
import jax
import jax.numpy as jnp
from jax import lax
from jax.experimental import pallas as pl
from jax.experimental.pallas import tpu as pltpu

F32 = jnp.float32
BF16 = jnp.bfloat16
SDS = jax.ShapeDtypeStruct
EPS = 1e-6
N_DEV = 8
D_MODEL = 1024
HEAD_DIM = 64
ATTN_DIM = 512
KV_DIM = 256
SSM_INNER = 1024
SSM_HEADS = 16
BC_DIM = 256
XBC_DIM = SSM_INNER + 2 * BC_DIM
MIX_DIM = ATTN_DIM + SSM_INNER
IN_PROJ = 3600
IN_PROJ_PAD = 3840
D_FF = 2816
PLE_DIM = 256
CHUNK = 128
SUPER = 2048
DILATIONS = (1, 4, 16)
TILE_UNROLL = 4
VMEM_LIMIT = 56 * 1024 * 1024
ADAM_LR, ADAM_B1, ADAM_B2, ADAM_EPS, ADAM_WD, ADAM_STEP = 0.001, 0.9, 0.999, 1e-08, 0.01, 10

NT = (((1,), (1,)), ((), ()))
TN = (((0,), (0,)), ((), ()))


def _params(*sem):
    return pltpu.CompilerParams(dimension_semantics=sem if sem else None, vmem_limit_bytes=VMEM_LIMIT)


def _dot(a, b, dims=None):
    if dims is None:
        return jnp.dot(a, b, preferred_element_type=F32)
    return lax.dot_general(a, b, dims, preferred_element_type=F32)


def _hdot(a, b, parts=2):
    a_exact = a.dtype == BF16
    x = b if a_exact else a
    acc = None
    for _ in range(parts):
        piece = x.astype(BF16)
        x = x - piece.astype(F32)
        d = _dot(a, piece) if a_exact else _dot(piece, b)
        acc = d if acc is None else acc + d
    return acc


def _sigmoid(x):
    return 1.0 / (1.0 + jnp.exp(-x))


def _shift_down(x, halo8, s):
    if s == 0:
        return x
    n = x.shape[0]
    row = lax.broadcasted_iota(jnp.int32, x.shape, 0)
    return jnp.where(row < s, jnp.tile(pltpu.roll(halo8, s, 0), (n // 8, 1)), pltpu.roll(x, s, 0))


def _shift_up(x, halo8, s):
    if s == 0:
        return x
    n = x.shape[0]
    row = lax.broadcasted_iota(jnp.int32, x.shape, 0)
    return jnp.where(row >= n - s, jnp.tile(pltpu.roll(halo8, 8 - s, 0), (n // 8, 1)), pltpu.roll(x, n - s, 0))


def _norm_matmul(x, g, wt, tm, tn, name):
    t, k = x.shape
    n = wt.shape[0]

    def body(x_ref, g_ref, w_ref, o_ref, h_ref):
        @pl.when(pl.program_id(1) == 0)
        def _():
            xv = x_ref[...]
            r = lax.rsqrt(jnp.mean(xv * xv, axis=-1, keepdims=True) + EPS)
            h_ref[...] = (xv * r * g_ref[...]).astype(BF16)
        o_ref[...] = _dot(h_ref[...], w_ref[...], NT)

    return pl.pallas_call(
        body, name=name, grid=(t // tm, n // tn),
        in_specs=[pl.BlockSpec((tm, k), lambda i, j: (i, 0)), pl.BlockSpec((1, k), lambda i, j: (0, 0)),
                  pl.BlockSpec((tn, k), lambda i, j: (j, 0))],
        out_specs=[pl.BlockSpec((tm, tn), lambda i, j: (i, j)), pl.BlockSpec((tm, k), lambda i, j: (i, 0))],
        out_shape=[SDS((t, n), F32), SDS((t, k), BF16)],
        compiler_params=_params("parallel", "arbitrary"))(x, g, wt)


def _a_spec(a, lead, tm):
    if lead is None:
        return pl.BlockSpec((tm, a.shape[-1]), lambda i: (i, 0))
    return pl.BlockSpec((None, tm, a.shape[-1]), lambda i, _l=lead: (_l, i, 0))


def _mm_resid(pairs, res, dims, tm, out_dtype, name):
    t = pairs[0][0].shape[-2]
    n = pairs[0][2].shape[1] if dims is None else pairs[0][2].shape[0]
    np_ = len(pairs)

    def body(*refs):
        o_ref = refs[-1]
        acc = refs[2 * np_][...] if res is not None else None
        for q in range(np_):
            d = _dot(refs[q][...].astype(BF16), refs[np_ + q][...], dims)
            acc = d if acc is None else acc + d
        o_ref[...] = acc.astype(out_dtype)

    in_specs = [_a_spec(a, lead, tm) for a, lead, _ in pairs]
    in_specs += [pl.BlockSpec(b.shape, lambda i: (0, 0)) for _, _, b in pairs]
    args = [a for a, _, _ in pairs] + [b for _, _, b in pairs]
    if res is not None:
        in_specs.append(pl.BlockSpec((tm, n), lambda i: (i, 0)))
        args.append(res)
    return pl.pallas_call(
        body, name=name, grid=(t // tm,), in_specs=in_specs,
        out_specs=pl.BlockSpec((tm, n), lambda i: (i, 0)), out_shape=SDS((t, n), out_dtype),
        compiler_params=_params("parallel"))(*args)


def _mm_normbwd(pairs, x, g, dres, tm, name):
    t, k = x.shape
    np_ = len(pairs)

    def body(*refs):
        x_ref, g_ref, dres_ref, dx_ref, dg_ref = refs[2 * np_:]
        dh = None
        for q in range(np_):
            d = _dot(refs[q][...], refs[np_ + q][...])
            dh = d if dh is None else dh + d
        xv = x_ref[...]
        r = lax.rsqrt(jnp.mean(xv * xv, axis=-1, keepdims=True) + EPS)
        xh = xv * r

        @pl.when(pl.program_id(0) == 0)
        def _():
            dg_ref[...] = jnp.zeros_like(dg_ref)
        dg_ref[...] += jnp.sum(dh * xh, axis=0, keepdims=True)
        gd = dh * g_ref[...]
        dx_ref[...] = dres_ref[...] + r * (gd - xh * jnp.mean(gd * xh, axis=-1, keepdims=True))

    in_specs = [_a_spec(a, lead, tm) for a, lead, _ in pairs]
    in_specs += [pl.BlockSpec(b.shape, lambda i: (0, 0)) for _, _, b in pairs]
    in_specs += [pl.BlockSpec((tm, k), lambda i: (i, 0)), pl.BlockSpec((1, k), lambda i: (0, 0)),
                 pl.BlockSpec((tm, k), lambda i: (i, 0))]
    args = [a for a, _, _ in pairs] + [b for _, _, b in pairs] + [x, g, dres]
    return pl.pallas_call(
        body, name=name, grid=(t // tm,), in_specs=in_specs,
        out_specs=[pl.BlockSpec((tm, k), lambda i: (i, 0)), pl.BlockSpec((1, k), lambda i: (0, 0))],
        out_shape=[SDS((t, k), F32), SDS((1, k), F32)],
        compiler_params=_params("arbitrary"))(*args)


def _wgrad(a, a_lead, b, name, tk=512):
    t, m = a.shape[-2:]
    n = b.shape[1]
    tm = m if m <= 1024 else 1408
    assert m % tm == 0

    def body(a_ref, b_ref, o_ref):
        @pl.when(pl.program_id(1) == 0)
        def _():
            o_ref[...] = jnp.zeros_like(o_ref)
        o_ref[...] += _dot(a_ref[...].astype(BF16), b_ref[...].astype(BF16), TN)

    if a_lead is None:
        a_spec = pl.BlockSpec((tk, tm), lambda mi, ki: (ki, mi))
    else:
        a_spec = pl.BlockSpec((None, tk, tm), lambda mi, ki, _l=a_lead: (_l, ki, mi))
    return pl.pallas_call(
        body, name=name, grid=(m // tm, t // tk),
        in_specs=[a_spec, pl.BlockSpec((tk, n), lambda mi, ki: (ki, 0))],
        out_specs=pl.BlockSpec((tm, n), lambda mi, ki: (mi, 0)), out_shape=SDS((m, n), F32),
        compiler_params=_params("parallel", "arbitrary"))(a, b)


def _head_consts():
    iq = np.arange(ATTN_DIM)
    ik = np.arange(KV_DIM)
    ones_q = (iq[:, None] // HEAD_DIM == iq[None, :] // HEAD_DIM).astype(np.float32)
    ones_k = (ik[:, None] // HEAD_DIM == ik[None, :] // HEAD_DIM).astype(np.float32)
    dup = (ik[:, None] == (HEAD_DIM * (iq[None, :] // 128) + iq[None, :] % HEAD_DIM)).astype(np.float32)
    return jnp.asarray(ones_q, BF16), jnp.asarray(ones_k, BF16), jnp.asarray(dup, BF16), jnp.asarray(dup.T, BF16)


def _attn_prep(proj, gq, gk, ones_q, ones_k, dup, tm=512):
    t = proj.shape[0]

    def body(p_ref, gq_ref, gk_ref, oq_ref, ok_ref, dup_ref, qn_ref, kd_ref, vd_ref):
        q = p_ref[:, 0:ATTN_DIM]
        k = p_ref[:, ATTN_DIM:ATTN_DIM + KV_DIM]
        v = p_ref[:, ATTN_DIM + KV_DIM:]
        rq = lax.rsqrt(_hdot(q * q, oq_ref[...]) * (1.0 / HEAD_DIM) + EPS)
        qn_ref[...] = (q * rq * gq_ref[...]) * (HEAD_DIM ** -0.5)
        rk = lax.rsqrt(_hdot(k * k, ok_ref[...]) * (1.0 / HEAD_DIM) + EPS)
        kn = k * rk * gk_ref[...]
        kd_ref[...] = _dot(kn.astype(BF16), dup_ref[...])
        vd_ref[...] = _dot(v.astype(BF16), dup_ref[...])

    full = lambda a: pl.BlockSpec(a.shape, lambda i: (0, 0))
    o_spec = pl.BlockSpec((tm, ATTN_DIM), lambda i: (i, 0))
    return pl.pallas_call(
        body, name="attn_prep", grid=(t // tm,),
        in_specs=[pl.BlockSpec((tm, 1024), lambda i: (i, 0)), full(gq), full(gk), full(ones_q), full(ones_k), full(dup)],
        out_specs=[o_spec, o_spec, o_spec], out_shape=[SDS((t, ATTN_DIM), F32)] * 3,
        compiler_params=_params("parallel"))(proj, gq, gk, ones_q, ones_k, dup)


def _attn_prep_bwd(proj, dqn, dkc, dkp, dvc, dvp, gq, gk, ones_q, ones_k, dup_t, tm=512):
    t = proj.shape[0]
    nblk = t // tm
    off = SUPER // tm

    def body(p_ref, dqn_ref, dkc_ref, dkp_ref, dvc_ref, dvp_ref, gq_ref, gk_ref, oq_ref, ok_ref, dt_ref,
             o_ref, dgq_ref, dgk_ref):
        i = pl.program_id(0)
        has_next = (i + off < nblk).astype(F32)
        q = p_ref[:, 0:ATTN_DIM]
        k = p_ref[:, ATTN_DIM:ATTN_DIM + KV_DIM]
        dkn = _hdot(dkc_ref[...] + has_next * dkp_ref[...], dt_ref[...])
        dv = _hdot(dvc_ref[...] + has_next * dvp_ref[...], dt_ref[...])

        @pl.when(i == 0)
        def _():
            dgq_ref[...] = jnp.zeros_like(dgq_ref)
            dgk_ref[...] = jnp.zeros_like(dgk_ref)

        rq = lax.rsqrt(_hdot(q * q, oq_ref[...]) * (1.0 / HEAD_DIM) + EPS)
        xh = q * rq
        dy = dqn_ref[...] * (HEAD_DIM ** -0.5)
        dgq_ref[...] += jnp.sum(dy * xh, axis=0, keepdims=True)
        gd = dy * gq_ref[...]
        dq = rq * (gd - xh * (_hdot(gd * xh, oq_ref[...]) * (1.0 / HEAD_DIM)))
        rk = lax.rsqrt(_hdot(k * k, ok_ref[...]) * (1.0 / HEAD_DIM) + EPS)
        kh = k * rk
        dgk_ref[...] += jnp.sum(dkn * kh, axis=0, keepdims=True)
        gdk = dkn * gk_ref[...]
        dk = rk * (gdk - kh * (_hdot(gdk * kh, ok_ref[...]) * (1.0 / HEAD_DIM)))
        o_ref[:, 0:ATTN_DIM] = dq.astype(BF16)
        o_ref[:, ATTN_DIM:ATTN_DIM + KV_DIM] = dk.astype(BF16)
        o_ref[:, ATTN_DIM + KV_DIM:] = dv.astype(BF16)

    full = lambda a: pl.BlockSpec(a.shape, lambda i: (0, 0))
    cur = pl.BlockSpec((tm, ATTN_DIM), lambda i: (i, 0))
    nxt = pl.BlockSpec((tm, ATTN_DIM), lambda i: (jnp.minimum(i + off, nblk - 1), 0))
    return pl.pallas_call(
        body, name="attn_prep_bwd", grid=(nblk,),
        in_specs=[pl.BlockSpec((tm, 1024), lambda i: (i, 0)), cur, cur, nxt, cur, nxt,
                  full(gq), full(gk), full(ones_q), full(ones_k), full(dup_t)],
        out_specs=[pl.BlockSpec((tm, 1024), lambda i: (i, 0)), pl.BlockSpec((1, ATTN_DIM), lambda i: (0, 0)),
                   pl.BlockSpec((1, KV_DIM), lambda i: (0, 0))],
        out_shape=[SDS((t, 1024), BF16), SDS((1, ATTN_DIM), F32), SDS((1, KV_DIM), F32)],
        compiler_params=_params("arbitrary"))(proj, dqn, dkc, dkp, dvc, dvp, gq, gk, ones_q, ones_k, dup_t)


def _tile_masks():
    qi = lax.broadcasted_iota(jnp.int32, (2 * CHUNK, 2 * CHUNK), 0) & (CHUNK - 1)
    kj = lax.broadcasted_iota(jnp.int32, (2 * CHUNK, 2 * CHUNK), 1)
    delta = CHUNK + qi - kj
    band = (delta >= 0) & (delta <= CHUNK)
    return band, kj


def _deinterleave(dst, src, n_rows, d):
    per = n_rows // d
    for r in range(d):
        dst[r * per:(r + 1) * per, :] = src[pl.ds(r, per, stride=d), :]


def _attn_specs(t):
    blk = lambda f: pl.BlockSpec((SUPER, 128), f)
    cur = blk(lambda h, s: (s, h))
    prev = blk(lambda h, s: (jnp.maximum(s - 1, 0), h))
    return cur, prev


def _attn_fwd(qn, kd, vd):
    t = qn.shape[0]
    cur, prev = _attn_specs(t)

    def body(q_ref, kp_ref, kc_ref, vp_ref, vc_ref, o_ref, lse_ref, kk, vv, qd, kdd, vdd, po, pm, pll, acc, mm, ll):
        s = pl.program_id(1)
        kk[0:SUPER, :] = kp_ref[...]
        kk[SUPER:, :] = kc_ref[...]
        vv[0:SUPER, :] = vp_ref[...]
        vv[SUPER:, :] = vc_ref[...]
        m0 = lax.broadcasted_iota(jnp.int32, (CHUNK, 128), 1) < HEAD_DIM
        band, kj = _tile_masks()
        for d in DILATIONS:
            lq = SUPER // d
            if d == 1:
                qs_ref, ks_ref, vs_ref = q_ref, kk, vv
            else:
                _deinterleave(qd, q_ref, SUPER, d)
                _deinterleave(kdd, kk, 2 * SUPER, d)
                _deinterleave(vdd, vv, 2 * SUPER, d)
                qs_ref, ks_ref, vs_ref = qd, kdd, vdd

            def tile(ti, carry):
                r = ti // (lq // CHUNK)
                nb = ti % (lq // CHUNK)
                qoff = pl.multiple_of(ti * CHUNK, CHUNK)
                koff = pl.multiple_of(r * 2 * lq + lq + (nb - 1) * CHUNK, CHUNK)
                qt = qs_ref[pl.ds(qoff, CHUNK), :]
                qs = jnp.concatenate([jnp.where(m0, qt, 0.0), jnp.where(m0, 0.0, qt)], axis=0).astype(BF16)
                kt = ks_ref[pl.ds(koff, 2 * CHUNK), :].astype(BF16)
                vt = vs_ref[pl.ds(koff, 2 * CHUNK), :].astype(BF16)
                sc = _dot(qs, kt, NT)
                ok = band & (kj >= jnp.where((s > 0) | (nb > 0), 0, CHUNK))
                sc = jnp.where(ok, sc, -jnp.inf)
                mt = jnp.max(sc, axis=-1, keepdims=True)
                p = jnp.exp(sc - mt)
                lt = jnp.sum(p, axis=-1, keepdims=True)
                ot = _dot(p.astype(BF16), vt)
                po[pl.ds(qoff, CHUNK), :] = jnp.where(m0, ot[:CHUNK], ot[CHUNK:])
                pm[pl.ds(qoff, CHUNK), :] = jnp.where(m0, mt[:CHUNK], mt[CHUNK:])
                pll[pl.ds(qoff, CHUNK), :] = jnp.where(m0, lt[:CHUNK], lt[CHUNK:])
                return carry

            lax.fori_loop(0, SUPER // CHUNK, tile, 0, unroll=TILE_UNROLL)
            if d == 1:
                acc[...] = po[...]
                mm[...] = pm[...]
                ll[...] = pll[...]
            else:
                for r in range(d):
                    rows = pl.ds(r, lq, stride=d)
                    seg = slice(r * lq, (r + 1) * lq)
                    m_old, m_new = mm[rows, :], pm[seg, :]
                    m_all = jnp.maximum(m_old, m_new)
                    a, b = jnp.exp(m_old - m_all), jnp.exp(m_new - m_all)
                    acc[rows, :] = acc[rows, :] * a + po[seg, :] * b
                    ll[rows, :] = ll[rows, :] * a + pll[seg, :] * b
                    mm[rows, :] = m_all
        o_ref[...] = acc[...] / ll[...]
        lse_ref[...] = mm[...] + jnp.log(ll[...])

    big = pltpu.VMEM((2 * SUPER, 128), F32)
    one = pltpu.VMEM((SUPER, 128), F32)
    return pl.pallas_call(
        body, name="attn_fwd", grid=(4, t // SUPER),
        in_specs=[cur, prev, cur, prev, cur], out_specs=[cur, cur],
        out_shape=[SDS((t, ATTN_DIM), F32)] * 2,
        scratch_shapes=[big, big, one, big, big, one, one, one, one, one, one],
        compiler_params=_params("parallel", "arbitrary"))(qn, kd, kd, vd, vd)


def _attn_bwd(qn, kd, vd, out, lse, dout):
    t = qn.shape[0]
    cur, prev = _attn_specs(t)

    def body(q_ref, kp_ref, kc_ref, vp_ref, vc_ref, o_ref, lse_ref, do_ref,
             dq_ref, dkc_ref, dkp_ref, dvc_ref, dvp_ref,
             kk, vv, dkk, dvv, qd, od, ld, dod, kdd, vdd, dkd, dvd, pdq):
        s = pl.program_id(1)
        kk[0:SUPER, :] = kp_ref[...]
        kk[SUPER:, :] = kc_ref[...]
        vv[0:SUPER, :] = vp_ref[...]
        vv[SUPER:, :] = vc_ref[...]
        dkk[...] = jnp.zeros_like(dkk)
        dvv[...] = jnp.zeros_like(dvv)
        dq_ref[...] = jnp.zeros_like(dq_ref)
        m0 = lax.broadcasted_iota(jnp.int32, (CHUNK, 128), 1) < HEAD_DIM
        band, kj = _tile_masks()
        ninf = -jnp.inf
        for d in DILATIONS:
            lq = SUPER // d
            if d == 1:
                qs_ref, os_ref, ls_ref, dos_ref, ks_ref, vs_ref, dks_ref, dvs_ref = (
                    q_ref, o_ref, lse_ref, do_ref, kk, vv, dkk, dvv)
            else:
                _deinterleave(qd, q_ref, SUPER, d)
                _deinterleave(od, o_ref, SUPER, d)
                _deinterleave(ld, lse_ref, SUPER, d)
                _deinterleave(dod, do_ref, SUPER, d)
                _deinterleave(kdd, kk, 2 * SUPER, d)
                _deinterleave(vdd, vv, 2 * SUPER, d)
                dkd[...] = jnp.zeros_like(dkd)
                dvd[...] = jnp.zeros_like(dvd)
                qs_ref, os_ref, ls_ref, dos_ref, ks_ref, vs_ref, dks_ref, dvs_ref = (
                    qd, od, ld, dod, kdd, vdd, dkd, dvd)

            def tile(ti, carry):
                r = ti // (lq // CHUNK)
                nb = ti % (lq // CHUNK)
                qoff = pl.multiple_of(ti * CHUNK, CHUNK)
                koff = pl.multiple_of(r * 2 * lq + lq + (nb - 1) * CHUNK, CHUNK)
                qrows = pl.ds(qoff, CHUNK)
                krows = pl.ds(koff, 2 * CHUNK)
                qt, ot, lt, dot_ = qs_ref[qrows, :], os_ref[qrows, :], ls_ref[qrows, :], dos_ref[qrows, :]
                qs = jnp.concatenate([jnp.where(m0, qt, 0.0), jnp.where(m0, 0.0, qt)], axis=0).astype(BF16)
                dos = jnp.concatenate([jnp.where(m0, dot_, 0.0), jnp.where(m0, 0.0, dot_)], axis=0).astype(BF16)
                lse_rows = jnp.concatenate([jnp.max(jnp.where(m0, lt, ninf), axis=-1, keepdims=True),
                                            jnp.max(jnp.where(m0, ninf, lt), axis=-1, keepdims=True)], axis=0)
                prod = dot_ * ot
                dl_rows = jnp.concatenate([jnp.sum(jnp.where(m0, prod, 0.0), axis=-1, keepdims=True),
                                           jnp.sum(jnp.where(m0, 0.0, prod), axis=-1, keepdims=True)], axis=0)
                kt = ks_ref[krows, :].astype(BF16)
                vt = vs_ref[krows, :].astype(BF16)
                sc = _dot(qs, kt, NT)
                ok = band & (kj >= jnp.where((s > 0) | (nb > 0), 0, CHUNK))
                p = jnp.exp(jnp.where(ok, sc, ninf) - lse_rows)
                dp = _dot(dos, vt, NT)
                ds = p * (dp - dl_rows)
                dqs = _dot(ds.astype(BF16), kt)
                pdq[qrows, :] = jnp.where(m0, dqs[:CHUNK], dqs[CHUNK:])
                dks_ref[krows, :] += _dot(ds.astype(BF16), qs, TN)
                dvs_ref[krows, :] += _dot(p.astype(BF16), dos, TN)
                return carry

            lax.fori_loop(0, SUPER // CHUNK, tile, 0, unroll=TILE_UNROLL)
            if d == 1:
                dq_ref[...] += pdq[...]
            else:
                for r in range(d):
                    dq_ref[pl.ds(r, lq, stride=d), :] += pdq[r * lq:(r + 1) * lq, :]
                    dkk[pl.ds(r, 2 * lq, stride=d), :] += dkd[r * 2 * lq:(r + 1) * 2 * lq, :]
                    dvv[pl.ds(r, 2 * lq, stride=d), :] += dvd[r * 2 * lq:(r + 1) * 2 * lq, :]
        dkp_ref[...] = dkk[0:SUPER, :]
        dkc_ref[...] = dkk[SUPER:, :]
        dvp_ref[...] = dvv[0:SUPER, :]
        dvc_ref[...] = dvv[SUPER:, :]

    big = pltpu.VMEM((2 * SUPER, 128), F32)
    one = pltpu.VMEM((SUPER, 128), F32)
    return pl.pallas_call(
        body, name="attn_bwd", grid=(4, t // SUPER),
        in_specs=[cur, prev, cur, prev, cur, cur, cur, cur], out_specs=[cur] * 5,
        out_shape=[SDS((t, ATTN_DIM), F32)] * 5,
        scratch_shapes=[big, big, big, big, one, one, one, one, big, big, big, big, one],
        compiler_params=_params("parallel", "arbitrary"))(qn, kd, kd, vd, vd, out, lse, dout)


def _ssd_consts():
    tri = np.tril(np.ones((CHUNK, CHUNK), np.float32))
    expand = np.zeros((128, SSM_INNER), np.float32)
    for h in range(SSM_HEADS):
        expand[h, h * HEAD_DIM:(h + 1) * HEAD_DIM] = 1.0
    return jnp.asarray(tri, BF16), jnp.asarray(tri.T, BF16), jnp.asarray(expand, BF16), jnp.asarray(expand.T, BF16)


def _conv4(x, halo, w_ref, b_ref):
    acc = b_ref[...] + w_ref[3:4, :] * x
    for k in range(3):
        acc = acc + w_ref[k:k + 1, :] * _shift_down(x, halo, 3 - k)
    return acc


def _softplus(x):
    return jnp.maximum(x, 0.0) + jnp.log(1.0 + jnp.exp(-jnp.abs(x)))


def _ssd_common(xs_ref, bc_ref, dt_ref, hx_ref, hb_ref, cwx_ref, cbx_ref, cwb_ref, cbb_ref, dtb_ref, alog_ref,
                tri_ref, exp_ref, first):
    keep = 1.0 - first.astype(F32)
    hx = hx_ref[...] * keep
    hb = hb_ref[...] * keep
    pre_x = _conv4(xs_ref[...], hx, cwx_ref, cbx_ref)
    pre_b = _conv4(bc_ref[...], hb, cwb_ref, cbb_ref)
    xa = pre_x * _sigmoid(pre_x)
    ba = pre_b * _sigmoid(pre_b)
    dtv = _softplus(dt_ref[...] + dtb_ref[...])
    a_neg = -jnp.exp(alog_ref[...])
    acum = _hdot(tri_ref[...], dtv * a_neg, parts=3)
    lam = jnp.exp(acum)
    gam = jnp.exp(acum[CHUNK - 1:CHUNK, :] - acum)
    dt_e = _hdot(dtv, exp_ref[...])
    lam_e = _hdot(lam, exp_ref[...])
    gam_e = _hdot(gam, exp_ref[...])
    return dict(hx=hx, hb=hb, pre_x=pre_x, pre_b=pre_b, xa=xa, ba=ba, dtv=dtv, a_neg=a_neg, acum=acum,
                dt_e=dt_e, lam_e=lam_e, gam_e=gam_e, xdt=xa * dt_e)


def _decay(acum_t, h, transposed):
    rb = jnp.broadcast_to(acum_t[h:h + 1, :], (CHUNK, CHUNK))
    ri = lax.broadcasted_iota(jnp.int32, (CHUNK, CHUNK), 0)
    ci = lax.broadcasted_iota(jnp.int32, (CHUNK, CHUNK), 1)
    if transposed:
        return jnp.exp(jnp.where(ci >= ri, rb - rb.T, -jnp.inf))
    return jnp.exp(jnp.where(ri >= ci, rb.T - rb, -jnp.inf))


def _ssd_specs(t, rev):
    nc = t // CHUNK
    ch = (lambda c: nc - 1 - c) if rev else (lambda c: c)
    col = lambda w, j: pl.BlockSpec((CHUNK, w), lambda c: (ch(c), j))
    halo = lambda w, j: pl.BlockSpec((8, w), lambda c: (jnp.maximum(ch(c) * (CHUNK // 8) - 1, 0), j))
    return nc, ch, col, halo


def _ssd_fwd(proj, cwx, cbx, cwb, cbb, dtb, alog, dsk_e, norm_g, tri, expand):
    t = proj.shape[0]
    nc, _, col, halo = _ssd_specs(t, False)

    def body(z_ref, xs_ref, bc_ref, dt_ref, hx_ref, hb_ref, cwx_ref, cbx_ref, cwb_ref, cbb_ref, dtb_ref, alog_ref,
             dsk_ref, g_ref, tri_ref, exp_ref, y_ref, hs_ref, o_ref, state):
        c = pl.program_id(0)

        @pl.when(c == 0)
        def _():
            state[...] = jnp.zeros_like(state)

        v = _ssd_common(xs_ref, bc_ref, dt_ref, hx_ref, hb_ref, cwx_ref, cbx_ref, cwb_ref, cbb_ref, dtb_ref,
                        alog_ref, tri_ref, exp_ref, c == 0)
        acum_t = v["acum"].T
        xdt, ba = v["xdt"], v["ba"]
        h_in = state[...]
        hs_ref[0] = h_in
        xg = xdt * v["gam_e"]
        m0 = lax.broadcasted_iota(jnp.int32, (CHUNK, 128), 1) < HEAD_DIM
        for g in range(2):
            bg = ba[:, g * 128:(g + 1) * 128].astype(BF16)
            cg = ba[:, 256 + g * 128:256 + (g + 1) * 128].astype(BF16)
            gl = slice(g * 512, (g + 1) * 512)
            cb = _dot(cg, bg, NT)
            y_off = _dot(cg, h_in[:, gl].astype(BF16)) * v["lam_e"][:, gl]
            s_new = _dot(bg.T, xg[:, gl].astype(BF16))
            state[:, gl] = h_in[:, gl] * v["lam_e"][CHUNK - 1:CHUNK, gl] + s_new
            for j in range(4):
                h0 = 8 * g + 2 * j
                ln = slice(g * 512 + j * 128, g * 512 + (j + 1) * 128)
                xp = xdt[:, ln].astype(BF16)
                y0 = _dot((cb * _decay(acum_t, h0, False)).astype(BF16), xp)
                y1 = _dot((cb * _decay(acum_t, h0 + 1, False)).astype(BF16), xp)
                y_ref[:, ln] = jnp.where(m0, y0, y1) + y_off[:, j * 128:(j + 1) * 128]
        z = z_ref[...]
        yg = (y_ref[...] + dsk_ref[...] * v["xa"]) * (z * _sigmoid(z))
        r = lax.rsqrt(jnp.mean(yg * yg, axis=-1, keepdims=True) + EPS)
        o_ref[...] = (yg * r * g_ref[...]).astype(BF16)

    full = lambda a: pl.BlockSpec(a.shape, lambda c: (0,) * a.ndim)
    return pl.pallas_call(
        body, name="ssd_fwd", grid=(nc,),
        in_specs=[col(1024, 1), col(1024, 2), col(512, 6), col(128, 28), halo(1024, 2), halo(512, 6),
                  full(cwx), full(cbx), full(cwb), full(cbb), full(dtb), full(alog), full(dsk_e), full(norm_g),
                  full(tri), full(expand)],
        out_specs=[pl.BlockSpec((CHUNK, SSM_INNER), lambda c: (c, 0)),
                   pl.BlockSpec((1, 128, SSM_INNER), lambda c: (c, 0, 0)),
                   pl.BlockSpec((CHUNK, SSM_INNER), lambda c: (c, 0))],
        out_shape=[SDS((t, SSM_INNER), F32), SDS((nc, 128, SSM_INNER), F32), SDS((t, SSM_INNER), BF16)],
        scratch_shapes=[pltpu.VMEM((128, SSM_INNER), F32)],
        compiler_params=_params("arbitrary"))(proj, proj, proj, proj, proj, proj, cwx, cbx, cwb, cbb, dtb, alog,
                                              dsk_e, norm_g, tri, expand)


def _ssd_bwd(proj, y_ssd, hs, dout, cwx, cbx, cwb, cbb, dtb, alog, dsk_e, norm_g, tri, triu, expand, expand_t):
    t = proj.shape[0]
    nc, ch, col, halo = _ssd_specs(t, True)

    def body(z_ref, xs_ref, bc_ref, dt_ref, hx_ref, hb_ref, y_ref, hin_ref, do_ref,
             cwx_ref, cbx_ref, cwb_ref, cbb_ref, dtb_ref, alog_ref, dsk_ref, g_ref, tri_ref, triu_ref, exp_ref, expt_ref,
             dz_ref, dxs_ref, dbc_ref, ddt_ref, dg_ref, ddsk_ref, dalog_ref, ddtb_ref, dcwx_ref, dcbx_ref, dcwb_ref,
             dcbb_ref, gstate, nx_x, nx_b, dact_b, dxdt_s):
        step = pl.program_id(0)
        c = nc - 1 - step

        @pl.when(step == 0)
        def _():
            gstate[...] = jnp.zeros_like(gstate)
            nx_x[...] = jnp.zeros_like(nx_x)
            nx_b[...] = jnp.zeros_like(nx_b)
            for ref in (dg_ref, ddsk_ref, dalog_ref, ddtb_ref, dcwx_ref, dcbx_ref, dcwb_ref, dcbb_ref):
                ref[...] = jnp.zeros_like(ref)

        v = _ssd_common(xs_ref, bc_ref, dt_ref, hx_ref, hb_ref, cwx_ref, cbx_ref, cwb_ref, cbb_ref, dtb_ref,
                        alog_ref, tri_ref, exp_ref, c == 0)
        acum_t = v["acum"].T
        xa, ba, xdt, dtv = v["xa"], v["ba"], v["xdt"], v["dtv"]
        lam_e, gam_e, dt_e = v["lam_e"], v["gam_e"], v["dt_e"]
        z = z_ref[...]
        y = y_ref[...]
        sz = _sigmoid(z)
        zs = z * sz
        y_tot = y + dsk_ref[...] * xa
        yg = y_tot * zs
        r = lax.rsqrt(jnp.mean(yg * yg, axis=-1, keepdims=True) + EPS)
        yh = yg * r
        do = do_ref[...]
        dg_ref[...] += jnp.sum(do * yh, axis=0, keepdims=True)
        gd = do * g_ref[...]
        dyg = r * (gd - yh * jnp.mean(gd * yh, axis=-1, keepdims=True))
        dz_ref[...] = (dyg * y_tot * (sz * (1.0 + z * (1.0 - sz)))).astype(BF16)
        dy = dyg * zs
        ddsk_ref[...] += jnp.sum(dy * xa, axis=0, keepdims=True)
        g_out = gstate[...]
        h_in = hin_ref[0]
        lam_dy = lam_e * dy
        gam_x = gam_e * xdt
        m0 = lax.broadcasted_iota(jnp.int32, (CHUNK, 128), 1) < HEAD_DIM
        lane = lax.broadcasted_iota(jnp.int32, (CHUNK, 128), 1)
        below = (lax.broadcasted_iota(jnp.int32, (CHUNK, CHUNK), 0) >
                 lax.broadcasted_iota(jnp.int32, (CHUNK, CHUNK), 1))
        da_in = jnp.zeros((CHUNK, 128), F32)
        off_y, off_x = [], []
        for g in range(2):
            bg = ba[:, g * 128:(g + 1) * 128].astype(BF16)
            cg = ba[:, 256 + g * 128:256 + (g + 1) * 128].astype(BF16)
            gl = slice(g * 512, (g + 1) * 512)
            gg = g_out[:, gl].astype(BF16)
            bc_t = _dot(bg, cg, NT)
            cb = _dot(cg, bg, NT)
            dxdt_off = _dot(bg, gg) * gam_e[:, gl]
            off_x.append(xdt[:, gl] * dxdt_off)
            off_y.append(dy[:, gl] * (_dot(cg, h_in[:, gl].astype(BF16)) * lam_e[:, gl]))
            q_sum = jnp.zeros((CHUNK, CHUNK), F32)
            for j in range(4):
                h0 = 8 * g + 2 * j
                ln = slice(g * 512 + j * 128, g * 512 + (j + 1) * 128)
                dyp = dy[:, ln]
                dyb = dyp.astype(BF16)
                xpb = xdt[:, ln].astype(BF16)
                d0 = _dot((bc_t * _decay(acum_t, h0, True)).astype(BF16), dyb)
                d1 = _dot((bc_t * _decay(acum_t, h0 + 1, True)).astype(BF16), dyb)
                dxdt_s[:, ln] = jnp.where(m0, d0, d1) + dxdt_off[:, j * 128:(j + 1) * 128]
                for hh, dym in ((h0, jnp.where(m0, dyp, 0.0)), (h0 + 1, jnp.where(m0, 0.0, dyp))):
                    qd = _dot(dym.astype(BF16), xpb, NT) * _decay(acum_t, hh, False)
                    q_sum = q_sum + qd
                    reach = jnp.where(below, _hdot(triu_ref[...], qd * cb), 0.0)
                    da_in = jnp.where(lane == hh, jnp.sum(reach, axis=-1, keepdims=True), da_in)
            gstate[:, gl] = g_out[:, gl] * lam_e[CHUNK - 1:CHUNK, gl] + _dot(cg.T, lam_dy[:, gl].astype(BF16))
            qb = q_sum.astype(BF16)
            dact_b[:, 256 + g * 128:256 + (g + 1) * 128] = (
                _dot(qb, bg) + _dot(lam_dy[:, gl].astype(BF16), h_in[:, gl].astype(BF16), NT))
            dact_b[:, g * 128:(g + 1) * 128] = _dot(qb.T, cg) + _dot(gam_x[:, gl].astype(BF16), gg, NT)
        dxdt = dxdt_s[...]
        seg_y = _hdot(jnp.concatenate(off_y, axis=1), expt_ref[...])
        seg_x = _hdot(jnp.concatenate(off_x, axis=1), expt_ref[...])
        e_col = jnp.sum(g_out * h_in * lam_e[CHUNK - 1:CHUNK, :], axis=0, keepdims=True)
        e_seg = _hdot(jnp.broadcast_to(e_col, (8, SSM_INNER)), expt_ref[...])[0:1, :]
        da = da_in + _hdot(triu_ref[...], seg_y) + (_hdot(tri_ref[...], seg_x) - seg_x) + e_seg
        a_neg = v["a_neg"]
        ddtv = da * a_neg + _hdot(dxdt * xa, expt_ref[...])
        dalog_ref[...] += jnp.sum(da * dtv, axis=0, keepdims=True) * a_neg
        lane16 = lax.broadcasted_iota(jnp.int32, (CHUNK, 128), 1) < SSM_HEADS
        draw = jnp.where(lane16, ddtv * _sigmoid(dt_ref[...] + dtb_ref[...]), 0.0)
        ddtb_ref[...] += jnp.sum(draw, axis=0, keepdims=True)
        ddt_ref[...] = draw.astype(BF16)
        dxa = dxdt * dt_e + dy * dsk_ref[...]
        for (dact, pre, x_ref, halo_v, nx, cw_ref, dcw_ref, dcb_ref, dx_ref) in (
                (dxa, v["pre_x"], xs_ref, v["hx"], nx_x, cwx_ref, dcwx_ref, dcbx_ref, dxs_ref),
                (dact_b[...], v["pre_b"], bc_ref, v["hb"], nx_b, cwb_ref, dcwb_ref, dcbb_ref, dbc_ref)):
            sp = _sigmoid(pre)
            dpre = dact * (sp * (1.0 + pre * (1.0 - sp)))
            dcb_ref[...] += jnp.sum(dpre, axis=0, keepdims=True)
            xv = x_ref[...]
            nxt = nx[...]
            dx = cw_ref[3:4, :] * dpre
            dcw_ref[3:4, :] += jnp.sum(dpre * xv, axis=0, keepdims=True)
            for k in range(3):
                dcw_ref[k:k + 1, :] += jnp.sum(dpre * _shift_down(xv, halo_v, 3 - k), axis=0, keepdims=True)
                dx = dx + cw_ref[k:k + 1, :] * _shift_up(dpre, nxt, 3 - k)
            nx[...] = dpre[0:8, :]
            dx_ref[...] = dx.astype(dx_ref.dtype)

    full = lambda a: pl.BlockSpec(a.shape, lambda c: (0,) * a.ndim)
    rowblk = lambda w: pl.BlockSpec((CHUNK, w), lambda c: (ch(c), 0))
    acc = lambda a, b: pl.BlockSpec((a, b), lambda c: (0, 0))
    return pl.pallas_call(
        body, name="ssd_bwd", grid=(nc,),
        in_specs=[col(1024, 1), col(1024, 2), col(512, 6), col(128, 28), halo(1024, 2), halo(512, 6),
                  rowblk(SSM_INNER),
                  pl.BlockSpec((1, 128, SSM_INNER), lambda c: (ch(c), 0, 0)),
                  rowblk(SSM_INNER),
                  full(cwx), full(cbx), full(cwb), full(cbb), full(dtb), full(alog), full(dsk_e), full(norm_g),
                  full(tri), full(triu), full(expand), full(expand_t)],
        out_specs=[rowblk(SSM_INNER), rowblk(SSM_INNER), rowblk(512), rowblk(128),
                   acc(1, 1024), acc(1, 1024), acc(1, 128), acc(1, 128), acc(4, 1024), acc(1, 1024), acc(4, 512),
                   acc(1, 512)],
        out_shape=[SDS((t, SSM_INNER), BF16), SDS((t, SSM_INNER), BF16), SDS((t, 512), BF16), SDS((t, 128), BF16),
                   SDS((1, 1024), F32), SDS((1, 1024), F32), SDS((1, 128), F32), SDS((1, 128), F32),
                   SDS((4, 1024), F32), SDS((1, 1024), F32), SDS((4, 512), F32), SDS((1, 512), F32)],
        scratch_shapes=[pltpu.VMEM((128, SSM_INNER), F32), pltpu.VMEM((8, 1024), F32), pltpu.VMEM((8, 512), F32),
                        pltpu.VMEM((CHUNK, 512), F32), pltpu.VMEM((CHUNK, SSM_INNER), F32)],
        compiler_params=_params("arbitrary"))(proj, proj, proj, proj, proj, proj, y_ssd, hs, dout,
                                              cwx, cbx, cwb, cbb, dtb, alog, dsk_e, norm_g, tri, triu, expand,
                                              expand_t)


def _conv3(x, halo, w_ref, b_ref, part):
    acc = b_ref[part] + w_ref[2, part] * x
    for k in range(2):
        acc = acc + w_ref[k, part] * _shift_down(x, halo, 2 - k)
    return acc


def _up_act(x, g, w_up_t, cw, cb, tm=2048, tn=256):
    t, k = x.shape
    nj = D_FF // tn

    def body(x_ref, g_ref, wg_ref, wv_ref, w_ref, b_ref, u_ref, h_ref, f_ref, halo):
        i, j = pl.program_id(0), pl.program_id(1)

        @pl.when(j == 0)
        def _():
            xv = x_ref[...]
            r = lax.rsqrt(jnp.mean(xv * xv, axis=-1, keepdims=True) + EPS)
            h_ref[...] = (xv * r * g_ref[...]).astype(BF16)

        @pl.when(i == 0)
        def _():
            halo[j] = jnp.zeros((2, 8, tn), F32)

        parts = []
        for part, wt_ref in enumerate((wg_ref, wv_ref)):
            u = _dot(h_ref[...], wt_ref[...], NT)
            u_ref[part] = u
            parts.append(_conv3(u, halo[j, part], w_ref, b_ref, part))
            halo[j, part] = u[tm - 8:, :]
        gate, val = parts
        f_ref[...] = (gate * _sigmoid(gate) * val).astype(BF16)

    return pl.pallas_call(
        body, name="up_proj", grid=(t // tm, nj),
        in_specs=[pl.BlockSpec((tm, k), lambda i, j: (i, 0)), pl.BlockSpec((1, k), lambda i, j: (0, 0)),
                  pl.BlockSpec((tn, k), lambda i, j: (j, 0)), pl.BlockSpec((tn, k), lambda i, j: (j + nj, 0)),
                  pl.BlockSpec((3, 2, 1, tn), lambda i, j: (0, 0, 0, j)), pl.BlockSpec((2, 1, tn), lambda i, j: (0, 0, j))],
        out_specs=[pl.BlockSpec((2, tm, tn), lambda i, j: (0, i, j)), pl.BlockSpec((tm, k), lambda i, j: (i, 0)),
                   pl.BlockSpec((tm, tn), lambda i, j: (i, j))],
        out_shape=[SDS((2, t, D_FF), F32), SDS((t, k), BF16), SDS((t, D_FF), BF16)],
        scratch_shapes=[pltpu.VMEM((nj, 2, 8, tn), F32)],
        compiler_params=_params("arbitrary", "arbitrary"))(x, g, w_up_t, w_up_t, cw, cb)


def _ffn_bwd(dx2, w_down, u, cw, cb, tm=512, tn=1408):
    t = u.shape[1]
    nj, ni = D_FF // tn, t // tm
    rev = lambda i: ni - 1 - i

    def body(dx_ref, wd_ref, u_ref, uh_ref, w_ref, b_ref, du_ref, dcw_ref, dcb_ref, nxt):
        i = pl.program_id(1)

        @pl.when(i == 0)
        def _():
            nxt[...] = jnp.zeros_like(nxt)
            dcw_ref[...] = jnp.zeros_like(dcw_ref)
            dcb_ref[...] = jnp.zeros_like(dcb_ref)

        df = _dot(dx_ref[...].astype(BF16), wd_ref[...], NT)
        keep = (i < ni - 1).astype(F32)
        ug, uv = u_ref[0], u_ref[1]
        hg, hv = uh_ref[0] * keep, uh_ref[1] * keep
        gate = _conv3(ug, hg, w_ref, b_ref, 0)
        val = _conv3(uv, hv, w_ref, b_ref, 1)
        sg = _sigmoid(gate)
        dgate = df * val * (sg * (1.0 + gate * (1.0 - sg)))
        dval = df * (gate * sg)
        for part, (d, uu, hh) in enumerate(((dgate, ug, hg), (dval, uv, hv))):
            dcb_ref[part] += jnp.sum(d, axis=0, keepdims=True)
            ahead = nxt[part]
            acc = w_ref[2, part] * d
            dcw_ref[2, part] += jnp.sum(d * uu, axis=0, keepdims=True)
            for k in range(2):
                dcw_ref[k, part] += jnp.sum(d * _shift_down(uu, hh, 2 - k), axis=0, keepdims=True)
                acc = acc + w_ref[k, part] * _shift_up(d, ahead, 2 - k)
            nxt[part] = d[0:8, :]
            du_ref[part] = acc.astype(BF16)

    w_spec = pl.BlockSpec((3, 2, 1, tn), lambda j, i: (0, 0, 0, j))
    b_spec = pl.BlockSpec((2, 1, tn), lambda j, i: (0, 0, j))
    return pl.pallas_call(
        body, name="ffn_bwd", grid=(nj, ni),
        in_specs=[pl.BlockSpec((tm, D_MODEL), lambda j, i: (rev(i), 0)), pl.BlockSpec((tn, D_MODEL), lambda j, i: (j, 0)),
                  pl.BlockSpec((2, tm, tn), lambda j, i: (0, rev(i), j)),
                  pl.BlockSpec((2, 8, tn), lambda j, i: (0, jnp.maximum(rev(i) * (tm // 8) - 1, 0), j)),
                  w_spec, b_spec],
        out_specs=[pl.BlockSpec((2, tm, tn), lambda j, i: (0, rev(i), j)), w_spec, b_spec],
        out_shape=[SDS((2, t, D_FF), BF16), SDS((3, 2, 1, D_FF), F32), SDS((2, 1, D_FF), F32)],
        scratch_shapes=[pltpu.VMEM((2, 8, tn), F32)],
        compiler_params=_params("parallel", "arbitrary"))(dx2, w_down, u, u, cw, cb)


def _ple_loss(x2, g, w_gate, p, w_proj_t, target, tm=256):
    t = x2.shape[0]

    def body(x_ref, g_ref, wg_ref, p_ref, wp_ref, tg_ref, dx_ref, dpre_ref, dpp_ref, h_ref, loss_ref, dg_ref):
        i = pl.program_id(0)
        xv = x_ref[...]
        r = lax.rsqrt(jnp.mean(xv * xv, axis=-1, keepdims=True) + EPS)
        xh = xv * r
        h = (xh * g_ref[...]).astype(BF16)
        h_ref[...] = h
        gate = _sigmoid(_dot(h, wg_ref[...]))
        pp = _dot(p_ref[...].astype(BF16), wp_ref[...], NT)
        err = (xv + gate * pp) - tg_ref[...]

        @pl.when(i == 0)
        def _():
            loss_ref[...] = jnp.zeros_like(loss_ref)
            dg_ref[...] = jnp.zeros_like(dg_ref)

        loss_ref[...] += 0.5 * jnp.sum(jnp.mean(err * err, axis=-1, keepdims=True), axis=0, keepdims=True)
        dy = err * (1.0 / D_MODEL)
        dpre = (dy * pp * gate * (1.0 - gate)).astype(BF16)
        dpre_ref[...] = dpre
        dpp_ref[...] = (dy * gate).astype(BF16)
        dh = _dot(dpre, wg_ref[...], NT)
        dg_ref[...] += jnp.sum(dh * xh, axis=0, keepdims=True)
        gd = dh * g_ref[...]
        dx_ref[...] = dy + r * (gd - xh * jnp.mean(gd * xh, axis=-1, keepdims=True))

    row = lambda w: pl.BlockSpec((tm, w), lambda i: (i, 0))
    full = lambda a: pl.BlockSpec(a.shape, lambda i: (0, 0))
    return pl.pallas_call(
        body, name="ple_loss", grid=(t // tm,),
        in_specs=[row(D_MODEL), full(g), full(w_gate), row(PLE_DIM), full(w_proj_t), row(D_MODEL)],
        out_specs=[row(D_MODEL), row(D_MODEL), row(D_MODEL), row(D_MODEL),
                   pl.BlockSpec((1, 128), lambda i: (0, 0)), pl.BlockSpec((1, D_MODEL), lambda i: (0, 0))],
        out_shape=[SDS((t, D_MODEL), F32), SDS((t, D_MODEL), BF16), SDS((t, D_MODEL), BF16), SDS((t, D_MODEL), BF16),
                   SDS((1, 128), F32), SDS((1, D_MODEL), F32)],
        compiler_params=_params("arbitrary"))(x2, g, w_gate, p, w_proj_t, target)


def _exchange(scatter, gather, name):
    arrays = list(scatter) + list(gather)
    n_a, n_s = len(arrays), len(scatter)

    def body(*refs):
        src, dst = refs[:n_a], refs[n_a:2 * n_a]
        send_sems, recv_sems, local_sems = refs[2 * n_a:]
        x, y, c = lax.axis_index("x"), lax.axis_index("y"), lax.axis_index("c")
        me = 4 * x + 2 * y + c

        def src_of(a, slot):
            return src[a].at[slot] if a < n_s else src[a]

        local = [pltpu.make_async_copy(src_of(a, me), dst[a].at[me], local_sems.at[a]) for a in range(n_a)]
        for cp in local:
            cp.start()
        sends, peers = [], []
        for k in range(1, N_DEV):
            px = 1 - x if k & 4 else x
            py = 1 - y if k & 2 else y
            pc = 1 - c if k & 1 else c
            peer = 4 * px + 2 * py + pc
            peers.append(peer)
            for a in range(n_a):
                cp = pltpu.make_async_remote_copy(
                    src_ref=src_of(a, peer), dst_ref=dst[a].at[me], send_sem=send_sems.at[a, k - 1],
                    recv_sem=recv_sems.at[a, k - 1], device_id=(px, py, pc), device_id_type=pl.DeviceIdType.MESH)
                cp.start()
                sends.append(cp)
        for k in range(1, N_DEV):
            peer = peers[k - 1]
            for a in range(n_a):
                pltpu.make_async_remote_copy(
                    src_ref=src_of(a, peer), dst_ref=dst[a].at[peer], send_sem=send_sems.at[a, k - 1],
                    recv_sem=recv_sems.at[a, k - 1], device_id=(x, y, c),
                    device_id_type=pl.DeviceIdType.MESH).wait_recv()
        for cp in sends:
            cp.wait_send()
        for cp in local:
            cp.wait()

    out_shape = [SDS(a.shape, a.dtype) for a in scatter] + [SDS((N_DEV,) + a.shape, a.dtype) for a in gather]
    hbm = pl.BlockSpec(memory_space=pl.ANY)
    return pl.pallas_call(
        body, name=name, in_specs=[hbm] * n_a, out_specs=[hbm] * n_a, out_shape=out_shape,
        scratch_shapes=[pltpu.SemaphoreType.DMA((n_a, N_DEV - 1)), pltpu.SemaphoreType.DMA((n_a, N_DEV - 1)),
                        pltpu.SemaphoreType.DMA((n_a,))],
        )(*arrays)


def _peer(k):
    x, y, c = lax.axis_index("x"), lax.axis_index("y"), lax.axis_index("c")
    px = 1 - x if k & 4 else x
    py = 1 - y if k & 2 else y
    pc = 1 - c if k & 1 else c
    return (px, py, pc), 4 * px + 2 * py + pc


_HBM = pl.BlockSpec(memory_space=pltpu.HBM)
_SEM = pl.BlockSpec(memory_space=pltpu.SEMAPHORE)


def _split_copies(src, land, send_sems, recv_sems, scatter, arrivals):
    _, me = _peer(0)
    out = []
    for k in range(1, N_DEV):
        coords, peer = _peer(k)
        for a in range(len(src)):
            sem = a * (N_DEV - 1) + k - 1
            if scatter[a]:
                s, d = src[a].at[peer], land[a].at[k]
            else:
                s, d = src[a], land[a].at[peer if arrivals else me]
            out.append(pltpu.make_async_remote_copy(
                src_ref=s, dst_ref=d, send_sem=send_sems.at[sem], recv_sem=recv_sems.at[sem], device_id=coords,
                device_id_type=pl.DeviceIdType.MESH))
    return out


def _exchange_start(srcs, lands, scatter, name):
    n = len(srcs)

    def body(*refs):
        src, land = refs[:n], refs[n:2 * n]
        send_sems, recv_sems = refs[2 * n], refs[2 * n + 1]
        token = refs[-1]
        for cp in _split_copies(src, land, send_sems, recv_sems, scatter, False):
            cp.start()
        token[...] = jnp.zeros_like(token)

    hbm_shape = lambda a: pltpu.HBM(a.shape, a.dtype)
    sem_shape = pltpu.SemaphoreType.DMA((n * (N_DEV - 1),))
    outs = pl.pallas_call(
        body, name=name,
        out_shape=(sem_shape, sem_shape, *[hbm_shape(a) for a in srcs], *[hbm_shape(a) for a in lands],
                   SDS((8, 128), F32)),
        in_specs=[_HBM] * (2 * n), out_specs=(_SEM, _SEM, *[_HBM] * (2 * n), pl.BlockSpec(memory_space=pltpu.VMEM)),
        input_output_aliases={a: 2 + a for a in range(2 * n)},
        compiler_params=pltpu.CompilerParams(has_side_effects=pltpu.SideEffectType.DATAFLOW_SIDE_EFFECTING),
    )(*[pltpu.with_memory_space_constraint(a, pltpu.HBM) for a in list(srcs) + list(lands)])
    return outs[0], outs[1], outs[2:2 + n], outs[2 + n:2 + 2 * n], outs[-1]


def _exchange_wait(send_sems, recv_sems, srcs, lands, scatter, after, name):
    n = len(srcs)

    def body(*refs):
        src, land = refs[:n], refs[n:2 * n]
        for cp in _split_copies(src, land, refs[2 * n], refs[2 * n + 1], scatter, False):
            cp.wait_send()
        for cp in _split_copies(src, land, refs[2 * n], refs[2 * n + 1], scatter, True):
            cp.wait_recv()

    hbm_shape = lambda a: pltpu.HBM(a.shape, a.dtype)
    outs = pl.pallas_call(
        body, name=name, out_shape=tuple(hbm_shape(a) for a in list(srcs) + list(lands)),
        in_specs=[_HBM] * (2 * n) + [_SEM, _SEM, pl.BlockSpec(memory_space=pl.ANY)], out_specs=(_HBM,) * (2 * n),
        input_output_aliases={a: a for a in range(2 * n)},
        compiler_params=pltpu.CompilerParams(has_side_effects=pltpu.SideEffectType.DATAFLOW_SIDE_EFFECTING),
    )(*srcs, *lands, send_sems, recv_sems, after)
    return outs[:n], outs[n:]


def _reduce8(a, tr, name):
    _, rows, cols = a.shape

    def body(a_ref, o_ref):
        acc = a_ref[0]
        for j in range(1, N_DEV):
            acc = acc + a_ref[j]
        o_ref[...] = acc

    return pl.pallas_call(
        body, name=name, grid=(rows // tr,),
        in_specs=[pl.BlockSpec((N_DEV, tr, cols), lambda i: (0, i, 0))],
        out_specs=pl.BlockSpec((tr, cols), lambda i: (i, 0)), out_shape=SDS((rows, cols), F32),
        compiler_params=_params("parallel"))(a)


def _reduce_landed(own, land, name, tc=256):
    rows, cols = own.shape

    def body(own_ref, land_ref, o_ref):
        acc = own_ref[...]
        for k in range(1, N_DEV):
            acc = acc + land_ref[k].astype(F32)
        o_ref[...] = acc

    return pl.pallas_call(
        body, name=name, grid=(cols // tc,),
        in_specs=[pl.BlockSpec((rows, tc), lambda j: (0, j)), pl.BlockSpec((N_DEV, rows, tc), lambda j: (0, 0, j))],
        out_specs=pl.BlockSpec((rows, tc), lambda j: (0, j)), out_shape=SDS((rows, cols), F32),
        compiler_params=_params("parallel"))(own, land)


def _adamw(w, g, m, v, name, tr=None):
    rows, cols = w.shape
    tr = rows if tr is None else tr
    c1 = 1.0 - ADAM_B1 ** ADAM_STEP
    c2 = 1.0 - ADAM_B2 ** ADAM_STEP

    def body(w_ref, g_ref, m_ref, v_ref, d_ref, mo_ref, vo_ref):
        gv = g_ref[...]
        mn = ADAM_B1 * m_ref[...] + (1.0 - ADAM_B1) * gv
        vn = ADAM_B2 * v_ref[...] + (1.0 - ADAM_B2) * (gv * gv)
        mo_ref[...] = mn
        vo_ref[...] = vn
        d_ref[...] = -ADAM_LR * ((mn / c1) / (jnp.sqrt(vn / c2) + ADAM_EPS) + ADAM_WD * w_ref[...])

    blk = pl.BlockSpec((tr, cols), lambda i: (i, 0))
    return pl.pallas_call(
        body, name=name, grid=(rows // tr,), in_specs=[blk] * 4, out_specs=[blk] * 3,
        out_shape=[SDS((rows, cols), F32)] * 3, compiler_params=_params("parallel"))(w, g, m, v)


def _pad_rows(a, rows):
    return jnp.pad(a, ((0, rows - a.shape[0]),) + ((0, 0),) * (a.ndim - 1))


def _local_step(x, p, target, sm, wts, fetch_rest, send, tok):
    ones_q, ones_k, dup, dup_t = _head_consts()
    tri, triu, expand, expand_t = _ssd_consts()
    w_in_t = wts["in_t"]
    cwx, cwb = wts["ssm_cw"][:, :SSM_INNER], wts["ssm_cw"][:, SSM_INNER:]
    cbx, cbb = sm["ssm_conv_b"][:, :SSM_INNER], sm["ssm_conv_b"][:, SSM_INNER:]
    pad128 = lambda a: jnp.pad(a, ((0, 0), (0, 128 - a.shape[1])))
    dtb, alog = pad128(sm["dt_bias"]), pad128(sm["a_log"])
    dsk_e = jnp.repeat(sm["d_skip"], HEAD_DIM, axis=1)
    gq = jnp.tile(sm["q_norm_g"], (1, ATTN_DIM // HEAD_DIM))
    gk = jnp.tile(sm["k_norm_g"], (1, KV_DIM // HEAD_DIM))
    ffn_cw = wts["ffn_cw"].reshape(3, 2, 1, D_FF)
    ffn_cb = sm["ffn_conv_b"].reshape(2, 1, D_FF)

    proj, h1 = _norm_matmul(x, sm["attn_norm_g"] + tok, w_in_t, 1024, 768, "in_proj")
    qn, kd, vd = _attn_prep(proj, gq, gk, ones_q, ones_k, dup)
    attn_out, lse = _attn_fwd(qn, kd, vd)
    y_ssd, hs, ssm_out = _ssd_fwd(proj, cwx, cbx, cwb, cbb, dtb, alog, dsk_e, sm["ssm_norm_g"], tri, expand)
    rest = fetch_rest(ssm_out)
    w_out, w_up_t, w_down, w_gate, w_proj_t = (rest[k] for k in ("out", "up_t", "down", "gate", "proj_t"))
    x1 = _mm_resid([(attn_out, None, w_out[:ATTN_DIM]), (ssm_out, None, w_out[ATTN_DIM:])], x, None, 512, F32,
                   "out_proj")
    u, h2, f = _up_act(x1, sm["ffn_norm_g"], w_up_t, ffn_cw, ffn_cb)
    x2 =_mm_resid([(f, None, w_down)], x1, None, 512, F32, "down_proj")
    dx2, dpre, dpp, h3, loss, dg_ple = _ple_loss(x2, sm["ple_norm_g"], w_gate, p, w_proj_t, target)

    g_gate = _wgrad(h3, None, dpre, "wg_gate")
    g_proj_t = _wgrad(dpp, None, p, "wg_proj")
    g_down = _wgrad(f, None, dx2, "wg_down")
    du, d_ffn_cw, d_ffn_cb = _ffn_bwd(dx2, w_down, u, ffn_cw, ffn_cb)
    dx1, dg_ffn = _mm_normbwd([(du, 0, w_up_t[:D_FF]), (du, 1, w_up_t[D_FF:])], x1, sm["ffn_norm_g"], dx2, 256,
                              "up_proj_bwd")
    g_up_t = jnp.concatenate([_wgrad(du, 0, h2, "wg_up_gate"), _wgrad(du, 1, h2, "wg_up_val")], axis=0)
    tok = send(dict(gate=g_gate, proj_t=g_proj_t, down=g_down, up_t=g_up_t)).astype(BF16)
    d_attn = _mm_resid([(dx1, None, w_out[:ATTN_DIM] + tok)], None, NT, 512, F32, "out_proj_bwd_attn")
    d_ssm = _mm_resid([(dx1, None, w_out[ATTN_DIM:] + tok)], None, NT, 512, F32, "out_proj_bwd_ssm")
    g_out = jnp.concatenate([_wgrad(attn_out, None, dx1, "wg_out_attn"), _wgrad(ssm_out, None, dx1, "wg_out_ssm")],
                            axis=0)
    tok = send(dict(out=g_out))
    (dz, dxs, dbc, ddt, dg_ssm, d_dsk_e, d_alog, d_dtb, d_cwx, d_cbx, d_cwb, d_cbb) = _ssd_bwd(
        proj, y_ssd, hs, d_ssm, cwx, cbx, cwb, cbb, dtb + tok, alog, dsk_e, sm["ssm_norm_g"], tri, triu, expand,
        expand_t)
    dqn, dkc, dkp, dvc, dvp = _attn_bwd(qn, kd, vd, attn_out, lse, d_attn)
    dqkv, dgq, dgk = _attn_prep_bwd(proj, dqn, dkc, dkp, dvc, dvp, gq + tok, gk, ones_q, ones_k, dup_t)
    pieces = [(dqkv, 0, 1024), (dz, 1024, 2048), (dxs, 2048, 3072), (dbc, 3072, 3584), (ddt, 3584, 3712)]
    g_in_t = jnp.concatenate([_wgrad(a, None, h1, "wg_in_%d" % lo) for a, lo, _ in pieces], axis=0)[:IN_PROJ]
    tok = send(dict(in_t=g_in_t))
    grad_x, dg_attn = _mm_normbwd([(a, None, w_in_t[lo:hi]) for a, lo, hi in pieces], x, sm["attn_norm_g"] + tok, dx1,
                                  256, "in_proj_bwd")

    small = dict(
        attn_norm_g=dg_attn, q_norm_g=dgq.reshape(-1, HEAD_DIM).sum(0, keepdims=True),
        k_norm_g=dgk.reshape(-1, HEAD_DIM).sum(0, keepdims=True),
        ssm_conv_w=jnp.concatenate([d_cwx, d_cwb], axis=1), ssm_conv_b=jnp.concatenate([d_cbx, d_cbb], axis=1),
        dt_bias=d_dtb[:, :SSM_HEADS], a_log=d_alog[:, :SSM_HEADS],
        d_skip=d_dsk_e.reshape(SSM_HEADS, HEAD_DIM).sum(1)[None, :], ssm_norm_g=dg_ssm, ffn_norm_g=dg_ffn,
        ffn_conv_w=d_ffn_cw.reshape(3, 2 * D_FF), ffn_conv_b=d_ffn_cb.reshape(1, 2 * D_FF), ple_norm_g=dg_ple)
    return loss[0, 0], grad_x, small


_SMALL = (("attn_norm_g", 1024), ("q_norm_g", 64), ("k_norm_g", 64), ("ssm_conv_w", 4 * XBC_DIM),
          ("ssm_conv_b", XBC_DIM), ("dt_bias", 16), ("a_log", 16), ("d_skip", 16), ("ssm_norm_g", 1024),
          ("ffn_norm_g", 1024), ("ffn_conv_w", 3 * 2 * D_FF), ("ffn_conv_b", 2 * D_FF), ("ple_norm_g", 1024))
_SMALL_ROWS = 34
_SHARD_SMALL = (("attn_norm_g", 1024), ("q_norm_g", 64), ("k_norm_g", 64), ("ssm_conv_w", 4 * XBC_DIM // N_DEV),
                ("ssm_conv_b", XBC_DIM), ("dt_bias", 16), ("a_log", 16), ("d_skip", 16), ("ssm_norm_g", 1024),
                ("ffn_norm_g", 1024), ("ffn_conv_w", 3 * 2 * D_FF // N_DEV), ("ffn_conv_b", 2 * D_FF),
                ("ple_norm_g", 1024))
_SHARD_SMALL_ROWS = 14


def _pack_flat(parts, order, rows):
    flat = jnp.concatenate([parts[name].reshape(-1) for name, _ in order])
    return jnp.pad(flat, (0, rows * 1024 - flat.shape[0])).reshape(rows, 1024)


def _unpack_flat(packed, order):
    flat, out, pos = packed.reshape(-1), {}, 0
    for name, size in order:
        out[name] = flat[pos:pos + size]
        pos += size
    return out


def kernel(x, p, attn_norm_g, w_in, q_norm_g, k_norm_g, ssm_conv_w, ssm_conv_b, dt_bias, a_log, d_skip, ssm_norm_g, w_out, ffn_norm_g, w_up, ffn_conv_w, ffn_conv_b, w_down, ple_norm_g, w_ple_gate, w_ple_proj, loss_target, m_attn_norm_g, m_w_in, m_q_norm_g, m_k_norm_g, m_ssm_conv_w, m_ssm_conv_b, m_dt_bias, m_a_log, m_d_skip, m_ssm_norm_g, m_w_out, m_ffn_norm_g, m_w_up, m_ffn_conv_w, m_ffn_conv_b, m_w_down, m_ple_norm_g, m_w_ple_gate, m_w_ple_proj, v_attn_norm_g, v_w_in, v_q_norm_g, v_k_norm_g, v_ssm_conv_w, v_ssm_conv_b, v_dt_bias, v_a_log, v_d_skip, v_ssm_norm_g, v_w_out, v_ffn_norm_g, v_w_up, v_ffn_conv_w, v_ffn_conv_b, v_w_down, v_ple_norm_g, v_w_ple_gate, v_w_ple_proj):
    names = ("attn_norm_g", "w_in", "q_norm_g", "k_norm_g", "ssm_conv_w", "ssm_conv_b", "dt_bias", "a_log", "d_skip",
             "ssm_norm_g", "w_out", "ffn_norm_g", "w_up", "ffn_conv_w", "ffn_conv_b", "w_down", "ple_norm_g",
             "w_ple_gate", "w_ple_proj")
    w = dict(zip(names, (attn_norm_g, w_in, q_norm_g, k_norm_g, ssm_conv_w, ssm_conv_b, dt_bias, a_log, d_skip,
                         ssm_norm_g, w_out, ffn_norm_g, w_up, ffn_conv_w, ffn_conv_b, w_down, ple_norm_g, w_ple_gate,
                         w_ple_proj)))
    m = dict(zip(names, (m_attn_norm_g, m_w_in, m_q_norm_g, m_k_norm_g, m_ssm_conv_w, m_ssm_conv_b, m_dt_bias,
                         m_a_log, m_d_skip, m_ssm_norm_g, m_w_out, m_ffn_norm_g, m_w_up, m_ffn_conv_w, m_ffn_conv_b,
                         m_w_down, m_ple_norm_g, m_w_ple_gate, m_w_ple_proj)))
    v = dict(zip(names, (v_attn_norm_g, v_w_in, v_q_norm_g, v_k_norm_g, v_ssm_conv_w, v_ssm_conv_b, v_dt_bias,
                         v_a_log, v_d_skip, v_ssm_norm_g, v_w_out, v_ffn_norm_g, v_w_up, v_ffn_conv_w, v_ffn_conv_b,
                         v_w_down, v_ple_norm_g, v_w_ple_gate, v_w_ple_proj)))
    w, m, v = ({k: a[0] for k, a in d.items()} for d in (w, m, v))
    me = 4 * lax.axis_index("x") + 2 * lax.axis_index("y") + lax.axis_index("c")

    mine = dict(in_t=w["w_in"].T, out=w["w_out"], up_t=w["w_up"].T, down=w["w_down"], gate=w["w_ple_gate"],
                proj_t=w["w_ple_proj"].T)
    mine = {k: a.astype(BF16) for k, a in mine.items()}
    conv_pack = jnp.pad(jnp.concatenate([w["ssm_conv_w"].reshape(-1), w["ffn_conv_w"].reshape(-1)]),
                        (0, 3072 - 2880)).reshape(8, 384)
    all_in, all_conv = _exchange([], [mine["in_t"], conv_pack], "gather_first")
    later = ("out", "up_t", "down", "gate", "proj_t")
    zones = [lax.dynamic_update_slice(lax.empty((N_DEV,) + mine[k].shape, BF16), mine[k][None], (me, 0, 0))
             for k in later]
    zones, all_in, all_conv = lax.optimization_barrier((zones, all_in, all_conv))
    rest_state = _exchange_start([mine[k] for k in later], zones, [False] * len(later), "gather_rest_start")

    def fetch_rest(after):
        _, landed = _exchange_wait(*rest_state[:4], [False] * len(later), after, "gather_rest_wait")
        return {k: a.reshape(N_DEV * a.shape[1], a.shape[2]) for k, a in zip(later, landed)}

    wts = dict(in_t=_pad_rows(all_in.reshape(IN_PROJ, D_MODEL), IN_PROJ_PAD))
    conv_flat = all_conv.reshape(N_DEV, 3072)
    wts["ssm_cw"] = conv_flat[:, :768].reshape(N_DEV, 4, XBC_DIM // N_DEV).transpose(1, 0, 2).reshape(4, XBC_DIM)
    wts["ffn_cw"] = conv_flat[:, 768:2880].reshape(N_DEV, 3, 2 * D_FF // N_DEV).transpose(1, 0, 2).reshape(3, 2 * D_FF)
    sm = {k: w[k].reshape(1, -1) for k, _ in _SMALL if k not in ("ssm_conv_w", "ffn_conv_w")}

    in_flight = []

    def send(grads):
        keys = sorted(grads)
        blocks = [grads[k].reshape(N_DEV, grads[k].shape[0] // N_DEV, grads[k].shape[1]) for k in keys]
        own = [lax.dynamic_index_in_dim(a, me, 0, keepdims=False) for a in blocks]
        srcs = [a.astype(BF16) for a in blocks]
        state = _exchange_start(srcs, [lax.empty(a.shape, BF16) for a in srcs], [True] * len(keys),
                                "send_" + "_".join(keys))
        in_flight.append((keys, state, own))
        return state[4][0:1, 0:1]

    loss, grad_x, small = _local_step(x[0], p[0, 0], loss_target[0], sm, wts, fetch_rest, send,
                                      rest_state[4][0:1, 0:1])
    loss = lax.psum(loss, ("x", "y", "c"))

    (got_small,) = _exchange([], [_pack_flat(small, _SMALL, _SMALL_ROWS)], "gather_small_grads")
    g_small = _unpack_flat(_reduce8(got_small, _SMALL_ROWS, "reduce_small"), _SMALL)
    grads = {}
    for keys, state, own in in_flight:
        _, landed = _exchange_wait(*state[:4], [True] * len(keys), grad_x, "wait_" + "_".join(keys))
        for k, mine_k, land in zip(keys, own, landed):
            grads[k] = _reduce_landed(mine_k, land, "reduce_" + k)
    gw = {"w_in": grads["in_t"].T, "w_out": grads["out"], "w_up": grads["up_t"].T, "w_down": grads["down"],
          "w_ple_gate": grads["gate"], "w_ple_proj": grads["proj_t"].T}
    for k, size in _SMALL:
        gw[k] = g_small[k].reshape(w[k].shape) if k not in ("ssm_conv_w", "ffn_conv_w") else None
    n_ssm, n_ffn = XBC_DIM // N_DEV, 2 * D_FF // N_DEV
    gw["ssm_conv_w"] = lax.dynamic_slice(g_small["ssm_conv_w"].reshape(4, XBC_DIM), (0, me * n_ssm), (4, n_ssm))
    gw["ffn_conv_w"] = lax.dynamic_slice(g_small["ffn_conv_w"].reshape(3, 2 * D_FF), (0, me * n_ffn), (3, n_ffn))

    delta, new_m, new_v = {}, {}, {}
    for k, tr in (("w_in", 256), ("w_out", None), ("w_up", 256), ("w_down", None), ("w_ple_gate", None),
                  ("w_ple_proj", None)):
        delta[k], new_m[k], new_v[k] = _adamw(w[k], gw[k], m[k], v[k], "adamw_" + k, tr)
    packs = [_pack_flat(d, _SHARD_SMALL, _SHARD_SMALL_ROWS) for d in (w, gw, m, v)]
    for d, packed in zip((delta, new_m, new_v), _adamw(*packs, "adamw_small")):
        for k, a in _unpack_flat(packed, _SHARD_SMALL).items():
            d[k] = a.reshape(w[k].shape)

    outs = [loss, grad_x[None]]
    for d in (gw, delta, new_m, new_v):
        outs += [d[k][None] for k in names]
    return tuple(outs)
```

```python
import functools

import numpy as np
import jax
import jax.numpy as jnp
from jax import lax
from jax.experimental import pallas as pl
from jax.experimental.pallas import tpu as pltpu

F32 = jnp.float32
BF16 = jnp.bfloat16
SDS = jax.ShapeDtypeStruct
EPS = 1e-6
N_DEV = 8
D_MODEL = 1024
HEAD_DIM = 64
ATTN_DIM = 512
KV_DIM = 256
SSM_INNER = 1024
SSM_HEADS = 16
BC_DIM = 256
XBC_DIM = SSM_INNER + 2 * BC_DIM
MIX_DIM = ATTN_DIM + SSM_INNER
IN_PROJ = 3600
IN_PROJ_PAD = 3840
D_FF = 2816
PLE_DIM = 256
CHUNK = 128
SUPER = 2048
DILATIONS = (1, 4, 16)
TILE_UNROLL = 8
VMEM_LIMIT = 56 * 1024 * 1024
ADAM_LR, ADAM_B1, ADAM_B2, ADAM_EPS, ADAM_WD, ADAM_STEP = 0.001, 0.9, 0.999, 1e-08, 0.01, 10

NT = (((1,), (1,)), ((), ()))
TN = (((0,), (0,)), ((), ()))


def _params(*sem):
    return pltpu.CompilerParams(dimension_semantics=sem if sem else None, vmem_limit_bytes=VMEM_LIMIT)


def _dot(a, b, dims=None):
    if dims is None:
        return jnp.dot(a, b, preferred_element_type=F32)
    return lax.dot_general(a, b, dims, preferred_element_type=F32)


def _hdot(a, b, parts=2):
    a_exact = a.dtype == BF16
    x = b if a_exact else a
    acc = None
    for _ in range(parts):
        piece = x.astype(BF16)
        x = x - piece.astype(F32)
        d = _dot(a, piece) if a_exact else _dot(piece, b)
        acc = d if acc is None else acc + d
    return acc


def _sigmoid(x):
    return 0.5 * jnp.tanh(0.5 * x) + 0.5


def _shift_down(x, halo8, s):
    xr = pltpu.roll(x, s, 0)
    row = lax.broadcasted_iota(jnp.int32, halo8.shape, 0)
    first = jnp.where(row < s, pltpu.roll(halo8, s, 0), xr[0:8])
    return jnp.concatenate([first, xr[8:]], axis=0)


def _shift_up(x, halo8, s):
    n = x.shape[0]
    xr = pltpu.roll(x, n - s, 0)
    row = lax.broadcasted_iota(jnp.int32, halo8.shape, 0)
    last = jnp.where(row >= 8 - s, pltpu.roll(halo8, 8 - s, 0), xr[n - 8:])
    return jnp.concatenate([xr[:n - 8], last], axis=0)


def _norm_matmul(x, g, wt, tm, tn, name):
    t, k = x.shape
    n = wt.shape[0]

    def body(x_ref, g_ref, w_ref, o_ref, h_ref):
        @pl.when(pl.program_id(1) == 0)
        def _():
            xv = x_ref[...]
            r = lax.rsqrt(jnp.mean(xv * xv, axis=-1, keepdims=True) + EPS)
            h_ref[...] = (xv * r * g_ref[...]).astype(BF16)
        o_ref[...] = _dot(h_ref[...], w_ref[...], NT)

    return pl.pallas_call(
        body, name=name, grid=(t // tm, n // tn),
        in_specs=[pl.BlockSpec((tm, k), lambda i, j: (i, 0)), pl.BlockSpec((1, k), lambda i, j: (0, 0)),
                  pl.BlockSpec((tn, k), lambda i, j: (j, 0))],
        out_specs=[pl.BlockSpec((tm, tn), lambda i, j: (i, j)), pl.BlockSpec((tm, k), lambda i, j: (i, 0))],
        out_shape=[SDS((t, n), F32), SDS((t, k), BF16)],
        compiler_params=_params("parallel", "arbitrary"))(x, g, wt)


def _a_spec(a, lead, tm):
    if lead is None:
        return pl.BlockSpec((tm, a.shape[-1]), lambda i: (i, 0))
    return pl.BlockSpec((None, tm, a.shape[-1]), lambda i, _l=lead: (_l, i, 0))


def _mm_resid(pairs, res, dims, tm, out_dtype, name):
    t = pairs[0][0].shape[-2]
    n = pairs[0][2].shape[1] if dims is None else pairs[0][2].shape[0]
    np_ = len(pairs)

    def body(*refs):
        o_ref = refs[-1]
        acc = refs[2 * np_][...] if res is not None else None
        for q in range(np_):
            d = _dot(refs[q][...].astype(BF16), refs[np_ + q][...], dims)
            acc = d if acc is None else acc + d
        o_ref[...] = acc.astype(out_dtype)

    in_specs = [_a_spec(a, lead, tm) for a, lead, _ in pairs]
    in_specs += [pl.BlockSpec(b.shape, lambda i: (0, 0)) for _, _, b in pairs]
    args = [a for a, _, _ in pairs] + [b for _, _, b in pairs]
    if res is not None:
        in_specs.append(pl.BlockSpec((tm, n), lambda i: (i, 0)))
        args.append(res)
    return pl.pallas_call(
        body, name=name, grid=(t // tm,), in_specs=in_specs,
        out_specs=pl.BlockSpec((tm, n), lambda i: (i, 0)), out_shape=SDS((t, n), out_dtype),
        compiler_params=_params("parallel"))(*args)


def _mm_normbwd(pairs, x, g, dres, tm, name):
    t, k = x.shape
    np_ = len(pairs)

    def body(*refs):
        x_ref, g_ref, dres_ref, dx_ref, dg_ref = refs[2 * np_:]
        dh = None
        for q in range(np_):
            d = _dot(refs[q][...], refs[np_ + q][...])
            dh = d if dh is None else dh + d
        xv = x_ref[...]
        r = lax.rsqrt(jnp.mean(xv * xv, axis=-1, keepdims=True) + EPS)
        xh = xv * r

        @pl.when(pl.program_id(0) == 0)
        def _():
            dg_ref[...] = jnp.zeros_like(dg_ref)
        dg_ref[...] += jnp.sum(dh * xh, axis=0, keepdims=True)
        gd = dh * g_ref[...]
        dx_ref[...] = dres_ref[...] + r * (gd - xh * jnp.mean(gd * xh, axis=-1, keepdims=True))

    in_specs = [_a_spec(a, lead, tm) for a, lead, _ in pairs]
    in_specs += [pl.BlockSpec(b.shape, lambda i: (0, 0)) for _, _, b in pairs]
    in_specs += [pl.BlockSpec((tm, k), lambda i: (i, 0)), pl.BlockSpec((1, k), lambda i: (0, 0)),
                 pl.BlockSpec((tm, k), lambda i: (i, 0))]
    args = [a for a, _, _ in pairs] + [b for _, _, b in pairs] + [x, g, dres]
    return pl.pallas_call(
        body, name=name, grid=(t // tm,), in_specs=in_specs,
        out_specs=[pl.BlockSpec((tm, k), lambda i: (i, 0)), pl.BlockSpec((1, k), lambda i: (0, 0))],
        out_shape=[SDS((t, k), F32), SDS((1, k), F32)],
        compiler_params=_params("arbitrary"))(*args)


def _wgrad(a, a_lead, b, name, tk=512):
    t, m = a.shape[-2:]
    n = b.shape[1]
    tm = m if m <= 1024 else 1408
    assert m % tm == 0

    def body(a_ref, b_ref, o_ref):
        @pl.when(pl.program_id(1) == 0)
        def _():
            o_ref[...] = jnp.zeros_like(o_ref)
        o_ref[...] += _dot(a_ref[...].astype(BF16), b_ref[...].astype(BF16), TN)

    if a_lead is None:
        a_spec = pl.BlockSpec((tk, tm), lambda mi, ki: (ki, mi))
    else:
        a_spec = pl.BlockSpec((None, tk, tm), lambda mi, ki, _l=a_lead: (_l, ki, mi))
    return pl.pallas_call(
        body, name=name, grid=(m // tm, t // tk),
        in_specs=[a_spec, pl.BlockSpec((tk, n), lambda mi, ki: (ki, 0))],
        out_specs=pl.BlockSpec((tm, n), lambda mi, ki: (mi, 0)), out_shape=SDS((m, n), F32),
        compiler_params=_params("parallel", "arbitrary"))(a, b)


def _head_consts():
    iq = np.arange(ATTN_DIM)
    ik = np.arange(KV_DIM)
    ones_q = (iq[:, None] // HEAD_DIM == iq[None, :] // HEAD_DIM).astype(np.float32)
    ones_k = (ik[:, None] // HEAD_DIM == ik[None, :] // HEAD_DIM).astype(np.float32)
    dup = (ik[:, None] == (HEAD_DIM * (iq[None, :] // 128) + iq[None, :] % HEAD_DIM)).astype(np.float32)
    return jnp.asarray(ones_q, BF16), jnp.asarray(ones_k, BF16), jnp.asarray(dup, BF16), jnp.asarray(dup.T, BF16)


def _attn_prep(proj, gq, gk, ones_q, ones_k, dup, tm=512):
    t = proj.shape[0]

    def body(p_ref, gq_ref, gk_ref, oq_ref, ok_ref, dup_ref, qn_ref, kd_ref, vd_ref):
        q = p_ref[:, 0:ATTN_DIM]
        k = p_ref[:, ATTN_DIM:ATTN_DIM + KV_DIM]
        v = p_ref[:, ATTN_DIM + KV_DIM:]
        rq = lax.rsqrt(_hdot(q * q, oq_ref[...]) * (1.0 / HEAD_DIM) + EPS)
        qn_ref[...] = (q * rq * gq_ref[...]) * (HEAD_DIM ** -0.5)
        rk = lax.rsqrt(_hdot(k * k, ok_ref[...]) * (1.0 / HEAD_DIM) + EPS)
        kn = k * rk * gk_ref[...]
        kd_ref[...] = _dot(kn.astype(BF16), dup_ref[...])
        vd_ref[...] = _dot(v.astype(BF16), dup_ref[...])

    full = lambda a: pl.BlockSpec(a.shape, lambda i: (0, 0))
    o_spec = pl.BlockSpec((tm, ATTN_DIM), lambda i: (i, 0))
    return pl.pallas_call(
        body, name="attn_prep", grid=(t // tm,),
        in_specs=[pl.BlockSpec((tm, 1024), lambda i: (i, 0)), full(gq), full(gk), full(ones_q), full(ones_k), full(dup)],
        out_specs=[o_spec, o_spec, o_spec], out_shape=[SDS((t, ATTN_DIM), F32)] * 3,
        compiler_params=_params("parallel"))(proj, gq, gk, ones_q, ones_k, dup)


def _attn_prep_bwd(proj, dqn, dkc, dkp, dvc, dvp, gq, gk, ones_q, ones_k, dup_t, tm=512):
    t = proj.shape[0]
    nblk = t // tm
    off = SUPER // tm

    def body(p_ref, dqn_ref, dkc_ref, dkp_ref, dvc_ref, dvp_ref, gq_ref, gk_ref, oq_ref, ok_ref, dt_ref,
             o_ref, dgq_ref, dgk_ref):
        i = pl.program_id(0)
        has_next = (i + off < nblk).astype(F32)
        q = p_ref[:, 0:ATTN_DIM]
        k = p_ref[:, ATTN_DIM:ATTN_DIM + KV_DIM]
        dkn = _hdot(dkc_ref[...] + has_next * dkp_ref[...], dt_ref[...])
        dv = _hdot(dvc_ref[...] + has_next * dvp_ref[...], dt_ref[...])

        @pl.when(i == 0)
        def _():
            dgq_ref[...] = jnp.zeros_like(dgq_ref)
            dgk_ref[...] = jnp.zeros_like(dgk_ref)

        rq = lax.rsqrt(_hdot(q * q, oq_ref[...]) * (1.0 / HEAD_DIM) + EPS)
        xh = q * rq
        dy = dqn_ref[...] * (HEAD_DIM ** -0.5)
        dgq_ref[...] += jnp.sum(dy * xh, axis=0, keepdims=True)
        gd = dy * gq_ref[...]
        dq = rq * (gd - xh * (_hdot(gd * xh, oq_ref[...]) * (1.0 / HEAD_DIM)))
        rk = lax.rsqrt(_hdot(k * k, ok_ref[...]) * (1.0 / HEAD_DIM) + EPS)
        kh = k * rk
        dgk_ref[...] += jnp.sum(dkn * kh, axis=0, keepdims=True)
        gdk = dkn * gk_ref[...]
        dk = rk * (gdk - kh * (_hdot(gdk * kh, ok_ref[...]) * (1.0 / HEAD_DIM)))
        o_ref[:, 0:ATTN_DIM] = dq.astype(BF16)
        o_ref[:, ATTN_DIM:ATTN_DIM + KV_DIM] = dk.astype(BF16)
        o_ref[:, ATTN_DIM + KV_DIM:] = dv.astype(BF16)

    full = lambda a: pl.BlockSpec(a.shape, lambda i: (0, 0))
    cur = pl.BlockSpec((tm, ATTN_DIM), lambda i: (i, 0))
    nxt = pl.BlockSpec((tm, ATTN_DIM), lambda i: (jnp.minimum(i + off, nblk - 1), 0))
    return pl.pallas_call(
        body, name="attn_prep_bwd", grid=(nblk,),
        in_specs=[pl.BlockSpec((tm, 1024), lambda i: (i, 0)), cur, cur, nxt, cur, nxt,
                  full(gq), full(gk), full(ones_q), full(ones_k), full(dup_t)],
        out_specs=[pl.BlockSpec((tm, 1024), lambda i: (i, 0)), pl.BlockSpec((1, ATTN_DIM), lambda i: (0, 0)),
                   pl.BlockSpec((1, KV_DIM), lambda i: (0, 0))],
        out_shape=[SDS((t, 1024), BF16), SDS((1, ATTN_DIM), F32), SDS((1, KV_DIM), F32)],
        compiler_params=_params("arbitrary"))(proj, dqn, dkc, dkp, dvc, dvp, gq, gk, ones_q, ones_k, dup_t)


def _tile_masks():
    qi = lax.broadcasted_iota(jnp.int32, (2 * CHUNK, 2 * CHUNK), 0) & (CHUNK - 1)
    kj = lax.broadcasted_iota(jnp.int32, (2 * CHUNK, 2 * CHUNK), 1)
    delta = CHUNK + qi - kj
    band = (delta >= 0) & (delta <= CHUNK)
    return band, kj


def _deinterleave(dst, src, n_rows, d):
    per = n_rows // d
    for r in range(d):
        dst[r * per:(r + 1) * per, :] = src[pl.ds(r, per, stride=d), :]


def _attn_specs(t):
    blk = lambda f: pl.BlockSpec((SUPER, 128), f)
    cur = blk(lambda h, s: (s, h))
    prev = blk(lambda h, s: (jnp.maximum(s - 1, 0), h))
    return cur, prev


def _attn_fwd(qn, kd, vd):
    t = qn.shape[0]
    cur, prev = _attn_specs(t)

    def body(q_ref, kp_ref, kc_ref, vp_ref, vc_ref, o_ref, lse_ref, kk, vv, qd, kdd, vdd, po, pm, pll, acc, mm, ll):
        s = pl.program_id(1)
        kk[0:SUPER, :] = kp_ref[...]
        kk[SUPER:, :] = kc_ref[...]
        vv[0:SUPER, :] = vp_ref[...]
        vv[SUPER:, :] = vc_ref[...]
        m0 = lax.broadcasted_iota(jnp.int32, (CHUNK, 128), 1) < HEAD_DIM
        band, kj = _tile_masks()
        for d in DILATIONS:
            lq = SUPER // d
            if d == 1:
                qs_ref, ks_ref, vs_ref = q_ref, kk, vv
            else:
                _deinterleave(qd, q_ref, SUPER, d)
                _deinterleave(kdd, kk, 2 * SUPER, d)
                _deinterleave(vdd, vv, 2 * SUPER, d)
                qs_ref, ks_ref, vs_ref = qd, kdd, vdd

            def tile(ti, carry):
                r = ti // (lq // CHUNK)
                nb = ti % (lq // CHUNK)
                qoff = pl.multiple_of(ti * CHUNK, CHUNK)
                koff = pl.multiple_of(r * 2 * lq + lq + (nb - 1) * CHUNK, CHUNK)
                qt = qs_ref[pl.ds(qoff, CHUNK), :]
                qs = jnp.concatenate([jnp.where(m0, qt, 0.0), jnp.where(m0, 0.0, qt)], axis=0).astype(BF16)
                kt = ks_ref[pl.ds(koff, 2 * CHUNK), :].astype(BF16)
                vt = vs_ref[pl.ds(koff, 2 * CHUNK), :].astype(BF16)
                sc = _dot(qs, kt, NT)
                ok = band & (kj >= jnp.where((s > 0) | (nb > 0), 0, CHUNK))
                sc = jnp.where(ok, sc, -jnp.inf)
                mt = jnp.max(sc, axis=-1, keepdims=True)
                p = jnp.exp(sc - mt)
                lt = jnp.sum(p, axis=-1, keepdims=True)
                ot = _dot(p.astype(BF16), vt)
                po[pl.ds(qoff, CHUNK), :] = jnp.where(m0, ot[:CHUNK], ot[CHUNK:])
                pm[pl.ds(qoff, CHUNK), :] = jnp.where(m0, mt[:CHUNK], mt[CHUNK:])
                pll[pl.ds(qoff, CHUNK), :] = jnp.where(m0, lt[:CHUNK], lt[CHUNK:])
                return carry

            lax.fori_loop(0, SUPER // CHUNK, tile, 0, unroll=2 * TILE_UNROLL)
            if d == 1:
                acc[...] = po[...]
                mm[...] = pm[...]
                ll[...] = pll[...]
            else:
                for r in range(d):
                    rows = pl.ds(r, lq, stride=d)
                    seg = slice(r * lq, (r + 1) * lq)
                    m_old, m_new = mm[rows, :], pm[seg, :]
                    m_all = jnp.maximum(m_old, m_new)
                    a, b = jnp.exp(m_old - m_all), jnp.exp(m_new - m_all)
                    acc[rows, :] = acc[rows, :] * a + po[seg, :] * b
                    ll[rows, :] = ll[rows, :] * a + pll[seg, :] * b
                    mm[rows, :] = m_all
        o_ref[...] = acc[...] / ll[...]
        lse_ref[...] = mm[...] + jnp.log(ll[...])

    big = pltpu.VMEM((2 * SUPER, 128), F32)
    one = pltpu.VMEM((SUPER, 128), F32)
    return pl.pallas_call(
        body, name="attn_fwd", grid=(4, t // SUPER),
        in_specs=[cur, prev, cur, prev, cur], out_specs=[cur, cur],
        out_shape=[SDS((t, ATTN_DIM), F32)] * 2,
        scratch_shapes=[big, big, one, big, big, one, one, one, one, one, one],
        compiler_params=_params("parallel", "arbitrary"))(qn, kd, kd, vd, vd)


def _attn_bwd(qn, kd, vd, out, lse, dout):
    t = qn.shape[0]
    cur, prev = _attn_specs(t)

    def body(q_ref, kp_ref, kc_ref, vp_ref, vc_ref, o_ref, lse_ref, do_ref,
             dq_ref, dkc_ref, dkp_ref, dvc_ref, dvp_ref,
             kk, vv, dkk, dvv, qd, od, ld, dod, kdd, vdd, dkd, dvd, pdq):
        s = pl.program_id(1)
        kk[0:SUPER, :] = kp_ref[...]
        kk[SUPER:, :] = kc_ref[...]
        vv[0:SUPER, :] = vp_ref[...]
        vv[SUPER:, :] = vc_ref[...]
        dkk[...] = jnp.zeros_like(dkk)
        dvv[...] = jnp.zeros_like(dvv)
        dq_ref[...] = jnp.zeros_like(dq_ref)
        m0 = lax.broadcasted_iota(jnp.int32, (CHUNK, 128), 1) < HEAD_DIM
        band, kj = _tile_masks()
        ninf = -jnp.inf
        for d in DILATIONS:
            lq = SUPER // d
            if d == 1:
                qs_ref, os_ref, ls_ref, dos_ref, ks_ref, vs_ref, dks_ref, dvs_ref = (
                    q_ref, o_ref, lse_ref, do_ref, kk, vv, dkk, dvv)
            else:
                _deinterleave(qd, q_ref, SUPER, d)
                _deinterleave(od, o_ref, SUPER, d)
                _deinterleave(ld, lse_ref, SUPER, d)
                _deinterleave(dod, do_ref, SUPER, d)
                _deinterleave(kdd, kk, 2 * SUPER, d)
                _deinterleave(vdd, vv, 2 * SUPER, d)
                dkd[...] = jnp.zeros_like(dkd)
                dvd[...] = jnp.zeros_like(dvd)
                qs_ref, os_ref, ls_ref, dos_ref, ks_ref, vs_ref, dks_ref, dvs_ref = (
                    qd, od, ld, dod, kdd, vdd, dkd, dvd)

            def tile(ti, carry):
                r = ti // (lq // CHUNK)
                nb = ti % (lq // CHUNK)
                qoff = pl.multiple_of(ti * CHUNK, CHUNK)
                koff = pl.multiple_of(r * 2 * lq + lq + (nb - 1) * CHUNK, CHUNK)
                qrows = pl.ds(qoff, CHUNK)
                krows = pl.ds(koff, 2 * CHUNK)
                qt, ot, lt, dot_ = qs_ref[qrows, :], os_ref[qrows, :], ls_ref[qrows, :], dos_ref[qrows, :]
                qs = jnp.concatenate([jnp.where(m0, qt, 0.0), jnp.where(m0, 0.0, qt)], axis=0).astype(BF16)
                dos = jnp.concatenate([jnp.where(m0, dot_, 0.0), jnp.where(m0, 0.0, dot_)], axis=0).astype(BF16)
                lse_rows = jnp.concatenate([jnp.max(jnp.where(m0, lt, ninf), axis=-1, keepdims=True),
                                            jnp.max(jnp.where(m0, ninf, lt), axis=-1, keepdims=True)], axis=0)
                prod = dot_ * ot
                dl_rows = jnp.concatenate([jnp.sum(jnp.where(m0, prod, 0.0), axis=-1, keepdims=True),
                                           jnp.sum(jnp.where(m0, 0.0, prod), axis=-1, keepdims=True)], axis=0)
                kt = ks_ref[krows, :].astype(BF16)
                vt = vs_ref[krows, :].astype(BF16)
                sc = _dot(qs, kt, NT)
                ok = band & (kj >= jnp.where((s > 0) | (nb > 0), 0, CHUNK))
                p = jnp.exp(jnp.where(ok, sc, ninf) - lse_rows)
                dp = _dot(dos, vt, NT)
                ds = p * (dp - dl_rows)
                dqs = _dot(ds.astype(BF16), kt)
                pdq[qrows, :] = jnp.where(m0, dqs[:CHUNK], dqs[CHUNK:])
                dks_ref[krows, :] += _dot(ds.astype(BF16), qs, TN)
                dvs_ref[krows, :] += _dot(p.astype(BF16), dos, TN)
                return carry

            lax.fori_loop(0, SUPER // CHUNK, tile, 0, unroll=2 * TILE_UNROLL)
            if d == 1:
                dq_ref[...] += pdq[...]
            else:
                for r in range(d):
                    dq_ref[pl.ds(r, lq, stride=d), :] += pdq[r * lq:(r + 1) * lq, :]
                    dkk[pl.ds(r, 2 * lq, stride=d), :] += dkd[r * 2 * lq:(r + 1) * 2 * lq, :]
                    dvv[pl.ds(r, 2 * lq, stride=d), :] += dvd[r * 2 * lq:(r + 1) * 2 * lq, :]
        dkp_ref[...] = dkk[0:SUPER, :]
        dkc_ref[...] = dkk[SUPER:, :]
        dvp_ref[...] = dvv[0:SUPER, :]
        dvc_ref[...] = dvv[SUPER:, :]

    big = pltpu.VMEM((2 * SUPER, 128), F32)
    one = pltpu.VMEM((SUPER, 128), F32)
    return pl.pallas_call(
        body, name="attn_bwd", grid=(4, t // SUPER),
        in_specs=[cur, prev, cur, prev, cur, cur, cur, cur], out_specs=[cur] * 5,
        out_shape=[SDS((t, ATTN_DIM), F32)] * 5,
        scratch_shapes=[big, big, big, big, one, one, one, one, big, big, big, big, one],
        compiler_params=_params("parallel", "arbitrary"))(qn, kd, kd, vd, vd, out, lse, dout)


def _ssd_consts():
    tri = np.tril(np.ones((CHUNK, CHUNK), np.float32))
    expand = np.zeros((128, SSM_INNER), np.float32)
    for h in range(SSM_HEADS):
        expand[h, h * HEAD_DIM:(h + 1) * HEAD_DIM] = 1.0
    return jnp.asarray(tri, BF16), jnp.asarray(tri.T, BF16), jnp.asarray(expand, BF16), jnp.asarray(expand.T, BF16)


def _conv4(x, halo, w_ref, b_ref):
    acc = b_ref[...] + w_ref[3:4, :] * x
    for k in range(3):
        acc = acc + w_ref[k:k + 1, :] * _shift_down(x, halo, 3 - k)
    return acc


def _softplus(x):
    return jnp.maximum(x, 0.0) + jnp.log(1.0 + jnp.exp(-jnp.abs(x)))


def _ssd_common(xs_ref, bc_ref, dt_ref, hx_ref, hb_ref, cwx_ref, cbx_ref, cwb_ref, cbb_ref, dtb_ref, alog_ref,
                tri_ref, exp_ref, first):
    keep = 1.0 - first.astype(F32)
    hx = hx_ref[...] * keep
    hb = hb_ref[...] * keep
    pre_x = _conv4(xs_ref[...], hx, cwx_ref, cbx_ref)
    pre_b = _conv4(bc_ref[...], hb, cwb_ref, cbb_ref)
    xa = pre_x * _sigmoid(pre_x)
    ba = pre_b * _sigmoid(pre_b)
    dtv = _softplus(dt_ref[...] + dtb_ref[...])
    a_neg = -jnp.exp(alog_ref[...])
    acum = _hdot(tri_ref[...], dtv * a_neg, parts=3)
    lam = jnp.exp(acum)
    gam = jnp.exp(acum[CHUNK - 1:CHUNK, :] - acum)
    dt_e = _hdot(dtv, exp_ref[...])
    lam_e = _hdot(lam, exp_ref[...])
    gam_e = _hdot(gam, exp_ref[...])
    return dict(hx=hx, hb=hb, pre_x=pre_x, pre_b=pre_b, xa=xa, ba=ba, dtv=dtv, a_neg=a_neg, acum=acum,
                dt_e=dt_e, lam_e=lam_e, gam_e=gam_e, xdt=xa * dt_e)


def _decay(acum_t, h, transposed):
    rb = jnp.broadcast_to(acum_t[h:h + 1, :], (CHUNK, CHUNK))
    ri = lax.broadcasted_iota(jnp.int32, (CHUNK, CHUNK), 0)
    ci = lax.broadcasted_iota(jnp.int32, (CHUNK, CHUNK), 1)
    if transposed:
        return jnp.exp(jnp.where(ci >= ri, rb - rb.T, -jnp.inf))
    return jnp.exp(jnp.where(ri >= ci, rb.T - rb, -jnp.inf))


def _ssd_specs(t, rev):
    nc = t // CHUNK
    ch = (lambda c: nc - 1 - c) if rev else (lambda c: c)
    col = lambda w, j: pl.BlockSpec((CHUNK, w), lambda c: (ch(c), j))
    halo = lambda w, j: pl.BlockSpec((8, w), lambda c: (jnp.maximum(ch(c) * (CHUNK // 8) - 1, 0), j))
    return nc, ch, col, halo


def _ssd_fwd(proj, cwx, cbx, cwb, cbb, dtb, alog, dsk_e, norm_g, tri, expand):
    t = proj.shape[0]
    nc, _, col, halo = _ssd_specs(t, False)

    def body(z_ref, xs_ref, bc_ref, dt_ref, hx_ref, hb_ref, cwx_ref, cbx_ref, cwb_ref, cbb_ref, dtb_ref, alog_ref,
             dsk_ref, g_ref, tri_ref, exp_ref, y_ref, hs_ref, o_ref, state):
        c = pl.program_id(0)

        @pl.when(c == 0)
        def _():
            state[...] = jnp.zeros_like(state)

        v = _ssd_common(xs_ref, bc_ref, dt_ref, hx_ref, hb_ref, cwx_ref, cbx_ref, cwb_ref, cbb_ref, dtb_ref,
                        alog_ref, tri_ref, exp_ref, c == 0)
        acum_t = v["acum"].T
        xdt, ba = v["xdt"], v["ba"]
        h_in = state[...]
        hs_ref[0] = h_in
        xg = xdt * v["gam_e"]
        m0 = lax.broadcasted_iota(jnp.int32, (CHUNK, 128), 1) < HEAD_DIM
        for g in range(2):
            bg = ba[:, g * 128:(g + 1) * 128].astype(BF16)
            cg = ba[:, 256 + g * 128:256 + (g + 1) * 128].astype(BF16)
            gl = slice(g * 512, (g + 1) * 512)
            cb = _dot(cg, bg, NT)
            y_off = _dot(cg, h_in[:, gl].astype(BF16)) * v["lam_e"][:, gl]
            s_new = _dot(bg.T, xg[:, gl].astype(BF16))
            state[:, gl] = h_in[:, gl] * v["lam_e"][CHUNK - 1:CHUNK, gl] + s_new
            for j in range(4):
                h0 = 8 * g + 2 * j
                ln = slice(g * 512 + j * 128, g * 512 + (j + 1) * 128)
                xp = xdt[:, ln].astype(BF16)
                y0 = _dot((cb * _decay(acum_t, h0, False)).astype(BF16), xp)
                y1 = _dot((cb * _decay(acum_t, h0 + 1, False)).astype(BF16), xp)
                y_ref[:, ln] = jnp.where(m0, y0, y1) + y_off[:, j * 128:(j + 1) * 128]
        z = z_ref[...]
        yg = (y_ref[...] + dsk_ref[...] * v["xa"]) * (z * _sigmoid(z))
        r = lax.rsqrt(jnp.mean(yg * yg, axis=-1, keepdims=True) + EPS)
        o_ref[...] = (yg * r * g_ref[...]).astype(BF16)

    full = lambda a: pl.BlockSpec(a.shape, lambda c: (0,) * a.ndim)
    return pl.pallas_call(
        body, name="ssd_fwd", grid=(nc,),
        in_specs=[col(1024, 1), col(1024, 2), col(512, 6), col(128, 28), halo(1024, 2), halo(512, 6),
                  full(cwx), full(cbx), full(cwb), full(cbb), full(dtb), full(alog), full(dsk_e), full(norm_g),
                  full(tri), full(expand)],
        out_specs=[pl.BlockSpec((CHUNK, SSM_INNER), lambda c: (c, 0)),
                   pl.BlockSpec((1, 128, SSM_INNER), lambda c: (c, 0, 0)),
                   pl.BlockSpec((CHUNK, SSM_INNER), lambda c: (c, 0))],
        out_shape=[SDS((t, SSM_INNER), F32), SDS((nc, 128, SSM_INNER), F32), SDS((t, SSM_INNER), BF16)],
        scratch_shapes=[pltpu.VMEM((128, SSM_INNER), F32)],
        compiler_params=_params("arbitrary"))(proj, proj, proj, proj, proj, proj, cwx, cbx, cwb, cbb, dtb, alog,
                                              dsk_e, norm_g, tri, expand)


def _ssd_bwd(proj, y_ssd, hs, dout, cwx, cbx, cwb, cbb, dtb, alog, dsk_e, norm_g, tri, triu, expand, expand_t):
    t = proj.shape[0]
    nc, ch, col, halo = _ssd_specs(t, True)

    def body(z_ref, xs_ref, bc_ref, dt_ref, hx_ref, hb_ref, y_ref, hin_ref, do_ref,
             cwx_ref, cbx_ref, cwb_ref, cbb_ref, dtb_ref, alog_ref, dsk_ref, g_ref, tri_ref, triu_ref, exp_ref, expt_ref,
             dz_ref, dxs_ref, dbc_ref, ddt_ref, dg_ref, ddsk_ref, dalog_ref, ddtb_ref, dcwx_ref, dcbx_ref, dcwb_ref,
             dcbb_ref, gstate, nx_x, nx_b, dact_b, dxdt_s):
        step = pl.program_id(0)
        c = nc - 1 - step

        @pl.when(step == 0)
        def _():
            gstate[...] = jnp.zeros_like(gstate)
            nx_x[...] = jnp.zeros_like(nx_x)
            nx_b[...] = jnp.zeros_like(nx_b)
            for ref in (dg_ref, ddsk_ref, dalog_ref, ddtb_ref, dcwx_ref, dcbx_ref, dcwb_ref, dcbb_ref):
                ref[...] = jnp.zeros_like(ref)

        v = _ssd_common(xs_ref, bc_ref, dt_ref, hx_ref, hb_ref, cwx_ref, cbx_ref, cwb_ref, cbb_ref, dtb_ref,
                        alog_ref, tri_ref, exp_ref, c == 0)
        acum_t = v["acum"].T
        xa, ba, xdt, dtv = v["xa"], v["ba"], v["xdt"], v["dtv"]
        lam_e, gam_e, dt_e = v["lam_e"], v["gam_e"], v["dt_e"]
        z = z_ref[...]
        y = y_ref[...]
        sz = _sigmoid(z)
        zs = z * sz
        y_tot = y + dsk_ref[...] * xa
        yg = y_tot * zs
        r = lax.rsqrt(jnp.mean(yg * yg, axis=-1, keepdims=True) + EPS)
        yh = yg * r
        do = do_ref[...]
        dg_ref[...] += jnp.sum(do * yh, axis=0, keepdims=True)
        gd = do * g_ref[...]
        dyg = r * (gd - yh * jnp.mean(gd * yh, axis=-1, keepdims=True))
        dz_ref[...] = (dyg * y_tot * (sz * (1.0 + z * (1.0 - sz)))).astype(BF16)
        dy = dyg * zs
        ddsk_ref[...] += jnp.sum(dy * xa, axis=0, keepdims=True)
        g_out = gstate[...]
        h_in = hin_ref[0]
        lam_dy = lam_e * dy
        gam_x = gam_e * xdt
        m0 = lax.broadcasted_iota(jnp.int32, (CHUNK, 128), 1) < HEAD_DIM
        lane = lax.broadcasted_iota(jnp.int32, (CHUNK, 128), 1)
        below = (lax.broadcasted_iota(jnp.int32, (CHUNK, CHUNK), 0) >
                 lax.broadcasted_iota(jnp.int32, (CHUNK, CHUNK), 1))
        da_in = jnp.zeros((CHUNK, 128), F32)
        off_y, off_x = [], []
        for g in range(2):
            bg = ba[:, g * 128:(g + 1) * 128].astype(BF16)
            cg = ba[:, 256 + g * 128:256 + (g + 1) * 128].astype(BF16)
            gl = slice(g * 512, (g + 1) * 512)
            gg = g_out[:, gl].astype(BF16)
            bc_t = _dot(bg, cg, NT)
            cb = _dot(cg, bg, NT)
            dxdt_off = _dot(bg, gg) * gam_e[:, gl]
            off_x.append(xdt[:, gl] * dxdt_off)
            off_y.append(dy[:, gl] * (_dot(cg, h_in[:, gl].astype(BF16)) * lam_e[:, gl]))
            q_sum = jnp.zeros((CHUNK, CHUNK), F32)
            for j in range(4):
                h0 = 8 * g + 2 * j
                ln = slice(g * 512 + j * 128, g * 512 + (j + 1) * 128)
                dyp = dy[:, ln]
                dyb = dyp.astype(BF16)
                xpb = xdt[:, ln].astype(BF16)
                d0 = _dot((bc_t * _decay(acum_t, h0, True)).astype(BF16), dyb)
                d1 = _dot((bc_t * _decay(acum_t, h0 + 1, True)).astype(BF16), dyb)
                dxdt_s[:, ln] = jnp.where(m0, d0, d1) + dxdt_off[:, j * 128:(j + 1) * 128]
                for hh, dym in ((h0, jnp.where(m0, dyp, 0.0)), (h0 + 1, jnp.where(m0, 0.0, dyp))):
                    qd = _dot(dym.astype(BF16), xpb, NT) * _decay(acum_t, hh, False)
                    q_sum = q_sum + qd
                    reach = jnp.where(below, _hdot(triu_ref[...], qd * cb), 0.0)
                    da_in = jnp.where(lane == hh, jnp.sum(reach, axis=-1, keepdims=True), da_in)
            gstate[:, gl] = g_out[:, gl] * lam_e[CHUNK - 1:CHUNK, gl] + _dot(cg.T, lam_dy[:, gl].astype(BF16))
            qb = q_sum.astype(BF16)
            dact_b[:, 256 + g * 128:256 + (g + 1) * 128] = (
                _dot(qb, bg) + _dot(lam_dy[:, gl].astype(BF16), h_in[:, gl].astype(BF16), NT))
            dact_b[:, g * 128:(g + 1) * 128] = _dot(qb.T, cg) + _dot(gam_x[:, gl].astype(BF16), gg, NT)
        dxdt = dxdt_s[...]
        seg_y = _hdot(jnp.concatenate(off_y, axis=1), expt_ref[...])
        seg_x = _hdot(jnp.concatenate(off_x, axis=1), expt_ref[...])
        e_col = jnp.sum(g_out * h_in * lam_e[CHUNK - 1:CHUNK, :], axis=0, keepdims=True)
        e_seg = _hdot(jnp.broadcast_to(e_col, (8, SSM_INNER)), expt_ref[...])[0:1, :]
        da = da_in + _hdot(triu_ref[...], seg_y) + (_hdot(tri_ref[...], seg_x) - seg_x) + e_seg
        a_neg = v["a_neg"]
        ddtv = da * a_neg + _hdot(dxdt * xa, expt_ref[...])
        dalog_ref[...] += jnp.sum(da * dtv, axis=0, keepdims=True) * a_neg
        lane16 = lax.broadcasted_iota(jnp.int32, (CHUNK, 128), 1) < SSM_HEADS
        draw = jnp.where(lane16, ddtv * _sigmoid(dt_ref[...] + dtb_ref[...]), 0.0)
        ddtb_ref[...] += jnp.sum(draw, axis=0, keepdims=True)
        ddt_ref[...] = draw.astype(BF16)
        dxa = dxdt * dt_e + dy * dsk_ref[...]
        for (dact, pre, x_ref, nx, cw_ref, dcw_ref, dcb_ref, dx_ref) in (
                (dxa, v["pre_x"], xs_ref, nx_x, cwx_ref, dcwx_ref, dcbx_ref, dxs_ref),
                (dact_b[...], v["pre_b"], bc_ref, nx_b, cwb_ref, dcwb_ref, dcbb_ref, dbc_ref)):
            sp = _sigmoid(pre)
            dpre = dact * (sp * (1.0 + pre * (1.0 - sp)))
            dcb_ref[...] += jnp.sum(dpre, axis=0, keepdims=True)
            xv = x_ref[...]
            nxt = nx[...]
            dx = cw_ref[3:4, :] * dpre
            dcw_ref[3:4, :] += jnp.sum(dpre * xv, axis=0, keepdims=True)
            for k in range(3):
                d_up = _shift_up(dpre, nxt, 3 - k)
                dcw_ref[k:k + 1, :] += jnp.sum(xv * d_up, axis=0, keepdims=True)
                dx = dx + cw_ref[k:k + 1, :] * d_up
            nx[...] = dpre[0:8, :]
            dx_ref[...] = dx.astype(dx_ref.dtype)

    full = lambda a: pl.BlockSpec(a.shape, lambda c: (0,) * a.ndim)
    rowblk = lambda w: pl.BlockSpec((CHUNK, w), lambda c: (ch(c), 0))
    acc = lambda a, b: pl.BlockSpec((a, b), lambda c: (0, 0))
    return pl.pallas_call(
        body, name="ssd_bwd", grid=(nc,),
        in_specs=[col(1024, 1), col(1024, 2), col(512, 6), col(128, 28), halo(1024, 2), halo(512, 6),
                  rowblk(SSM_INNER),
                  pl.BlockSpec((1, 128, SSM_INNER), lambda c: (ch(c), 0, 0)),
                  rowblk(SSM_INNER),
                  full(cwx), full(cbx), full(cwb), full(cbb), full(dtb), full(alog), full(dsk_e), full(norm_g),
                  full(tri), full(triu), full(expand), full(expand_t)],
        out_specs=[rowblk(SSM_INNER), rowblk(SSM_INNER), rowblk(512), rowblk(128),
                   acc(1, 1024), acc(1, 1024), acc(1, 128), acc(1, 128), acc(4, 1024), acc(1, 1024), acc(4, 512),
                   acc(1, 512)],
        out_shape=[SDS((t, SSM_INNER), BF16), SDS((t, SSM_INNER), BF16), SDS((t, 512), BF16), SDS((t, 128), BF16),
                   SDS((1, 1024), F32), SDS((1, 1024), F32), SDS((1, 128), F32), SDS((1, 128), F32),
                   SDS((4, 1024), F32), SDS((1, 1024), F32), SDS((4, 512), F32), SDS((1, 512), F32)],
        scratch_shapes=[pltpu.VMEM((128, SSM_INNER), F32), pltpu.VMEM((8, 1024), F32), pltpu.VMEM((8, 512), F32),
                        pltpu.VMEM((CHUNK, 512), F32), pltpu.VMEM((CHUNK, SSM_INNER), F32)],
        compiler_params=_params("arbitrary"))(proj, proj, proj, proj, proj, proj, y_ssd, hs, dout,
                                              cwx, cbx, cwb, cbb, dtb, alog, dsk_e, norm_g, tri, triu, expand,
                                              expand_t)


def _conv3(x, halo, w_ref, b_ref, part):
    acc = b_ref[part] + w_ref[2, part] * x
    for k in range(2):
        acc = acc + w_ref[k, part] * _shift_down(x, halo, 2 - k)
    return acc


def _up_act(x, g, w_up_t, cw, cb, tm=2048, tn=256):
    t, k = x.shape
    nj = D_FF // tn

    def body(x_ref, g_ref, wg_ref, wv_ref, w_ref, b_ref, u_ref, h_ref, f_ref, halo):
        i, j = pl.program_id(0), pl.program_id(1)

        @pl.when(j == 0)
        def _():
            xv = x_ref[...]
            r = lax.rsqrt(jnp.mean(xv * xv, axis=-1, keepdims=True) + EPS)
            h_ref[...] = (xv * r * g_ref[...]).astype(BF16)

        @pl.when(i == 0)
        def _():
            halo[j] = jnp.zeros((2, 8, tn), F32)

        parts = []
        for part, wt_ref in enumerate((wg_ref, wv_ref)):
            u = _dot(h_ref[...], wt_ref[...], NT)
            u_ref[part] = u
            parts.append(_conv3(u, halo[j, part], w_ref, b_ref, part))
            halo[j, part] = u[tm - 8:, :]
        gate, val = parts
        f_ref[...] = (gate * _sigmoid(gate) * val).astype(BF16)

    return pl.pallas_call(
        body, name="up_proj", grid=(t // tm, nj),
        in_specs=[pl.BlockSpec((tm, k), lambda i, j: (i, 0)), pl.BlockSpec((1, k), lambda i, j: (0, 0)),
                  pl.BlockSpec((tn, k), lambda i, j: (j, 0)), pl.BlockSpec((tn, k), lambda i, j: (j + nj, 0)),
                  pl.BlockSpec((3, 2, 1, tn), lambda i, j: (0, 0, 0, j)), pl.BlockSpec((2, 1, tn), lambda i, j: (0, 0, j))],
        out_specs=[pl.BlockSpec((2, tm, tn), lambda i, j: (0, i, j)), pl.BlockSpec((tm, k), lambda i, j: (i, 0)),
                   pl.BlockSpec((tm, tn), lambda i, j: (i, j))],
        out_shape=[SDS((2, t, D_FF), F32), SDS((t, k), BF16), SDS((t, D_FF), BF16)],
        scratch_shapes=[pltpu.VMEM((nj, 2, 8, tn), F32)],
        compiler_params=_params("arbitrary", "arbitrary"))(x, g, w_up_t, w_up_t, cw, cb)


def _ffn_bwd(dx2, w_down, u, cw, cb, tm=512, tn=1408):
    t = u.shape[1]
    nj, ni = D_FF // tn, t // tm
    rev = lambda i: ni - 1 - i

    def body(dx_ref, wd_ref, u_ref, uh_ref, w_ref, b_ref, du_ref, dcw_ref, dcb_ref, nxt):
        i = pl.program_id(1)

        @pl.when(i == 0)
        def _():
            nxt[...] = jnp.zeros_like(nxt)
            dcw_ref[...] = jnp.zeros_like(dcw_ref)
            dcb_ref[...] = jnp.zeros_like(dcb_ref)

        df = _dot(dx_ref[...].astype(BF16), wd_ref[...], NT)
        keep = (i < ni - 1).astype(F32)
        ug, uv = u_ref[0], u_ref[1]
        hg, hv = uh_ref[0] * keep, uh_ref[1] * keep
        gate = _conv3(ug, hg, w_ref, b_ref, 0)
        val = _conv3(uv, hv, w_ref, b_ref, 1)
        sg = _sigmoid(gate)
        dgate = df * val * (sg * (1.0 + gate * (1.0 - sg)))
        dval = df * (gate * sg)
        for part, (d, uu) in enumerate(((dgate, ug), (dval, uv))):
            dcb_ref[part] += jnp.sum(d, axis=0, keepdims=True)
            ahead = nxt[part]
            acc = w_ref[2, part] * d
            dcw_ref[2, part] += jnp.sum(d * uu, axis=0, keepdims=True)
            for k in range(2):
                d_up = _shift_up(d, ahead, 2 - k)
                dcw_ref[k, part] += jnp.sum(uu * d_up, axis=0, keepdims=True)
                acc = acc + w_ref[k, part] * d_up
            nxt[part] = d[0:8, :]
            du_ref[part] = acc.astype(BF16)

    w_spec = pl.BlockSpec((3, 2, 1, tn), lambda j, i: (0, 0, 0, j))
    b_spec = pl.BlockSpec((2, 1, tn), lambda j, i: (0, 0, j))
    return pl.pallas_call(
        body, name="ffn_bwd", grid=(nj, ni),
        in_specs=[pl.BlockSpec((tm, D_MODEL), lambda j, i: (rev(i), 0)), pl.BlockSpec((tn, D_MODEL), lambda j, i: (j, 0)),
                  pl.BlockSpec((2, tm, tn), lambda j, i: (0, rev(i), j)),
                  pl.BlockSpec((2, 8, tn), lambda j, i: (0, jnp.maximum(rev(i) * (tm // 8) - 1, 0), j)),
                  w_spec, b_spec],
        out_specs=[pl.BlockSpec((2, tm, tn), lambda j, i: (0, rev(i), j)), w_spec, b_spec],
        out_shape=[SDS((2, t, D_FF), BF16), SDS((3, 2, 1, D_FF), F32), SDS((2, 1, D_FF), F32)],
        scratch_shapes=[pltpu.VMEM((2, 8, tn), F32)],
        compiler_params=_params("parallel", "arbitrary"))(dx2, w_down, u, u, cw, cb)


def _ple_loss(x2, g, w_gate, p, w_proj_t, target, tm=256):
    t = x2.shape[0]

    def body(x_ref, g_ref, wg_ref, p_ref, wp_ref, tg_ref, dx_ref, dpre_ref, dpp_ref, h_ref, loss_ref, dg_ref):
        i = pl.program_id(0)
        xv = x_ref[...]
        r = lax.rsqrt(jnp.mean(xv * xv, axis=-1, keepdims=True) + EPS)
        xh = xv * r
        h = (xh * g_ref[...]).astype(BF16)
        h_ref[...] = h
        gate = _sigmoid(_dot(h, wg_ref[...]))
        pp = _dot(p_ref[...].astype(BF16), wp_ref[...], NT)
        err = (xv + gate * pp) - tg_ref[...]

        @pl.when(i == 0)
        def _():
            loss_ref[...] = jnp.zeros_like(loss_ref)
            dg_ref[...] = jnp.zeros_like(dg_ref)

        loss_ref[...] += 0.5 * jnp.sum(jnp.mean(err * err, axis=-1, keepdims=True), axis=0, keepdims=True)
        dy = err * (1.0 / D_MODEL)
        dpre = (dy * pp * gate * (1.0 - gate)).astype(BF16)
        dpre_ref[...] = dpre
        dpp_ref[...] = (dy * gate).astype(BF16)
        dh = _dot(dpre, wg_ref[...], NT)
        dg_ref[...] += jnp.sum(dh * xh, axis=0, keepdims=True)
        gd = dh * g_ref[...]
        dx_ref[...] = dy + r * (gd - xh * jnp.mean(gd * xh, axis=-1, keepdims=True))

    row = lambda w: pl.BlockSpec((tm, w), lambda i: (i, 0))
    full = lambda a: pl.BlockSpec(a.shape, lambda i: (0, 0))
    return pl.pallas_call(
        body, name="ple_loss", grid=(t // tm,),
        in_specs=[row(D_MODEL), full(g), full(w_gate), row(PLE_DIM), full(w_proj_t), row(D_MODEL)],
        out_specs=[row(D_MODEL), row(D_MODEL), row(D_MODEL), row(D_MODEL),
                   pl.BlockSpec((1, 128), lambda i: (0, 0)), pl.BlockSpec((1, D_MODEL), lambda i: (0, 0))],
        out_shape=[SDS((t, D_MODEL), F32), SDS((t, D_MODEL), BF16), SDS((t, D_MODEL), BF16), SDS((t, D_MODEL), BF16),
                   SDS((1, 128), F32), SDS((1, D_MODEL), F32)],
        compiler_params=_params("arbitrary"))(x2, g, w_gate, p, w_proj_t, target)


def _exchange(scatter, gather, name):
    arrays = list(scatter) + list(gather)
    n_a, n_s = len(arrays), len(scatter)

    def body(*refs):
        src, dst = refs[:n_a], refs[n_a:2 * n_a]
        send_sems, recv_sems, local_sems = refs[2 * n_a:]
        x, y, c = lax.axis_index("x"), lax.axis_index("y"), lax.axis_index("c")
        me = 4 * x + 2 * y + c

        def src_of(a, slot):
            return src[a].at[slot] if a < n_s else src[a]

        local = [pltpu.make_async_copy(src_of(a, me), dst[a].at[me], local_sems.at[a]) for a in range(n_a)]
        for cp in local:
            cp.start()
        sends, peers = [], []
        for k in range(1, N_DEV):
            px = 1 - x if k & 4 else x
            py = 1 - y if k & 2 else y
            pc = 1 - c if k & 1 else c
            peer = 4 * px + 2 * py + pc
            peers.append(peer)
            for a in range(n_a):
                cp = pltpu.make_async_remote_copy(
                    src_ref=src_of(a, peer), dst_ref=dst[a].at[me], send_sem=send_sems.at[a, k - 1],
                    recv_sem=recv_sems.at[a, k - 1], device_id=(px, py, pc), device_id_type=pl.DeviceIdType.MESH)
                cp.start()
                sends.append(cp)
        for k in range(1, N_DEV):
            peer = peers[k - 1]
            for a in range(n_a):
                pltpu.make_async_remote_copy(
                    src_ref=src_of(a, peer), dst_ref=dst[a].at[peer], send_sem=send_sems.at[a, k - 1],
                    recv_sem=recv_sems.at[a, k - 1], device_id=(x, y, c),
                    device_id_type=pl.DeviceIdType.MESH).wait_recv()
        for cp in sends:
            cp.wait_send()
        for cp in local:
            cp.wait()

    out_shape = [SDS(a.shape, a.dtype) for a in scatter] + [SDS((N_DEV,) + a.shape, a.dtype) for a in gather]
    hbm = pl.BlockSpec(memory_space=pl.ANY)
    return pl.pallas_call(
        body, name=name, in_specs=[hbm] * n_a, out_specs=[hbm] * n_a, out_shape=out_shape,
        scratch_shapes=[pltpu.SemaphoreType.DMA((n_a, N_DEV - 1)), pltpu.SemaphoreType.DMA((n_a, N_DEV - 1)),
                        pltpu.SemaphoreType.DMA((n_a,))],
        )(*arrays)


def _peer(k):
    x, y, c = lax.axis_index("x"), lax.axis_index("y"), lax.axis_index("c")
    px = 1 - x if k & 4 else x
    py = 1 - y if k & 2 else y
    pc = 1 - c if k & 1 else c
    return (px, py, pc), 4 * px + 2 * py + pc


_HBM = pl.BlockSpec(memory_space=pltpu.HBM)
_SEM = pl.BlockSpec(memory_space=pltpu.SEMAPHORE)


def _split_copies(src, land, send_sems, recv_sems, scatter, arrivals):
    _, me = _peer(0)
    out = []
    for k in range(1, N_DEV):
        coords, peer = _peer(k)
        for a in range(len(src)):
            sem = a * (N_DEV - 1) + k - 1
            if scatter[a]:
                s, d = src[a].at[peer], land[a].at[k]
            else:
                s, d = src[a], land[a].at[peer if arrivals else me]
            out.append(pltpu.make_async_remote_copy(
                src_ref=s, dst_ref=d, send_sem=send_sems.at[sem], recv_sem=recv_sems.at[sem], device_id=coords,
                device_id_type=pl.DeviceIdType.MESH))
    return out


def _exchange_start(srcs, lands, scatter, name):
    n = len(srcs)

    def body(*refs):
        src, land = refs[:n], refs[n:2 * n]
        send_sems, recv_sems = refs[2 * n], refs[2 * n + 1]
        token = refs[-1]
        for cp in _split_copies(src, land, send_sems, recv_sems, scatter, False):
            cp.start()
        token[...] = jnp.zeros_like(token)

    hbm_shape = lambda a: pltpu.HBM(a.shape, a.dtype)
    sem_shape = pltpu.SemaphoreType.DMA((n * (N_DEV - 1),))
    outs = pl.pallas_call(
        body, name=name,
        out_shape=(sem_shape, sem_shape, *[hbm_shape(a) for a in srcs], *[hbm_shape(a) for a in lands],
                   SDS((8, 128), F32)),
        in_specs=[_HBM] * (2 * n), out_specs=(_SEM, _SEM, *[_HBM] * (2 * n), pl.BlockSpec(memory_space=pltpu.VMEM)),
        input_output_aliases={a: 2 + a for a in range(2 * n)},
        compiler_params=pltpu.CompilerParams(has_side_effects=pltpu.SideEffectType.DATAFLOW_SIDE_EFFECTING),
    )(*[pltpu.with_memory_space_constraint(a, pltpu.HBM) for a in list(srcs) + list(lands)])
    return outs[0], outs[1], outs[2:2 + n], outs[2 + n:2 + 2 * n], outs[-1]


def _exchange_wait(send_sems, recv_sems, srcs, lands, scatter, after, name):
    n = len(srcs)

    def body(*refs):
        src, land = refs[:n], refs[n:2 * n]
        for cp in _split_copies(src, land, refs[2 * n], refs[2 * n + 1], scatter, False):
            cp.wait_send()
        for cp in _split_copies(src, land, refs[2 * n], refs[2 * n + 1], scatter, True):
            cp.wait_recv()

    hbm_shape = lambda a: pltpu.HBM(a.shape, a.dtype)
    outs = pl.pallas_call(
        body, name=name, out_shape=tuple(hbm_shape(a) for a in list(srcs) + list(lands)),
        in_specs=[_HBM] * (2 * n) + [_SEM, _SEM, pl.BlockSpec(memory_space=pl.ANY)], out_specs=(_HBM,) * (2 * n),
        input_output_aliases={a: a for a in range(2 * n)},
        compiler_params=pltpu.CompilerParams(has_side_effects=pltpu.SideEffectType.DATAFLOW_SIDE_EFFECTING),
    )(*srcs, *lands, send_sems, recv_sems, after)
    return outs[:n], outs[n:]


def _reduce8(a, tr, name):
    _, rows, cols = a.shape

    def body(a_ref, o_ref):
        acc = a_ref[0]
        for j in range(1, N_DEV):
            acc = acc + a_ref[j]
        o_ref[...] = acc

    return pl.pallas_call(
        body, name=name, grid=(rows // tr,),
        in_specs=[pl.BlockSpec((N_DEV, tr, cols), lambda i: (0, i, 0))],
        out_specs=pl.BlockSpec((tr, cols), lambda i: (i, 0)), out_shape=SDS((rows, cols), F32),
        compiler_params=_params("parallel"))(a)


def _reduce_landed(own, land, name, tc=256):
    rows, cols = own.shape

    def body(own_ref, land_ref, o_ref):
        acc = own_ref[...]
        for k in range(1, N_DEV):
            acc = acc + land_ref[k].astype(F32)
        o_ref[...] = acc

    return pl.pallas_call(
        body, name=name, grid=(cols // tc,),
        in_specs=[pl.BlockSpec((rows, tc), lambda j: (0, j)), pl.BlockSpec((N_DEV, rows, tc), lambda j: (0, 0, j))],
        out_specs=pl.BlockSpec((rows, tc), lambda j: (0, j)), out_shape=SDS((rows, cols), F32),
        compiler_params=_params("parallel"))(own, land)


def _adamw(w, g, m, v, name, tr=None):
    rows, cols = w.shape
    tr = rows if tr is None else tr
    c1 = 1.0 - ADAM_B1 ** ADAM_STEP
    c2 = 1.0 - ADAM_B2 ** ADAM_STEP

    def body(w_ref, g_ref, m_ref, v_ref, d_ref, mo_ref, vo_ref):
        gv = g_ref[...]
        mn = ADAM_B1 * m_ref[...] + (1.0 - ADAM_B1) * gv
        vn = ADAM_B2 * v_ref[...] + (1.0 - ADAM_B2) * (gv * gv)
        mo_ref[...] = mn
        vo_ref[...] = vn
        d_ref[...] = -ADAM_LR * ((mn / c1) / (jnp.sqrt(vn / c2) + ADAM_EPS) + ADAM_WD * w_ref[...])

    blk = pl.BlockSpec((tr, cols), lambda i: (i, 0))
    return pl.pallas_call(
        body, name=name, grid=(rows // tr,), in_specs=[blk] * 4, out_specs=[blk] * 3,
        out_shape=[SDS((rows, cols), F32)] * 3, compiler_params=_params("parallel"))(w, g, m, v)


def _pad_rows(a, rows):
    return jnp.pad(a, ((0, rows - a.shape[0]),) + ((0, 0),) * (a.ndim - 1))


def _local_step(x, p, target, sm, wts, fetch_rest, send, tok):
    ones_q, ones_k, dup, dup_t = _head_consts()
    tri, triu, expand, expand_t = _ssd_consts()
    w_in_t = wts["in_t"]
    cwx, cwb = wts["ssm_cw"][:, :SSM_INNER], wts["ssm_cw"][:, SSM_INNER:]
    cbx, cbb = sm["ssm_conv_b"][:, :SSM_INNER], sm["ssm_conv_b"][:, SSM_INNER:]
    pad128 = lambda a: jnp.pad(a, ((0, 0), (0, 128 - a.shape[1])))
    dtb, alog = pad128(sm["dt_bias"]), pad128(sm["a_log"])
    dsk_e = jnp.repeat(sm["d_skip"], HEAD_DIM, axis=1)
    gq = jnp.tile(sm["q_norm_g"], (1, ATTN_DIM // HEAD_DIM))
    gk = jnp.tile(sm["k_norm_g"], (1, KV_DIM // HEAD_DIM))
    ffn_cw = wts["ffn_cw"].reshape(3, 2, 1, D_FF)
    ffn_cb = sm["ffn_conv_b"].reshape(2, 1, D_FF)

    proj, h1 = _norm_matmul(x, sm["attn_norm_g"] + tok, w_in_t, 1024, 768, "in_proj")
    qn, kd, vd = _attn_prep(proj, gq, gk, ones_q, ones_k, dup)
    attn_out, lse = _attn_fwd(qn, kd, vd)
    y_ssd, hs, ssm_out = _ssd_fwd(proj, cwx, cbx, cwb, cbb, dtb, alog, dsk_e, sm["ssm_norm_g"], tri, expand)
    rest = fetch_rest(ssm_out)
    w_out, w_up_t, w_down, w_gate, w_proj_t = (rest[k] for k in ("out", "up_t", "down", "gate", "proj_t"))
    x1 = _mm_resid([(attn_out, None, w_out[:ATTN_DIM]), (ssm_out, None, w_out[ATTN_DIM:])], x, None, 512, F32,
                   "out_proj")
    u, h2, f = _up_act(x1, sm["ffn_norm_g"], w_up_t, ffn_cw, ffn_cb)
    x2 =_mm_resid([(f, None, w_down)], x1, None, 512, F32, "down_proj")
    dx2, dpre, dpp, h3, loss, dg_ple = _ple_loss(x2, sm["ple_norm_g"], w_gate, p, w_proj_t, target)

    g_gate = _wgrad(h3, None, dpre, "wg_gate")
    g_proj_t = _wgrad(dpp, None, p, "wg_proj")
    g_down = _wgrad(f, None, dx2, "wg_down")
    du, d_ffn_cw, d_ffn_cb = _ffn_bwd(dx2, w_down, u, ffn_cw, ffn_cb)
    dx1, dg_ffn = _mm_normbwd([(du, 0, w_up_t[:D_FF]), (du, 1, w_up_t[D_FF:])], x1, sm["ffn_norm_g"], dx2, 256,
                              "up_proj_bwd")
    g_up_t = jnp.concatenate([_wgrad(du, 0, h2, "wg_up_gate"), _wgrad(du, 1, h2, "wg_up_val")], axis=0)
    tok = send(dict(gate=g_gate, proj_t=g_proj_t, down=g_down, up_t=g_up_t)).astype(BF16)
    d_attn = _mm_resid([(dx1, None, w_out[:ATTN_DIM] + tok)], None, NT, 512, F32, "out_proj_bwd_attn")
    d_ssm = _mm_resid([(dx1, None, w_out[ATTN_DIM:] + tok)], None, NT, 512, F32, "out_proj_bwd_ssm")
    g_out = jnp.concatenate([_wgrad(attn_out, None, dx1, "wg_out_attn"), _wgrad(ssm_out, None, dx1, "wg_out_ssm")],
                            axis=0)
    tok = send(dict(out=g_out))
    (dz, dxs, dbc, ddt, dg_ssm, d_dsk_e, d_alog, d_dtb, d_cwx, d_cbx, d_cwb, d_cbb) = _ssd_bwd(
        proj, y_ssd, hs, d_ssm, cwx, cbx, cwb, cbb, dtb + tok, alog, dsk_e, sm["ssm_norm_g"], tri, triu, expand,
        expand_t)
    dqn, dkc, dkp, dvc, dvp = _attn_bwd(qn, kd, vd, attn_out, lse, d_attn)
    dqkv, dgq, dgk = _attn_prep_bwd(proj, dqn, dkc, dkp, dvc, dvp, gq + tok, gk, ones_q, ones_k, dup_t)
    pieces = [(dqkv, 0, 1024), (dz, 1024, 2048), (dxs, 2048, 3072), (dbc, 3072, 3584), (ddt, 3584, 3712)]
    g_in_t = jnp.concatenate([_wgrad(a, None, h1, "wg_in_%d" % lo) for a, lo, _ in pieces], axis=0)[:IN_PROJ]
    tok = send(dict(in_t=g_in_t))
    grad_x, dg_attn = _mm_normbwd([(a, None, w_in_t[lo:hi]) for a, lo, hi in pieces], x, sm["attn_norm_g"] + tok, dx1,
                                  256, "in_proj_bwd")

    small = dict(
        attn_norm_g=dg_attn, q_norm_g=dgq.reshape(-1, HEAD_DIM).sum(0, keepdims=True),
        k_norm_g=dgk.reshape(-1, HEAD_DIM).sum(0, keepdims=True),
        ssm_conv_w=jnp.concatenate([d_cwx, d_cwb], axis=1), ssm_conv_b=jnp.concatenate([d_cbx, d_cbb], axis=1),
        dt_bias=d_dtb[:, :SSM_HEADS], a_log=d_alog[:, :SSM_HEADS],
        d_skip=d_dsk_e.reshape(SSM_HEADS, HEAD_DIM).sum(1)[None, :], ssm_norm_g=dg_ssm, ffn_norm_g=dg_ffn,
        ffn_conv_w=d_ffn_cw.reshape(3, 2 * D_FF), ffn_conv_b=d_ffn_cb.reshape(1, 2 * D_FF), ple_norm_g=dg_ple)
    return loss[0, 0], grad_x, small


_SMALL = (("attn_norm_g", 1024), ("q_norm_g", 64), ("k_norm_g", 64), ("ssm_conv_w", 4 * XBC_DIM),
          ("ssm_conv_b", XBC_DIM), ("dt_bias", 16), ("a_log", 16), ("d_skip", 16), ("ssm_norm_g", 1024),
          ("ffn_norm_g", 1024), ("ffn_conv_w", 3 * 2 * D_FF), ("ffn_conv_b", 2 * D_FF), ("ple_norm_g", 1024))
_SMALL_ROWS = 34
_SHARD_SMALL = (("attn_norm_g", 1024), ("q_norm_g", 64), ("k_norm_g", 64), ("ssm_conv_w", 4 * XBC_DIM // N_DEV),
                ("ssm_conv_b", XBC_DIM), ("dt_bias", 16), ("a_log", 16), ("d_skip", 16), ("ssm_norm_g", 1024),
                ("ffn_norm_g", 1024), ("ffn_conv_w", 3 * 2 * D_FF // N_DEV), ("ffn_conv_b", 2 * D_FF),
                ("ple_norm_g", 1024))
_SHARD_SMALL_ROWS = 14


def _pack_flat(parts, order, rows):
    flat = jnp.concatenate([parts[name].reshape(-1) for name, _ in order])
    return jnp.pad(flat, (0, rows * 1024 - flat.shape[0])).reshape(rows, 1024)


def _unpack_flat(packed, order):
    flat, out, pos = packed.reshape(-1), {}, 0
    for name, size in order:
        out[name] = flat[pos:pos + size]
        pos += size
    return out


def kernel(x, p, attn_norm_g, w_in, q_norm_g, k_norm_g, ssm_conv_w, ssm_conv_b, dt_bias, a_log, d_skip, ssm_norm_g, w_out, ffn_norm_g, w_up, ffn_conv_w, ffn_conv_b, w_down, ple_norm_g, w_ple_gate, w_ple_proj, loss_target, m_attn_norm_g, m_w_in, m_q_norm_g, m_k_norm_g, m_ssm_conv_w, m_ssm_conv_b, m_dt_bias, m_a_log, m_d_skip, m_ssm_norm_g, m_w_out, m_ffn_norm_g, m_w_up, m_ffn_conv_w, m_ffn_conv_b, m_w_down, m_ple_norm_g, m_w_ple_gate, m_w_ple_proj, v_attn_norm_g, v_w_in, v_q_norm_g, v_k_norm_g, v_ssm_conv_w, v_ssm_conv_b, v_dt_bias, v_a_log, v_d_skip, v_ssm_norm_g, v_w_out, v_ffn_norm_g, v_w_up, v_ffn_conv_w, v_ffn_conv_b, v_w_down, v_ple_norm_g, v_w_ple_gate, v_w_ple_proj):
    names = ("attn_norm_g", "w_in", "q_norm_g", "k_norm_g", "ssm_conv_w", "ssm_conv_b", "dt_bias", "a_log", "d_skip",
             "ssm_norm_g", "w_out", "ffn_norm_g", "w_up", "ffn_conv_w", "ffn_conv_b", "w_down", "ple_norm_g",
             "w_ple_gate", "w_ple_proj")
    w = dict(zip(names, (attn_norm_g, w_in, q_norm_g, k_norm_g, ssm_conv_w, ssm_conv_b, dt_bias, a_log, d_skip,
                         ssm_norm_g, w_out, ffn_norm_g, w_up, ffn_conv_w, ffn_conv_b, w_down, ple_norm_g, w_ple_gate,
                         w_ple_proj)))
    m = dict(zip(names, (m_attn_norm_g, m_w_in, m_q_norm_g, m_k_norm_g, m_ssm_conv_w, m_ssm_conv_b, m_dt_bias,
                         m_a_log, m_d_skip, m_ssm_norm_g, m_w_out, m_ffn_norm_g, m_w_up, m_ffn_conv_w, m_ffn_conv_b,
                         m_w_down, m_ple_norm_g, m_w_ple_gate, m_w_ple_proj)))
    v = dict(zip(names, (v_attn_norm_g, v_w_in, v_q_norm_g, v_k_norm_g, v_ssm_conv_w, v_ssm_conv_b, v_dt_bias,
                         v_a_log, v_d_skip, v_ssm_norm_g, v_w_out, v_ffn_norm_g, v_w_up, v_ffn_conv_w, v_ffn_conv_b,
                         v_w_down, v_ple_norm_g, v_w_ple_gate, v_w_ple_proj)))
    w, m, v = ({k: a[0] for k, a in d.items()} for d in (w, m, v))
    me = 4 * lax.axis_index("x") + 2 * lax.axis_index("y") + lax.axis_index("c")

    mine = dict(in_t=w["w_in"].T, out=w["w_out"], up_t=w["w_up"].T, down=w["w_down"], gate=w["w_ple_gate"],
                proj_t=w["w_ple_proj"].T)
    mine = {k: a.astype(BF16) for k, a in mine.items()}
    conv_pack = jnp.pad(jnp.concatenate([w["ssm_conv_w"].reshape(-1), w["ffn_conv_w"].reshape(-1)]),
                        (0, 3072 - 2880)).reshape(8, 384)
    all_in, all_conv = _exchange([], [mine["in_t"], conv_pack], "gather_first")
    later = ("out", "up_t", "down", "gate", "proj_t")
    zones = [lax.dynamic_update_slice(lax.empty((N_DEV,) + mine[k].shape, BF16), mine[k][None], (me, 0, 0))
             for k in later]
    zones, all_in, all_conv = lax.optimization_barrier((zones, all_in, all_conv))
    rest_state = _exchange_start([mine[k] for k in later], zones, [False] * len(later), "gather_rest_start")

    def fetch_rest(after):
        _, landed = _exchange_wait(*rest_state[:4], [False] * len(later), after, "gather_rest_wait")
        return {k: a.reshape(N_DEV * a.shape[1], a.shape[2]) for k, a in zip(later, landed)}

    wts = dict(in_t=_pad_rows(all_in.reshape(IN_PROJ, D_MODEL), IN_PROJ_PAD))
    conv_flat = all_conv.reshape(N_DEV, 3072)
    wts["ssm_cw"] = conv_flat[:, :768].reshape(N_DEV, 4, XBC_DIM // N_DEV).transpose(1, 0, 2).reshape(4, XBC_DIM)
    wts["ffn_cw"] = conv_flat[:, 768:2880].reshape(N_DEV, 3, 2 * D_FF // N_DEV).transpose(1, 0, 2).reshape(3, 2 * D_FF)
    sm = {k: w[k].reshape(1, -1) for k, _ in _SMALL if k not in ("ssm_conv_w", "ffn_conv_w")}

    in_flight = []

    def send(grads):
        keys = sorted(grads)
        blocks = [grads[k].reshape(N_DEV, grads[k].shape[0] // N_DEV, grads[k].shape[1]) for k in keys]
        own = [lax.dynamic_index_in_dim(a, me, 0, keepdims=False) for a in blocks]
        srcs = [a.astype(BF16) for a in blocks]
        state = _exchange_start(srcs, [lax.empty(a.shape, BF16) for a in srcs], [True] * len(keys),
                                "send_" + "_".join(keys))
        in_flight.append((keys, state, own))
        return state[4][0:1, 0:1]

    loss, grad_x, small = _local_step(x[0], p[0, 0], loss_target[0], sm, wts, fetch_rest, send,
                                      rest_state[4][0:1, 0:1])
    loss = lax.psum(loss, ("x", "y", "c"))

    (got_small,) = _exchange([], [_pack_flat(small, _SMALL, _SMALL_ROWS)], "gather_small_grads")
    g_small = _unpack_flat(_reduce8(got_small, _SMALL_ROWS, "reduce_small"), _SMALL)
    grads = {}
    for keys, state, own in in_flight:
        _, landed = _exchange_wait(*state[:4], [True] * len(keys), grad_x, "wait_" + "_".join(keys))
        for k, mine_k, land in zip(keys, own, landed):
            grads[k] = _reduce_landed(mine_k, land, "reduce_" + k)
    gw = {"w_in": grads["in_t"].T, "w_out": grads["out"], "w_up": grads["up_t"].T, "w_down": grads["down"],
          "w_ple_gate": grads["gate"], "w_ple_proj": grads["proj_t"].T}
    for k, size in _SMALL:
        gw[k] = g_small[k].reshape(w[k].shape) if k not in ("ssm_conv_w", "ffn_conv_w") else None
    n_ssm, n_ffn = XBC_DIM // N_DEV, 2 * D_FF // N_DEV
    gw["ssm_conv_w"] = lax.dynamic_slice(g_small["ssm_conv_w"].reshape(4, XBC_DIM), (0, me * n_ssm), (4, n_ssm))
    gw["ffn_conv_w"] = lax.dynamic_slice(g_small["ffn_conv_w"].reshape(3, 2 * D_FF), (0, me * n_ffn), (3, n_ffn))

    delta, new_m, new_v = {}, {}, {}
    for k, tr in (("w_in", 256), ("w_out", None), ("w_up", 256), ("w_down", None), ("w_ple_gate", None),
                  ("w_ple_proj", None)):
        delta[k], new_m[k], new_v[k] = _adamw(w[k], gw[k], m[k], v[k], "adamw_" + k, tr)
    packs = [_pack_flat(d, _SHARD_SMALL, _SHARD_SMALL_ROWS) for d in (w, gw, m, v)]
    for d, packed in zip((delta, new_m, new_v), _adamw(*packs, "adamw_small")):
        for k, a in _unpack_flat(packed, _SHARD_SMALL).items():
            d[k] = a.reshape(w[k].shape)

    outs = [loss, grad_x[None]]
    for d in (gw, delta, new_m, new_v):
        outs += [d[k][None] for k in names]
    return tuple(outs)
```

```python
import functools

import numpy as np
import jax
import jax.numpy as jnp
from jax import lax
from jax.experimental import pallas as pl
from jax.experimental.pallas import tpu as pltpu

F32 = jnp.float32
BF16 = jnp.bfloat16
SDS = jax.ShapeDtypeStruct
EPS = 1e-6
N_DEV = 8
D_MODEL = 1024
HEAD_DIM = 64
ATTN_DIM = 512
KV_DIM = 256
SSM_INNER = 1024
SSM_HEADS = 16
BC_DIM = 256
XBC_DIM = SSM_INNER + 2 * BC_DIM
MIX_DIM = ATTN_DIM + SSM_INNER
IN_PROJ = 3600
IN_PROJ_PAD = 3840
D_FF = 2816
PLE_DIM = 256
CHUNK = 128
SUPER = 2048
DILATIONS = (1, 4, 16)
TILE_UNROLL = 8
VMEM_LIMIT = 56 * 1024 * 1024
ADAM_LR, ADAM_B1, ADAM_B2, ADAM_EPS, ADAM_WD, ADAM_STEP = 0.001, 0.9, 0.999, 1e-08, 0.01, 10

NT = (((1,), (1,)), ((), ()))
TN = (((0,), (0,)), ((), ()))


def _params(*sem):
    return pltpu.CompilerParams(dimension_semantics=sem if sem else None, vmem_limit_bytes=VMEM_LIMIT)


def _dot(a, b, dims=None):
    if dims is None:
        return jnp.dot(a, b, preferred_element_type=F32)
    return lax.dot_general(a, b, dims, preferred_element_type=F32)


def _hdot(a, b, parts=2):
    a_exact = a.dtype == BF16
    x = b if a_exact else a
    acc = None
    for _ in range(parts):
        piece = x.astype(BF16)
        x = x - piece.astype(F32)
        d = _dot(a, piece) if a_exact else _dot(piece, b)
        acc = d if acc is None else acc + d
    return acc


def _sigmoid(x):
    return 0.5 * jnp.tanh(0.5 * x) + 0.5


def _shift_down(x, halo8, s):
    xr = pltpu.roll(x, s, 0)
    row = lax.broadcasted_iota(jnp.int32, halo8.shape, 0)
    first = jnp.where(row < s, pltpu.roll(halo8, s, 0), xr[0:8])
    return jnp.concatenate([first, xr[8:]], axis=0)


def _shift_up(x, halo8, s):
    n = x.shape[0]
    xr = pltpu.roll(x, n - s, 0)
    row = lax.broadcasted_iota(jnp.int32, halo8.shape, 0)
    last = jnp.where(row >= 8 - s, pltpu.roll(halo8, 8 - s, 0), xr[n - 8:])
    return jnp.concatenate([xr[:n - 8], last], axis=0)


def _norm_matmul(x, g, wt, tm, tn, name):
    t, k = x.shape
    n = wt.shape[0]

    def body(x_ref, g_ref, w_ref, o_ref, h_ref):
        @pl.when(pl.program_id(1) == 0)
        def _():
            xv = x_ref[...]
            r = lax.rsqrt(jnp.mean(xv * xv, axis=-1, keepdims=True) + EPS)
            h_ref[...] = (xv * r * g_ref[...]).astype(BF16)
        o_ref[...] = _dot(h_ref[...], w_ref[...], NT)

    return pl.pallas_call(
        body, name=name, grid=(t // tm, n // tn),
        in_specs=[pl.BlockSpec((tm, k), lambda i, j: (i, 0)), pl.BlockSpec((1, k), lambda i, j: (0, 0)),
                  pl.BlockSpec((tn, k), lambda i, j: (j, 0))],
        out_specs=[pl.BlockSpec((tm, tn), lambda i, j: (i, j)), pl.BlockSpec((tm, k), lambda i, j: (i, 0))],
        out_shape=[SDS((t, n), F32), SDS((t, k), BF16)],
        compiler_params=_params("parallel", "arbitrary"))(x, g, wt)


def _a_spec(a, lead, tm):
    if lead is None:
        return pl.BlockSpec((tm, a.shape[-1]), lambda i: (i, 0))
    return pl.BlockSpec((None, tm, a.shape[-1]), lambda i, _l=lead: (_l, i, 0))


def _mm_resid(pairs, res, dims, tm, out_dtype, name):
    t = pairs[0][0].shape[-2]
    n = pairs[0][2].shape[1] if dims is None else pairs[0][2].shape[0]
    np_ = len(pairs)

    def body(*refs):
        o_ref = refs[-1]
        acc = refs[2 * np_][...] if res is not None else None
        for q in range(np_):
            d = _dot(refs[q][...].astype(BF16), refs[np_ + q][...], dims)
            acc = d if acc is None else acc + d
        o_ref[...] = acc.astype(out_dtype)

    in_specs = [_a_spec(a, lead, tm) for a, lead, _ in pairs]
    in_specs += [pl.BlockSpec(b.shape, lambda i: (0, 0)) for _, _, b in pairs]
    args = [a for a, _, _ in pairs] + [b for _, _, b in pairs]
    if res is not None:
        in_specs.append(pl.BlockSpec((tm, n), lambda i: (i, 0)))
        args.append(res)
    return pl.pallas_call(
        body, name=name, grid=(t // tm,), in_specs=in_specs,
        out_specs=pl.BlockSpec((tm, n), lambda i: (i, 0)), out_shape=SDS((t, n), out_dtype),
        compiler_params=_params("parallel"))(*args)


def _mm_normbwd(pairs, x, g, dres, tm, name):
    t, k = x.shape
    np_ = len(pairs)

    def body(*refs):
        x_ref, g_ref, dres_ref, dx_ref, dg_ref = refs[2 * np_:]
        dh = None
        for q in range(np_):
            d = _dot(refs[q][...], refs[np_ + q][...])
            dh = d if dh is None else dh + d
        xv = x_ref[...]
        r = lax.rsqrt(jnp.mean(xv * xv, axis=-1, keepdims=True) + EPS)
        xh = xv * r

        @pl.when(pl.program_id(0) == 0)
        def _():
            dg_ref[...] = jnp.zeros_like(dg_ref)
        dg_ref[...] += jnp.sum(dh * xh, axis=0, keepdims=True)
        gd = dh * g_ref[...]
        dx_ref[...] = dres_ref[...] + r * (gd - xh * jnp.mean(gd * xh, axis=-1, keepdims=True))

    in_specs = [_a_spec(a, lead, tm) for a, lead, _ in pairs]
    in_specs += [pl.BlockSpec(b.shape, lambda i: (0, 0)) for _, _, b in pairs]
    in_specs += [pl.BlockSpec((tm, k), lambda i: (i, 0)), pl.BlockSpec((1, k), lambda i: (0, 0)),
                 pl.BlockSpec((tm, k), lambda i: (i, 0))]
    args = [a for a, _, _ in pairs] + [b for _, _, b in pairs] + [x, g, dres]
    return pl.pallas_call(
        body, name=name, grid=(t // tm,), in_specs=in_specs,
        out_specs=[pl.BlockSpec((tm, k), lambda i: (i, 0)), pl.BlockSpec((1, k), lambda i: (0, 0))],
        out_shape=[SDS((t, k), F32), SDS((1, k), F32)],
        compiler_params=_params("arbitrary"))(*args)


def _wgrad(a, a_lead, b, name, tk=512):
    t, m = a.shape[-2:]
    n = b.shape[1]
    tm = m if m <= 1024 else 1408
    assert m % tm == 0

    def body(a_ref, b_ref, o_ref):
        @pl.when(pl.program_id(1) == 0)
        def _():
            o_ref[...] = jnp.zeros_like(o_ref)
        o_ref[...] += _dot(a_ref[...].astype(BF16), b_ref[...].astype(BF16), TN)

    if a_lead is None:
        a_spec = pl.BlockSpec((tk, tm), lambda mi, ki: (ki, mi))
    else:
        a_spec = pl.BlockSpec((None, tk, tm), lambda mi, ki, _l=a_lead: (_l, ki, mi))
    return pl.pallas_call(
        body, name=name, grid=(m // tm, t // tk),
        in_specs=[a_spec, pl.BlockSpec((tk, n), lambda mi, ki: (ki, 0))],
        out_specs=pl.BlockSpec((tm, n), lambda mi, ki: (mi, 0)), out_shape=SDS((m, n), F32),
        compiler_params=_params("parallel", "arbitrary"))(a, b)


def _head_consts():
    iq = np.arange(ATTN_DIM)
    ik = np.arange(KV_DIM)
    ones_q = (iq[:, None] // HEAD_DIM == iq[None, :] // HEAD_DIM).astype(np.float32)
    ones_k = (ik[:, None] // HEAD_DIM == ik[None, :] // HEAD_DIM).astype(np.float32)
    dup = (ik[:, None] == (HEAD_DIM * (iq[None, :] // 128) + iq[None, :] % HEAD_DIM)).astype(np.float32)
    return jnp.asarray(ones_q, BF16), jnp.asarray(ones_k, BF16), jnp.asarray(dup, BF16), jnp.asarray(dup.T, BF16)


def _attn_prep(proj, gq, gk, ones_q, ones_k, dup, tm=512):
    t = proj.shape[0]

    def body(p_ref, gq_ref, gk_ref, oq_ref, ok_ref, dup_ref, qn_ref, kd_ref, vd_ref):
        q = p_ref[:, 0:ATTN_DIM]
        k = p_ref[:, ATTN_DIM:ATTN_DIM + KV_DIM]
        v = p_ref[:, ATTN_DIM + KV_DIM:]
        rq = lax.rsqrt(_hdot(q * q, oq_ref[...]) * (1.0 / HEAD_DIM) + EPS)
        qn_ref[...] = (q * rq * gq_ref[...]) * (HEAD_DIM ** -0.5)
        rk = lax.rsqrt(_hdot(k * k, ok_ref[...]) * (1.0 / HEAD_DIM) + EPS)
        kn = k * rk * gk_ref[...]
        kd_ref[...] = _dot(kn.astype(BF16), dup_ref[...])
        vd_ref[...] = _dot(v.astype(BF16), dup_ref[...])

    full = lambda a: pl.BlockSpec(a.shape, lambda i: (0, 0))
    o_spec = pl.BlockSpec((tm, ATTN_DIM), lambda i: (i, 0))
    return pl.pallas_call(
        body, name="attn_prep", grid=(t // tm,),
        in_specs=[pl.BlockSpec((tm, 1024), lambda i: (i, 0)), full(gq), full(gk), full(ones_q), full(ones_k), full(dup)],
        out_specs=[o_spec, o_spec, o_spec], out_shape=[SDS((t, ATTN_DIM), F32)] * 3,
        compiler_params=_params("parallel"))(proj, gq, gk, ones_q, ones_k, dup)


def _attn_prep_bwd(proj, dqn, dkc, dkp, dvc, dvp, gq, gk, ones_q, ones_k, dup_t, tm=512):
    t = proj.shape[0]
    nblk = t // tm
    off = SUPER // tm

    def body(p_ref, dqn_ref, dkc_ref, dkp_ref, dvc_ref, dvp_ref, gq_ref, gk_ref, oq_ref, ok_ref, dt_ref,
             o_ref, dgq_ref, dgk_ref):
        i = pl.program_id(0)
        has_next = (i + off < nblk).astype(F32)
        q = p_ref[:, 0:ATTN_DIM]
        k = p_ref[:, ATTN_DIM:ATTN_DIM + KV_DIM]
        dkn = _hdot(dkc_ref[...] + has_next * dkp_ref[...], dt_ref[...])
        dv = _hdot(dvc_ref[...] + has_next * dvp_ref[...], dt_ref[...])

        @pl.when(i == 0)
        def _():
            dgq_ref[...] = jnp.zeros_like(dgq_ref)
            dgk_ref[...] = jnp.zeros_like(dgk_ref)

        rq = lax.rsqrt(_hdot(q * q, oq_ref[...]) * (1.0 / HEAD_DIM) + EPS)
        xh = q * rq
        dy = dqn_ref[...] * (HEAD_DIM ** -0.5)
        dgq_ref[...] += jnp.sum(dy * xh, axis=0, keepdims=True)
        gd = dy * gq_ref[...]
        dq = rq * (gd - xh * (_hdot(gd * xh, oq_ref[...]) * (1.0 / HEAD_DIM)))
        rk = lax.rsqrt(_hdot(k * k, ok_ref[...]) * (1.0 / HEAD_DIM) + EPS)
        kh = k * rk
        dgk_ref[...] += jnp.sum(dkn * kh, axis=0, keepdims=True)
        gdk = dkn * gk_ref[...]
        dk = rk * (gdk - kh * (_hdot(gdk * kh, ok_ref[...]) * (1.0 / HEAD_DIM)))
        o_ref[:, 0:ATTN_DIM] = dq.astype(BF16)
        o_ref[:, ATTN_DIM:ATTN_DIM + KV_DIM] = dk.astype(BF16)
        o_ref[:, ATTN_DIM + KV_DIM:] = dv.astype(BF16)

    full = lambda a: pl.BlockSpec(a.shape, lambda i: (0, 0))
    cur = pl.BlockSpec((tm, ATTN_DIM), lambda i: (i, 0))
    nxt = pl.BlockSpec((tm, ATTN_DIM), lambda i: (jnp.minimum(i + off, nblk - 1), 0))
    return pl.pallas_call(
        body, name="attn_prep_bwd", grid=(nblk,),
        in_specs=[pl.BlockSpec((tm, 1024), lambda i: (i, 0)), cur, cur, nxt, cur, nxt,
                  full(gq), full(gk), full(ones_q), full(ones_k), full(dup_t)],
        out_specs=[pl.BlockSpec((tm, 1024), lambda i: (i, 0)), pl.BlockSpec((1, ATTN_DIM), lambda i: (0, 0)),
                   pl.BlockSpec((1, KV_DIM), lambda i: (0, 0))],
        out_shape=[SDS((t, 1024), BF16), SDS((1, ATTN_DIM), F32), SDS((1, KV_DIM), F32)],
        compiler_params=_params("arbitrary"))(proj, dqn, dkc, dkp, dvc, dvp, gq, gk, ones_q, ones_k, dup_t)


def _tile_masks():
    qi = lax.broadcasted_iota(jnp.int32, (2 * CHUNK, 2 * CHUNK), 0) & (CHUNK - 1)
    kj = lax.broadcasted_iota(jnp.int32, (2 * CHUNK, 2 * CHUNK), 1)
    delta = CHUNK + qi - kj
    band = (delta >= 0) & (delta <= CHUNK)
    return band, kj


def _deinterleave(dst, src, n_rows, d):
    per = n_rows // d
    for r in range(d):
        dst[r * per:(r + 1) * per, :] = src[pl.ds(r, per, stride=d), :]


def _attn_specs(t):
    blk = lambda f: pl.BlockSpec((SUPER, 128), f)
    cur = blk(lambda h, s: (s, h))
    prev = blk(lambda h, s: (jnp.maximum(s - 1, 0), h))
    return cur, prev


def _attn_fwd(qn, kd, vd):
    t = qn.shape[0]
    cur, prev = _attn_specs(t)

    def body(q_ref, kp_ref, kc_ref, vp_ref, vc_ref, o_ref, lse_ref, kk, vv, qd, kdd, vdd, po, pm, pll, acc, mm, ll):
        s = pl.program_id(1)
        kk[0:SUPER, :] = kp_ref[...]
        kk[SUPER:, :] = kc_ref[...]
        vv[0:SUPER, :] = vp_ref[...]
        vv[SUPER:, :] = vc_ref[...]
        m0 = lax.broadcasted_iota(jnp.int32, (CHUNK, 128), 1) < HEAD_DIM
        band, kj = _tile_masks()
        for d in DILATIONS:
            lq = SUPER // d
            if d == 1:
                qs_ref, ks_ref, vs_ref = q_ref, kk, vv
            else:
                _deinterleave(qd, q_ref, SUPER, d)
                _deinterleave(kdd, kk, 2 * SUPER, d)
                _deinterleave(vdd, vv, 2 * SUPER, d)
                qs_ref, ks_ref, vs_ref = qd, kdd, vdd

            def tile(ti, carry):
                r = ti // (lq // CHUNK)
                nb = ti % (lq // CHUNK)
                qoff = ti * CHUNK
                koff = r * 2 * lq + lq + (nb - 1) * CHUNK
                qt = qs_ref[pl.ds(qoff, CHUNK), :]
                qs = jnp.concatenate([jnp.where(m0, qt, 0.0), jnp.where(m0, 0.0, qt)], axis=0).astype(BF16)
                kt = ks_ref[pl.ds(koff, 2 * CHUNK), :].astype(BF16)
                vt = vs_ref[pl.ds(koff, 2 * CHUNK), :].astype(BF16)
                sc = _dot(qs, kt, NT)
                ok = band if nb > 0 else band & (kj >= jnp.where(s > 0, 0, CHUNK))
                sc = jnp.where(ok, sc, -jnp.inf)
                mt = jnp.max(sc, axis=-1, keepdims=True)
                p = jnp.exp(sc - mt)
                lt = jnp.sum(p, axis=-1, keepdims=True)
                ot = _dot(p.astype(BF16), vt)
                po[pl.ds(qoff, CHUNK), :] = jnp.where(m0, ot[:CHUNK], ot[CHUNK:])
                pm[pl.ds(qoff, CHUNK), :] = jnp.where(m0, mt[:CHUNK], mt[CHUNK:])
                pll[pl.ds(qoff, CHUNK), :] = jnp.where(m0, lt[:CHUNK], lt[CHUNK:])
                return carry

            for ti in range(SUPER // CHUNK):
                tile(ti, 0)
            if d == 1:
                acc[...] = po[...]
                mm[...] = pm[...]
                ll[...] = pll[...]
            else:
                for r in range(d):
                    rows = pl.ds(r, lq, stride=d)
                    seg = slice(r * lq, (r + 1) * lq)
                    m_old, m_new = mm[rows, :], pm[seg, :]
                    m_all = jnp.maximum(m_old, m_new)
                    a, b = jnp.exp(m_old - m_all), jnp.exp(m_new - m_all)
                    acc[rows, :] = acc[rows, :] * a + po[seg, :] * b
                    ll[rows, :] = ll[rows, :] * a + pll[seg, :] * b
                    mm[rows, :] = m_all
        o_ref[...] = acc[...] / ll[...]
        lse_ref[...] = mm[...] + jnp.log(ll[...])

    big = pltpu.VMEM((2 * SUPER, 128), F32)
    one = pltpu.VMEM((SUPER, 128), F32)
    return pl.pallas_call(
        body, name="attn_fwd", grid=(4, t // SUPER),
        in_specs=[cur, prev, cur, prev, cur], out_specs=[cur, cur],
        out_shape=[SDS((t, ATTN_DIM), F32)] * 2,
        scratch_shapes=[big, big, one, big, big, one, one, one, one, one, one],
        compiler_params=_params("parallel", "arbitrary"))(qn, kd, kd, vd, vd)


def _attn_bwd(qn, kd, vd, out, lse, dout, ones_pair):
    t = qn.shape[0]
    cur, prev = _attn_specs(t)

    def body(q_ref, kp_ref, kc_ref, vp_ref, vc_ref, o_ref, lse_ref, do_ref, ones_ref,
             dq_ref, dkc_ref, dkp_ref, dvc_ref, dvp_ref,
             kk, vv, od, ld, kb, vb, qsb, dosb, tk, tv, pdq, delta):
        s = pl.program_id(1)
        delta[...] = _hdot(do_ref[...] * o_ref[...], ones_ref[...])

        def per_row(a):
            ar = pltpu.roll(a, HEAD_DIM, 1)
            rows = jnp.concatenate([jnp.where(m0, a, ar), jnp.where(m0, ar, a)], axis=0)
            return jnp.concatenate([rows, rows], axis=1)

        kk[0:SUPER, :] = kp_ref[...]
        kk[SUPER:, :] = kc_ref[...]
        vv[0:SUPER, :] = vp_ref[...]
        vv[SUPER:, :] = vc_ref[...]
        for ref in (dq_ref, dkc_ref, dkp_ref, dvc_ref, dvp_ref):
            ref[...] = jnp.zeros_like(ref)
        m0 = lax.broadcasted_iota(jnp.int32, (CHUNK, 128), 1) < HEAD_DIM
        band, kj = _tile_masks()
        ninf = -jnp.inf
        for d in DILATIONS:
            lq = SUPER // d
            nblk = lq // CHUNK
            for r in range(d):
                seg = slice(r * 2 * lq, (r + 1) * 2 * lq)
                kb[seg, :] = kk[pl.ds(r, 2 * lq, stride=d), :].astype(BF16)
                vb[seg, :] = vv[pl.ds(r, 2 * lq, stride=d), :].astype(BF16)
            for ti in range(SUPER // CHUNK):
                rows = pl.ds(ti // nblk + d * CHUNK * (ti % nblk), CHUNK, stride=d)
                for src, dst in ((q_ref, qsb), (do_ref, dosb)):
                    a = src[rows, :]
                    dst[ti * 2 * CHUNK:(ti + 1) * 2 * CHUNK, :] = jnp.concatenate(
                        [jnp.where(m0, a, 0.0), jnp.where(m0, 0.0, a)], axis=0).astype(BF16)
                ld[ti * CHUNK:(ti + 1) * CHUNK, :] = lse_ref[rows, :]
                od[ti * CHUNK:(ti + 1) * CHUNK, :] = delta[rows, :]

            def operands(ti):
                r, nb = ti // nblk, ti % nblk
                stacked = slice(ti * 2 * CHUNK, (ti + 1) * 2 * CHUNK)
                krows = pl.ds(r * 2 * lq + lq + (nb - 1) * CHUNK, 2 * CHUNK)
                return stacked, krows

            def scores(ti):
                stacked, krows = operands(ti)
                kt = kb[krows, :]
                return dict(ti=ti, sc=_dot(qsb[stacked, :], kt, NT), dp=_dot(dosb[stacked, :], vb[krows, :], NT))

            def softmax_grad(c):
                qrows = slice(c["ti"] * CHUNK, (c["ti"] + 1) * CHUNK)
                ok = band if c["ti"] % nblk > 0 else band & (kj >= jnp.where(s > 0, 0, CHUNK))
                p = jnp.exp(jnp.where(ok, c.pop("sc"), ninf) - per_row(ld[qrows, :]))
                ds = p * (c.pop("dp") - per_row(od[qrows, :]))
                c.update(p=p.astype(BF16), ds=ds.astype(BF16))
                return c

            def grads(c):
                ti = c["ti"]
                stacked, krows = operands(ti)
                dqs = _dot(c["ds"], kb[krows, :])
                pdq[ti * CHUNK:(ti + 1) * CHUNK, :] = jnp.where(m0, dqs[:CHUNK], dqs[CHUNK:])
                tk[stacked, :] = _dot(c["ds"], qsb[stacked, :], TN)
                tv[stacked, :] = _dot(c["p"], dosb[stacked, :], TN)

            n_tiles = SUPER // CHUNK
            stage_a, stage_b = None, None
            for ti in range(n_tiles + 2):
                if stage_b is not None:
                    grads(stage_b)
                stage_b = softmax_grad(stage_a) if stage_a is not None else None
                stage_a = scores(ti) if ti < n_tiles else None

            for r in range(d):
                dq_ref[pl.ds(r, lq, stride=d), :] += pdq[r * lq:(r + 1) * lq, :]
                for tile_out, cur_ref, prev_ref in ((tk, dkc_ref, dkp_ref), (tv, dvc_ref, dvp_ref)):
                    first = r * nblk * 2 * CHUNK
                    prev_ref[pl.ds(SUPER - CHUNK * d + r, CHUNK, stride=d), :] += tile_out[first:first + CHUNK, :]
                    for nb in range(nblk):
                        at = (r * nblk + nb) * 2 * CHUNK
                        part = tile_out[at + CHUNK:at + 2 * CHUNK, :]
                        if nb + 1 < nblk:
                            part = part + tile_out[at + 2 * CHUNK:at + 3 * CHUNK, :]
                        cur_ref[pl.ds(r + d * nb * CHUNK, CHUNK, stride=d), :] += part

    big = pltpu.VMEM((2 * SUPER, 128), F32)
    one = pltpu.VMEM((SUPER, 128), F32)
    half = pltpu.VMEM((2 * SUPER, 128), BF16)
    return pl.pallas_call(
        body, name="attn_bwd", grid=(4, t // SUPER),
        in_specs=[cur, prev, cur, prev, cur, cur, cur, cur, pl.BlockSpec((128, 128), lambda h, s: (0, 0))],
        out_specs=[cur] * 5, out_shape=[SDS((t, ATTN_DIM), F32)] * 5,
        scratch_shapes=[big, big, one, one, half, half, half, half, big, big, one, one],
        compiler_params=_params("parallel", "arbitrary"))(qn, kd, kd, vd, vd, out, lse, dout, ones_pair)


def _ssd_consts():
    tri = np.tril(np.ones((CHUNK, CHUNK), np.float32))
    expand = np.zeros((128, SSM_INNER), np.float32)
    for h in range(SSM_HEADS):
        expand[h, h * HEAD_DIM:(h + 1) * HEAD_DIM] = 1.0
    return jnp.asarray(tri, BF16), jnp.asarray(tri.T, BF16), jnp.asarray(expand, BF16), jnp.asarray(expand.T, BF16)


def _conv4(x, halo, w_ref, b_ref):
    acc = b_ref[...] + w_ref[3:4, :] * x
    for k in range(3):
        acc = acc + w_ref[k:k + 1, :] * _shift_down(x, halo, 3 - k)
    return acc


def _softplus(x):
    return jnp.maximum(x, 0.0) + jnp.log(1.0 + jnp.exp(-jnp.abs(x)))


def _ssd_common(xs_ref, bc_ref, dt_ref, hx_ref, hb_ref, cwx_ref, cbx_ref, cwb_ref, cbb_ref, dtb_ref, alog_ref,
                tri_ref, exp_ref, first):
    keep = 1.0 - first.astype(F32)
    hx = hx_ref[...] * keep
    hb = hb_ref[...] * keep
    pre_x = _conv4(xs_ref[...], hx, cwx_ref, cbx_ref)
    pre_b = _conv4(bc_ref[...], hb, cwb_ref, cbb_ref)
    xa = pre_x * _sigmoid(pre_x)
    ba = pre_b * _sigmoid(pre_b)
    dtv = _softplus(dt_ref[...] + dtb_ref[...])
    a_neg = -jnp.exp(alog_ref[...])
    acum = _hdot(tri_ref[...], dtv * a_neg, parts=3)
    lam = jnp.exp(acum)
    gam = jnp.exp(acum[CHUNK - 1:CHUNK, :] - acum)
    dt_e = _hdot(dtv, exp_ref[...])
    lam_e = _hdot(lam, exp_ref[...])
    gam_e = _hdot(gam, exp_ref[...])
    return dict(hx=hx, hb=hb, pre_x=pre_x, pre_b=pre_b, xa=xa, ba=ba, dtv=dtv, a_neg=a_neg, acum=acum,
                dt_e=dt_e, lam_e=lam_e, gam_e=gam_e, xdt=xa * dt_e)


def _decay(acum_t, h, transposed):
    rb = jnp.broadcast_to(acum_t[h:h + 1, :], (CHUNK, CHUNK))
    ri = lax.broadcasted_iota(jnp.int32, (CHUNK, CHUNK), 0)
    ci = lax.broadcasted_iota(jnp.int32, (CHUNK, CHUNK), 1)
    if transposed:
        return jnp.exp(jnp.where(ci >= ri, rb - rb.T, -jnp.inf))
    return jnp.exp(jnp.where(ri >= ci, rb.T - rb, -jnp.inf))


def _ssd_specs(t, rev):
    nc = t // CHUNK
    ch = (lambda c: nc - 1 - c) if rev else (lambda c: c)
    col = lambda w, j: pl.BlockSpec((CHUNK, w), lambda c: (ch(c), j))
    halo = lambda w, j: pl.BlockSpec((8, w), lambda c: (jnp.maximum(ch(c) * (CHUNK // 8) - 1, 0), j))
    return nc, ch, col, halo


def _ssd_fwd(proj, cwx, cbx, cwb, cbb, dtb, alog, dsk_e, norm_g, tri, expand):
    t = proj.shape[0]
    nc, _, col, halo = _ssd_specs(t, False)

    def body(z_ref, xs_ref, bc_ref, dt_ref, hx_ref, hb_ref, cwx_ref, cbx_ref, cwb_ref, cbb_ref, dtb_ref, alog_ref,
             dsk_ref, g_ref, tri_ref, exp_ref, y_ref, hs_ref, o_ref, state):
        c = pl.program_id(0)

        @pl.when(c == 0)
        def _():
            state[...] = jnp.zeros_like(state)

        v = _ssd_common(xs_ref, bc_ref, dt_ref, hx_ref, hb_ref, cwx_ref, cbx_ref, cwb_ref, cbb_ref, dtb_ref,
                        alog_ref, tri_ref, exp_ref, c == 0)
        acum_t = v["acum"].T
        xdt, ba = v["xdt"], v["ba"]
        h_in = state[...]
        hs_ref[0] = h_in
        xg = xdt * v["gam_e"]
        m0 = lax.broadcasted_iota(jnp.int32, (CHUNK, 128), 1) < HEAD_DIM
        for g in range(2):
            bg = ba[:, g * 128:(g + 1) * 128].astype(BF16)
            cg = ba[:, 256 + g * 128:256 + (g + 1) * 128].astype(BF16)
            gl = slice(g * 512, (g + 1) * 512)
            cb = _dot(cg, bg, NT)
            y_off = _dot(cg, h_in[:, gl].astype(BF16)) * v["lam_e"][:, gl]
            s_new = _dot(bg.T, xg[:, gl].astype(BF16))
            state[:, gl] = h_in[:, gl] * v["lam_e"][CHUNK - 1:CHUNK, gl] + s_new
            for j in range(4):
                h0 = 8 * g + 2 * j
                ln = slice(g * 512 + j * 128, g * 512 + (j + 1) * 128)
                xp = xdt[:, ln].astype(BF16)
                y0 = _dot((cb * _decay(acum_t, h0, False)).astype(BF16), xp)
                y1 = _dot((cb * _decay(acum_t, h0 + 1, False)).astype(BF16), xp)
                y_ref[:, ln] = jnp.where(m0, y0, y1) + y_off[:, j * 128:(j + 1) * 128]
        z = z_ref[...]
        yg = (y_ref[...] + dsk_ref[...] * v["xa"]) * (z * _sigmoid(z))
        r = lax.rsqrt(jnp.mean(yg * yg, axis=-1, keepdims=True) + EPS)
        o_ref[...] = (yg * r * g_ref[...]).astype(BF16)

    full = lambda a: pl.BlockSpec(a.shape, lambda c: (0,) * a.ndim)
    return pl.pallas_call(
        body, name="ssd_fwd", grid=(nc,),
        in_specs=[col(1024, 1), col(1024, 2), col(512, 6), col(128, 28), halo(1024, 2), halo(512, 6),
                  full(cwx), full(cbx), full(cwb), full(cbb), full(dtb), full(alog), full(dsk_e), full(norm_g),
                  full(tri), full(expand)],
        out_specs=[pl.BlockSpec((CHUNK, SSM_INNER), lambda c: (c, 0)),
                   pl.BlockSpec((1, 128, SSM_INNER), lambda c: (c, 0, 0)),
                   pl.BlockSpec((CHUNK, SSM_INNER), lambda c: (c, 0))],
        out_shape=[SDS((t, SSM_INNER), F32), SDS((nc, 128, SSM_INNER), F32), SDS((t, SSM_INNER), BF16)],
        scratch_shapes=[pltpu.VMEM((128, SSM_INNER), F32)],
        compiler_params=_params("arbitrary"))(proj, proj, proj, proj, proj, proj, cwx, cbx, cwb, cbb, dtb, alog,
                                              dsk_e, norm_g, tri, expand)


def _ssd_bwd(proj, y_ssd, hs, dout, cwx, cbx, cwb, cbb, dtb, alog, dsk_e, norm_g, tri, triu, expand, expand_t):
    t = proj.shape[0]
    nc, ch, col, halo = _ssd_specs(t, True)

    def body(z_ref, xs_ref, bc_ref, dt_ref, hx_ref, hb_ref, y_ref, hin_ref, do_ref,
             cwx_ref, cbx_ref, cwb_ref, cbb_ref, dtb_ref, alog_ref, dsk_ref, g_ref, tri_ref, triu_ref, exp_ref, expt_ref,
             dz_ref, dxs_ref, dbc_ref, ddt_ref, dg_ref, ddsk_ref, dalog_ref, ddtb_ref, dcwx_ref, dcbx_ref, dcwb_ref,
             dcbb_ref, gstate, nx_x, nx_b, dact_b, dxdt_s):
        step = pl.program_id(0)
        c = nc - 1 - step

        @pl.when(step == 0)
        def _():
            gstate[...] = jnp.zeros_like(gstate)
            nx_x[...] = jnp.zeros_like(nx_x)
            nx_b[...] = jnp.zeros_like(nx_b)
            for ref in (dg_ref, ddsk_ref, dalog_ref, ddtb_ref, dcwx_ref, dcbx_ref, dcwb_ref, dcbb_ref):
                ref[...] = jnp.zeros_like(ref)

        v = _ssd_common(xs_ref, bc_ref, dt_ref, hx_ref, hb_ref, cwx_ref, cbx_ref, cwb_ref, cbb_ref, dtb_ref,
                        alog_ref, tri_ref, exp_ref, c == 0)
        acum_t = v["acum"].T
        xa, ba, xdt, dtv = v["xa"], v["ba"], v["xdt"], v["dtv"]
        lam_e, gam_e, dt_e = v["lam_e"], v["gam_e"], v["dt_e"]
        z = z_ref[...]
        y = y_ref[...]
        sz = _sigmoid(z)
        zs = z * sz
        y_tot = y + dsk_ref[...] * xa
        yg = y_tot * zs
        r = lax.rsqrt(jnp.mean(yg * yg, axis=-1, keepdims=True) + EPS)
        yh = yg * r
        do = do_ref[...]
        dg_ref[...] += jnp.sum(do * yh, axis=0, keepdims=True)
        gd = do * g_ref[...]
        dyg = r * (gd - yh * jnp.mean(gd * yh, axis=-1, keepdims=True))
        dz_ref[...] = (dyg * y_tot * (sz * (1.0 + z * (1.0 - sz)))).astype(BF16)
        dy = dyg * zs
        ddsk_ref[...] += jnp.sum(dy * xa, axis=0, keepdims=True)
        g_out = gstate[...]
        h_in = hin_ref[0]
        lam_dy = lam_e * dy
        gam_x = gam_e * xdt
        m0 = lax.broadcasted_iota(jnp.int32, (CHUNK, 128), 1) < HEAD_DIM
        lane = lax.broadcasted_iota(jnp.int32, (CHUNK, 128), 1)
        below = (lax.broadcasted_iota(jnp.int32, (CHUNK, CHUNK), 0) >
                 lax.broadcasted_iota(jnp.int32, (CHUNK, CHUNK), 1))
        da_in = jnp.zeros((CHUNK, 128), F32)
        off_y, off_x = [], []
        for g in range(2):
            bg = ba[:, g * 128:(g + 1) * 128].astype(BF16)
            cg = ba[:, 256 + g * 128:256 + (g + 1) * 128].astype(BF16)
            gl = slice(g * 512, (g + 1) * 512)
            gg = g_out[:, gl].astype(BF16)
            bc_t = _dot(bg, cg, NT)
            cb = _dot(cg, bg, NT)
            dxdt_off = _dot(bg, gg) * gam_e[:, gl]
            off_x.append(xdt[:, gl] * dxdt_off)
            off_y.append(dy[:, gl] * (_dot(cg, h_in[:, gl].astype(BF16)) * lam_e[:, gl]))
            q_sum = jnp.zeros((CHUNK, CHUNK), F32)
            for j in range(4):
                h0 = 8 * g + 2 * j
                ln = slice(g * 512 + j * 128, g * 512 + (j + 1) * 128)
                dyp = dy[:, ln]
                dyb = dyp.astype(BF16)
                xpb = xdt[:, ln].astype(BF16)
                d0 = _dot((bc_t * _decay(acum_t, h0, True)).astype(BF16), dyb)
                d1 = _dot((bc_t * _decay(acum_t, h0 + 1, True)).astype(BF16), dyb)
                dxdt_s[:, ln] = jnp.where(m0, d0, d1) + dxdt_off[:, j * 128:(j + 1) * 128]
                for hh, dym in ((h0, jnp.where(m0, dyp, 0.0)), (h0 + 1, jnp.where(m0, 0.0, dyp))):
                    qd = _dot(dym.astype(BF16), xpb, NT) * _decay(acum_t, hh, False)
                    q_sum = q_sum + qd
                    reach = jnp.where(below, _hdot(triu_ref[...], qd * cb), 0.0)
                    da_in = jnp.where(lane == hh, jnp.sum(reach, axis=-1, keepdims=True), da_in)
            gstate[:, gl] = g_out[:, gl] * lam_e[CHUNK - 1:CHUNK, gl] + _dot(cg.T, lam_dy[:, gl].astype(BF16))
            qb = q_sum.astype(BF16)
            dact_b[:, 256 + g * 128:256 + (g + 1) * 128] = (
                _dot(qb, bg) + _dot(lam_dy[:, gl].astype(BF16), h_in[:, gl].astype(BF16), NT))
            dact_b[:, g * 128:(g + 1) * 128] = _dot(qb.T, cg) + _dot(gam_x[:, gl].astype(BF16), gg, NT)
        dxdt = dxdt_s[...]
        seg_y = _hdot(jnp.concatenate(off_y, axis=1), expt_ref[...])
        seg_x = _hdot(jnp.concatenate(off_x, axis=1), expt_ref[...])
        e_col = jnp.sum(g_out * h_in * lam_e[CHUNK - 1:CHUNK, :], axis=0, keepdims=True)
        e_seg = _hdot(jnp.broadcast_to(e_col, (8, SSM_INNER)), expt_ref[...])[0:1, :]
        da = da_in + _hdot(triu_ref[...], seg_y) + (_hdot(tri_ref[...], seg_x) - seg_x) + e_seg
        a_neg = v["a_neg"]
        ddtv = da * a_neg + _hdot(dxdt * xa, expt_ref[...])
        dalog_ref[...] += jnp.sum(da * dtv, axis=0, keepdims=True) * a_neg
        lane16 = lax.broadcasted_iota(jnp.int32, (CHUNK, 128), 1) < SSM_HEADS
        draw = jnp.where(lane16, ddtv * _sigmoid(dt_ref[...] + dtb_ref[...]), 0.0)
        ddtb_ref[...] += jnp.sum(draw, axis=0, keepdims=True)
        ddt_ref[...] = draw.astype(BF16)
        dxa = dxdt * dt_e + dy * dsk_ref[...]
        for (dact, pre, x_ref, nx, cw_ref, dcw_ref, dcb_ref, dx_ref) in (
                (dxa, v["pre_x"], xs_ref, nx_x, cwx_ref, dcwx_ref, dcbx_ref, dxs_ref),
                (dact_b[...], v["pre_b"], bc_ref, nx_b, cwb_ref, dcwb_ref, dcbb_ref, dbc_ref)):
            sp = _sigmoid(pre)
            dpre = dact * (sp * (1.0 + pre * (1.0 - sp)))
            dcb_ref[...] += jnp.sum(dpre, axis=0, keepdims=True)
            xv = x_ref[...]
            nxt = nx[...]
            dx = cw_ref[3:4, :] * dpre
            dcw_ref[3:4, :] += jnp.sum(dpre * xv, axis=0, keepdims=True)
            for k in range(3):
                d_up = _shift_up(dpre, nxt, 3 - k)
                dcw_ref[k:k + 1, :] += jnp.sum(xv * d_up, axis=0, keepdims=True)
                dx = dx + cw_ref[k:k + 1, :] * d_up
            nx[...] = dpre[0:8, :]
            dx_ref[...] = dx.astype(dx_ref.dtype)

    full = lambda a: pl.BlockSpec(a.shape, lambda c: (0,) * a.ndim)
    rowblk = lambda w: pl.BlockSpec((CHUNK, w), lambda c: (ch(c), 0))
    acc = lambda a, b: pl.BlockSpec((a, b), lambda c: (0, 0))
    return pl.pallas_call(
        body, name="ssd_bwd", grid=(nc,),
        in_specs=[col(1024, 1), col(1024, 2), col(512, 6), col(128, 28), halo(1024, 2), halo(512, 6),
                  rowblk(SSM_INNER),
                  pl.BlockSpec((1, 128, SSM_INNER), lambda c: (ch(c), 0, 0)),
                  rowblk(SSM_INNER),
                  full(cwx), full(cbx), full(cwb), full(cbb), full(dtb), full(alog), full(dsk_e), full(norm_g),
                  full(tri), full(triu), full(expand), full(expand_t)],
        out_specs=[rowblk(SSM_INNER), rowblk(SSM_INNER), rowblk(512), rowblk(128),
                   acc(1, 1024), acc(1, 1024), acc(1, 128), acc(1, 128), acc(4, 1024), acc(1, 1024), acc(4, 512),
                   acc(1, 512)],
        out_shape=[SDS((t, SSM_INNER), BF16), SDS((t, SSM_INNER), BF16), SDS((t, 512), BF16), SDS((t, 128), BF16),
                   SDS((1, 1024), F32), SDS((1, 1024), F32), SDS((1, 128), F32), SDS((1, 128), F32),
                   SDS((4, 1024), F32), SDS((1, 1024), F32), SDS((4, 512), F32), SDS((1, 512), F32)],
        scratch_shapes=[pltpu.VMEM((128, SSM_INNER), F32), pltpu.VMEM((8, 1024), F32), pltpu.VMEM((8, 512), F32),
                        pltpu.VMEM((CHUNK, 512), F32), pltpu.VMEM((CHUNK, SSM_INNER), F32)],
        compiler_params=_params("arbitrary"))(proj, proj, proj, proj, proj, proj, y_ssd, hs, dout,
                                              cwx, cbx, cwb, cbb, dtb, alog, dsk_e, norm_g, tri, triu, expand,
                                              expand_t)


def _conv3(x, halo, w_ref, b_ref, part):
    acc = b_ref[part] + w_ref[2, part] * x
    for k in range(2):
        acc = acc + w_ref[k, part] * _shift_down(x, halo, 2 - k)
    return acc


def _up_act(x, g, w_up_t, cw, cb, tm=2048, tn=256):
    t, k = x.shape
    nj = D_FF // tn

    def body(x_ref, g_ref, wg_ref, wv_ref, w_ref, b_ref, u_ref, h_ref, f_ref, halo):
        i, j = pl.program_id(0), pl.program_id(1)

        @pl.when(j == 0)
        def _():
            xv = x_ref[...]
            r = lax.rsqrt(jnp.mean(xv * xv, axis=-1, keepdims=True) + EPS)
            h_ref[...] = (xv * r * g_ref[...]).astype(BF16)

        @pl.when(i == 0)
        def _():
            halo[j] = jnp.zeros((2, 8, tn), F32)

        parts = []
        for part, wt_ref in enumerate((wg_ref, wv_ref)):
            u = _dot(h_ref[...], wt_ref[...], NT)
            u_ref[part] = u
            parts.append(_conv3(u, halo[j, part], w_ref, b_ref, part))
            halo[j, part] = u[tm - 8:, :]
        gate, val = parts
        f_ref[...] = (gate * _sigmoid(gate) * val).astype(BF16)

    return pl.pallas_call(
        body, name="up_proj", grid=(t // tm, nj),
        in_specs=[pl.BlockSpec((tm, k), lambda i, j: (i, 0)), pl.BlockSpec((1, k), lambda i, j: (0, 0)),
                  pl.BlockSpec((tn, k), lambda i, j: (j, 0)), pl.BlockSpec((tn, k), lambda i, j: (j + nj, 0)),
                  pl.BlockSpec((3, 2, 1, tn), lambda i, j: (0, 0, 0, j)), pl.BlockSpec((2, 1, tn), lambda i, j: (0, 0, j))],
        out_specs=[pl.BlockSpec((2, tm, tn), lambda i, j: (0, i, j)), pl.BlockSpec((tm, k), lambda i, j: (i, 0)),
                   pl.BlockSpec((tm, tn), lambda i, j: (i, j))],
        out_shape=[SDS((2, t, D_FF), F32), SDS((t, k), BF16), SDS((t, D_FF), BF16)],
        scratch_shapes=[pltpu.VMEM((nj, 2, 8, tn), F32)],
        compiler_params=_params("arbitrary", "arbitrary"))(x, g, w_up_t, w_up_t, cw, cb)


def _ffn_bwd(dx2, w_down, u, cw, cb, tm=512, tn=1408):
    t = u.shape[1]
    nj, ni = D_FF // tn, t // tm
    rev = lambda i: ni - 1 - i

    def body(dx_ref, wd_ref, u_ref, uh_ref, w_ref, b_ref, du_ref, dcw_ref, dcb_ref, nxt):
        i = pl.program_id(1)

        @pl.when(i == 0)
        def _():
            nxt[...] = jnp.zeros_like(nxt)
            dcw_ref[...] = jnp.zeros_like(dcw_ref)
            dcb_ref[...] = jnp.zeros_like(dcb_ref)

        df = _dot(dx_ref[...].astype(BF16), wd_ref[...], NT)
        keep = (i < ni - 1).astype(F32)
        ug, uv = u_ref[0], u_ref[1]
        hg, hv = uh_ref[0] * keep, uh_ref[1] * keep
        gate = _conv3(ug, hg, w_ref, b_ref, 0)
        val = _conv3(uv, hv, w_ref, b_ref, 1)
        sg = _sigmoid(gate)
        dgate = df * val * (sg * (1.0 + gate * (1.0 - sg)))
        dval = df * (gate * sg)
        for part, (d, uu) in enumerate(((dgate, ug), (dval, uv))):
            dcb_ref[part] += jnp.sum(d, axis=0, keepdims=True)
            ahead = nxt[part]
            acc = w_ref[2, part] * d
            dcw_ref[2, part] += jnp.sum(d * uu, axis=0, keepdims=True)
            for k in range(2):
                d_up = _shift_up(d, ahead, 2 - k)
                dcw_ref[k, part] += jnp.sum(uu * d_up, axis=0, keepdims=True)
                acc = acc + w_ref[k, part] * d_up
            nxt[part] = d[0:8, :]
            du_ref[part] = acc.astype(BF16)

    w_spec = pl.BlockSpec((3, 2, 1, tn), lambda j, i: (0, 0, 0, j))
    b_spec = pl.BlockSpec((2, 1, tn), lambda j, i: (0, 0, j))
    return pl.pallas_call(
        body, name="ffn_bwd", grid=(nj, ni),
        in_specs=[pl.BlockSpec((tm, D_MODEL), lambda j, i: (rev(i), 0)), pl.BlockSpec((tn, D_MODEL), lambda j, i: (j, 0)),
                  pl.BlockSpec((2, tm, tn), lambda j, i: (0, rev(i), j)),
                  pl.BlockSpec((2, 8, tn), lambda j, i: (0, jnp.maximum(rev(i) * (tm // 8) - 1, 0), j)),
                  w_spec, b_spec],
        out_specs=[pl.BlockSpec((2, tm, tn), lambda j, i: (0, rev(i), j)), w_spec, b_spec],
        out_shape=[SDS((2, t, D_FF), BF16), SDS((3, 2, 1, D_FF), F32), SDS((2, 1, D_FF), F32)],
        scratch_shapes=[pltpu.VMEM((2, 8, tn), F32)],
        compiler_params=_params("parallel", "arbitrary"))(dx2, w_down, u, u, cw, cb)


def _ple_loss(x2, g, w_gate, p, w_proj_t, target, tm=256):
    t = x2.shape[0]

    def body(x_ref, g_ref, wg_ref, p_ref, wp_ref, tg_ref, dx_ref, dpre_ref, dpp_ref, h_ref, loss_ref, dg_ref):
        i = pl.program_id(0)
        xv = x_ref[...]
        r = lax.rsqrt(jnp.mean(xv * xv, axis=-1, keepdims=True) + EPS)
        xh = xv * r
        h = (xh * g_ref[...]).astype(BF16)
        h_ref[...] = h
        gate = _sigmoid(_dot(h, wg_ref[...]))
        pp = _dot(p_ref[...].astype(BF16), wp_ref[...], NT)
        err = (xv + gate * pp) - tg_ref[...]

        @pl.when(i == 0)
        def _():
            loss_ref[...] = jnp.zeros_like(loss_ref)
            dg_ref[...] = jnp.zeros_like(dg_ref)

        loss_ref[...] += 0.5 * jnp.sum(jnp.mean(err * err, axis=-1, keepdims=True), axis=0, keepdims=True)
        dy = err * (1.0 / D_MODEL)
        dpre = (dy * pp * gate * (1.0 - gate)).astype(BF16)
        dpre_ref[...] = dpre
        dpp_ref[...] = (dy * gate).astype(BF16)
        dh = _dot(dpre, wg_ref[...], NT)
        dg_ref[...] += jnp.sum(dh * xh, axis=0, keepdims=True)
        gd = dh * g_ref[...]
        dx_ref[...] = dy + r * (gd - xh * jnp.mean(gd * xh, axis=-1, keepdims=True))

    row = lambda w: pl.BlockSpec((tm, w), lambda i: (i, 0))
    full = lambda a: pl.BlockSpec(a.shape, lambda i: (0, 0))
    return pl.pallas_call(
        body, name="ple_loss", grid=(t // tm,),
        in_specs=[row(D_MODEL), full(g), full(w_gate), row(PLE_DIM), full(w_proj_t), row(D_MODEL)],
        out_specs=[row(D_MODEL), row(D_MODEL), row(D_MODEL), row(D_MODEL),
                   pl.BlockSpec((1, 128), lambda i: (0, 0)), pl.BlockSpec((1, D_MODEL), lambda i: (0, 0))],
        out_shape=[SDS((t, D_MODEL), F32), SDS((t, D_MODEL), BF16), SDS((t, D_MODEL), BF16), SDS((t, D_MODEL), BF16),
                   SDS((1, 128), F32), SDS((1, D_MODEL), F32)],
        compiler_params=_params("arbitrary"))(x2, g, w_gate, p, w_proj_t, target)


def _exchange(scatter, gather, name):
    arrays = list(scatter) + list(gather)
    n_a, n_s = len(arrays), len(scatter)

    def body(*refs):
        src, dst = refs[:n_a], refs[n_a:2 * n_a]
        send_sems, recv_sems, local_sems = refs[2 * n_a:]
        x, y, c = lax.axis_index("x"), lax.axis_index("y"), lax.axis_index("c")
        me = 4 * x + 2 * y + c

        def src_of(a, slot):
            return src[a].at[slot] if a < n_s else src[a]

        local = [pltpu.make_async_copy(src_of(a, me), dst[a].at[me], local_sems.at[a]) for a in range(n_a)]
        for cp in local:
            cp.start()
        sends, peers = [], []
        for k in range(1, N_DEV):
            px = 1 - x if k & 4 else x
            py = 1 - y if k & 2 else y
            pc = 1 - c if k & 1 else c
            peer = 4 * px + 2 * py + pc
            peers.append(peer)
            for a in range(n_a):
                cp = pltpu.make_async_remote_copy(
                    src_ref=src_of(a, peer), dst_ref=dst[a].at[me], send_sem=send_sems.at[a, k - 1],
                    recv_sem=recv_sems.at[a, k - 1], device_id=(px, py, pc), device_id_type=pl.DeviceIdType.MESH)
                cp.start()
                sends.append(cp)
        for k in range(1, N_DEV):
            peer = peers[k - 1]
            for a in range(n_a):
                pltpu.make_async_remote_copy(
                    src_ref=src_of(a, peer), dst_ref=dst[a].at[peer], send_sem=send_sems.at[a, k - 1],
                    recv_sem=recv_sems.at[a, k - 1], device_id=(x, y, c),
                    device_id_type=pl.DeviceIdType.MESH).wait_recv()
        for cp in sends:
            cp.wait_send()
        for cp in local:
            cp.wait()

    out_shape = [SDS(a.shape, a.dtype) for a in scatter] + [SDS((N_DEV,) + a.shape, a.dtype) for a in gather]
    hbm = pl.BlockSpec(memory_space=pl.ANY)
    return pl.pallas_call(
        body, name=name, in_specs=[hbm] * n_a, out_specs=[hbm] * n_a, out_shape=out_shape,
        scratch_shapes=[pltpu.SemaphoreType.DMA((n_a, N_DEV - 1)), pltpu.SemaphoreType.DMA((n_a, N_DEV - 1)),
                        pltpu.SemaphoreType.DMA((n_a,))],
        )(*arrays)


def _peer(k):
    x, y, c = lax.axis_index("x"), lax.axis_index("y"), lax.axis_index("c")
    px = 1 - x if k & 4 else x
    py = 1 - y if k & 2 else y
    pc = 1 - c if k & 1 else c
    return (px, py, pc), 4 * px + 2 * py + pc


_HBM = pl.BlockSpec(memory_space=pltpu.HBM)
_SEM = pl.BlockSpec(memory_space=pltpu.SEMAPHORE)


def _split_copies(src, land, send_sems, recv_sems, scatter, arrivals):
    _, me = _peer(0)
    out = []
    for k in range(1, N_DEV):
        coords, peer = _peer(k)
        for a in range(len(src)):
            sem = a * (N_DEV - 1) + k - 1
            if scatter[a]:
                s, d = src[a].at[peer], land[a].at[k]
            else:
                s, d = src[a], land[a].at[peer if arrivals else me]
            out.append(pltpu.make_async_remote_copy(
                src_ref=s, dst_ref=d, send_sem=send_sems.at[sem], recv_sem=recv_sems.at[sem], device_id=coords,
                device_id_type=pl.DeviceIdType.MESH))
    return out


def _exchange_start(srcs, lands, scatter, name):
    n = len(srcs)

    def body(*refs):
        src, land = refs[:n], refs[n:2 * n]
        send_sems, recv_sems = refs[2 * n], refs[2 * n + 1]
        token = refs[-1]
        for cp in _split_copies(src, land, send_sems, recv_sems, scatter, False):
            cp.start()
        token[...] = jnp.zeros_like(token)

    hbm_shape = lambda a: pltpu.HBM(a.shape, a.dtype)
    sem_shape = pltpu.SemaphoreType.DMA((n * (N_DEV - 1),))
    outs = pl.pallas_call(
        body, name=name,
        out_shape=(sem_shape, sem_shape, *[hbm_shape(a) for a in srcs], *[hbm_shape(a) for a in lands],
                   SDS((8, 128), F32)),
        in_specs=[_HBM] * (2 * n), out_specs=(_SEM, _SEM, *[_HBM] * (2 * n), pl.BlockSpec(memory_space=pltpu.VMEM)),
        input_output_aliases={a: 2 + a for a in range(2 * n)},
        compiler_params=pltpu.CompilerParams(has_side_effects=pltpu.SideEffectType.DATAFLOW_SIDE_EFFECTING),
    )(*[pltpu.with_memory_space_constraint(a, pltpu.HBM) for a in list(srcs) + list(lands)])
    return outs[0], outs[1], outs[2:2 + n], outs[2 + n:2 + 2 * n], outs[-1]


def _exchange_wait(send_sems, recv_sems, srcs, lands, scatter, after, name):
    n = len(srcs)

    def body(*refs):
        src, land = refs[:n], refs[n:2 * n]
        for cp in _split_copies(src, land, refs[2 * n], refs[2 * n + 1], scatter, False):
            cp.wait_send()
        for cp in _split_copies(src, land, refs[2 * n], refs[2 * n + 1], scatter, True):
            cp.wait_recv()

    hbm_shape = lambda a: pltpu.HBM(a.shape, a.dtype)
    outs = pl.pallas_call(
        body, name=name, out_shape=tuple(hbm_shape(a) for a in list(srcs) + list(lands)),
        in_specs=[_HBM] * (2 * n) + [_SEM, _SEM, pl.BlockSpec(memory_space=pl.ANY)], out_specs=(_HBM,) * (2 * n),
        input_output_aliases={a: a for a in range(2 * n)},
        compiler_params=pltpu.CompilerParams(has_side_effects=pltpu.SideEffectType.DATAFLOW_SIDE_EFFECTING),
    )(*srcs, *lands, send_sems, recv_sems, after)
    return outs[:n], outs[n:]


def _reduce8(a, tr, name):
    _, rows, cols = a.shape

    def body(a_ref, o_ref):
        acc = a_ref[0]
        for j in range(1, N_DEV):
            acc = acc + a_ref[j]
        o_ref[...] = acc

    return pl.pallas_call(
        body, name=name, grid=(rows // tr,),
        in_specs=[pl.BlockSpec((N_DEV, tr, cols), lambda i: (0, i, 0))],
        out_specs=pl.BlockSpec((tr, cols), lambda i: (i, 0)), out_shape=SDS((rows, cols), F32),
        compiler_params=_params("parallel"))(a)


def _reduce_landed(own, land, name, tc=256):
    rows, cols = own.shape

    def body(own_ref, land_ref, o_ref):
        acc = own_ref[...]
        for k in range(1, N_DEV):
            acc = acc + land_ref[k].astype(F32)
        o_ref[...] = acc

    return pl.pallas_call(
        body, name=name, grid=(cols // tc,),
        in_specs=[pl.BlockSpec((rows, tc), lambda j: (0, j)), pl.BlockSpec((N_DEV, rows, tc), lambda j: (0, 0, j))],
        out_specs=pl.BlockSpec((rows, tc), lambda j: (0, j)), out_shape=SDS((rows, cols), F32),
        compiler_params=_params("parallel"))(own, land)


def _adamw(w, g, m, v, name, tr=None):
    rows, cols = w.shape
    tr = rows if tr is None else tr
    c1 = 1.0 - ADAM_B1 ** ADAM_STEP
    c2 = 1.0 - ADAM_B2 ** ADAM_STEP

    def body(w_ref, g_ref, m_ref, v_ref, d_ref, mo_ref, vo_ref):
        gv = g_ref[...]
        mn = ADAM_B1 * m_ref[...] + (1.0 - ADAM_B1) * gv
        vn = ADAM_B2 * v_ref[...] + (1.0 - ADAM_B2) * (gv * gv)
        mo_ref[...] = mn
        vo_ref[...] = vn
        d_ref[...] = -ADAM_LR * ((mn / c1) / (jnp.sqrt(vn / c2) + ADAM_EPS) + ADAM_WD * w_ref[...])

    blk = pl.BlockSpec((tr, cols), lambda i: (i, 0))
    return pl.pallas_call(
        body, name=name, grid=(rows // tr,), in_specs=[blk] * 4, out_specs=[blk] * 3,
        out_shape=[SDS((rows, cols), F32)] * 3, compiler_params=_params("parallel"))(w, g, m, v)


def _pad_rows(a, rows):
    return jnp.pad(a, ((0, rows - a.shape[0]),) + ((0, 0),) * (a.ndim - 1))


def _local_step(x, p, target, sm, wts, fetch_rest, send, tok):
    ones_q, ones_k, dup, dup_t = _head_consts()
    tri, triu, expand, expand_t = _ssd_consts()
    w_in_t = wts["in_t"]
    cwx, cwb = wts["ssm_cw"][:, :SSM_INNER], wts["ssm_cw"][:, SSM_INNER:]
    cbx, cbb = sm["ssm_conv_b"][:, :SSM_INNER], sm["ssm_conv_b"][:, SSM_INNER:]
    pad128 = lambda a: jnp.pad(a, ((0, 0), (0, 128 - a.shape[1])))
    dtb, alog = pad128(sm["dt_bias"]), pad128(sm["a_log"])
    dsk_e = jnp.repeat(sm["d_skip"], HEAD_DIM, axis=1)
    gq = jnp.tile(sm["q_norm_g"], (1, ATTN_DIM // HEAD_DIM))
    gk = jnp.tile(sm["k_norm_g"], (1, KV_DIM // HEAD_DIM))
    ffn_cw = wts["ffn_cw"].reshape(3, 2, 1, D_FF)
    ffn_cb = sm["ffn_conv_b"].reshape(2, 1, D_FF)

    proj, h1 = _norm_matmul(x, sm["attn_norm_g"] + tok, w_in_t, 1024, 768, "in_proj")
    qn, kd, vd = _attn_prep(proj, gq, gk, ones_q, ones_k, dup)
    attn_out, lse = _attn_fwd(qn, kd, vd)
    y_ssd, hs, ssm_out = _ssd_fwd(proj, cwx, cbx, cwb, cbb, dtb, alog, dsk_e, sm["ssm_norm_g"], tri, expand)
    rest = fetch_rest(ssm_out)
    w_out, w_up_t, w_down, w_gate, w_proj_t = (rest[k] for k in ("out", "up_t", "down", "gate", "proj_t"))
    x1 = _mm_resid([(attn_out, None, w_out[:ATTN_DIM]), (ssm_out, None, w_out[ATTN_DIM:])], x, None, 512, F32,
                   "out_proj")
    u, h2, f = _up_act(x1, sm["ffn_norm_g"], w_up_t, ffn_cw, ffn_cb)
    x2 =_mm_resid([(f, None, w_down)], x1, None, 512, F32, "down_proj")
    dx2, dpre, dpp, h3, loss, dg_ple = _ple_loss(x2, sm["ple_norm_g"], w_gate, p, w_proj_t, target)

    g_gate = _wgrad(h3, None, dpre, "wg_gate")
    g_proj_t = _wgrad(dpp, None, p, "wg_proj")
    g_down = _wgrad(f, None, dx2, "wg_down")
    du, d_ffn_cw, d_ffn_cb = _ffn_bwd(dx2, w_down, u, ffn_cw, ffn_cb)
    dx1, dg_ffn = _mm_normbwd([(du, 0, w_up_t[:D_FF]), (du, 1, w_up_t[D_FF:])], x1, sm["ffn_norm_g"], dx2, 256,
                              "up_proj_bwd")
    g_up_t = jnp.concatenate([_wgrad(du, 0, h2, "wg_up_gate"), _wgrad(du, 1, h2, "wg_up_val")], axis=0)
    tok = send(dict(gate=g_gate, proj_t=g_proj_t, down=g_down, up_t=g_up_t)).astype(BF16)
    d_attn = _mm_resid([(dx1, None, w_out[:ATTN_DIM] + tok)], None, NT, 512, F32, "out_proj_bwd_attn")
    d_ssm = _mm_resid([(dx1, None, w_out[ATTN_DIM:] + tok)], None, NT, 512, F32, "out_proj_bwd_ssm")
    g_out = jnp.concatenate([_wgrad(attn_out, None, dx1, "wg_out_attn"), _wgrad(ssm_out, None, dx1, "wg_out_ssm")],
                            axis=0)
    tok = send(dict(out=g_out))
    (dz, dxs, dbc, ddt, dg_ssm, d_dsk_e, d_alog, d_dtb, d_cwx, d_cbx, d_cwb, d_cbb) = _ssd_bwd(
        proj, y_ssd, hs, d_ssm, cwx, cbx, cwb, cbb, dtb + tok, alog, dsk_e, sm["ssm_norm_g"], tri, triu, expand,
        expand_t)
    dqn, dkc, dkp, dvc, dvp = _attn_bwd(qn, kd, vd, attn_out, lse, d_attn, ones_k[:128, :128])
    dqkv, dgq, dgk = _attn_prep_bwd(proj, dqn, dkc, dkp, dvc, dvp, gq + tok, gk, ones_q, ones_k, dup_t)
    pieces = [(dqkv, 0, 1024), (dz, 1024, 2048), (dxs, 2048, 3072), (dbc, 3072, 3584), (ddt, 3584, 3712)]
    g_in_t = jnp.concatenate([_wgrad(a, None, h1, "wg_in_%d" % lo) for a, lo, _ in pieces], axis=0)[:IN_PROJ]
    tok = send(dict(in_t=g_in_t))
    grad_x, dg_attn = _mm_normbwd([(a, None, w_in_t[lo:hi]) for a, lo, hi in pieces], x, sm["attn_norm_g"] + tok, dx1,
                                  256, "in_proj_bwd")

    small = dict(
        attn_norm_g=dg_attn, q_norm_g=dgq.reshape(-1, HEAD_DIM).sum(0, keepdims=True),
        k_norm_g=dgk.reshape(-1, HEAD_DIM).sum(0, keepdims=True),
        ssm_conv_w=jnp.concatenate([d_cwx, d_cwb], axis=1), ssm_conv_b=jnp.concatenate([d_cbx, d_cbb], axis=1),
        dt_bias=d_dtb[:, :SSM_HEADS], a_log=d_alog[:, :SSM_HEADS],
        d_skip=d_dsk_e.reshape(SSM_HEADS, HEAD_DIM).sum(1)[None, :], ssm_norm_g=dg_ssm, ffn_norm_g=dg_ffn,
        ffn_conv_w=d_ffn_cw.reshape(3, 2 * D_FF), ffn_conv_b=d_ffn_cb.reshape(1, 2 * D_FF), ple_norm_g=dg_ple)
    return loss[0, 0], grad_x, small


_SMALL = (("attn_norm_g", 1024), ("q_norm_g", 64), ("k_norm_g", 64), ("ssm_conv_w", 4 * XBC_DIM),
          ("ssm_conv_b", XBC_DIM), ("dt_bias", 16), ("a_log", 16), ("d_skip", 16), ("ssm_norm_g", 1024),
          ("ffn_norm_g", 1024), ("ffn_conv_w", 3 * 2 * D_FF), ("ffn_conv_b", 2 * D_FF), ("ple_norm_g", 1024))
_SMALL_ROWS = 34
_SHARD_SMALL = (("attn_norm_g", 1024), ("q_norm_g", 64), ("k_norm_g", 64), ("ssm_conv_w", 4 * XBC_DIM // N_DEV),
                ("ssm_conv_b", XBC_DIM), ("dt_bias", 16), ("a_log", 16), ("d_skip", 16), ("ssm_norm_g", 1024),
                ("ffn_norm_g", 1024), ("ffn_conv_w", 3 * 2 * D_FF // N_DEV), ("ffn_conv_b", 2 * D_FF),
                ("ple_norm_g", 1024))
_SHARD_SMALL_ROWS = 14


def _pack_flat(parts, order, rows):
    flat = jnp.concatenate([parts[name].reshape(-1) for name, _ in order])
    return jnp.pad(flat, (0, rows * 1024 - flat.shape[0])).reshape(rows, 1024)


def _unpack_flat(packed, order):
    flat, out, pos = packed.reshape(-1), {}, 0
    for name, size in order:
        out[name] = flat[pos:pos + size]
        pos += size
    return out


def kernel(x, p, attn_norm_g, w_in, q_norm_g, k_norm_g, ssm_conv_w, ssm_conv_b, dt_bias, a_log, d_skip, ssm_norm_g, w_out, ffn_norm_g, w_up, ffn_conv_w, ffn_conv_b, w_down, ple_norm_g, w_ple_gate, w_ple_proj, loss_target, m_attn_norm_g, m_w_in, m_q_norm_g, m_k_norm_g, m_ssm_conv_w, m_ssm_conv_b, m_dt_bias, m_a_log, m_d_skip, m_ssm_norm_g, m_w_out, m_ffn_norm_g, m_w_up, m_ffn_conv_w, m_ffn_conv_b, m_w_down, m_ple_norm_g, m_w_ple_gate, m_w_ple_proj, v_attn_norm_g, v_w_in, v_q_norm_g, v_k_norm_g, v_ssm_conv_w, v_ssm_conv_b, v_dt_bias, v_a_log, v_d_skip, v_ssm_norm_g, v_w_out, v_ffn_norm_g, v_w_up, v_ffn_conv_w, v_ffn_conv_b, v_w_down, v_ple_norm_g, v_w_ple_gate, v_w_ple_proj):
    names = ("attn_norm_g", "w_in", "q_norm_g", "k_norm_g", "ssm_conv_w", "ssm_conv_b", "dt_bias", "a_log", "d_skip",
             "ssm_norm_g", "w_out", "ffn_norm_g", "w_up", "ffn_conv_w", "ffn_conv_b", "w_down", "ple_norm_g",
             "w_ple_gate", "w_ple_proj")
    w = dict(zip(names, (attn_norm_g, w_in, q_norm_g, k_norm_g, ssm_conv_w, ssm_conv_b, dt_bias, a_log, d_skip,
                         ssm_norm_g, w_out, ffn_norm_g, w_up, ffn_conv_w, ffn_conv_b, w_down, ple_norm_g, w_ple_gate,
                         w_ple_proj)))
    m = dict(zip(names, (m_attn_norm_g, m_w_in, m_q_norm_g, m_k_norm_g, m_ssm_conv_w, m_ssm_conv_b, m_dt_bias,
                         m_a_log, m_d_skip, m_ssm_norm_g, m_w_out, m_ffn_norm_g, m_w_up, m_ffn_conv_w, m_ffn_conv_b,
                         m_w_down, m_ple_norm_g, m_w_ple_gate, m_w_ple_proj)))
    v = dict(zip(names, (v_attn_norm_g, v_w_in, v_q_norm_g, v_k_norm_g, v_ssm_conv_w, v_ssm_conv_b, v_dt_bias,
                         v_a_log, v_d_skip, v_ssm_norm_g, v_w_out, v_ffn_norm_g, v_w_up, v_ffn_conv_w, v_ffn_conv_b,
                         v_w_down, v_ple_norm_g, v_w_ple_gate, v_w_ple_proj)))
    w, m, v = ({k: a[0] for k, a in d.items()} for d in (w, m, v))
    me = 4 * lax.axis_index("x") + 2 * lax.axis_index("y") + lax.axis_index("c")

    mine = dict(in_t=w["w_in"].T, out=w["w_out"], up_t=w["w_up"].T, down=w["w_down"], gate=w["w_ple_gate"],
                proj_t=w["w_ple_proj"].T)
    mine = {k: a.astype(BF16) for k, a in mine.items()}
    conv_pack = jnp.pad(jnp.concatenate([w["ssm_conv_w"].reshape(-1), w["ffn_conv_w"].reshape(-1)]),
                        (0, 3072 - 2880)).reshape(8, 384)
    all_in, all_conv = _exchange([], [mine["in_t"], conv_pack], "gather_first")
    later = ("out", "up_t", "down", "gate", "proj_t")
    zones = [lax.dynamic_update_slice(lax.empty((N_DEV,) + mine[k].shape, BF16), mine[k][None], (me, 0, 0))
             for k in later]
    zones, all_in, all_conv = lax.optimization_barrier((zones, all_in, all_conv))
    rest_state = _exchange_start([mine[k] for k in later], zones, [False] * len(later), "gather_rest_start")

    def fetch_rest(after):
        _, landed = _exchange_wait(*rest_state[:4], [False] * len(later), after, "gather_rest_wait")
        return {k: a.reshape(N_DEV * a.shape[1], a.shape[2]) for k, a in zip(later, landed)}

    wts = dict(in_t=_pad_rows(all_in.reshape(IN_PROJ, D_MODEL), IN_PROJ_PAD))
    conv_flat = all_conv.reshape(N_DEV, 3072)
    wts["ssm_cw"] = conv_flat[:, :768].reshape(N_DEV, 4, XBC_DIM // N_DEV).transpose(1, 0, 2).reshape(4, XBC_DIM)
    wts["ffn_cw"] = conv_flat[:, 768:2880].reshape(N_DEV, 3, 2 * D_FF // N_DEV).transpose(1, 0, 2).reshape(3, 2 * D_FF)
    sm = {k: w[k].reshape(1, -1) for k, _ in _SMALL if k not in ("ssm_conv_w", "ffn_conv_w")}

    in_flight = []

    def send(grads):
        keys = sorted(grads)
        blocks = [grads[k].reshape(N_DEV, grads[k].shape[0] // N_DEV, grads[k].shape[1]) for k in keys]
        own = [lax.dynamic_index_in_dim(a, me, 0, keepdims=False) for a in blocks]
        srcs = [a.astype(BF16) for a in blocks]
        state = _exchange_start(srcs, [lax.empty(a.shape, BF16) for a in srcs], [True] * len(keys),
                                "send_" + "_".join(keys))
        in_flight.append((keys, state, own))
        return state[4][0:1, 0:1]

    loss, grad_x, small = _local_step(x[0], p[0, 0], loss_target[0], sm, wts, fetch_rest, send,
                                      rest_state[4][0:1, 0:1])
    loss = lax.psum(loss, ("x", "y", "c"))

    (got_small,) = _exchange([], [_pack_flat(small, _SMALL, _SMALL_ROWS)], "gather_small_grads")
    g_small = _unpack_flat(_reduce8(got_small, _SMALL_ROWS, "reduce_small"), _SMALL)
    grads = {}
    for keys, state, own in in_flight:
        _, landed = _exchange_wait(*state[:4], [True] * len(keys), grad_x, "wait_" + "_".join(keys))
        for k, mine_k, land in zip(keys, own, landed):
            grads[k] = _reduce_landed(mine_k, land, "reduce_" + k)
    gw = {"w_in": grads["in_t"].T, "w_out": grads["out"], "w_up": grads["up_t"].T, "w_down": grads["down"],
          "w_ple_gate": grads["gate"], "w_ple_proj": grads["proj_t"].T}
    for k, size in _SMALL:
        gw[k] = g_small[k].reshape(w[k].shape) if k not in ("ssm_conv_w", "ffn_conv_w") else None
    n_ssm, n_ffn = XBC_DIM // N_DEV, 2 * D_FF // N_DEV
    gw["ssm_conv_w"] = lax.dynamic_slice(g_small["ssm_conv_w"].reshape(4, XBC_DIM), (0, me * n_ssm), (4, n_ssm))
    gw["ffn_conv_w"] = lax.dynamic_slice(g_small["ffn_conv_w"].reshape(3, 2 * D_FF), (0, me * n_ffn), (3, n_ffn))

    delta, new_m, new_v = {}, {}, {}
    for k, tr in (("w_in", 256), ("w_out", None), ("w_up", 256), ("w_down", None), ("w_ple_gate", None),
                  ("w_ple_proj", None)):
        delta[k], new_m[k], new_v[k] = _adamw(w[k], gw[k], m[k], v[k], "adamw_" + k, tr)
    packs = [_pack_flat(d, _SHARD_SMALL, _SHARD_SMALL_ROWS) for d in (w, gw, m, v)]
    for d, packed in zip((delta, new_m, new_v), _adamw(*packs, "adamw_small")):
        for k, a in _unpack_flat(packed, _SHARD_SMALL).items():
            d[k] = a.reshape(w[k].shape)

    outs = [loss, grad_x[None]]
    for d in (gw, delta, new_m, new_v):
        outs += [d[k][None] for k in names]
    return tuple(outs)
```

```python
import functools

import numpy as np
import jax
import jax.numpy as jnp
from jax import lax
from jax.experimental import pallas as pl
from jax.experimental.pallas import tpu as pltpu

F32 = jnp.float32
BF16 = jnp.bfloat16
SDS = jax.ShapeDtypeStruct
EPS = 1e-6
N_DEV = 8
D_MODEL = 1024
HEAD_DIM = 64
ATTN_DIM = 512
KV_DIM = 256
SSM_INNER = 1024
SSM_HEADS = 16
BC_DIM = 256
XBC_DIM = SSM_INNER + 2 * BC_DIM
MIX_DIM = ATTN_DIM + SSM_INNER
IN_PROJ = 3600
IN_PROJ_PAD = 3840
D_FF = 2816
PLE_DIM = 256
CHUNK = 128
SUPER = 2048
DILATIONS = (1, 4, 16)
TILE_UNROLL = 8
VMEM_LIMIT = 56 * 1024 * 1024
ADAM_LR, ADAM_B1, ADAM_B2, ADAM_EPS, ADAM_WD, ADAM_STEP = 0.001, 0.9, 0.999, 1e-08, 0.01, 10

NT = (((1,), (1,)), ((), ()))
TN = (((0,), (0,)), ((), ()))


def _params(*sem):
    return pltpu.CompilerParams(dimension_semantics=sem if sem else None, vmem_limit_bytes=VMEM_LIMIT)


def _dot(a, b, dims=None):
    if dims is None:
        return jnp.dot(a, b, preferred_element_type=F32)
    return lax.dot_general(a, b, dims, preferred_element_type=F32)


def _hdot(a, b, parts=2):
    a_exact = a.dtype == BF16
    x = b if a_exact else a
    acc = None
    for _ in range(parts):
        piece = x.astype(BF16)
        x = x - piece.astype(F32)
        d = _dot(a, piece) if a_exact else _dot(piece, b)
        acc = d if acc is None else acc + d
    return acc


def _sigmoid(x):
    return 0.5 * jnp.tanh(0.5 * x) + 0.5


def _shift_down(x, halo8, s):
    xr = pltpu.roll(x, s, 0)
    row = lax.broadcasted_iota(jnp.int32, halo8.shape, 0)
    first = jnp.where(row < s, pltpu.roll(halo8, s, 0), xr[0:8])
    return jnp.concatenate([first, xr[8:]], axis=0)


def _shift_up(x, halo8, s):
    n = x.shape[0]
    xr = pltpu.roll(x, n - s, 0)
    row = lax.broadcasted_iota(jnp.int32, halo8.shape, 0)
    last = jnp.where(row >= 8 - s, pltpu.roll(halo8, 8 - s, 0), xr[n - 8:])
    return jnp.concatenate([xr[:n - 8], last], axis=0)


def _norm_matmul(x, g, wt, tm, tn, name):
    t, k = x.shape
    n = wt.shape[0]

    def body(x_ref, g_ref, w_ref, o_ref, h_ref):
        @pl.when(pl.program_id(1) == 0)
        def _():
            xv = x_ref[...]
            r = lax.rsqrt(jnp.mean(xv * xv, axis=-1, keepdims=True) + EPS)
            h_ref[...] = (xv * r * g_ref[...]).astype(BF16)
        o_ref[...] = _dot(h_ref[...], w_ref[...], NT)

    return pl.pallas_call(
        body, name=name, grid=(t // tm, n // tn),
        in_specs=[pl.BlockSpec((tm, k), lambda i, j: (i, 0)), pl.BlockSpec((1, k), lambda i, j: (0, 0)),
                  pl.BlockSpec((tn, k), lambda i, j: (j, 0))],
        out_specs=[pl.BlockSpec((tm, tn), lambda i, j: (i, j)), pl.BlockSpec((tm, k), lambda i, j: (i, 0))],
        out_shape=[SDS((t, n), F32), SDS((t, k), BF16)],
        compiler_params=_params("parallel", "arbitrary"))(x, g, wt)


def _a_spec(a, lead, tm):
    if lead is None:
        return pl.BlockSpec((tm, a.shape[-1]), lambda i: (i, 0))
    return pl.BlockSpec((None, tm, a.shape[-1]), lambda i, _l=lead: (_l, i, 0))


def _mm_resid(pairs, res, dims, tm, out_dtype, name):
    t = pairs[0][0].shape[-2]
    n = pairs[0][2].shape[1] if dims is None else pairs[0][2].shape[0]
    np_ = len(pairs)

    def body(*refs):
        o_ref = refs[-1]
        acc = refs[2 * np_][...] if res is not None else None
        for q in range(np_):
            d = _dot(refs[q][...].astype(BF16), refs[np_ + q][...], dims)
            acc = d if acc is None else acc + d
        o_ref[...] = acc.astype(out_dtype)

    in_specs = [_a_spec(a, lead, tm) for a, lead, _ in pairs]
    in_specs += [pl.BlockSpec(b.shape, lambda i: (0, 0)) for _, _, b in pairs]
    args = [a for a, _, _ in pairs] + [b for _, _, b in pairs]
    if res is not None:
        in_specs.append(pl.BlockSpec((tm, n), lambda i: (i, 0)))
        args.append(res)
    return pl.pallas_call(
        body, name=name, grid=(t // tm,), in_specs=in_specs,
        out_specs=pl.BlockSpec((tm, n), lambda i: (i, 0)), out_shape=SDS((t, n), out_dtype),
        compiler_params=_params("parallel"))(*args)


def _mm_normbwd(pairs, x, g, dres, tm, name):
    t, k = x.shape
    np_ = len(pairs)
    b_specs = [pl.BlockSpec((rows, b.shape[1]), lambda i, _b=blk: (_b, 0)) for _, _, b, rows, blk in pairs]
    pairs = [(a, lead, b) for a, lead, b, _, _ in pairs]

    def body(*refs):
        x_ref, g_ref, dres_ref, dx_ref, dg_ref = refs[2 * np_:]
        dh = None
        for q in range(np_):
            d = _dot(refs[q][...], refs[np_ + q][...])
            dh = d if dh is None else dh + d
        xv = x_ref[...]
        r = lax.rsqrt(jnp.mean(xv * xv, axis=-1, keepdims=True) + EPS)
        xh = xv * r

        @pl.when(pl.program_id(0) == 0)
        def _():
            dg_ref[...] = jnp.zeros_like(dg_ref)
        dg_ref[...] += jnp.sum(dh * xh, axis=0, keepdims=True)
        gd = dh * g_ref[...]
        dx_ref[...] = dres_ref[...] + r * (gd - xh * jnp.mean(gd * xh, axis=-1, keepdims=True))

    in_specs = [_a_spec(a, lead, tm) for a, lead, _ in pairs] + b_specs
    in_specs += [pl.BlockSpec((tm, k), lambda i: (i, 0)), pl.BlockSpec((1, k), lambda i: (0, 0)),
                 pl.BlockSpec((tm, k), lambda i: (i, 0))]
    args = [a for a, _, _ in pairs] + [b for _, _, b in pairs] + [x, g, dres]
    return pl.pallas_call(
        body, name=name, grid=(t // tm,), in_specs=in_specs,
        out_specs=[pl.BlockSpec((tm, k), lambda i: (i, 0)), pl.BlockSpec((1, k), lambda i: (0, 0))],
        out_shape=[SDS((t, k), F32), SDS((1, k), F32)],
        compiler_params=_params("arbitrary"))(*args)


def _wgrad(a, a_lead, b, name, tk=2048):
    t, m = a.shape[-2:]
    n = b.shape[1]
    tm = m if m <= 1024 else 1408
    assert m % tm == 0

    def body(a_ref, b_ref, o_ref):
        @pl.when(pl.program_id(1) == 0)
        def _():
            o_ref[...] = jnp.zeros_like(o_ref)
        o_ref[...] += _dot(a_ref[...].astype(BF16), b_ref[...].astype(BF16), TN)

    if a_lead is None:
        a_spec = pl.BlockSpec((tk, tm), lambda mi, ki: (ki, mi))
    else:
        a_spec = pl.BlockSpec((None, tk, tm), lambda mi, ki, _l=a_lead: (_l, ki, mi))
    return pl.pallas_call(
        body, name=name, grid=(m // tm, t // tk),
        in_specs=[a_spec, pl.BlockSpec((tk, n), lambda mi, ki: (ki, 0))],
        out_specs=pl.BlockSpec((tm, n), lambda mi, ki: (mi, 0)), out_shape=SDS((m, n), F32),
        compiler_params=_params("parallel", "arbitrary"))(a, b)


def _head_consts():
    iq = np.arange(ATTN_DIM)
    ik = np.arange(KV_DIM)
    ones_q = (iq[:, None] // HEAD_DIM == iq[None, :] // HEAD_DIM).astype(np.float32)
    ones_k = (ik[:, None] // HEAD_DIM == ik[None, :] // HEAD_DIM).astype(np.float32)
    dup = (ik[:, None] == (HEAD_DIM * (iq[None, :] // 128) + iq[None, :] % HEAD_DIM)).astype(np.float32)
    return jnp.asarray(ones_q, BF16), jnp.asarray(ones_k, BF16), jnp.asarray(dup, BF16), jnp.asarray(dup.T, BF16)


def _attn_prep(proj, gq, gk, ones_q, ones_k, dup, tm=512):
    t = proj.shape[0]

    def body(p_ref, gq_ref, gk_ref, oq_ref, ok_ref, dup_ref, qn_ref, kd_ref, vd_ref):
        q = p_ref[:, 0:ATTN_DIM]
        k = p_ref[:, ATTN_DIM:ATTN_DIM + KV_DIM]
        v = p_ref[:, ATTN_DIM + KV_DIM:]
        rq = lax.rsqrt(_hdot(q * q, oq_ref[...]) * (1.0 / HEAD_DIM) + EPS)
        qn_ref[...] = (q * rq * gq_ref[...]) * (HEAD_DIM ** -0.5)
        rk = lax.rsqrt(_hdot(k * k, ok_ref[...]) * (1.0 / HEAD_DIM) + EPS)
        kn = k * rk * gk_ref[...]
        kd_ref[...] = _dot(kn.astype(BF16), dup_ref[...])
        vd_ref[...] = _dot(v.astype(BF16), dup_ref[...])

    full = lambda a: pl.BlockSpec(a.shape, lambda i: (0, 0))
    o_spec = pl.BlockSpec((tm, ATTN_DIM), lambda i: (i, 0))
    return pl.pallas_call(
        body, name="attn_prep", grid=(t // tm,),
        in_specs=[pl.BlockSpec((tm, 1024), lambda i: (i, 0)), full(gq), full(gk), full(ones_q), full(ones_k), full(dup)],
        out_specs=[o_spec, o_spec, o_spec], out_shape=[SDS((t, ATTN_DIM), F32)] * 3,
        compiler_params=_params("parallel"))(proj, gq, gk, ones_q, ones_k, dup)


def _attn_prep_bwd(proj, dqn, dkc, dkp, dvc, dvp, gq, gk, ones_q, ones_k, dup_t, tm=512):
    t = proj.shape[0]
    nblk = t // tm
    off = SUPER // tm

    def body(p_ref, dqn_ref, dkc_ref, dkp_ref, dvc_ref, dvp_ref, gq_ref, gk_ref, oq_ref, ok_ref, dt_ref,
             o_ref, dgq_ref, dgk_ref):
        i = pl.program_id(0)
        has_next = (i + off < nblk).astype(F32)
        q = p_ref[:, 0:ATTN_DIM]
        k = p_ref[:, ATTN_DIM:ATTN_DIM + KV_DIM]
        dkn = _hdot(dkc_ref[...] + has_next * dkp_ref[...], dt_ref[...])
        dv = _hdot(dvc_ref[...] + has_next * dvp_ref[...], dt_ref[...])

        @pl.when(i == 0)
        def _():
            dgq_ref[...] = jnp.zeros_like(dgq_ref)
            dgk_ref[...] = jnp.zeros_like(dgk_ref)

        rq = lax.rsqrt(_hdot(q * q, oq_ref[...]) * (1.0 / HEAD_DIM) + EPS)
        xh = q * rq
        dy = dqn_ref[...] * (HEAD_DIM ** -0.5)
        dgq_ref[...] += jnp.sum(dy * xh, axis=0, keepdims=True)
        gd = dy * gq_ref[...]
        dq = rq * (gd - xh * (_hdot(gd * xh, oq_ref[...]) * (1.0 / HEAD_DIM)))
        rk = lax.rsqrt(_hdot(k * k, ok_ref[...]) * (1.0 / HEAD_DIM) + EPS)
        kh = k * rk
        dgk_ref[...] += jnp.sum(dkn * kh, axis=0, keepdims=True)
        gdk = dkn * gk_ref[...]
        dk = rk * (gdk - kh * (_hdot(gdk * kh, ok_ref[...]) * (1.0 / HEAD_DIM)))
        o_ref[:, 0:ATTN_DIM] = dq.astype(BF16)
        o_ref[:, ATTN_DIM:ATTN_DIM + KV_DIM] = dk.astype(BF16)
        o_ref[:, ATTN_DIM + KV_DIM:] = dv.astype(BF16)

    full = lambda a: pl.BlockSpec(a.shape, lambda i: (0, 0))
    cur = pl.BlockSpec((tm, ATTN_DIM), lambda i: (i, 0))
    nxt = pl.BlockSpec((tm, ATTN_DIM), lambda i: (jnp.minimum(i + off, nblk - 1), 0))
    return pl.pallas_call(
        body, name="attn_prep_bwd", grid=(nblk,),
        in_specs=[pl.BlockSpec((tm, 1024), lambda i: (i, 0)), cur, cur, nxt, cur, nxt,
                  full(gq), full(gk), full(ones_q), full(ones_k), full(dup_t)],
        out_specs=[pl.BlockSpec((tm, 1024), lambda i: (i, 0)), pl.BlockSpec((1, ATTN_DIM), lambda i: (0, 0)),
                   pl.BlockSpec((1, KV_DIM), lambda i: (0, 0))],
        out_shape=[SDS((t, 1024), BF16), SDS((1, ATTN_DIM), F32), SDS((1, KV_DIM), F32)],
        compiler_params=_params("arbitrary"))(proj, dqn, dkc, dkp, dvc, dvp, gq, gk, ones_q, ones_k, dup_t)


def _tile_masks():
    qi = lax.broadcasted_iota(jnp.int32, (2 * CHUNK, 2 * CHUNK), 0) & (CHUNK - 1)
    kj = lax.broadcasted_iota(jnp.int32, (2 * CHUNK, 2 * CHUNK), 1)
    delta = CHUNK + qi - kj
    band = (delta >= 0) & (delta <= CHUNK)
    return band, kj


def _deinterleave(dst, src, n_rows, d):
    per = n_rows // d
    for r in range(d):
        dst[r * per:(r + 1) * per, :] = src[pl.ds(r, per, stride=d), :]


def _attn_specs(t):
    blk = lambda f: pl.BlockSpec((SUPER, 128), f)
    cur = blk(lambda h, s: (s, h))
    prev = blk(lambda h, s: (jnp.maximum(s - 1, 0), h))
    return cur, prev


def _attn_fwd(qn, kd, vd):
    t = qn.shape[0]
    cur, prev = _attn_specs(t)

    def body(q_ref, kp_ref, kc_ref, vp_ref, vc_ref, o_ref, lse_ref, kk, vv, qd, kdd, vdd, po, pm, pll, acc, mm, ll):
        s = pl.program_id(1)
        kk[0:SUPER, :] = kp_ref[...]
        kk[SUPER:, :] = kc_ref[...]
        vv[0:SUPER, :] = vp_ref[...]
        vv[SUPER:, :] = vc_ref[...]
        m0 = lax.broadcasted_iota(jnp.int32, (CHUNK, 128), 1) < HEAD_DIM
        band, kj = _tile_masks()
        for d in DILATIONS:
            lq = SUPER // d
            if d == 1:
                qs_ref, ks_ref, vs_ref = q_ref, kk, vv
            else:
                _deinterleave(qd, q_ref, SUPER, d)
                _deinterleave(kdd, kk, 2 * SUPER, d)
                _deinterleave(vdd, vv, 2 * SUPER, d)
                qs_ref, ks_ref, vs_ref = qd, kdd, vdd

            nblk = lq // CHUNK

            def key_rows(ti):
                return pl.ds((ti // nblk) * 2 * lq + lq + (ti % nblk - 1) * CHUNK, 2 * CHUNK)

            def scores(ti):
                qt = qs_ref[pl.ds(ti * CHUNK, CHUNK), :]
                qs = jnp.concatenate([jnp.where(m0, qt, 0.0), jnp.where(m0, 0.0, qt)], axis=0).astype(BF16)
                return _dot(qs, ks_ref[key_rows(ti), :].astype(BF16), NT)

            def softmax_pv(ti, sc):
                ok = band if ti % nblk > 0 else band & (kj >= jnp.where(s > 0, 0, CHUNK))
                sc = jnp.where(ok, sc, -jnp.inf)
                mt = jnp.max(sc, axis=-1, keepdims=True)
                p = jnp.exp(sc - mt)
                lt = jnp.sum(p, axis=-1, keepdims=True)
                ot = _dot(p.astype(BF16), vs_ref[key_rows(ti), :].astype(BF16))
                qrows = pl.ds(ti * CHUNK, CHUNK)
                po[qrows, :] = jnp.where(m0, ot[:CHUNK], ot[CHUNK:])
                pm[qrows, :] = jnp.where(m0, mt[:CHUNK], mt[CHUNK:])
                pll[qrows, :] = jnp.where(m0, lt[:CHUNK], lt[CHUNK:])

            for ti in range(SUPER // CHUNK):
                softmax_pv(ti, scores(ti))
            if d == 1:
                acc[...] = po[...]
                mm[...] = pm[...]
                ll[...] = pll[...]
            else:
                for r in range(d):
                    rows = pl.ds(r, lq, stride=d)
                    seg = slice(r * lq, (r + 1) * lq)
                    m_old, m_new = mm[rows, :], pm[seg, :]
                    m_all = jnp.maximum(m_old, m_new)
                    a, b = jnp.exp(m_old - m_all), jnp.exp(m_new - m_all)
                    acc[rows, :] = acc[rows, :] * a + po[seg, :] * b
                    ll[rows, :] = ll[rows, :] * a + pll[seg, :] * b
                    mm[rows, :] = m_all
        o_ref[...] = acc[...] / ll[...]
        lse_ref[...] = mm[...] + jnp.log(ll[...])

    big = pltpu.VMEM((2 * SUPER, 128), F32)
    one = pltpu.VMEM((SUPER, 128), F32)
    return pl.pallas_call(
        body, name="attn_fwd", grid=(4, t // SUPER),
        in_specs=[cur, prev, cur, prev, cur], out_specs=[cur, cur],
        out_shape=[SDS((t, ATTN_DIM), F32)] * 2,
        scratch_shapes=[big, big, one, big, big, one, one, one, one, one, one],
        compiler_params=_params("parallel", "arbitrary"))(qn, kd, kd, vd, vd)


def _attn_bwd(qn, kd, vd, out, lse, dout, ones_pair):
    t = qn.shape[0]
    cur, prev = _attn_specs(t)

    def body(q_ref, kp_ref, kc_ref, vp_ref, vc_ref, o_ref, lse_ref, do_ref, ones_ref,
             dq_ref, dkc_ref, dkp_ref, dvc_ref, dvp_ref,
             kk, vv, od, ld, kb, vb, qsb, dosb, tk, tv, pdq, delta):
        s = pl.program_id(1)
        delta[...] = _hdot(do_ref[...] * o_ref[...], ones_ref[...])

        def per_row(a):
            ar = pltpu.roll(a, HEAD_DIM, 1)
            rows = jnp.concatenate([jnp.where(m0, a, ar), jnp.where(m0, ar, a)], axis=0)
            return jnp.concatenate([rows, rows], axis=1)

        kk[0:SUPER, :] = kp_ref[...]
        kk[SUPER:, :] = kc_ref[...]
        vv[0:SUPER, :] = vp_ref[...]
        vv[SUPER:, :] = vc_ref[...]
        for ref in (dq_ref, dkc_ref, dkp_ref, dvc_ref, dvp_ref):
            ref[...] = jnp.zeros_like(ref)
        m0 = lax.broadcasted_iota(jnp.int32, (CHUNK, 128), 1) < HEAD_DIM
        band, kj = _tile_masks()
        ninf = -jnp.inf
        for d in DILATIONS:
            lq = SUPER // d
            nblk = lq // CHUNK
            for r in range(d):
                seg = slice(r * 2 * lq, (r + 1) * 2 * lq)
                kb[seg, :] = kk[pl.ds(r, 2 * lq, stride=d), :].astype(BF16)
                vb[seg, :] = vv[pl.ds(r, 2 * lq, stride=d), :].astype(BF16)
            for ti in range(SUPER // CHUNK):
                rows = pl.ds(ti // nblk + d * CHUNK * (ti % nblk), CHUNK, stride=d)
                for src, dst in ((q_ref, qsb), (do_ref, dosb)):
                    a = src[rows, :]
                    dst[ti * 2 * CHUNK:(ti + 1) * 2 * CHUNK, :] = jnp.concatenate(
                        [jnp.where(m0, a, 0.0), jnp.where(m0, 0.0, a)], axis=0).astype(BF16)
                ld[ti * CHUNK:(ti + 1) * CHUNK, :] = lse_ref[rows, :]
                od[ti * CHUNK:(ti + 1) * CHUNK, :] = delta[rows, :]

            def operands(ti):
                r, nb = ti // nblk, ti % nblk
                stacked = slice(ti * 2 * CHUNK, (ti + 1) * 2 * CHUNK)
                krows = pl.ds(r * 2 * lq + lq + (nb - 1) * CHUNK, 2 * CHUNK)
                return stacked, krows

            def scores(ti):
                stacked, krows = operands(ti)
                kt = kb[krows, :]
                return dict(ti=ti, sc=_dot(qsb[stacked, :], kt, NT), dp=_dot(dosb[stacked, :], vb[krows, :], NT))

            def softmax_grad(c):
                qrows = slice(c["ti"] * CHUNK, (c["ti"] + 1) * CHUNK)
                ok = band if c["ti"] % nblk > 0 else band & (kj >= jnp.where(s > 0, 0, CHUNK))
                p = jnp.exp(jnp.where(ok, c.pop("sc"), ninf) - per_row(ld[qrows, :]))
                ds = p * (c.pop("dp") - per_row(od[qrows, :]))
                c.update(p=p.astype(BF16), ds=ds.astype(BF16))
                return c

            def grads(c):
                ti = c["ti"]
                stacked, krows = operands(ti)
                dqs = _dot(c["ds"], kb[krows, :])
                pdq[ti * CHUNK:(ti + 1) * CHUNK, :] = jnp.where(m0, dqs[:CHUNK], dqs[CHUNK:])
                tk[stacked, :] = _dot(c["ds"], qsb[stacked, :], TN)
                tv[stacked, :] = _dot(c["p"], dosb[stacked, :], TN)

            n_tiles = SUPER // CHUNK
            stage_a = scores(0)
            for ti in range(n_tiles):
                ahead = scores(ti + 1) if ti + 1 < n_tiles else None
                grads(softmax_grad(stage_a))
                stage_a = ahead

            for r in range(d):
                dq_ref[pl.ds(r, lq, stride=d), :] += pdq[r * lq:(r + 1) * lq, :]
                for tile_out, cur_ref, prev_ref in ((tk, dkc_ref, dkp_ref), (tv, dvc_ref, dvp_ref)):
                    first = r * nblk * 2 * CHUNK
                    prev_ref[pl.ds(SUPER - CHUNK * d + r, CHUNK, stride=d), :] += tile_out[first:first + CHUNK, :]
                    for nb in range(nblk):
                        at = (r * nblk + nb) * 2 * CHUNK
                        part = tile_out[at + CHUNK:at + 2 * CHUNK, :]
                        if nb + 1 < nblk:
                            part = part + tile_out[at + 2 * CHUNK:at + 3 * CHUNK, :]
                        cur_ref[pl.ds(r + d * nb * CHUNK, CHUNK, stride=d), :] += part

    big = pltpu.VMEM((2 * SUPER, 128), F32)
    one = pltpu.VMEM((SUPER, 128), F32)
    half = pltpu.VMEM((2 * SUPER, 128), BF16)
    return pl.pallas_call(
        body, name="attn_bwd", grid=(4, t // SUPER),
        in_specs=[cur, prev, cur, prev, cur, cur, cur, cur, pl.BlockSpec((128, 128), lambda h, s: (0, 0))],
        out_specs=[cur] * 5, out_shape=[SDS((t, ATTN_DIM), F32)] * 5,
        scratch_shapes=[big, big, one, one, half, half, half, half, big, big, one, one],
        compiler_params=_params("parallel", "arbitrary"))(qn, kd, kd, vd, vd, out, lse, dout, ones_pair)


def _ssd_consts():
    tri = np.tril(np.ones((CHUNK, CHUNK), np.float32))
    expand = np.zeros((128, SSM_INNER), np.float32)
    for h in range(SSM_HEADS):
        expand[h, h * HEAD_DIM:(h + 1) * HEAD_DIM] = 1.0
    return jnp.asarray(tri, BF16), jnp.asarray(tri.T, BF16), jnp.asarray(expand, BF16), jnp.asarray(expand.T, BF16)


def _conv4(x, halo, w_ref, b_ref):
    acc = b_ref[...] + w_ref[3:4, :] * x
    for k in range(3):
        acc = acc + w_ref[k:k + 1, :] * _shift_down(x, halo, 3 - k)
    return acc


def _softplus(x):
    return jnp.maximum(x, 0.0) + jnp.log(1.0 + jnp.exp(-jnp.abs(x)))


def _ssd_common(xs_ref, bc_ref, dt_ref, hx_ref, hb_ref, cwx_ref, cbx_ref, cwb_ref, cbb_ref, dtb_ref, alog_ref,
                tri_ref, exp_ref, first):
    keep = 1.0 - first.astype(F32)
    hx = hx_ref[...] * keep
    hb = hb_ref[...] * keep
    pre_x = _conv4(xs_ref[...], hx, cwx_ref, cbx_ref)
    pre_b = _conv4(bc_ref[...], hb, cwb_ref, cbb_ref)
    xa = pre_x * _sigmoid(pre_x)
    ba = pre_b * _sigmoid(pre_b)
    dtv = _softplus(dt_ref[...] + dtb_ref[...])
    a_neg = -jnp.exp(alog_ref[...])
    acum = _hdot(tri_ref[...], dtv * a_neg, parts=3)
    lam = jnp.exp(acum)
    gam = jnp.exp(acum[CHUNK - 1:CHUNK, :] - acum)
    dt_e = _hdot(dtv, exp_ref[...])
    lam_e = _hdot(lam, exp_ref[...])
    gam_e = _hdot(gam, exp_ref[...])
    return dict(hx=hx, hb=hb, pre_x=pre_x, pre_b=pre_b, xa=xa, ba=ba, dtv=dtv, a_neg=a_neg, acum=acum,
                dt_e=dt_e, lam_e=lam_e, gam_e=gam_e, xdt=xa * dt_e)


def _decay(acum_t, h, transposed):
    rb = jnp.broadcast_to(acum_t[h:h + 1, :], (CHUNK, CHUNK))
    ri = lax.broadcasted_iota(jnp.int32, (CHUNK, CHUNK), 0)
    ci = lax.broadcasted_iota(jnp.int32, (CHUNK, CHUNK), 1)
    if transposed:
        return jnp.exp(jnp.where(ci >= ri, rb - rb.T, -jnp.inf))
    return jnp.exp(jnp.where(ri >= ci, rb.T - rb, -jnp.inf))


def _ssd_specs(t, rev):
    nc = t // CHUNK
    ch = (lambda c: nc - 1 - c) if rev else (lambda c: c)
    col = lambda w, j: pl.BlockSpec((CHUNK, w), lambda c: (ch(c), j))
    halo = lambda w, j: pl.BlockSpec((8, w), lambda c: (jnp.maximum(ch(c) * (CHUNK // 8) - 1, 0), j))
    return nc, ch, col, halo


def _ssd_fwd(proj, cwx, cbx, cwb, cbb, dtb, alog, dsk_e, norm_g, tri, expand):
    t = proj.shape[0]
    nc, _, col, halo = _ssd_specs(t, False)

    def body(z_ref, xs_ref, bc_ref, dt_ref, hx_ref, hb_ref, cwx_ref, cbx_ref, cwb_ref, cbb_ref, dtb_ref, alog_ref,
             dsk_ref, g_ref, tri_ref, exp_ref, y_ref, hs_ref, o_ref, state):
        c = pl.program_id(0)

        @pl.when(c == 0)
        def _():
            state[...] = jnp.zeros_like(state)

        v = _ssd_common(xs_ref, bc_ref, dt_ref, hx_ref, hb_ref, cwx_ref, cbx_ref, cwb_ref, cbb_ref, dtb_ref,
                        alog_ref, tri_ref, exp_ref, c == 0)
        acum_t = v["acum"].T
        xdt, ba = v["xdt"], v["ba"]
        h_in = state[...]
        hs_ref[0] = h_in
        xg = xdt * v["gam_e"]
        m0 = lax.broadcasted_iota(jnp.int32, (CHUNK, 128), 1) < HEAD_DIM
        for g in range(2):
            bg = ba[:, g * 128:(g + 1) * 128].astype(BF16)
            cg = ba[:, 256 + g * 128:256 + (g + 1) * 128].astype(BF16)
            gl = slice(g * 512, (g + 1) * 512)
            cb = _dot(cg, bg, NT)
            y_off = _dot(cg, h_in[:, gl].astype(BF16)) * v["lam_e"][:, gl]
            s_new = _dot(bg.T, xg[:, gl].astype(BF16))
            state[:, gl] = h_in[:, gl] * v["lam_e"][CHUNK - 1:CHUNK, gl] + s_new
            for j in range(4):
                h0 = 8 * g + 2 * j
                ln = slice(g * 512 + j * 128, g * 512 + (j + 1) * 128)
                xp = xdt[:, ln].astype(BF16)
                y0 = _dot((cb * _decay(acum_t, h0, False)).astype(BF16), xp)
                y1 = _dot((cb * _decay(acum_t, h0 + 1, False)).astype(BF16), xp)
                y_ref[:, ln] = jnp.where(m0, y0, y1) + y_off[:, j * 128:(j + 1) * 128]
        z = z_ref[...]
        yg = (y_ref[...] + dsk_ref[...] * v["xa"]) * (z * _sigmoid(z))
        r = lax.rsqrt(jnp.mean(yg * yg, axis=-1, keepdims=True) + EPS)
        o_ref[...] = (yg * r * g_ref[...]).astype(BF16)

    full = lambda a: pl.BlockSpec(a.shape, lambda c: (0,) * a.ndim)
    return pl.pallas_call(
        body, name="ssd_fwd", grid=(nc,),
        in_specs=[col(1024, 1), col(1024, 2), col(512, 6), col(128, 28), halo(1024, 2), halo(512, 6),
                  full(cwx), full(cbx), full(cwb), full(cbb), full(dtb), full(alog), full(dsk_e), full(norm_g),
                  full(tri), full(expand)],
        out_specs=[pl.BlockSpec((CHUNK, SSM_INNER), lambda c: (c, 0)),
                   pl.BlockSpec((1, 128, SSM_INNER), lambda c: (c, 0, 0)),
                   pl.BlockSpec((CHUNK, SSM_INNER), lambda c: (c, 0))],
        out_shape=[SDS((t, SSM_INNER), F32), SDS((nc, 128, SSM_INNER), F32), SDS((t, SSM_INNER), BF16)],
        scratch_shapes=[pltpu.VMEM((128, SSM_INNER), F32)],
        compiler_params=_params("arbitrary"))(proj, proj, proj, proj, proj, proj, cwx, cbx, cwb, cbb, dtb, alog,
                                              dsk_e, norm_g, tri, expand)


def _ssd_bwd(proj, y_ssd, hs, dout, cwx, cbx, cwb, cbb, dtb, alog, dsk_e, norm_g, tri, triu, expand, expand_t):
    t = proj.shape[0]
    nc, ch, col, halo = _ssd_specs(t, True)

    def body(z_ref, xs_ref, bc_ref, dt_ref, hx_ref, hb_ref, y_ref, hin_ref, do_ref,
             cwx_ref, cbx_ref, cwb_ref, cbb_ref, dtb_ref, alog_ref, dsk_ref, g_ref, tri_ref, triu_ref, exp_ref, expt_ref,
             dz_ref, dxs_ref, dbc_ref, ddt_ref, dg_ref, ddsk_ref, dalog_ref, ddtb_ref, dcwx_ref, dcbx_ref, dcwb_ref,
             dcbb_ref, gstate, nx_x, nx_b, dact_b, dxdt_s):
        step = pl.program_id(0)
        c = nc - 1 - step

        @pl.when(step == 0)
        def _():
            gstate[...] = jnp.zeros_like(gstate)
            nx_x[...] = jnp.zeros_like(nx_x)
            nx_b[...] = jnp.zeros_like(nx_b)
            for ref in (dg_ref, ddsk_ref, dalog_ref, ddtb_ref, dcwx_ref, dcbx_ref, dcwb_ref, dcbb_ref):
                ref[...] = jnp.zeros_like(ref)

        v = _ssd_common(xs_ref, bc_ref, dt_ref, hx_ref, hb_ref, cwx_ref, cbx_ref, cwb_ref, cbb_ref, dtb_ref,
                        alog_ref, tri_ref, exp_ref, c == 0)
        acum_t = v["acum"].T
        xa, ba, xdt, dtv = v["xa"], v["ba"], v["xdt"], v["dtv"]
        lam_e, gam_e, dt_e = v["lam_e"], v["gam_e"], v["dt_e"]
        z = z_ref[...]
        y = y_ref[...]
        sz = _sigmoid(z)
        zs = z * sz
        y_tot = y + dsk_ref[...] * xa
        yg = y_tot * zs
        r = lax.rsqrt(jnp.mean(yg * yg, axis=-1, keepdims=True) + EPS)
        yh = yg * r
        do = do_ref[...]
        dg_ref[...] += jnp.sum(do * yh, axis=0, keepdims=True)
        gd = do * g_ref[...]
        dyg = r * (gd - yh * jnp.mean(gd * yh, axis=-1, keepdims=True))
        dz_ref[...] = (dyg * y_tot * (sz * (1.0 + z * (1.0 - sz)))).astype(BF16)
        dy = dyg * zs
        ddsk_ref[...] += jnp.sum(dy * xa, axis=0, keepdims=True)
        g_out = gstate[...]
        h_in = hin_ref[0]
        lam_dy = lam_e * dy
        gam_x = gam_e * xdt
        m0 = lax.broadcasted_iota(jnp.int32, (CHUNK, 128), 1) < HEAD_DIM
        lane = lax.broadcasted_iota(jnp.int32, (CHUNK, 128), 1)
        below = (lax.broadcasted_iota(jnp.int32, (CHUNK, CHUNK), 0) >
                 lax.broadcasted_iota(jnp.int32, (CHUNK, CHUNK), 1))
        da_in = jnp.zeros((CHUNK, 128), F32)
        off_y, off_x = [], []
        for g in range(2):
            bg = ba[:, g * 128:(g + 1) * 128].astype(BF16)
            cg = ba[:, 256 + g * 128:256 + (g + 1) * 128].astype(BF16)
            gl = slice(g * 512, (g + 1) * 512)
            gg = g_out[:, gl].astype(BF16)
            bc_t = _dot(bg, cg, NT)
            cb = _dot(cg, bg, NT)
            dxdt_off = _dot(bg, gg) * gam_e[:, gl]
            off_x.append(xdt[:, gl] * dxdt_off)
            off_y.append(dy[:, gl] * (_dot(cg, h_in[:, gl].astype(BF16)) * lam_e[:, gl]))
            q_sum = jnp.zeros((CHUNK, CHUNK), F32)
            for j in range(4):
                h0 = 8 * g + 2 * j
                ln = slice(g * 512 + j * 128, g * 512 + (j + 1) * 128)
                dyp = dy[:, ln]
                dyb = dyp.astype(BF16)
                xpb = xdt[:, ln].astype(BF16)
                d0 = _dot((bc_t * _decay(acum_t, h0, True)).astype(BF16), dyb)
                d1 = _dot((bc_t * _decay(acum_t, h0 + 1, True)).astype(BF16), dyb)
                dxdt_s[:, ln] = jnp.where(m0, d0, d1) + dxdt_off[:, j * 128:(j + 1) * 128]
                for hh, dym in ((h0, jnp.where(m0, dyp, 0.0)), (h0 + 1, jnp.where(m0, 0.0, dyp))):
                    qd = _dot(dym.astype(BF16), xpb, NT) * _decay(acum_t, hh, False)
                    q_sum = q_sum + qd
                    reach = jnp.where(below, _hdot(triu_ref[...], qd * cb), 0.0)
                    da_in = jnp.where(lane == hh, jnp.sum(reach, axis=-1, keepdims=True), da_in)
            gstate[:, gl] = g_out[:, gl] * lam_e[CHUNK - 1:CHUNK, gl] + _dot(cg.T, lam_dy[:, gl].astype(BF16))
            qb = q_sum.astype(BF16)
            dact_b[:, 256 + g * 128:256 + (g + 1) * 128] = (
                _dot(qb, bg) + _dot(lam_dy[:, gl].astype(BF16), h_in[:, gl].astype(BF16), NT))
            dact_b[:, g * 128:(g + 1) * 128] = _dot(qb.T, cg) + _dot(gam_x[:, gl].astype(BF16), gg, NT)
        dxdt = dxdt_s[...]
        seg_y = _hdot(jnp.concatenate(off_y, axis=1), expt_ref[...])
        seg_x = _hdot(jnp.concatenate(off_x, axis=1), expt_ref[...])
        e_col = jnp.sum(g_out * h_in * lam_e[CHUNK - 1:CHUNK, :], axis=0, keepdims=True)
        e_seg = _hdot(jnp.broadcast_to(e_col, (8, SSM_INNER)), expt_ref[...])[0:1, :]
        da = da_in + _hdot(triu_ref[...], seg_y) + (_hdot(tri_ref[...], seg_x) - seg_x) + e_seg
        a_neg = v["a_neg"]
        ddtv = da * a_neg + _hdot(dxdt * xa, expt_ref[...])
        dalog_ref[...] += jnp.sum(da * dtv, axis=0, keepdims=True) * a_neg
        lane16 = lax.broadcasted_iota(jnp.int32, (CHUNK, 128), 1) < SSM_HEADS
        draw = jnp.where(lane16, ddtv * _sigmoid(dt_ref[...] + dtb_ref[...]), 0.0)
        ddtb_ref[...] += jnp.sum(draw, axis=0, keepdims=True)
        ddt_ref[...] = draw.astype(BF16)
        dxa = dxdt * dt_e + dy * dsk_ref[...]
        for (dact, pre, x_ref, nx, cw_ref, dcw_ref, dcb_ref, dx_ref) in (
                (dxa, v["pre_x"], xs_ref, nx_x, cwx_ref, dcwx_ref, dcbx_ref, dxs_ref),
                (dact_b[...], v["pre_b"], bc_ref, nx_b, cwb_ref, dcwb_ref, dcbb_ref, dbc_ref)):
            sp = _sigmoid(pre)
            dpre = dact * (sp * (1.0 + pre * (1.0 - sp)))
            dcb_ref[...] += jnp.sum(dpre, axis=0, keepdims=True)
            xv = x_ref[...]
            nxt = nx[...]
            dx = cw_ref[3:4, :] * dpre
            dcw_ref[3:4, :] += jnp.sum(dpre * xv, axis=0, keepdims=True)
            for k in range(3):
                d_up = _shift_up(dpre, nxt, 3 - k)
                dcw_ref[k:k + 1, :] += jnp.sum(xv * d_up, axis=0, keepdims=True)
                dx = dx + cw_ref[k:k + 1, :] * d_up
            nx[...] = dpre[0:8, :]
            dx_ref[...] = dx.astype(dx_ref.dtype)

    full = lambda a: pl.BlockSpec(a.shape, lambda c: (0,) * a.ndim)
    rowblk = lambda w: pl.BlockSpec((CHUNK, w), lambda c: (ch(c), 0))
    acc = lambda a, b: pl.BlockSpec((a, b), lambda c: (0, 0))
    return pl.pallas_call(
        body, name="ssd_bwd", grid=(nc,),
        in_specs=[col(1024, 1), col(1024, 2), col(512, 6), col(128, 28), halo(1024, 2), halo(512, 6),
                  rowblk(SSM_INNER),
                  pl.BlockSpec((1, 128, SSM_INNER), lambda c: (ch(c), 0, 0)),
                  rowblk(SSM_INNER),
                  full(cwx), full(cbx), full(cwb), full(cbb), full(dtb), full(alog), full(dsk_e), full(norm_g),
                  full(tri), full(triu), full(expand), full(expand_t)],
        out_specs=[rowblk(SSM_INNER), rowblk(SSM_INNER), rowblk(512), rowblk(128),
                   acc(1, 1024), acc(1, 1024), acc(1, 128), acc(1, 128), acc(4, 1024), acc(1, 1024), acc(4, 512),
                   acc(1, 512)],
        out_shape=[SDS((t, SSM_INNER), BF16), SDS((t, SSM_INNER), BF16), SDS((t, 512), BF16), SDS((t, 128), BF16),
                   SDS((1, 1024), F32), SDS((1, 1024), F32), SDS((1, 128), F32), SDS((1, 128), F32),
                   SDS((4, 1024), F32), SDS((1, 1024), F32), SDS((4, 512), F32), SDS((1, 512), F32)],
        scratch_shapes=[pltpu.VMEM((128, SSM_INNER), F32), pltpu.VMEM((8, 1024), F32), pltpu.VMEM((8, 512), F32),
                        pltpu.VMEM((CHUNK, 512), F32), pltpu.VMEM((CHUNK, SSM_INNER), F32)],
        compiler_params=_params("arbitrary"))(proj, proj, proj, proj, proj, proj, y_ssd, hs, dout,
                                              cwx, cbx, cwb, cbb, dtb, alog, dsk_e, norm_g, tri, triu, expand,
                                              expand_t)


def _conv3(x, halo, w_ref, b_ref, part):
    acc = b_ref[part] + w_ref[2, part] * x
    for k in range(2):
        acc = acc + w_ref[k, part] * _shift_down(x, halo, 2 - k)
    return acc


def _up_act(x, g, w_up_t, cw, cb, tm=2048, tn=256):
    t, k = x.shape
    nj = D_FF // tn

    def body(x_ref, g_ref, wg_ref, wv_ref, w_ref, b_ref, u_ref, h_ref, f_ref, halo):
        i, j = pl.program_id(0), pl.program_id(1)

        @pl.when(j == 0)
        def _():
            xv = x_ref[...]
            r = lax.rsqrt(jnp.mean(xv * xv, axis=-1, keepdims=True) + EPS)
            h_ref[...] = (xv * r * g_ref[...]).astype(BF16)

        @pl.when(i == 0)
        def _():
            halo[j] = jnp.zeros((2, 8, tn), F32)

        us = [_dot(h_ref[...], wt_ref[...], NT) for wt_ref in (wg_ref, wv_ref)]
        parts = []
        for part, u in enumerate(us):
            u_ref[part] = u
            parts.append(_conv3(u, halo[j, part], w_ref, b_ref, part))
            halo[j, part] = u[tm - 8:, :]
        gate, val = parts
        f_ref[...] = (gate * _sigmoid(gate) * val).astype(BF16)

    return pl.pallas_call(
        body, name="up_proj", grid=(t // tm, nj),
        in_specs=[pl.BlockSpec((tm, k), lambda i, j: (i, 0)), pl.BlockSpec((1, k), lambda i, j: (0, 0)),
                  pl.BlockSpec((tn, k), lambda i, j: (j, 0)), pl.BlockSpec((tn, k), lambda i, j: (j + nj, 0)),
                  pl.BlockSpec((3, 2, 1, tn), lambda i, j: (0, 0, 0, j)), pl.BlockSpec((2, 1, tn), lambda i, j: (0, 0, j))],
        out_specs=[pl.BlockSpec((2, tm, tn), lambda i, j: (0, i, j)), pl.BlockSpec((tm, k), lambda i, j: (i, 0)),
                   pl.BlockSpec((tm, tn), lambda i, j: (i, j))],
        out_shape=[SDS((2, t, D_FF), F32), SDS((t, k), BF16), SDS((t, D_FF), BF16)],
        scratch_shapes=[pltpu.VMEM((nj, 2, 8, tn), F32)],
        compiler_params=_params("arbitrary", "arbitrary"))(x, g, w_up_t, w_up_t, cw, cb)


def _ffn_bwd(dx2, w_down, u, cw, cb, tm=512, tn=1408):
    t = u.shape[1]
    nj, ni = D_FF // tn, t // tm
    rev = lambda i: ni - 1 - i

    def body(dx_ref, wd_ref, u_ref, uh_ref, w_ref, b_ref, du_ref, dcw_ref, dcb_ref, nxt):
        i = pl.program_id(1)

        @pl.when(i == 0)
        def _():
            nxt[...] = jnp.zeros_like(nxt)
            dcw_ref[...] = jnp.zeros_like(dcw_ref)
            dcb_ref[...] = jnp.zeros_like(dcb_ref)

        df = _dot(dx_ref[...].astype(BF16), wd_ref[...], NT)
        keep = (i < ni - 1).astype(F32)
        ug, uv = u_ref[0], u_ref[1]
        hg, hv = uh_ref[0] * keep, uh_ref[1] * keep
        gate = _conv3(ug, hg, w_ref, b_ref, 0)
        val = _conv3(uv, hv, w_ref, b_ref, 1)
        sg = _sigmoid(gate)
        dgate = df * val * (sg * (1.0 + gate * (1.0 - sg)))
        dval = df * (gate * sg)
        for part, (d, uu) in enumerate(((dgate, ug), (dval, uv))):
            dcb_ref[part] += jnp.sum(d, axis=0, keepdims=True)
            ahead = nxt[part]
            acc = w_ref[2, part] * d
            dcw_ref[2, part] += jnp.sum(d * uu, axis=0, keepdims=True)
            for k in range(2):
                d_up = _shift_up(d, ahead, 2 - k)
                dcw_ref[k, part] += jnp.sum(uu * d_up, axis=0, keepdims=True)
                acc = acc + w_ref[k, part] * d_up
            nxt[part] = d[0:8, :]
            du_ref[part] = acc.astype(BF16)

    w_spec = pl.BlockSpec((3, 2, 1, tn), lambda j, i: (0, 0, 0, j))
    b_spec = pl.BlockSpec((2, 1, tn), lambda j, i: (0, 0, j))
    return pl.pallas_call(
        body, name="ffn_bwd", grid=(nj, ni),
        in_specs=[pl.BlockSpec((tm, D_MODEL), lambda j, i: (rev(i), 0)), pl.BlockSpec((tn, D_MODEL), lambda j, i: (j, 0)),
                  pl.BlockSpec((2, tm, tn), lambda j, i: (0, rev(i), j)),
                  pl.BlockSpec((2, 8, tn), lambda j, i: (0, jnp.maximum(rev(i) * (tm // 8) - 1, 0), j)),
                  w_spec, b_spec],
        out_specs=[pl.BlockSpec((2, tm, tn), lambda j, i: (0, rev(i), j)), w_spec, b_spec],
        out_shape=[SDS((2, t, D_FF), BF16), SDS((3, 2, 1, D_FF), F32), SDS((2, 1, D_FF), F32)],
        scratch_shapes=[pltpu.VMEM((2, 8, tn), F32)],
        compiler_params=_params("parallel", "arbitrary"))(dx2, w_down, u, u, cw, cb)


def _ple_loss(x2, g, w_gate, p, w_proj_t, target, tm=256):
    t = x2.shape[0]

    def body(x_ref, g_ref, wg_ref, p_ref, wp_ref, tg_ref, dx_ref, dpre_ref, dpp_ref, h_ref, loss_ref, dg_ref):
        i = pl.program_id(0)
        xv = x_ref[...]
        r = lax.rsqrt(jnp.mean(xv * xv, axis=-1, keepdims=True) + EPS)
        xh = xv * r
        h = (xh * g_ref[...]).astype(BF16)
        h_ref[...] = h
        gate = _sigmoid(_dot(h, wg_ref[...]))
        pp = _dot(p_ref[...].astype(BF16), wp_ref[...], NT)
        err = (xv + gate * pp) - tg_ref[...]

        @pl.when(i == 0)
        def _():
            loss_ref[...] = jnp.zeros_like(loss_ref)
            dg_ref[...] = jnp.zeros_like(dg_ref)

        loss_ref[...] += 0.5 * jnp.sum(jnp.mean(err * err, axis=-1, keepdims=True), axis=0, keepdims=True)
        dy = err * (1.0 / D_MODEL)
        dpre = (dy * pp * gate * (1.0 - gate)).astype(BF16)
        dpre_ref[...] = dpre
        dpp_ref[...] = (dy * gate).astype(BF16)
        dh = _dot(dpre, wg_ref[...], NT)
        dg_ref[...] += jnp.sum(dh * xh, axis=0, keepdims=True)
        gd = dh * g_ref[...]
        dx_ref[...] = dy + r * (gd - xh * jnp.mean(gd * xh, axis=-1, keepdims=True))

    row = lambda w: pl.BlockSpec((tm, w), lambda i: (i, 0))
    full = lambda a: pl.BlockSpec(a.shape, lambda i: (0, 0))
    return pl.pallas_call(
        body, name="ple_loss", grid=(t // tm,),
        in_specs=[row(D_MODEL), full(g), full(w_gate), row(PLE_DIM), full(w_proj_t), row(D_MODEL)],
        out_specs=[row(D_MODEL), row(D_MODEL), row(D_MODEL), row(D_MODEL),
                   pl.BlockSpec((1, 128), lambda i: (0, 0)), pl.BlockSpec((1, D_MODEL), lambda i: (0, 0))],
        out_shape=[SDS((t, D_MODEL), F32), SDS((t, D_MODEL), BF16), SDS((t, D_MODEL), BF16), SDS((t, D_MODEL), BF16),
                   SDS((1, 128), F32), SDS((1, D_MODEL), F32)],
        compiler_params=_params("arbitrary"))(x2, g, w_gate, p, w_proj_t, target)


def _exchange(scatter, gather, name):
    arrays = list(scatter) + list(gather)
    n_a, n_s = len(arrays), len(scatter)

    def body(*refs):
        src, dst = refs[:n_a], refs[n_a:2 * n_a]
        send_sems, recv_sems, local_sems = refs[2 * n_a:]
        x, y, c = lax.axis_index("x"), lax.axis_index("y"), lax.axis_index("c")
        me = 4 * x + 2 * y + c

        def src_of(a, slot):
            return src[a].at[slot] if a < n_s else src[a]

        local = [pltpu.make_async_copy(src_of(a, me), dst[a].at[me], local_sems.at[a]) for a in range(n_a)]
        for cp in local:
            cp.start()
        sends, peers = [], []
        for k in range(1, N_DEV):
            px = 1 - x if k & 4 else x
            py = 1 - y if k & 2 else y
            pc = 1 - c if k & 1 else c
            peer = 4 * px + 2 * py + pc
            peers.append(peer)
            for a in range(n_a):
                cp = pltpu.make_async_remote_copy(
                    src_ref=src_of(a, peer), dst_ref=dst[a].at[me], send_sem=send_sems.at[a, k - 1],
                    recv_sem=recv_sems.at[a, k - 1], device_id=(px, py, pc), device_id_type=pl.DeviceIdType.MESH)
                cp.start()
                sends.append(cp)
        for k in range(1, N_DEV):
            peer = peers[k - 1]
            for a in range(n_a):
                pltpu.make_async_remote_copy(
                    src_ref=src_of(a, peer), dst_ref=dst[a].at[peer], send_sem=send_sems.at[a, k - 1],
                    recv_sem=recv_sems.at[a, k - 1], device_id=(x, y, c),
                    device_id_type=pl.DeviceIdType.MESH).wait_recv()
        for cp in sends:
            cp.wait_send()
        for cp in local:
            cp.wait()

    out_shape = [SDS(a.shape, a.dtype) for a in scatter] + [SDS((N_DEV,) + a.shape, a.dtype) for a in gather]
    hbm = pl.BlockSpec(memory_space=pl.ANY)
    return pl.pallas_call(
        body, name=name, in_specs=[hbm] * n_a, out_specs=[hbm] * n_a, out_shape=out_shape,
        scratch_shapes=[pltpu.SemaphoreType.DMA((n_a, N_DEV - 1)), pltpu.SemaphoreType.DMA((n_a, N_DEV - 1)),
                        pltpu.SemaphoreType.DMA((n_a,))],
        )(*arrays)


def _peer(k):
    x, y, c = lax.axis_index("x"), lax.axis_index("y"), lax.axis_index("c")
    px = 1 - x if k & 4 else x
    py = 1 - y if k & 2 else y
    pc = 1 - c if k & 1 else c
    return (px, py, pc), 4 * px + 2 * py + pc


_HBM = pl.BlockSpec(memory_space=pltpu.HBM)
_SEM = pl.BlockSpec(memory_space=pltpu.SEMAPHORE)


def _split_copies(src, land, send_sems, recv_sems, scatter, arrivals):
    _, me = _peer(0)
    out = []
    for k in range(1, N_DEV):
        coords, peer = _peer(k)
        for a in range(len(src)):
            sem = a * (N_DEV - 1) + k - 1
            if scatter[a]:
                s, d = src[a].at[peer], land[a].at[k]
            else:
                s, d = src[a], land[a].at[peer if arrivals else me]
            out.append(pltpu.make_async_remote_copy(
                src_ref=s, dst_ref=d, send_sem=send_sems.at[sem], recv_sem=recv_sems.at[sem], device_id=coords,
                device_id_type=pl.DeviceIdType.MESH))
    return out


def _exchange_start(srcs, lands, scatter, name):
    n = len(srcs)

    def body(*refs):
        src, land = refs[:n], refs[n:2 * n]
        send_sems, recv_sems = refs[2 * n], refs[2 * n + 1]
        token = refs[-1]
        for cp in _split_copies(src, land, send_sems, recv_sems, scatter, False):
            cp.start()
        token[...] = jnp.zeros_like(token)

    hbm_shape = lambda a: pltpu.HBM(a.shape, a.dtype)
    sem_shape = pltpu.SemaphoreType.DMA((n * (N_DEV - 1),))
    outs = pl.pallas_call(
        body, name=name,
        out_shape=(sem_shape, sem_shape, *[hbm_shape(a) for a in srcs], *[hbm_shape(a) for a in lands],
                   SDS((8, 128), F32)),
        in_specs=[_HBM] * (2 * n), out_specs=(_SEM, _SEM, *[_HBM] * (2 * n), pl.BlockSpec(memory_space=pltpu.VMEM)),
        input_output_aliases={a: 2 + a for a in range(2 * n)},
        compiler_params=pltpu.CompilerParams(has_side_effects=pltpu.SideEffectType.DATAFLOW_SIDE_EFFECTING),
    )(*[pltpu.with_memory_space_constraint(a, pltpu.HBM) for a in list(srcs) + list(lands)])
    return outs[0], outs[1], outs[2:2 + n], outs[2 + n:2 + 2 * n], outs[-1]


def _exchange_wait(send_sems, recv_sems, srcs, lands, scatter, after, name):
    n = len(srcs)

    def body(*refs):
        src, land = refs[:n], refs[n:2 * n]
        for cp in _split_copies(src, land, refs[2 * n], refs[2 * n + 1], scatter, False):
            cp.wait_send()
        for cp in _split_copies(src, land, refs[2 * n], refs[2 * n + 1], scatter, True):
            cp.wait_recv()

    hbm_shape = lambda a: pltpu.HBM(a.shape, a.dtype)
    outs = pl.pallas_call(
        body, name=name, out_shape=tuple(hbm_shape(a) for a in list(srcs) + list(lands)),
        in_specs=[_HBM] * (2 * n) + [_SEM, _SEM, pl.BlockSpec(memory_space=pl.ANY)], out_specs=(_HBM,) * (2 * n),
        input_output_aliases={a: a for a in range(2 * n)},
        compiler_params=pltpu.CompilerParams(has_side_effects=pltpu.SideEffectType.DATAFLOW_SIDE_EFFECTING),
    )(*srcs, *lands, send_sems, recv_sems, after)
    return outs[:n], outs[n:]


def _reduce8(a, tr, name):
    _, rows, cols = a.shape

    def body(a_ref, o_ref):
        acc = a_ref[0]
        for j in range(1, N_DEV):
            acc = acc + a_ref[j]
        o_ref[...] = acc

    return pl.pallas_call(
        body, name=name, grid=(rows // tr,),
        in_specs=[pl.BlockSpec((N_DEV, tr, cols), lambda i: (0, i, 0))],
        out_specs=pl.BlockSpec((tr, cols), lambda i: (i, 0)), out_shape=SDS((rows, cols), F32),
        compiler_params=_params("parallel"))(a)


def _reduce_landed(own, land, name, tc=256):
    rows, cols = own.shape

    def body(own_ref, land_ref, o_ref):
        acc = own_ref[...]
        for k in range(1, N_DEV):
            acc = acc + land_ref[k].astype(F32)
        o_ref[...] = acc

    return pl.pallas_call(
        body, name=name, grid=(cols // tc,),
        in_specs=[pl.BlockSpec((rows, tc), lambda j: (0, j)), pl.BlockSpec((N_DEV, rows, tc), lambda j: (0, 0, j))],
        out_specs=pl.BlockSpec((rows, tc), lambda j: (0, j)), out_shape=SDS((rows, cols), F32),
        compiler_params=_params("parallel"))(own, land)


def _adamw(w, g, m, v, name, tr=None):
    rows, cols = w.shape
    tr = rows if tr is None else tr

    def body(w_ref, g_ref, m_ref, v_ref, d_ref, mo_ref, vo_ref):
        d_ref[...], mo_ref[...], vo_ref[...] = _adam_update(w_ref[...], g_ref[...], m_ref[...], v_ref[...])

    blk = pl.BlockSpec((tr, cols), lambda i: (i, 0))
    return pl.pallas_call(
        body, name=name, grid=(rows // tr,), in_specs=[blk] * 4, out_specs=[blk] * 3,
        out_shape=[SDS((rows, cols), F32)] * 3, compiler_params=_params("parallel"))(w, g, m, v)


def _pad_rows(a, rows):
    return jnp.pad(a, ((0, rows - a.shape[0]),) + ((0, 0),) * (a.ndim - 1))


def _local_step(x, p, target, sm, wts, fetch_rest, send, tok):
    ones_q, ones_k, dup, dup_t = _head_consts()
    tri, triu, expand, expand_t = _ssd_consts()
    w_in_t = wts["in_t"]
    cwx, cwb = wts["ssm_cw"][:, :SSM_INNER], wts["ssm_cw"][:, SSM_INNER:]
    cbx, cbb = sm["ssm_conv_b"][:, :SSM_INNER], sm["ssm_conv_b"][:, SSM_INNER:]
    pad128 = lambda a: jnp.pad(a, ((0, 0), (0, 128 - a.shape[1])))
    dtb, alog = pad128(sm["dt_bias"]), pad128(sm["a_log"])
    dsk_e = jnp.repeat(sm["d_skip"], HEAD_DIM, axis=1)
    gq = jnp.tile(sm["q_norm_g"], (1, ATTN_DIM // HEAD_DIM))
    gk = jnp.tile(sm["k_norm_g"], (1, KV_DIM // HEAD_DIM))
    ffn_cw = wts["ffn_cw"].reshape(3, 2, 1, D_FF)
    ffn_cb = sm["ffn_conv_b"].reshape(2, 1, D_FF)

    proj, h1 = _norm_matmul(x, sm["attn_norm_g"] + tok, w_in_t, 1024, 768, "in_proj")
    qn, kd, vd = _attn_prep(proj, gq, gk, ones_q, ones_k, dup)
    attn_out, lse = _attn_fwd(qn, kd, vd)
    y_ssd, hs, ssm_out = _ssd_fwd(proj, cwx, cbx, cwb, cbb, dtb, alog, dsk_e, sm["ssm_norm_g"], tri, expand)
    rest = fetch_rest(ssm_out)
    w_out, w_up_t, w_down, w_gate, w_proj_t = (rest[k] for k in ("out", "up_t", "down", "gate", "proj_t"))
    x1 = _mm_resid([(attn_out, None, w_out[:ATTN_DIM]), (ssm_out, None, w_out[ATTN_DIM:])], x, None, 512, F32,
                   "out_proj")
    u, h2, f = _up_act(x1, sm["ffn_norm_g"], w_up_t, ffn_cw, ffn_cb)
    x2 =_mm_resid([(f, None, w_down)], x1, None, 512, F32, "down_proj")
    dx2, dpre, dpp, h3, loss, dg_ple = _ple_loss(x2, sm["ple_norm_g"], w_gate, p, w_proj_t, target)

    g_gate = _wgrad(h3, None, dpre, "wg_gate")
    g_proj_t = _wgrad(dpp, None, p, "wg_proj")
    g_down = _wgrad(f, None, dx2, "wg_down")
    du, d_ffn_cw, d_ffn_cb = _ffn_bwd(dx2, w_down, u, ffn_cw, ffn_cb)
    dx1, dg_ffn = _mm_normbwd([(du, 0, w_up_t, D_FF, 0), (du, 1, w_up_t, D_FF, 1)], x1, sm["ffn_norm_g"], dx2, 256,
                              "up_proj_bwd")
    g_up_t = jnp.concatenate([_wgrad(du, 0, h2, "wg_up_gate"), _wgrad(du, 1, h2, "wg_up_val")], axis=0)
    tok = send(dict(gate=g_gate, proj_t=g_proj_t, down=g_down, up_t=g_up_t)).astype(BF16)
    d_attn = _mm_resid([(dx1, None, w_out[:ATTN_DIM] + tok)], None, NT, 512, F32, "out_proj_bwd_attn")
    d_ssm = _mm_resid([(dx1, None, w_out[ATTN_DIM:] + tok)], None, NT, 512, F32, "out_proj_bwd_ssm")
    g_out = jnp.concatenate([_wgrad(attn_out, None, dx1, "wg_out_attn"), _wgrad(ssm_out, None, dx1, "wg_out_ssm")],
                            axis=0)
    tok = send(dict(out=g_out))
    (dz, dxs, dbc, ddt, dg_ssm, d_dsk_e, d_alog, d_dtb, d_cwx, d_cbx, d_cwb, d_cbb) = _ssd_bwd(
        proj, y_ssd, hs, d_ssm, cwx, cbx, cwb, cbb, dtb + tok, alog, dsk_e, sm["ssm_norm_g"], tri, triu, expand,
        expand_t)
    dqn, dkc, dkp, dvc, dvp = _attn_bwd(qn, kd, vd, attn_out, lse, d_attn, ones_k[:128, :128])
    dqkv, dgq, dgk = _attn_prep_bwd(proj, dqn, dkc, dkp, dvc, dvp, gq + tok, gk, ones_q, ones_k, dup_t)
    pieces = [(dqkv, 0, 1024), (dz, 1024, 2048), (dxs, 2048, 3072), (dbc, 3072, 3584), (ddt, 3584, 3712)]
    g_in_t = jnp.concatenate([_wgrad(a, None, h1, "wg_in_%d" % lo) for a, lo, _ in pieces], axis=0)[:IN_PROJ]
    tok = send(dict(in_t=g_in_t))
    grad_x, dg_attn = _mm_normbwd([(a, None, w_in_t, hi - lo, lo // (hi - lo)) for a, lo, hi in pieces], x,
                                  sm["attn_norm_g"] + tok, dx1, 256, "in_proj_bwd")

    small = dict(
        attn_norm_g=dg_attn, q_norm_g=dgq.reshape(-1, HEAD_DIM).sum(0, keepdims=True),
        k_norm_g=dgk.reshape(-1, HEAD_DIM).sum(0, keepdims=True),
        ssm_conv_w=jnp.concatenate([d_cwx, d_cwb], axis=1), ssm_conv_b=jnp.concatenate([d_cbx, d_cbb], axis=1),
        dt_bias=d_dtb[:, :SSM_HEADS], a_log=d_alog[:, :SSM_HEADS],
        d_skip=d_dsk_e.reshape(SSM_HEADS, HEAD_DIM).sum(1)[None, :], ssm_norm_g=dg_ssm, ffn_norm_g=dg_ffn,
        ffn_conv_w=d_ffn_cw.reshape(3, 2 * D_FF), ffn_conv_b=d_ffn_cb.reshape(1, 2 * D_FF), ple_norm_g=dg_ple)
    return loss[0, 0], grad_x, small


_SMALL = (("attn_norm_g", 1, 1024), ("q_norm_g", 1, 64), ("k_norm_g", 1, 64), ("ssm_conv_w", 4, XBC_DIM),
          ("ssm_conv_b", 1, XBC_DIM), ("dt_bias", 1, 16), ("a_log", 1, 16), ("d_skip", 1, 16), ("ssm_norm_g", 1, 1024),
          ("ffn_norm_g", 1, 1024), ("ffn_conv_w", 3, 2 * D_FF), ("ffn_conv_b", 1, 2 * D_FF), ("ple_norm_g", 1, 1024))
_SMALL_ROWS, _SMALL_COLS = 24, 2 * D_FF
_SHARDED_SMALL = ("ssm_conv_w", "ffn_conv_w")


def _pack_small(parts):
    rows = [jnp.pad(parts[k].reshape(r, n), ((0, 0), (0, _SMALL_COLS - n))) for k, r, n in _SMALL]
    return _pad_rows(jnp.concatenate(rows, axis=0), _SMALL_ROWS)


def _adam_update(w, g, m, v):
    c1 = 1.0 - ADAM_B1 ** ADAM_STEP
    c2 = 1.0 - ADAM_B2 ** ADAM_STEP
    mn = ADAM_B1 * m + (1.0 - ADAM_B1) * g
    vn = ADAM_B2 * v + (1.0 - ADAM_B2) * (g * g)
    return -ADAM_LR * ((mn / c1) / (jnp.sqrt(vn / c2) + ADAM_EPS) + ADAM_WD * w), mn, vn


def _adamw_small(g_all, g_shard, w, m, v):
    ins, shapes = [g_all], []
    for k, _, _ in _SMALL:
        shape2 = w[k].shape if w[k].ndim == 2 else (1, w[k].shape[0])
        shapes.append(shape2)
        ins += ([g_shard[k]] if k in _SHARDED_SMALL else []) + [a.reshape(shape2) for a in (w[k], m[k], v[k])]

    def body(*refs):
        g_ref, pos, row = refs[0], 1, 0
        outs = refs[len(ins):]
        for i, (k, r, n) in enumerate(_SMALL):
            if k in _SHARDED_SMALL:
                g = refs[pos][...]
                pos += 1
            else:
                g = g_ref[row:row + r, 0:n]
            row += r
            d, mn, vn = _adam_update(refs[pos][...], g, refs[pos + 1][...], refs[pos + 2][...])
            pos += 3
            for o_ref, val in zip(outs[4 * i:4 * i + 4], (g, d, mn, vn)):
                o_ref[...] = val

    res = pl.pallas_call(body, name="adamw_small",
                         out_shape=[SDS(s, F32) for s in shapes for _ in range(4)])(*ins)
    return {k: tuple(a.reshape(w[k].shape) for a in res[4 * i:4 * i + 4]) for i, (k, _, _) in enumerate(_SMALL)}


def kernel(x, p, attn_norm_g, w_in, q_norm_g, k_norm_g, ssm_conv_w, ssm_conv_b, dt_bias, a_log, d_skip, ssm_norm_g, w_out, ffn_norm_g, w_up, ffn_conv_w, ffn_conv_b, w_down, ple_norm_g, w_ple_gate, w_ple_proj, loss_target, m_attn_norm_g, m_w_in, m_q_norm_g, m_k_norm_g, m_ssm_conv_w, m_ssm_conv_b, m_dt_bias, m_a_log, m_d_skip, m_ssm_norm_g, m_w_out, m_ffn_norm_g, m_w_up, m_ffn_conv_w, m_ffn_conv_b, m_w_down, m_ple_norm_g, m_w_ple_gate, m_w_ple_proj, v_attn_norm_g, v_w_in, v_q_norm_g, v_k_norm_g, v_ssm_conv_w, v_ssm_conv_b, v_dt_bias, v_a_log, v_d_skip, v_ssm_norm_g, v_w_out, v_ffn_norm_g, v_w_up, v_ffn_conv_w, v_ffn_conv_b, v_w_down, v_ple_norm_g, v_w_ple_gate, v_w_ple_proj):
    names = ("attn_norm_g", "w_in", "q_norm_g", "k_norm_g", "ssm_conv_w", "ssm_conv_b", "dt_bias", "a_log", "d_skip",
             "ssm_norm_g", "w_out", "ffn_norm_g", "w_up", "ffn_conv_w", "ffn_conv_b", "w_down", "ple_norm_g",
             "w_ple_gate", "w_ple_proj")
    w = dict(zip(names, (attn_norm_g, w_in, q_norm_g, k_norm_g, ssm_conv_w, ssm_conv_b, dt_bias, a_log, d_skip,
                         ssm_norm_g, w_out, ffn_norm_g, w_up, ffn_conv_w, ffn_conv_b, w_down, ple_norm_g, w_ple_gate,
                         w_ple_proj)))
    m = dict(zip(names, (m_attn_norm_g, m_w_in, m_q_norm_g, m_k_norm_g, m_ssm_conv_w, m_ssm_conv_b, m_dt_bias,
                         m_a_log, m_d_skip, m_ssm_norm_g, m_w_out, m_ffn_norm_g, m_w_up, m_ffn_conv_w, m_ffn_conv_b,
                         m_w_down, m_ple_norm_g, m_w_ple_gate, m_w_ple_proj)))
    v = dict(zip(names, (v_attn_norm_g, v_w_in, v_q_norm_g, v_k_norm_g, v_ssm_conv_w, v_ssm_conv_b, v_dt_bias,
                         v_a_log, v_d_skip, v_ssm_norm_g, v_w_out, v_ffn_norm_g, v_w_up, v_ffn_conv_w, v_ffn_conv_b,
                         v_w_down, v_ple_norm_g, v_w_ple_gate, v_w_ple_proj)))
    w, m, v = ({k: a[0] for k, a in d.items()} for d in (w, m, v))
    me = 4 * lax.axis_index("x") + 2 * lax.axis_index("y") + lax.axis_index("c")

    mine = dict(in_t=w["w_in"].T, out=w["w_out"], up_t=w["w_up"].T, down=w["w_down"], gate=w["w_ple_gate"],
                proj_t=w["w_ple_proj"].T)
    mine = {k: a.astype(BF16) for k, a in mine.items()}
    conv_pack = jnp.pad(jnp.concatenate([w["ssm_conv_w"].reshape(-1), w["ffn_conv_w"].reshape(-1)]),
                        (0, 3072 - 2880)).reshape(8, 384)
    all_in, all_conv = _exchange([], [mine["in_t"], conv_pack], "gather_first")
    later = ("out", "up_t", "down", "gate", "proj_t")
    zones = [lax.dynamic_update_slice(lax.empty((N_DEV,) + mine[k].shape, BF16), mine[k][None], (me, 0, 0))
             for k in later]
    zones, all_in, all_conv = lax.optimization_barrier((zones, all_in, all_conv))
    rest_state = _exchange_start([mine[k] for k in later], zones, [False] * len(later), "gather_rest_start")

    def fetch_rest(after):
        _, landed = _exchange_wait(*rest_state[:4], [False] * len(later), after, "gather_rest_wait")
        return {k: a.reshape(N_DEV * a.shape[1], a.shape[2]) for k, a in zip(later, landed)}

    wts = dict(in_t=_pad_rows(all_in.reshape(IN_PROJ, D_MODEL), IN_PROJ_PAD))
    conv_flat = all_conv.reshape(N_DEV, 3072)
    wts["ssm_cw"] = conv_flat[:, :768].reshape(N_DEV, 4, XBC_DIM // N_DEV).transpose(1, 0, 2).reshape(4, XBC_DIM)
    wts["ffn_cw"] = conv_flat[:, 768:2880].reshape(N_DEV, 3, 2 * D_FF // N_DEV).transpose(1, 0, 2).reshape(3, 2 * D_FF)
    sm = {k: w[k].reshape(1, -1) for k, _, _ in _SMALL if k not in _SHARDED_SMALL}

    in_flight = []

    def send(grads):
        keys = sorted(grads)
        blocks = [grads[k].reshape(N_DEV, grads[k].shape[0] // N_DEV, grads[k].shape[1]) for k in keys]
        own = [lax.dynamic_index_in_dim(a, me, 0, keepdims=False) for a in blocks]
        srcs = [a.astype(BF16) for a in blocks]
        state = _exchange_start(srcs, [lax.empty(a.shape, BF16) for a in srcs], [True] * len(keys),
                                "send_" + "_".join(keys))
        in_flight.append((keys, state, own))
        return state[4][0:1, 0:1]

    loss, grad_x, small = _local_step(x[0], p[0, 0], loss_target[0], sm, wts, fetch_rest, send,
                                      rest_state[4][0:1, 0:1])
    loss = lax.psum(loss, ("x", "y", "c"))

    (got_small,) = _exchange([], [_pack_small(small)], "gather_small_grads")
    g_small = _reduce8(got_small, _SMALL_ROWS, "reduce_small")
    grads = {}
    for keys, state, own in in_flight:
        _, landed = _exchange_wait(*state[:4], [True] * len(keys), grad_x, "wait_" + "_".join(keys))
        for k, mine_k, land in zip(keys, own, landed):
            grads[k] = _reduce_landed(mine_k, land, "reduce_" + k)
    gw = {"w_in": grads["in_t"].T, "w_out": grads["out"], "w_up": grads["up_t"].T, "w_down": grads["down"],
          "w_ple_gate": grads["gate"], "w_ple_proj": grads["proj_t"].T}
    n_ssm, n_ffn = XBC_DIM // N_DEV, 2 * D_FF // N_DEV
    g_shard = {"ssm_conv_w": lax.dynamic_slice(g_small, (3, me * n_ssm), (4, n_ssm)),
               "ffn_conv_w": lax.dynamic_slice(g_small, (13, me * n_ffn), (3, n_ffn))}

    delta, new_m, new_v = {}, {}, {}
    for k, tr in (("w_in", 256), ("w_out", None), ("w_up", 256), ("w_down", None), ("w_ple_gate", None),
                  ("w_ple_proj", None)):
        delta[k], new_m[k], new_v[k] = _adamw(w[k], gw[k], m[k], v[k], "adamw_" + k, tr)
    for k, (g_k, d_k, m_k, v_k) in _adamw_small(g_small, g_shard, w, m, v).items():
        gw[k], delta[k], new_m[k], new_v[k] = g_k, d_k, m_k, v_k

    outs = [loss, grad_x[None]]
    for d in (gw, delta, new_m, new_v):
        outs += [d[k][None] for k in names]
    return tuple(outs)
```

```python
import functools

import numpy as np
import jax
import jax.numpy as jnp
from jax import lax
from jax.experimental import pallas as pl
from jax.experimental.pallas import tpu as pltpu

F32 = jnp.float32
BF16 = jnp.bfloat16
SDS = jax.ShapeDtypeStruct
EPS = 1e-6
N_DEV = 8
D_MODEL = 1024
HEAD_DIM = 64
ATTN_DIM = 512
KV_DIM = 256
SSM_INNER = 1024
SSM_HEADS = 16
BC_DIM = 256
XBC_DIM = SSM_INNER + 2 * BC_DIM
MIX_DIM = ATTN_DIM + SSM_INNER
IN_PROJ = 3600
IN_PROJ_PAD = 3840
D_FF = 2816
PLE_DIM = 256
CHUNK = 128
SUPER = 2048
DILATIONS = (1, 4, 16)
TILE_UNROLL = 8
VMEM_LIMIT = 56 * 1024 * 1024
ADAM_LR, ADAM_B1, ADAM_B2, ADAM_EPS, ADAM_WD, ADAM_STEP = 0.001, 0.9, 0.999, 1e-08, 0.01, 10

NT = (((1,), (1,)), ((), ()))
TN = (((0,), (0,)), ((), ()))


def _params(*sem):
    return pltpu.CompilerParams(dimension_semantics=sem if sem else None, vmem_limit_bytes=VMEM_LIMIT)


def _dot(a, b, dims=None):
    if dims is None:
        return jnp.dot(a, b, preferred_element_type=F32)
    return lax.dot_general(a, b, dims, preferred_element_type=F32)


def _hdot(a, b, parts=2):
    a_exact = a.dtype == BF16
    x = b if a_exact else a
    acc = None
    for _ in range(parts):
        piece = x.astype(BF16)
        x = x - piece.astype(F32)
        d = _dot(a, piece) if a_exact else _dot(piece, b)
        acc = d if acc is None else acc + d
    return acc


def _sigmoid(x):
    return 0.5 * jnp.tanh(0.5 * x) + 0.5


def _shift_down(x, halo8, s):
    xr = pltpu.roll(x, s, 0)
    row = lax.broadcasted_iota(jnp.int32, halo8.shape, 0)
    first = jnp.where(row < s, pltpu.roll(halo8, s, 0), xr[0:8])
    return jnp.concatenate([first, xr[8:]], axis=0)


def _shift_up(x, halo8, s):
    n = x.shape[0]
    xr = pltpu.roll(x, n - s, 0)
    row = lax.broadcasted_iota(jnp.int32, halo8.shape, 0)
    last = jnp.where(row >= 8 - s, pltpu.roll(halo8, 8 - s, 0), xr[n - 8:])
    return jnp.concatenate([xr[:n - 8], last], axis=0)


def _norm_matmul(x, g, wt, tm, tn, name):
    t, k = x.shape
    n = wt.shape[0]

    def body(x_ref, g_ref, w_ref, o_ref, h_ref):
        @pl.when(pl.program_id(1) == 0)
        def _():
            xv = x_ref[...]
            r = lax.rsqrt(jnp.mean(xv * xv, axis=-1, keepdims=True) + EPS)
            h_ref[...] = (xv * r * g_ref[...]).astype(BF16)
        o_ref[...] = _dot(h_ref[...], w_ref[...], NT)

    return pl.pallas_call(
        body, name=name, grid=(t // tm, n // tn),
        in_specs=[pl.BlockSpec((tm, k), lambda i, j: (i, 0)), pl.BlockSpec((1, k), lambda i, j: (0, 0)),
                  pl.BlockSpec((tn, k), lambda i, j: (j, 0))],
        out_specs=[pl.BlockSpec((tm, tn), lambda i, j: (i, j)), pl.BlockSpec((tm, k), lambda i, j: (i, 0))],
        out_shape=[SDS((t, n), F32), SDS((t, k), BF16)],
        compiler_params=_params("parallel", "arbitrary"))(x, g, wt)


def _a_spec(a, lead, tm):
    if lead is None:
        return pl.BlockSpec((tm, a.shape[-1]), lambda i: (i, 0))
    return pl.BlockSpec((None, tm, a.shape[-1]), lambda i, _l=lead: (_l, i, 0))


def _mm_resid(pairs, res, dims, tm, out_dtype, name):
    t = pairs[0][0].shape[-2]
    n = pairs[0][2].shape[1] if dims is None else pairs[0][2].shape[0]
    np_ = len(pairs)

    def body(*refs):
        o_ref = refs[-1]
        acc = refs[2 * np_][...] if res is not None else None
        for q in range(np_):
            d = _dot(refs[q][...].astype(BF16), refs[np_ + q][...], dims)
            acc = d if acc is None else acc + d
        o_ref[...] = acc.astype(out_dtype)

    in_specs = [_a_spec(a, lead, tm) for a, lead, _ in pairs]
    in_specs += [pl.BlockSpec(b.shape, lambda i: (0, 0)) for _, _, b in pairs]
    args = [a for a, _, _ in pairs] + [b for _, _, b in pairs]
    if res is not None:
        in_specs.append(pl.BlockSpec((tm, n), lambda i: (i, 0)))
        args.append(res)
    return pl.pallas_call(
        body, name=name, grid=(t // tm,), in_specs=in_specs,
        out_specs=pl.BlockSpec((tm, n), lambda i: (i, 0)), out_shape=SDS((t, n), out_dtype),
        compiler_params=_params("parallel"))(*args)


def _mm_normbwd(pairs, x, g, dres, tm, name):
    t, k = x.shape
    np_ = len(pairs)
    b_specs = [pl.BlockSpec((rows, b.shape[1]), lambda i, _b=blk: (_b, 0)) for _, _, b, rows, blk in pairs]
    pairs = [(a, lead, b) for a, lead, b, _, _ in pairs]

    def body(*refs):
        x_ref, g_ref, dres_ref, dx_ref, dg_ref = refs[2 * np_:]
        dh = None
        for q in range(np_):
            d = _dot(refs[q][...], refs[np_ + q][...])
            dh = d if dh is None else dh + d
        xv = x_ref[...]
        r = lax.rsqrt(jnp.mean(xv * xv, axis=-1, keepdims=True) + EPS)
        xh = xv * r

        @pl.when(pl.program_id(0) == 0)
        def _():
            dg_ref[...] = jnp.zeros_like(dg_ref)
        dg_ref[...] += jnp.sum(dh * xh, axis=0, keepdims=True)
        gd = dh * g_ref[...]
        dx_ref[...] = dres_ref[...] + r * (gd - xh * jnp.mean(gd * xh, axis=-1, keepdims=True))

    in_specs = [_a_spec(a, lead, tm) for a, lead, _ in pairs] + b_specs
    in_specs += [pl.BlockSpec((tm, k), lambda i: (i, 0)), pl.BlockSpec((1, k), lambda i: (0, 0)),
                 pl.BlockSpec((tm, k), lambda i: (i, 0))]
    args = [a for a, _, _ in pairs] + [b for _, _, b in pairs] + [x, g, dres]
    return pl.pallas_call(
        body, name=name, grid=(t // tm,), in_specs=in_specs,
        out_specs=[pl.BlockSpec((tm, k), lambda i: (i, 0)), pl.BlockSpec((1, k), lambda i: (0, 0))],
        out_shape=[SDS((t, k), F32), SDS((1, k), F32)],
        compiler_params=_params("arbitrary"))(*args)


def _wgrad(a, a_lead, b, name, tk=2048):
    t, m = a.shape[-2:]
    n = b.shape[1]
    tm = m if m <= 1024 else 1408
    assert m % tm == 0

    def body(a_ref, b_ref, o_ref):
        @pl.when(pl.program_id(1) == 0)
        def _():
            o_ref[...] = jnp.zeros_like(o_ref)
        o_ref[...] += _dot(a_ref[...].astype(BF16), b_ref[...].astype(BF16), TN)

    if a_lead is None:
        a_spec = pl.BlockSpec((tk, tm), lambda mi, ki: (ki, mi))
    else:
        a_spec = pl.BlockSpec((None, tk, tm), lambda mi, ki, _l=a_lead: (_l, ki, mi))
    return pl.pallas_call(
        body, name=name, grid=(m // tm, t // tk),
        in_specs=[a_spec, pl.BlockSpec((tk, n), lambda mi, ki: (ki, 0))],
        out_specs=pl.BlockSpec((tm, n), lambda mi, ki: (mi, 0)), out_shape=SDS((m, n), F32),
        compiler_params=_params("parallel", "arbitrary"))(a, b)


def _head_consts():
    iq = np.arange(ATTN_DIM)
    ik = np.arange(KV_DIM)
    ones_q = (iq[:, None] // HEAD_DIM == iq[None, :] // HEAD_DIM).astype(np.float32)
    ones_k = (ik[:, None] // HEAD_DIM == ik[None, :] // HEAD_DIM).astype(np.float32)
    dup = (ik[:, None] == (HEAD_DIM * (iq[None, :] // 128) + iq[None, :] % HEAD_DIM)).astype(np.float32)
    return jnp.asarray(ones_q, BF16), jnp.asarray(ones_k, BF16), jnp.asarray(dup, BF16), jnp.asarray(dup.T, BF16)


def _attn_prep(proj, gq, gk, ones_q, ones_k, dup, tm=512):
    t = proj.shape[0]

    def body(p_ref, gq_ref, gk_ref, oq_ref, ok_ref, dup_ref, qn_ref, kd_ref, vd_ref):
        q = p_ref[:, 0:ATTN_DIM]
        k = p_ref[:, ATTN_DIM:ATTN_DIM + KV_DIM]
        v = p_ref[:, ATTN_DIM + KV_DIM:]
        rq = lax.rsqrt(_hdot(q * q, oq_ref[...]) * (1.0 / HEAD_DIM) + EPS)
        qn_ref[...] = (q * rq * gq_ref[...]) * (HEAD_DIM ** -0.5)
        rk = lax.rsqrt(_hdot(k * k, ok_ref[...]) * (1.0 / HEAD_DIM) + EPS)
        kn = k * rk * gk_ref[...]
        kd_ref[...] = _dot(kn.astype(BF16), dup_ref[...])
        vd_ref[...] = _dot(v.astype(BF16), dup_ref[...])

    full = lambda a: pl.BlockSpec(a.shape, lambda i: (0, 0))
    o_spec = pl.BlockSpec((tm, ATTN_DIM), lambda i: (i, 0))
    return pl.pallas_call(
        body, name="attn_prep", grid=(t // tm,),
        in_specs=[pl.BlockSpec((tm, 1024), lambda i: (i, 0)), full(gq), full(gk), full(ones_q), full(ones_k), full(dup)],
        out_specs=[o_spec, o_spec, o_spec], out_shape=[SDS((t, ATTN_DIM), F32)] * 3,
        compiler_params=_params("parallel"))(proj, gq, gk, ones_q, ones_k, dup)


def _attn_prep_bwd(proj, dqn, dkc, dkp, dvc, dvp, gq, gk, ones_q, ones_k, dup_t, tm=512):
    t = proj.shape[0]
    nblk = t // tm
    off = SUPER // tm

    def body(p_ref, dqn_ref, dkc_ref, dkp_ref, dvc_ref, dvp_ref, gq_ref, gk_ref, oq_ref, ok_ref, dt_ref,
             o_ref, dgq_ref, dgk_ref):
        i = pl.program_id(0)
        has_next = (i + off < nblk).astype(F32)
        q = p_ref[:, 0:ATTN_DIM]
        k = p_ref[:, ATTN_DIM:ATTN_DIM + KV_DIM]
        dkn = _hdot(dkc_ref[...] + has_next * dkp_ref[...], dt_ref[...])
        dv = _hdot(dvc_ref[...] + has_next * dvp_ref[...], dt_ref[...])

        @pl.when(i == 0)
        def _():
            dgq_ref[...] = jnp.zeros_like(dgq_ref)
            dgk_ref[...] = jnp.zeros_like(dgk_ref)

        rq = lax.rsqrt(_hdot(q * q, oq_ref[...]) * (1.0 / HEAD_DIM) + EPS)
        xh = q * rq
        dy = dqn_ref[...] * (HEAD_DIM ** -0.5)
        dgq_ref[...] += jnp.sum(dy * xh, axis=0, keepdims=True)
        gd = dy * gq_ref[...]
        dq = rq * (gd - xh * (_hdot(gd * xh, oq_ref[...]) * (1.0 / HEAD_DIM)))
        rk = lax.rsqrt(_hdot(k * k, ok_ref[...]) * (1.0 / HEAD_DIM) + EPS)
        kh = k * rk
        dgk_ref[...] += jnp.sum(dkn * kh, axis=0, keepdims=True)
        gdk = dkn * gk_ref[...]
        dk = rk * (gdk - kh * (_hdot(gdk * kh, ok_ref[...]) * (1.0 / HEAD_DIM)))
        o_ref[:, 0:ATTN_DIM] = dq.astype(BF16)
        o_ref[:, ATTN_DIM:ATTN_DIM + KV_DIM] = dk.astype(BF16)
        o_ref[:, ATTN_DIM + KV_DIM:] = dv.astype(BF16)

    full = lambda a: pl.BlockSpec(a.shape, lambda i: (0, 0))
    cur = pl.BlockSpec((tm, ATTN_DIM), lambda i: (i, 0))
    nxt = pl.BlockSpec((tm, ATTN_DIM), lambda i: (jnp.minimum(i + off, nblk - 1), 0))
    return pl.pallas_call(
        body, name="attn_prep_bwd", grid=(nblk,),
        in_specs=[pl.BlockSpec((tm, 1024), lambda i: (i, 0)), cur, cur, nxt, cur, nxt,
                  full(gq), full(gk), full(ones_q), full(ones_k), full(dup_t)],
        out_specs=[pl.BlockSpec((tm, 1024), lambda i: (i, 0)), pl.BlockSpec((1, ATTN_DIM), lambda i: (0, 0)),
                   pl.BlockSpec((1, KV_DIM), lambda i: (0, 0))],
        out_shape=[SDS((t, 1024), BF16), SDS((1, ATTN_DIM), F32), SDS((1, KV_DIM), F32)],
        compiler_params=_params("arbitrary"))(proj, dqn, dkc, dkp, dvc, dvp, gq, gk, ones_q, ones_k, dup_t)


def _tile_masks():
    qi = lax.broadcasted_iota(jnp.int32, (2 * CHUNK, 2 * CHUNK), 0) & (CHUNK - 1)
    kj = lax.broadcasted_iota(jnp.int32, (2 * CHUNK, 2 * CHUNK), 1)
    delta = CHUNK + qi - kj
    band = (delta >= 0) & (delta <= CHUNK)
    return band, kj


def _deinterleave(dst, src, n_rows, d):
    per = n_rows // d
    for r in range(d):
        dst[r * per:(r + 1) * per, :] = src[pl.ds(r, per, stride=d), :]


def _attn_specs(t):
    blk = lambda f: pl.BlockSpec((SUPER, 128), f)
    cur = blk(lambda h, s: (s, h))
    prev = blk(lambda h, s: (jnp.maximum(s - 1, 0), h))
    return cur, prev


def _attn_fwd(qn, kd, vd):
    t = qn.shape[0]
    cur, prev = _attn_specs(t)

    def body(q_ref, kp_ref, kc_ref, vp_ref, vc_ref, o_ref, lse_ref, kk, vv, qd, kdd, vdd, po, pm, pll, acc, mm, ll):
        s = pl.program_id(1)
        kk[0:SUPER, :] = kp_ref[...]
        kk[SUPER:, :] = kc_ref[...]
        vv[0:SUPER, :] = vp_ref[...]
        vv[SUPER:, :] = vc_ref[...]
        m0 = lax.broadcasted_iota(jnp.int32, (CHUNK, 128), 1) < HEAD_DIM
        band, kj = _tile_masks()
        for d in DILATIONS:
            lq = SUPER // d
            if d == 1:
                qs_ref, ks_ref, vs_ref = q_ref, kk, vv
            else:
                _deinterleave(qd, q_ref, SUPER, d)
                _deinterleave(kdd, kk, 2 * SUPER, d)
                _deinterleave(vdd, vv, 2 * SUPER, d)
                qs_ref, ks_ref, vs_ref = qd, kdd, vdd

            nblk = lq // CHUNK

            def key_rows(ti):
                return pl.ds((ti // nblk) * 2 * lq + lq + (ti % nblk - 1) * CHUNK, 2 * CHUNK)

            def scores(ti):
                qt = qs_ref[pl.ds(ti * CHUNK, CHUNK), :]
                qs = jnp.concatenate([jnp.where(m0, qt, 0.0), jnp.where(m0, 0.0, qt)], axis=0).astype(BF16)
                return _dot(qs, ks_ref[key_rows(ti), :].astype(BF16), NT)

            def softmax_pv(ti, sc):
                ok = band if ti % nblk > 0 else band & (kj >= jnp.where(s > 0, 0, CHUNK))
                sc = jnp.where(ok, sc, -jnp.inf)
                mt = jnp.max(sc, axis=-1, keepdims=True)
                p = jnp.exp(sc - mt)
                lt = jnp.sum(p, axis=-1, keepdims=True)
                ot = _dot(p.astype(BF16), vs_ref[key_rows(ti), :].astype(BF16))
                qrows = pl.ds(ti * CHUNK, CHUNK)
                po[qrows, :] = jnp.where(m0, ot[:CHUNK], ot[CHUNK:])
                pm[qrows, :] = jnp.where(m0, mt[:CHUNK], mt[CHUNK:])
                pll[qrows, :] = jnp.where(m0, lt[:CHUNK], lt[CHUNK:])

            for ti in range(SUPER // CHUNK):
                softmax_pv(ti, scores(ti))
            if d == 1:
                acc[...] = po[...]
                mm[...] = pm[...]
                ll[...] = pll[...]
            else:
                for r in range(d):
                    rows = pl.ds(r, lq, stride=d)
                    seg = slice(r * lq, (r + 1) * lq)
                    m_old, m_new = mm[rows, :], pm[seg, :]
                    m_all = jnp.maximum(m_old, m_new)
                    a, b = jnp.exp(m_old - m_all), jnp.exp(m_new - m_all)
                    acc[rows, :] = acc[rows, :] * a + po[seg, :] * b
                    ll[rows, :] = ll[rows, :] * a + pll[seg, :] * b
                    mm[rows, :] = m_all
        o_ref[...] = acc[...] / ll[...]
        lse_ref[...] = mm[...] + jnp.log(ll[...])

    big = pltpu.VMEM((2 * SUPER, 128), F32)
    one = pltpu.VMEM((SUPER, 128), F32)
    return pl.pallas_call(
        body, name="attn_fwd", grid=(4, t // SUPER),
        in_specs=[cur, prev, cur, prev, cur], out_specs=[cur, cur],
        out_shape=[SDS((t, ATTN_DIM), F32)] * 2,
        scratch_shapes=[big, big, one, big, big, one, one, one, one, one, one],
        compiler_params=_params("parallel", "arbitrary"))(qn, kd, kd, vd, vd)


def _attn_bwd(qn, kd, vd, out, lse, dout, ones_pair):
    t = qn.shape[0]
    cur, prev = _attn_specs(t)

    def body(q_ref, kp_ref, kc_ref, vp_ref, vc_ref, o_ref, lse_ref, do_ref, ones_ref,
             dq_ref, dkc_ref, dkp_ref, dvc_ref, dvp_ref,
             kk, vv, od, ld, kb, vb, qsb, dosb, tk, tv, pdq, delta):
        s = pl.program_id(1)
        delta[...] = _hdot(do_ref[...] * o_ref[...], ones_ref[...])

        def per_row(a):
            ar = pltpu.roll(a, HEAD_DIM, 1)
            rows = jnp.concatenate([jnp.where(m0, a, ar), jnp.where(m0, ar, a)], axis=0)
            return jnp.concatenate([rows, rows], axis=1)

        kk[0:SUPER, :] = kp_ref[...]
        kk[SUPER:, :] = kc_ref[...]
        vv[0:SUPER, :] = vp_ref[...]
        vv[SUPER:, :] = vc_ref[...]
        for ref in (dq_ref, dkc_ref, dkp_ref, dvc_ref, dvp_ref):
            ref[...] = jnp.zeros_like(ref)
        m0 = lax.broadcasted_iota(jnp.int32, (CHUNK, 128), 1) < HEAD_DIM
        band, kj = _tile_masks()
        ninf = -jnp.inf
        for d in DILATIONS:
            lq = SUPER // d
            nblk = lq // CHUNK
            for r in range(d):
                seg = slice(r * 2 * lq, (r + 1) * 2 * lq)
                kb[seg, :] = kk[pl.ds(r, 2 * lq, stride=d), :].astype(BF16)
                vb[seg, :] = vv[pl.ds(r, 2 * lq, stride=d), :].astype(BF16)
            for ti in range(SUPER // CHUNK):
                rows = pl.ds(ti // nblk + d * CHUNK * (ti % nblk), CHUNK, stride=d)
                for src, dst in ((q_ref, qsb), (do_ref, dosb)):
                    a = src[rows, :]
                    dst[ti * 2 * CHUNK:(ti + 1) * 2 * CHUNK, :] = jnp.concatenate(
                        [jnp.where(m0, a, 0.0), jnp.where(m0, 0.0, a)], axis=0).astype(BF16)
                ld[ti * CHUNK:(ti + 1) * CHUNK, :] = lse_ref[rows, :]
                od[ti * CHUNK:(ti + 1) * CHUNK, :] = delta[rows, :]

            def operands(ti):
                r, nb = ti // nblk, ti % nblk
                stacked = slice(ti * 2 * CHUNK, (ti + 1) * 2 * CHUNK)
                krows = pl.ds(r * 2 * lq + lq + (nb - 1) * CHUNK, 2 * CHUNK)
                return stacked, krows

            def scores(ti):
                stacked, krows = operands(ti)
                kt = kb[krows, :]
                return dict(ti=ti, sc=_dot(qsb[stacked, :], kt, NT), dp=_dot(dosb[stacked, :], vb[krows, :], NT))

            def softmax_grad(c):
                qrows = slice(c["ti"] * CHUNK, (c["ti"] + 1) * CHUNK)
                ok = band if c["ti"] % nblk > 0 else band & (kj >= jnp.where(s > 0, 0, CHUNK))
                p = jnp.exp(jnp.where(ok, c.pop("sc"), ninf) - per_row(ld[qrows, :]))
                ds = p * (c.pop("dp") - per_row(od[qrows, :]))
                c.update(p=p.astype(BF16), ds=ds.astype(BF16))
                return c

            def grads(c):
                ti = c["ti"]
                stacked, krows = operands(ti)
                dqs = _dot(c["ds"], kb[krows, :])
                pdq[ti * CHUNK:(ti + 1) * CHUNK, :] = jnp.where(m0, dqs[:CHUNK], dqs[CHUNK:])
                tk[stacked, :] = _dot(c["ds"], qsb[stacked, :], TN)
                tv[stacked, :] = _dot(c["p"], dosb[stacked, :], TN)

            n_tiles = SUPER // CHUNK
            stage_a = scores(0)
            for ti in range(n_tiles):
                ahead = scores(ti + 1) if ti + 1 < n_tiles else None
                grads(softmax_grad(stage_a))
                stage_a = ahead

            for r in range(d):
                dq_ref[pl.ds(r, lq, stride=d), :] += pdq[r * lq:(r + 1) * lq, :]
                for tile_out, cur_ref, prev_ref in ((tk, dkc_ref, dkp_ref), (tv, dvc_ref, dvp_ref)):
                    first = r * nblk * 2 * CHUNK
                    prev_ref[pl.ds(SUPER - CHUNK * d + r, CHUNK, stride=d), :] += tile_out[first:first + CHUNK, :]
                    for nb in range(nblk):
                        at = (r * nblk + nb) * 2 * CHUNK
                        part = tile_out[at + CHUNK:at + 2 * CHUNK, :]
                        if nb + 1 < nblk:
                            part = part + tile_out[at + 2 * CHUNK:at + 3 * CHUNK, :]
                        cur_ref[pl.ds(r + d * nb * CHUNK, CHUNK, stride=d), :] += part

    big = pltpu.VMEM((2 * SUPER, 128), F32)
    one = pltpu.VMEM((SUPER, 128), F32)
    half = pltpu.VMEM((2 * SUPER, 128), BF16)
    return pl.pallas_call(
        body, name="attn_bwd", grid=(4, t // SUPER),
        in_specs=[cur, prev, cur, prev, cur, cur, cur, cur, pl.BlockSpec((128, 128), lambda h, s: (0, 0))],
        out_specs=[cur] * 5, out_shape=[SDS((t, ATTN_DIM), F32)] * 5,
        scratch_shapes=[big, big, one, one, half, half, half, half, big, big, one, one],
        compiler_params=_params("parallel", "arbitrary"))(qn, kd, kd, vd, vd, out, lse, dout, ones_pair)


def _ssd_consts():
    tri = np.tril(np.ones((CHUNK, CHUNK), np.float32))
    expand = np.zeros((128, SSM_INNER), np.float32)
    for h in range(SSM_HEADS):
        expand[h, h * HEAD_DIM:(h + 1) * HEAD_DIM] = 1.0
    return jnp.asarray(tri, BF16), jnp.asarray(tri.T, BF16), jnp.asarray(expand, BF16), jnp.asarray(expand.T, BF16)


def _conv4(x, halo, w_ref, b_ref):
    acc = b_ref[...] + w_ref[3:4, :] * x
    for k in range(3):
        acc = acc + w_ref[k:k + 1, :] * _shift_down(x, halo, 3 - k)
    return acc


def _softplus(x):
    return jnp.maximum(x, 0.0) + jnp.log(1.0 + jnp.exp(-jnp.abs(x)))


def _ssd_common(xs_ref, bc_ref, dt_ref, hx_ref, hb_ref, cwx_ref, cbx_ref, cwb_ref, cbb_ref, dtb_ref, alog_ref,
                tri_ref, exp_ref, first):
    keep = 1.0 - first.astype(F32)
    hx = hx_ref[...] * keep
    hb = hb_ref[...] * keep
    pre_x = _conv4(xs_ref[...], hx, cwx_ref, cbx_ref)
    pre_b = _conv4(bc_ref[...], hb, cwb_ref, cbb_ref)
    xa = pre_x * _sigmoid(pre_x)
    ba = pre_b * _sigmoid(pre_b)
    dtv = _softplus(dt_ref[...] + dtb_ref[...])
    a_neg = -jnp.exp(alog_ref[...])
    acum = _hdot(tri_ref[...], dtv * a_neg, parts=3)
    lam = jnp.exp(acum)
    gam = jnp.exp(acum[CHUNK - 1:CHUNK, :] - acum)
    dt_e = _hdot(dtv, exp_ref[...])
    lam_e = _hdot(lam, exp_ref[...])
    gam_e = _hdot(gam, exp_ref[...])
    return dict(hx=hx, hb=hb, pre_x=pre_x, pre_b=pre_b, xa=xa, ba=ba, dtv=dtv, a_neg=a_neg, acum=acum,
                dt_e=dt_e, lam_e=lam_e, gam_e=gam_e, xdt=xa * dt_e)


def _decay(acum_t, h, transposed):
    rb = jnp.broadcast_to(acum_t[h:h + 1, :], (CHUNK, CHUNK))
    ri = lax.broadcasted_iota(jnp.int32, (CHUNK, CHUNK), 0)
    ci = lax.broadcasted_iota(jnp.int32, (CHUNK, CHUNK), 1)
    if transposed:
        return jnp.exp(jnp.where(ci >= ri, rb - rb.T, -jnp.inf))
    return jnp.exp(jnp.where(ri >= ci, rb.T - rb, -jnp.inf))


def _ssd_specs(t, rev):
    nc = t // CHUNK
    ch = (lambda c: nc - 1 - c) if rev else (lambda c: c)
    col = lambda w, j: pl.BlockSpec((CHUNK, w), lambda c: (ch(c), j))
    halo = lambda w, j: pl.BlockSpec((8, w), lambda c: (jnp.maximum(ch(c) * (CHUNK // 8) - 1, 0), j))
    return nc, ch, col, halo


def _ssd_fwd(proj, cwx, cbx, cwb, cbb, dtb, alog, dsk_e, norm_g, tri, expand):
    t = proj.shape[0]
    nc, _, col, halo = _ssd_specs(t, False)

    def body(z_ref, xs_ref, bc_ref, dt_ref, hx_ref, hb_ref, cwx_ref, cbx_ref, cwb_ref, cbb_ref, dtb_ref, alog_ref,
             dsk_ref, g_ref, tri_ref, exp_ref, y_ref, hs_ref, o_ref, state):
        c = pl.program_id(0)

        @pl.when(c == 0)
        def _():
            state[...] = jnp.zeros_like(state)

        v = _ssd_common(xs_ref, bc_ref, dt_ref, hx_ref, hb_ref, cwx_ref, cbx_ref, cwb_ref, cbb_ref, dtb_ref,
                        alog_ref, tri_ref, exp_ref, c == 0)
        acum_t = v["acum"].T
        xdt, ba = v["xdt"], v["ba"]
        h_in = state[...]
        hs_ref[0] = h_in
        xg = xdt * v["gam_e"]
        m0 = lax.broadcasted_iota(jnp.int32, (CHUNK, 128), 1) < HEAD_DIM
        for g in range(2):
            bg = ba[:, g * 128:(g + 1) * 128].astype(BF16)
            cg = ba[:, 256 + g * 128:256 + (g + 1) * 128].astype(BF16)
            gl = slice(g * 512, (g + 1) * 512)
            cb = _dot(cg, bg, NT)
            y_off = _dot(cg, h_in[:, gl].astype(BF16)) * v["lam_e"][:, gl]
            s_new = _dot(bg.T, xg[:, gl].astype(BF16))
            state[:, gl] = h_in[:, gl] * v["lam_e"][CHUNK - 1:CHUNK, gl] + s_new
            for j in range(4):
                h0 = 8 * g + 2 * j
                ln = slice(g * 512 + j * 128, g * 512 + (j + 1) * 128)
                xp = xdt[:, ln].astype(BF16)
                y0 = _dot((cb * _decay(acum_t, h0, False)).astype(BF16), xp)
                y1 = _dot((cb * _decay(acum_t, h0 + 1, False)).astype(BF16), xp)
                y_ref[:, ln] = jnp.where(m0, y0, y1) + y_off[:, j * 128:(j + 1) * 128]
        z = z_ref[...]
        yg = (y_ref[...] + dsk_ref[...] * v["xa"]) * (z * _sigmoid(z))
        r = lax.rsqrt(jnp.mean(yg * yg, axis=-1, keepdims=True) + EPS)
        o_ref[...] = (yg * r * g_ref[...]).astype(BF16)

    full = lambda a: pl.BlockSpec(a.shape, lambda c: (0,) * a.ndim)
    return pl.pallas_call(
        body, name="ssd_fwd", grid=(nc,),
        in_specs=[col(1024, 1), col(1024, 2), col(512, 6), col(128, 28), halo(1024, 2), halo(512, 6),
                  full(cwx), full(cbx), full(cwb), full(cbb), full(dtb), full(alog), full(dsk_e), full(norm_g),
                  full(tri), full(expand)],
        out_specs=[pl.BlockSpec((CHUNK, SSM_INNER), lambda c: (c, 0)),
                   pl.BlockSpec((1, 128, SSM_INNER), lambda c: (c, 0, 0)),
                   pl.BlockSpec((CHUNK, SSM_INNER), lambda c: (c, 0))],
        out_shape=[SDS((t, SSM_INNER), F32), SDS((nc, 128, SSM_INNER), F32), SDS((t, SSM_INNER), BF16)],
        scratch_shapes=[pltpu.VMEM((128, SSM_INNER), F32)],
        compiler_params=_params("arbitrary"))(proj, proj, proj, proj, proj, proj, cwx, cbx, cwb, cbb, dtb, alog,
                                              dsk_e, norm_g, tri, expand)


def _ssd_bwd(proj, y_ssd, hs, dout, cwx, cbx, cwb, cbb, dtb, alog, dsk_e, norm_g, tri, triu, expand, expand_t):
    t = proj.shape[0]
    nc, ch, col, halo = _ssd_specs(t, True)

    def body(z_ref, xs_ref, bc_ref, dt_ref, hx_ref, hb_ref, y_ref, hin_ref, do_ref,
             cwx_ref, cbx_ref, cwb_ref, cbb_ref, dtb_ref, alog_ref, dsk_ref, g_ref, tri_ref, triu_ref, exp_ref, expt_ref,
             dz_ref, dxs_ref, dbc_ref, ddt_ref, dg_ref, ddsk_ref, dalog_ref, ddtb_ref, dcwx_ref, dcbx_ref, dcwb_ref,
             dcbb_ref, gstate, nx_x, nx_b, dact_b, dxdt_s):
        step = pl.program_id(0)
        c = nc - 1 - step

        @pl.when(step == 0)
        def _():
            gstate[...] = jnp.zeros_like(gstate)
            nx_x[...] = jnp.zeros_like(nx_x)
            nx_b[...] = jnp.zeros_like(nx_b)
            for ref in (dg_ref, ddsk_ref, dalog_ref, ddtb_ref, dcwx_ref, dcbx_ref, dcwb_ref, dcbb_ref):
                ref[...] = jnp.zeros_like(ref)

        v = _ssd_common(xs_ref, bc_ref, dt_ref, hx_ref, hb_ref, cwx_ref, cbx_ref, cwb_ref, cbb_ref, dtb_ref,
                        alog_ref, tri_ref, exp_ref, c == 0)
        acum_t = v["acum"].T
        xa, ba, xdt, dtv = v["xa"], v["ba"], v["xdt"], v["dtv"]
        lam_e, gam_e, dt_e = v["lam_e"], v["gam_e"], v["dt_e"]
        z = z_ref[...]
        y = y_ref[...]
        sz = _sigmoid(z)
        zs = z * sz
        y_tot = y + dsk_ref[...] * xa
        yg = y_tot * zs
        r = lax.rsqrt(jnp.mean(yg * yg, axis=-1, keepdims=True) + EPS)
        yh = yg * r
        do = do_ref[...]
        dg_ref[...] += jnp.sum(do * yh, axis=0, keepdims=True)
        gd = do * g_ref[...]
        dyg = r * (gd - yh * jnp.mean(gd * yh, axis=-1, keepdims=True))
        dz_ref[...] = (dyg * y_tot * (sz * (1.0 + z * (1.0 - sz)))).astype(BF16)
        dy = dyg * zs
        ddsk_ref[...] += jnp.sum(dy * xa, axis=0, keepdims=True)
        g_out = gstate[...]
        h_in = hin_ref[0]
        lam_dy = lam_e * dy
        gam_x = gam_e * xdt
        m0 = lax.broadcasted_iota(jnp.int32, (CHUNK, 128), 1) < HEAD_DIM
        lane = lax.broadcasted_iota(jnp.int32, (CHUNK, 128), 1)
        below = (lax.broadcasted_iota(jnp.int32, (CHUNK, CHUNK), 0) >
                 lax.broadcasted_iota(jnp.int32, (CHUNK, CHUNK), 1))
        da_in = jnp.zeros((CHUNK, 128), F32)
        off_y, off_x = [], []
        for g in range(2):
            bg = ba[:, g * 128:(g + 1) * 128].astype(BF16)
            cg = ba[:, 256 + g * 128:256 + (g + 1) * 128].astype(BF16)
            gl = slice(g * 512, (g + 1) * 512)
            gg = g_out[:, gl].astype(BF16)
            bc_t = _dot(bg, cg, NT)
            cb = _dot(cg, bg, NT)
            dxdt_off = _dot(bg, gg) * gam_e[:, gl]
            off_x.append(xdt[:, gl] * dxdt_off)
            off_y.append(dy[:, gl] * (_dot(cg, h_in[:, gl].astype(BF16)) * lam_e[:, gl]))
            q_sum = jnp.zeros((CHUNK, CHUNK), F32)
            for j in range(4):
                h0 = 8 * g + 2 * j
                ln = slice(g * 512 + j * 128, g * 512 + (j + 1) * 128)
                dyp = dy[:, ln]
                dyb = dyp.astype(BF16)
                xpb = xdt[:, ln].astype(BF16)
                d0 = _dot((bc_t * _decay(acum_t, h0, True)).astype(BF16), dyb)
                d1 = _dot((bc_t * _decay(acum_t, h0 + 1, True)).astype(BF16), dyb)
                dxdt_s[:, ln] = jnp.where(m0, d0, d1) + dxdt_off[:, j * 128:(j + 1) * 128]
                for hh, dym in ((h0, jnp.where(m0, dyp, 0.0)), (h0 + 1, jnp.where(m0, 0.0, dyp))):
                    qd = _dot(dym.astype(BF16), xpb, NT) * _decay(acum_t, hh, False)
                    q_sum = q_sum + qd
                    reach = jnp.where(below, _hdot(triu_ref[...], qd * cb), 0.0)
                    da_in = jnp.where(lane == hh, jnp.sum(reach, axis=-1, keepdims=True), da_in)
            gstate[:, gl] = g_out[:, gl] * lam_e[CHUNK - 1:CHUNK, gl] + _dot(cg.T, lam_dy[:, gl].astype(BF16))
            qb = q_sum.astype(BF16)
            dact_b[:, 256 + g * 128:256 + (g + 1) * 128] = (
                _dot(qb, bg) + _dot(lam_dy[:, gl].astype(BF16), h_in[:, gl].astype(BF16), NT))
            dact_b[:, g * 128:(g + 1) * 128] = _dot(qb.T, cg) + _dot(gam_x[:, gl].astype(BF16), gg, NT)
        dxdt = dxdt_s[...]
        seg_y = _hdot(jnp.concatenate(off_y, axis=1), expt_ref[...])
        seg_x = _hdot(jnp.concatenate(off_x, axis=1), expt_ref[...])
        e_col = jnp.sum(g_out * h_in * lam_e[CHUNK - 1:CHUNK, :], axis=0, keepdims=True)
        e_seg = _hdot(jnp.broadcast_to(e_col, (8, SSM_INNER)), expt_ref[...])[0:1, :]
        da = da_in + _hdot(triu_ref[...], seg_y) + (_hdot(tri_ref[...], seg_x) - seg_x) + e_seg
        a_neg = v["a_neg"]
        ddtv = da * a_neg + _hdot(dxdt * xa, expt_ref[...])
        dalog_ref[...] += jnp.sum(da * dtv, axis=0, keepdims=True) * a_neg
        lane16 = lax.broadcasted_iota(jnp.int32, (CHUNK, 128), 1) < SSM_HEADS
        draw = jnp.where(lane16, ddtv * _sigmoid(dt_ref[...] + dtb_ref[...]), 0.0)
        ddtb_ref[...] += jnp.sum(draw, axis=0, keepdims=True)
        ddt_ref[...] = draw.astype(BF16)
        dxa = dxdt * dt_e + dy * dsk_ref[...]
        for (dact, pre, x_ref, nx, cw_ref, dcw_ref, dcb_ref, dx_ref) in (
                (dxa, v["pre_x"], xs_ref, nx_x, cwx_ref, dcwx_ref, dcbx_ref, dxs_ref),
                (dact_b[...], v["pre_b"], bc_ref, nx_b, cwb_ref, dcwb_ref, dcbb_ref, dbc_ref)):
            sp = _sigmoid(pre)
            dpre = dact * (sp * (1.0 + pre * (1.0 - sp)))
            dcb_ref[...] += jnp.sum(dpre, axis=0, keepdims=True)
            xv = x_ref[...]
            nxt = nx[...]
            dx = cw_ref[3:4, :] * dpre
            dcw_ref[3:4, :] += jnp.sum(dpre * xv, axis=0, keepdims=True)
            for k in range(3):
                d_up = _shift_up(dpre, nxt, 3 - k)
                dcw_ref[k:k + 1, :] += jnp.sum(xv * d_up, axis=0, keepdims=True)
                dx = dx + cw_ref[k:k + 1, :] * d_up
            nx[...] = dpre[0:8, :]
            dx_ref[...] = dx.astype(dx_ref.dtype)

    full = lambda a: pl.BlockSpec(a.shape, lambda c: (0,) * a.ndim)
    rowblk = lambda w: pl.BlockSpec((CHUNK, w), lambda c: (ch(c), 0))
    acc = lambda a, b: pl.BlockSpec((a, b), lambda c: (0, 0))
    return pl.pallas_call(
        body, name="ssd_bwd", grid=(nc,),
        in_specs=[col(1024, 1), col(1024, 2), col(512, 6), col(128, 28), halo(1024, 2), halo(512, 6),
                  rowblk(SSM_INNER),
                  pl.BlockSpec((1, 128, SSM_INNER), lambda c: (ch(c), 0, 0)),
                  rowblk(SSM_INNER),
                  full(cwx), full(cbx), full(cwb), full(cbb), full(dtb), full(alog), full(dsk_e), full(norm_g),
                  full(tri), full(triu), full(expand), full(expand_t)],
        out_specs=[rowblk(SSM_INNER), rowblk(SSM_INNER), rowblk(512), rowblk(128),
                   acc(1, 1024), acc(1, 1024), acc(1, 128), acc(1, 128), acc(4, 1024), acc(1, 1024), acc(4, 512),
                   acc(1, 512)],
        out_shape=[SDS((t, SSM_INNER), BF16), SDS((t, SSM_INNER), BF16), SDS((t, 512), BF16), SDS((t, 128), BF16),
                   SDS((1, 1024), F32), SDS((1, 1024), F32), SDS((1, 128), F32), SDS((1, 128), F32),
                   SDS((4, 1024), F32), SDS((1, 1024), F32), SDS((4, 512), F32), SDS((1, 512), F32)],
        scratch_shapes=[pltpu.VMEM((128, SSM_INNER), F32), pltpu.VMEM((8, 1024), F32), pltpu.VMEM((8, 512), F32),
                        pltpu.VMEM((CHUNK, 512), F32), pltpu.VMEM((CHUNK, SSM_INNER), F32)],
        compiler_params=_params("arbitrary"))(proj, proj, proj, proj, proj, proj, y_ssd, hs, dout,
                                              cwx, cbx, cwb, cbb, dtb, alog, dsk_e, norm_g, tri, triu, expand,
                                              expand_t)


def _conv3(x, halo, w_ref, b_ref, part):
    acc = b_ref[part] + w_ref[2, part] * x
    for k in range(2):
        acc = acc + w_ref[k, part] * _shift_down(x, halo, 2 - k)
    return acc


def _up_act(x, g, w_up_t, cw, cb, tm=2048, tn=256):
    t, k = x.shape
    nj = D_FF // tn

    def body(x_ref, g_ref, wg_ref, wv_ref, w_ref, b_ref, u_ref, h_ref, f_ref, halo):
        i, j = pl.program_id(0), pl.program_id(1)

        @pl.when(j == 0)
        def _():
            xv = x_ref[...]
            r = lax.rsqrt(jnp.mean(xv * xv, axis=-1, keepdims=True) + EPS)
            h_ref[...] = (xv * r * g_ref[...]).astype(BF16)

        @pl.when(i == 0)
        def _():
            halo[j] = jnp.zeros((2, 8, tn), F32)

        us = [_dot(h_ref[...], wt_ref[...], NT) for wt_ref in (wg_ref, wv_ref)]
        parts = []
        for part, u in enumerate(us):
            u_ref[part] = u
            parts.append(_conv3(u, halo[j, part], w_ref, b_ref, part))
            halo[j, part] = u[tm - 8:, :]
        gate, val = parts
        f_ref[...] = (gate * _sigmoid(gate) * val).astype(BF16)

    return pl.pallas_call(
        body, name="up_proj", grid=(t // tm, nj),
        in_specs=[pl.BlockSpec((tm, k), lambda i, j: (i, 0)), pl.BlockSpec((1, k), lambda i, j: (0, 0)),
                  pl.BlockSpec((tn, k), lambda i, j: (j, 0)), pl.BlockSpec((tn, k), lambda i, j: (j + nj, 0)),
                  pl.BlockSpec((3, 2, 1, tn), lambda i, j: (0, 0, 0, j)), pl.BlockSpec((2, 1, tn), lambda i, j: (0, 0, j))],
        out_specs=[pl.BlockSpec((2, tm, tn), lambda i, j: (0, i, j)), pl.BlockSpec((tm, k), lambda i, j: (i, 0)),
                   pl.BlockSpec((tm, tn), lambda i, j: (i, j))],
        out_shape=[SDS((2, t, D_FF), F32), SDS((t, k), BF16), SDS((t, D_FF), BF16)],
        scratch_shapes=[pltpu.VMEM((nj, 2, 8, tn), F32)],
        compiler_params=_params("arbitrary", "arbitrary"))(x, g, w_up_t, w_up_t, cw, cb)


def _ffn_bwd(dx2, w_down, u, cw, cb, tm=512, tn=1408):
    t = u.shape[1]
    nj, ni = D_FF // tn, t // tm
    rev = lambda i: ni - 1 - i

    def body(dx_ref, wd_ref, u_ref, uh_ref, w_ref, b_ref, du_ref, dcw_ref, dcb_ref, nxt):
        i = pl.program_id(1)

        @pl.when(i == 0)
        def _():
            nxt[...] = jnp.zeros_like(nxt)
            dcw_ref[...] = jnp.zeros_like(dcw_ref)
            dcb_ref[...] = jnp.zeros_like(dcb_ref)

        df = _dot(dx_ref[...].astype(BF16), wd_ref[...], NT)
        keep = (i < ni - 1).astype(F32)
        ug, uv = u_ref[0], u_ref[1]
        hg, hv = uh_ref[0] * keep, uh_ref[1] * keep
        gate = _conv3(ug, hg, w_ref, b_ref, 0)
        val = _conv3(uv, hv, w_ref, b_ref, 1)
        sg = _sigmoid(gate)
        dgate = df * val * (sg * (1.0 + gate * (1.0 - sg)))
        dval = df * (gate * sg)
        for part, (d, uu) in enumerate(((dgate, ug), (dval, uv))):
            dcb_ref[part] += jnp.sum(d, axis=0, keepdims=True)
            ahead = nxt[part]
            acc = w_ref[2, part] * d
            dcw_ref[2, part] += jnp.sum(d * uu, axis=0, keepdims=True)
            for k in range(2):
                d_up = _shift_up(d, ahead, 2 - k)
                dcw_ref[k, part] += jnp.sum(uu * d_up, axis=0, keepdims=True)
                acc = acc + w_ref[k, part] * d_up
            nxt[part] = d[0:8, :]
            du_ref[part] = acc.astype(BF16)

    w_spec = pl.BlockSpec((3, 2, 1, tn), lambda j, i: (0, 0, 0, j))
    b_spec = pl.BlockSpec((2, 1, tn), lambda j, i: (0, 0, j))
    return pl.pallas_call(
        body, name="ffn_bwd", grid=(nj, ni),
        in_specs=[pl.BlockSpec((tm, D_MODEL), lambda j, i: (rev(i), 0)), pl.BlockSpec((tn, D_MODEL), lambda j, i: (j, 0)),
                  pl.BlockSpec((2, tm, tn), lambda j, i: (0, rev(i), j)),
                  pl.BlockSpec((2, 8, tn), lambda j, i: (0, jnp.maximum(rev(i) * (tm // 8) - 1, 0), j)),
                  w_spec, b_spec],
        out_specs=[pl.BlockSpec((2, tm, tn), lambda j, i: (0, rev(i), j)), w_spec, b_spec],
        out_shape=[SDS((2, t, D_FF), BF16), SDS((3, 2, 1, D_FF), F32), SDS((2, 1, D_FF), F32)],
        scratch_shapes=[pltpu.VMEM((2, 8, tn), F32)],
        compiler_params=_params("parallel", "arbitrary"))(dx2, w_down, u, u, cw, cb)


def _ple_loss(x2, g, w_gate, p, w_proj_t, target, tm=256):
    t = x2.shape[0]

    def body(x_ref, g_ref, wg_ref, p_ref, wp_ref, tg_ref, dx_ref, dpre_ref, dpp_ref, h_ref, loss_ref, dg_ref):
        i = pl.program_id(0)
        xv = x_ref[...]
        r = lax.rsqrt(jnp.mean(xv * xv, axis=-1, keepdims=True) + EPS)
        xh = xv * r
        h = (xh * g_ref[...]).astype(BF16)
        h_ref[...] = h
        gate = _sigmoid(_dot(h, wg_ref[...]))
        pp = _dot(p_ref[...].astype(BF16), wp_ref[...], NT)
        err = (xv + gate * pp) - tg_ref[...]

        @pl.when(i == 0)
        def _():
            loss_ref[...] = jnp.zeros_like(loss_ref)
            dg_ref[...] = jnp.zeros_like(dg_ref)

        loss_ref[...] += 0.5 * jnp.sum(jnp.mean(err * err, axis=-1, keepdims=True), axis=0, keepdims=True)
        dy = err * (1.0 / D_MODEL)
        dpre = (dy * pp * gate * (1.0 - gate)).astype(BF16)
        dpre_ref[...] = dpre
        dpp_ref[...] = (dy * gate).astype(BF16)
        dh = _dot(dpre, wg_ref[...], NT)
        dg_ref[...] += jnp.sum(dh * xh, axis=0, keepdims=True)
        gd = dh * g_ref[...]
        dx_ref[...] = dy + r * (gd - xh * jnp.mean(gd * xh, axis=-1, keepdims=True))

    row = lambda w: pl.BlockSpec((tm, w), lambda i: (i, 0))
    full = lambda a: pl.BlockSpec(a.shape, lambda i: (0, 0))
    return pl.pallas_call(
        body, name="ple_loss", grid=(t // tm,),
        in_specs=[row(D_MODEL), full(g), full(w_gate), row(PLE_DIM), full(w_proj_t), row(D_MODEL)],
        out_specs=[row(D_MODEL), row(D_MODEL), row(D_MODEL), row(D_MODEL),
                   pl.BlockSpec((1, 128), lambda i: (0, 0)), pl.BlockSpec((1, D_MODEL), lambda i: (0, 0))],
        out_shape=[SDS((t, D_MODEL), F32), SDS((t, D_MODEL), BF16), SDS((t, D_MODEL), BF16), SDS((t, D_MODEL), BF16),
                   SDS((1, 128), F32), SDS((1, D_MODEL), F32)],
        compiler_params=_params("arbitrary"))(x2, g, w_gate, p, w_proj_t, target)


def _all_gather(arrays, name):
    n_a = len(arrays)

    def body(*refs):
        src, dst = refs[:n_a], refs[n_a:2 * n_a]
        send_sems, recv_sems, local_sems = refs[2 * n_a:]
        x, y, c = lax.axis_index("x"), lax.axis_index("y"), lax.axis_index("c")
        slot = lambda px, py, pc: 4 * px + 2 * py + pc
        me, sibling = (x, y, c), (x, y, 1 - c)
        chips = [(1 - x, y), (x, 1 - y), (1 - x, 1 - y)]

        def copy(a, k, block, to, own=False):
            return pltpu.make_async_remote_copy(
                src_ref=src[a] if own else dst[a].at[slot(*block)], dst_ref=dst[a].at[slot(*block)],
                send_sem=send_sems.at[a, k], recv_sem=recv_sems.at[a, k], device_id=to,
                device_id_type=pl.DeviceIdType.MESH)

        local = [pltpu.make_async_copy(src[a], dst[a].at[slot(*me)], local_sems.at[a]) for a in range(n_a)]
        for cp in local:
            cp.start()
        sends = []
        for a in range(n_a):
            sends.append(copy(a, 0, me, sibling, own=True))
            sends += [copy(a, 1 + j, me, (*chip, c), own=True) for j, chip in enumerate(chips)]
        for cp in sends:
            cp.start()
        for j, chip in enumerate(chips):
            for a in range(n_a):
                copy(a, 1 + j, (*chip, c), me).wait_recv()
                passed = copy(a, 4 + j, (*chip, c), sibling)
                passed.start()
                sends.append(passed)
        for a in range(n_a):
            copy(a, 0, sibling, me).wait_recv()
            for j, chip in enumerate(chips):
                copy(a, 4 + j, (*chip, 1 - c), me).wait_recv()
        for cp in sends:
            cp.wait_send()
        for cp in local:
            cp.wait()

    hbm = pl.BlockSpec(memory_space=pl.ANY)
    return pl.pallas_call(
        body, name=name, in_specs=[hbm] * n_a, out_specs=[hbm] * n_a,
        out_shape=[SDS((N_DEV,) + a.shape, a.dtype) for a in arrays],
        scratch_shapes=[pltpu.SemaphoreType.DMA((n_a, N_DEV - 1)), pltpu.SemaphoreType.DMA((n_a, N_DEV - 1)),
                        pltpu.SemaphoreType.DMA((n_a,))],
        )(*arrays)


def _peer(k):
    x, y, c = lax.axis_index("x"), lax.axis_index("y"), lax.axis_index("c")
    px = 1 - x if k & 4 else x
    py = 1 - y if k & 2 else y
    pc = 1 - c if k & 1 else c
    return (px, py, pc), 4 * px + 2 * py + pc


_HBM = pl.BlockSpec(memory_space=pltpu.HBM)
_SEM = pl.BlockSpec(memory_space=pltpu.SEMAPHORE)


def _split_copies(src, land, send_sems, recv_sems, scatter, arrivals):
    _, me = _peer(0)
    out = []
    for k in range(1, N_DEV):
        coords, peer = _peer(k)
        for a in range(len(src)):
            sem = a * (N_DEV - 1) + k - 1
            if scatter[a]:
                s, d = src[a].at[peer], land[a].at[k]
            else:
                s, d = src[a], land[a].at[peer if arrivals else me]
            out.append(pltpu.make_async_remote_copy(
                src_ref=s, dst_ref=d, send_sem=send_sems.at[sem], recv_sem=recv_sems.at[sem], device_id=coords,
                device_id_type=pl.DeviceIdType.MESH))
    return out


def _exchange_start(srcs, lands, scatter, name):
    n = len(srcs)

    def body(*refs):
        src, land = refs[:n], refs[n:2 * n]
        send_sems, recv_sems = refs[2 * n], refs[2 * n + 1]
        token = refs[-1]
        for cp in _split_copies(src, land, send_sems, recv_sems, scatter, False):
            cp.start()
        token[...] = jnp.zeros_like(token)

    hbm_shape = lambda a: pltpu.HBM(a.shape, a.dtype)
    sem_shape = pltpu.SemaphoreType.DMA((n * (N_DEV - 1),))
    outs = pl.pallas_call(
        body, name=name,
        out_shape=(sem_shape, sem_shape, *[hbm_shape(a) for a in srcs], *[hbm_shape(a) for a in lands],
                   SDS((8, 128), F32)),
        in_specs=[_HBM] * (2 * n), out_specs=(_SEM, _SEM, *[_HBM] * (2 * n), pl.BlockSpec(memory_space=pltpu.VMEM)),
        input_output_aliases={a: 2 + a for a in range(2 * n)},
        compiler_params=pltpu.CompilerParams(has_side_effects=pltpu.SideEffectType.DATAFLOW_SIDE_EFFECTING),
    )(*[pltpu.with_memory_space_constraint(a, pltpu.HBM) for a in list(srcs) + list(lands)])
    return outs[0], outs[1], outs[2:2 + n], outs[2 + n:2 + 2 * n], outs[-1]


def _exchange_wait(send_sems, recv_sems, srcs, lands, scatter, after, name):
    n = len(srcs)

    def body(*refs):
        src, land = refs[:n], refs[n:2 * n]
        for cp in _split_copies(src, land, refs[2 * n], refs[2 * n + 1], scatter, False):
            cp.wait_send()
        for cp in _split_copies(src, land, refs[2 * n], refs[2 * n + 1], scatter, True):
            cp.wait_recv()

    hbm_shape = lambda a: pltpu.HBM(a.shape, a.dtype)
    outs = pl.pallas_call(
        body, name=name, out_shape=tuple(hbm_shape(a) for a in list(srcs) + list(lands)),
        in_specs=[_HBM] * (2 * n) + [_SEM, _SEM, pl.BlockSpec(memory_space=pl.ANY)], out_specs=(_HBM,) * (2 * n),
        input_output_aliases={a: a for a in range(2 * n)},
        compiler_params=pltpu.CompilerParams(has_side_effects=pltpu.SideEffectType.DATAFLOW_SIDE_EFFECTING),
    )(*srcs, *lands, send_sems, recv_sems, after)
    return outs[:n], outs[n:]


def _reduce8(a, tr, name):
    _, rows, cols = a.shape

    def body(a_ref, o_ref):
        acc = a_ref[0]
        for j in range(1, N_DEV):
            acc = acc + a_ref[j]
        o_ref[...] = acc

    return pl.pallas_call(
        body, name=name, grid=(rows // tr,),
        in_specs=[pl.BlockSpec((N_DEV, tr, cols), lambda i: (0, i, 0))],
        out_specs=pl.BlockSpec((tr, cols), lambda i: (i, 0)), out_shape=SDS((rows, cols), F32),
        compiler_params=_params("parallel"))(a)


def _reduce_landed(own, land, name, tc=256):
    rows, cols = own.shape

    def body(own_ref, land_ref, o_ref):
        acc = own_ref[...]
        for k in range(1, N_DEV):
            acc = acc + land_ref[k].astype(F32)
        o_ref[...] = acc

    return pl.pallas_call(
        body, name=name, grid=(cols // tc,),
        in_specs=[pl.BlockSpec((rows, tc), lambda j: (0, j)), pl.BlockSpec((N_DEV, rows, tc), lambda j: (0, 0, j))],
        out_specs=pl.BlockSpec((rows, tc), lambda j: (0, j)), out_shape=SDS((rows, cols), F32),
        compiler_params=_params("parallel"))(own, land)


def _adamw(w, g, m, v, name, tr=None):
    rows, cols = w.shape
    tr = rows if tr is None else tr

    def body(w_ref, g_ref, m_ref, v_ref, d_ref, mo_ref, vo_ref):
        d_ref[...], mo_ref[...], vo_ref[...] = _adam_update(w_ref[...], g_ref[...], m_ref[...], v_ref[...])

    blk = pl.BlockSpec((tr, cols), lambda i: (i, 0))
    return pl.pallas_call(
        body, name=name, grid=(rows // tr,), in_specs=[blk] * 4, out_specs=[blk] * 3,
        out_shape=[SDS((rows, cols), F32)] * 3, compiler_params=_params("parallel"))(w, g, m, v)


def _pad_rows(a, rows):
    return jnp.pad(a, ((0, rows - a.shape[0]),) + ((0, 0),) * (a.ndim - 1))


def _local_step(x, p, target, sm, wts, fetch_rest, send, tok):
    ones_q, ones_k, dup, dup_t = _head_consts()
    tri, triu, expand, expand_t = _ssd_consts()
    w_in_t = wts["in_t"]
    cwx, cwb = wts["ssm_cw"][:, :SSM_INNER], wts["ssm_cw"][:, SSM_INNER:]
    cbx, cbb = sm["ssm_conv_b"][:, :SSM_INNER], sm["ssm_conv_b"][:, SSM_INNER:]
    pad128 = lambda a: jnp.pad(a, ((0, 0), (0, 128 - a.shape[1])))
    dtb, alog = pad128(sm["dt_bias"]), pad128(sm["a_log"])
    dsk_e = jnp.repeat(sm["d_skip"], HEAD_DIM, axis=1)
    gq = jnp.tile(sm["q_norm_g"], (1, ATTN_DIM // HEAD_DIM))
    gk = jnp.tile(sm["k_norm_g"], (1, KV_DIM // HEAD_DIM))
    ffn_cw = wts["ffn_cw"].reshape(3, 2, 1, D_FF)
    ffn_cb = sm["ffn_conv_b"].reshape(2, 1, D_FF)

    proj, h1 = _norm_matmul(x, sm["attn_norm_g"] + tok, w_in_t, 1024, 768, "in_proj")
    qn, kd, vd = _attn_prep(proj, gq, gk, ones_q, ones_k, dup)
    attn_out, lse = _attn_fwd(qn, kd, vd)
    y_ssd, hs, ssm_out = _ssd_fwd(proj, cwx, cbx, cwb, cbb, dtb, alog, dsk_e, sm["ssm_norm_g"], tri, expand)
    rest = fetch_rest(ssm_out)
    w_out, w_up_t, w_down, w_gate, w_proj_t = (rest[k] for k in ("out", "up_t", "down", "gate", "proj_t"))
    x1 = _mm_resid([(attn_out, None, w_out[:ATTN_DIM]), (ssm_out, None, w_out[ATTN_DIM:])], x, None, 512, F32,
                   "out_proj")
    u, h2, f = _up_act(x1, sm["ffn_norm_g"], w_up_t, ffn_cw, ffn_cb)
    x2 =_mm_resid([(f, None, w_down)], x1, None, 512, F32, "down_proj")
    dx2, dpre, dpp, h3, loss, dg_ple = _ple_loss(x2, sm["ple_norm_g"], w_gate, p, w_proj_t, target)

    g_gate = _wgrad(h3, None, dpre, "wg_gate")
    g_proj_t = _wgrad(dpp, None, p, "wg_proj")
    g_down = _wgrad(f, None, dx2, "wg_down")
    du, d_ffn_cw, d_ffn_cb = _ffn_bwd(dx2, w_down, u, ffn_cw, ffn_cb)
    dx1, dg_ffn = _mm_normbwd([(du, 0, w_up_t, D_FF, 0), (du, 1, w_up_t, D_FF, 1)], x1, sm["ffn_norm_g"], dx2, 256,
                              "up_proj_bwd")
    g_up_t = jnp.concatenate([_wgrad(du, 0, h2, "wg_up_gate"), _wgrad(du, 1, h2, "wg_up_val")], axis=0)
    tok = send(dict(gate=g_gate, proj_t=g_proj_t, down=g_down, up_t=g_up_t)).astype(BF16)
    d_attn = _mm_resid([(dx1, None, w_out[:ATTN_DIM] + tok)], None, NT, 512, F32, "out_proj_bwd_attn")
    d_ssm = _mm_resid([(dx1, None, w_out[ATTN_DIM:] + tok)], None, NT, 512, F32, "out_proj_bwd_ssm")
    g_out = jnp.concatenate([_wgrad(attn_out, None, dx1, "wg_out_attn"), _wgrad(ssm_out, None, dx1, "wg_out_ssm")],
                            axis=0)
    tok = send(dict(out=g_out))
    (dz, dxs, dbc, ddt, dg_ssm, d_dsk_e, d_alog, d_dtb, d_cwx, d_cbx, d_cwb, d_cbb) = _ssd_bwd(
        proj, y_ssd, hs, d_ssm, cwx, cbx, cwb, cbb, dtb + tok, alog, dsk_e, sm["ssm_norm_g"], tri, triu, expand,
        expand_t)
    dqn, dkc, dkp, dvc, dvp = _attn_bwd(qn, kd, vd, attn_out, lse, d_attn, ones_k[:128, :128])
    dqkv, dgq, dgk = _attn_prep_bwd(proj, dqn, dkc, dkp, dvc, dvp, gq + tok, gk, ones_q, ones_k, dup_t)
    pieces = [(dqkv, 0, 1024), (dz, 1024, 2048), (dxs, 2048, 3072), (dbc, 3072, 3584), (ddt, 3584, 3712)]
    g_in_t = jnp.concatenate([_wgrad(a, None, h1, "wg_in_%d" % lo) for a, lo, _ in pieces], axis=0)[:IN_PROJ]
    tok = send(dict(in_t=g_in_t))
    grad_x, dg_attn = _mm_normbwd([(a, None, w_in_t, hi - lo, lo // (hi - lo)) for a, lo, hi in pieces], x,
                                  sm["attn_norm_g"] + tok, dx1, 256, "in_proj_bwd")

    small = dict(
        attn_norm_g=dg_attn, q_norm_g=dgq.reshape(-1, HEAD_DIM).sum(0, keepdims=True),
        k_norm_g=dgk.reshape(-1, HEAD_DIM).sum(0, keepdims=True),
        ssm_conv_w=jnp.concatenate([d_cwx, d_cwb], axis=1), ssm_conv_b=jnp.concatenate([d_cbx, d_cbb], axis=1),
        dt_bias=d_dtb[:, :SSM_HEADS], a_log=d_alog[:, :SSM_HEADS],
        d_skip=d_dsk_e.reshape(SSM_HEADS, HEAD_DIM).sum(1)[None, :], ssm_norm_g=dg_ssm, ffn_norm_g=dg_ffn,
        ffn_conv_w=d_ffn_cw.reshape(3, 2 * D_FF), ffn_conv_b=d_ffn_cb.reshape(1, 2 * D_FF), ple_norm_g=dg_ple)
    return loss[0, 0], grad_x, small


_SMALL = (("attn_norm_g", 1, 1024), ("q_norm_g", 1, 64), ("k_norm_g", 1, 64), ("ssm_conv_w", 4, XBC_DIM),
          ("ssm_conv_b", 1, XBC_DIM), ("dt_bias", 1, 16), ("a_log", 1, 16), ("d_skip", 1, 16), ("ssm_norm_g", 1, 1024),
          ("ffn_norm_g", 1, 1024), ("ffn_conv_w", 3, 2 * D_FF), ("ffn_conv_b", 1, 2 * D_FF), ("ple_norm_g", 1, 1024))
_SMALL_ROWS, _SMALL_COLS = 32, XBC_DIM
_SHARDED_SMALL = ("ssm_conv_w", "ffn_conv_w")


def _small_chunks(n):
    return 1 if n <= _SMALL_COLS else 4


def _pack_small(parts):
    rows = []
    for k, r, n in _SMALL:
        c = _small_chunks(n)
        rows.append(jnp.pad(parts[k].reshape(r * c, n // c), ((0, 0), (0, _SMALL_COLS - n // c))))
    return _pad_rows(jnp.concatenate(rows, axis=0), _SMALL_ROWS)


def _adam_update(w, g, m, v):
    c1 = 1.0 - ADAM_B1 ** ADAM_STEP
    c2 = 1.0 - ADAM_B2 ** ADAM_STEP
    mn = ADAM_B1 * m + (1.0 - ADAM_B1) * g
    vn = ADAM_B2 * v + (1.0 - ADAM_B2) * (g * g)
    return -ADAM_LR * ((mn / c1) / (jnp.sqrt(vn / c2) + ADAM_EPS) + ADAM_WD * w), mn, vn


def _adamw_small(g_all, g_shard, w, m, v):
    ins, shapes = [g_all], []
    for k, _, _ in _SMALL:
        shape2 = w[k].shape if w[k].ndim == 2 else (1, w[k].shape[0])
        shapes.append(shape2)
        ins += ([g_shard[k]] if k in _SHARDED_SMALL else []) + [a.reshape(shape2) for a in (w[k], m[k], v[k])]

    def body(*refs):
        g_ref, pos, row = refs[0], 1, 0
        outs = refs[len(ins):]
        for i, (k, r, n) in enumerate(_SMALL):
            c = _small_chunks(n)
            if k in _SHARDED_SMALL:
                g = refs[pos][...]
                pos += 1
            elif c == 1:
                g = g_ref[row:row + r, 0:n]
            else:
                g = jnp.concatenate([g_ref[row + j:row + j + 1, 0:n // c] for j in range(c)], axis=1)
            row += r * c
            d, mn, vn = _adam_update(refs[pos][...], g, refs[pos + 1][...], refs[pos + 2][...])
            pos += 3
            for o_ref, val in zip(outs[4 * i:4 * i + 4], (g, d, mn, vn)):
                o_ref[...] = val

    res = pl.pallas_call(body, name="adamw_small",
                         out_shape=[SDS(s, F32) for s in shapes for _ in range(4)])(*ins)
    return {k: tuple(a.reshape(w[k].shape) for a in res[4 * i:4 * i + 4]) for i, (k, _, _) in enumerate(_SMALL)}


def kernel(x, p, attn_norm_g, w_in, q_norm_g, k_norm_g, ssm_conv_w, ssm_conv_b, dt_bias, a_log, d_skip, ssm_norm_g, w_out, ffn_norm_g, w_up, ffn_conv_w, ffn_conv_b, w_down, ple_norm_g, w_ple_gate, w_ple_proj, loss_target, m_attn_norm_g, m_w_in, m_q_norm_g, m_k_norm_g, m_ssm_conv_w, m_ssm_conv_b, m_dt_bias, m_a_log, m_d_skip, m_ssm_norm_g, m_w_out, m_ffn_norm_g, m_w_up, m_ffn_conv_w, m_ffn_conv_b, m_w_down, m_ple_norm_g, m_w_ple_gate, m_w_ple_proj, v_attn_norm_g, v_w_in, v_q_norm_g, v_k_norm_g, v_ssm_conv_w, v_ssm_conv_b, v_dt_bias, v_a_log, v_d_skip, v_ssm_norm_g, v_w_out, v_ffn_norm_g, v_w_up, v_ffn_conv_w, v_ffn_conv_b, v_w_down, v_ple_norm_g, v_w_ple_gate, v_w_ple_proj):
    names = ("attn_norm_g", "w_in", "q_norm_g", "k_norm_g", "ssm_conv_w", "ssm_conv_b", "dt_bias", "a_log", "d_skip",
             "ssm_norm_g", "w_out", "ffn_norm_g", "w_up", "ffn_conv_w", "ffn_conv_b", "w_down", "ple_norm_g",
             "w_ple_gate", "w_ple_proj")
    w = dict(zip(names, (attn_norm_g, w_in, q_norm_g, k_norm_g, ssm_conv_w, ssm_conv_b, dt_bias, a_log, d_skip,
                         ssm_norm_g, w_out, ffn_norm_g, w_up, ffn_conv_w, ffn_conv_b, w_down, ple_norm_g, w_ple_gate,
                         w_ple_proj)))
    m = dict(zip(names, (m_attn_norm_g, m_w_in, m_q_norm_g, m_k_norm_g, m_ssm_conv_w, m_ssm_conv_b, m_dt_bias,
                         m_a_log, m_d_skip, m_ssm_norm_g, m_w_out, m_ffn_norm_g, m_w_up, m_ffn_conv_w, m_ffn_conv_b,
                         m_w_down, m_ple_norm_g, m_w_ple_gate, m_w_ple_proj)))
    v = dict(zip(names, (v_attn_norm_g, v_w_in, v_q_norm_g, v_k_norm_g, v_ssm_conv_w, v_ssm_conv_b, v_dt_bias,
                         v_a_log, v_d_skip, v_ssm_norm_g, v_w_out, v_ffn_norm_g, v_w_up, v_ffn_conv_w, v_ffn_conv_b,
                         v_w_down, v_ple_norm_g, v_w_ple_gate, v_w_ple_proj)))
    w, m, v = ({k: a[0] for k, a in d.items()} for d in (w, m, v))
    me = 4 * lax.axis_index("x") + 2 * lax.axis_index("y") + lax.axis_index("c")

    mine = dict(in_t=w["w_in"].T, out=w["w_out"], up_t=w["w_up"].T, down=w["w_down"], gate=w["w_ple_gate"],
                proj_t=w["w_ple_proj"].T)
    mine = {k: a.astype(BF16) for k, a in mine.items()}
    conv_pack = jnp.pad(jnp.concatenate([w["ssm_conv_w"].reshape(-1), w["ffn_conv_w"].reshape(-1)]),
                        (0, 3072 - 2880)).reshape(8, 384)
    all_in, all_conv = _all_gather([mine["in_t"], conv_pack], "gather_first")
    later = ("out", "up_t", "down", "gate", "proj_t")
    zones = [lax.dynamic_update_slice(lax.empty((N_DEV,) + mine[k].shape, BF16), mine[k][None], (me, 0, 0))
             for k in later]
    zones, all_in, all_conv = lax.optimization_barrier((zones, all_in, all_conv))
    rest_state = _exchange_start([mine[k] for k in later], zones, [False] * len(later), "gather_rest_start")

    def fetch_rest(after):
        _, landed = _exchange_wait(*rest_state[:4], [False] * len(later), after, "gather_rest_wait")
        return {k: a.reshape(N_DEV * a.shape[1], a.shape[2]) for k, a in zip(later, landed)}

    wts = dict(in_t=_pad_rows(all_in.reshape(IN_PROJ, D_MODEL), IN_PROJ_PAD))
    conv_flat = all_conv.reshape(N_DEV, 3072)
    wts["ssm_cw"] = conv_flat[:, :768].reshape(N_DEV, 4, XBC_DIM // N_DEV).transpose(1, 0, 2).reshape(4, XBC_DIM)
    wts["ffn_cw"] = conv_flat[:, 768:2880].reshape(N_DEV, 3, 2 * D_FF // N_DEV).transpose(1, 0, 2).reshape(3, 2 * D_FF)
    sm = {k: w[k].reshape(1, -1) for k, _, _ in _SMALL if k not in _SHARDED_SMALL}

    in_flight = []

    def send(grads):
        keys = sorted(grads)
        blocks = [grads[k].reshape(N_DEV, grads[k].shape[0] // N_DEV, grads[k].shape[1]) for k in keys]
        own = [lax.dynamic_index_in_dim(a, me, 0, keepdims=False) for a in blocks]
        srcs = [a.astype(BF16) for a in blocks]
        state = _exchange_start(srcs, [lax.empty(a.shape, BF16) for a in srcs], [True] * len(keys),
                                "send_" + "_".join(keys))
        in_flight.append((keys, state, own))
        return state[4][0:1, 0:1]

    loss, grad_x, small = _local_step(x[0], p[0, 0], loss_target[0], sm, wts, fetch_rest, send,
                                      rest_state[4][0:1, 0:1])
    loss = lax.psum(loss, ("x", "y", "c"))

    (got_small,) = _all_gather([_pack_small(small)], "gather_small_grads")
    g_small = _reduce8(got_small, _SMALL_ROWS, "reduce_small")
    grads = {}
    for keys, state, own in in_flight:
        _, landed = _exchange_wait(*state[:4], [True] * len(keys), grad_x, "wait_" + "_".join(keys))
        for k, mine_k, land in zip(keys, own, landed):
            grads[k] = _reduce_landed(mine_k, land, "reduce_" + k)
    gw = {"w_in": grads["in_t"].T, "w_out": grads["out"], "w_up": grads["up_t"].T, "w_down": grads["down"],
          "w_ple_gate": grads["gate"], "w_ple_proj": grads["proj_t"].T}
    n_ssm, n_ffn = XBC_DIM // N_DEV, 2 * D_FF // N_DEV
    g_shard = {"ssm_conv_w": lax.dynamic_slice(g_small, (3, me * n_ssm), (4, n_ssm)),
               "ffn_conv_w": lax.dynamic_slice(g_small[13:25, :2 * D_FF // 4].reshape(3, 2 * D_FF), (0, me * n_ffn),
                                               (3, n_ffn))}

    delta, new_m, new_v = {}, {}, {}
    for k, tr in (("w_in", 256), ("w_out", None), ("w_up", 256), ("w_down", None), ("w_ple_gate", None),
                  ("w_ple_proj", None)):
        delta[k], new_m[k], new_v[k] = _adamw(w[k], gw[k], m[k], v[k], "adamw_" + k, tr)
    for k, (g_k, d_k, m_k, v_k) in _adamw_small(g_small, g_shard, w, m, v).items():
        gw[k], delta[k], new_m[k], new_v[k] = g_k, d_k, m_k, v_k

    outs = [loss, grad_x[None]]
    for d in (gw, delta, new_m, new_v):
        outs += [d[k][None] for k in names]
    return tuple(outs)
```

```python
import functools

import numpy as np
import jax
import jax.numpy as jnp
from jax import lax
from jax.experimental import pallas as pl
from jax.experimental.pallas import tpu as pltpu

F32 = jnp.float32
BF16 = jnp.bfloat16
SDS = jax.ShapeDtypeStruct
EPS = 1e-6
N_DEV = 8
D_MODEL = 1024
HEAD_DIM = 64
ATTN_DIM = 512
KV_DIM = 256
SSM_INNER = 1024
SSM_HEADS = 16
BC_DIM = 256
XBC_DIM = SSM_INNER + 2 * BC_DIM
MIX_DIM = ATTN_DIM + SSM_INNER
IN_PROJ = 3600
IN_PROJ_PAD = 3840
D_FF = 2816
PLE_DIM = 256
CHUNK = 128
SUPER = 2048
DILATIONS = (1, 4, 16)
TILE_UNROLL = 8
VMEM_LIMIT = 56 * 1024 * 1024
ADAM_LR, ADAM_B1, ADAM_B2, ADAM_EPS, ADAM_WD, ADAM_STEP = 0.001, 0.9, 0.999, 1e-08, 0.01, 10

NT = (((1,), (1,)), ((), ()))
TN = (((0,), (0,)), ((), ()))


def _params(*sem):
    return pltpu.CompilerParams(dimension_semantics=sem if sem else None, vmem_limit_bytes=VMEM_LIMIT)


def _dot(a, b, dims=None):
    if dims is None:
        return jnp.dot(a, b, preferred_element_type=F32)
    return lax.dot_general(a, b, dims, preferred_element_type=F32)


def _hdot(a, b, parts=2):
    a_exact = a.dtype == BF16
    x = b if a_exact else a
    acc = None
    for _ in range(parts):
        piece = x.astype(BF16)
        x = x - piece.astype(F32)
        d = _dot(a, piece) if a_exact else _dot(piece, b)
        acc = d if acc is None else acc + d
    return acc


def _sigmoid(x):
    return 0.5 * jnp.tanh(0.5 * x) + 0.5


def _shift_down(x, halo8, s):
    xr = pltpu.roll(x, s, 0)
    row = lax.broadcasted_iota(jnp.int32, halo8.shape, 0)
    first = jnp.where(row < s, pltpu.roll(halo8, s, 0), xr[0:8])
    return jnp.concatenate([first, xr[8:]], axis=0)


def _shift_up(x, halo8, s):
    n = x.shape[0]
    xr = pltpu.roll(x, n - s, 0)
    row = lax.broadcasted_iota(jnp.int32, halo8.shape, 0)
    last = jnp.where(row >= 8 - s, pltpu.roll(halo8, 8 - s, 0), xr[n - 8:])
    return jnp.concatenate([xr[:n - 8], last], axis=0)


def _norm_matmul(x, g, wt, tm, tn, name):
    t, k = x.shape
    n = wt.shape[0]

    def body(x_ref, g_ref, w_ref, o_ref, h_ref):
        @pl.when(pl.program_id(1) == 0)
        def _():
            xv = x_ref[...]
            r = lax.rsqrt(jnp.mean(xv * xv, axis=-1, keepdims=True) + EPS)
            h_ref[...] = (xv * r * g_ref[...]).astype(BF16)
        o_ref[...] = _dot(h_ref[...], w_ref[...], NT)

    return pl.pallas_call(
        body, name=name, grid=(t // tm, n // tn),
        in_specs=[pl.BlockSpec((tm, k), lambda i, j: (i, 0)), pl.BlockSpec((1, k), lambda i, j: (0, 0)),
                  pl.BlockSpec((tn, k), lambda i, j: (j, 0))],
        out_specs=[pl.BlockSpec((tm, tn), lambda i, j: (i, j)), pl.BlockSpec((tm, k), lambda i, j: (i, 0))],
        out_shape=[SDS((t, n), F32), SDS((t, k), BF16)],
        compiler_params=_params("parallel", "arbitrary"))(x, g, wt)


def _a_spec(a, lead, tm):
    if lead is None:
        return pl.BlockSpec((tm, a.shape[-1]), lambda i: (i, 0))
    return pl.BlockSpec((None, tm, a.shape[-1]), lambda i, _l=lead: (_l, i, 0))


def _mm_resid(pairs, res, dims, tm, out_dtype, name):
    t = pairs[0][0].shape[-2]
    n = pairs[0][2].shape[1] if dims is None else pairs[0][2].shape[0]
    np_ = len(pairs)

    def body(*refs):
        o_ref = refs[-1]
        acc = refs[2 * np_][...] if res is not None else None
        for q in range(np_):
            d = _dot(refs[q][...].astype(BF16), refs[np_ + q][...], dims)
            acc = d if acc is None else acc + d
        o_ref[...] = acc.astype(out_dtype)

    in_specs = [_a_spec(a, lead, tm) for a, lead, _ in pairs]
    in_specs += [pl.BlockSpec(b.shape, lambda i: (0, 0)) for _, _, b in pairs]
    args = [a for a, _, _ in pairs] + [b for _, _, b in pairs]
    if res is not None:
        in_specs.append(pl.BlockSpec((tm, n), lambda i: (i, 0)))
        args.append(res)
    return pl.pallas_call(
        body, name=name, grid=(t // tm,), in_specs=in_specs,
        out_specs=pl.BlockSpec((tm, n), lambda i: (i, 0)), out_shape=SDS((t, n), out_dtype),
        compiler_params=_params("parallel"))(*args)


def _mm_normbwd(pairs, x, g, dres, tm, name):
    t, k = x.shape
    np_ = len(pairs)
    b_specs = [pl.BlockSpec((rows, b.shape[1]), lambda i, _b=blk: (_b, 0)) for _, _, b, rows, blk in pairs]
    pairs = [(a, lead, b) for a, lead, b, _, _ in pairs]

    def body(*refs):
        x_ref, g_ref, dres_ref, dx_ref, dg_ref = refs[2 * np_:]
        dh = None
        for q in range(np_):
            d = _dot(refs[q][...], refs[np_ + q][...])
            dh = d if dh is None else dh + d
        xv = x_ref[...]
        r = lax.rsqrt(jnp.mean(xv * xv, axis=-1, keepdims=True) + EPS)
        xh = xv * r

        @pl.when(pl.program_id(0) == 0)
        def _():
            dg_ref[...] = jnp.zeros_like(dg_ref)
        dg_ref[...] += jnp.sum(dh * xh, axis=0, keepdims=True)
        gd = dh * g_ref[...]
        dx_ref[...] = dres_ref[...] + r * (gd - xh * jnp.mean(gd * xh, axis=-1, keepdims=True))

    in_specs = [_a_spec(a, lead, tm) for a, lead, _ in pairs] + b_specs
    in_specs += [pl.BlockSpec((tm, k), lambda i: (i, 0)), pl.BlockSpec((1, k), lambda i: (0, 0)),
                 pl.BlockSpec((tm, k), lambda i: (i, 0))]
    args = [a for a, _, _ in pairs] + [b for _, _, b in pairs] + [x, g, dres]
    return pl.pallas_call(
        body, name=name, grid=(t // tm,), in_specs=in_specs,
        out_specs=[pl.BlockSpec((tm, k), lambda i: (i, 0)), pl.BlockSpec((1, k), lambda i: (0, 0))],
        out_shape=[SDS((t, k), F32), SDS((1, k), F32)],
        compiler_params=_params("arbitrary"))(*args)


def _wgrad(a, a_lead, b, name, tk=2048):
    t, m = a.shape[-2:]
    n = b.shape[1]
    tm = m if m <= 1024 else 1408
    assert m % tm == 0

    def body(a_ref, b_ref, o_ref):
        @pl.when(pl.program_id(1) == 0)
        def _():
            o_ref[...] = jnp.zeros_like(o_ref)
        o_ref[...] += _dot(a_ref[...].astype(BF16), b_ref[...].astype(BF16), TN)

    if a_lead == "all":
        per = m // tm
        return pl.pallas_call(
            body, name=name, grid=(a.shape[0] * per, t // tk),
            in_specs=[pl.BlockSpec((None, tk, tm), lambda mi, ki: (mi // per, ki, mi % per)),
                      pl.BlockSpec((tk, n), lambda mi, ki: (ki, 0))],
            out_specs=pl.BlockSpec((tm, n), lambda mi, ki: (mi, 0)), out_shape=SDS((a.shape[0] * m, n), F32),
            compiler_params=_params("parallel", "arbitrary"))(a, b)
    if a_lead is None:
        a_spec = pl.BlockSpec((tk, tm), lambda mi, ki: (ki, mi))
    else:
        a_spec = pl.BlockSpec((None, tk, tm), lambda mi, ki, _l=a_lead: (_l, ki, mi))
    return pl.pallas_call(
        body, name=name, grid=(m // tm, t // tk),
        in_specs=[a_spec, pl.BlockSpec((tk, n), lambda mi, ki: (ki, 0))],
        out_specs=pl.BlockSpec((tm, n), lambda mi, ki: (mi, 0)), out_shape=SDS((m, n), F32),
        compiler_params=_params("parallel", "arbitrary"))(a, b)


def _head_consts():
    iq = np.arange(ATTN_DIM)
    ik = np.arange(KV_DIM)
    ones_q = (iq[:, None] // HEAD_DIM == iq[None, :] // HEAD_DIM).astype(np.float32)
    ones_k = (ik[:, None] // HEAD_DIM == ik[None, :] // HEAD_DIM).astype(np.float32)
    dup = (ik[:, None] == (HEAD_DIM * (iq[None, :] // 128) + iq[None, :] % HEAD_DIM)).astype(np.float32)
    return jnp.asarray(ones_q, BF16), jnp.asarray(ones_k, BF16), jnp.asarray(dup, BF16), jnp.asarray(dup.T, BF16)


def _attn_prep(proj, gq, gk, ones_q, ones_k, dup, tm=512):
    t = proj.shape[0]

    def body(p_ref, gq_ref, gk_ref, oq_ref, ok_ref, dup_ref, qn_ref, kd_ref, vd_ref):
        q = p_ref[:, 0:ATTN_DIM]
        k = p_ref[:, ATTN_DIM:ATTN_DIM + KV_DIM]
        v = p_ref[:, ATTN_DIM + KV_DIM:]
        rq = lax.rsqrt(_hdot(q * q, oq_ref[...]) * (1.0 / HEAD_DIM) + EPS)
        qn_ref[...] = (q * rq * gq_ref[...]) * (HEAD_DIM ** -0.5)
        rk = lax.rsqrt(_hdot(k * k, ok_ref[...]) * (1.0 / HEAD_DIM) + EPS)
        kn = k * rk * gk_ref[...]
        kd_ref[...] = _dot(kn.astype(BF16), dup_ref[...])
        vd_ref[...] = _dot(v.astype(BF16), dup_ref[...])

    full = lambda a: pl.BlockSpec(a.shape, lambda i: (0, 0))
    o_spec = pl.BlockSpec((tm, ATTN_DIM), lambda i: (i, 0))
    return pl.pallas_call(
        body, name="attn_prep", grid=(t // tm,),
        in_specs=[pl.BlockSpec((tm, 1024), lambda i: (i, 0)), full(gq), full(gk), full(ones_q), full(ones_k), full(dup)],
        out_specs=[o_spec, o_spec, o_spec], out_shape=[SDS((t, ATTN_DIM), F32)] * 3,
        compiler_params=_params("parallel"))(proj, gq, gk, ones_q, ones_k, dup)


def _attn_prep_bwd(proj, dqn, dkc, dkp, dvc, dvp, gq, gk, ones_q, ones_k, dup_t, tm=512):
    t = proj.shape[0]
    nblk = t // tm
    off = SUPER // tm

    def body(p_ref, dqn_ref, dkc_ref, dkp_ref, dvc_ref, dvp_ref, gq_ref, gk_ref, oq_ref, ok_ref, dt_ref,
             o_ref, dgq_ref, dgk_ref):
        i = pl.program_id(0)
        has_next = (i + off < nblk).astype(F32)
        q = p_ref[:, 0:ATTN_DIM]
        k = p_ref[:, ATTN_DIM:ATTN_DIM + KV_DIM]
        dkn = _hdot(dkc_ref[...] + has_next * dkp_ref[...], dt_ref[...])
        dv = _hdot(dvc_ref[...] + has_next * dvp_ref[...], dt_ref[...])

        @pl.when(i == 0)
        def _():
            dgq_ref[...] = jnp.zeros_like(dgq_ref)
            dgk_ref[...] = jnp.zeros_like(dgk_ref)

        rq = lax.rsqrt(_hdot(q * q, oq_ref[...]) * (1.0 / HEAD_DIM) + EPS)
        xh = q * rq
        dy = dqn_ref[...] * (HEAD_DIM ** -0.5)
        dgq_ref[...] += jnp.sum(dy * xh, axis=0, keepdims=True)
        gd = dy * gq_ref[...]
        dq = rq * (gd - xh * (_hdot(gd * xh, oq_ref[...]) * (1.0 / HEAD_DIM)))
        rk = lax.rsqrt(_hdot(k * k, ok_ref[...]) * (1.0 / HEAD_DIM) + EPS)
        kh = k * rk
        dgk_ref[...] += jnp.sum(dkn * kh, axis=0, keepdims=True)
        gdk = dkn * gk_ref[...]
        dk = rk * (gdk - kh * (_hdot(gdk * kh, ok_ref[...]) * (1.0 / HEAD_DIM)))
        o_ref[:, 0:ATTN_DIM] = dq.astype(BF16)
        o_ref[:, ATTN_DIM:ATTN_DIM + KV_DIM] = dk.astype(BF16)
        o_ref[:, ATTN_DIM + KV_DIM:] = dv.astype(BF16)

    full = lambda a: pl.BlockSpec(a.shape, lambda i: (0, 0))
    cur = pl.BlockSpec((tm, ATTN_DIM), lambda i: (i, 0))
    nxt = pl.BlockSpec((tm, ATTN_DIM), lambda i: (jnp.minimum(i + off, nblk - 1), 0))
    return pl.pallas_call(
        body, name="attn_prep_bwd", grid=(nblk,),
        in_specs=[pl.BlockSpec((tm, 1024), lambda i: (i, 0)), cur, cur, nxt, cur, nxt,
                  full(gq), full(gk), full(ones_q), full(ones_k), full(dup_t)],
        out_specs=[pl.BlockSpec((tm, 1024), lambda i: (i, 0)), pl.BlockSpec((1, ATTN_DIM), lambda i: (0, 0)),
                   pl.BlockSpec((1, KV_DIM), lambda i: (0, 0))],
        out_shape=[SDS((t, 1024), BF16), SDS((1, ATTN_DIM), F32), SDS((1, KV_DIM), F32)],
        compiler_params=_params("arbitrary"))(proj, dqn, dkc, dkp, dvc, dvp, gq, gk, ones_q, ones_k, dup_t)


def _tile_masks():
    qi = lax.broadcasted_iota(jnp.int32, (2 * CHUNK, 2 * CHUNK), 0) & (CHUNK - 1)
    kj = lax.broadcasted_iota(jnp.int32, (2 * CHUNK, 2 * CHUNK), 1)
    delta = CHUNK + qi - kj
    band = (delta >= 0) & (delta <= CHUNK)
    return band, kj


def _deinterleave(dst, src, n_rows, d):
    per = n_rows // d
    for r in range(d):
        dst[r * per:(r + 1) * per, :] = src[pl.ds(r, per, stride=d), :]


def _attn_specs(t):
    blk = lambda f: pl.BlockSpec((SUPER, 128), f)
    cur = blk(lambda h, s: (s, h))
    prev = blk(lambda h, s: (jnp.maximum(s - 1, 0), h))
    return cur, prev


def _attn_fwd(qn, kd, vd):
    t = qn.shape[0]
    cur, prev = _attn_specs(t)

    def body(q_ref, kp_ref, kc_ref, vp_ref, vc_ref, o_ref, lse_ref, kk, vv, qd, kdd, vdd, po, pm, pll, acc, mm, ll):
        s = pl.program_id(1)
        kk[0:SUPER, :] = kp_ref[...]
        kk[SUPER:, :] = kc_ref[...]
        vv[0:SUPER, :] = vp_ref[...]
        vv[SUPER:, :] = vc_ref[...]
        m0 = lax.broadcasted_iota(jnp.int32, (CHUNK, 128), 1) < HEAD_DIM
        band, kj = _tile_masks()
        for d in DILATIONS:
            lq = SUPER // d
            if d == 1:
                qs_ref, ks_ref, vs_ref = q_ref, kk, vv
            else:
                _deinterleave(qd, q_ref, SUPER, d)
                _deinterleave(kdd, kk, 2 * SUPER, d)
                _deinterleave(vdd, vv, 2 * SUPER, d)
                qs_ref, ks_ref, vs_ref = qd, kdd, vdd

            nblk = lq // CHUNK

            def key_rows(ti):
                return pl.ds((ti // nblk) * 2 * lq + lq + (ti % nblk - 1) * CHUNK, 2 * CHUNK)

            def scores(ti):
                qt = qs_ref[pl.ds(ti * CHUNK, CHUNK), :]
                qs = jnp.concatenate([jnp.where(m0, qt, 0.0), jnp.where(m0, 0.0, qt)], axis=0).astype(BF16)
                return _dot(qs, ks_ref[key_rows(ti), :].astype(BF16), NT)

            def softmax_pv(ti, sc):
                ok = band if ti % nblk > 0 else band & (kj >= jnp.where(s > 0, 0, CHUNK))
                sc = jnp.where(ok, sc, -jnp.inf)
                mt = jnp.max(sc, axis=-1, keepdims=True)
                p = jnp.exp(sc - mt)
                lt = jnp.sum(p, axis=-1, keepdims=True)
                ot = _dot(p.astype(BF16), vs_ref[key_rows(ti), :].astype(BF16))
                qrows = pl.ds(ti * CHUNK, CHUNK)
                po[qrows, :] = jnp.where(m0, ot[:CHUNK], ot[CHUNK:])
                pm[qrows, :] = jnp.where(m0, mt[:CHUNK], mt[CHUNK:])
                pll[qrows, :] = jnp.where(m0, lt[:CHUNK], lt[CHUNK:])

            for ti in range(SUPER // CHUNK):
                softmax_pv(ti, scores(ti))
            if d == 1:
                acc[...] = po[...]
                mm[...] = pm[...]
                ll[...] = pll[...]
            else:
                for r in range(d):
                    rows = pl.ds(r, lq, stride=d)
                    seg = slice(r * lq, (r + 1) * lq)
                    m_old, m_new = mm[rows, :], pm[seg, :]
                    m_all = jnp.maximum(m_old, m_new)
                    a, b = jnp.exp(m_old - m_all), jnp.exp(m_new - m_all)
                    acc[rows, :] = acc[rows, :] * a + po[seg, :] * b
                    ll[rows, :] = ll[rows, :] * a + pll[seg, :] * b
                    mm[rows, :] = m_all
        o_ref[...] = acc[...] / ll[...]
        lse_ref[...] = mm[...] + jnp.log(ll[...])

    big = pltpu.VMEM((2 * SUPER, 128), F32)
    one = pltpu.VMEM((SUPER, 128), F32)
    return pl.pallas_call(
        body, name="attn_fwd", grid=(4, t // SUPER),
        in_specs=[cur, prev, cur, prev, cur], out_specs=[cur, cur],
        out_shape=[SDS((t, ATTN_DIM), F32)] * 2,
        scratch_shapes=[big, big, one, big, big, one, one, one, one, one, one],
        compiler_params=_params("parallel", "arbitrary"))(qn, kd, kd, vd, vd)


def _attn_bwd(qn, kd, vd, out, lse, dout, ones_pair):
    t = qn.shape[0]
    cur, prev = _attn_specs(t)

    def body(q_ref, kp_ref, kc_ref, vp_ref, vc_ref, o_ref, lse_ref, do_ref, ones_ref,
             dq_ref, dkc_ref, dkp_ref, dvc_ref, dvp_ref,
             kk, vv, od, ld, kb, vb, qsb, dosb, tk, tv, pdq, delta):
        s = pl.program_id(1)
        delta[...] = _hdot(do_ref[...] * o_ref[...], ones_ref[...])

        def per_row(a):
            ar = pltpu.roll(a, HEAD_DIM, 1)
            rows = jnp.concatenate([jnp.where(m0, a, ar), jnp.where(m0, ar, a)], axis=0)
            return jnp.concatenate([rows, rows], axis=1)

        kk[0:SUPER, :] = kp_ref[...]
        kk[SUPER:, :] = kc_ref[...]
        vv[0:SUPER, :] = vp_ref[...]
        vv[SUPER:, :] = vc_ref[...]
        for ref in (dq_ref, dkc_ref, dkp_ref, dvc_ref, dvp_ref):
            ref[...] = jnp.zeros_like(ref)
        m0 = lax.broadcasted_iota(jnp.int32, (CHUNK, 128), 1) < HEAD_DIM
        band, kj = _tile_masks()
        ninf = -jnp.inf
        for d in DILATIONS:
            lq = SUPER // d
            nblk = lq // CHUNK
            for r in range(d):
                seg = slice(r * 2 * lq, (r + 1) * 2 * lq)
                kb[seg, :] = kk[pl.ds(r, 2 * lq, stride=d), :].astype(BF16)
                vb[seg, :] = vv[pl.ds(r, 2 * lq, stride=d), :].astype(BF16)
            for ti in range(SUPER // CHUNK):
                rows = pl.ds(ti // nblk + d * CHUNK * (ti % nblk), CHUNK, stride=d)
                for src, dst in ((q_ref, qsb), (do_ref, dosb)):
                    a = src[rows, :]
                    dst[ti * 2 * CHUNK:(ti + 1) * 2 * CHUNK, :] = jnp.concatenate(
                        [jnp.where(m0, a, 0.0), jnp.where(m0, 0.0, a)], axis=0).astype(BF16)
                ld[ti * CHUNK:(ti + 1) * CHUNK, :] = lse_ref[rows, :]
                od[ti * CHUNK:(ti + 1) * CHUNK, :] = delta[rows, :]

            def operands(ti):
                r, nb = ti // nblk, ti % nblk
                stacked = slice(ti * 2 * CHUNK, (ti + 1) * 2 * CHUNK)
                krows = pl.ds(r * 2 * lq + lq + (nb - 1) * CHUNK, 2 * CHUNK)
                return stacked, krows

            def scores(ti):
                stacked, krows = operands(ti)
                kt = kb[krows, :]
                return dict(ti=ti, sc=_dot(qsb[stacked, :], kt, NT), dp=_dot(dosb[stacked, :], vb[krows, :], NT))

            def softmax_grad(c):
                qrows = slice(c["ti"] * CHUNK, (c["ti"] + 1) * CHUNK)
                ok = band if c["ti"] % nblk > 0 else band & (kj >= jnp.where(s > 0, 0, CHUNK))
                p = jnp.exp(jnp.where(ok, c.pop("sc"), ninf) - per_row(ld[qrows, :]))
                ds = p * (c.pop("dp") - per_row(od[qrows, :]))
                c.update(p=p.astype(BF16), ds=ds.astype(BF16))
                return c

            def grads(c):
                ti = c["ti"]
                stacked, krows = operands(ti)
                dqs = _dot(c["ds"], kb[krows, :])
                pdq[ti * CHUNK:(ti + 1) * CHUNK, :] = jnp.where(m0, dqs[:CHUNK], dqs[CHUNK:])
                tk[stacked, :] = _dot(c["ds"], qsb[stacked, :], TN)
                tv[stacked, :] = _dot(c["p"], dosb[stacked, :], TN)

            n_tiles = SUPER // CHUNK
            stage_a = scores(0)
            for ti in range(n_tiles):
                ahead = scores(ti + 1) if ti + 1 < n_tiles else None
                grads(softmax_grad(stage_a))
                stage_a = ahead

            for r in range(d):
                dq_ref[pl.ds(r, lq, stride=d), :] += pdq[r * lq:(r + 1) * lq, :]
                for tile_out, cur_ref, prev_ref in ((tk, dkc_ref, dkp_ref), (tv, dvc_ref, dvp_ref)):
                    first = r * nblk * 2 * CHUNK
                    prev_ref[pl.ds(SUPER - CHUNK * d + r, CHUNK, stride=d), :] += tile_out[first:first + CHUNK, :]
                    for nb in range(nblk):
                        at = (r * nblk + nb) * 2 * CHUNK
                        part = tile_out[at + CHUNK:at + 2 * CHUNK, :]
                        if nb + 1 < nblk:
                            part = part + tile_out[at + 2 * CHUNK:at + 3 * CHUNK, :]
                        cur_ref[pl.ds(r + d * nb * CHUNK, CHUNK, stride=d), :] += part

    big = pltpu.VMEM((2 * SUPER, 128), F32)
    one = pltpu.VMEM((SUPER, 128), F32)
    half = pltpu.VMEM((2 * SUPER, 128), BF16)
    return pl.pallas_call(
        body, name="attn_bwd", grid=(4, t // SUPER),
        in_specs=[cur, prev, cur, prev, cur, cur, cur, cur, pl.BlockSpec((128, 128), lambda h, s: (0, 0))],
        out_specs=[cur] * 5, out_shape=[SDS((t, ATTN_DIM), F32)] * 5,
        scratch_shapes=[big, big, one, one, half, half, half, half, big, big, one, one],
        compiler_params=_params("parallel", "arbitrary"))(qn, kd, kd, vd, vd, out, lse, dout, ones_pair)


def _ssd_consts():
    tri = np.tril(np.ones((CHUNK, CHUNK), np.float32))
    expand = np.zeros((128, SSM_INNER), np.float32)
    for h in range(SSM_HEADS):
        expand[h, h * HEAD_DIM:(h + 1) * HEAD_DIM] = 1.0
    return jnp.asarray(tri, BF16), jnp.asarray(tri.T, BF16), jnp.asarray(expand, BF16), jnp.asarray(expand.T, BF16)


def _conv4(x, halo, w_ref, b_ref):
    acc = b_ref[...] + w_ref[3:4, :] * x
    for k in range(3):
        acc = acc + w_ref[k:k + 1, :] * _shift_down(x, halo, 3 - k)
    return acc


def _softplus(x):
    return jnp.maximum(x, 0.0) + jnp.log(1.0 + jnp.exp(-jnp.abs(x)))


def _ssd_common(xs_ref, bc_ref, dt_ref, hx_ref, hb_ref, cwx_ref, cbx_ref, cwb_ref, cbb_ref, dtb_ref, alog_ref,
                tri_ref, exp_ref, first):
    keep = 1.0 - first.astype(F32)
    hx = hx_ref[...] * keep
    hb = hb_ref[...] * keep
    pre_x = _conv4(xs_ref[...], hx, cwx_ref, cbx_ref)
    pre_b = _conv4(bc_ref[...], hb, cwb_ref, cbb_ref)
    xa = pre_x * _sigmoid(pre_x)
    ba = pre_b * _sigmoid(pre_b)
    dtv = _softplus(dt_ref[...] + dtb_ref[...])
    a_neg = -jnp.exp(alog_ref[...])
    acum = _hdot(tri_ref[...], dtv * a_neg, parts=3)
    lam = jnp.exp(acum)
    gam = jnp.exp(acum[CHUNK - 1:CHUNK, :] - acum)
    dt_e = _hdot(dtv, exp_ref[...])
    lam_e = _hdot(lam, exp_ref[...])
    gam_e = _hdot(gam, exp_ref[...])
    return dict(hx=hx, hb=hb, pre_x=pre_x, pre_b=pre_b, xa=xa, ba=ba, dtv=dtv, a_neg=a_neg, acum=acum,
                dt_e=dt_e, lam_e=lam_e, gam_e=gam_e, xdt=xa * dt_e)


def _decay(acum_t, h, transposed):
    rb = jnp.broadcast_to(acum_t[h:h + 1, :], (CHUNK, CHUNK))
    ri = lax.broadcasted_iota(jnp.int32, (CHUNK, CHUNK), 0)
    ci = lax.broadcasted_iota(jnp.int32, (CHUNK, CHUNK), 1)
    if transposed:
        return jnp.exp(jnp.where(ci >= ri, rb - rb.T, -jnp.inf))
    return jnp.exp(jnp.where(ri >= ci, rb.T - rb, -jnp.inf))


def _ssd_specs(t, rev):
    nc = t // CHUNK
    ch = (lambda c: nc - 1 - c) if rev else (lambda c: c)
    col = lambda w, j: pl.BlockSpec((CHUNK, w), lambda c: (ch(c), j))
    halo = lambda w, j: pl.BlockSpec((8, w), lambda c: (jnp.maximum(ch(c) * (CHUNK // 8) - 1, 0), j))
    return nc, ch, col, halo


def _ssd_fwd(proj, cwx, cbx, cwb, cbb, dtb, alog, dsk_e, norm_g, tri, expand):
    t = proj.shape[0]
    nc, _, col, halo = _ssd_specs(t, False)

    def body(z_ref, xs_ref, bc_ref, dt_ref, hx_ref, hb_ref, cwx_ref, cbx_ref, cwb_ref, cbb_ref, dtb_ref, alog_ref,
             dsk_ref, g_ref, tri_ref, exp_ref, y_ref, hs_ref, o_ref, state):
        c = pl.program_id(0)

        @pl.when(c == 0)
        def _():
            state[...] = jnp.zeros_like(state)

        v = _ssd_common(xs_ref, bc_ref, dt_ref, hx_ref, hb_ref, cwx_ref, cbx_ref, cwb_ref, cbb_ref, dtb_ref,
                        alog_ref, tri_ref, exp_ref, c == 0)
        acum_t = v["acum"].T
        xdt, ba = v["xdt"], v["ba"]
        h_in = state[...]
        hs_ref[0] = h_in
        xg = xdt * v["gam_e"]
        m0 = lax.broadcasted_iota(jnp.int32, (CHUNK, 128), 1) < HEAD_DIM
        for g in range(2):
            bg = ba[:, g * 128:(g + 1) * 128].astype(BF16)
            cg = ba[:, 256 + g * 128:256 + (g + 1) * 128].astype(BF16)
            gl = slice(g * 512, (g + 1) * 512)
            cb = _dot(cg, bg, NT)
            y_off = _dot(cg, h_in[:, gl].astype(BF16)) * v["lam_e"][:, gl]
            s_new = _dot(bg.T, xg[:, gl].astype(BF16))
            state[:, gl] = h_in[:, gl] * v["lam_e"][CHUNK - 1:CHUNK, gl] + s_new
            for j in range(4):
                h0 = 8 * g + 2 * j
                ln = slice(g * 512 + j * 128, g * 512 + (j + 1) * 128)
                xp = xdt[:, ln].astype(BF16)
                y0 = _dot((cb * _decay(acum_t, h0, False)).astype(BF16), xp)
                y1 = _dot((cb * _decay(acum_t, h0 + 1, False)).astype(BF16), xp)
                y_ref[:, ln] = jnp.where(m0, y0, y1) + y_off[:, j * 128:(j + 1) * 128]
        z = z_ref[...]
        yg = (y_ref[...] + dsk_ref[...] * v["xa"]) * (z * _sigmoid(z))
        r = lax.rsqrt(jnp.mean(yg * yg, axis=-1, keepdims=True) + EPS)
        o_ref[...] = (yg * r * g_ref[...]).astype(BF16)

    full = lambda a: pl.BlockSpec(a.shape, lambda c: (0,) * a.ndim)
    return pl.pallas_call(
        body, name="ssd_fwd", grid=(nc,),
        in_specs=[col(1024, 1), col(1024, 2), col(512, 6), col(128, 28), halo(1024, 2), halo(512, 6),
                  full(cwx), full(cbx), full(cwb), full(cbb), full(dtb), full(alog), full(dsk_e), full(norm_g),
                  full(tri), full(expand)],
        out_specs=[pl.BlockSpec((CHUNK, SSM_INNER), lambda c: (c, 0)),
                   pl.BlockSpec((1, 128, SSM_INNER), lambda c: (c, 0, 0)),
                   pl.BlockSpec((CHUNK, SSM_INNER), lambda c: (c, 0))],
        out_shape=[SDS((t, SSM_INNER), F32), SDS((nc, 128, SSM_INNER), F32), SDS((t, SSM_INNER), BF16)],
        scratch_shapes=[pltpu.VMEM((128, SSM_INNER), F32)],
        compiler_params=_params("arbitrary"))(proj, proj, proj, proj, proj, proj, cwx, cbx, cwb, cbb, dtb, alog,
                                              dsk_e, norm_g, tri, expand)


def _ssd_bwd(proj, y_ssd, hs, dout, cwx, cbx, cwb, cbb, dtb, alog, dsk_e, norm_g, tri, triu, expand, expand_t):
    t = proj.shape[0]
    nc, ch, col, halo = _ssd_specs(t, True)

    def body(z_ref, xs_ref, bc_ref, dt_ref, hx_ref, hb_ref, y_ref, hin_ref, do_ref,
             cwx_ref, cbx_ref, cwb_ref, cbb_ref, dtb_ref, alog_ref, dsk_ref, g_ref, tri_ref, triu_ref, exp_ref, expt_ref,
             dz_ref, dxs_ref, dbc_ref, ddt_ref, dg_ref, ddsk_ref, dalog_ref, ddtb_ref, dcwx_ref, dcbx_ref, dcwb_ref,
             dcbb_ref, gstate, nx_x, nx_b, dact_b, dxdt_s):
        step = pl.program_id(0)
        c = nc - 1 - step

        @pl.when(step == 0)
        def _():
            gstate[...] = jnp.zeros_like(gstate)
            nx_x[...] = jnp.zeros_like(nx_x)
            nx_b[...] = jnp.zeros_like(nx_b)
            for ref in (dg_ref, ddsk_ref, dalog_ref, ddtb_ref, dcwx_ref, dcbx_ref, dcwb_ref, dcbb_ref):
                ref[...] = jnp.zeros_like(ref)

        v = _ssd_common(xs_ref, bc_ref, dt_ref, hx_ref, hb_ref, cwx_ref, cbx_ref, cwb_ref, cbb_ref, dtb_ref,
                        alog_ref, tri_ref, exp_ref, c == 0)
        acum_t = v["acum"].T
        xa, ba, xdt, dtv = v["xa"], v["ba"], v["xdt"], v["dtv"]
        lam_e, gam_e, dt_e = v["lam_e"], v["gam_e"], v["dt_e"]
        z = z_ref[...]
        y = y_ref[...]
        sz = _sigmoid(z)
        zs = z * sz
        y_tot = y + dsk_ref[...] * xa
        yg = y_tot * zs
        r = lax.rsqrt(jnp.mean(yg * yg, axis=-1, keepdims=True) + EPS)
        yh = yg * r
        do = do_ref[...]
        dg_ref[...] += jnp.sum(do * yh, axis=0, keepdims=True)
        gd = do * g_ref[...]
        dyg = r * (gd - yh * jnp.mean(gd * yh, axis=-1, keepdims=True))
        dz_ref[...] = (dyg * y_tot * (sz * (1.0 + z * (1.0 - sz)))).astype(BF16)
        dy = dyg * zs
        ddsk_ref[...] += jnp.sum(dy * xa, axis=0, keepdims=True)
        g_out = gstate[...]
        h_in = hin_ref[0]
        lam_dy = lam_e * dy
        gam_x = gam_e * xdt
        m0 = lax.broadcasted_iota(jnp.int32, (CHUNK, 128), 1) < HEAD_DIM
        lane = lax.broadcasted_iota(jnp.int32, (CHUNK, 128), 1)
        below = (lax.broadcasted_iota(jnp.int32, (CHUNK, CHUNK), 0) >
                 lax.broadcasted_iota(jnp.int32, (CHUNK, CHUNK), 1))
        da_in = jnp.zeros((CHUNK, 128), F32)
        off_y, off_x = [], []
        for g in range(2):
            bg = ba[:, g * 128:(g + 1) * 128].astype(BF16)
            cg = ba[:, 256 + g * 128:256 + (g + 1) * 128].astype(BF16)
            gl = slice(g * 512, (g + 1) * 512)
            gg = g_out[:, gl].astype(BF16)
            bc_t = _dot(bg, cg, NT)
            cb = _dot(cg, bg, NT)
            dxdt_off = _dot(bg, gg) * gam_e[:, gl]
            off_x.append(xdt[:, gl] * dxdt_off)
            off_y.append(dy[:, gl] * (_dot(cg, h_in[:, gl].astype(BF16)) * lam_e[:, gl]))
            q_sum = jnp.zeros((CHUNK, CHUNK), F32)
            for j in range(4):
                h0 = 8 * g + 2 * j
                ln = slice(g * 512 + j * 128, g * 512 + (j + 1) * 128)
                dyp = dy[:, ln]
                dyb = dyp.astype(BF16)
                xpb = xdt[:, ln].astype(BF16)
                d0 = _dot((bc_t * _decay(acum_t, h0, True)).astype(BF16), dyb)
                d1 = _dot((bc_t * _decay(acum_t, h0 + 1, True)).astype(BF16), dyb)
                dxdt_s[:, ln] = jnp.where(m0, d0, d1) + dxdt_off[:, j * 128:(j + 1) * 128]
                for hh, dym in ((h0, jnp.where(m0, dyp, 0.0)), (h0 + 1, jnp.where(m0, 0.0, dyp))):
                    qd = _dot(dym.astype(BF16), xpb, NT) * _decay(acum_t, hh, False)
                    q_sum = q_sum + qd
                    reach = jnp.where(below, _hdot(triu_ref[...], qd * cb), 0.0)
                    da_in = jnp.where(lane == hh, jnp.sum(reach, axis=-1, keepdims=True), da_in)
            gstate[:, gl] = g_out[:, gl] * lam_e[CHUNK - 1:CHUNK, gl] + _dot(cg.T, lam_dy[:, gl].astype(BF16))
            qb = q_sum.astype(BF16)
            dact_b[:, 256 + g * 128:256 + (g + 1) * 128] = (
                _dot(qb, bg) + _dot(lam_dy[:, gl].astype(BF16), h_in[:, gl].astype(BF16), NT))
            dact_b[:, g * 128:(g + 1) * 128] = _dot(qb.T, cg) + _dot(gam_x[:, gl].astype(BF16), gg, NT)
        dxdt = dxdt_s[...]
        seg_y = _hdot(jnp.concatenate(off_y, axis=1), expt_ref[...])
        seg_x = _hdot(jnp.concatenate(off_x, axis=1), expt_ref[...])
        e_col = jnp.sum(g_out * h_in * lam_e[CHUNK - 1:CHUNK, :], axis=0, keepdims=True)
        e_seg = _hdot(jnp.broadcast_to(e_col, (8, SSM_INNER)), expt_ref[...])[0:1, :]
        da = da_in + _hdot(triu_ref[...], seg_y) + (_hdot(tri_ref[...], seg_x) - seg_x) + e_seg
        a_neg = v["a_neg"]
        ddtv = da * a_neg + _hdot(dxdt * xa, expt_ref[...])
        dalog_ref[...] += jnp.sum(da * dtv, axis=0, keepdims=True) * a_neg
        lane16 = lax.broadcasted_iota(jnp.int32, (CHUNK, 128), 1) < SSM_HEADS
        draw = jnp.where(lane16, ddtv * _sigmoid(dt_ref[...] + dtb_ref[...]), 0.0)
        ddtb_ref[...] += jnp.sum(draw, axis=0, keepdims=True)
        ddt_ref[...] = draw.astype(BF16)
        dxa = dxdt * dt_e + dy * dsk_ref[...]
        for (dact, pre, x_ref, nx, cw_ref, dcw_ref, dcb_ref, dx_ref) in (
                (dxa, v["pre_x"], xs_ref, nx_x, cwx_ref, dcwx_ref, dcbx_ref, dxs_ref),
                (dact_b[...], v["pre_b"], bc_ref, nx_b, cwb_ref, dcwb_ref, dcbb_ref, dbc_ref)):
            sp = _sigmoid(pre)
            dpre = dact * (sp * (1.0 + pre * (1.0 - sp)))
            dcb_ref[...] += jnp.sum(dpre, axis=0, keepdims=True)
            xv = x_ref[...]
            nxt = nx[...]
            dx = cw_ref[3:4, :] * dpre
            dcw_ref[3:4, :] += jnp.sum(dpre * xv, axis=0, keepdims=True)
            for k in range(3):
                d_up = _shift_up(dpre, nxt, 3 - k)
                dcw_ref[k:k + 1, :] += jnp.sum(xv * d_up, axis=0, keepdims=True)
                dx = dx + cw_ref[k:k + 1, :] * d_up
            nx[...] = dpre[0:8, :]
            dx_ref[...] = dx.astype(dx_ref.dtype)

    full = lambda a: pl.BlockSpec(a.shape, lambda c: (0,) * a.ndim)
    rowblk = lambda w: pl.BlockSpec((CHUNK, w), lambda c: (ch(c), 0))
    acc = lambda a, b: pl.BlockSpec((a, b), lambda c: (0, 0))
    return pl.pallas_call(
        body, name="ssd_bwd", grid=(nc,),
        in_specs=[col(1024, 1), col(1024, 2), col(512, 6), col(128, 28), halo(1024, 2), halo(512, 6),
                  rowblk(SSM_INNER),
                  pl.BlockSpec((1, 128, SSM_INNER), lambda c: (ch(c), 0, 0)),
                  rowblk(SSM_INNER),
                  full(cwx), full(cbx), full(cwb), full(cbb), full(dtb), full(alog), full(dsk_e), full(norm_g),
                  full(tri), full(triu), full(expand), full(expand_t)],
        out_specs=[rowblk(SSM_INNER), rowblk(SSM_INNER), rowblk(512), rowblk(128),
                   acc(1, 1024), acc(1, 1024), acc(1, 128), acc(1, 128), acc(4, 1024), acc(1, 1024), acc(4, 512),
                   acc(1, 512)],
        out_shape=[SDS((t, SSM_INNER), BF16), SDS((t, SSM_INNER), BF16), SDS((t, 512), BF16), SDS((t, 128), BF16),
                   SDS((1, 1024), F32), SDS((1, 1024), F32), SDS((1, 128), F32), SDS((1, 128), F32),
                   SDS((4, 1024), F32), SDS((1, 1024), F32), SDS((4, 512), F32), SDS((1, 512), F32)],
        scratch_shapes=[pltpu.VMEM((128, SSM_INNER), F32), pltpu.VMEM((8, 1024), F32), pltpu.VMEM((8, 512), F32),
                        pltpu.VMEM((CHUNK, 512), F32), pltpu.VMEM((CHUNK, SSM_INNER), F32)],
        compiler_params=_params("arbitrary"))(proj, proj, proj, proj, proj, proj, y_ssd, hs, dout,
                                              cwx, cbx, cwb, cbb, dtb, alog, dsk_e, norm_g, tri, triu, expand,
                                              expand_t)


def _conv3(x, halo, w_ref, b_ref, part):
    acc = b_ref[part] + w_ref[2, part] * x
    for k in range(2):
        acc = acc + w_ref[k, part] * _shift_down(x, halo, 2 - k)
    return acc


def _up_act(x, g, w_up_t, cw, cb, tm=2048, tn=256):
    t, k = x.shape
    nj = D_FF // tn

    def body(x_ref, g_ref, wg_ref, wv_ref, w_ref, b_ref, u_ref, c_ref, h_ref, f_ref, halo):
        i, j = pl.program_id(0), pl.program_id(1)

        @pl.when(j == 0)
        def _():
            xv = x_ref[...]
            r = lax.rsqrt(jnp.mean(xv * xv, axis=-1, keepdims=True) + EPS)
            h_ref[...] = (xv * r * g_ref[...]).astype(BF16)

        @pl.when(i == 0)
        def _():
            halo[j] = jnp.zeros((2, 8, tn), F32)

        us = [_dot(h_ref[...], wt_ref[...], NT) for wt_ref in (wg_ref, wv_ref)]
        parts = []
        for part, u in enumerate(us):
            u_ref[part] = u.astype(BF16)
            parts.append(_conv3(u, halo[j, part], w_ref, b_ref, part))
            c_ref[part] = parts[-1].astype(BF16)
            halo[j, part] = u[tm - 8:, :]
        gate, val = parts
        f_ref[...] = (gate * _sigmoid(gate) * val).astype(BF16)

    return pl.pallas_call(
        body, name="up_proj", grid=(t // tm, nj),
        in_specs=[pl.BlockSpec((tm, k), lambda i, j: (i, 0)), pl.BlockSpec((1, k), lambda i, j: (0, 0)),
                  pl.BlockSpec((tn, k), lambda i, j: (j, 0)), pl.BlockSpec((tn, k), lambda i, j: (j + nj, 0)),
                  pl.BlockSpec((3, 2, 1, tn), lambda i, j: (0, 0, 0, j)), pl.BlockSpec((2, 1, tn), lambda i, j: (0, 0, j))],
        out_specs=[pl.BlockSpec((2, tm, tn), lambda i, j: (0, i, j)), pl.BlockSpec((2, tm, tn), lambda i, j: (0, i, j)),
                   pl.BlockSpec((tm, k), lambda i, j: (i, 0)), pl.BlockSpec((tm, tn), lambda i, j: (i, j))],
        out_shape=[SDS((2, t, D_FF), BF16), SDS((2, t, D_FF), BF16), SDS((t, k), BF16), SDS((t, D_FF), BF16)],
        scratch_shapes=[pltpu.VMEM((nj, 2, 8, tn), F32)],
        compiler_params=_params("arbitrary", "arbitrary"))(x, g, w_up_t, w_up_t, cw, cb)


def _ffn_bwd(dx2, w_down, u, c, cw, tm=512, tn=1408):
    t = u.shape[1]
    nj, ni = D_FF // tn, t // tm
    rev = lambda i: ni - 1 - i

    def body(dx_ref, wd_ref, u_ref, c_ref, w_ref, du_ref, dcw_ref, dcb_ref, nxt):
        i = pl.program_id(1)

        @pl.when(i == 0)
        def _():
            nxt[...] = jnp.zeros_like(nxt)
            dcw_ref[...] = jnp.zeros_like(dcw_ref)
            dcb_ref[...] = jnp.zeros_like(dcb_ref)

        df = _dot(dx_ref[...].astype(BF16), wd_ref[...], NT)
        gate, val = c_ref[0].astype(F32), c_ref[1].astype(F32)
        sg = _sigmoid(gate)
        dgate = df * val * (sg * (1.0 + gate * (1.0 - sg)))
        dval = df * (gate * sg)
        for part, d in enumerate((dgate, dval)):
            uu = u_ref[part].astype(F32)
            dcb_ref[part] += jnp.sum(d, axis=0, keepdims=True)
            ahead = nxt[part]
            acc = w_ref[2, part] * d
            dcw_ref[2, part] += jnp.sum(d * uu, axis=0, keepdims=True)
            for k in range(2):
                d_up = _shift_up(d, ahead, 2 - k)
                dcw_ref[k, part] += jnp.sum(uu * d_up, axis=0, keepdims=True)
                acc = acc + w_ref[k, part] * d_up
            nxt[part] = d[0:8, :]
            du_ref[part] = acc.astype(BF16)

    w_spec = pl.BlockSpec((3, 2, 1, tn), lambda j, i: (0, 0, 0, j))
    b_spec = pl.BlockSpec((2, 1, tn), lambda j, i: (0, 0, j))
    tile = pl.BlockSpec((2, tm, tn), lambda j, i: (0, rev(i), j))
    return pl.pallas_call(
        body, name="ffn_bwd", grid=(nj, ni),
        in_specs=[pl.BlockSpec((tm, D_MODEL), lambda j, i: (rev(i), 0)), pl.BlockSpec((tn, D_MODEL), lambda j, i: (j, 0)),
                  tile, tile, w_spec],
        out_specs=[tile, w_spec, b_spec],
        out_shape=[SDS((2, t, D_FF), BF16), SDS((3, 2, 1, D_FF), F32), SDS((2, 1, D_FF), F32)],
        scratch_shapes=[pltpu.VMEM((2, 8, tn), F32)],
        compiler_params=_params("parallel", "arbitrary"))(dx2, w_down, u, c, cw)


def _ple_loss(x2, g, w_gate, p, w_proj_t, target, tm=256):
    t = x2.shape[0]

    def body(x_ref, g_ref, wg_ref, p_ref, wp_ref, tg_ref, dx_ref, dpre_ref, dpp_ref, h_ref, loss_ref, dg_ref):
        i = pl.program_id(0)
        xv = x_ref[...]
        r = lax.rsqrt(jnp.mean(xv * xv, axis=-1, keepdims=True) + EPS)
        xh = xv * r
        h = (xh * g_ref[...]).astype(BF16)
        h_ref[...] = h
        gate = _sigmoid(_dot(h, wg_ref[...]))
        pp = _dot(p_ref[...].astype(BF16), wp_ref[...], NT)
        err = (xv + gate * pp) - tg_ref[...]

        @pl.when(i == 0)
        def _():
            loss_ref[...] = jnp.zeros_like(loss_ref)
            dg_ref[...] = jnp.zeros_like(dg_ref)

        loss_ref[...] += 0.5 * jnp.sum(jnp.mean(err * err, axis=-1, keepdims=True), axis=0, keepdims=True)
        dy = err * (1.0 / D_MODEL)
        dpre = (dy * pp * gate * (1.0 - gate)).astype(BF16)
        dpre_ref[...] = dpre
        dpp_ref[...] = (dy * gate).astype(BF16)
        dh = _dot(dpre, wg_ref[...], NT)
        dg_ref[...] += jnp.sum(dh * xh, axis=0, keepdims=True)
        gd = dh * g_ref[...]
        dx_ref[...] = dy + r * (gd - xh * jnp.mean(gd * xh, axis=-1, keepdims=True))

    row = lambda w: pl.BlockSpec((tm, w), lambda i: (i, 0))
    full = lambda a: pl.BlockSpec(a.shape, lambda i: (0, 0))
    return pl.pallas_call(
        body, name="ple_loss", grid=(t // tm,),
        in_specs=[row(D_MODEL), full(g), full(w_gate), row(PLE_DIM), full(w_proj_t), row(D_MODEL)],
        out_specs=[row(D_MODEL), row(D_MODEL), row(D_MODEL), row(D_MODEL),
                   pl.BlockSpec((1, 128), lambda i: (0, 0)), pl.BlockSpec((1, D_MODEL), lambda i: (0, 0))],
        out_shape=[SDS((t, D_MODEL), F32), SDS((t, D_MODEL), BF16), SDS((t, D_MODEL), BF16), SDS((t, D_MODEL), BF16),
                   SDS((1, 128), F32), SDS((1, D_MODEL), F32)],
        compiler_params=_params("arbitrary"))(x2, g, w_gate, p, w_proj_t, target)


def _all_gather(arrays, name):
    n_a = len(arrays)

    def body(*refs):
        src, dst = refs[:n_a], refs[n_a:2 * n_a]
        send_sems, recv_sems, local_sems = refs[2 * n_a:]
        x, y, c = lax.axis_index("x"), lax.axis_index("y"), lax.axis_index("c")
        slot = lambda px, py, pc: 4 * px + 2 * py + pc
        me, sibling = (x, y, c), (x, y, 1 - c)
        chips = [(1 - x, y), (x, 1 - y), (1 - x, 1 - y)]

        def copy(a, k, block, to, own=False):
            return pltpu.make_async_remote_copy(
                src_ref=src[a] if own else dst[a].at[slot(*block)], dst_ref=dst[a].at[slot(*block)],
                send_sem=send_sems.at[a, k], recv_sem=recv_sems.at[a, k], device_id=to,
                device_id_type=pl.DeviceIdType.MESH)

        local = [pltpu.make_async_copy(src[a], dst[a].at[slot(*me)], local_sems.at[a]) for a in range(n_a)]
        for cp in local:
            cp.start()
        sends = []
        for a in range(n_a):
            sends.append(copy(a, 0, me, sibling, own=True))
            sends += [copy(a, 1 + j, me, (*chip, c), own=True) for j, chip in enumerate(chips)]
        for cp in sends:
            cp.start()
        for j, chip in enumerate(chips):
            for a in range(n_a):
                copy(a, 1 + j, (*chip, c), me).wait_recv()
                passed = copy(a, 4 + j, (*chip, c), sibling)
                passed.start()
                sends.append(passed)
        for a in range(n_a):
            copy(a, 0, sibling, me).wait_recv()
            for j, chip in enumerate(chips):
                copy(a, 4 + j, (*chip, 1 - c), me).wait_recv()
        for cp in sends:
            cp.wait_send()
        for cp in local:
            cp.wait()

    hbm = pl.BlockSpec(memory_space=pl.ANY)
    return pl.pallas_call(
        body, name=name, in_specs=[hbm] * n_a, out_specs=[hbm] * n_a,
        out_shape=[SDS((N_DEV,) + a.shape, a.dtype) for a in arrays],
        scratch_shapes=[pltpu.SemaphoreType.DMA((n_a, N_DEV - 1)), pltpu.SemaphoreType.DMA((n_a, N_DEV - 1)),
                        pltpu.SemaphoreType.DMA((n_a,))],
        )(*arrays)


def _peer(k):
    x, y, c = lax.axis_index("x"), lax.axis_index("y"), lax.axis_index("c")
    px = 1 - x if k & 4 else x
    py = 1 - y if k & 2 else y
    pc = 1 - c if k & 1 else c
    return (px, py, pc), 4 * px + 2 * py + pc


_HBM = pl.BlockSpec(memory_space=pltpu.HBM)
_SEM = pl.BlockSpec(memory_space=pltpu.SEMAPHORE)


def _split_copies(src, land, send_sems, recv_sems, scatter, arrivals):
    _, me = _peer(0)
    out = []
    for k in range(1, N_DEV):
        coords, peer = _peer(k)
        for a in range(len(src)):
            sem = a * (N_DEV - 1) + k - 1
            if scatter[a]:
                s, d = src[a].at[peer], land[a].at[k]
            else:
                s, d = src[a], land[a].at[peer if arrivals else me]
            out.append(pltpu.make_async_remote_copy(
                src_ref=s, dst_ref=d, send_sem=send_sems.at[sem], recv_sem=recv_sems.at[sem], device_id=coords,
                device_id_type=pl.DeviceIdType.MESH))
    return out


def _exchange_start(srcs, lands, scatter, name):
    n = len(srcs)

    def body(*refs):
        src, land = refs[:n], refs[n:2 * n]
        send_sems, recv_sems = refs[2 * n], refs[2 * n + 1]
        token = refs[-1]
        for cp in _split_copies(src, land, send_sems, recv_sems, scatter, False):
            cp.start()
        token[...] = jnp.zeros_like(token)

    hbm_shape = lambda a: pltpu.HBM(a.shape, a.dtype)
    sem_shape = pltpu.SemaphoreType.DMA((n * (N_DEV - 1),))
    outs = pl.pallas_call(
        body, name=name,
        out_shape=(sem_shape, sem_shape, *[hbm_shape(a) for a in srcs], *[hbm_shape(a) for a in lands],
                   SDS((8, 128), F32)),
        in_specs=[_HBM] * (2 * n), out_specs=(_SEM, _SEM, *[_HBM] * (2 * n), pl.BlockSpec(memory_space=pltpu.VMEM)),
        input_output_aliases={a: 2 + a for a in range(2 * n)},
        compiler_params=pltpu.CompilerParams(has_side_effects=pltpu.SideEffectType.DATAFLOW_SIDE_EFFECTING),
    )(*[pltpu.with_memory_space_constraint(a, pltpu.HBM) for a in list(srcs) + list(lands)])
    return outs[0], outs[1], outs[2:2 + n], outs[2 + n:2 + 2 * n], outs[-1]


def _exchange_wait(send_sems, recv_sems, srcs, lands, scatter, after, name):
    n = len(srcs)

    def body(*refs):
        src, land = refs[:n], refs[n:2 * n]
        for cp in _split_copies(src, land, refs[2 * n], refs[2 * n + 1], scatter, False):
            cp.wait_send()
        for cp in _split_copies(src, land, refs[2 * n], refs[2 * n + 1], scatter, True):
            cp.wait_recv()

    hbm_shape = lambda a: pltpu.HBM(a.shape, a.dtype)
    outs = pl.pallas_call(
        body, name=name, out_shape=tuple(hbm_shape(a) for a in list(srcs) + list(lands)),
        in_specs=[_HBM] * (2 * n) + [_SEM, _SEM, pl.BlockSpec(memory_space=pl.ANY)], out_specs=(_HBM,) * (2 * n),
        input_output_aliases={a: a for a in range(2 * n)},
        compiler_params=pltpu.CompilerParams(has_side_effects=pltpu.SideEffectType.DATAFLOW_SIDE_EFFECTING),
    )(*srcs, *lands, send_sems, recv_sems, after)
    return outs[:n], outs[n:]


def _reduce8(a, tr, name):
    _, rows, cols = a.shape

    def body(a_ref, o_ref):
        acc = a_ref[0]
        for j in range(1, N_DEV):
            acc = acc + a_ref[j]
        o_ref[...] = acc

    return pl.pallas_call(
        body, name=name, grid=(rows // tr,),
        in_specs=[pl.BlockSpec((N_DEV, tr, cols), lambda i: (0, i, 0))],
        out_specs=pl.BlockSpec((tr, cols), lambda i: (i, 0)), out_shape=SDS((rows, cols), F32),
        compiler_params=_params("parallel"))(a)


def _reduce_landed(own, land, name, tc=256):
    rows, cols = own.shape

    def body(own_ref, land_ref, o_ref):
        acc = own_ref[...]
        for k in range(1, N_DEV):
            acc = acc + land_ref[k].astype(F32)
        o_ref[...] = acc

    return pl.pallas_call(
        body, name=name, grid=(cols // tc,),
        in_specs=[pl.BlockSpec((rows, tc), lambda j: (0, j)), pl.BlockSpec((N_DEV, rows, tc), lambda j: (0, 0, j))],
        out_specs=pl.BlockSpec((rows, tc), lambda j: (0, j)), out_shape=SDS((rows, cols), F32),
        compiler_params=_params("parallel"))(own, land)


def _adamw(w, g, m, v, name, tr=None):
    rows, cols = w.shape
    tr = rows if tr is None else tr

    def body(w_ref, g_ref, m_ref, v_ref, d_ref, mo_ref, vo_ref):
        d_ref[...], mo_ref[...], vo_ref[...] = _adam_update(w_ref[...], g_ref[...], m_ref[...], v_ref[...])

    blk = pl.BlockSpec((tr, cols), lambda i: (i, 0))
    return pl.pallas_call(
        body, name=name, grid=(rows // tr,), in_specs=[blk] * 4, out_specs=[blk] * 3,
        out_shape=[SDS((rows, cols), F32)] * 3, compiler_params=_params("parallel"))(w, g, m, v)


def _pad_rows(a, rows):
    return jnp.pad(a, ((0, rows - a.shape[0]),) + ((0, 0),) * (a.ndim - 1))


def _local_step(x, p, target, sm, wts, fetch_rest, send, tok):
    ones_q, ones_k, dup, dup_t = _head_consts()
    tri, triu, expand, expand_t = _ssd_consts()
    w_in_t = wts["in_t"]
    cwx, cwb = wts["ssm_cw"][:, :SSM_INNER], wts["ssm_cw"][:, SSM_INNER:]
    cbx, cbb = sm["ssm_conv_b"][:, :SSM_INNER], sm["ssm_conv_b"][:, SSM_INNER:]
    pad128 = lambda a: jnp.pad(a, ((0, 0), (0, 128 - a.shape[1])))
    dtb, alog = pad128(sm["dt_bias"]), pad128(sm["a_log"])
    dsk_e = jnp.repeat(sm["d_skip"], HEAD_DIM, axis=1)
    gq = jnp.tile(sm["q_norm_g"], (1, ATTN_DIM // HEAD_DIM))
    gk = jnp.tile(sm["k_norm_g"], (1, KV_DIM // HEAD_DIM))
    ffn_cw = wts["ffn_cw"].reshape(3, 2, 1, D_FF)
    ffn_cb = sm["ffn_conv_b"].reshape(2, 1, D_FF)

    proj, h1 = _norm_matmul(x, sm["attn_norm_g"] + tok, w_in_t, 1024, 768, "in_proj")
    qn, kd, vd = _attn_prep(proj, gq, gk, ones_q, ones_k, dup)
    attn_out, lse = _attn_fwd(qn, kd, vd)
    y_ssd, hs, ssm_out = _ssd_fwd(proj, cwx, cbx, cwb, cbb, dtb, alog, dsk_e, sm["ssm_norm_g"], tri, expand)
    rest = fetch_rest(ssm_out)
    w_out, w_up_t, w_down, w_gate, w_proj_t = (rest[k] for k in ("out", "up_t", "down", "gate", "proj_t"))
    x1 = _mm_resid([(attn_out, None, w_out[:ATTN_DIM]), (ssm_out, None, w_out[ATTN_DIM:])], x, None, 512, F32,
                   "out_proj")
    u, uc, h2, f = _up_act(x1, sm["ffn_norm_g"], w_up_t, ffn_cw, ffn_cb)
    x2 =_mm_resid([(f, None, w_down)], x1, None, 512, F32, "down_proj")
    dx2, dpre, dpp, h3, loss, dg_ple = _ple_loss(x2, sm["ple_norm_g"], w_gate, p, w_proj_t, target)

    g_gate = _wgrad(h3, None, dpre, "wg_gate")
    g_proj_t = _wgrad(dpp, None, p, "wg_proj")
    g_down = _wgrad(f, None, dx2, "wg_down")
    du, d_ffn_cw, d_ffn_cb = _ffn_bwd(dx2, w_down, u, uc, ffn_cw)
    dx1, dg_ffn = _mm_normbwd([(du, 0, w_up_t, D_FF, 0), (du, 1, w_up_t, D_FF, 1)], x1, sm["ffn_norm_g"], dx2, 256,
                              "up_proj_bwd")
    g_up_t = _wgrad(du, "all", h2, "wg_up")
    tok = send(dict(gate=g_gate, proj_t=g_proj_t, down=g_down, up_t=g_up_t)).astype(BF16)
    d_attn = _mm_resid([(dx1, None, w_out[:ATTN_DIM] + tok)], None, NT, 512, F32, "out_proj_bwd_attn")
    d_ssm = _mm_resid([(dx1, None, w_out[ATTN_DIM:] + tok)], None, NT, 512, F32, "out_proj_bwd_ssm")
    g_out = jnp.concatenate([_wgrad(attn_out, None, dx1, "wg_out_attn"), _wgrad(ssm_out, None, dx1, "wg_out_ssm")],
                            axis=0)
    tok = send(dict(out=g_out))
    (dz, dxs, dbc, ddt, dg_ssm, d_dsk_e, d_alog, d_dtb, d_cwx, d_cbx, d_cwb, d_cbb) = _ssd_bwd(
        proj, y_ssd, hs, d_ssm, cwx, cbx, cwb, cbb, dtb + tok, alog, dsk_e, sm["ssm_norm_g"], tri, triu, expand,
        expand_t)
    dqn, dkc, dkp, dvc, dvp = _attn_bwd(qn, kd, vd, attn_out, lse, d_attn, ones_k[:128, :128])
    dqkv, dgq, dgk = _attn_prep_bwd(proj, dqn, dkc, dkp, dvc, dvp, gq + tok, gk, ones_q, ones_k, dup_t)
    pieces = [(dqkv, 0, 1024), (dz, 1024, 2048), (dxs, 2048, 3072), (dbc, 3072, 3584), (ddt, 3584, 3712)]
    g_in_t = jnp.concatenate([_wgrad(a, None, h1, "wg_in_%d" % lo) for a, lo, _ in pieces], axis=0)[:IN_PROJ]
    tok = send(dict(in_t=g_in_t))
    grad_x, dg_attn = _mm_normbwd([(a, None, w_in_t, hi - lo, lo // (hi - lo)) for a, lo, hi in pieces], x,
                                  sm["attn_norm_g"] + tok, dx1, 256, "in_proj_bwd")

    small = dict(
        attn_norm_g=dg_attn, q_norm_g=dgq.reshape(-1, HEAD_DIM).sum(0, keepdims=True),
        k_norm_g=dgk.reshape(-1, HEAD_DIM).sum(0, keepdims=True),
        ssm_conv_w=jnp.concatenate([d_cwx, d_cwb], axis=1), ssm_conv_b=jnp.concatenate([d_cbx, d_cbb], axis=1),
        dt_bias=d_dtb[:, :SSM_HEADS], a_log=d_alog[:, :SSM_HEADS],
        d_skip=d_dsk_e.reshape(SSM_HEADS, HEAD_DIM).sum(1)[None, :], ssm_norm_g=dg_ssm, ffn_norm_g=dg_ffn,
        ffn_conv_w=d_ffn_cw.reshape(3, 2 * D_FF), ffn_conv_b=d_ffn_cb.reshape(1, 2 * D_FF), ple_norm_g=dg_ple)
    return loss[0, 0], grad_x, small


_SMALL = (("attn_norm_g", 1, 1024), ("q_norm_g", 1, 64), ("k_norm_g", 1, 64), ("ssm_conv_w", 4, XBC_DIM),
          ("ssm_conv_b", 1, XBC_DIM), ("dt_bias", 1, 16), ("a_log", 1, 16), ("d_skip", 1, 16), ("ssm_norm_g", 1, 1024),
          ("ffn_norm_g", 1, 1024), ("ffn_conv_w", 3, 2 * D_FF), ("ffn_conv_b", 1, 2 * D_FF), ("ple_norm_g", 1, 1024))
_SMALL_ROWS, _SMALL_COLS = 32, XBC_DIM
_SHARDED_SMALL = ("ssm_conv_w", "ffn_conv_w")


def _small_chunks(n):
    return 1 if n <= _SMALL_COLS else 4


def _pack_small(parts, loss):
    rows = []
    for k, r, n in _SMALL:
        c = _small_chunks(n)
        rows.append(jnp.pad(parts[k].reshape(r * c, n // c), ((0, 0), (0, _SMALL_COLS - n // c))))
    packed = _pad_rows(jnp.concatenate(rows, axis=0), _SMALL_ROWS)
    at_loss = ((lax.broadcasted_iota(jnp.int32, packed.shape, 0) == _SMALL_ROWS - 1) &
               (lax.broadcasted_iota(jnp.int32, packed.shape, 1) == 0))
    return jnp.where(at_loss, loss, packed)


def _adam_update(w, g, m, v):
    c1 = 1.0 - ADAM_B1 ** ADAM_STEP
    c2 = 1.0 - ADAM_B2 ** ADAM_STEP
    mn = ADAM_B1 * m + (1.0 - ADAM_B1) * g
    vn = ADAM_B2 * v + (1.0 - ADAM_B2) * (g * g)
    return -ADAM_LR * ((mn / c1) / (jnp.sqrt(vn / c2) + ADAM_EPS) + ADAM_WD * w), mn, vn


def _adamw_small(g_all, g_shard, w, m, v):
    ins, shapes = [g_all], []
    for k, _, _ in _SMALL:
        shape2 = w[k].shape if w[k].ndim == 2 else (1, w[k].shape[0])
        shapes.append(shape2)
        ins += ([g_shard[k]] if k in _SHARDED_SMALL else []) + [a.reshape(shape2) for a in (w[k], m[k], v[k])]

    def body(*refs):
        g_ref, pos, row = refs[0], 1, 0
        outs = refs[len(ins):]
        for i, (k, r, n) in enumerate(_SMALL):
            c = _small_chunks(n)
            if k in _SHARDED_SMALL:
                g = refs[pos][...]
                pos += 1
            elif c == 1:
                g = g_ref[row:row + r, 0:n]
            else:
                g = jnp.concatenate([g_ref[row + j:row + j + 1, 0:n // c] for j in range(c)], axis=1)
            row += r * c
            d, mn, vn = _adam_update(refs[pos][...], g, refs[pos + 1][...], refs[pos + 2][...])
            pos += 3
            for o_ref, val in zip(outs[4 * i:4 * i + 4], (g, d, mn, vn)):
                o_ref[...] = val

    res = pl.pallas_call(body, name="adamw_small",
                         out_shape=[SDS(s, F32) for s in shapes for _ in range(4)])(*ins)
    return {k: tuple(a.reshape(w[k].shape) for a in res[4 * i:4 * i + 4]) for i, (k, _, _) in enumerate(_SMALL)}


def kernel(x, p, attn_norm_g, w_in, q_norm_g, k_norm_g, ssm_conv_w, ssm_conv_b, dt_bias, a_log, d_skip, ssm_norm_g, w_out, ffn_norm_g, w_up, ffn_conv_w, ffn_conv_b, w_down, ple_norm_g, w_ple_gate, w_ple_proj, loss_target, m_attn_norm_g, m_w_in, m_q_norm_g, m_k_norm_g, m_ssm_conv_w, m_ssm_conv_b, m_dt_bias, m_a_log, m_d_skip, m_ssm_norm_g, m_w_out, m_ffn_norm_g, m_w_up, m_ffn_conv_w, m_ffn_conv_b, m_w_down, m_ple_norm_g, m_w_ple_gate, m_w_ple_proj, v_attn_norm_g, v_w_in, v_q_norm_g, v_k_norm_g, v_ssm_conv_w, v_ssm_conv_b, v_dt_bias, v_a_log, v_d_skip, v_ssm_norm_g, v_w_out, v_ffn_norm_g, v_w_up, v_ffn_conv_w, v_ffn_conv_b, v_w_down, v_ple_norm_g, v_w_ple_gate, v_w_ple_proj):
    names = ("attn_norm_g", "w_in", "q_norm_g", "k_norm_g", "ssm_conv_w", "ssm_conv_b", "dt_bias", "a_log", "d_skip",
             "ssm_norm_g", "w_out", "ffn_norm_g", "w_up", "ffn_conv_w", "ffn_conv_b", "w_down", "ple_norm_g",
             "w_ple_gate", "w_ple_proj")
    w = dict(zip(names, (attn_norm_g, w_in, q_norm_g, k_norm_g, ssm_conv_w, ssm_conv_b, dt_bias, a_log, d_skip,
                         ssm_norm_g, w_out, ffn_norm_g, w_up, ffn_conv_w, ffn_conv_b, w_down, ple_norm_g, w_ple_gate,
                         w_ple_proj)))
    m = dict(zip(names, (m_attn_norm_g, m_w_in, m_q_norm_g, m_k_norm_g, m_ssm_conv_w, m_ssm_conv_b, m_dt_bias,
                         m_a_log, m_d_skip, m_ssm_norm_g, m_w_out, m_ffn_norm_g, m_w_up, m_ffn_conv_w, m_ffn_conv_b,
                         m_w_down, m_ple_norm_g, m_w_ple_gate, m_w_ple_proj)))
    v = dict(zip(names, (v_attn_norm_g, v_w_in, v_q_norm_g, v_k_norm_g, v_ssm_conv_w, v_ssm_conv_b, v_dt_bias,
                         v_a_log, v_d_skip, v_ssm_norm_g, v_w_out, v_ffn_norm_g, v_w_up, v_ffn_conv_w, v_ffn_conv_b,
                         v_w_down, v_ple_norm_g, v_w_ple_gate, v_w_ple_proj)))
    w, m, v = ({k: a[0] for k, a in d.items()} for d in (w, m, v))
    me = 4 * lax.axis_index("x") + 2 * lax.axis_index("y") + lax.axis_index("c")

    mine = dict(in_t=w["w_in"].T, out=w["w_out"], up_t=w["w_up"].T, down=w["w_down"], gate=w["w_ple_gate"],
                proj_t=w["w_ple_proj"].T)
    mine = {k: a.astype(BF16) for k, a in mine.items()}
    conv_pack = jnp.pad(jnp.concatenate([w["ssm_conv_w"].reshape(-1), w["ffn_conv_w"].reshape(-1)]),
                        (0, 3072 - 2880)).reshape(8, 384)
    all_in, all_conv = _all_gather([mine["in_t"], conv_pack], "gather_first")
    later = ("out", "up_t", "down", "gate", "proj_t")
    zones = [lax.dynamic_update_slice(lax.empty((N_DEV,) + mine[k].shape, BF16), mine[k][None], (me, 0, 0))
             for k in later]
    zones, all_in, all_conv = lax.optimization_barrier((zones, all_in, all_conv))
    rest_state = _exchange_start([mine[k] for k in later], zones, [False] * len(later), "gather_rest_start")

    def fetch_rest(after):
        _, landed = _exchange_wait(*rest_state[:4], [False] * len(later), after, "gather_rest_wait")
        return {k: a.reshape(N_DEV * a.shape[1], a.shape[2]) for k, a in zip(later, landed)}

    wts = dict(in_t=_pad_rows(all_in.reshape(IN_PROJ, D_MODEL), IN_PROJ_PAD))
    conv_flat = all_conv.reshape(N_DEV, 3072)
    wts["ssm_cw"] = conv_flat[:, :768].reshape(N_DEV, 4, XBC_DIM // N_DEV).transpose(1, 0, 2).reshape(4, XBC_DIM)
    wts["ffn_cw"] = conv_flat[:, 768:2880].reshape(N_DEV, 3, 2 * D_FF // N_DEV).transpose(1, 0, 2).reshape(3, 2 * D_FF)
    sm = {k: w[k].reshape(1, -1) for k, _, _ in _SMALL if k not in _SHARDED_SMALL}

    in_flight = []

    def send(grads):
        keys = sorted(grads)
        blocks = [grads[k].reshape(N_DEV, grads[k].shape[0] // N_DEV, grads[k].shape[1]) for k in keys]
        own = [lax.dynamic_index_in_dim(a, me, 0, keepdims=False) for a in blocks]
        srcs = [a.astype(BF16) for a in blocks]
        state = _exchange_start(srcs, [lax.empty(a.shape, BF16) for a in srcs], [True] * len(keys),
                                "send_" + "_".join(keys))
        in_flight.append((keys, state, own))
        return state[4][0:1, 0:1]

    loss, grad_x, small = _local_step(x[0], p[0, 0], loss_target[0], sm, wts, fetch_rest, send,
                                      rest_state[4][0:1, 0:1])

    (got_small,) = _all_gather([_pack_small(small, loss)], "gather_small_grads")
    g_small = _reduce8(got_small, _SMALL_ROWS, "reduce_small")
    loss = g_small[_SMALL_ROWS - 1, 0]
    grads = {}
    for keys, state, own in in_flight:
        _, landed = _exchange_wait(*state[:4], [True] * len(keys), grad_x, "wait_" + "_".join(keys))
        for k, mine_k, land in zip(keys, own, landed):
            grads[k] = _reduce_landed(mine_k, land, "reduce_" + k)
    gw = {"w_in": grads["in_t"].T, "w_out": grads["out"], "w_up": grads["up_t"].T, "w_down": grads["down"],
          "w_ple_gate": grads["gate"], "w_ple_proj": grads["proj_t"].T}
    n_ssm, n_ffn = XBC_DIM // N_DEV, 2 * D_FF // N_DEV
    g_shard = {"ssm_conv_w": lax.dynamic_slice(g_small, (3, me * n_ssm), (4, n_ssm)),
               "ffn_conv_w": lax.dynamic_slice(g_small[13:25, :2 * D_FF // 4].reshape(3, 2 * D_FF), (0, me * n_ffn),
                                               (3, n_ffn))}

    delta, new_m, new_v = {}, {}, {}
    for k, tr in (("w_in", 256), ("w_out", None), ("w_up", 256), ("w_down", None), ("w_ple_gate", None),
                  ("w_ple_proj", None)):
        delta[k], new_m[k], new_v[k] = _adamw(w[k], gw[k], m[k], v[k], "adamw_" + k, tr)
    for k, (g_k, d_k, m_k, v_k) in _adamw_small(g_small, g_shard, w, m, v).items():
        gw[k], delta[k], new_m[k], new_v[k] = g_k, d_k, m_k, v_k

    outs = [loss, grad_x[None]]
    for d in (gw, delta, new_m, new_v):
        outs += [d[k][None] for k in names]
    return tuple(outs)
```

```python
import functools

import numpy as np
import jax
import jax.numpy as jnp
from jax import lax
from jax.experimental import pallas as pl
from jax.experimental.pallas import tpu as pltpu

F32 = jnp.float32
BF16 = jnp.bfloat16
SDS = jax.ShapeDtypeStruct
EPS = 1e-6
N_DEV = 8
D_MODEL = 1024
HEAD_DIM = 64
ATTN_DIM = 512
KV_DIM = 256
SSM_INNER = 1024
SSM_HEADS = 16
BC_DIM = 256
XBC_DIM = SSM_INNER + 2 * BC_DIM
MIX_DIM = ATTN_DIM + SSM_INNER
IN_PROJ = 3600
IN_PROJ_PAD = 3840
D_FF = 2816
PLE_DIM = 256
CHUNK = 128
SUPER = 2048
DILATIONS = (1, 4, 16)
TILE_UNROLL = 8
VMEM_LIMIT = 56 * 1024 * 1024
ADAM_LR, ADAM_B1, ADAM_B2, ADAM_EPS, ADAM_WD, ADAM_STEP = 0.001, 0.9, 0.999, 1e-08, 0.01, 10

NT = (((1,), (1,)), ((), ()))
TN = (((0,), (0,)), ((), ()))


def _params(*sem):
    return pltpu.CompilerParams(dimension_semantics=sem if sem else None, vmem_limit_bytes=VMEM_LIMIT)


def _dot(a, b, dims=None):
    if dims is None:
        return jnp.dot(a, b, preferred_element_type=F32)
    return lax.dot_general(a, b, dims, preferred_element_type=F32)


def _hdot(a, b, parts=2):
    a_exact = a.dtype == BF16
    x = b if a_exact else a
    acc = None
    for _ in range(parts):
        piece = x.astype(BF16)
        x = x - piece.astype(F32)
        d = _dot(a, piece) if a_exact else _dot(piece, b)
        acc = d if acc is None else acc + d
    return acc


def _sigmoid(x):
    return 0.5 * jnp.tanh(0.5 * x) + 0.5


def _shift_down(x, halo8, s):
    xr = pltpu.roll(x, s, 0)
    row = lax.broadcasted_iota(jnp.int32, halo8.shape, 0)
    first = jnp.where(row < s, pltpu.roll(halo8, s, 0), xr[0:8])
    return jnp.concatenate([first, xr[8:]], axis=0)


def _shift_up(x, halo8, s):
    n = x.shape[0]
    xr = pltpu.roll(x, n - s, 0)
    row = lax.broadcasted_iota(jnp.int32, halo8.shape, 0)
    last = jnp.where(row >= 8 - s, pltpu.roll(halo8, 8 - s, 0), xr[n - 8:])
    return jnp.concatenate([xr[:n - 8], last], axis=0)


def _norm_matmul(x, g, wt, tm, tn, name):
    t, k = x.shape
    n = wt.shape[0]

    def body(x_ref, g_ref, w_ref, o_ref, h_ref):
        @pl.when(pl.program_id(1) == 0)
        def _():
            xv = x_ref[...]
            r = lax.rsqrt(jnp.mean(xv * xv, axis=-1, keepdims=True) + EPS)
            h_ref[...] = (xv * r * g_ref[...]).astype(BF16)
        o_ref[...] = _dot(h_ref[...], w_ref[...], NT)

    return pl.pallas_call(
        body, name=name, grid=(t // tm, n // tn),
        in_specs=[pl.BlockSpec((tm, k), lambda i, j: (i, 0)), pl.BlockSpec((1, k), lambda i, j: (0, 0)),
                  pl.BlockSpec((tn, k), lambda i, j: (j, 0))],
        out_specs=[pl.BlockSpec((tm, tn), lambda i, j: (i, j)), pl.BlockSpec((tm, k), lambda i, j: (i, 0))],
        out_shape=[SDS((t, n), F32), SDS((t, k), BF16)],
        compiler_params=_params("parallel", "arbitrary"))(x, g, wt)


def _a_spec(a, lead, tm):
    if lead is None:
        return pl.BlockSpec((tm, a.shape[-1]), lambda i: (i, 0))
    return pl.BlockSpec((None, tm, a.shape[-1]), lambda i, _l=lead: (_l, i, 0))


def _mm_resid(pairs, res, dims, tm, out_dtype, name):
    t = pairs[0][0].shape[-2]
    n = pairs[0][2].shape[1] if dims is None else pairs[0][2].shape[0]
    np_ = len(pairs)

    def body(*refs):
        o_ref = refs[-1]
        acc = refs[2 * np_][...] if res is not None else None
        for q in range(np_):
            d = _dot(refs[q][...].astype(BF16), refs[np_ + q][...], dims)
            acc = d if acc is None else acc + d
        o_ref[...] = acc.astype(out_dtype)

    in_specs = [_a_spec(a, lead, tm) for a, lead, _ in pairs]
    in_specs += [pl.BlockSpec(b.shape, lambda i: (0, 0)) for _, _, b in pairs]
    args = [a for a, _, _ in pairs] + [b for _, _, b in pairs]
    if res is not None:
        in_specs.append(pl.BlockSpec((tm, n), lambda i: (i, 0)))
        args.append(res)
    return pl.pallas_call(
        body, name=name, grid=(t // tm,), in_specs=in_specs,
        out_specs=pl.BlockSpec((tm, n), lambda i: (i, 0)), out_shape=SDS((t, n), out_dtype),
        compiler_params=_params("parallel"))(*args)


def _mm_normbwd(pairs, x, g, dres, tm, name):
    t, k = x.shape
    np_ = len(pairs)
    b_specs = [pl.BlockSpec((rows, b.shape[1]), lambda i, _b=blk: (_b, 0)) for _, _, b, rows, blk in pairs]
    pairs = [(a, lead, b) for a, lead, b, _, _ in pairs]

    def body(*refs):
        x_ref, g_ref, dres_ref, dx_ref, dg_ref = refs[2 * np_:]
        dh = None
        for q in range(np_):
            d = _dot(refs[q][...], refs[np_ + q][...])
            dh = d if dh is None else dh + d
        xv = x_ref[...]
        r = lax.rsqrt(jnp.mean(xv * xv, axis=-1, keepdims=True) + EPS)
        xh = xv * r

        @pl.when(pl.program_id(0) == 0)
        def _():
            dg_ref[...] = jnp.zeros_like(dg_ref)
        dg_ref[...] += jnp.sum(dh * xh, axis=0, keepdims=True)
        gd = dh * g_ref[...]
        dx_ref[...] = dres_ref[...] + r * (gd - xh * jnp.mean(gd * xh, axis=-1, keepdims=True))

    in_specs = [_a_spec(a, lead, tm) for a, lead, _ in pairs] + b_specs
    in_specs += [pl.BlockSpec((tm, k), lambda i: (i, 0)), pl.BlockSpec((1, k), lambda i: (0, 0)),
                 pl.BlockSpec((tm, k), lambda i: (i, 0))]
    args = [a for a, _, _ in pairs] + [b for _, _, b in pairs] + [x, g, dres]
    return pl.pallas_call(
        body, name=name, grid=(t // tm,), in_specs=in_specs,
        out_specs=[pl.BlockSpec((tm, k), lambda i: (i, 0)), pl.BlockSpec((1, k), lambda i: (0, 0))],
        out_shape=[SDS((t, k), F32), SDS((1, k), F32)],
        compiler_params=_params("arbitrary"))(*args)


def _wgrad(a, a_lead, b, name, tk=2048):
    t, m = a.shape[-2:]
    n = b.shape[1]
    tm = m if m <= 1024 else 1408
    assert m % tm == 0

    def body(a_ref, b_ref, o_ref):
        @pl.when(pl.program_id(1) == 0)
        def _():
            o_ref[...] = jnp.zeros_like(o_ref)
        o_ref[...] += _dot(a_ref[...].astype(BF16), b_ref[...].astype(BF16), TN)

    if a_lead == "all":
        per = m // tm
        return pl.pallas_call(
            body, name=name, grid=(a.shape[0] * per, t // tk),
            in_specs=[pl.BlockSpec((None, tk, tm), lambda mi, ki: (mi // per, ki, mi % per)),
                      pl.BlockSpec((tk, n), lambda mi, ki: (ki, 0))],
            out_specs=pl.BlockSpec((tm, n), lambda mi, ki: (mi, 0)), out_shape=SDS((a.shape[0] * m, n), F32),
            compiler_params=_params("parallel", "arbitrary"))(a, b)
    if a_lead is None:
        a_spec = pl.BlockSpec((tk, tm), lambda mi, ki: (ki, mi))
    else:
        a_spec = pl.BlockSpec((None, tk, tm), lambda mi, ki, _l=a_lead: (_l, ki, mi))
    return pl.pallas_call(
        body, name=name, grid=(m // tm, t // tk),
        in_specs=[a_spec, pl.BlockSpec((tk, n), lambda mi, ki: (ki, 0))],
        out_specs=pl.BlockSpec((tm, n), lambda mi, ki: (mi, 0)), out_shape=SDS((m, n), F32),
        compiler_params=_params("parallel", "arbitrary"))(a, b)


def _head_consts():
    iq = np.arange(ATTN_DIM)
    ik = np.arange(KV_DIM)
    ones_q = (iq[:, None] // HEAD_DIM == iq[None, :] // HEAD_DIM).astype(np.float32)
    ones_k = (ik[:, None] // HEAD_DIM == ik[None, :] // HEAD_DIM).astype(np.float32)
    dup = (ik[:, None] == (HEAD_DIM * (iq[None, :] // 128) + iq[None, :] % HEAD_DIM)).astype(np.float32)
    return jnp.asarray(ones_q, BF16), jnp.asarray(ones_k, BF16), jnp.asarray(dup, BF16), jnp.asarray(dup.T, BF16)


def _attn_prep(proj, gq, gk, ones_q, ones_k, dup, tm=512):
    t = proj.shape[0]

    def body(p_ref, gq_ref, gk_ref, oq_ref, ok_ref, dup_ref, qn_ref, kd_ref, vd_ref):
        q = p_ref[:, 0:ATTN_DIM]
        k = p_ref[:, ATTN_DIM:ATTN_DIM + KV_DIM]
        v = p_ref[:, ATTN_DIM + KV_DIM:]
        rq = lax.rsqrt(_hdot(q * q, oq_ref[...]) * (1.0 / HEAD_DIM) + EPS)
        qn_ref[...] = (q * rq * gq_ref[...]) * (HEAD_DIM ** -0.5)
        rk = lax.rsqrt(_hdot(k * k, ok_ref[...]) * (1.0 / HEAD_DIM) + EPS)
        kn = k * rk * gk_ref[...]
        kd_ref[...] = _dot(kn.astype(BF16), dup_ref[...])
        vd_ref[...] = _dot(v.astype(BF16), dup_ref[...])

    full = lambda a: pl.BlockSpec(a.shape, lambda i: (0, 0))
    o_spec = pl.BlockSpec((tm, ATTN_DIM), lambda i: (i, 0))
    return pl.pallas_call(
        body, name="attn_prep", grid=(t // tm,),
        in_specs=[pl.BlockSpec((tm, 1024), lambda i: (i, 0)), full(gq), full(gk), full(ones_q), full(ones_k), full(dup)],
        out_specs=[o_spec, o_spec, o_spec], out_shape=[SDS((t, ATTN_DIM), F32)] * 3,
        compiler_params=_params("parallel"))(proj, gq, gk, ones_q, ones_k, dup)


def _attn_prep_bwd(proj, dqn, dkc, dkp, dvc, dvp, gq, gk, ones_q, ones_k, dup_t, tm=512):
    t = proj.shape[0]
    nblk = t // tm
    off = SUPER // tm

    def body(p_ref, dqn_ref, dkc_ref, dkp_ref, dvc_ref, dvp_ref, gq_ref, gk_ref, oq_ref, ok_ref, dt_ref,
             o_ref, dgq_ref, dgk_ref):
        i = pl.program_id(0)
        has_next = (i + off < nblk).astype(F32)
        q = p_ref[:, 0:ATTN_DIM]
        k = p_ref[:, ATTN_DIM:ATTN_DIM + KV_DIM]
        dkn = _hdot(dkc_ref[...] + has_next * dkp_ref[...], dt_ref[...])
        dv = _hdot(dvc_ref[...] + has_next * dvp_ref[...], dt_ref[...])

        @pl.when(i == 0)
        def _():
            dgq_ref[...] = jnp.zeros_like(dgq_ref)
            dgk_ref[...] = jnp.zeros_like(dgk_ref)

        rq = lax.rsqrt(_hdot(q * q, oq_ref[...]) * (1.0 / HEAD_DIM) + EPS)
        xh = q * rq
        dy = dqn_ref[...] * (HEAD_DIM ** -0.5)
        dgq_ref[...] += jnp.sum(dy * xh, axis=0, keepdims=True)
        gd = dy * gq_ref[...]
        dq = rq * (gd - xh * (_hdot(gd * xh, oq_ref[...]) * (1.0 / HEAD_DIM)))
        rk = lax.rsqrt(_hdot(k * k, ok_ref[...]) * (1.0 / HEAD_DIM) + EPS)
        kh = k * rk
        dgk_ref[...] += jnp.sum(dkn * kh, axis=0, keepdims=True)
        gdk = dkn * gk_ref[...]
        dk = rk * (gdk - kh * (_hdot(gdk * kh, ok_ref[...]) * (1.0 / HEAD_DIM)))
        o_ref[:, 0:ATTN_DIM] = dq.astype(BF16)
        o_ref[:, ATTN_DIM:ATTN_DIM + KV_DIM] = dk.astype(BF16)
        o_ref[:, ATTN_DIM + KV_DIM:] = dv.astype(BF16)

    full = lambda a: pl.BlockSpec(a.shape, lambda i: (0, 0))
    cur = pl.BlockSpec((tm, ATTN_DIM), lambda i: (i, 0))
    nxt = pl.BlockSpec((tm, ATTN_DIM), lambda i: (jnp.minimum(i + off, nblk - 1), 0))
    return pl.pallas_call(
        body, name="attn_prep_bwd", grid=(nblk,),
        in_specs=[pl.BlockSpec((tm, 1024), lambda i: (i, 0)), cur, cur, nxt, cur, nxt,
                  full(gq), full(gk), full(ones_q), full(ones_k), full(dup_t)],
        out_specs=[pl.BlockSpec((tm, 1024), lambda i: (i, 0)), pl.BlockSpec((1, ATTN_DIM), lambda i: (0, 0)),
                   pl.BlockSpec((1, KV_DIM), lambda i: (0, 0))],
        out_shape=[SDS((t, 1024), BF16), SDS((1, ATTN_DIM), F32), SDS((1, KV_DIM), F32)],
        compiler_params=_params("arbitrary"))(proj, dqn, dkc, dkp, dvc, dvp, gq, gk, ones_q, ones_k, dup_t)


def _tile_masks():
    qi = lax.broadcasted_iota(jnp.int32, (2 * CHUNK, 2 * CHUNK), 0) & (CHUNK - 1)
    kj = lax.broadcasted_iota(jnp.int32, (2 * CHUNK, 2 * CHUNK), 1)
    delta = CHUNK + qi - kj
    band = (delta >= 0) & (delta <= CHUNK)
    return band, kj


def _deinterleave(dst, src, n_rows, d):
    per = n_rows // d
    for r in range(d):
        dst[r * per:(r + 1) * per, :] = src[pl.ds(r, per, stride=d), :]


def _attn_specs(t):
    blk = lambda f: pl.BlockSpec((SUPER, 128), f)
    cur = blk(lambda h, s: (s, h))
    prev = blk(lambda h, s: (jnp.maximum(s - 1, 0), h))
    return cur, prev


def _attn_fwd(qn, kd, vd):
    t = qn.shape[0]
    cur, prev = _attn_specs(t)

    def body(q_ref, kp_ref, kc_ref, vp_ref, vc_ref, o_ref, lse_ref, kk, vv, qd, kdd, vdd, po, pm, pll, acc, mm, ll):
        s = pl.program_id(1)
        kk[0:SUPER, :] = kp_ref[...]
        kk[SUPER:, :] = kc_ref[...]
        vv[0:SUPER, :] = vp_ref[...]
        vv[SUPER:, :] = vc_ref[...]
        m0 = lax.broadcasted_iota(jnp.int32, (CHUNK, 128), 1) < HEAD_DIM
        band, kj = _tile_masks()
        for d in DILATIONS:
            lq = SUPER // d
            if d == 1:
                qs_ref, ks_ref, vs_ref = q_ref, kk, vv
            else:
                _deinterleave(qd, q_ref, SUPER, d)
                _deinterleave(kdd, kk, 2 * SUPER, d)
                _deinterleave(vdd, vv, 2 * SUPER, d)
                qs_ref, ks_ref, vs_ref = qd, kdd, vdd

            nblk = lq // CHUNK

            def key_rows(ti):
                return pl.ds((ti // nblk) * 2 * lq + lq + (ti % nblk - 1) * CHUNK, 2 * CHUNK)

            def scores(ti):
                qt = qs_ref[pl.ds(ti * CHUNK, CHUNK), :]
                qs = jnp.concatenate([jnp.where(m0, qt, 0.0), jnp.where(m0, 0.0, qt)], axis=0).astype(BF16)
                return _dot(qs, ks_ref[key_rows(ti), :].astype(BF16), NT)

            def softmax_pv(ti, sc):
                ok = band if ti % nblk > 0 else band & (kj >= jnp.where(s > 0, 0, CHUNK))
                sc = jnp.where(ok, sc, -jnp.inf)
                mt = jnp.max(sc, axis=-1, keepdims=True)
                p = jnp.exp(sc - mt)
                lt = jnp.sum(p, axis=-1, keepdims=True)
                ot = _dot(p.astype(BF16), vs_ref[key_rows(ti), :].astype(BF16))
                qrows = pl.ds(ti * CHUNK, CHUNK)
                po[qrows, :] = jnp.where(m0, ot[:CHUNK], ot[CHUNK:])
                pm[qrows, :] = jnp.where(m0, mt[:CHUNK], mt[CHUNK:])
                pll[qrows, :] = jnp.where(m0, lt[:CHUNK], lt[CHUNK:])

            for ti in range(SUPER // CHUNK):
                softmax_pv(ti, scores(ti))
            if d == 1:
                acc[...] = po[...]
                mm[...] = pm[...]
                ll[...] = pll[...]
            else:
                for r in range(d):
                    rows = pl.ds(r, lq, stride=d)
                    seg = slice(r * lq, (r + 1) * lq)
                    m_old, m_new = mm[rows, :], pm[seg, :]
                    m_all = jnp.maximum(m_old, m_new)
                    a, b = jnp.exp(m_old - m_all), jnp.exp(m_new - m_all)
                    acc[rows, :] = acc[rows, :] * a + po[seg, :] * b
                    ll[rows, :] = ll[rows, :] * a + pll[seg, :] * b
                    mm[rows, :] = m_all
        o_ref[...] = acc[...] / ll[...]
        lse_ref[...] = mm[...] + jnp.log(ll[...])

    big = pltpu.VMEM((2 * SUPER, 128), F32)
    one = pltpu.VMEM((SUPER, 128), F32)
    return pl.pallas_call(
        body, name="attn_fwd", grid=(4, t // SUPER),
        in_specs=[cur, prev, cur, prev, cur], out_specs=[cur, cur],
        out_shape=[SDS((t, ATTN_DIM), F32)] * 2,
        scratch_shapes=[big, big, one, big, big, one, one, one, one, one, one],
        compiler_params=_params("parallel", "arbitrary"))(qn, kd, kd, vd, vd)


def _attn_bwd(qn, kd, vd, out, lse, dout, ones_pair):
    t = qn.shape[0]
    cur, prev = _attn_specs(t)

    def body(q_ref, kp_ref, kc_ref, vp_ref, vc_ref, o_ref, lse_ref, do_ref, ones_ref,
             dq_ref, dkc_ref, dkp_ref, dvc_ref, dvp_ref,
             kk, vv, od, ld, kb, vb, qsb, dosb, tk, tv, pdq, delta):
        s = pl.program_id(1)
        delta[...] = _hdot(do_ref[...] * o_ref[...], ones_ref[...])

        def per_row(a):
            ar = pltpu.roll(a, HEAD_DIM, 1)
            rows = jnp.concatenate([jnp.where(m0, a, ar), jnp.where(m0, ar, a)], axis=0)
            return jnp.concatenate([rows, rows], axis=1)

        kk[0:SUPER, :] = kp_ref[...]
        kk[SUPER:, :] = kc_ref[...]
        vv[0:SUPER, :] = vp_ref[...]
        vv[SUPER:, :] = vc_ref[...]
        for ref in (dq_ref, dkc_ref, dkp_ref, dvc_ref, dvp_ref):
            ref[...] = jnp.zeros_like(ref)
        m0 = lax.broadcasted_iota(jnp.int32, (CHUNK, 128), 1) < HEAD_DIM
        band, kj = _tile_masks()
        ninf = -jnp.inf
        for d in DILATIONS:
            lq = SUPER // d
            nblk = lq // CHUNK
            for r in range(d):
                seg = slice(r * 2 * lq, (r + 1) * 2 * lq)
                kb[seg, :] = kk[pl.ds(r, 2 * lq, stride=d), :].astype(BF16)
                vb[seg, :] = vv[pl.ds(r, 2 * lq, stride=d), :].astype(BF16)
            for ti in range(SUPER // CHUNK):
                rows = pl.ds(ti // nblk + d * CHUNK * (ti % nblk), CHUNK, stride=d)
                for src, dst in ((q_ref, qsb), (do_ref, dosb)):
                    a = src[rows, :]
                    dst[ti * 2 * CHUNK:(ti + 1) * 2 * CHUNK, :] = jnp.concatenate(
                        [jnp.where(m0, a, 0.0), jnp.where(m0, 0.0, a)], axis=0).astype(BF16)
                ld[ti * CHUNK:(ti + 1) * CHUNK, :] = lse_ref[rows, :]
                od[ti * CHUNK:(ti + 1) * CHUNK, :] = delta[rows, :]

            def operands(ti):
                r, nb = ti // nblk, ti % nblk
                stacked = slice(ti * 2 * CHUNK, (ti + 1) * 2 * CHUNK)
                krows = pl.ds(r * 2 * lq + lq + (nb - 1) * CHUNK, 2 * CHUNK)
                return stacked, krows

            def scores(ti):
                stacked, krows = operands(ti)
                kt = kb[krows, :]
                return dict(ti=ti, sc=_dot(qsb[stacked, :], kt, NT), dp=_dot(dosb[stacked, :], vb[krows, :], NT))

            def softmax_grad(c):
                qrows = slice(c["ti"] * CHUNK, (c["ti"] + 1) * CHUNK)
                ok = band if c["ti"] % nblk > 0 else band & (kj >= jnp.where(s > 0, 0, CHUNK))
                p = jnp.exp(jnp.where(ok, c.pop("sc"), ninf) - per_row(ld[qrows, :]))
                ds = p * (c.pop("dp") - per_row(od[qrows, :]))
                c.update(p=p.astype(BF16), ds=ds.astype(BF16))
                return c

            def grads(c):
                ti = c["ti"]
                stacked, krows = operands(ti)
                dqs = _dot(c["ds"], kb[krows, :])
                pdq[ti * CHUNK:(ti + 1) * CHUNK, :] = jnp.where(m0, dqs[:CHUNK], dqs[CHUNK:])
                tk[stacked, :] = _dot(c["ds"], qsb[stacked, :], TN)
                tv[stacked, :] = _dot(c["p"], dosb[stacked, :], TN)

            n_tiles = SUPER // CHUNK
            stage_a = scores(0)
            for ti in range(n_tiles):
                ahead = scores(ti + 1) if ti + 1 < n_tiles else None
                grads(softmax_grad(stage_a))
                stage_a = ahead

            for r in range(d):
                dq_ref[pl.ds(r, lq, stride=d), :] += pdq[r * lq:(r + 1) * lq, :]
                for tile_out, cur_ref, prev_ref in ((tk, dkc_ref, dkp_ref), (tv, dvc_ref, dvp_ref)):
                    first = r * nblk * 2 * CHUNK
                    prev_ref[pl.ds(SUPER - CHUNK * d + r, CHUNK, stride=d), :] += tile_out[first:first + CHUNK, :]
                    for nb in range(nblk):
                        at = (r * nblk + nb) * 2 * CHUNK
                        part = tile_out[at + CHUNK:at + 2 * CHUNK, :]
                        if nb + 1 < nblk:
                            part = part + tile_out[at + 2 * CHUNK:at + 3 * CHUNK, :]
                        cur_ref[pl.ds(r + d * nb * CHUNK, CHUNK, stride=d), :] += part

    big = pltpu.VMEM((2 * SUPER, 128), F32)
    one = pltpu.VMEM((SUPER, 128), F32)
    half = pltpu.VMEM((2 * SUPER, 128), BF16)
    return pl.pallas_call(
        body, name="attn_bwd", grid=(4, t // SUPER),
        in_specs=[cur, prev, cur, prev, cur, cur, cur, cur, pl.BlockSpec((128, 128), lambda h, s: (0, 0))],
        out_specs=[cur] * 5, out_shape=[SDS((t, ATTN_DIM), F32)] * 5,
        scratch_shapes=[big, big, one, one, half, half, half, half, big, big, one, one],
        compiler_params=_params("parallel", "arbitrary"))(qn, kd, kd, vd, vd, out, lse, dout, ones_pair)


def _ssd_consts():
    tri = np.tril(np.ones((CHUNK, CHUNK), np.float32))
    expand = np.zeros((128, SSM_INNER), np.float32)
    for h in range(SSM_HEADS):
        expand[h, h * HEAD_DIM:(h + 1) * HEAD_DIM] = 1.0
    return jnp.asarray(tri, BF16), jnp.asarray(tri.T, BF16), jnp.asarray(expand, BF16), jnp.asarray(expand.T, BF16)


def _conv4(x, halo, w_ref, b_ref):
    acc = b_ref[...] + w_ref[3:4, :] * x
    for k in range(3):
        acc = acc + w_ref[k:k + 1, :] * _shift_down(x, halo, 3 - k)
    return acc


def _softplus(x):
    return jnp.maximum(x, 0.0) + jnp.log(1.0 + jnp.exp(-jnp.abs(x)))


def _ssd_common(pre_x, pre_b, dt_ref, dtb_ref, alog_ref, tri_ref, exp_ref):
    xa = pre_x * _sigmoid(pre_x)
    ba = pre_b * _sigmoid(pre_b)
    dtv = _softplus(dt_ref[...] + dtb_ref[...])
    a_neg = -jnp.exp(alog_ref[...])
    acum = _hdot(tri_ref[...], dtv * a_neg, parts=3)
    lam = jnp.exp(acum)
    gam = jnp.exp(acum[CHUNK - 1:CHUNK, :] - acum)
    dt_e = _hdot(dtv, exp_ref[...])
    lam_e = _hdot(lam, exp_ref[...])
    gam_e = _hdot(gam, exp_ref[...])
    return dict(pre_x=pre_x, pre_b=pre_b, xa=xa, ba=ba, dtv=dtv, a_neg=a_neg, acum=acum,
                dt_e=dt_e, lam_e=lam_e, gam_e=gam_e, xdt=xa * dt_e)


def _decay(acum_t, h, transposed):
    rb = jnp.broadcast_to(acum_t[h:h + 1, :], (CHUNK, CHUNK))
    ri = lax.broadcasted_iota(jnp.int32, (CHUNK, CHUNK), 0)
    ci = lax.broadcasted_iota(jnp.int32, (CHUNK, CHUNK), 1)
    if transposed:
        return jnp.exp(jnp.where(ci >= ri, rb - rb.T, -jnp.inf))
    return jnp.exp(jnp.where(ri >= ci, rb.T - rb, -jnp.inf))


def _ssd_specs(t, rev):
    nc = t // CHUNK
    ch = (lambda c: nc - 1 - c) if rev else (lambda c: c)
    col = lambda w, j: pl.BlockSpec((CHUNK, w), lambda c: (ch(c), j))
    halo = lambda w, j: pl.BlockSpec((8, w), lambda c: (jnp.maximum(ch(c) * (CHUNK // 8) - 1, 0), j))
    return nc, ch, col, halo


def _ssd_fwd(proj, cwx, cbx, cwb, cbb, dtb, alog, dsk_e, norm_g, tri, expand):
    t = proj.shape[0]
    nc, _, col, halo = _ssd_specs(t, False)

    def body(z_ref, xs_ref, bc_ref, dt_ref, hx_ref, hb_ref, cwx_ref, cbx_ref, cwb_ref, cbb_ref, dtb_ref, alog_ref,
             dsk_ref, g_ref, tri_ref, exp_ref, y_ref, hs_ref, o_ref, px_ref, pb_ref, state):
        c = pl.program_id(0)

        @pl.when(c == 0)
        def _():
            state[...] = jnp.zeros_like(state)

        keep = (c > 0).astype(F32)
        pre_x = _conv4(xs_ref[...], hx_ref[...] * keep, cwx_ref, cbx_ref)
        pre_b = _conv4(bc_ref[...], hb_ref[...] * keep, cwb_ref, cbb_ref)
        px_ref[...] = pre_x.astype(BF16)
        pb_ref[...] = pre_b.astype(BF16)
        v = _ssd_common(pre_x, pre_b, dt_ref, dtb_ref, alog_ref, tri_ref, exp_ref)
        acum_t = v["acum"].T
        xdt, ba = v["xdt"], v["ba"]
        h_in = state[...]
        hs_ref[0] = h_in
        xg = xdt * v["gam_e"]
        m0 = lax.broadcasted_iota(jnp.int32, (CHUNK, 128), 1) < HEAD_DIM
        for g in range(2):
            bg = ba[:, g * 128:(g + 1) * 128].astype(BF16)
            cg = ba[:, 256 + g * 128:256 + (g + 1) * 128].astype(BF16)
            gl = slice(g * 512, (g + 1) * 512)
            cb = _dot(cg, bg, NT)
            y_off = _dot(cg, h_in[:, gl].astype(BF16)) * v["lam_e"][:, gl]
            s_new = _dot(bg.T, xg[:, gl].astype(BF16))
            state[:, gl] = h_in[:, gl] * v["lam_e"][CHUNK - 1:CHUNK, gl] + s_new
            for j in range(4):
                h0 = 8 * g + 2 * j
                ln = slice(g * 512 + j * 128, g * 512 + (j + 1) * 128)
                xp = xdt[:, ln].astype(BF16)
                y0 = _dot((cb * _decay(acum_t, h0, False)).astype(BF16), xp)
                y1 = _dot((cb * _decay(acum_t, h0 + 1, False)).astype(BF16), xp)
                y_ref[:, ln] = jnp.where(m0, y0, y1) + y_off[:, j * 128:(j + 1) * 128]
        z = z_ref[...]
        yg = (y_ref[...] + dsk_ref[...] * v["xa"]) * (z * _sigmoid(z))
        r = lax.rsqrt(jnp.mean(yg * yg, axis=-1, keepdims=True) + EPS)
        o_ref[...] = (yg * r * g_ref[...]).astype(BF16)

    full = lambda a: pl.BlockSpec(a.shape, lambda c: (0,) * a.ndim)
    return pl.pallas_call(
        body, name="ssd_fwd", grid=(nc,),
        in_specs=[col(1024, 1), col(1024, 2), col(512, 6), col(128, 28), halo(1024, 2), halo(512, 6),
                  full(cwx), full(cbx), full(cwb), full(cbb), full(dtb), full(alog), full(dsk_e), full(norm_g),
                  full(tri), full(expand)],
        out_specs=[pl.BlockSpec((CHUNK, SSM_INNER), lambda c: (c, 0)),
                   pl.BlockSpec((1, 128, SSM_INNER), lambda c: (c, 0, 0)),
                   pl.BlockSpec((CHUNK, SSM_INNER), lambda c: (c, 0)),
                   pl.BlockSpec((CHUNK, SSM_INNER), lambda c: (c, 0)), pl.BlockSpec((CHUNK, 512), lambda c: (c, 0))],
        out_shape=[SDS((t, SSM_INNER), F32), SDS((nc, 128, SSM_INNER), F32), SDS((t, SSM_INNER), BF16),
                   SDS((t, SSM_INNER), BF16), SDS((t, 512), BF16)],
        scratch_shapes=[pltpu.VMEM((128, SSM_INNER), F32)],
        compiler_params=_params("arbitrary"))(proj, proj, proj, proj, proj, proj, cwx, cbx, cwb, cbb, dtb, alog,
                                              dsk_e, norm_g, tri, expand)


def _ssd_bwd(proj, pre_x, pre_b, y_ssd, hs, dout, cwx, cwb, dtb, alog, dsk_e, norm_g, tri, triu, expand, expand_t):
    t = proj.shape[0]
    nc, ch, col, halo = _ssd_specs(t, True)

    def body(z_ref, xs_ref, bc_ref, dt_ref, px_ref, pb_ref, y_ref, hin_ref, do_ref,
             cwx_ref, cwb_ref, dtb_ref, alog_ref, dsk_ref, g_ref, tri_ref, triu_ref, exp_ref, expt_ref,
             dz_ref, dxs_ref, dbc_ref, ddt_ref, dg_ref, ddsk_ref, dalog_ref, ddtb_ref, dcwx_ref, dcbx_ref, dcwb_ref,
             dcbb_ref, gstate, nx_x, nx_b, dact_b, dxdt_s):
        step = pl.program_id(0)
        c = nc - 1 - step

        @pl.when(step == 0)
        def _():
            gstate[...] = jnp.zeros_like(gstate)
            nx_x[...] = jnp.zeros_like(nx_x)
            nx_b[...] = jnp.zeros_like(nx_b)
            for ref in (dg_ref, ddsk_ref, dalog_ref, ddtb_ref, dcwx_ref, dcbx_ref, dcwb_ref, dcbb_ref):
                ref[...] = jnp.zeros_like(ref)

        v = _ssd_common(px_ref[...].astype(F32), pb_ref[...].astype(F32), dt_ref, dtb_ref, alog_ref, tri_ref, exp_ref)
        acum_t = v["acum"].T
        xa, ba, xdt, dtv = v["xa"], v["ba"], v["xdt"], v["dtv"]
        lam_e, gam_e, dt_e = v["lam_e"], v["gam_e"], v["dt_e"]
        z = z_ref[...]
        y = y_ref[...]
        sz = _sigmoid(z)
        zs = z * sz
        y_tot = y + dsk_ref[...] * xa
        yg = y_tot * zs
        r = lax.rsqrt(jnp.mean(yg * yg, axis=-1, keepdims=True) + EPS)
        yh = yg * r
        do = do_ref[...]
        dg_ref[...] += jnp.sum(do * yh, axis=0, keepdims=True)
        gd = do * g_ref[...]
        dyg = r * (gd - yh * jnp.mean(gd * yh, axis=-1, keepdims=True))
        dz_ref[...] = (dyg * y_tot * (sz * (1.0 + z * (1.0 - sz)))).astype(BF16)
        dy = dyg * zs
        ddsk_ref[...] += jnp.sum(dy * xa, axis=0, keepdims=True)
        g_out = gstate[...]
        h_in = hin_ref[0]
        lam_dy = lam_e * dy
        gam_x = gam_e * xdt
        m0 = lax.broadcasted_iota(jnp.int32, (CHUNK, 128), 1) < HEAD_DIM
        lane = lax.broadcasted_iota(jnp.int32, (CHUNK, 128), 1)
        below = (lax.broadcasted_iota(jnp.int32, (CHUNK, CHUNK), 0) >
                 lax.broadcasted_iota(jnp.int32, (CHUNK, CHUNK), 1))
        da_in = jnp.zeros((CHUNK, 128), F32)
        off_y, off_x = [], []
        for g in range(2):
            bg = ba[:, g * 128:(g + 1) * 128].astype(BF16)
            cg = ba[:, 256 + g * 128:256 + (g + 1) * 128].astype(BF16)
            gl = slice(g * 512, (g + 1) * 512)
            gg = g_out[:, gl].astype(BF16)
            bc_t = _dot(bg, cg, NT)
            cb = _dot(cg, bg, NT)
            dxdt_off = _dot(bg, gg) * gam_e[:, gl]
            off_x.append(xdt[:, gl] * dxdt_off)
            off_y.append(dy[:, gl] * (_dot(cg, h_in[:, gl].astype(BF16)) * lam_e[:, gl]))
            q_sum = jnp.zeros((CHUNK, CHUNK), F32)
            for j in range(4):
                h0 = 8 * g + 2 * j
                ln = slice(g * 512 + j * 128, g * 512 + (j + 1) * 128)
                dyp = dy[:, ln]
                dyb = dyp.astype(BF16)
                xpb = xdt[:, ln].astype(BF16)
                d0 = _dot((bc_t * _decay(acum_t, h0, True)).astype(BF16), dyb)
                d1 = _dot((bc_t * _decay(acum_t, h0 + 1, True)).astype(BF16), dyb)
                dxdt_s[:, ln] = jnp.where(m0, d0, d1) + dxdt_off[:, j * 128:(j + 1) * 128]
                for hh, dym in ((h0, jnp.where(m0, dyp, 0.0)), (h0 + 1, jnp.where(m0, 0.0, dyp))):
                    qd = _dot(dym.astype(BF16), xpb, NT) * _decay(acum_t, hh, False)
                    q_sum = q_sum + qd
                    reach = jnp.where(below, _hdot(triu_ref[...], qd * cb), 0.0)
                    da_in = jnp.where(lane == hh, jnp.sum(reach, axis=-1, keepdims=True), da_in)
            gstate[:, gl] = g_out[:, gl] * lam_e[CHUNK - 1:CHUNK, gl] + _dot(cg.T, lam_dy[:, gl].astype(BF16))
            qb = q_sum.astype(BF16)
            dact_b[:, 256 + g * 128:256 + (g + 1) * 128] = (
                _dot(qb, bg) + _dot(lam_dy[:, gl].astype(BF16), h_in[:, gl].astype(BF16), NT))
            dact_b[:, g * 128:(g + 1) * 128] = _dot(qb.T, cg) + _dot(gam_x[:, gl].astype(BF16), gg, NT)
        dxdt = dxdt_s[...]
        seg_y = _hdot(jnp.concatenate(off_y, axis=1), expt_ref[...])
        seg_x = _hdot(jnp.concatenate(off_x, axis=1), expt_ref[...])
        e_col = jnp.sum(g_out * h_in * lam_e[CHUNK - 1:CHUNK, :], axis=0, keepdims=True)
        e_seg = _hdot(jnp.broadcast_to(e_col, (8, SSM_INNER)), expt_ref[...])[0:1, :]
        da = da_in + _hdot(triu_ref[...], seg_y) + (_hdot(tri_ref[...], seg_x) - seg_x) + e_seg
        a_neg = v["a_neg"]
        ddtv = da * a_neg + _hdot(dxdt * xa, expt_ref[...])
        dalog_ref[...] += jnp.sum(da * dtv, axis=0, keepdims=True) * a_neg
        lane16 = lax.broadcasted_iota(jnp.int32, (CHUNK, 128), 1) < SSM_HEADS
        draw = jnp.where(lane16, ddtv * _sigmoid(dt_ref[...] + dtb_ref[...]), 0.0)
        ddtb_ref[...] += jnp.sum(draw, axis=0, keepdims=True)
        ddt_ref[...] = draw.astype(BF16)
        dxa = dxdt * dt_e + dy * dsk_ref[...]
        for (dact, pre, x_ref, nx, cw_ref, dcw_ref, dcb_ref, dx_ref) in (
                (dxa, v["pre_x"], xs_ref, nx_x, cwx_ref, dcwx_ref, dcbx_ref, dxs_ref),
                (dact_b[...], v["pre_b"], bc_ref, nx_b, cwb_ref, dcwb_ref, dcbb_ref, dbc_ref)):
            sp = _sigmoid(pre)
            dpre = dact * (sp * (1.0 + pre * (1.0 - sp)))
            dcb_ref[...] += jnp.sum(dpre, axis=0, keepdims=True)
            xv = x_ref[...]
            nxt = nx[...]
            dx = cw_ref[3:4, :] * dpre
            dcw_ref[3:4, :] += jnp.sum(dpre * xv, axis=0, keepdims=True)
            for k in range(3):
                d_up = _shift_up(dpre, nxt, 3 - k)
                dcw_ref[k:k + 1, :] += jnp.sum(xv * d_up, axis=0, keepdims=True)
                dx = dx + cw_ref[k:k + 1, :] * d_up
            nx[...] = dpre[0:8, :]
            dx_ref[...] = dx.astype(dx_ref.dtype)

    full = lambda a: pl.BlockSpec(a.shape, lambda c: (0,) * a.ndim)
    rowblk = lambda w: pl.BlockSpec((CHUNK, w), lambda c: (ch(c), 0))
    acc = lambda a, b: pl.BlockSpec((a, b), lambda c: (0, 0))
    return pl.pallas_call(
        body, name="ssd_bwd", grid=(nc,),
        in_specs=[col(1024, 1), col(1024, 2), col(512, 6), col(128, 28), rowblk(SSM_INNER), rowblk(512),
                  rowblk(SSM_INNER),
                  pl.BlockSpec((1, 128, SSM_INNER), lambda c: (ch(c), 0, 0)),
                  rowblk(SSM_INNER),
                  full(cwx), full(cwb), full(dtb), full(alog), full(dsk_e), full(norm_g),
                  full(tri), full(triu), full(expand), full(expand_t)],
        out_specs=[rowblk(SSM_INNER), rowblk(SSM_INNER), rowblk(512), rowblk(128),
                   acc(1, 1024), acc(1, 1024), acc(1, 128), acc(1, 128), acc(4, 1024), acc(1, 1024), acc(4, 512),
                   acc(1, 512)],
        out_shape=[SDS((t, SSM_INNER), BF16), SDS((t, SSM_INNER), BF16), SDS((t, 512), BF16), SDS((t, 128), BF16),
                   SDS((1, 1024), F32), SDS((1, 1024), F32), SDS((1, 128), F32), SDS((1, 128), F32),
                   SDS((4, 1024), F32), SDS((1, 1024), F32), SDS((4, 512), F32), SDS((1, 512), F32)],
        scratch_shapes=[pltpu.VMEM((128, SSM_INNER), F32), pltpu.VMEM((8, 1024), F32), pltpu.VMEM((8, 512), F32),
                        pltpu.VMEM((CHUNK, 512), F32), pltpu.VMEM((CHUNK, SSM_INNER), F32)],
        compiler_params=_params("arbitrary"))(proj, proj, proj, proj, pre_x, pre_b, y_ssd, hs, dout,
                                              cwx, cwb, dtb, alog, dsk_e, norm_g, tri, triu, expand, expand_t)


def _conv3(x, halo, w_ref, b_ref, part):
    acc = b_ref[part] + w_ref[2, part] * x
    for k in range(2):
        acc = acc + w_ref[k, part] * _shift_down(x, halo, 2 - k)
    return acc


def _up_act(x, g, w_up_t, cw, cb, tm=2048, tn=256):
    t, k = x.shape
    nj = D_FF // tn

    def body(x_ref, g_ref, wg_ref, wv_ref, w_ref, b_ref, u_ref, c_ref, h_ref, f_ref, halo):
        i, j = pl.program_id(0), pl.program_id(1)

        @pl.when(j == 0)
        def _():
            xv = x_ref[...]
            r = lax.rsqrt(jnp.mean(xv * xv, axis=-1, keepdims=True) + EPS)
            h_ref[...] = (xv * r * g_ref[...]).astype(BF16)

        @pl.when(i == 0)
        def _():
            halo[j] = jnp.zeros((2, 8, tn), F32)

        us = [_dot(h_ref[...], wt_ref[...], NT) for wt_ref in (wg_ref, wv_ref)]
        parts = []
        for part, u in enumerate(us):
            u_ref[part] = u.astype(BF16)
            parts.append(_conv3(u, halo[j, part], w_ref, b_ref, part))
            c_ref[part] = parts[-1].astype(BF16)
            halo[j, part] = u[tm - 8:, :]
        gate, val = parts
        f_ref[...] = (gate * _sigmoid(gate) * val).astype(BF16)

    return pl.pallas_call(
        body, name="up_proj", grid=(t // tm, nj),
        in_specs=[pl.BlockSpec((tm, k), lambda i, j: (i, 0)), pl.BlockSpec((1, k), lambda i, j: (0, 0)),
                  pl.BlockSpec((tn, k), lambda i, j: (j, 0)), pl.BlockSpec((tn, k), lambda i, j: (j + nj, 0)),
                  pl.BlockSpec((3, 2, 1, tn), lambda i, j: (0, 0, 0, j)), pl.BlockSpec((2, 1, tn), lambda i, j: (0, 0, j))],
        out_specs=[pl.BlockSpec((2, tm, tn), lambda i, j: (0, i, j)), pl.BlockSpec((2, tm, tn), lambda i, j: (0, i, j)),
                   pl.BlockSpec((tm, k), lambda i, j: (i, 0)), pl.BlockSpec((tm, tn), lambda i, j: (i, j))],
        out_shape=[SDS((2, t, D_FF), BF16), SDS((2, t, D_FF), BF16), SDS((t, k), BF16), SDS((t, D_FF), BF16)],
        scratch_shapes=[pltpu.VMEM((nj, 2, 8, tn), F32)],
        compiler_params=_params("arbitrary", "arbitrary"))(x, g, w_up_t, w_up_t, cw, cb)


def _ffn_bwd(dx2, w_down, u, c, cw, tm=512, tn=1408):
    t = u.shape[1]
    nj, ni = D_FF // tn, t // tm
    rev = lambda i: ni - 1 - i

    def body(dx_ref, wd_ref, u_ref, c_ref, w_ref, du_ref, dcw_ref, dcb_ref, nxt):
        i = pl.program_id(1)

        @pl.when(i == 0)
        def _():
            nxt[...] = jnp.zeros_like(nxt)
            dcw_ref[...] = jnp.zeros_like(dcw_ref)
            dcb_ref[...] = jnp.zeros_like(dcb_ref)

        df = _dot(dx_ref[...].astype(BF16), wd_ref[...], NT)
        gate, val = c_ref[0].astype(F32), c_ref[1].astype(F32)
        sg = _sigmoid(gate)
        dgate = df * val * (sg * (1.0 + gate * (1.0 - sg)))
        dval = df * (gate * sg)
        for part, d in enumerate((dgate, dval)):
            uu = u_ref[part].astype(F32)
            dcb_ref[part] += jnp.sum(d, axis=0, keepdims=True)
            ahead = nxt[part]
            acc = w_ref[2, part] * d
            dcw_ref[2, part] += jnp.sum(d * uu, axis=0, keepdims=True)
            for k in range(2):
                d_up = _shift_up(d, ahead, 2 - k)
                dcw_ref[k, part] += jnp.sum(uu * d_up, axis=0, keepdims=True)
                acc = acc + w_ref[k, part] * d_up
            nxt[part] = d[0:8, :]
            du_ref[part] = acc.astype(BF16)

    w_spec = pl.BlockSpec((3, 2, 1, tn), lambda j, i: (0, 0, 0, j))
    b_spec = pl.BlockSpec((2, 1, tn), lambda j, i: (0, 0, j))
    tile = pl.BlockSpec((2, tm, tn), lambda j, i: (0, rev(i), j))
    return pl.pallas_call(
        body, name="ffn_bwd", grid=(nj, ni),
        in_specs=[pl.BlockSpec((tm, D_MODEL), lambda j, i: (rev(i), 0)), pl.BlockSpec((tn, D_MODEL), lambda j, i: (j, 0)),
                  tile, tile, w_spec],
        out_specs=[tile, w_spec, b_spec],
        out_shape=[SDS((2, t, D_FF), BF16), SDS((3, 2, 1, D_FF), F32), SDS((2, 1, D_FF), F32)],
        scratch_shapes=[pltpu.VMEM((2, 8, tn), F32)],
        compiler_params=_params("parallel", "arbitrary"))(dx2, w_down, u, c, cw)


def _ple_loss(x2, g, w_gate, p, w_proj_t, target, tm=512):
    t = x2.shape[0]

    def body(x_ref, g_ref, wg_ref, p_ref, wp_ref, tg_ref, dx_ref, dpre_ref, dpp_ref, h_ref, loss_ref, dg_ref):
        i = pl.program_id(0)
        xv = x_ref[...]
        r = lax.rsqrt(jnp.mean(xv * xv, axis=-1, keepdims=True) + EPS)
        xh = xv * r
        h = (xh * g_ref[...]).astype(BF16)
        h_ref[...] = h
        gate = _sigmoid(_dot(h, wg_ref[...]))
        pp = _dot(p_ref[...].astype(BF16), wp_ref[...], NT)
        err = (xv + gate * pp) - tg_ref[...]

        @pl.when(i == 0)
        def _():
            loss_ref[...] = jnp.zeros_like(loss_ref)
            dg_ref[...] = jnp.zeros_like(dg_ref)

        loss_ref[...] += 0.5 * jnp.sum(jnp.mean(err * err, axis=-1, keepdims=True), axis=0, keepdims=True)
        dy = err * (1.0 / D_MODEL)
        dpre = (dy * pp * gate * (1.0 - gate)).astype(BF16)
        dpre_ref[...] = dpre
        dpp_ref[...] = (dy * gate).astype(BF16)
        dh = _dot(dpre, wg_ref[...], NT)
        dg_ref[...] += jnp.sum(dh * xh, axis=0, keepdims=True)
        gd = dh * g_ref[...]
        dx_ref[...] = dy + r * (gd - xh * jnp.mean(gd * xh, axis=-1, keepdims=True))

    row = lambda w: pl.BlockSpec((tm, w), lambda i: (i, 0))
    full = lambda a: pl.BlockSpec(a.shape, lambda i: (0, 0))
    return pl.pallas_call(
        body, name="ple_loss", grid=(t // tm,),
        in_specs=[row(D_MODEL), full(g), full(w_gate), row(PLE_DIM), full(w_proj_t), row(D_MODEL)],
        out_specs=[row(D_MODEL), row(D_MODEL), row(D_MODEL), row(D_MODEL),
                   pl.BlockSpec((1, 128), lambda i: (0, 0)), pl.BlockSpec((1, D_MODEL), lambda i: (0, 0))],
        out_shape=[SDS((t, D_MODEL), F32), SDS((t, D_MODEL), BF16), SDS((t, D_MODEL), BF16), SDS((t, D_MODEL), BF16),
                   SDS((1, 128), F32), SDS((1, D_MODEL), F32)],
        compiler_params=_params("arbitrary"))(x2, g, w_gate, p, w_proj_t, target)


def _all_gather(arrays, name):
    n_a = len(arrays)

    def body(*refs):
        src, dst = refs[:n_a], refs[n_a:2 * n_a]
        send_sems, recv_sems, local_sems = refs[2 * n_a:]
        x, y, c = lax.axis_index("x"), lax.axis_index("y"), lax.axis_index("c")
        slot = lambda px, py, pc: 4 * px + 2 * py + pc
        me, sibling = (x, y, c), (x, y, 1 - c)
        chips = [(1 - x, y), (x, 1 - y), (1 - x, 1 - y)]

        def copy(a, k, block, to, own=False):
            return pltpu.make_async_remote_copy(
                src_ref=src[a] if own else dst[a].at[slot(*block)], dst_ref=dst[a].at[slot(*block)],
                send_sem=send_sems.at[a, k], recv_sem=recv_sems.at[a, k], device_id=to,
                device_id_type=pl.DeviceIdType.MESH)

        local = [pltpu.make_async_copy(src[a], dst[a].at[slot(*me)], local_sems.at[a]) for a in range(n_a)]
        for cp in local:
            cp.start()
        sends = []
        for a in range(n_a):
            sends.append(copy(a, 0, me, sibling, own=True))
            sends += [copy(a, 1 + j, me, (*chip, c), own=True) for j, chip in enumerate(chips)]
        for cp in sends:
            cp.start()
        for j, chip in enumerate(chips):
            for a in range(n_a):
                copy(a, 1 + j, (*chip, c), me).wait_recv()
                passed = copy(a, 4 + j, (*chip, c), sibling)
                passed.start()
                sends.append(passed)
        for a in range(n_a):
            copy(a, 0, sibling, me).wait_recv()
            for j, chip in enumerate(chips):
                copy(a, 4 + j, (*chip, 1 - c), me).wait_recv()
        for cp in sends:
            cp.wait_send()
        for cp in local:
            cp.wait()

    hbm = pl.BlockSpec(memory_space=pl.ANY)
    return pl.pallas_call(
        body, name=name, in_specs=[hbm] * n_a, out_specs=[hbm] * n_a,
        out_shape=[SDS((N_DEV,) + a.shape, a.dtype) for a in arrays],
        scratch_shapes=[pltpu.SemaphoreType.DMA((n_a, N_DEV - 1)), pltpu.SemaphoreType.DMA((n_a, N_DEV - 1)),
                        pltpu.SemaphoreType.DMA((n_a,))],
        )(*arrays)


def _peer(k):
    x, y, c = lax.axis_index("x"), lax.axis_index("y"), lax.axis_index("c")
    px = 1 - x if k & 4 else x
    py = 1 - y if k & 2 else y
    pc = 1 - c if k & 1 else c
    return (px, py, pc), 4 * px + 2 * py + pc


_HBM = pl.BlockSpec(memory_space=pltpu.HBM)
_SEM = pl.BlockSpec(memory_space=pltpu.SEMAPHORE)


def _split_copies(src, land, send_sems, recv_sems, scatter, arrivals):
    _, me = _peer(0)
    out = []
    for k in range(1, N_DEV):
        coords, peer = _peer(k)
        for a in range(len(src)):
            sem = a * (N_DEV - 1) + k - 1
            if scatter[a]:
                s, d = src[a].at[peer], land[a].at[k]
            else:
                s, d = src[a], land[a].at[peer if arrivals else me]
            out.append(pltpu.make_async_remote_copy(
                src_ref=s, dst_ref=d, send_sem=send_sems.at[sem], recv_sem=recv_sems.at[sem], device_id=coords,
                device_id_type=pl.DeviceIdType.MESH))
    return out


def _exchange_start(srcs, lands, scatter, name):
    n = len(srcs)

    def body(*refs):
        src, land = refs[:n], refs[n:2 * n]
        send_sems, recv_sems = refs[2 * n], refs[2 * n + 1]
        token = refs[-1]
        for cp in _split_copies(src, land, send_sems, recv_sems, scatter, False):
            cp.start()
        token[...] = jnp.zeros_like(token)

    hbm_shape = lambda a: pltpu.HBM(a.shape, a.dtype)
    sem_shape = pltpu.SemaphoreType.DMA((n * (N_DEV - 1),))
    outs = pl.pallas_call(
        body, name=name,
        out_shape=(sem_shape, sem_shape, *[hbm_shape(a) for a in srcs], *[hbm_shape(a) for a in lands],
                   SDS((8, 128), F32)),
        in_specs=[_HBM] * (2 * n), out_specs=(_SEM, _SEM, *[_HBM] * (2 * n), pl.BlockSpec(memory_space=pltpu.VMEM)),
        input_output_aliases={a: 2 + a for a in range(2 * n)},
        compiler_params=pltpu.CompilerParams(has_side_effects=pltpu.SideEffectType.DATAFLOW_SIDE_EFFECTING),
    )(*[pltpu.with_memory_space_constraint(a, pltpu.HBM) for a in list(srcs) + list(lands)])
    return outs[0], outs[1], outs[2:2 + n], outs[2 + n:2 + 2 * n], outs[-1]


def _exchange_wait(send_sems, recv_sems, srcs, lands, scatter, after, name):
    n = len(srcs)

    def body(*refs):
        src, land = refs[:n], refs[n:2 * n]
        for cp in _split_copies(src, land, refs[2 * n], refs[2 * n + 1], scatter, False):
            cp.wait_send()
        for cp in _split_copies(src, land, refs[2 * n], refs[2 * n + 1], scatter, True):
            cp.wait_recv()

    hbm_shape = lambda a: pltpu.HBM(a.shape, a.dtype)
    outs = pl.pallas_call(
        body, name=name, out_shape=tuple(hbm_shape(a) for a in list(srcs) + list(lands)),
        in_specs=[_HBM] * (2 * n) + [_SEM, _SEM, pl.BlockSpec(memory_space=pl.ANY)], out_specs=(_HBM,) * (2 * n),
        input_output_aliases={a: a for a in range(2 * n)},
        compiler_params=pltpu.CompilerParams(has_side_effects=pltpu.SideEffectType.DATAFLOW_SIDE_EFFECTING),
    )(*srcs, *lands, send_sems, recv_sems, after)
    return outs[:n], outs[n:]


def _reduce8(a, tr, name):
    _, rows, cols = a.shape

    def body(a_ref, o_ref):
        acc = a_ref[0]
        for j in range(1, N_DEV):
            acc = acc + a_ref[j]
        o_ref[...] = acc

    return pl.pallas_call(
        body, name=name, grid=(rows // tr,),
        in_specs=[pl.BlockSpec((N_DEV, tr, cols), lambda i: (0, i, 0))],
        out_specs=pl.BlockSpec((tr, cols), lambda i: (i, 0)), out_shape=SDS((rows, cols), F32),
        compiler_params=_params("parallel"))(a)


def _reduce_landed(own, land, name, tc=256):
    rows, cols = own.shape

    def body(own_ref, land_ref, o_ref):
        acc = own_ref[...]
        for k in range(1, N_DEV):
            acc = acc + land_ref[k].astype(F32)
        o_ref[...] = acc

    return pl.pallas_call(
        body, name=name, grid=(cols // tc,),
        in_specs=[pl.BlockSpec((rows, tc), lambda j: (0, j)), pl.BlockSpec((N_DEV, rows, tc), lambda j: (0, 0, j))],
        out_specs=pl.BlockSpec((rows, tc), lambda j: (0, j)), out_shape=SDS((rows, cols), F32),
        compiler_params=_params("parallel"))(own, land)


def _adamw(w, g, m, v, name, tr=None):
    rows, cols = w.shape
    tr = rows if tr is None else tr

    def body(w_ref, g_ref, m_ref, v_ref, d_ref, mo_ref, vo_ref):
        d_ref[...], mo_ref[...], vo_ref[...] = _adam_update(w_ref[...], g_ref[...], m_ref[...], v_ref[...])

    blk = pl.BlockSpec((tr, cols), lambda i: (i, 0))
    return pl.pallas_call(
        body, name=name, grid=(rows // tr,), in_specs=[blk] * 4, out_specs=[blk] * 3,
        out_shape=[SDS((rows, cols), F32)] * 3, compiler_params=_params("parallel"))(w, g, m, v)


def _pad_rows(a, rows):
    return jnp.pad(a, ((0, rows - a.shape[0]),) + ((0, 0),) * (a.ndim - 1))


def _local_step(x, p, target, sm, wts, fetch_rest, send, tok):
    ones_q, ones_k, dup, dup_t = _head_consts()
    tri, triu, expand, expand_t = _ssd_consts()
    w_in_t = wts["in_t"]
    cwx, cwb = wts["ssm_cw"][:, :SSM_INNER], wts["ssm_cw"][:, SSM_INNER:]
    cbx, cbb = sm["ssm_conv_b"][:, :SSM_INNER], sm["ssm_conv_b"][:, SSM_INNER:]
    pad128 = lambda a: jnp.pad(a, ((0, 0), (0, 128 - a.shape[1])))
    dtb, alog = pad128(sm["dt_bias"]), pad128(sm["a_log"])
    dsk_e = jnp.repeat(sm["d_skip"], HEAD_DIM, axis=1)
    gq = jnp.tile(sm["q_norm_g"], (1, ATTN_DIM // HEAD_DIM))
    gk = jnp.tile(sm["k_norm_g"], (1, KV_DIM // HEAD_DIM))
    ffn_cw = wts["ffn_cw"].reshape(3, 2, 1, D_FF)
    ffn_cb = sm["ffn_conv_b"].reshape(2, 1, D_FF)

    proj, h1 = _norm_matmul(x, sm["attn_norm_g"] + tok, w_in_t, 1024, 768, "in_proj")
    qn, kd, vd = _attn_prep(proj, gq, gk, ones_q, ones_k, dup)
    attn_out, lse = _attn_fwd(qn, kd, vd)
    y_ssd, hs, ssm_out, pre_x, pre_b = _ssd_fwd(proj, cwx, cbx, cwb, cbb, dtb, alog, dsk_e, sm["ssm_norm_g"], tri,
                                                expand)
    rest = fetch_rest(ssm_out)
    w_out, w_up_t, w_down, w_gate, w_proj_t = (rest[k] for k in ("out", "up_t", "down", "gate", "proj_t"))
    x1 = _mm_resid([(attn_out, None, w_out[:ATTN_DIM]), (ssm_out, None, w_out[ATTN_DIM:])], x, None, 512, F32,
                   "out_proj")
    u, uc, h2, f = _up_act(x1, sm["ffn_norm_g"], w_up_t, ffn_cw, ffn_cb)
    x2 =_mm_resid([(f, None, w_down)], x1, None, 512, F32, "down_proj")
    dx2, dpre, dpp, h3, loss, dg_ple = _ple_loss(x2, sm["ple_norm_g"], w_gate, p, w_proj_t, target)

    g_gate = _wgrad(h3, None, dpre, "wg_gate")
    g_proj_t = _wgrad(dpp, None, p, "wg_proj")
    g_down = _wgrad(f, None, dx2, "wg_down")
    du, d_ffn_cw, d_ffn_cb = _ffn_bwd(dx2, w_down, u, uc, ffn_cw)
    dx1, dg_ffn = _mm_normbwd([(du, 0, w_up_t, D_FF, 0), (du, 1, w_up_t, D_FF, 1)], x1, sm["ffn_norm_g"], dx2, 512,
                              "up_proj_bwd")
    g_up_t = _wgrad(du, "all", h2, "wg_up")
    tok = send(dict(gate=g_gate, proj_t=g_proj_t, down=g_down, up_t=g_up_t)).astype(BF16)
    d_attn = _mm_resid([(dx1, None, w_out[:ATTN_DIM] + tok)], None, NT, 512, F32, "out_proj_bwd_attn")
    d_ssm = _mm_resid([(dx1, None, w_out[ATTN_DIM:] + tok)], None, NT, 512, F32, "out_proj_bwd_ssm")
    g_out = jnp.concatenate([_wgrad(attn_out, None, dx1, "wg_out_attn"), _wgrad(ssm_out, None, dx1, "wg_out_ssm")],
                            axis=0)
    tok = send(dict(out=g_out))
    (dz, dxs, dbc, ddt, dg_ssm, d_dsk_e, d_alog, d_dtb, d_cwx, d_cbx, d_cwb, d_cbb) = _ssd_bwd(
        proj, pre_x, pre_b, y_ssd, hs, d_ssm, cwx, cwb, dtb + tok, alog, dsk_e, sm["ssm_norm_g"], tri, triu, expand,
        expand_t)
    dqn, dkc, dkp, dvc, dvp = _attn_bwd(qn, kd, vd, attn_out, lse, d_attn, ones_k[:128, :128])
    dqkv, dgq, dgk = _attn_prep_bwd(proj, dqn, dkc, dkp, dvc, dvp, gq + tok, gk, ones_q, ones_k, dup_t)
    pieces = [(dqkv, 0, 1024), (dz, 1024, 2048), (dxs, 2048, 3072), (dbc, 3072, 3584), (ddt, 3584, 3712)]
    g_in_t = jnp.concatenate([_wgrad(a, None, h1, "wg_in_%d" % lo) for a, lo, _ in pieces], axis=0)[:IN_PROJ]
    tok = send(dict(in_t=g_in_t))
    grad_x, dg_attn = _mm_normbwd([(a, None, w_in_t, hi - lo, lo // (hi - lo)) for a, lo, hi in pieces], x,
                                  sm["attn_norm_g"] + tok, dx1, 512, "in_proj_bwd")

    small = dict(
        attn_norm_g=dg_attn, q_norm_g=dgq.reshape(-1, HEAD_DIM).sum(0, keepdims=True),
        k_norm_g=dgk.reshape(-1, HEAD_DIM).sum(0, keepdims=True),
        ssm_conv_w=jnp.concatenate([d_cwx, d_cwb], axis=1), ssm_conv_b=jnp.concatenate([d_cbx, d_cbb], axis=1),
        dt_bias=d_dtb[:, :SSM_HEADS], a_log=d_alog[:, :SSM_HEADS],
        d_skip=d_dsk_e.reshape(SSM_HEADS, HEAD_DIM).sum(1)[None, :], ssm_norm_g=dg_ssm, ffn_norm_g=dg_ffn,
        ffn_conv_w=d_ffn_cw.reshape(3, 2 * D_FF), ffn_conv_b=d_ffn_cb.reshape(1, 2 * D_FF), ple_norm_g=dg_ple)
    return loss[0, 0], grad_x, small


_SMALL = (("attn_norm_g", 1, 1024), ("q_norm_g", 1, 64), ("k_norm_g", 1, 64), ("ssm_conv_w", 4, XBC_DIM),
          ("ssm_conv_b", 1, XBC_DIM), ("dt_bias", 1, 16), ("a_log", 1, 16), ("d_skip", 1, 16), ("ssm_norm_g", 1, 1024),
          ("ffn_norm_g", 1, 1024), ("ffn_conv_w", 3, 2 * D_FF), ("ffn_conv_b", 1, 2 * D_FF), ("ple_norm_g", 1, 1024))
_SMALL_ROWS, _SMALL_COLS = 32, XBC_DIM
_SHARDED_SMALL = ("ssm_conv_w", "ffn_conv_w")


def _small_chunks(n):
    return 1 if n <= _SMALL_COLS else 4


def _pack_small(parts, loss):
    rows = []
    for k, r, n in _SMALL:
        c = _small_chunks(n)
        rows.append(jnp.pad(parts[k].reshape(r * c, n // c), ((0, 0), (0, _SMALL_COLS - n // c))))
    packed = _pad_rows(jnp.concatenate(rows, axis=0), _SMALL_ROWS)
    at_loss = ((lax.broadcasted_iota(jnp.int32, packed.shape, 0) == _SMALL_ROWS - 1) &
               (lax.broadcasted_iota(jnp.int32, packed.shape, 1) == 0))
    return jnp.where(at_loss, loss, packed)


def _adam_update(w, g, m, v):
    c1 = 1.0 - ADAM_B1 ** ADAM_STEP
    c2 = 1.0 - ADAM_B2 ** ADAM_STEP
    mn = ADAM_B1 * m + (1.0 - ADAM_B1) * g
    vn = ADAM_B2 * v + (1.0 - ADAM_B2) * (g * g)
    return -ADAM_LR * ((mn / c1) / (jnp.sqrt(vn / c2) + ADAM_EPS) + ADAM_WD * w), mn, vn


def _adamw_small(g_all, g_shard, w, m, v):
    ins, shapes = [g_all], []
    for k, _, _ in _SMALL:
        shape2 = w[k].shape if w[k].ndim == 2 else (1, w[k].shape[0])
        shapes.append(shape2)
        ins += ([g_shard[k]] if k in _SHARDED_SMALL else []) + [a.reshape(shape2) for a in (w[k], m[k], v[k])]

    def body(*refs):
        g_ref, pos, row = refs[0], 1, 0
        outs = refs[len(ins):]
        for i, (k, r, n) in enumerate(_SMALL):
            c = _small_chunks(n)
            if k in _SHARDED_SMALL:
                g = refs[pos][...]
                pos += 1
            elif c == 1:
                g = g_ref[row:row + r, 0:n]
            else:
                g = jnp.concatenate([g_ref[row + j:row + j + 1, 0:n // c] for j in range(c)], axis=1)
            row += r * c
            d, mn, vn = _adam_update(refs[pos][...], g, refs[pos + 1][...], refs[pos + 2][...])
            pos += 3
            for o_ref, val in zip(outs[4 * i:4 * i + 4], (g, d, mn, vn)):
                o_ref[...] = val

    res = pl.pallas_call(body, name="adamw_small",
                         out_shape=[SDS(s, F32) for s in shapes for _ in range(4)])(*ins)
    return {k: tuple(a.reshape(w[k].shape) for a in res[4 * i:4 * i + 4]) for i, (k, _, _) in enumerate(_SMALL)}


def kernel(x, p, attn_norm_g, w_in, q_norm_g, k_norm_g, ssm_conv_w, ssm_conv_b, dt_bias, a_log, d_skip, ssm_norm_g, w_out, ffn_norm_g, w_up, ffn_conv_w, ffn_conv_b, w_down, ple_norm_g, w_ple_gate, w_ple_proj, loss_target, m_attn_norm_g, m_w_in, m_q_norm_g, m_k_norm_g, m_ssm_conv_w, m_ssm_conv_b, m_dt_bias, m_a_log, m_d_skip, m_ssm_norm_g, m_w_out, m_ffn_norm_g, m_w_up, m_ffn_conv_w, m_ffn_conv_b, m_w_down, m_ple_norm_g, m_w_ple_gate, m_w_ple_proj, v_attn_norm_g, v_w_in, v_q_norm_g, v_k_norm_g, v_ssm_conv_w, v_ssm_conv_b, v_dt_bias, v_a_log, v_d_skip, v_ssm_norm_g, v_w_out, v_ffn_norm_g, v_w_up, v_ffn_conv_w, v_ffn_conv_b, v_w_down, v_ple_norm_g, v_w_ple_gate, v_w_ple_proj):
    names = ("attn_norm_g", "w_in", "q_norm_g", "k_norm_g", "ssm_conv_w", "ssm_conv_b", "dt_bias", "a_log", "d_skip",
             "ssm_norm_g", "w_out", "ffn_norm_g", "w_up", "ffn_conv_w", "ffn_conv_b", "w_down", "ple_norm_g",
             "w_ple_gate", "w_ple_proj")
    w = dict(zip(names, (attn_norm_g, w_in, q_norm_g, k_norm_g, ssm_conv_w, ssm_conv_b, dt_bias, a_log, d_skip,
                         ssm_norm_g, w_out, ffn_norm_g, w_up, ffn_conv_w, ffn_conv_b, w_down, ple_norm_g, w_ple_gate,
                         w_ple_proj)))
    m = dict(zip(names, (m_attn_norm_g, m_w_in, m_q_norm_g, m_k_norm_g, m_ssm_conv_w, m_ssm_conv_b, m_dt_bias,
                         m_a_log, m_d_skip, m_ssm_norm_g, m_w_out, m_ffn_norm_g, m_w_up, m_ffn_conv_w, m_ffn_conv_b,
                         m_w_down, m_ple_norm_g, m_w_ple_gate, m_w_ple_proj)))
    v = dict(zip(names, (v_attn_norm_g, v_w_in, v_q_norm_g, v_k_norm_g, v_ssm_conv_w, v_ssm_conv_b, v_dt_bias,
                         v_a_log, v_d_skip, v_ssm_norm_g, v_w_out, v_ffn_norm_g, v_w_up, v_ffn_conv_w, v_ffn_conv_b,
                         v_w_down, v_ple_norm_g, v_w_ple_gate, v_w_ple_proj)))
    w, m, v = ({k: a[0] for k, a in d.items()} for d in (w, m, v))
    me = 4 * lax.axis_index("x") + 2 * lax.axis_index("y") + lax.axis_index("c")

    mine = dict(in_t=w["w_in"].T, out=w["w_out"], up_t=w["w_up"].T, down=w["w_down"], gate=w["w_ple_gate"],
                proj_t=w["w_ple_proj"].T)
    mine = {k: a.astype(BF16) for k, a in mine.items()}
    conv_pack = jnp.pad(jnp.concatenate([w["ssm_conv_w"].reshape(-1), w["ffn_conv_w"].reshape(-1)]),
                        (0, 3072 - 2880)).reshape(8, 384)
    all_in, all_conv = _all_gather([mine["in_t"], conv_pack], "gather_first")
    later = ("out", "up_t", "down", "gate", "proj_t")
    zones = [lax.dynamic_update_slice(lax.empty((N_DEV,) + mine[k].shape, BF16), mine[k][None], (me, 0, 0))
             for k in later]
    zones, all_in, all_conv = lax.optimization_barrier((zones, all_in, all_conv))
    rest_state = _exchange_start([mine[k] for k in later], zones, [False] * len(later), "gather_rest_start")

    def fetch_rest(after):
        _, landed = _exchange_wait(*rest_state[:4], [False] * len(later), after, "gather_rest_wait")
        return {k: a.reshape(N_DEV * a.shape[1], a.shape[2]) for k, a in zip(later, landed)}

    wts = dict(in_t=_pad_rows(all_in.reshape(IN_PROJ, D_MODEL), IN_PROJ_PAD))
    conv_flat = all_conv.reshape(N_DEV, 3072)
    wts["ssm_cw"] = conv_flat[:, :768].reshape(N_DEV, 4, XBC_DIM // N_DEV).transpose(1, 0, 2).reshape(4, XBC_DIM)
    wts["ffn_cw"] = conv_flat[:, 768:2880].reshape(N_DEV, 3, 2 * D_FF // N_DEV).transpose(1, 0, 2).reshape(3, 2 * D_FF)
    sm = {k: w[k].reshape(1, -1) for k, _, _ in _SMALL if k not in _SHARDED_SMALL}

    in_flight = []

    def send(grads):
        keys = sorted(grads)
        blocks = [grads[k].reshape(N_DEV, grads[k].shape[0] // N_DEV, grads[k].shape[1]) for k in keys]
        own = [lax.dynamic_index_in_dim(a, me, 0, keepdims=False) for a in blocks]
        srcs = [a.astype(BF16) for a in blocks]
        state = _exchange_start(srcs, [lax.empty(a.shape, BF16) for a in srcs], [True] * len(keys),
                                "send_" + "_".join(keys))
        in_flight.append((keys, state, own))
        return state[4][0:1, 0:1]

    loss, grad_x, small = _local_step(x[0], p[0, 0], loss_target[0], sm, wts, fetch_rest, send,
                                      rest_state[4][0:1, 0:1])

    (got_small,) = _all_gather([_pack_small(small, loss)], "gather_small_grads")
    g_small = _reduce8(got_small, _SMALL_ROWS, "reduce_small")
    loss = g_small[_SMALL_ROWS - 1, 0]
    grads = {}
    for keys, state, own in in_flight:
        _, landed = _exchange_wait(*state[:4], [True] * len(keys), grad_x, "wait_" + "_".join(keys))
        for k, mine_k, land in zip(keys, own, landed):
            grads[k] = _reduce_landed(mine_k, land, "reduce_" + k)
    gw = {"w_in": grads["in_t"].T, "w_out": grads["out"], "w_up": grads["up_t"].T, "w_down": grads["down"],
          "w_ple_gate": grads["gate"], "w_ple_proj": grads["proj_t"].T}
    n_ssm, n_ffn = XBC_DIM // N_DEV, 2 * D_FF // N_DEV
    g_shard = {"ssm_conv_w": lax.dynamic_slice(g_small, (3, me * n_ssm), (4, n_ssm)),
               "ffn_conv_w": lax.dynamic_slice(g_small[13:25, :2 * D_FF // 4].reshape(3, 2 * D_FF), (0, me * n_ffn),
                                               (3, n_ffn))}

    delta, new_m, new_v = {}, {}, {}
    for k, tr in (("w_in", 256), ("w_out", None), ("w_up", 256), ("w_down", None), ("w_ple_gate", None),
                  ("w_ple_proj", None)):
        delta[k], new_m[k], new_v[k] = _adamw(w[k], gw[k], m[k], v[k], "adamw_" + k, tr)
    for k, (g_k, d_k, m_k, v_k) in _adamw_small(g_small, g_shard, w, m, v).items():
        gw[k], delta[k], new_m[k], new_v[k] = g_k, d_k, m_k, v_k

    outs = [loss, grad_x[None]]
    for d in (gw, delta, new_m, new_v):
        outs += [d[k][None] for k in names]
    return tuple(outs)
```

```python
import functools

import numpy as np
import jax
import jax.numpy as jnp
from jax import lax
from jax.experimental import pallas as pl
from jax.experimental.pallas import tpu as pltpu

F32 = jnp.float32
BF16 = jnp.bfloat16
SDS = jax.ShapeDtypeStruct
EPS = 1e-6
N_DEV = 8
D_MODEL = 1024
HEAD_DIM = 64
ATTN_DIM = 512
KV_DIM = 256
SSM_INNER = 1024
SSM_HEADS = 16
BC_DIM = 256
XBC_DIM = SSM_INNER + 2 * BC_DIM
MIX_DIM = ATTN_DIM + SSM_INNER
IN_PROJ = 3600
IN_PROJ_PAD = 3840
D_FF = 2816
PLE_DIM = 256
CHUNK = 128
SUPER = 2048
DILATIONS = (1, 4, 16)
TILE_UNROLL = 8
VMEM_LIMIT = 56 * 1024 * 1024
ADAM_LR, ADAM_B1, ADAM_B2, ADAM_EPS, ADAM_WD, ADAM_STEP = 0.001, 0.9, 0.999, 1e-08, 0.01, 10

NT = (((1,), (1,)), ((), ()))
TN = (((0,), (0,)), ((), ()))


def _params(*sem):
    return pltpu.CompilerParams(dimension_semantics=sem if sem else None, vmem_limit_bytes=VMEM_LIMIT)


def _dot(a, b, dims=None):
    if dims is None:
        return jnp.dot(a, b, preferred_element_type=F32)
    return lax.dot_general(a, b, dims, preferred_element_type=F32)


def _hdot(a, b, parts=2):
    a_exact = a.dtype == BF16
    x = b if a_exact else a
    acc = None
    for _ in range(parts):
        piece = x.astype(BF16)
        x = x - piece.astype(F32)
        d = _dot(a, piece) if a_exact else _dot(piece, b)
        acc = d if acc is None else acc + d
    return acc


def _sigmoid(x):
    return 0.5 * jnp.tanh(0.5 * x) + 0.5


def _shift_down(x, halo8, s):
    xr = pltpu.roll(x, s, 0)
    row = lax.broadcasted_iota(jnp.int32, halo8.shape, 0)
    first = jnp.where(row < s, pltpu.roll(halo8, s, 0), xr[0:8])
    return jnp.concatenate([first, xr[8:]], axis=0)


def _shift_up(x, halo8, s):
    n = x.shape[0]
    xr = pltpu.roll(x, n - s, 0)
    row = lax.broadcasted_iota(jnp.int32, halo8.shape, 0)
    last = jnp.where(row >= 8 - s, pltpu.roll(halo8, 8 - s, 0), xr[n - 8:])
    return jnp.concatenate([xr[:n - 8], last], axis=0)


def _norm_matmul(x, g, wt, tm, tn, name):
    t, k = x.shape
    n = wt.shape[0]

    def body(x_ref, g_ref, w_ref, o_ref, h_ref):
        @pl.when(pl.program_id(1) == 0)
        def _():
            xv = x_ref[...]
            r = lax.rsqrt(jnp.mean(xv * xv, axis=-1, keepdims=True) + EPS)
            h_ref[...] = (xv * r * g_ref[...]).astype(BF16)
        o_ref[...] = _dot(h_ref[...], w_ref[...], NT)

    return pl.pallas_call(
        body, name=name, grid=(t // tm, n // tn),
        in_specs=[pl.BlockSpec((tm, k), lambda i, j: (i, 0)), pl.BlockSpec((1, k), lambda i, j: (0, 0)),
                  pl.BlockSpec((tn, k), lambda i, j: (j, 0))],
        out_specs=[pl.BlockSpec((tm, tn), lambda i, j: (i, j)), pl.BlockSpec((tm, k), lambda i, j: (i, 0))],
        out_shape=[SDS((t, n), F32), SDS((t, k), BF16)],
        compiler_params=_params("parallel", "arbitrary"))(x, g, wt)


def _a_spec(a, lead, tm):
    if lead is None:
        return pl.BlockSpec((tm, a.shape[-1]), lambda i: (i, 0))
    return pl.BlockSpec((None, tm, a.shape[-1]), lambda i, _l=lead: (_l, i, 0))


def _mm_resid(pairs, res, dims, tm, out_dtype, name):
    t = pairs[0][0].shape[-2]
    n = pairs[0][2].shape[1] if dims is None else pairs[0][2].shape[0]
    np_ = len(pairs)

    def body(*refs):
        o_ref = refs[-1]
        acc = refs[2 * np_][...] if res is not None else None
        for q in range(np_):
            d = _dot(refs[q][...].astype(BF16), refs[np_ + q][...], dims)
            acc = d if acc is None else acc + d
        o_ref[...] = acc.astype(out_dtype)

    in_specs = [_a_spec(a, lead, tm) for a, lead, _ in pairs]
    in_specs += [pl.BlockSpec(b.shape, lambda i: (0, 0)) for _, _, b in pairs]
    args = [a for a, _, _ in pairs] + [b for _, _, b in pairs]
    if res is not None:
        in_specs.append(pl.BlockSpec((tm, n), lambda i: (i, 0)))
        args.append(res)
    return pl.pallas_call(
        body, name=name, grid=(t // tm,), in_specs=in_specs,
        out_specs=pl.BlockSpec((tm, n), lambda i: (i, 0)), out_shape=SDS((t, n), out_dtype),
        compiler_params=_params("parallel"))(*args)


def _mm_normbwd(pairs, x, g, dres, tm, name):
    t, k = x.shape
    np_ = len(pairs)
    b_specs = [pl.BlockSpec((rows, b.shape[1]), lambda i, _b=blk: (_b, 0)) for _, _, b, rows, blk in pairs]
    pairs = [(a, lead, b) for a, lead, b, _, _ in pairs]

    def body(*refs):
        x_ref, g_ref, dres_ref, dx_ref, dg_ref = refs[2 * np_:]
        dh = None
        for q in range(np_):
            d = _dot(refs[q][...], refs[np_ + q][...])
            dh = d if dh is None else dh + d
        xv = x_ref[...]
        r = lax.rsqrt(jnp.mean(xv * xv, axis=-1, keepdims=True) + EPS)
        xh = xv * r

        @pl.when(pl.program_id(0) == 0)
        def _():
            dg_ref[...] = jnp.zeros_like(dg_ref)
        dg_ref[...] += jnp.sum(dh * xh, axis=0, keepdims=True)
        gd = dh * g_ref[...]
        dx_ref[...] = dres_ref[...] + r * (gd - xh * jnp.mean(gd * xh, axis=-1, keepdims=True))

    in_specs = [_a_spec(a, lead, tm) for a, lead, _ in pairs] + b_specs
    in_specs += [pl.BlockSpec((tm, k), lambda i: (i, 0)), pl.BlockSpec((1, k), lambda i: (0, 0)),
                 pl.BlockSpec((tm, k), lambda i: (i, 0))]
    args = [a for a, _, _ in pairs] + [b for _, _, b in pairs] + [x, g, dres]
    return pl.pallas_call(
        body, name=name, grid=(t // tm,), in_specs=in_specs,
        out_specs=[pl.BlockSpec((tm, k), lambda i: (i, 0)), pl.BlockSpec((1, k), lambda i: (0, 0))],
        out_shape=[SDS((t, k), F32), SDS((1, k), F32)],
        compiler_params=_params("arbitrary"))(*args)


def _wgrad(a, a_lead, b, name, tk=2048):
    t, m = a.shape[-2:]
    n = b.shape[1]
    tm = m if m <= 1024 else 1408
    assert m % tm == 0

    def body(a_ref, b_ref, o_ref, acc):
        @pl.when(pl.program_id(1) == 0)
        def _():
            acc[...] = jnp.zeros_like(acc)
        acc[...] += _dot(a_ref[...].astype(BF16), b_ref[...].astype(BF16), TN)

        @pl.when(pl.program_id(1) == pl.num_programs(1) - 1)
        def _():
            o_ref[...] = acc[...].astype(BF16)

    per, lead = m // tm, 1
    if a_lead == "all":
        lead = a.shape[0]
        a_spec = pl.BlockSpec((None, tk, tm), lambda mi, ki: (mi // per, ki, mi % per))
    elif a_lead is None:
        a_spec = pl.BlockSpec((tk, tm), lambda mi, ki: (ki, mi))
    else:
        a_spec = pl.BlockSpec((None, tk, tm), lambda mi, ki, _l=a_lead: (_l, ki, mi))
    return pl.pallas_call(
        body, name=name, grid=(lead * per, t // tk),
        in_specs=[a_spec, pl.BlockSpec((tk, n), lambda mi, ki: (ki, 0))],
        out_specs=pl.BlockSpec((tm, n), lambda mi, ki: (mi, 0)), out_shape=SDS((lead * m, n), BF16),
        scratch_shapes=[pltpu.VMEM((tm, n), F32)],
        compiler_params=_params("parallel", "arbitrary"))(a, b)


def _head_consts():
    iq = np.arange(ATTN_DIM)
    ik = np.arange(KV_DIM)
    ones_q = (iq[:, None] // HEAD_DIM == iq[None, :] // HEAD_DIM).astype(np.float32)
    ones_k = (ik[:, None] // HEAD_DIM == ik[None, :] // HEAD_DIM).astype(np.float32)
    dup = (ik[:, None] == (HEAD_DIM * (iq[None, :] // 128) + iq[None, :] % HEAD_DIM)).astype(np.float32)
    return jnp.asarray(ones_q, BF16), jnp.asarray(ones_k, BF16), jnp.asarray(dup, BF16), jnp.asarray(dup.T, BF16)


def _attn_prep(proj, gq, gk, ones_q, ones_k, dup, tm=512):
    t = proj.shape[0]

    def body(p_ref, gq_ref, gk_ref, oq_ref, ok_ref, dup_ref, qn_ref, kd_ref, vd_ref):
        q = p_ref[:, 0:ATTN_DIM]
        k = p_ref[:, ATTN_DIM:ATTN_DIM + KV_DIM]
        v = p_ref[:, ATTN_DIM + KV_DIM:]
        rq = lax.rsqrt(_hdot(q * q, oq_ref[...]) * (1.0 / HEAD_DIM) + EPS)
        qn_ref[...] = (q * rq * gq_ref[...]) * (HEAD_DIM ** -0.5)
        rk = lax.rsqrt(_hdot(k * k, ok_ref[...]) * (1.0 / HEAD_DIM) + EPS)
        kn = k * rk * gk_ref[...]
        kd_ref[...] = _dot(kn.astype(BF16), dup_ref[...])
        vd_ref[...] = _dot(v.astype(BF16), dup_ref[...])

    full = lambda a: pl.BlockSpec(a.shape, lambda i: (0, 0))
    o_spec = pl.BlockSpec((tm, ATTN_DIM), lambda i: (i, 0))
    return pl.pallas_call(
        body, name="attn_prep", grid=(t // tm,),
        in_specs=[pl.BlockSpec((tm, 1024), lambda i: (i, 0)), full(gq), full(gk), full(ones_q), full(ones_k), full(dup)],
        out_specs=[o_spec, o_spec, o_spec], out_shape=[SDS((t, ATTN_DIM), F32)] * 3,
        compiler_params=_params("parallel"))(proj, gq, gk, ones_q, ones_k, dup)


def _attn_prep_bwd(proj, dqn, dkc, dkp, dvc, dvp, gq, gk, ones_q, ones_k, dup_t, tm=512):
    t = proj.shape[0]
    nblk = t // tm
    off = SUPER // tm

    def body(p_ref, dqn_ref, dkc_ref, dkp_ref, dvc_ref, dvp_ref, gq_ref, gk_ref, oq_ref, ok_ref, dt_ref,
             o_ref, dgq_ref, dgk_ref):
        i = pl.program_id(0)
        has_next = (i + off < nblk).astype(F32)
        q = p_ref[:, 0:ATTN_DIM]
        k = p_ref[:, ATTN_DIM:ATTN_DIM + KV_DIM]
        dkn = _hdot(dkc_ref[...] + has_next * dkp_ref[...], dt_ref[...])
        dv = _hdot(dvc_ref[...] + has_next * dvp_ref[...], dt_ref[...])

        @pl.when(i == 0)
        def _():
            dgq_ref[...] = jnp.zeros_like(dgq_ref)
            dgk_ref[...] = jnp.zeros_like(dgk_ref)

        rq = lax.rsqrt(_hdot(q * q, oq_ref[...]) * (1.0 / HEAD_DIM) + EPS)
        xh = q * rq
        dy = dqn_ref[...] * (HEAD_DIM ** -0.5)
        dgq_ref[...] += jnp.sum(dy * xh, axis=0, keepdims=True)
        gd = dy * gq_ref[...]
        dq = rq * (gd - xh * (_hdot(gd * xh, oq_ref[...]) * (1.0 / HEAD_DIM)))
        rk = lax.rsqrt(_hdot(k * k, ok_ref[...]) * (1.0 / HEAD_DIM) + EPS)
        kh = k * rk
        dgk_ref[...] += jnp.sum(dkn * kh, axis=0, keepdims=True)
        gdk = dkn * gk_ref[...]
        dk = rk * (gdk - kh * (_hdot(gdk * kh, ok_ref[...]) * (1.0 / HEAD_DIM)))
        o_ref[:, 0:ATTN_DIM] = dq.astype(BF16)
        o_ref[:, ATTN_DIM:ATTN_DIM + KV_DIM] = dk.astype(BF16)
        o_ref[:, ATTN_DIM + KV_DIM:] = dv.astype(BF16)

    full = lambda a: pl.BlockSpec(a.shape, lambda i: (0, 0))
    cur = pl.BlockSpec((tm, ATTN_DIM), lambda i: (i, 0))
    nxt = pl.BlockSpec((tm, ATTN_DIM), lambda i: (jnp.minimum(i + off, nblk - 1), 0))
    return pl.pallas_call(
        body, name="attn_prep_bwd", grid=(nblk,),
        in_specs=[pl.BlockSpec((tm, 1024), lambda i: (i, 0)), cur, cur, nxt, cur, nxt,
                  full(gq), full(gk), full(ones_q), full(ones_k), full(dup_t)],
        out_specs=[pl.BlockSpec((tm, 1024), lambda i: (i, 0)), pl.BlockSpec((1, ATTN_DIM), lambda i: (0, 0)),
                   pl.BlockSpec((1, KV_DIM), lambda i: (0, 0))],
        out_shape=[SDS((t, 1024), BF16), SDS((1, ATTN_DIM), F32), SDS((1, KV_DIM), F32)],
        compiler_params=_params("arbitrary"))(proj, dqn, dkc, dkp, dvc, dvp, gq, gk, ones_q, ones_k, dup_t)


def _tile_masks():
    qi = lax.broadcasted_iota(jnp.int32, (2 * CHUNK, 2 * CHUNK), 0) & (CHUNK - 1)
    kj = lax.broadcasted_iota(jnp.int32, (2 * CHUNK, 2 * CHUNK), 1)
    delta = CHUNK + qi - kj
    band = (delta >= 0) & (delta <= CHUNK)
    return band, kj


def _deinterleave(dst, src, n_rows, d):
    per = n_rows // d
    for r in range(d):
        dst[r * per:(r + 1) * per, :] = src[pl.ds(r, per, stride=d), :]


def _attn_specs(t):
    blk = lambda f: pl.BlockSpec((SUPER, 128), f)
    cur = blk(lambda h, s: (s, h))
    prev = blk(lambda h, s: (jnp.maximum(s - 1, 0), h))
    return cur, prev


def _attn_fwd(qn, kd, vd):
    t = qn.shape[0]
    cur, prev = _attn_specs(t)

    def body(q_ref, kp_ref, kc_ref, vp_ref, vc_ref, o_ref, lse_ref, kk, vv, qd, kdd, vdd, po, pm, pll, acc, mm, ll):
        s = pl.program_id(1)
        kk[0:SUPER, :] = kp_ref[...]
        kk[SUPER:, :] = kc_ref[...]
        vv[0:SUPER, :] = vp_ref[...]
        vv[SUPER:, :] = vc_ref[...]
        m0 = lax.broadcasted_iota(jnp.int32, (CHUNK, 128), 1) < HEAD_DIM
        band, kj = _tile_masks()
        for d in DILATIONS:
            lq = SUPER // d
            if d == 1:
                qs_ref, ks_ref, vs_ref = q_ref, kk, vv
            else:
                _deinterleave(qd, q_ref, SUPER, d)
                _deinterleave(kdd, kk, 2 * SUPER, d)
                _deinterleave(vdd, vv, 2 * SUPER, d)
                qs_ref, ks_ref, vs_ref = qd, kdd, vdd

            nblk = lq // CHUNK

            def key_rows(ti):
                return pl.ds((ti // nblk) * 2 * lq + lq + (ti % nblk - 1) * CHUNK, 2 * CHUNK)

            def scores(ti):
                qt = qs_ref[pl.ds(ti * CHUNK, CHUNK), :]
                qs = jnp.concatenate([jnp.where(m0, qt, 0.0), jnp.where(m0, 0.0, qt)], axis=0).astype(BF16)
                return _dot(qs, ks_ref[key_rows(ti), :].astype(BF16), NT)

            def softmax_pv(ti, sc):
                ok = band if ti % nblk > 0 else band & (kj >= jnp.where(s > 0, 0, CHUNK))
                sc = jnp.where(ok, sc, -jnp.inf)
                mt = jnp.max(sc, axis=-1, keepdims=True)
                p = jnp.exp(sc - mt)
                lt = jnp.sum(p, axis=-1, keepdims=True)
                ot = _dot(p.astype(BF16), vs_ref[key_rows(ti), :].astype(BF16))
                qrows = pl.ds(ti * CHUNK, CHUNK)
                po[qrows, :] = jnp.where(m0, ot[:CHUNK], ot[CHUNK:])
                pm[qrows, :] = jnp.where(m0, mt[:CHUNK], mt[CHUNK:])
                pll[qrows, :] = jnp.where(m0, lt[:CHUNK], lt[CHUNK:])

            for ti in range(SUPER // CHUNK):
                softmax_pv(ti, scores(ti))
            if d == 1:
                acc[...] = po[...]
                mm[...] = pm[...]
                ll[...] = pll[...]
            else:
                for r in range(d):
                    rows = pl.ds(r, lq, stride=d)
                    seg = slice(r * lq, (r + 1) * lq)
                    m_old, m_new = mm[rows, :], pm[seg, :]
                    m_all = jnp.maximum(m_old, m_new)
                    a, b = jnp.exp(m_old - m_all), jnp.exp(m_new - m_all)
                    acc[rows, :] = acc[rows, :] * a + po[seg, :] * b
                    ll[rows, :] = ll[rows, :] * a + pll[seg, :] * b
                    mm[rows, :] = m_all
        o_ref[...] = acc[...] / ll[...]
        lse_ref[...] = mm[...] + jnp.log(ll[...])

    big = pltpu.VMEM((2 * SUPER, 128), F32)
    one = pltpu.VMEM((SUPER, 128), F32)
    return pl.pallas_call(
        body, name="attn_fwd", grid=(4, t // SUPER),
        in_specs=[cur, prev, cur, prev, cur], out_specs=[cur, cur],
        out_shape=[SDS((t, ATTN_DIM), F32)] * 2,
        scratch_shapes=[big, big, one, big, big, one, one, one, one, one, one],
        compiler_params=_params("parallel", "arbitrary"))(qn, kd, kd, vd, vd)


def _attn_bwd(qn, kd, vd, out, lse, dout, ones_pair):
    t = qn.shape[0]
    cur, prev = _attn_specs(t)

    def body(q_ref, kp_ref, kc_ref, vp_ref, vc_ref, o_ref, lse_ref, do_ref, ones_ref,
             dq_ref, dkc_ref, dkp_ref, dvc_ref, dvp_ref,
             kk, vv, od, ld, kb, vb, qsb, dosb, tk, tv, pdq, delta):
        s = pl.program_id(1)
        delta[...] = _hdot(do_ref[...] * o_ref[...], ones_ref[...])

        def per_row(a):
            ar = pltpu.roll(a, HEAD_DIM, 1)
            rows = jnp.concatenate([jnp.where(m0, a, ar), jnp.where(m0, ar, a)], axis=0)
            return jnp.concatenate([rows, rows], axis=1)

        kk[0:SUPER, :] = kp_ref[...]
        kk[SUPER:, :] = kc_ref[...]
        vv[0:SUPER, :] = vp_ref[...]
        vv[SUPER:, :] = vc_ref[...]
        for ref in (dq_ref, dkc_ref, dkp_ref, dvc_ref, dvp_ref):
            ref[...] = jnp.zeros_like(ref)
        m0 = lax.broadcasted_iota(jnp.int32, (CHUNK, 128), 1) < HEAD_DIM
        band, kj = _tile_masks()
        ninf = -jnp.inf
        for d in DILATIONS:
            lq = SUPER // d
            nblk = lq // CHUNK
            for r in range(d):
                seg = slice(r * 2 * lq, (r + 1) * 2 * lq)
                kb[seg, :] = kk[pl.ds(r, 2 * lq, stride=d), :].astype(BF16)
                vb[seg, :] = vv[pl.ds(r, 2 * lq, stride=d), :].astype(BF16)
            for ti in range(SUPER // CHUNK):
                rows = pl.ds(ti // nblk + d * CHUNK * (ti % nblk), CHUNK, stride=d)
                for src, dst in ((q_ref, qsb), (do_ref, dosb)):
                    a = src[rows, :]
                    dst[ti * 2 * CHUNK:(ti + 1) * 2 * CHUNK, :] = jnp.concatenate(
                        [jnp.where(m0, a, 0.0), jnp.where(m0, 0.0, a)], axis=0).astype(BF16)
                ld[ti * CHUNK:(ti + 1) * CHUNK, :] = lse_ref[rows, :]
                od[ti * CHUNK:(ti + 1) * CHUNK, :] = delta[rows, :]

            def operands(ti):
                r, nb = ti // nblk, ti % nblk
                stacked = slice(ti * 2 * CHUNK, (ti + 1) * 2 * CHUNK)
                krows = pl.ds(r * 2 * lq + lq + (nb - 1) * CHUNK, 2 * CHUNK)
                return stacked, krows

            def scores(ti):
                stacked, krows = operands(ti)
                kt = kb[krows, :]
                return dict(ti=ti, sc=_dot(qsb[stacked, :], kt, NT), dp=_dot(dosb[stacked, :], vb[krows, :], NT))

            def softmax_grad(c):
                qrows = slice(c["ti"] * CHUNK, (c["ti"] + 1) * CHUNK)
                ok = band if c["ti"] % nblk > 0 else band & (kj >= jnp.where(s > 0, 0, CHUNK))
                p = jnp.exp(jnp.where(ok, c.pop("sc"), ninf) - per_row(ld[qrows, :]))
                ds = p * (c.pop("dp") - per_row(od[qrows, :]))
                c.update(p=p.astype(BF16), ds=ds.astype(BF16))
                return c

            def grads(c):
                ti = c["ti"]
                stacked, krows = operands(ti)
                dqs = _dot(c["ds"], kb[krows, :])
                pdq[ti * CHUNK:(ti + 1) * CHUNK, :] = jnp.where(m0, dqs[:CHUNK], dqs[CHUNK:])
                tk[stacked, :] = _dot(c["ds"], qsb[stacked, :], TN)
                tv[stacked, :] = _dot(c["p"], dosb[stacked, :], TN)

            n_tiles = SUPER // CHUNK
            stage_a = scores(0)
            for ti in range(n_tiles):
                ahead = scores(ti + 1) if ti + 1 < n_tiles else None
                grads(softmax_grad(stage_a))
                stage_a = ahead

            for r in range(d):
                dq_ref[pl.ds(r, lq, stride=d), :] += pdq[r * lq:(r + 1) * lq, :]
                for tile_out, cur_ref, prev_ref in ((tk, dkc_ref, dkp_ref), (tv, dvc_ref, dvp_ref)):
                    first = r * nblk * 2 * CHUNK
                    prev_ref[pl.ds(SUPER - CHUNK * d + r, CHUNK, stride=d), :] += tile_out[first:first + CHUNK, :]
                    for nb in range(nblk):
                        at = (r * nblk + nb) * 2 * CHUNK
                        part = tile_out[at + CHUNK:at + 2 * CHUNK, :]
                        if nb + 1 < nblk:
                            part = part + tile_out[at + 2 * CHUNK:at + 3 * CHUNK, :]
                        cur_ref[pl.ds(r + d * nb * CHUNK, CHUNK, stride=d), :] += part

    big = pltpu.VMEM((2 * SUPER, 128), F32)
    one = pltpu.VMEM((SUPER, 128), F32)
    half = pltpu.VMEM((2 * SUPER, 128), BF16)
    return pl.pallas_call(
        body, name="attn_bwd", grid=(4, t // SUPER),
        in_specs=[cur, prev, cur, prev, cur, cur, cur, cur, pl.BlockSpec((128, 128), lambda h, s: (0, 0))],
        out_specs=[cur] * 5, out_shape=[SDS((t, ATTN_DIM), F32)] * 5,
        scratch_shapes=[big, big, one, one, half, half, half, half, big, big, one, one],
        compiler_params=_params("parallel", "arbitrary"))(qn, kd, kd, vd, vd, out, lse, dout, ones_pair)


def _ssd_consts():
    tri = np.tril(np.ones((CHUNK, CHUNK), np.float32))
    expand = np.zeros((128, SSM_INNER), np.float32)
    for h in range(SSM_HEADS):
        expand[h, h * HEAD_DIM:(h + 1) * HEAD_DIM] = 1.0
    return jnp.asarray(tri, BF16), jnp.asarray(tri.T, BF16), jnp.asarray(expand, BF16), jnp.asarray(expand.T, BF16)


def _conv4(x, halo, w_ref, b_ref):
    acc = b_ref[...] + w_ref[3:4, :] * x
    for k in range(3):
        acc = acc + w_ref[k:k + 1, :] * _shift_down(x, halo, 3 - k)
    return acc


def _softplus(x):
    return jnp.maximum(x, 0.0) + jnp.log(1.0 + jnp.exp(-jnp.abs(x)))


def _ssd_common(pre_x, pre_b, dt_ref, dtb_ref, alog_ref, tri_ref, exp_ref):
    xa = pre_x * _sigmoid(pre_x)
    ba = pre_b * _sigmoid(pre_b)
    dtv = _softplus(dt_ref[...] + dtb_ref[...])
    a_neg = -jnp.exp(alog_ref[...])
    acum = _hdot(tri_ref[...], dtv * a_neg, parts=3)
    lam = jnp.exp(acum)
    gam = jnp.exp(acum[CHUNK - 1:CHUNK, :] - acum)
    dt_e = _hdot(dtv, exp_ref[...])
    lam_e = _hdot(lam, exp_ref[...])
    gam_e = _hdot(gam, exp_ref[...])
    return dict(pre_x=pre_x, pre_b=pre_b, xa=xa, ba=ba, dtv=dtv, a_neg=a_neg, acum=acum,
                dt_e=dt_e, lam_e=lam_e, gam_e=gam_e, xdt=xa * dt_e)


def _decay(acum_t, h, transposed):
    rb = jnp.broadcast_to(acum_t[h:h + 1, :], (CHUNK, CHUNK))
    ri = lax.broadcasted_iota(jnp.int32, (CHUNK, CHUNK), 0)
    ci = lax.broadcasted_iota(jnp.int32, (CHUNK, CHUNK), 1)
    if transposed:
        return jnp.exp(jnp.where(ci >= ri, rb - rb.T, -jnp.inf))
    return jnp.exp(jnp.where(ri >= ci, rb.T - rb, -jnp.inf))


def _ssd_specs(t, rev):
    nc = t // CHUNK
    ch = (lambda c: nc - 1 - c) if rev else (lambda c: c)
    col = lambda w, j: pl.BlockSpec((CHUNK, w), lambda c: (ch(c), j))
    halo = lambda w, j: pl.BlockSpec((8, w), lambda c: (jnp.maximum(ch(c) * (CHUNK // 8) - 1, 0), j))
    return nc, ch, col, halo


def _ssd_fwd(proj, cwx, cbx, cwb, cbb, dtb, alog, dsk_e, norm_g, tri, expand):
    t = proj.shape[0]
    nc, _, col, halo = _ssd_specs(t, False)

    def body(z_ref, xs_ref, bc_ref, dt_ref, hx_ref, hb_ref, cwx_ref, cbx_ref, cwb_ref, cbb_ref, dtb_ref, alog_ref,
             dsk_ref, g_ref, tri_ref, exp_ref, y_ref, hs_ref, o_ref, px_ref, pb_ref, state):
        c = pl.program_id(0)

        @pl.when(c == 0)
        def _():
            state[...] = jnp.zeros_like(state)

        keep = (c > 0).astype(F32)
        pre_x = _conv4(xs_ref[...], hx_ref[...] * keep, cwx_ref, cbx_ref)
        pre_b = _conv4(bc_ref[...], hb_ref[...] * keep, cwb_ref, cbb_ref)
        px_ref[...] = pre_x.astype(BF16)
        pb_ref[...] = pre_b.astype(BF16)
        v = _ssd_common(pre_x, pre_b, dt_ref, dtb_ref, alog_ref, tri_ref, exp_ref)
        acum_t = v["acum"].T
        xdt, ba = v["xdt"], v["ba"]
        h_in = state[...]
        hs_ref[0] = h_in
        xg = xdt * v["gam_e"]
        m0 = lax.broadcasted_iota(jnp.int32, (CHUNK, 128), 1) < HEAD_DIM
        for g in range(2):
            bg = ba[:, g * 128:(g + 1) * 128].astype(BF16)
            cg = ba[:, 256 + g * 128:256 + (g + 1) * 128].astype(BF16)
            gl = slice(g * 512, (g + 1) * 512)
            cb = _dot(cg, bg, NT)
            y_off = _dot(cg, h_in[:, gl].astype(BF16)) * v["lam_e"][:, gl]
            s_new = _dot(bg.T, xg[:, gl].astype(BF16))
            state[:, gl] = h_in[:, gl] * v["lam_e"][CHUNK - 1:CHUNK, gl] + s_new
            for j in range(4):
                h0 = 8 * g + 2 * j
                ln = slice(g * 512 + j * 128, g * 512 + (j + 1) * 128)
                xp = xdt[:, ln].astype(BF16)
                y0 = _dot((cb * _decay(acum_t, h0, False)).astype(BF16), xp)
                y1 = _dot((cb * _decay(acum_t, h0 + 1, False)).astype(BF16), xp)
                y_ref[:, ln] = jnp.where(m0, y0, y1) + y_off[:, j * 128:(j + 1) * 128]
        z = z_ref[...]
        yg = (y_ref[...] + dsk_ref[...] * v["xa"]) * (z * _sigmoid(z))
        r = lax.rsqrt(jnp.mean(yg * yg, axis=-1, keepdims=True) + EPS)
        o_ref[...] = (yg * r * g_ref[...]).astype(BF16)

    full = lambda a: pl.BlockSpec(a.shape, lambda c: (0,) * a.ndim)
    return pl.pallas_call(
        body, name="ssd_fwd", grid=(nc,),
        in_specs=[col(1024, 1), col(1024, 2), col(512, 6), col(128, 28), halo(1024, 2), halo(512, 6),
                  full(cwx), full(cbx), full(cwb), full(cbb), full(dtb), full(alog), full(dsk_e), full(norm_g),
                  full(tri), full(expand)],
        out_specs=[pl.BlockSpec((CHUNK, SSM_INNER), lambda c: (c, 0)),
                   pl.BlockSpec((1, 128, SSM_INNER), lambda c: (c, 0, 0)),
                   pl.BlockSpec((CHUNK, SSM_INNER), lambda c: (c, 0)),
                   pl.BlockSpec((CHUNK, SSM_INNER), lambda c: (c, 0)), pl.BlockSpec((CHUNK, 512), lambda c: (c, 0))],
        out_shape=[SDS((t, SSM_INNER), F32), SDS((nc, 128, SSM_INNER), F32), SDS((t, SSM_INNER), BF16),
                   SDS((t, SSM_INNER), BF16), SDS((t, 512), BF16)],
        scratch_shapes=[pltpu.VMEM((128, SSM_INNER), F32)],
        compiler_params=_params("arbitrary"))(proj, proj, proj, proj, proj, proj, cwx, cbx, cwb, cbb, dtb, alog,
                                              dsk_e, norm_g, tri, expand)


def _ssd_bwd(proj, pre_x, pre_b, y_ssd, hs, dout, cwx, cwb, dtb, alog, dsk_e, norm_g, tri, triu, expand, expand_t):
    t = proj.shape[0]
    nc, ch, col, halo = _ssd_specs(t, True)

    def body(z_ref, xs_ref, bc_ref, dt_ref, px_ref, pb_ref, y_ref, hin_ref, do_ref,
             cwx_ref, cwb_ref, dtb_ref, alog_ref, dsk_ref, g_ref, tri_ref, triu_ref, exp_ref, expt_ref,
             dz_ref, dxs_ref, dbc_ref, ddt_ref, dg_ref, ddsk_ref, dalog_ref, ddtb_ref, dcwx_ref, dcbx_ref, dcwb_ref,
             dcbb_ref, gstate, nx_x, nx_b, dact_b, dxdt_s):
        step = pl.program_id(0)
        c = nc - 1 - step

        @pl.when(step == 0)
        def _():
            gstate[...] = jnp.zeros_like(gstate)
            nx_x[...] = jnp.zeros_like(nx_x)
            nx_b[...] = jnp.zeros_like(nx_b)
            for ref in (dg_ref, ddsk_ref, dalog_ref, ddtb_ref, dcwx_ref, dcbx_ref, dcwb_ref, dcbb_ref):
                ref[...] = jnp.zeros_like(ref)

        v = _ssd_common(px_ref[...].astype(F32), pb_ref[...].astype(F32), dt_ref, dtb_ref, alog_ref, tri_ref, exp_ref)
        acum_t = v["acum"].T
        xa, ba, xdt, dtv = v["xa"], v["ba"], v["xdt"], v["dtv"]
        lam_e, gam_e, dt_e = v["lam_e"], v["gam_e"], v["dt_e"]
        z = z_ref[...]
        y = y_ref[...]
        sz = _sigmoid(z)
        zs = z * sz
        y_tot = y + dsk_ref[...] * xa
        yg = y_tot * zs
        r = lax.rsqrt(jnp.mean(yg * yg, axis=-1, keepdims=True) + EPS)
        yh = yg * r
        do = do_ref[...]
        dg_ref[...] += jnp.sum(do * yh, axis=0, keepdims=True)
        gd = do * g_ref[...]
        dyg = r * (gd - yh * jnp.mean(gd * yh, axis=-1, keepdims=True))
        dz_ref[...] = (dyg * y_tot * (sz * (1.0 + z * (1.0 - sz)))).astype(BF16)
        dy = dyg * zs
        ddsk_ref[...] += jnp.sum(dy * xa, axis=0, keepdims=True)
        g_out = gstate[...]
        h_in = hin_ref[0]
        lam_dy = lam_e * dy
        gam_x = gam_e * xdt
        m0 = lax.broadcasted_iota(jnp.int32, (CHUNK, 128), 1) < HEAD_DIM
        lane = lax.broadcasted_iota(jnp.int32, (CHUNK, 128), 1)
        below = (lax.broadcasted_iota(jnp.int32, (CHUNK, CHUNK), 0) >
                 lax.broadcasted_iota(jnp.int32, (CHUNK, CHUNK), 1))
        da_in = jnp.zeros((CHUNK, 128), F32)
        off_y, off_x = [], []
        for g in range(2):
            bg = ba[:, g * 128:(g + 1) * 128].astype(BF16)
            cg = ba[:, 256 + g * 128:256 + (g + 1) * 128].astype(BF16)
            gl = slice(g * 512, (g + 1) * 512)
            gg = g_out[:, gl].astype(BF16)
            bc_t = _dot(bg, cg, NT)
            cb = _dot(cg, bg, NT)
            dxdt_off = _dot(bg, gg) * gam_e[:, gl]
            off_x.append(xdt[:, gl] * dxdt_off)
            off_y.append(dy[:, gl] * (_dot(cg, h_in[:, gl].astype(BF16)) * lam_e[:, gl]))
            q_sum = jnp.zeros((CHUNK, CHUNK), F32)
            for j in range(4):
                h0 = 8 * g + 2 * j
                ln = slice(g * 512 + j * 128, g * 512 + (j + 1) * 128)
                dyp = dy[:, ln]
                dyb = dyp.astype(BF16)
                xpb = xdt[:, ln].astype(BF16)
                d0 = _dot((bc_t * _decay(acum_t, h0, True)).astype(BF16), dyb)
                d1 = _dot((bc_t * _decay(acum_t, h0 + 1, True)).astype(BF16), dyb)
                dxdt_s[:, ln] = jnp.where(m0, d0, d1) + dxdt_off[:, j * 128:(j + 1) * 128]
                for hh, dym in ((h0, jnp.where(m0, dyp, 0.0)), (h0 + 1, jnp.where(m0, 0.0, dyp))):
                    qd = _dot(dym.astype(BF16), xpb, NT) * _decay(acum_t, hh, False)
                    q_sum = q_sum + qd
                    reach = jnp.where(below, _hdot(triu_ref[...], qd * cb), 0.0)
                    da_in = jnp.where(lane == hh, jnp.sum(reach, axis=-1, keepdims=True), da_in)
            gstate[:, gl] = g_out[:, gl] * lam_e[CHUNK - 1:CHUNK, gl] + _dot(cg.T, lam_dy[:, gl].astype(BF16))
            qb = q_sum.astype(BF16)
            dact_b[:, 256 + g * 128:256 + (g + 1) * 128] = (
                _dot(qb, bg) + _dot(lam_dy[:, gl].astype(BF16), h_in[:, gl].astype(BF16), NT))
            dact_b[:, g * 128:(g + 1) * 128] = _dot(qb.T, cg) + _dot(gam_x[:, gl].astype(BF16), gg, NT)
        dxdt = dxdt_s[...]
        seg_y = _hdot(jnp.concatenate(off_y, axis=1), expt_ref[...])
        seg_x = _hdot(jnp.concatenate(off_x, axis=1), expt_ref[...])
        e_col = jnp.sum(g_out * h_in * lam_e[CHUNK - 1:CHUNK, :], axis=0, keepdims=True)
        e_seg = _hdot(jnp.broadcast_to(e_col, (8, SSM_INNER)), expt_ref[...])[0:1, :]
        da = da_in + _hdot(triu_ref[...], seg_y) + (_hdot(tri_ref[...], seg_x) - seg_x) + e_seg
        a_neg = v["a_neg"]
        ddtv = da * a_neg + _hdot(dxdt * xa, expt_ref[...])
        dalog_ref[...] += jnp.sum(da * dtv, axis=0, keepdims=True) * a_neg
        lane16 = lax.broadcasted_iota(jnp.int32, (CHUNK, 128), 1) < SSM_HEADS
        draw = jnp.where(lane16, ddtv * _sigmoid(dt_ref[...] + dtb_ref[...]), 0.0)
        ddtb_ref[...] += jnp.sum(draw, axis=0, keepdims=True)
        ddt_ref[...] = draw.astype(BF16)
        dxa = dxdt * dt_e + dy * dsk_ref[...]
        for (dact, pre, x_ref, nx, cw_ref, dcw_ref, dcb_ref, dx_ref) in (
                (dxa, v["pre_x"], xs_ref, nx_x, cwx_ref, dcwx_ref, dcbx_ref, dxs_ref),
                (dact_b[...], v["pre_b"], bc_ref, nx_b, cwb_ref, dcwb_ref, dcbb_ref, dbc_ref)):
            sp = _sigmoid(pre)
            dpre = dact * (sp * (1.0 + pre * (1.0 - sp)))
            dcb_ref[...] += jnp.sum(dpre, axis=0, keepdims=True)
            xv = x_ref[...]
            nxt = nx[...]
            dx = cw_ref[3:4, :] * dpre
            dcw_ref[3:4, :] += jnp.sum(dpre * xv, axis=0, keepdims=True)
            for k in range(3):
                d_up = _shift_up(dpre, nxt, 3 - k)
                dcw_ref[k:k + 1, :] += jnp.sum(xv * d_up, axis=0, keepdims=True)
                dx = dx + cw_ref[k:k + 1, :] * d_up
            nx[...] = dpre[0:8, :]
            dx_ref[...] = dx.astype(dx_ref.dtype)

    full = lambda a: pl.BlockSpec(a.shape, lambda c: (0,) * a.ndim)
    rowblk = lambda w: pl.BlockSpec((CHUNK, w), lambda c: (ch(c), 0))
    acc = lambda a, b: pl.BlockSpec((a, b), lambda c: (0, 0))
    return pl.pallas_call(
        body, name="ssd_bwd", grid=(nc,),
        in_specs=[col(1024, 1), col(1024, 2), col(512, 6), col(128, 28), rowblk(SSM_INNER), rowblk(512),
                  rowblk(SSM_INNER),
                  pl.BlockSpec((1, 128, SSM_INNER), lambda c: (ch(c), 0, 0)),
                  rowblk(SSM_INNER),
                  full(cwx), full(cwb), full(dtb), full(alog), full(dsk_e), full(norm_g),
                  full(tri), full(triu), full(expand), full(expand_t)],
        out_specs=[rowblk(SSM_INNER), rowblk(SSM_INNER), rowblk(512), rowblk(128),
                   acc(1, 1024), acc(1, 1024), acc(1, 128), acc(1, 128), acc(4, 1024), acc(1, 1024), acc(4, 512),
                   acc(1, 512)],
        out_shape=[SDS((t, SSM_INNER), BF16), SDS((t, SSM_INNER), BF16), SDS((t, 512), BF16), SDS((t, 128), BF16),
                   SDS((1, 1024), F32), SDS((1, 1024), F32), SDS((1, 128), F32), SDS((1, 128), F32),
                   SDS((4, 1024), F32), SDS((1, 1024), F32), SDS((4, 512), F32), SDS((1, 512), F32)],
        scratch_shapes=[pltpu.VMEM((128, SSM_INNER), F32), pltpu.VMEM((8, 1024), F32), pltpu.VMEM((8, 512), F32),
                        pltpu.VMEM((CHUNK, 512), F32), pltpu.VMEM((CHUNK, SSM_INNER), F32)],
        compiler_params=_params("arbitrary"))(proj, proj, proj, proj, pre_x, pre_b, y_ssd, hs, dout,
                                              cwx, cwb, dtb, alog, dsk_e, norm_g, tri, triu, expand, expand_t)


def _conv3(x, halo, w_ref, b_ref, part):
    acc = b_ref[part] + w_ref[2, part] * x
    for k in range(2):
        acc = acc + w_ref[k, part] * _shift_down(x, halo, 2 - k)
    return acc


def _up_act(x, g, w_up_t, cw, cb, tm=2048, tn=256, tr=512):
    t, k = x.shape
    nj = D_FF // tn

    def body(x_ref, g_ref, wg_ref, wv_ref, w_ref, b_ref, u_ref, c_ref, h_ref, f_ref, halo):
        i, j = pl.program_id(0), pl.program_id(1)

        @pl.when(j == 0)
        def _():
            xv = x_ref[...]
            r = lax.rsqrt(jnp.mean(xv * xv, axis=-1, keepdims=True) + EPS)
            h_ref[...] = (xv * r * g_ref[...]).astype(BF16)

        @pl.when(i == 0)
        def _():
            halo[j] = jnp.zeros((2, 8, tn), F32)

        def matmuls(r):
            rows = slice(r * tr, (r + 1) * tr)
            return [_dot(h_ref[rows, :], wt_ref[...], NT) for wt_ref in (wg_ref, wv_ref)]

        def epilogue(r, us, before):
            rows = slice(r * tr, (r + 1) * tr)
            parts = []
            for part, u in enumerate(us):
                u_ref[part, rows, :] = u.astype(BF16)
                parts.append(_conv3(u, before[part], w_ref, b_ref, part))
                c_ref[part, rows, :] = parts[-1].astype(BF16)
            gate, val = parts
            f_ref[rows, :] = (gate * _sigmoid(gate) * val).astype(BF16)
            return [u[tr - 8:, :] for u in us]

        before = [halo[j, 0], halo[j, 1]]
        pending = matmuls(0)
        for r in range(tm // tr):
            ahead = matmuls(r + 1) if r + 1 < tm // tr else None
            before = epilogue(r, pending, before)
            pending = ahead
        halo[j, 0], halo[j, 1] = before

    return pl.pallas_call(
        body, name="up_proj", grid=(t // tm, nj),
        in_specs=[pl.BlockSpec((tm, k), lambda i, j: (i, 0)), pl.BlockSpec((1, k), lambda i, j: (0, 0)),
                  pl.BlockSpec((tn, k), lambda i, j: (j, 0)), pl.BlockSpec((tn, k), lambda i, j: (j + nj, 0)),
                  pl.BlockSpec((3, 2, 1, tn), lambda i, j: (0, 0, 0, j)), pl.BlockSpec((2, 1, tn), lambda i, j: (0, 0, j))],
        out_specs=[pl.BlockSpec((2, tm, tn), lambda i, j: (0, i, j)), pl.BlockSpec((2, tm, tn), lambda i, j: (0, i, j)),
                   pl.BlockSpec((tm, k), lambda i, j: (i, 0)), pl.BlockSpec((tm, tn), lambda i, j: (i, j))],
        out_shape=[SDS((2, t, D_FF), BF16), SDS((2, t, D_FF), BF16), SDS((t, k), BF16), SDS((t, D_FF), BF16)],
        scratch_shapes=[pltpu.VMEM((nj, 2, 8, tn), F32)],
        compiler_params=_params("arbitrary", "arbitrary"))(x, g, w_up_t, w_up_t, cw, cb)


def _ffn_bwd(dx2, w_down, u, c, cw, tm=512, tn=1408):
    t = u.shape[1]
    nj, ni = D_FF // tn, t // tm
    rev = lambda i: ni - 1 - i

    def body(dx_ref, wd_ref, u_ref, c_ref, w_ref, du_ref, dcw_ref, dcb_ref, nxt):
        i = pl.program_id(1)

        @pl.when(i == 0)
        def _():
            nxt[...] = jnp.zeros_like(nxt)
            dcw_ref[...] = jnp.zeros_like(dcw_ref)
            dcb_ref[...] = jnp.zeros_like(dcb_ref)

        df = _dot(dx_ref[...].astype(BF16), wd_ref[...], NT)
        gate, val = c_ref[0].astype(F32), c_ref[1].astype(F32)
        sg = _sigmoid(gate)
        dgate = df * val * (sg * (1.0 + gate * (1.0 - sg)))
        dval = df * (gate * sg)
        for part, d in enumerate((dgate, dval)):
            uu = u_ref[part].astype(F32)
            dcb_ref[part] += jnp.sum(d, axis=0, keepdims=True)
            ahead = nxt[part]
            acc = w_ref[2, part] * d
            dcw_ref[2, part] += jnp.sum(d * uu, axis=0, keepdims=True)
            for k in range(2):
                d_up = _shift_up(d, ahead, 2 - k)
                dcw_ref[k, part] += jnp.sum(uu * d_up, axis=0, keepdims=True)
                acc = acc + w_ref[k, part] * d_up
            nxt[part] = d[0:8, :]
            du_ref[part] = acc.astype(BF16)

    w_spec = pl.BlockSpec((3, 2, 1, tn), lambda j, i: (0, 0, 0, j))
    b_spec = pl.BlockSpec((2, 1, tn), lambda j, i: (0, 0, j))
    tile = pl.BlockSpec((2, tm, tn), lambda j, i: (0, rev(i), j))
    return pl.pallas_call(
        body, name="ffn_bwd", grid=(nj, ni),
        in_specs=[pl.BlockSpec((tm, D_MODEL), lambda j, i: (rev(i), 0)), pl.BlockSpec((tn, D_MODEL), lambda j, i: (j, 0)),
                  tile, tile, w_spec],
        out_specs=[tile, w_spec, b_spec],
        out_shape=[SDS((2, t, D_FF), BF16), SDS((3, 2, 1, D_FF), F32), SDS((2, 1, D_FF), F32)],
        scratch_shapes=[pltpu.VMEM((2, 8, tn), F32)],
        compiler_params=_params("parallel", "arbitrary"))(dx2, w_down, u, c, cw)


def _ple_loss(x2, g, w_gate, p, w_proj_t, target, tm=512):
    t = x2.shape[0]

    def body(x_ref, g_ref, wg_ref, p_ref, wp_ref, tg_ref, dx_ref, dpre_ref, dpp_ref, h_ref, loss_ref, dg_ref):
        i = pl.program_id(0)
        xv = x_ref[...]
        r = lax.rsqrt(jnp.mean(xv * xv, axis=-1, keepdims=True) + EPS)
        xh = xv * r
        h = (xh * g_ref[...]).astype(BF16)
        h_ref[...] = h
        gate = _sigmoid(_dot(h, wg_ref[...]))
        pp = _dot(p_ref[...].astype(BF16), wp_ref[...], NT)
        err = (xv + gate * pp) - tg_ref[...]

        @pl.when(i == 0)
        def _():
            loss_ref[...] = jnp.zeros_like(loss_ref)
            dg_ref[...] = jnp.zeros_like(dg_ref)

        loss_ref[...] += 0.5 * jnp.sum(jnp.mean(err * err, axis=-1, keepdims=True), axis=0, keepdims=True)
        dy = err * (1.0 / D_MODEL)
        dpre = (dy * pp * gate * (1.0 - gate)).astype(BF16)
        dpre_ref[...] = dpre
        dpp_ref[...] = (dy * gate).astype(BF16)
        dh = _dot(dpre, wg_ref[...], NT)
        dg_ref[...] += jnp.sum(dh * xh, axis=0, keepdims=True)
        gd = dh * g_ref[...]
        dx_ref[...] = dy + r * (gd - xh * jnp.mean(gd * xh, axis=-1, keepdims=True))

    row = lambda w: pl.BlockSpec((tm, w), lambda i: (i, 0))
    full = lambda a: pl.BlockSpec(a.shape, lambda i: (0, 0))
    return pl.pallas_call(
        body, name="ple_loss", grid=(t // tm,),
        in_specs=[row(D_MODEL), full(g), full(w_gate), row(PLE_DIM), full(w_proj_t), row(D_MODEL)],
        out_specs=[row(D_MODEL), row(D_MODEL), row(D_MODEL), row(D_MODEL),
                   pl.BlockSpec((1, 128), lambda i: (0, 0)), pl.BlockSpec((1, D_MODEL), lambda i: (0, 0))],
        out_shape=[SDS((t, D_MODEL), F32), SDS((t, D_MODEL), BF16), SDS((t, D_MODEL), BF16), SDS((t, D_MODEL), BF16),
                   SDS((1, 128), F32), SDS((1, D_MODEL), F32)],
        compiler_params=_params("arbitrary"))(x2, g, w_gate, p, w_proj_t, target)


def _all_gather(arrays, name):
    n_a = len(arrays)

    def body(*refs):
        src, dst = refs[:n_a], refs[n_a:2 * n_a]
        send_sems, recv_sems, local_sems = refs[2 * n_a:]
        x, y, c = lax.axis_index("x"), lax.axis_index("y"), lax.axis_index("c")
        slot = lambda px, py, pc: 4 * px + 2 * py + pc
        me, sibling = (x, y, c), (x, y, 1 - c)
        chips = [(1 - x, y), (x, 1 - y), (1 - x, 1 - y)]

        def copy(a, k, block, to, own=False):
            return pltpu.make_async_remote_copy(
                src_ref=src[a] if own else dst[a].at[slot(*block)], dst_ref=dst[a].at[slot(*block)],
                send_sem=send_sems.at[a, k], recv_sem=recv_sems.at[a, k], device_id=to,
                device_id_type=pl.DeviceIdType.MESH)

        local = [pltpu.make_async_copy(src[a], dst[a].at[slot(*me)], local_sems.at[a]) for a in range(n_a)]
        for cp in local:
            cp.start()
        sends = []
        for a in range(n_a):
            sends.append(copy(a, 0, me, sibling, own=True))
            sends += [copy(a, 1 + j, me, (*chip, c), own=True) for j, chip in enumerate(chips)]
        for cp in sends:
            cp.start()
        for j, chip in enumerate(chips):
            for a in range(n_a):
                copy(a, 1 + j, (*chip, c), me).wait_recv()
                passed = copy(a, 4 + j, (*chip, c), sibling)
                passed.start()
                sends.append(passed)
        for a in range(n_a):
            copy(a, 0, sibling, me).wait_recv()
            for j, chip in enumerate(chips):
                copy(a, 4 + j, (*chip, 1 - c), me).wait_recv()
        for cp in sends:
            cp.wait_send()
        for cp in local:
            cp.wait()

    hbm = pl.BlockSpec(memory_space=pl.ANY)
    return pl.pallas_call(
        body, name=name, in_specs=[hbm] * n_a, out_specs=[hbm] * n_a,
        out_shape=[SDS((N_DEV,) + a.shape, a.dtype) for a in arrays],
        scratch_shapes=[pltpu.SemaphoreType.DMA((n_a, N_DEV - 1)), pltpu.SemaphoreType.DMA((n_a, N_DEV - 1)),
                        pltpu.SemaphoreType.DMA((n_a,))],
        )(*arrays)


def _peer(k):
    x, y, c = lax.axis_index("x"), lax.axis_index("y"), lax.axis_index("c")
    px = 1 - x if k & 4 else x
    py = 1 - y if k & 2 else y
    pc = 1 - c if k & 1 else c
    return (px, py, pc), 4 * px + 2 * py + pc


_HBM = pl.BlockSpec(memory_space=pltpu.HBM)
_SEM = pl.BlockSpec(memory_space=pltpu.SEMAPHORE)


def _split_copies(src, land, send_sems, recv_sems, scatter, arrivals):
    _, me = _peer(0)
    out = []
    for k in range(1, N_DEV):
        coords, peer = _peer(k)
        for a in range(len(src)):
            sem = a * (N_DEV - 1) + k - 1
            if scatter[a]:
                s, d = src[a].at[peer], land[a].at[k]
            else:
                s, d = src[a], land[a].at[peer if arrivals else me]
            out.append(pltpu.make_async_remote_copy(
                src_ref=s, dst_ref=d, send_sem=send_sems.at[sem], recv_sem=recv_sems.at[sem], device_id=coords,
                device_id_type=pl.DeviceIdType.MESH))
    return out


def _exchange_start(srcs, lands, scatter, name):
    n = len(srcs)

    def body(*refs):
        src, land = refs[:n], refs[n:2 * n]
        send_sems, recv_sems = refs[2 * n], refs[2 * n + 1]
        token = refs[-1]
        for cp in _split_copies(src, land, send_sems, recv_sems, scatter, False):
            cp.start()
        token[...] = jnp.zeros_like(token)

    hbm_shape = lambda a: pltpu.HBM(a.shape, a.dtype)
    sem_shape = pltpu.SemaphoreType.DMA((n * (N_DEV - 1),))
    outs = pl.pallas_call(
        body, name=name,
        out_shape=(sem_shape, sem_shape, *[hbm_shape(a) for a in srcs], *[hbm_shape(a) for a in lands],
                   SDS((8, 128), F32)),
        in_specs=[_HBM] * (2 * n), out_specs=(_SEM, _SEM, *[_HBM] * (2 * n), pl.BlockSpec(memory_space=pltpu.VMEM)),
        input_output_aliases={a: 2 + a for a in range(2 * n)},
        compiler_params=pltpu.CompilerParams(has_side_effects=pltpu.SideEffectType.DATAFLOW_SIDE_EFFECTING),
    )(*[pltpu.with_memory_space_constraint(a, pltpu.HBM) for a in list(srcs) + list(lands)])
    return outs[0], outs[1], outs[2:2 + n], outs[2 + n:2 + 2 * n], outs[-1]


def _exchange_wait(send_sems, recv_sems, srcs, lands, scatter, after, name):
    n = len(srcs)

    def body(*refs):
        src, land = refs[:n], refs[n:2 * n]
        for cp in _split_copies(src, land, refs[2 * n], refs[2 * n + 1], scatter, False):
            cp.wait_send()
        for cp in _split_copies(src, land, refs[2 * n], refs[2 * n + 1], scatter, True):
            cp.wait_recv()

    hbm_shape = lambda a: pltpu.HBM(a.shape, a.dtype)
    outs = pl.pallas_call(
        body, name=name, out_shape=tuple(hbm_shape(a) for a in list(srcs) + list(lands)),
        in_specs=[_HBM] * (2 * n) + [_SEM, _SEM, pl.BlockSpec(memory_space=pl.ANY)], out_specs=(_HBM,) * (2 * n),
        input_output_aliases={a: a for a in range(2 * n)},
        compiler_params=pltpu.CompilerParams(has_side_effects=pltpu.SideEffectType.DATAFLOW_SIDE_EFFECTING),
    )(*srcs, *lands, send_sems, recv_sems, after)
    return outs[:n], outs[n:]


def _reduce8(a, tr, name):
    _, rows, cols = a.shape

    def body(a_ref, o_ref):
        acc = a_ref[0]
        for j in range(1, N_DEV):
            acc = acc + a_ref[j]
        o_ref[...] = acc

    return pl.pallas_call(
        body, name=name, grid=(rows // tr,),
        in_specs=[pl.BlockSpec((N_DEV, tr, cols), lambda i: (0, i, 0))],
        out_specs=pl.BlockSpec((tr, cols), lambda i: (i, 0)), out_shape=SDS((rows, cols), F32),
        compiler_params=_params("parallel"))(a)


def _reduce_landed(own, land, name, tc=256):
    rows, cols = own.shape

    def body(own_ref, land_ref, o_ref):
        acc = own_ref[...].astype(F32)
        for k in range(1, N_DEV):
            acc = acc + land_ref[k].astype(F32)
        o_ref[...] = acc

    return pl.pallas_call(
        body, name=name, grid=(cols // tc,),
        in_specs=[pl.BlockSpec((rows, tc), lambda j: (0, j)), pl.BlockSpec((N_DEV, rows, tc), lambda j: (0, 0, j))],
        out_specs=pl.BlockSpec((rows, tc), lambda j: (0, j)), out_shape=SDS((rows, cols), F32),
        compiler_params=_params("parallel"))(own, land)


def _adamw(w, g, m, v, name, tr=None):
    rows, cols = w.shape
    tr = rows if tr is None else tr

    def body(w_ref, g_ref, m_ref, v_ref, d_ref, mo_ref, vo_ref):
        d_ref[...], mo_ref[...], vo_ref[...] = _adam_update(w_ref[...], g_ref[...], m_ref[...], v_ref[...])

    blk = pl.BlockSpec((tr, cols), lambda i: (i, 0))
    return pl.pallas_call(
        body, name=name, grid=(rows // tr,), in_specs=[blk] * 4, out_specs=[blk] * 3,
        out_shape=[SDS((rows, cols), F32)] * 3, compiler_params=_params("parallel"))(w, g, m, v)


def _pad_rows(a, rows):
    return jnp.pad(a, ((0, rows - a.shape[0]),) + ((0, 0),) * (a.ndim - 1))


def _local_step(x, p, target, sm, wts, fetch_rest, send, tok):
    ones_q, ones_k, dup, dup_t = _head_consts()
    tri, triu, expand, expand_t = _ssd_consts()
    w_in_t = wts["in_t"]
    cwx, cwb = wts["ssm_cw"][:, :SSM_INNER], wts["ssm_cw"][:, SSM_INNER:]
    cbx, cbb = sm["ssm_conv_b"][:, :SSM_INNER], sm["ssm_conv_b"][:, SSM_INNER:]
    pad128 = lambda a: jnp.pad(a, ((0, 0), (0, 128 - a.shape[1])))
    dtb, alog = pad128(sm["dt_bias"]), pad128(sm["a_log"])
    dsk_e = jnp.repeat(sm["d_skip"], HEAD_DIM, axis=1)
    gq = jnp.tile(sm["q_norm_g"], (1, ATTN_DIM // HEAD_DIM))
    gk = jnp.tile(sm["k_norm_g"], (1, KV_DIM // HEAD_DIM))
    ffn_cw = wts["ffn_cw"].reshape(3, 2, 1, D_FF)
    ffn_cb = sm["ffn_conv_b"].reshape(2, 1, D_FF)

    proj, h1 = _norm_matmul(x, sm["attn_norm_g"] + tok, w_in_t, 1024, 768, "in_proj")
    qn, kd, vd = _attn_prep(proj, gq, gk, ones_q, ones_k, dup)
    attn_out, lse = _attn_fwd(qn, kd, vd)
    y_ssd, hs, ssm_out, pre_x, pre_b = _ssd_fwd(proj, cwx, cbx, cwb, cbb, dtb, alog, dsk_e, sm["ssm_norm_g"], tri,
                                                expand)
    rest = fetch_rest(ssm_out)
    w_out, w_up_t, w_down, w_gate, w_proj_t = (rest[k] for k in ("out", "up_t", "down", "gate", "proj_t"))
    x1 = _mm_resid([(attn_out, None, w_out[:ATTN_DIM]), (ssm_out, None, w_out[ATTN_DIM:])], x, None, 512, F32,
                   "out_proj")
    u, uc, h2, f = _up_act(x1, sm["ffn_norm_g"], w_up_t, ffn_cw, ffn_cb)
    x2 =_mm_resid([(f, None, w_down)], x1, None, 512, F32, "down_proj")
    dx2, dpre, dpp, h3, loss, dg_ple = _ple_loss(x2, sm["ple_norm_g"], w_gate, p, w_proj_t, target)

    g_gate = _wgrad(h3, None, dpre, "wg_gate")
    g_proj_t = _wgrad(dpp, None, p, "wg_proj")
    g_down = _wgrad(f, None, dx2, "wg_down")
    du, d_ffn_cw, d_ffn_cb = _ffn_bwd(dx2, w_down, u, uc, ffn_cw)
    dx1, dg_ffn = _mm_normbwd([(du, 0, w_up_t, D_FF, 0), (du, 1, w_up_t, D_FF, 1)], x1, sm["ffn_norm_g"], dx2, 512,
                              "up_proj_bwd")
    g_up_t = _wgrad(du, "all", h2, "wg_up")
    tok = send(dict(gate=g_gate, proj_t=g_proj_t, down=g_down, up_t=g_up_t)).astype(BF16)
    d_attn = _mm_resid([(dx1, None, w_out[:ATTN_DIM] + tok)], None, NT, 512, F32, "out_proj_bwd_attn")
    d_ssm = _mm_resid([(dx1, None, w_out[ATTN_DIM:] + tok)], None, NT, 512, F32, "out_proj_bwd_ssm")
    g_out = jnp.concatenate([_wgrad(attn_out, None, dx1, "wg_out_attn"), _wgrad(ssm_out, None, dx1, "wg_out_ssm")],
                            axis=0)
    tok = send(dict(out=g_out))
    (dz, dxs, dbc, ddt, dg_ssm, d_dsk_e, d_alog, d_dtb, d_cwx, d_cbx, d_cwb, d_cbb) = _ssd_bwd(
        proj, pre_x, pre_b, y_ssd, hs, d_ssm, cwx, cwb, dtb + tok, alog, dsk_e, sm["ssm_norm_g"], tri, triu, expand,
        expand_t)
    dqn, dkc, dkp, dvc, dvp = _attn_bwd(qn, kd, vd, attn_out, lse, d_attn, ones_k[:128, :128])
    dqkv, dgq, dgk = _attn_prep_bwd(proj, dqn, dkc, dkp, dvc, dvp, gq + tok, gk, ones_q, ones_k, dup_t)
    pieces = [(dqkv, 0, 1024), (dz, 1024, 2048), (dxs, 2048, 3072), (dbc, 3072, 3584), (ddt, 3584, 3712)]
    g_in_t = jnp.concatenate([_wgrad(a, None, h1, "wg_in_%d" % lo) for a, lo, _ in pieces], axis=0)[:IN_PROJ]
    tok = send(dict(in_t=g_in_t))
    grad_x, dg_attn = _mm_normbwd([(a, None, w_in_t, hi - lo, lo // (hi - lo)) for a, lo, hi in pieces], x,
                                  sm["attn_norm_g"] + tok, dx1, 512, "in_proj_bwd")

    small = dict(
        attn_norm_g=dg_attn, q_norm_g=dgq.reshape(-1, HEAD_DIM).sum(0, keepdims=True),
        k_norm_g=dgk.reshape(-1, HEAD_DIM).sum(0, keepdims=True),
        ssm_conv_w=jnp.concatenate([d_cwx, d_cwb], axis=1), ssm_conv_b=jnp.concatenate([d_cbx, d_cbb], axis=1),
        dt_bias=d_dtb[:, :SSM_HEADS], a_log=d_alog[:, :SSM_HEADS],
        d_skip=d_dsk_e.reshape(SSM_HEADS, HEAD_DIM).sum(1)[None, :], ssm_norm_g=dg_ssm, ffn_norm_g=dg_ffn,
        ffn_conv_w=d_ffn_cw.reshape(3, 2 * D_FF), ffn_conv_b=d_ffn_cb.reshape(1, 2 * D_FF), ple_norm_g=dg_ple)
    return loss[0, 0], grad_x, small


_SMALL = (("attn_norm_g", 1, 1024), ("q_norm_g", 1, 64), ("k_norm_g", 1, 64), ("ssm_conv_w", 4, XBC_DIM),
          ("ssm_conv_b", 1, XBC_DIM), ("dt_bias", 1, 16), ("a_log", 1, 16), ("d_skip", 1, 16), ("ssm_norm_g", 1, 1024),
          ("ffn_norm_g", 1, 1024), ("ffn_conv_w", 3, 2 * D_FF), ("ffn_conv_b", 1, 2 * D_FF), ("ple_norm_g", 1, 1024))
_SMALL_ROWS, _SMALL_COLS = 32, XBC_DIM
_SHARDED_SMALL = ("ssm_conv_w", "ffn_conv_w")


def _small_chunks(n):
    return 1 if n <= _SMALL_COLS else 4


def _pack_small(parts, loss):
    rows = []
    for k, r, n in _SMALL:
        c = _small_chunks(n)
        rows.append(jnp.pad(parts[k].reshape(r * c, n // c), ((0, 0), (0, _SMALL_COLS - n // c))))
    packed = _pad_rows(jnp.concatenate(rows, axis=0), _SMALL_ROWS)
    at_loss = ((lax.broadcasted_iota(jnp.int32, packed.shape, 0) == _SMALL_ROWS - 1) &
               (lax.broadcasted_iota(jnp.int32, packed.shape, 1) == 0))
    return jnp.where(at_loss, loss, packed)


def _adam_update(w, g, m, v):
    c1 = 1.0 - ADAM_B1 ** ADAM_STEP
    c2 = 1.0 - ADAM_B2 ** ADAM_STEP
    mn = ADAM_B1 * m + (1.0 - ADAM_B1) * g
    vn = ADAM_B2 * v + (1.0 - ADAM_B2) * (g * g)
    return -ADAM_LR * ((mn / c1) / (jnp.sqrt(vn / c2) + ADAM_EPS) + ADAM_WD * w), mn, vn


def _adamw_small(g_all, g_shard, w, m, v):
    ins, shapes = [g_all], []
    for k, _, _ in _SMALL:
        shape2 = w[k].shape if w[k].ndim == 2 else (1, w[k].shape[0])
        shapes.append(shape2)
        ins += ([g_shard[k]] if k in _SHARDED_SMALL else []) + [a.reshape(shape2) for a in (w[k], m[k], v[k])]

    def body(*refs):
        g_ref, pos, row = refs[0], 1, 0
        outs = refs[len(ins):]
        for i, (k, r, n) in enumerate(_SMALL):
            c = _small_chunks(n)
            if k in _SHARDED_SMALL:
                g = refs[pos][...]
                pos += 1
            elif c == 1:
                g = g_ref[row:row + r, 0:n]
            else:
                g = jnp.concatenate([g_ref[row + j:row + j + 1, 0:n // c] for j in range(c)], axis=1)
            row += r * c
            d, mn, vn = _adam_update(refs[pos][...], g, refs[pos + 1][...], refs[pos + 2][...])
            pos += 3
            for o_ref, val in zip(outs[4 * i:4 * i + 4], (g, d, mn, vn)):
                o_ref[...] = val

    res = pl.pallas_call(body, name="adamw_small",
                         out_shape=[SDS(s, F32) for s in shapes for _ in range(4)])(*ins)
    return {k: tuple(a.reshape(w[k].shape) for a in res[4 * i:4 * i + 4]) for i, (k, _, _) in enumerate(_SMALL)}


def kernel(x, p, attn_norm_g, w_in, q_norm_g, k_norm_g, ssm_conv_w, ssm_conv_b, dt_bias, a_log, d_skip, ssm_norm_g, w_out, ffn_norm_g, w_up, ffn_conv_w, ffn_conv_b, w_down, ple_norm_g, w_ple_gate, w_ple_proj, loss_target, m_attn_norm_g, m_w_in, m_q_norm_g, m_k_norm_g, m_ssm_conv_w, m_ssm_conv_b, m_dt_bias, m_a_log, m_d_skip, m_ssm_norm_g, m_w_out, m_ffn_norm_g, m_w_up, m_ffn_conv_w, m_ffn_conv_b, m_w_down, m_ple_norm_g, m_w_ple_gate, m_w_ple_proj, v_attn_norm_g, v_w_in, v_q_norm_g, v_k_norm_g, v_ssm_conv_w, v_ssm_conv_b, v_dt_bias, v_a_log, v_d_skip, v_ssm_norm_g, v_w_out, v_ffn_norm_g, v_w_up, v_ffn_conv_w, v_ffn_conv_b, v_w_down, v_ple_norm_g, v_w_ple_gate, v_w_ple_proj):
    names = ("attn_norm_g", "w_in", "q_norm_g", "k_norm_g", "ssm_conv_w", "ssm_conv_b", "dt_bias", "a_log", "d_skip",
             "ssm_norm_g", "w_out", "ffn_norm_g", "w_up", "ffn_conv_w", "ffn_conv_b", "w_down", "ple_norm_g",
             "w_ple_gate", "w_ple_proj")
    w = dict(zip(names, (attn_norm_g, w_in, q_norm_g, k_norm_g, ssm_conv_w, ssm_conv_b, dt_bias, a_log, d_skip,
                         ssm_norm_g, w_out, ffn_norm_g, w_up, ffn_conv_w, ffn_conv_b, w_down, ple_norm_g, w_ple_gate,
                         w_ple_proj)))
    m = dict(zip(names, (m_attn_norm_g, m_w_in, m_q_norm_g, m_k_norm_g, m_ssm_conv_w, m_ssm_conv_b, m_dt_bias,
                         m_a_log, m_d_skip, m_ssm_norm_g, m_w_out, m_ffn_norm_g, m_w_up, m_ffn_conv_w, m_ffn_conv_b,
                         m_w_down, m_ple_norm_g, m_w_ple_gate, m_w_ple_proj)))
    v = dict(zip(names, (v_attn_norm_g, v_w_in, v_q_norm_g, v_k_norm_g, v_ssm_conv_w, v_ssm_conv_b, v_dt_bias,
                         v_a_log, v_d_skip, v_ssm_norm_g, v_w_out, v_ffn_norm_g, v_w_up, v_ffn_conv_w, v_ffn_conv_b,
                         v_w_down, v_ple_norm_g, v_w_ple_gate, v_w_ple_proj)))
    w, m, v = ({k: a[0] for k, a in d.items()} for d in (w, m, v))
    me = 4 * lax.axis_index("x") + 2 * lax.axis_index("y") + lax.axis_index("c")

    mine = dict(in_t=w["w_in"].T, out=w["w_out"], up_t=w["w_up"].T, down=w["w_down"], gate=w["w_ple_gate"],
                proj_t=w["w_ple_proj"].T)
    mine = {k: a.astype(BF16) for k, a in mine.items()}
    conv_pack = jnp.pad(jnp.concatenate([w["ssm_conv_w"].reshape(-1), w["ffn_conv_w"].reshape(-1)]),
                        (0, 3072 - 2880)).reshape(8, 384)
    all_in, all_conv = _all_gather([mine["in_t"], conv_pack], "gather_first")
    later = ("out", "up_t", "down", "gate", "proj_t")
    zones = [lax.dynamic_update_slice(lax.empty((N_DEV,) + mine[k].shape, BF16), mine[k][None], (me, 0, 0))
             for k in later]
    zones, all_in, all_conv = lax.optimization_barrier((zones, all_in, all_conv))
    rest_state = _exchange_start([mine[k] for k in later], zones, [False] * len(later), "gather_rest_start")

    def fetch_rest(after):
        _, landed = _exchange_wait(*rest_state[:4], [False] * len(later), after, "gather_rest_wait")
        return {k: a.reshape(N_DEV * a.shape[1], a.shape[2]) for k, a in zip(later, landed)}

    wts = dict(in_t=_pad_rows(all_in.reshape(IN_PROJ, D_MODEL), IN_PROJ_PAD))
    conv_flat = all_conv.reshape(N_DEV, 3072)
    wts["ssm_cw"] = conv_flat[:, :768].reshape(N_DEV, 4, XBC_DIM // N_DEV).transpose(1, 0, 2).reshape(4, XBC_DIM)
    wts["ffn_cw"] = conv_flat[:, 768:2880].reshape(N_DEV, 3, 2 * D_FF // N_DEV).transpose(1, 0, 2).reshape(3, 2 * D_FF)
    sm = {k: w[k].reshape(1, -1) for k, _, _ in _SMALL if k not in _SHARDED_SMALL}

    in_flight = []

    def send(grads):
        keys = sorted(grads)
        srcs = [grads[k].reshape(N_DEV, grads[k].shape[0] // N_DEV, grads[k].shape[1]) for k in keys]
        state = _exchange_start(srcs, [lax.empty(a.shape, BF16) for a in srcs], [True] * len(keys),
                                "send_" + "_".join(keys))
        in_flight.append((keys, state))
        return state[4][0:1, 0:1]

    loss, grad_x, small = _local_step(x[0], p[0, 0], loss_target[0], sm, wts, fetch_rest, send,
                                      rest_state[4][0:1, 0:1])

    (got_small,) = _all_gather([_pack_small(small, loss)], "gather_small_grads")
    g_small = _reduce8(got_small, _SMALL_ROWS, "reduce_small")
    loss = g_small[_SMALL_ROWS - 1, 0]
    grads = {}
    for keys, state in in_flight:
        sent, landed = _exchange_wait(*state[:4], [True] * len(keys), grad_x, "wait_" + "_".join(keys))
        for k, shares, land in zip(keys, sent, landed):
            grads[k] = _reduce_landed(lax.dynamic_index_in_dim(shares, me, 0, keepdims=False), land, "reduce_" + k)
    gw = {"w_in": grads["in_t"].T, "w_out": grads["out"], "w_up": grads["up_t"].T, "w_down": grads["down"],
          "w_ple_gate": grads["gate"], "w_ple_proj": grads["proj_t"].T}
    n_ssm, n_ffn = XBC_DIM // N_DEV, 2 * D_FF // N_DEV
    g_shard = {"ssm_conv_w": lax.dynamic_slice(g_small, (3, me * n_ssm), (4, n_ssm)),
               "ffn_conv_w": lax.dynamic_slice(g_small[13:25, :2 * D_FF // 4].reshape(3, 2 * D_FF), (0, me * n_ffn),
                                               (3, n_ffn))}

    delta, new_m, new_v = {}, {}, {}
    for k, tr in (("w_in", 256), ("w_out", None), ("w_up", 256), ("w_down", None), ("w_ple_gate", None),
                  ("w_ple_proj", None)):
        delta[k], new_m[k], new_v[k] = _adamw(w[k], gw[k], m[k], v[k], "adamw_" + k, tr)
    for k, (g_k, d_k, m_k, v_k) in _adamw_small(g_small, g_shard, w, m, v).items():
        gw[k], delta[k], new_m[k], new_v[k] = g_k, d_k, m_k, v_k

    outs = [loss, grad_x[None]]
    for d in (gw, delta, new_m, new_v):
        outs += [d[k][None] for k in names]
    return tuple(outs)
```

```python
import functools

import numpy as np
import jax
import jax.numpy as jnp
from jax import lax
from jax.experimental import pallas as pl
from jax.experimental.pallas import tpu as pltpu

F32 = jnp.float32
BF16 = jnp.bfloat16
SDS = jax.ShapeDtypeStruct
EPS = 1e-6
N_DEV = 8
D_MODEL = 1024
HEAD_DIM = 64
ATTN_DIM = 512
KV_DIM = 256
SSM_INNER = 1024
SSM_HEADS = 16
BC_DIM = 256
XBC_DIM = SSM_INNER + 2 * BC_DIM
MIX_DIM = ATTN_DIM + SSM_INNER
IN_PROJ = 3600
IN_PROJ_PAD = 3840
D_FF = 2816
PLE_DIM = 256
CHUNK = 128
SUPER = 2048
DILATIONS = (1, 4, 16)
TILE_UNROLL = 8
VMEM_LIMIT = 56 * 1024 * 1024
ADAM_LR, ADAM_B1, ADAM_B2, ADAM_EPS, ADAM_WD, ADAM_STEP = 0.001, 0.9, 0.999, 1e-08, 0.01, 10

NT = (((1,), (1,)), ((), ()))
TN = (((0,), (0,)), ((), ()))


def _params(*sem):
    return pltpu.CompilerParams(dimension_semantics=sem if sem else None, vmem_limit_bytes=VMEM_LIMIT)


def _dot(a, b, dims=None):
    if dims is None:
        return jnp.dot(a, b, preferred_element_type=F32)
    return lax.dot_general(a, b, dims, preferred_element_type=F32)


def _hdot(a, b, parts=2):
    a_exact = a.dtype == BF16
    x = b if a_exact else a
    acc = None
    for _ in range(parts):
        piece = x.astype(BF16)
        x = x - piece.astype(F32)
        d = _dot(a, piece) if a_exact else _dot(piece, b)
        acc = d if acc is None else acc + d
    return acc


def _sigmoid(x):
    return 0.5 * jnp.tanh(0.5 * x) + 0.5


def _shift_down(x, halo8, s):
    xr = pltpu.roll(x, s, 0)
    row = lax.broadcasted_iota(jnp.int32, halo8.shape, 0)
    first = jnp.where(row < s, pltpu.roll(halo8, s, 0), xr[0:8])
    return jnp.concatenate([first, xr[8:]], axis=0)


def _shift_up(x, halo8, s):
    n = x.shape[0]
    xr = pltpu.roll(x, n - s, 0)
    row = lax.broadcasted_iota(jnp.int32, halo8.shape, 0)
    last = jnp.where(row >= 8 - s, pltpu.roll(halo8, 8 - s, 0), xr[n - 8:])
    return jnp.concatenate([xr[:n - 8], last], axis=0)


def _norm_matmul(x, g, wt, tm, tn, name):
    t, k = x.shape
    n = wt.shape[0]

    def body(x_ref, g_ref, w_ref, o_ref, h_ref):
        @pl.when(pl.program_id(1) == 0)
        def _():
            xv = x_ref[...]
            r = lax.rsqrt(jnp.mean(xv * xv, axis=-1, keepdims=True) + EPS)
            h_ref[...] = (xv * r * g_ref[...]).astype(BF16)
        o_ref[...] = _dot(h_ref[...], w_ref[...], NT)

    return pl.pallas_call(
        body, name=name, grid=(t // tm, n // tn),
        in_specs=[pl.BlockSpec((tm, k), lambda i, j: (i, 0)), pl.BlockSpec((1, k), lambda i, j: (0, 0)),
                  pl.BlockSpec((tn, k), lambda i, j: (j, 0))],
        out_specs=[pl.BlockSpec((tm, tn), lambda i, j: (i, j)), pl.BlockSpec((tm, k), lambda i, j: (i, 0))],
        out_shape=[SDS((t, n), F32), SDS((t, k), BF16)],
        compiler_params=_params("parallel", "arbitrary"))(x, g, wt)


def _a_spec(a, lead, tm):
    if lead is None:
        return pl.BlockSpec((tm, a.shape[-1]), lambda i: (i, 0))
    return pl.BlockSpec((None, tm, a.shape[-1]), lambda i, _l=lead: (_l, i, 0))


def _mm_resid(pairs, res, dims, tm, out_dtype, name):
    t = pairs[0][0].shape[-2]
    n = pairs[0][2].shape[1] if dims is None else pairs[0][2].shape[0]
    np_ = len(pairs)

    def body(*refs):
        o_ref = refs[-1]
        acc = refs[2 * np_][...] if res is not None else None
        for q in range(np_):
            d = _dot(refs[q][...].astype(BF16), refs[np_ + q][...], dims)
            acc = d if acc is None else acc + d
        o_ref[...] = acc.astype(out_dtype)

    in_specs = [_a_spec(a, lead, tm) for a, lead, _ in pairs]
    in_specs += [pl.BlockSpec(b.shape, lambda i: (0, 0)) for _, _, b in pairs]
    args = [a for a, _, _ in pairs] + [b for _, _, b in pairs]
    if res is not None:
        in_specs.append(pl.BlockSpec((tm, n), lambda i: (i, 0)))
        args.append(res)
    return pl.pallas_call(
        body, name=name, grid=(t // tm,), in_specs=in_specs,
        out_specs=pl.BlockSpec((tm, n), lambda i: (i, 0)), out_shape=SDS((t, n), out_dtype),
        compiler_params=_params("parallel"))(*args)


def _mm_normbwd(pairs, x, g, dres, tm, name):
    t, k = x.shape
    np_ = len(pairs)
    b_specs = [pl.BlockSpec((rows, b.shape[1]), lambda i, _b=blk: (_b, 0)) for _, _, b, rows, blk in pairs]
    pairs = [(a, lead, b) for a, lead, b, _, _ in pairs]

    def body(*refs):
        x_ref, g_ref, dres_ref, dx_ref, dg_ref = refs[2 * np_:]
        dh = None
        for q in range(np_):
            d = _dot(refs[q][...], refs[np_ + q][...])
            dh = d if dh is None else dh + d
        xv = x_ref[...]
        r = lax.rsqrt(jnp.mean(xv * xv, axis=-1, keepdims=True) + EPS)
        xh = xv * r

        @pl.when(pl.program_id(0) == 0)
        def _():
            dg_ref[...] = jnp.zeros_like(dg_ref)
        dg_ref[...] += jnp.sum(dh * xh, axis=0, keepdims=True)
        gd = dh * g_ref[...]
        dx_ref[...] = dres_ref[...] + r * (gd - xh * jnp.mean(gd * xh, axis=-1, keepdims=True))

    in_specs = [_a_spec(a, lead, tm) for a, lead, _ in pairs] + b_specs
    in_specs += [pl.BlockSpec((tm, k), lambda i: (i, 0)), pl.BlockSpec((1, k), lambda i: (0, 0)),
                 pl.BlockSpec((tm, k), lambda i: (i, 0))]
    args = [a for a, _, _ in pairs] + [b for _, _, b in pairs] + [x, g, dres]
    return pl.pallas_call(
        body, name=name, grid=(t // tm,), in_specs=in_specs,
        out_specs=[pl.BlockSpec((tm, k), lambda i: (i, 0)), pl.BlockSpec((1, k), lambda i: (0, 0))],
        out_shape=[SDS((t, k), F32), SDS((1, k), F32)],
        compiler_params=_params("arbitrary"))(*args)


def _wgrad(a, a_lead, b, name, tk=2048):
    t, m = a.shape[-2:]
    n = b.shape[1]
    tm = m if m <= 1024 else 1408
    assert m % tm == 0

    def body(a_ref, b_ref, o_ref, acc):
        @pl.when(pl.program_id(1) == 0)
        def _():
            acc[...] = jnp.zeros_like(acc)
        acc[...] += _dot(a_ref[...].astype(BF16), b_ref[...].astype(BF16), TN)

        @pl.when(pl.program_id(1) == pl.num_programs(1) - 1)
        def _():
            o_ref[...] = acc[...].astype(BF16)

    per, lead = m // tm, 1
    if a_lead == "all":
        lead = a.shape[0]
        a_spec = pl.BlockSpec((None, tk, tm), lambda mi, ki: (mi // per, ki, mi % per))
    elif a_lead is None:
        a_spec = pl.BlockSpec((tk, tm), lambda mi, ki: (ki, mi))
    else:
        a_spec = pl.BlockSpec((None, tk, tm), lambda mi, ki, _l=a_lead: (_l, ki, mi))
    return pl.pallas_call(
        body, name=name, grid=(lead * per, t // tk),
        in_specs=[a_spec, pl.BlockSpec((tk, n), lambda mi, ki: (ki, 0))],
        out_specs=pl.BlockSpec((tm, n), lambda mi, ki: (mi, 0)), out_shape=SDS((lead * m, n), BF16),
        scratch_shapes=[pltpu.VMEM((tm, n), F32)],
        compiler_params=_params("parallel", "arbitrary"))(a, b)


def _head_consts():
    iq = np.arange(ATTN_DIM)
    ik = np.arange(KV_DIM)
    ones_q = (iq[:, None] // HEAD_DIM == iq[None, :] // HEAD_DIM).astype(np.float32)
    ones_k = (ik[:, None] // HEAD_DIM == ik[None, :] // HEAD_DIM).astype(np.float32)
    dup = (ik[:, None] == (HEAD_DIM * (iq[None, :] // 128) + iq[None, :] % HEAD_DIM)).astype(np.float32)
    return jnp.asarray(ones_q, BF16), jnp.asarray(ones_k, BF16), jnp.asarray(dup, BF16), jnp.asarray(dup.T, BF16)


def _attn_prep(proj, gq, gk, ones_q, ones_k, dup, tm=512):
    t = proj.shape[0]

    def body(p_ref, gq_ref, gk_ref, oq_ref, ok_ref, dup_ref, qn_ref, kd_ref, vd_ref):
        q = p_ref[:, 0:ATTN_DIM]
        k = p_ref[:, ATTN_DIM:ATTN_DIM + KV_DIM]
        v = p_ref[:, ATTN_DIM + KV_DIM:]
        rq = lax.rsqrt(_hdot(q * q, oq_ref[...]) * (1.0 / HEAD_DIM) + EPS)
        qn_ref[...] = (q * rq * gq_ref[...]) * (HEAD_DIM ** -0.5)
        rk = lax.rsqrt(_hdot(k * k, ok_ref[...]) * (1.0 / HEAD_DIM) + EPS)
        kn = k * rk * gk_ref[...]
        kd_ref[...] = _dot(kn.astype(BF16), dup_ref[...])
        vd_ref[...] = _dot(v.astype(BF16), dup_ref[...])

    full = lambda a: pl.BlockSpec(a.shape, lambda i: (0, 0))
    o_spec = pl.BlockSpec((tm, ATTN_DIM), lambda i: (i, 0))
    return pl.pallas_call(
        body, name="attn_prep", grid=(t // tm,),
        in_specs=[pl.BlockSpec((tm, 1024), lambda i: (i, 0)), full(gq), full(gk), full(ones_q), full(ones_k), full(dup)],
        out_specs=[o_spec, o_spec, o_spec], out_shape=[SDS((t, ATTN_DIM), F32)] * 3,
        compiler_params=_params("parallel"))(proj, gq, gk, ones_q, ones_k, dup)


def _attn_prep_bwd(proj, dqn, dkc, dkp, dvc, dvp, gq, gk, ones_q, ones_k, dup_t, tm=512):
    t = proj.shape[0]
    nblk = t // tm
    off = SUPER // tm

    def body(p_ref, dqn_ref, dkc_ref, dkp_ref, dvc_ref, dvp_ref, gq_ref, gk_ref, oq_ref, ok_ref, dt_ref,
             o_ref, dgq_ref, dgk_ref):
        i = pl.program_id(0)
        has_next = (i + off < nblk).astype(F32)
        q = p_ref[:, 0:ATTN_DIM]
        k = p_ref[:, ATTN_DIM:ATTN_DIM + KV_DIM]
        dkn = _hdot(dkc_ref[...] + has_next * dkp_ref[...], dt_ref[...])
        dv = _hdot(dvc_ref[...] + has_next * dvp_ref[...], dt_ref[...])

        @pl.when(i == 0)
        def _():
            dgq_ref[...] = jnp.zeros_like(dgq_ref)
            dgk_ref[...] = jnp.zeros_like(dgk_ref)

        rq = lax.rsqrt(_hdot(q * q, oq_ref[...]) * (1.0 / HEAD_DIM) + EPS)
        xh = q * rq
        dy = dqn_ref[...] * (HEAD_DIM ** -0.5)
        dgq_ref[...] += jnp.sum(dy * xh, axis=0, keepdims=True)
        gd = dy * gq_ref[...]
        dq = rq * (gd - xh * (_hdot(gd * xh, oq_ref[...]) * (1.0 / HEAD_DIM)))
        rk = lax.rsqrt(_hdot(k * k, ok_ref[...]) * (1.0 / HEAD_DIM) + EPS)
        kh = k * rk
        dgk_ref[...] += jnp.sum(dkn * kh, axis=0, keepdims=True)
        gdk = dkn * gk_ref[...]
        dk = rk * (gdk - kh * (_hdot(gdk * kh, ok_ref[...]) * (1.0 / HEAD_DIM)))
        o_ref[:, 0:ATTN_DIM] = dq.astype(BF16)
        o_ref[:, ATTN_DIM:ATTN_DIM + KV_DIM] = dk.astype(BF16)
        o_ref[:, ATTN_DIM + KV_DIM:] = dv.astype(BF16)

    full = lambda a: pl.BlockSpec(a.shape, lambda i: (0, 0))
    cur = pl.BlockSpec((tm, ATTN_DIM), lambda i: (i, 0))
    nxt = pl.BlockSpec((tm, ATTN_DIM), lambda i: (jnp.minimum(i + off, nblk - 1), 0))
    return pl.pallas_call(
        body, name="attn_prep_bwd", grid=(nblk,),
        in_specs=[pl.BlockSpec((tm, 1024), lambda i: (i, 0)), cur, cur, nxt, cur, nxt,
                  full(gq), full(gk), full(ones_q), full(ones_k), full(dup_t)],
        out_specs=[pl.BlockSpec((tm, 1024), lambda i: (i, 0)), pl.BlockSpec((1, ATTN_DIM), lambda i: (0, 0)),
                   pl.BlockSpec((1, KV_DIM), lambda i: (0, 0))],
        out_shape=[SDS((t, 1024), BF16), SDS((1, ATTN_DIM), F32), SDS((1, KV_DIM), F32)],
        compiler_params=_params("arbitrary"))(proj, dqn, dkc, dkp, dvc, dvp, gq, gk, ones_q, ones_k, dup_t)


def _tile_masks():
    qi = lax.broadcasted_iota(jnp.int32, (2 * CHUNK, 2 * CHUNK), 0) & (CHUNK - 1)
    kj = lax.broadcasted_iota(jnp.int32, (2 * CHUNK, 2 * CHUNK), 1)
    delta = CHUNK + qi - kj
    band = (delta >= 0) & (delta <= CHUNK)
    return band, kj


def _deinterleave(dst, src, n_rows, d):
    per = n_rows // d
    for r in range(d):
        dst[r * per:(r + 1) * per, :] = src[pl.ds(r, per, stride=d), :]


def _attn_specs(t):
    blk = lambda f: pl.BlockSpec((SUPER, 128), f)
    cur = blk(lambda h, s: (s, h))
    prev = blk(lambda h, s: (jnp.maximum(s - 1, 0), h))
    return cur, prev


def _attn_fwd(qn, kd, vd):
    t = qn.shape[0]
    cur, prev = _attn_specs(t)

    def body(q_ref, kp_ref, kc_ref, vp_ref, vc_ref, o_ref, lse_ref, kk, vv, qd, kdd, vdd, po, pm, pll, acc, mm, ll):
        s = pl.program_id(1)
        kk[0:SUPER, :] = kp_ref[...]
        kk[SUPER:, :] = kc_ref[...]
        vv[0:SUPER, :] = vp_ref[...]
        vv[SUPER:, :] = vc_ref[...]
        m0 = lax.broadcasted_iota(jnp.int32, (CHUNK, 128), 1) < HEAD_DIM
        band, kj = _tile_masks()
        for d in DILATIONS:
            lq = SUPER // d
            if d == 1:
                qs_ref, ks_ref, vs_ref = q_ref, kk, vv
            else:
                _deinterleave(qd, q_ref, SUPER, d)
                _deinterleave(kdd, kk, 2 * SUPER, d)
                _deinterleave(vdd, vv, 2 * SUPER, d)
                qs_ref, ks_ref, vs_ref = qd, kdd, vdd

            nblk = lq // CHUNK

            def key_rows(ti):
                return pl.ds((ti // nblk) * 2 * lq + lq + (ti % nblk - 1) * CHUNK, 2 * CHUNK)

            def scores(ti):
                qt = qs_ref[pl.ds(ti * CHUNK, CHUNK), :]
                qs = jnp.concatenate([jnp.where(m0, qt, 0.0), jnp.where(m0, 0.0, qt)], axis=0).astype(BF16)
                return _dot(qs, ks_ref[key_rows(ti), :].astype(BF16), NT)

            def softmax_pv(ti, sc):
                ok = band if ti % nblk > 0 else band & (kj >= jnp.where(s > 0, 0, CHUNK))
                sc = jnp.where(ok, sc, -jnp.inf)
                mt = jnp.max(sc, axis=-1, keepdims=True)
                p = jnp.exp(sc - mt)
                lt = jnp.sum(p, axis=-1, keepdims=True)
                ot = _dot(p.astype(BF16), vs_ref[key_rows(ti), :].astype(BF16))
                qrows = pl.ds(ti * CHUNK, CHUNK)
                po[qrows, :] = jnp.where(m0, ot[:CHUNK], ot[CHUNK:])
                pm[qrows, :] = jnp.where(m0, mt[:CHUNK], mt[CHUNK:])
                pll[qrows, :] = jnp.where(m0, lt[:CHUNK], lt[CHUNK:])

            for ti in range(SUPER // CHUNK):
                softmax_pv(ti, scores(ti))
            if d == 1:
                acc[...] = po[...]
                mm[...] = pm[...]
                ll[...] = pll[...]
            else:
                for r in range(d):
                    rows = pl.ds(r, lq, stride=d)
                    seg = slice(r * lq, (r + 1) * lq)
                    m_old, m_new = mm[rows, :], pm[seg, :]
                    m_all = jnp.maximum(m_old, m_new)
                    a, b = jnp.exp(m_old - m_all), jnp.exp(m_new - m_all)
                    acc[rows, :] = acc[rows, :] * a + po[seg, :] * b
                    ll[rows, :] = ll[rows, :] * a + pll[seg, :] * b
                    mm[rows, :] = m_all
        o_ref[...] = acc[...] / ll[...]
        lse_ref[...] = mm[...] + jnp.log(ll[...])

    big = pltpu.VMEM((2 * SUPER, 128), F32)
    one = pltpu.VMEM((SUPER, 128), F32)
    return pl.pallas_call(
        body, name="attn_fwd", grid=(4, t // SUPER),
        in_specs=[cur, prev, cur, prev, cur], out_specs=[cur, cur],
        out_shape=[SDS((t, ATTN_DIM), F32)] * 2,
        scratch_shapes=[big, big, one, big, big, one, one, one, one, one, one],
        compiler_params=_params("parallel", "arbitrary"))(qn, kd, kd, vd, vd)


def _attn_bwd(qn, kd, vd, out, lse, dout, ones_pair):
    t = qn.shape[0]
    cur, prev = _attn_specs(t)

    def body(q_ref, kp_ref, kc_ref, vp_ref, vc_ref, o_ref, lse_ref, do_ref, ones_ref,
             dq_ref, dkc_ref, dkp_ref, dvc_ref, dvp_ref,
             kk, vv, od, ld, kb, vb, qsb, dosb, tk, tv, pdq, delta):
        s = pl.program_id(1)
        delta[...] = _hdot(do_ref[...] * o_ref[...], ones_ref[...])

        def per_row(a):
            ar = pltpu.roll(a, HEAD_DIM, 1)
            rows = jnp.concatenate([jnp.where(m0, a, ar), jnp.where(m0, ar, a)], axis=0)
            return jnp.concatenate([rows, rows], axis=1)

        kk[0:SUPER, :] = kp_ref[...]
        kk[SUPER:, :] = kc_ref[...]
        vv[0:SUPER, :] = vp_ref[...]
        vv[SUPER:, :] = vc_ref[...]
        for ref in (dq_ref, dkc_ref, dkp_ref, dvc_ref, dvp_ref):
            ref[...] = jnp.zeros_like(ref)
        m0 = lax.broadcasted_iota(jnp.int32, (CHUNK, 128), 1) < HEAD_DIM
        band, kj = _tile_masks()
        ninf = -jnp.inf
        for d in DILATIONS:
            lq = SUPER // d
            nblk = lq // CHUNK
            for r in range(d):
                seg = slice(r * 2 * lq, (r + 1) * 2 * lq)
                kb[seg, :] = kk[pl.ds(r, 2 * lq, stride=d), :].astype(BF16)
                vb[seg, :] = vv[pl.ds(r, 2 * lq, stride=d), :].astype(BF16)
            for ti in range(SUPER // CHUNK):
                rows = pl.ds(ti // nblk + d * CHUNK * (ti % nblk), CHUNK, stride=d)
                for src, dst in ((q_ref, qsb), (do_ref, dosb)):
                    a = src[rows, :]
                    dst[ti * 2 * CHUNK:(ti + 1) * 2 * CHUNK, :] = jnp.concatenate(
                        [jnp.where(m0, a, 0.0), jnp.where(m0, 0.0, a)], axis=0).astype(BF16)
                ld[ti * CHUNK:(ti + 1) * CHUNK, :] = lse_ref[rows, :]
                od[ti * CHUNK:(ti + 1) * CHUNK, :] = delta[rows, :]

            def operands(ti):
                r, nb = ti // nblk, ti % nblk
                stacked = slice(ti * 2 * CHUNK, (ti + 1) * 2 * CHUNK)
                krows = pl.ds(r * 2 * lq + lq + (nb - 1) * CHUNK, 2 * CHUNK)
                return stacked, krows

            def scores(ti):
                stacked, krows = operands(ti)
                kt = kb[krows, :]
                return dict(ti=ti, sc=_dot(qsb[stacked, :], kt, NT), dp=_dot(dosb[stacked, :], vb[krows, :], NT))

            def softmax_grad(c):
                qrows = slice(c["ti"] * CHUNK, (c["ti"] + 1) * CHUNK)
                ok = band if c["ti"] % nblk > 0 else band & (kj >= jnp.where(s > 0, 0, CHUNK))
                p = jnp.exp(jnp.where(ok, c.pop("sc"), ninf) - per_row(ld[qrows, :]))
                ds = p * (c.pop("dp") - per_row(od[qrows, :]))
                c.update(p=p.astype(BF16), ds=ds.astype(BF16))
                return c

            def grads(c):
                ti = c["ti"]
                stacked, krows = operands(ti)
                dqs = _dot(c["ds"], kb[krows, :])
                pdq[ti * CHUNK:(ti + 1) * CHUNK, :] = jnp.where(m0, dqs[:CHUNK], dqs[CHUNK:])
                tk[stacked, :] = _dot(c["ds"], qsb[stacked, :], TN)
                tv[stacked, :] = _dot(c["p"], dosb[stacked, :], TN)

            n_tiles = SUPER // CHUNK
            stage_a = scores(0)
            for ti in range(n_tiles):
                ahead = scores(ti + 1) if ti + 1 < n_tiles else None
                grads(softmax_grad(stage_a))
                stage_a = ahead

            for r in range(d):
                dq_ref[pl.ds(r, lq, stride=d), :] += pdq[r * lq:(r + 1) * lq, :]
                for tile_out, cur_ref, prev_ref in ((tk, dkc_ref, dkp_ref), (tv, dvc_ref, dvp_ref)):
                    first = r * nblk * 2 * CHUNK
                    prev_ref[pl.ds(SUPER - CHUNK * d + r, CHUNK, stride=d), :] += tile_out[first:first + CHUNK, :]
                    for nb in range(nblk):
                        at = (r * nblk + nb) * 2 * CHUNK
                        part = tile_out[at + CHUNK:at + 2 * CHUNK, :]
                        if nb + 1 < nblk:
                            part = part + tile_out[at + 2 * CHUNK:at + 3 * CHUNK, :]
                        cur_ref[pl.ds(r + d * nb * CHUNK, CHUNK, stride=d), :] += part

    big = pltpu.VMEM((2 * SUPER, 128), F32)
    one = pltpu.VMEM((SUPER, 128), F32)
    half = pltpu.VMEM((2 * SUPER, 128), BF16)
    return pl.pallas_call(
        body, name="attn_bwd", grid=(4, t // SUPER),
        in_specs=[cur, prev, cur, prev, cur, cur, cur, cur, pl.BlockSpec((128, 128), lambda h, s: (0, 0))],
        out_specs=[cur] * 5, out_shape=[SDS((t, ATTN_DIM), F32)] * 5,
        scratch_shapes=[big, big, one, one, half, half, half, half, big, big, one, one],
        compiler_params=_params("parallel", "arbitrary"))(qn, kd, kd, vd, vd, out, lse, dout, ones_pair)


def _ssd_consts():
    tri = np.tril(np.ones((CHUNK, CHUNK), np.float32))
    expand = np.zeros((128, SSM_INNER), np.float32)
    for h in range(SSM_HEADS):
        expand[h, h * HEAD_DIM:(h + 1) * HEAD_DIM] = 1.0
    return jnp.asarray(tri, BF16), jnp.asarray(tri.T, BF16), jnp.asarray(expand, BF16), jnp.asarray(expand.T, BF16)


def _conv4(x, halo, w_ref, b_ref):
    acc = b_ref[...] + w_ref[3:4, :] * x
    for k in range(3):
        acc = acc + w_ref[k:k + 1, :] * _shift_down(x, halo, 3 - k)
    return acc


def _softplus(x):
    return jnp.maximum(x, 0.0) + jnp.log(1.0 + jnp.exp(-jnp.abs(x)))


def _ssd_common(pre_x, pre_b, dt_ref, dtb_ref, alog_ref, tri_ref, exp_ref):
    xa = pre_x * _sigmoid(pre_x)
    ba = pre_b * _sigmoid(pre_b)
    dtv = _softplus(dt_ref[...] + dtb_ref[...])
    a_neg = -jnp.exp(alog_ref[...])
    acum = _hdot(tri_ref[...], dtv * a_neg, parts=3)
    lam = jnp.exp(acum)
    gam = jnp.exp(acum[CHUNK - 1:CHUNK, :] - acum)
    dt_e = _hdot(dtv, exp_ref[...])
    lam_e = _hdot(lam, exp_ref[...])
    gam_e = _hdot(gam, exp_ref[...])
    return dict(pre_x=pre_x, pre_b=pre_b, xa=xa, ba=ba, dtv=dtv, a_neg=a_neg, acum=acum,
                dt_e=dt_e, lam_e=lam_e, gam_e=gam_e, xdt=xa * dt_e)


def _decay(acum_t, h, transposed):
    rb = jnp.broadcast_to(acum_t[h:h + 1, :], (CHUNK, CHUNK))
    ri = lax.broadcasted_iota(jnp.int32, (CHUNK, CHUNK), 0)
    ci = lax.broadcasted_iota(jnp.int32, (CHUNK, CHUNK), 1)
    if transposed:
        return jnp.exp(jnp.where(ci >= ri, rb - rb.T, -jnp.inf))
    return jnp.exp(jnp.where(ri >= ci, rb.T - rb, -jnp.inf))


SSD_STEP = 4 * CHUNK


def _ssd_specs(t, rev):
    nc = t // SSD_STEP
    ch = (lambda c: nc - 1 - c) if rev else (lambda c: c)
    col = lambda w, j: pl.BlockSpec((SSD_STEP, w), lambda c: (ch(c), j))
    halo = lambda w, j: pl.BlockSpec((8, w), lambda c: (jnp.maximum(ch(c) * (SSD_STEP // 8) - 1, 0), j))
    return nc, ch, col, halo


def _ssd_fwd(proj, cwx, cbx, cwb, cbb, dtb, alog, dsk_e, norm_g, tri, expand):
    t = proj.shape[0]
    nc, _, col, halo = _ssd_specs(t, False)

    def body(z_all, xs_all, bc_all, dt_all, hx_ref, hb_ref, cwx_ref, cbx_ref, cwb_ref, cbb_ref, dtb_ref, alog_ref,
             dsk_ref, g_ref, tri_ref, exp_ref, y_all, hs_all, o_all, px_all, pb_all, state):
        @pl.when(pl.program_id(0) == 0)
        def _():
            state[...] = jnp.zeros_like(state)

        keep = (pl.program_id(0) > 0).astype(F32)
        for sc in range(SSD_STEP // CHUNK):
            rows = pl.ds(sc * CHUNK, CHUNK)
            before = pl.ds(sc * CHUNK - 8, 8)
            hx = hx_ref[...] * keep if sc == 0 else xs_all[before, :]
            hb = hb_ref[...] * keep if sc == 0 else bc_all[before, :]
            chunk(z_all.at[rows], xs_all.at[rows], bc_all.at[rows], dt_all.at[rows], hx, hb, cwx_ref, cbx_ref, cwb_ref,
                  cbb_ref, dtb_ref, alog_ref, dsk_ref, g_ref, tri_ref, exp_ref, y_all.at[rows],
                  hs_all.at[pl.ds(sc, 1)], o_all.at[rows], px_all.at[rows], pb_all.at[rows], state)

    def chunk(z_ref, xs_ref, bc_ref, dt_ref, hx, hb, cwx_ref, cbx_ref, cwb_ref, cbb_ref, dtb_ref, alog_ref,
              dsk_ref, g_ref, tri_ref, exp_ref, y_ref, hs_ref, o_ref, px_ref, pb_ref, state):
        pre_x = _conv4(xs_ref[...], hx, cwx_ref, cbx_ref)
        pre_b = _conv4(bc_ref[...], hb, cwb_ref, cbb_ref)
        px_ref[...] = pre_x.astype(BF16)
        pb_ref[...] = pre_b.astype(BF16)
        v = _ssd_common(pre_x, pre_b, dt_ref, dtb_ref, alog_ref, tri_ref, exp_ref)
        acum_t = v["acum"].T
        xdt, ba = v["xdt"], v["ba"]
        h_in = state[...]
        hs_ref[0] = h_in
        xg = xdt * v["gam_e"]
        m0 = lax.broadcasted_iota(jnp.int32, (CHUNK, 128), 1) < HEAD_DIM
        for g in range(2):
            bg = ba[:, g * 128:(g + 1) * 128].astype(BF16)
            cg = ba[:, 256 + g * 128:256 + (g + 1) * 128].astype(BF16)
            gl = slice(g * 512, (g + 1) * 512)
            cb = _dot(cg, bg, NT)
            y_off = _dot(cg, h_in[:, gl].astype(BF16)) * v["lam_e"][:, gl]
            s_new = _dot(bg.T, xg[:, gl].astype(BF16))
            state[:, gl] = h_in[:, gl] * v["lam_e"][CHUNK - 1:CHUNK, gl] + s_new
            for j in range(4):
                h0 = 8 * g + 2 * j
                ln = slice(g * 512 + j * 128, g * 512 + (j + 1) * 128)
                xp = xdt[:, ln].astype(BF16)
                y0 = _dot((cb * _decay(acum_t, h0, False)).astype(BF16), xp)
                y1 = _dot((cb * _decay(acum_t, h0 + 1, False)).astype(BF16), xp)
                y_ref[:, ln] = jnp.where(m0, y0, y1) + y_off[:, j * 128:(j + 1) * 128]
        z = z_ref[...]
        yg = (y_ref[...] + dsk_ref[...] * v["xa"]) * (z * _sigmoid(z))
        r = lax.rsqrt(jnp.mean(yg * yg, axis=-1, keepdims=True) + EPS)
        o_ref[...] = (yg * r * g_ref[...]).astype(BF16)

    full = lambda a: pl.BlockSpec(a.shape, lambda c: (0,) * a.ndim)
    return pl.pallas_call(
        body, name="ssd_fwd", grid=(nc,),
        in_specs=[col(1024, 1), col(1024, 2), col(512, 6), col(128, 28), halo(1024, 2), halo(512, 6),
                  full(cwx), full(cbx), full(cwb), full(cbb), full(dtb), full(alog), full(dsk_e), full(norm_g),
                  full(tri), full(expand)],
        out_specs=[pl.BlockSpec((SSD_STEP, SSM_INNER), lambda c: (c, 0)),
                   pl.BlockSpec((SSD_STEP // CHUNK, 128, SSM_INNER), lambda c: (c, 0, 0)),
                   pl.BlockSpec((SSD_STEP, SSM_INNER), lambda c: (c, 0)),
                   pl.BlockSpec((SSD_STEP, SSM_INNER), lambda c: (c, 0)), pl.BlockSpec((SSD_STEP, 512), lambda c: (c, 0))],
        out_shape=[SDS((t, SSM_INNER), F32), SDS((t // CHUNK, 128, SSM_INNER), F32), SDS((t, SSM_INNER), BF16),
                   SDS((t, SSM_INNER), BF16), SDS((t, 512), BF16)],
        scratch_shapes=[pltpu.VMEM((128, SSM_INNER), F32)],
        compiler_params=_params("arbitrary"))(proj, proj, proj, proj, proj, proj, cwx, cbx, cwb, cbb, dtb, alog,
                                              dsk_e, norm_g, tri, expand)


def _ssd_bwd(proj, pre_x, pre_b, y_ssd, hs, dout, cwx, cwb, dtb, alog, dsk_e, norm_g, tri, triu, expand, expand_t):
    t = proj.shape[0]
    nc, ch, col, halo = _ssd_specs(t, True)

    def body(z_all, xs_all, bc_all, dt_all, px_all, pb_all, y_all, hin_all, do_all,
             cwx_ref, cwb_ref, dtb_ref, alog_ref, dsk_ref, g_ref, tri_ref, triu_ref, exp_ref, expt_ref,
             dz_all, dxs_all, dbc_all, ddt_all, dg_ref, ddsk_ref, dalog_ref, ddtb_ref, dcwx_ref, dcbx_ref, dcwb_ref,
             dcbb_ref, gstate, nx_x, nx_b, dact_b, dxdt_s):
        @pl.when(pl.program_id(0) == 0)
        def _():
            gstate[...] = jnp.zeros_like(gstate)
            nx_x[...] = jnp.zeros_like(nx_x)
            nx_b[...] = jnp.zeros_like(nx_b)
            for ref in (dg_ref, ddsk_ref, dalog_ref, ddtb_ref, dcwx_ref, dcbx_ref, dcwb_ref, dcbb_ref):
                ref[...] = jnp.zeros_like(ref)

        for sc in reversed(range(SSD_STEP // CHUNK)):
            rows = pl.ds(sc * CHUNK, CHUNK)
            by_rows = [r.at[rows] for r in (z_all, xs_all, bc_all, dt_all, px_all, pb_all, y_all)]
            outs = [r.at[rows] for r in (dz_all, dxs_all, dbc_all, ddt_all)]
            chunk(*by_rows, hin_all.at[pl.ds(sc, 1)], do_all.at[rows],
                  cwx_ref, cwb_ref, dtb_ref, alog_ref, dsk_ref, g_ref, tri_ref, triu_ref, exp_ref, expt_ref,
                  *outs, dg_ref, ddsk_ref, dalog_ref, ddtb_ref, dcwx_ref, dcbx_ref, dcwb_ref, dcbb_ref,
                  gstate, nx_x, nx_b, dact_b, dxdt_s)

    def chunk(z_ref, xs_ref, bc_ref, dt_ref, px_ref, pb_ref, y_ref, hin_ref, do_ref,
              cwx_ref, cwb_ref, dtb_ref, alog_ref, dsk_ref, g_ref, tri_ref, triu_ref, exp_ref, expt_ref,
              dz_ref, dxs_ref, dbc_ref, ddt_ref, dg_ref, ddsk_ref, dalog_ref, ddtb_ref, dcwx_ref, dcbx_ref, dcwb_ref,
              dcbb_ref, gstate, nx_x, nx_b, dact_b, dxdt_s):
        v = _ssd_common(px_ref[...].astype(F32), pb_ref[...].astype(F32), dt_ref, dtb_ref, alog_ref, tri_ref, exp_ref)
        acum_t = v["acum"].T
        xa, ba, xdt, dtv = v["xa"], v["ba"], v["xdt"], v["dtv"]
        lam_e, gam_e, dt_e = v["lam_e"], v["gam_e"], v["dt_e"]
        z = z_ref[...]
        y = y_ref[...]
        sz = _sigmoid(z)
        zs = z * sz
        y_tot = y + dsk_ref[...] * xa
        yg = y_tot * zs
        r = lax.rsqrt(jnp.mean(yg * yg, axis=-1, keepdims=True) + EPS)
        yh = yg * r
        do = do_ref[...]
        dg_ref[...] += jnp.sum(do * yh, axis=0, keepdims=True)
        gd = do * g_ref[...]
        dyg = r * (gd - yh * jnp.mean(gd * yh, axis=-1, keepdims=True))
        dz_ref[...] = (dyg * y_tot * (sz * (1.0 + z * (1.0 - sz)))).astype(BF16)
        dy = dyg * zs
        ddsk_ref[...] += jnp.sum(dy * xa, axis=0, keepdims=True)
        g_out = gstate[...]
        h_in = hin_ref[0]
        lam_dy = lam_e * dy
        gam_x = gam_e * xdt
        m0 = lax.broadcasted_iota(jnp.int32, (CHUNK, 128), 1) < HEAD_DIM
        lane = lax.broadcasted_iota(jnp.int32, (CHUNK, 128), 1)
        below = (lax.broadcasted_iota(jnp.int32, (CHUNK, CHUNK), 0) >
                 lax.broadcasted_iota(jnp.int32, (CHUNK, CHUNK), 1))
        da_in = jnp.zeros((CHUNK, 128), F32)
        off_y, off_x = [], []
        for g in range(2):
            bg = ba[:, g * 128:(g + 1) * 128].astype(BF16)
            cg = ba[:, 256 + g * 128:256 + (g + 1) * 128].astype(BF16)
            gl = slice(g * 512, (g + 1) * 512)
            gg = g_out[:, gl].astype(BF16)
            bc_t = _dot(bg, cg, NT)
            cb = _dot(cg, bg, NT)
            dxdt_off = _dot(bg, gg) * gam_e[:, gl]
            off_x.append(xdt[:, gl] * dxdt_off)
            off_y.append(dy[:, gl] * (_dot(cg, h_in[:, gl].astype(BF16)) * lam_e[:, gl]))
            q_sum = jnp.zeros((CHUNK, CHUNK), F32)
            for j in range(4):
                h0 = 8 * g + 2 * j
                ln = slice(g * 512 + j * 128, g * 512 + (j + 1) * 128)
                dyp = dy[:, ln]
                dyb = dyp.astype(BF16)
                xpb = xdt[:, ln].astype(BF16)
                d0 = _dot((bc_t * _decay(acum_t, h0, True)).astype(BF16), dyb)
                d1 = _dot((bc_t * _decay(acum_t, h0 + 1, True)).astype(BF16), dyb)
                dxdt_s[:, ln] = jnp.where(m0, d0, d1) + dxdt_off[:, j * 128:(j + 1) * 128]
                for hh, dym in ((h0, jnp.where(m0, dyp, 0.0)), (h0 + 1, jnp.where(m0, 0.0, dyp))):
                    qd = _dot(dym.astype(BF16), xpb, NT) * _decay(acum_t, hh, False)
                    q_sum = q_sum + qd
                    reach = jnp.where(below, _hdot(triu_ref[...], qd * cb), 0.0)
                    da_in = jnp.where(lane == hh, jnp.sum(reach, axis=-1, keepdims=True), da_in)
            gstate[:, gl] = g_out[:, gl] * lam_e[CHUNK - 1:CHUNK, gl] + _dot(cg.T, lam_dy[:, gl].astype(BF16))
            qb = q_sum.astype(BF16)
            dact_b[:, 256 + g * 128:256 + (g + 1) * 128] = (
                _dot(qb, bg) + _dot(lam_dy[:, gl].astype(BF16), h_in[:, gl].astype(BF16), NT))
            dact_b[:, g * 128:(g + 1) * 128] = _dot(qb.T, cg) + _dot(gam_x[:, gl].astype(BF16), gg, NT)
        dxdt = dxdt_s[...]
        seg_y = _hdot(jnp.concatenate(off_y, axis=1), expt_ref[...])
        seg_x = _hdot(jnp.concatenate(off_x, axis=1), expt_ref[...])
        e_col = jnp.sum(g_out * h_in * lam_e[CHUNK - 1:CHUNK, :], axis=0, keepdims=True)
        e_seg = _hdot(jnp.broadcast_to(e_col, (8, SSM_INNER)), expt_ref[...])[0:1, :]
        da = da_in + _hdot(triu_ref[...], seg_y) + (_hdot(tri_ref[...], seg_x) - seg_x) + e_seg
        a_neg = v["a_neg"]
        ddtv = da * a_neg + _hdot(dxdt * xa, expt_ref[...])
        dalog_ref[...] += jnp.sum(da * dtv, axis=0, keepdims=True) * a_neg
        lane16 = lax.broadcasted_iota(jnp.int32, (CHUNK, 128), 1) < SSM_HEADS
        draw = jnp.where(lane16, ddtv * _sigmoid(dt_ref[...] + dtb_ref[...]), 0.0)
        ddtb_ref[...] += jnp.sum(draw, axis=0, keepdims=True)
        ddt_ref[...] = draw.astype(BF16)
        dxa = dxdt * dt_e + dy * dsk_ref[...]
        for (dact, pre, x_ref, nx, cw_ref, dcw_ref, dcb_ref, dx_ref) in (
                (dxa, v["pre_x"], xs_ref, nx_x, cwx_ref, dcwx_ref, dcbx_ref, dxs_ref),
                (dact_b[...], v["pre_b"], bc_ref, nx_b, cwb_ref, dcwb_ref, dcbb_ref, dbc_ref)):
            sp = _sigmoid(pre)
            dpre = dact * (sp * (1.0 + pre * (1.0 - sp)))
            dcb_ref[...] += jnp.sum(dpre, axis=0, keepdims=True)
            xv = x_ref[...]
            nxt = nx[...]
            dx = cw_ref[3:4, :] * dpre
            dcw_ref[3:4, :] += jnp.sum(dpre * xv, axis=0, keepdims=True)
            for k in range(3):
                d_up = _shift_up(dpre, nxt, 3 - k)
                dcw_ref[k:k + 1, :] += jnp.sum(xv * d_up, axis=0, keepdims=True)
                dx = dx + cw_ref[k:k + 1, :] * d_up
            nx[...] = dpre[0:8, :]
            dx_ref[...] = dx.astype(dx_ref.dtype)

    full = lambda a: pl.BlockSpec(a.shape, lambda c: (0,) * a.ndim)
    rowblk = lambda w: pl.BlockSpec((SSD_STEP, w), lambda c: (ch(c), 0))
    acc = lambda a, b: pl.BlockSpec((a, b), lambda c: (0, 0))
    return pl.pallas_call(
        body, name="ssd_bwd", grid=(nc,),
        in_specs=[col(1024, 1), col(1024, 2), col(512, 6), col(128, 28), rowblk(SSM_INNER), rowblk(512),
                  rowblk(SSM_INNER),
                  pl.BlockSpec((SSD_STEP // CHUNK, 128, SSM_INNER), lambda c: (ch(c), 0, 0)),
                  rowblk(SSM_INNER),
                  full(cwx), full(cwb), full(dtb), full(alog), full(dsk_e), full(norm_g),
                  full(tri), full(triu), full(expand), full(expand_t)],
        out_specs=[rowblk(SSM_INNER), rowblk(SSM_INNER), rowblk(512), rowblk(128),
                   acc(1, 1024), acc(1, 1024), acc(1, 128), acc(1, 128), acc(4, 1024), acc(1, 1024), acc(4, 512),
                   acc(1, 512)],
        out_shape=[SDS((t, SSM_INNER), BF16), SDS((t, SSM_INNER), BF16), SDS((t, 512), BF16), SDS((t, 128), BF16),
                   SDS((1, 1024), F32), SDS((1, 1024), F32), SDS((1, 128), F32), SDS((1, 128), F32),
                   SDS((4, 1024), F32), SDS((1, 1024), F32), SDS((4, 512), F32), SDS((1, 512), F32)],
        scratch_shapes=[pltpu.VMEM((128, SSM_INNER), F32), pltpu.VMEM((8, 1024), F32), pltpu.VMEM((8, 512), F32),
                        pltpu.VMEM((CHUNK, 512), F32), pltpu.VMEM((CHUNK, SSM_INNER), F32)],
        compiler_params=_params("arbitrary"))(proj, proj, proj, proj, pre_x, pre_b, y_ssd, hs, dout,
                                              cwx, cwb, dtb, alog, dsk_e, norm_g, tri, triu, expand, expand_t)


def _conv3(x, halo, w_ref, b_ref, part):
    acc = b_ref[part] + w_ref[2, part] * x
    for k in range(2):
        acc = acc + w_ref[k, part] * _shift_down(x, halo, 2 - k)
    return acc


def _up_act(x, g, w_up_t, cw, cb, tm=2048, tn=256, tr=512):
    t, k = x.shape
    nj = D_FF // tn

    def body(x_ref, g_ref, wg_ref, wv_ref, w_ref, b_ref, u_ref, c_ref, h_ref, f_ref, halo):
        i, j = pl.program_id(0), pl.program_id(1)

        @pl.when(j == 0)
        def _():
            xv = x_ref[...]
            r = lax.rsqrt(jnp.mean(xv * xv, axis=-1, keepdims=True) + EPS)
            h_ref[...] = (xv * r * g_ref[...]).astype(BF16)

        @pl.when(i == 0)
        def _():
            halo[j] = jnp.zeros((2, 8, tn), F32)

        def matmuls(r):
            rows = slice(r * tr, (r + 1) * tr)
            return [_dot(h_ref[rows, :], wt_ref[...], NT) for wt_ref in (wg_ref, wv_ref)]

        def epilogue(r, us, before):
            rows = slice(r * tr, (r + 1) * tr)
            parts = []
            for part, u in enumerate(us):
                u_ref[part, rows, :] = u.astype(BF16)
                parts.append(_conv3(u, before[part], w_ref, b_ref, part))
                c_ref[part, rows, :] = parts[-1].astype(BF16)
            gate, val = parts
            f_ref[rows, :] = (gate * _sigmoid(gate) * val).astype(BF16)
            return [u[tr - 8:, :] for u in us]

        before = [halo[j, 0], halo[j, 1]]
        pending = matmuls(0)
        for r in range(tm // tr):
            ahead = matmuls(r + 1) if r + 1 < tm // tr else None
            before = epilogue(r, pending, before)
            pending = ahead
        halo[j, 0], halo[j, 1] = before

    return pl.pallas_call(
        body, name="up_proj", grid=(t // tm, nj),
        in_specs=[pl.BlockSpec((tm, k), lambda i, j: (i, 0)), pl.BlockSpec((1, k), lambda i, j: (0, 0)),
                  pl.BlockSpec((tn, k), lambda i, j: (j, 0)), pl.BlockSpec((tn, k), lambda i, j: (j + nj, 0)),
                  pl.BlockSpec((3, 2, 1, tn), lambda i, j: (0, 0, 0, j)), pl.BlockSpec((2, 1, tn), lambda i, j: (0, 0, j))],
        out_specs=[pl.BlockSpec((2, tm, tn), lambda i, j: (0, i, j)), pl.BlockSpec((2, tm, tn), lambda i, j: (0, i, j)),
                   pl.BlockSpec((tm, k), lambda i, j: (i, 0)), pl.BlockSpec((tm, tn), lambda i, j: (i, j))],
        out_shape=[SDS((2, t, D_FF), BF16), SDS((2, t, D_FF), BF16), SDS((t, k), BF16), SDS((t, D_FF), BF16)],
        scratch_shapes=[pltpu.VMEM((nj, 2, 8, tn), F32)],
        compiler_params=_params("arbitrary", "arbitrary"))(x, g, w_up_t, w_up_t, cw, cb)


def _ffn_bwd(dx2, w_down, u, c, cw, tm=512, tn=1408):
    t = u.shape[1]
    nj, ni = D_FF // tn, t // tm
    rev = lambda i: ni - 1 - i

    def body(dx_ref, wd_ref, u_ref, c_ref, w_ref, du_ref, dcw_ref, dcb_ref, nxt):
        i = pl.program_id(1)

        @pl.when(i == 0)
        def _():
            nxt[...] = jnp.zeros_like(nxt)
            dcw_ref[...] = jnp.zeros_like(dcw_ref)
            dcb_ref[...] = jnp.zeros_like(dcb_ref)

        df = _dot(dx_ref[...].astype(BF16), wd_ref[...], NT)
        gate, val = c_ref[0].astype(F32), c_ref[1].astype(F32)
        sg = _sigmoid(gate)
        dgate = df * val * (sg * (1.0 + gate * (1.0 - sg)))
        dval = df * (gate * sg)
        for part, d in enumerate((dgate, dval)):
            uu = u_ref[part].astype(F32)
            dcb_ref[part] += jnp.sum(d, axis=0, keepdims=True)
            ahead = nxt[part]
            acc = w_ref[2, part] * d
            dcw_ref[2, part] += jnp.sum(d * uu, axis=0, keepdims=True)
            for k in range(2):
                d_up = _shift_up(d, ahead, 2 - k)
                dcw_ref[k, part] += jnp.sum(uu * d_up, axis=0, keepdims=True)
                acc = acc + w_ref[k, part] * d_up
            nxt[part] = d[0:8, :]
            du_ref[part] = acc.astype(BF16)

    w_spec = pl.BlockSpec((3, 2, 1, tn), lambda j, i: (0, 0, 0, j))
    b_spec = pl.BlockSpec((2, 1, tn), lambda j, i: (0, 0, j))
    tile = pl.BlockSpec((2, tm, tn), lambda j, i: (0, rev(i), j))
    return pl.pallas_call(
        body, name="ffn_bwd", grid=(nj, ni),
        in_specs=[pl.BlockSpec((tm, D_MODEL), lambda j, i: (rev(i), 0)), pl.BlockSpec((tn, D_MODEL), lambda j, i: (j, 0)),
                  tile, tile, w_spec],
        out_specs=[tile, w_spec, b_spec],
        out_shape=[SDS((2, t, D_FF), BF16), SDS((3, 2, 1, D_FF), F32), SDS((2, 1, D_FF), F32)],
        scratch_shapes=[pltpu.VMEM((2, 8, tn), F32)],
        compiler_params=_params("parallel", "arbitrary"))(dx2, w_down, u, c, cw)


def _ple_loss(x2, g, w_gate, p, w_proj_t, target, tm=512):
    t = x2.shape[0]

    def body(x_ref, g_ref, wg_ref, p_ref, wp_ref, tg_ref, dx_ref, dpre_ref, dpp_ref, h_ref, loss_ref, dg_ref):
        i = pl.program_id(0)
        xv = x_ref[...]
        r = lax.rsqrt(jnp.mean(xv * xv, axis=-1, keepdims=True) + EPS)
        xh = xv * r
        h = (xh * g_ref[...]).astype(BF16)
        h_ref[...] = h
        gate = _sigmoid(_dot(h, wg_ref[...]))
        pp = _dot(p_ref[...].astype(BF16), wp_ref[...], NT)
        err = (xv + gate * pp) - tg_ref[...]

        @pl.when(i == 0)
        def _():
            loss_ref[...] = jnp.zeros_like(loss_ref)
            dg_ref[...] = jnp.zeros_like(dg_ref)

        loss_ref[...] += 0.5 * jnp.sum(jnp.mean(err * err, axis=-1, keepdims=True), axis=0, keepdims=True)
        dy = err * (1.0 / D_MODEL)
        dpre = (dy * pp * gate * (1.0 - gate)).astype(BF16)
        dpre_ref[...] = dpre
        dpp_ref[...] = (dy * gate).astype(BF16)
        dh = _dot(dpre, wg_ref[...], NT)
        dg_ref[...] += jnp.sum(dh * xh, axis=0, keepdims=True)
        gd = dh * g_ref[...]
        dx_ref[...] = dy + r * (gd - xh * jnp.mean(gd * xh, axis=-1, keepdims=True))

    row = lambda w: pl.BlockSpec((tm, w), lambda i: (i, 0))
    full = lambda a: pl.BlockSpec(a.shape, lambda i: (0, 0))
    return pl.pallas_call(
        body, name="ple_loss", grid=(t // tm,),
        in_specs=[row(D_MODEL), full(g), full(w_gate), row(PLE_DIM), full(w_proj_t), row(D_MODEL)],
        out_specs=[row(D_MODEL), row(D_MODEL), row(D_MODEL), row(D_MODEL),
                   pl.BlockSpec((1, 128), lambda i: (0, 0)), pl.BlockSpec((1, D_MODEL), lambda i: (0, 0))],
        out_shape=[SDS((t, D_MODEL), F32), SDS((t, D_MODEL), BF16), SDS((t, D_MODEL), BF16), SDS((t, D_MODEL), BF16),
                   SDS((1, 128), F32), SDS((1, D_MODEL), F32)],
        compiler_params=_params("arbitrary"))(x2, g, w_gate, p, w_proj_t, target)


def _all_gather(arrays, name):
    n_a = len(arrays)

    def body(*refs):
        src, dst = refs[:n_a], refs[n_a:2 * n_a]
        send_sems, recv_sems, local_sems = refs[2 * n_a:]
        x, y, c = lax.axis_index("x"), lax.axis_index("y"), lax.axis_index("c")
        slot = lambda px, py, pc: 4 * px + 2 * py + pc
        me, sibling = (x, y, c), (x, y, 1 - c)
        chips = [(1 - x, y), (x, 1 - y), (1 - x, 1 - y)]

        def copy(a, k, block, to, own=False):
            return pltpu.make_async_remote_copy(
                src_ref=src[a] if own else dst[a].at[slot(*block)], dst_ref=dst[a].at[slot(*block)],
                send_sem=send_sems.at[a, k], recv_sem=recv_sems.at[a, k], device_id=to,
                device_id_type=pl.DeviceIdType.MESH)

        local = [pltpu.make_async_copy(src[a], dst[a].at[slot(*me)], local_sems.at[a]) for a in range(n_a)]
        for cp in local:
            cp.start()
        sends = []
        for a in range(n_a):
            sends.append(copy(a, 0, me, sibling, own=True))
            sends += [copy(a, 1 + j, me, (*chip, c), own=True) for j, chip in enumerate(chips)]
        for cp in sends:
            cp.start()
        for j, chip in enumerate(chips):
            for a in range(n_a):
                copy(a, 1 + j, (*chip, c), me).wait_recv()
                passed = copy(a, 4 + j, (*chip, c), sibling)
                passed.start()
                sends.append(passed)
        for a in range(n_a):
            copy(a, 0, sibling, me).wait_recv()
            for j, chip in enumerate(chips):
                copy(a, 4 + j, (*chip, 1 - c), me).wait_recv()
        for cp in sends:
            cp.wait_send()
        for cp in local:
            cp.wait()

    hbm = pl.BlockSpec(memory_space=pl.ANY)
    return pl.pallas_call(
        body, name=name, in_specs=[hbm] * n_a, out_specs=[hbm] * n_a,
        out_shape=[SDS((N_DEV,) + a.shape, a.dtype) for a in arrays],
        scratch_shapes=[pltpu.SemaphoreType.DMA((n_a, N_DEV - 1)), pltpu.SemaphoreType.DMA((n_a, N_DEV - 1)),
                        pltpu.SemaphoreType.DMA((n_a,))],
        )(*arrays)


def _peer(k):
    x, y, c = lax.axis_index("x"), lax.axis_index("y"), lax.axis_index("c")
    px = 1 - x if k & 4 else x
    py = 1 - y if k & 2 else y
    pc = 1 - c if k & 1 else c
    return (px, py, pc), 4 * px + 2 * py + pc


_HBM = pl.BlockSpec(memory_space=pltpu.HBM)
_SEM = pl.BlockSpec(memory_space=pltpu.SEMAPHORE)


def _split_copies(src, land, send_sems, recv_sems, scatter, arrivals):
    _, me = _peer(0)
    out = []
    for k in range(1, N_DEV):
        coords, peer = _peer(k)
        for a in range(len(src)):
            sem = a * (N_DEV - 1) + k - 1
            if scatter[a]:
                s, d = src[a].at[peer], land[a].at[k]
            else:
                s, d = src[a], land[a].at[peer if arrivals else me]
            out.append(pltpu.make_async_remote_copy(
                src_ref=s, dst_ref=d, send_sem=send_sems.at[sem], recv_sem=recv_sems.at[sem], device_id=coords,
                device_id_type=pl.DeviceIdType.MESH))
    return out


def _exchange_start(srcs, lands, scatter, name):
    n = len(srcs)

    def body(*refs):
        src, land = refs[:n], refs[n:2 * n]
        send_sems, recv_sems = refs[2 * n], refs[2 * n + 1]
        token = refs[-1]
        for cp in _split_copies(src, land, send_sems, recv_sems, scatter, False):
            cp.start()
        token[...] = jnp.zeros_like(token)

    hbm_shape = lambda a: pltpu.HBM(a.shape, a.dtype)
    sem_shape = pltpu.SemaphoreType.DMA((n * (N_DEV - 1),))
    outs = pl.pallas_call(
        body, name=name,
        out_shape=(sem_shape, sem_shape, *[hbm_shape(a) for a in srcs], *[hbm_shape(a) for a in lands],
                   SDS((8, 128), F32)),
        in_specs=[_HBM] * (2 * n), out_specs=(_SEM, _SEM, *[_HBM] * (2 * n), pl.BlockSpec(memory_space=pltpu.VMEM)),
        input_output_aliases={a: 2 + a for a in range(2 * n)},
        compiler_params=pltpu.CompilerParams(has_side_effects=pltpu.SideEffectType.DATAFLOW_SIDE_EFFECTING),
    )(*[pltpu.with_memory_space_constraint(a, pltpu.HBM) for a in list(srcs) + list(lands)])
    return outs[0], outs[1], outs[2:2 + n], outs[2 + n:2 + 2 * n], outs[-1]


def _exchange_wait(send_sems, recv_sems, srcs, lands, scatter, after, name):
    n = len(srcs)

    def body(*refs):
        src, land = refs[:n], refs[n:2 * n]
        for cp in _split_copies(src, land, refs[2 * n], refs[2 * n + 1], scatter, False):
            cp.wait_send()
        for cp in _split_copies(src, land, refs[2 * n], refs[2 * n + 1], scatter, True):
            cp.wait_recv()

    hbm_shape = lambda a: pltpu.HBM(a.shape, a.dtype)
    outs = pl.pallas_call(
        body, name=name, out_shape=tuple(hbm_shape(a) for a in list(srcs) + list(lands)),
        in_specs=[_HBM] * (2 * n) + [_SEM, _SEM, pl.BlockSpec(memory_space=pl.ANY)], out_specs=(_HBM,) * (2 * n),
        input_output_aliases={a: a for a in range(2 * n)},
        compiler_params=pltpu.CompilerParams(has_side_effects=pltpu.SideEffectType.DATAFLOW_SIDE_EFFECTING),
    )(*srcs, *lands, send_sems, recv_sems, after)
    return outs[:n], outs[n:]


def _reduce8(a, tr, name):
    _, rows, cols = a.shape

    def body(a_ref, o_ref):
        acc = a_ref[0]
        for j in range(1, N_DEV):
            acc = acc + a_ref[j]
        o_ref[...] = acc

    return pl.pallas_call(
        body, name=name, grid=(rows // tr,),
        in_specs=[pl.BlockSpec((N_DEV, tr, cols), lambda i: (0, i, 0))],
        out_specs=pl.BlockSpec((tr, cols), lambda i: (i, 0)), out_shape=SDS((rows, cols), F32),
        compiler_params=_params("parallel"))(a)


def _reduce_landed(own, land, name, tc=256):
    rows, cols = own.shape

    def body(own_ref, land_ref, o_ref):
        acc = own_ref[...].astype(F32)
        for k in range(1, N_DEV):
            acc = acc + land_ref[k].astype(F32)
        o_ref[...] = acc

    return pl.pallas_call(
        body, name=name, grid=(cols // tc,),
        in_specs=[pl.BlockSpec((rows, tc), lambda j: (0, j)), pl.BlockSpec((N_DEV, rows, tc), lambda j: (0, 0, j))],
        out_specs=pl.BlockSpec((rows, tc), lambda j: (0, j)), out_shape=SDS((rows, cols), F32),
        compiler_params=_params("parallel"))(own, land)


def _adamw(w, g, m, v, name, tr=None):
    rows, cols = w.shape
    tr = rows if tr is None else tr

    def body(w_ref, g_ref, m_ref, v_ref, d_ref, mo_ref, vo_ref):
        d_ref[...], mo_ref[...], vo_ref[...] = _adam_update(w_ref[...], g_ref[...], m_ref[...], v_ref[...])

    blk = pl.BlockSpec((tr, cols), lambda i: (i, 0))
    return pl.pallas_call(
        body, name=name, grid=(rows // tr,), in_specs=[blk] * 4, out_specs=[blk] * 3,
        out_shape=[SDS((rows, cols), F32)] * 3, compiler_params=_params("parallel"))(w, g, m, v)


def _pad_rows(a, rows):
    return jnp.pad(a, ((0, rows - a.shape[0]),) + ((0, 0),) * (a.ndim - 1))


def _local_step(x, p, target, sm, wts, fetch_rest, send, tok):
    ones_q, ones_k, dup, dup_t = _head_consts()
    tri, triu, expand, expand_t = _ssd_consts()
    w_in_t = wts["in_t"]
    cwx, cwb = wts["ssm_cw"][:, :SSM_INNER], wts["ssm_cw"][:, SSM_INNER:]
    cbx, cbb = sm["ssm_conv_b"][:, :SSM_INNER], sm["ssm_conv_b"][:, SSM_INNER:]
    pad128 = lambda a: jnp.pad(a, ((0, 0), (0, 128 - a.shape[1])))
    dtb, alog = pad128(sm["dt_bias"]), pad128(sm["a_log"])
    dsk_e = jnp.repeat(sm["d_skip"], HEAD_DIM, axis=1)
    gq = jnp.tile(sm["q_norm_g"], (1, ATTN_DIM // HEAD_DIM))
    gk = jnp.tile(sm["k_norm_g"], (1, KV_DIM // HEAD_DIM))
    ffn_cw = wts["ffn_cw"].reshape(3, 2, 1, D_FF)
    ffn_cb = sm["ffn_conv_b"].reshape(2, 1, D_FF)

    proj, h1 = _norm_matmul(x, sm["attn_norm_g"] + tok, w_in_t, 1024, 768, "in_proj")
    qn, kd, vd = _attn_prep(proj, gq, gk, ones_q, ones_k, dup)
    attn_out, lse = _attn_fwd(qn, kd, vd)
    y_ssd, hs, ssm_out, pre_x, pre_b = _ssd_fwd(proj, cwx, cbx, cwb, cbb, dtb, alog, dsk_e, sm["ssm_norm_g"], tri,
                                                expand)
    rest = fetch_rest(ssm_out)
    w_out, w_up_t, w_down, w_gate, w_proj_t = (rest[k] for k in ("out", "up_t", "down", "gate", "proj_t"))
    x1 = _mm_resid([(attn_out, None, w_out[:ATTN_DIM]), (ssm_out, None, w_out[ATTN_DIM:])], x, None, 512, F32,
                   "out_proj")
    u, uc, h2, f = _up_act(x1, sm["ffn_norm_g"], w_up_t, ffn_cw, ffn_cb)
    x2 =_mm_resid([(f, None, w_down)], x1, None, 512, F32, "down_proj")
    dx2, dpre, dpp, h3, loss, dg_ple = _ple_loss(x2, sm["ple_norm_g"], w_gate, p, w_proj_t, target)

    g_gate = _wgrad(h3, None, dpre, "wg_gate")
    g_proj_t = _wgrad(dpp, None, p, "wg_proj")
    g_down = _wgrad(f, None, dx2, "wg_down")
    du, d_ffn_cw, d_ffn_cb = _ffn_bwd(dx2, w_down, u, uc, ffn_cw)
    dx1, dg_ffn = _mm_normbwd([(du, 0, w_up_t, D_FF, 0), (du, 1, w_up_t, D_FF, 1)], x1, sm["ffn_norm_g"], dx2, 512,
                              "up_proj_bwd")
    g_up_t = _wgrad(du, "all", h2, "wg_up")
    tok = send(dict(gate=g_gate, proj_t=g_proj_t, down=g_down, up_t=g_up_t)).astype(BF16)
    d_attn = _mm_resid([(dx1, None, w_out[:ATTN_DIM] + tok)], None, NT, 512, F32, "out_proj_bwd_attn")
    d_ssm = _mm_resid([(dx1, None, w_out[ATTN_DIM:] + tok)], None, NT, 512, F32, "out_proj_bwd_ssm")
    g_out = jnp.concatenate([_wgrad(attn_out, None, dx1, "wg_out_attn"), _wgrad(ssm_out, None, dx1, "wg_out_ssm")],
                            axis=0)
    tok = send(dict(out=g_out))
    (dz, dxs, dbc, ddt, dg_ssm, d_dsk_e, d_alog, d_dtb, d_cwx, d_cbx, d_cwb, d_cbb) = _ssd_bwd(
        proj, pre_x, pre_b, y_ssd, hs, d_ssm, cwx, cwb, dtb + tok, alog, dsk_e, sm["ssm_norm_g"], tri, triu, expand,
        expand_t)
    dqn, dkc, dkp, dvc, dvp = _attn_bwd(qn, kd, vd, attn_out, lse, d_attn, ones_k[:128, :128])
    dqkv, dgq, dgk = _attn_prep_bwd(proj, dqn, dkc, dkp, dvc, dvp, gq + tok, gk, ones_q, ones_k, dup_t)
    pieces = [(dqkv, 0, 1024), (dz, 1024, 2048), (dxs, 2048, 3072), (dbc, 3072, 3584), (ddt, 3584, 3712)]
    g_in_t = jnp.concatenate([_wgrad(a, None, h1, "wg_in_%d" % lo) for a, lo, _ in pieces], axis=0)[:IN_PROJ]
    tok = send(dict(in_t=g_in_t))
    grad_x, dg_attn = _mm_normbwd([(a, None, w_in_t, hi - lo, lo // (hi - lo)) for a, lo, hi in pieces], x,
                                  sm["attn_norm_g"] + tok, dx1, 512, "in_proj_bwd")

    small = dict(
        attn_norm_g=dg_attn, q_norm_g=dgq.reshape(-1, HEAD_DIM).sum(0, keepdims=True),
        k_norm_g=dgk.reshape(-1, HEAD_DIM).sum(0, keepdims=True),
        ssm_conv_w=jnp.concatenate([d_cwx, d_cwb], axis=1), ssm_conv_b=jnp.concatenate([d_cbx, d_cbb], axis=1),
        dt_bias=d_dtb[:, :SSM_HEADS], a_log=d_alog[:, :SSM_HEADS],
        d_skip=d_dsk_e.reshape(SSM_HEADS, HEAD_DIM).sum(1)[None, :], ssm_norm_g=dg_ssm, ffn_norm_g=dg_ffn,
        ffn_conv_w=d_ffn_cw.reshape(3, 2 * D_FF), ffn_conv_b=d_ffn_cb.reshape(1, 2 * D_FF), ple_norm_g=dg_ple)
    return loss[0, 0], grad_x, small


_SMALL = (("attn_norm_g", 1, 1024), ("q_norm_g", 1, 64), ("k_norm_g", 1, 64), ("ssm_conv_w", 4, XBC_DIM),
          ("ssm_conv_b", 1, XBC_DIM), ("dt_bias", 1, 16), ("a_log", 1, 16), ("d_skip", 1, 16), ("ssm_norm_g", 1, 1024),
          ("ffn_norm_g", 1, 1024), ("ffn_conv_w", 3, 2 * D_FF), ("ffn_conv_b", 1, 2 * D_FF), ("ple_norm_g", 1, 1024))
_SMALL_ROWS, _SMALL_COLS = 32, XBC_DIM
_SHARDED_SMALL = ("ssm_conv_w", "ffn_conv_w")


def _small_chunks(n):
    return 1 if n <= _SMALL_COLS else 4


def _pack_small(parts, loss):
    rows = []
    for k, r, n in _SMALL:
        c = _small_chunks(n)
        rows.append(jnp.pad(parts[k].reshape(r * c, n // c), ((0, 0), (0, _SMALL_COLS - n // c))))
    packed = _pad_rows(jnp.concatenate(rows, axis=0), _SMALL_ROWS)
    at_loss = ((lax.broadcasted_iota(jnp.int32, packed.shape, 0) == _SMALL_ROWS - 1) &
               (lax.broadcasted_iota(jnp.int32, packed.shape, 1) == 0))
    return jnp.where(at_loss, loss, packed)


def _adam_update(w, g, m, v):
    c1 = 1.0 - ADAM_B1 ** ADAM_STEP
    c2 = 1.0 - ADAM_B2 ** ADAM_STEP
    mn = ADAM_B1 * m + (1.0 - ADAM_B1) * g
    vn = ADAM_B2 * v + (1.0 - ADAM_B2) * (g * g)
    return -ADAM_LR * ((mn / c1) / (jnp.sqrt(vn / c2) + ADAM_EPS) + ADAM_WD * w), mn, vn


def _adamw_small(g_all, g_shard, w, m, v):
    ins, shapes = [g_all], []
    for k, _, _ in _SMALL:
        shape2 = w[k].shape if w[k].ndim == 2 else (1, w[k].shape[0])
        shapes.append(shape2)
        ins += ([g_shard[k]] if k in _SHARDED_SMALL else []) + [a.reshape(shape2) for a in (w[k], m[k], v[k])]

    def body(*refs):
        g_ref, pos, row = refs[0], 1, 0
        outs = refs[len(ins):]
        for i, (k, r, n) in enumerate(_SMALL):
            c = _small_chunks(n)
            if k in _SHARDED_SMALL:
                g = refs[pos][...]
                pos += 1
            elif c == 1:
                g = g_ref[row:row + r, 0:n]
            else:
                g = jnp.concatenate([g_ref[row + j:row + j + 1, 0:n // c] for j in range(c)], axis=1)
            row += r * c
            d, mn, vn = _adam_update(refs[pos][...], g, refs[pos + 1][...], refs[pos + 2][...])
            pos += 3
            for o_ref, val in zip(outs[4 * i:4 * i + 4], (g, d, mn, vn)):
                o_ref[...] = val

    res = pl.pallas_call(body, name="adamw_small",
                         out_shape=[SDS(s, F32) for s in shapes for _ in range(4)])(*ins)
    return {k: tuple(a.reshape(w[k].shape) for a in res[4 * i:4 * i + 4]) for i, (k, _, _) in enumerate(_SMALL)}


def kernel(x, p, attn_norm_g, w_in, q_norm_g, k_norm_g, ssm_conv_w, ssm_conv_b, dt_bias, a_log, d_skip, ssm_norm_g, w_out, ffn_norm_g, w_up, ffn_conv_w, ffn_conv_b, w_down, ple_norm_g, w_ple_gate, w_ple_proj, loss_target, m_attn_norm_g, m_w_in, m_q_norm_g, m_k_norm_g, m_ssm_conv_w, m_ssm_conv_b, m_dt_bias, m_a_log, m_d_skip, m_ssm_norm_g, m_w_out, m_ffn_norm_g, m_w_up, m_ffn_conv_w, m_ffn_conv_b, m_w_down, m_ple_norm_g, m_w_ple_gate, m_w_ple_proj, v_attn_norm_g, v_w_in, v_q_norm_g, v_k_norm_g, v_ssm_conv_w, v_ssm_conv_b, v_dt_bias, v_a_log, v_d_skip, v_ssm_norm_g, v_w_out, v_ffn_norm_g, v_w_up, v_ffn_conv_w, v_ffn_conv_b, v_w_down, v_ple_norm_g, v_w_ple_gate, v_w_ple_proj):
    names = ("attn_norm_g", "w_in", "q_norm_g", "k_norm_g", "ssm_conv_w", "ssm_conv_b", "dt_bias", "a_log", "d_skip",
             "ssm_norm_g", "w_out", "ffn_norm_g", "w_up", "ffn_conv_w", "ffn_conv_b", "w_down", "ple_norm_g",
             "w_ple_gate", "w_ple_proj")
    w = dict(zip(names, (attn_norm_g, w_in, q_norm_g, k_norm_g, ssm_conv_w, ssm_conv_b, dt_bias, a_log, d_skip,
                         ssm_norm_g, w_out, ffn_norm_g, w_up, ffn_conv_w, ffn_conv_b, w_down, ple_norm_g, w_ple_gate,
                         w_ple_proj)))
    m = dict(zip(names, (m_attn_norm_g, m_w_in, m_q_norm_g, m_k_norm_g, m_ssm_conv_w, m_ssm_conv_b, m_dt_bias,
                         m_a_log, m_d_skip, m_ssm_norm_g, m_w_out, m_ffn_norm_g, m_w_up, m_ffn_conv_w, m_ffn_conv_b,
                         m_w_down, m_ple_norm_g, m_w_ple_gate, m_w_ple_proj)))
    v = dict(zip(names, (v_attn_norm_g, v_w_in, v_q_norm_g, v_k_norm_g, v_ssm_conv_w, v_ssm_conv_b, v_dt_bias,
                         v_a_log, v_d_skip, v_ssm_norm_g, v_w_out, v_ffn_norm_g, v_w_up, v_ffn_conv_w, v_ffn_conv_b,
                         v_w_down, v_ple_norm_g, v_w_ple_gate, v_w_ple_proj)))
    w, m, v = ({k: a[0] for k, a in d.items()} for d in (w, m, v))
    me = 4 * lax.axis_index("x") + 2 * lax.axis_index("y") + lax.axis_index("c")

    mine = dict(in_t=w["w_in"].T, out=w["w_out"], up_t=w["w_up"].T, down=w["w_down"], gate=w["w_ple_gate"],
                proj_t=w["w_ple_proj"].T)
    mine = {k: a.astype(BF16) for k, a in mine.items()}
    conv_pack = jnp.pad(jnp.concatenate([w["ssm_conv_w"].reshape(-1), w["ffn_conv_w"].reshape(-1)]),
                        (0, 3072 - 2880)).reshape(8, 384)
    all_in, all_conv = _all_gather([mine["in_t"], conv_pack], "gather_first")
    later = ("out", "up_t", "down", "gate", "proj_t")
    zones = [lax.dynamic_update_slice(lax.empty((N_DEV,) + mine[k].shape, BF16), mine[k][None], (me, 0, 0))
             for k in later]
    zones, all_in, all_conv = lax.optimization_barrier((zones, all_in, all_conv))
    rest_state = _exchange_start([mine[k] for k in later], zones, [False] * len(later), "gather_rest_start")

    def fetch_rest(after):
        _, landed = _exchange_wait(*rest_state[:4], [False] * len(later), after, "gather_rest_wait")
        return {k: a.reshape(N_DEV * a.shape[1], a.shape[2]) for k, a in zip(later, landed)}

    wts = dict(in_t=_pad_rows(all_in.reshape(IN_PROJ, D_MODEL), IN_PROJ_PAD))
    conv_flat = all_conv.reshape(N_DEV, 3072)
    wts["ssm_cw"] = conv_flat[:, :768].reshape(N_DEV, 4, XBC_DIM // N_DEV).transpose(1, 0, 2).reshape(4, XBC_DIM)
    wts["ffn_cw"] = conv_flat[:, 768:2880].reshape(N_DEV, 3, 2 * D_FF // N_DEV).transpose(1, 0, 2).reshape(3, 2 * D_FF)
    sm = {k: w[k].reshape(1, -1) for k, _, _ in _SMALL if k not in _SHARDED_SMALL}

    in_flight = []

    def send(grads):
        keys = sorted(grads)
        srcs = [grads[k].reshape(N_DEV, grads[k].shape[0] // N_DEV, grads[k].shape[1]) for k in keys]
        state = _exchange_start(srcs, [lax.empty(a.shape, BF16) for a in srcs], [True] * len(keys),
                                "send_" + "_".join(keys))
        in_flight.append((keys, state))
        return state[4][0:1, 0:1]

    loss, grad_x, small = _local_step(x[0], p[0, 0], loss_target[0], sm, wts, fetch_rest, send,
                                      rest_state[4][0:1, 0:1])

    (got_small,) = _all_gather([_pack_small(small, loss)], "gather_small_grads")
    g_small = _reduce8(got_small, _SMALL_ROWS, "reduce_small")
    loss = g_small[_SMALL_ROWS - 1, 0]
    grads = {}
    for keys, state in in_flight:
        sent, landed = _exchange_wait(*state[:4], [True] * len(keys), grad_x, "wait_" + "_".join(keys))
        for k, shares, land in zip(keys, sent, landed):
            grads[k] = _reduce_landed(lax.dynamic_index_in_dim(shares, me, 0, keepdims=False), land, "reduce_" + k)
    gw = {"w_in": grads["in_t"].T, "w_out": grads["out"], "w_up": grads["up_t"].T, "w_down": grads["down"],
          "w_ple_gate": grads["gate"], "w_ple_proj": grads["proj_t"].T}
    n_ssm, n_ffn = XBC_DIM // N_DEV, 2 * D_FF // N_DEV
    g_shard = {"ssm_conv_w": lax.dynamic_slice(g_small, (3, me * n_ssm), (4, n_ssm)),
               "ffn_conv_w": lax.dynamic_slice(g_small[13:25, :2 * D_FF // 4].reshape(3, 2 * D_FF), (0, me * n_ffn),
                                               (3, n_ffn))}

    delta, new_m, new_v = {}, {}, {}
    for k, tr in (("w_in", 256), ("w_out", None), ("w_up", 256), ("w_down", None), ("w_ple_gate", None),
                  ("w_ple_proj", None)):
        delta[k], new_m[k], new_v[k] = _adamw(w[k], gw[k], m[k], v[k], "adamw_" + k, tr)
    for k, (g_k, d_k, m_k, v_k) in _adamw_small(g_small, g_shard, w, m, v).items():
        gw[k], delta[k], new_m[k], new_v[k] = g_k, d_k, m_k, v_k

    outs = [loss, grad_x[None]]
    for d in (gw, delta, new_m, new_v):
        outs += [d[k][None] for k in names]
    return tuple(outs)
```

```python
import functools

import numpy as np
import jax
import jax.numpy as jnp
from jax import lax
from jax.experimental import pallas as pl
from jax.experimental.pallas import tpu as pltpu

F32 = jnp.float32
BF16 = jnp.bfloat16
SDS = jax.ShapeDtypeStruct
EPS = 1e-6
N_DEV = 8
D_MODEL = 1024
HEAD_DIM = 64
ATTN_DIM = 512
KV_DIM = 256
SSM_INNER = 1024
SSM_HEADS = 16
BC_DIM = 256
XBC_DIM = SSM_INNER + 2 * BC_DIM
MIX_DIM = ATTN_DIM + SSM_INNER
IN_PROJ = 3600
IN_PROJ_PAD = 3840
D_FF = 2816
PLE_DIM = 256
CHUNK = 128
SUPER = 2048
DILATIONS = (1, 4, 16)
TILE_UNROLL = 8
VMEM_LIMIT = 56 * 1024 * 1024
ADAM_LR, ADAM_B1, ADAM_B2, ADAM_EPS, ADAM_WD, ADAM_STEP = 0.001, 0.9, 0.999, 1e-08, 0.01, 10

NT = (((1,), (1,)), ((), ()))
TN = (((0,), (0,)), ((), ()))


def _params(*sem):
    return pltpu.CompilerParams(dimension_semantics=sem if sem else None, vmem_limit_bytes=VMEM_LIMIT)


def _dot(a, b, dims=None):
    if dims is None:
        return jnp.dot(a, b, preferred_element_type=F32)
    return lax.dot_general(a, b, dims, preferred_element_type=F32)


def _hdot(a, b, parts=2):
    a_exact = a.dtype == BF16
    x = b if a_exact else a
    acc = None
    for _ in range(parts):
        piece = x.astype(BF16)
        x = x - piece.astype(F32)
        d = _dot(a, piece) if a_exact else _dot(piece, b)
        acc = d if acc is None else acc + d
    return acc


def _sigmoid(x):
    return 0.5 * jnp.tanh(0.5 * x) + 0.5


def _shift_down(x, halo8, s):
    xr = pltpu.roll(x, s, 0)
    row = lax.broadcasted_iota(jnp.int32, halo8.shape, 0)
    first = jnp.where(row < s, pltpu.roll(halo8, s, 0), xr[0:8])
    return jnp.concatenate([first, xr[8:]], axis=0)


def _shift_up(x, halo8, s):
    n = x.shape[0]
    xr = pltpu.roll(x, n - s, 0)
    row = lax.broadcasted_iota(jnp.int32, halo8.shape, 0)
    last = jnp.where(row >= 8 - s, pltpu.roll(halo8, 8 - s, 0), xr[n - 8:])
    return jnp.concatenate([xr[:n - 8], last], axis=0)


def _norm_matmul(x, g, wt, tm, tn, name):
    t, k = x.shape
    n = wt.shape[0]

    def body(x_ref, g_ref, w_ref, o_ref, h_ref):
        @pl.when(pl.program_id(1) == 0)
        def _():
            xv = x_ref[...]
            r = lax.rsqrt(jnp.mean(xv * xv, axis=-1, keepdims=True) + EPS)
            h_ref[...] = (xv * r * g_ref[...]).astype(BF16)
        o_ref[...] = _dot(h_ref[...], w_ref[...], NT)

    return pl.pallas_call(
        body, name=name, grid=(t // tm, n // tn),
        in_specs=[pl.BlockSpec((tm, k), lambda i, j: (i, 0)), pl.BlockSpec((1, k), lambda i, j: (0, 0)),
                  pl.BlockSpec((tn, k), lambda i, j: (j, 0))],
        out_specs=[pl.BlockSpec((tm, tn), lambda i, j: (i, j)), pl.BlockSpec((tm, k), lambda i, j: (i, 0))],
        out_shape=[SDS((t, n), F32), SDS((t, k), BF16)],
        compiler_params=_params("parallel", "arbitrary"))(x, g, wt)


def _a_spec(a, lead, tm):
    if lead is None:
        return pl.BlockSpec((tm, a.shape[-1]), lambda i: (i, 0))
    return pl.BlockSpec((None, tm, a.shape[-1]), lambda i, _l=lead: (_l, i, 0))


def _mm_resid(pairs, res, dims, tm, out_dtype, name):
    t = pairs[0][0].shape[-2]
    n = pairs[0][2].shape[1] if dims is None else pairs[0][2].shape[0]
    np_ = len(pairs)

    def body(*refs):
        o_ref = refs[-1]
        acc = refs[2 * np_][...] if res is not None else None
        for q in range(np_):
            d = _dot(refs[q][...].astype(BF16), refs[np_ + q][...], dims)
            acc = d if acc is None else acc + d
        o_ref[...] = acc.astype(out_dtype)

    in_specs = [_a_spec(a, lead, tm) for a, lead, _ in pairs]
    in_specs += [pl.BlockSpec(b.shape, lambda i: (0, 0)) for _, _, b in pairs]
    args = [a for a, _, _ in pairs] + [b for _, _, b in pairs]
    if res is not None:
        in_specs.append(pl.BlockSpec((tm, n), lambda i: (i, 0)))
        args.append(res)
    return pl.pallas_call(
        body, name=name, grid=(t // tm,), in_specs=in_specs,
        out_specs=pl.BlockSpec((tm, n), lambda i: (i, 0)), out_shape=SDS((t, n), out_dtype),
        compiler_params=_params("parallel"))(*args)


def _wgrad_multi(parts, b, name, tk=2048):
    t, n = b.shape
    widths = [a.shape[1] for a in parts]
    m = sum(widths)

    def body(*refs):
        b_ref, o_ref, acc = refs[len(parts):]

        @pl.when(pl.program_id(0) == 0)
        def _():
            acc[...] = jnp.zeros_like(acc)
        bv = b_ref[...].astype(BF16)
        row = 0
        for a_ref, w in zip(refs, widths):
            acc[row:row + w, :] += _dot(a_ref[...].astype(BF16), bv, TN)
            row += w

        @pl.when(pl.program_id(0) == pl.num_programs(0) - 1)
        def _():
            o_ref[...] = acc[...].astype(BF16)

    return pl.pallas_call(
        body, name=name, grid=(t // tk,),
        in_specs=[pl.BlockSpec((tk, w), lambda k: (k, 0)) for w in widths] + [pl.BlockSpec((tk, n), lambda k: (k, 0))],
        out_specs=pl.BlockSpec((m, n), lambda k: (0, 0)), out_shape=SDS((m, n), BF16),
        scratch_shapes=[pltpu.VMEM((m, n), F32)],
        compiler_params=_params("arbitrary"))(*parts, b)


def _out_proj_bwd(dx, w_out, tm=1024):
    t = dx.shape[0]

    def body(dx_ref, w_ref, da_ref, ds_ref):
        a = dx_ref[...].astype(BF16)
        da_ref[...] = _dot(a, w_ref[0:ATTN_DIM, :], NT)
        ds_ref[...] = _dot(a, w_ref[ATTN_DIM:, :], NT)

    return pl.pallas_call(
        body, name="out_proj_bwd", grid=(t // tm,),
        in_specs=[pl.BlockSpec((tm, D_MODEL), lambda i: (i, 0)), pl.BlockSpec(w_out.shape, lambda i: (0, 0))],
        out_specs=[pl.BlockSpec((tm, ATTN_DIM), lambda i: (i, 0)), pl.BlockSpec((tm, SSM_INNER), lambda i: (i, 0))],
        out_shape=[SDS((t, ATTN_DIM), F32), SDS((t, SSM_INNER), F32)],
        compiler_params=_params("parallel"))(dx, w_out)


def _mm_normbwd(pairs, x, g, dres, tm, name):
    t, k = x.shape
    np_ = len(pairs)
    b_specs = [pl.BlockSpec((rows, b.shape[1]), lambda i, _b=blk: (_b, 0)) for _, _, b, rows, blk in pairs]
    pairs = [(a, lead, b) for a, lead, b, _, _ in pairs]

    def body(*refs):
        x_ref, g_ref, dres_ref, dx_ref, dg_ref = refs[2 * np_:]
        dh = None
        for q in range(np_):
            d = _dot(refs[q][...], refs[np_ + q][...])
            dh = d if dh is None else dh + d
        xv = x_ref[...]
        r = lax.rsqrt(jnp.mean(xv * xv, axis=-1, keepdims=True) + EPS)
        xh = xv * r

        @pl.when(pl.program_id(0) == 0)
        def _():
            dg_ref[...] = jnp.zeros_like(dg_ref)
        dg_ref[...] += jnp.sum(dh * xh, axis=0, keepdims=True)
        gd = dh * g_ref[...]
        dx_ref[...] = dres_ref[...] + r * (gd - xh * jnp.mean(gd * xh, axis=-1, keepdims=True))

    in_specs = [_a_spec(a, lead, tm) for a, lead, _ in pairs] + b_specs
    in_specs += [pl.BlockSpec((tm, k), lambda i: (i, 0)), pl.BlockSpec((1, k), lambda i: (0, 0)),
                 pl.BlockSpec((tm, k), lambda i: (i, 0))]
    args = [a for a, _, _ in pairs] + [b for _, _, b in pairs] + [x, g, dres]
    return pl.pallas_call(
        body, name=name, grid=(t // tm,), in_specs=in_specs,
        out_specs=[pl.BlockSpec((tm, k), lambda i: (i, 0)), pl.BlockSpec((1, k), lambda i: (0, 0))],
        out_shape=[SDS((t, k), F32), SDS((1, k), F32)],
        compiler_params=_params("arbitrary"))(*args)


def _wgrad(a, a_lead, b, name, tk=2048):
    t, m = a.shape[-2:]
    n = b.shape[1]
    tm = m if m <= 1024 else 1408
    assert m % tm == 0

    def body(a_ref, b_ref, o_ref, acc):
        @pl.when(pl.program_id(1) == 0)
        def _():
            acc[...] = jnp.zeros_like(acc)
        acc[...] += _dot(a_ref[...].astype(BF16), b_ref[...].astype(BF16), TN)

        @pl.when(pl.program_id(1) == pl.num_programs(1) - 1)
        def _():
            o_ref[...] = acc[...].astype(BF16)

    per, lead = m // tm, 1
    if a_lead == "all":
        lead = a.shape[0]
        a_spec = pl.BlockSpec((None, tk, tm), lambda mi, ki: (mi // per, ki, mi % per))
    elif a_lead is None:
        a_spec = pl.BlockSpec((tk, tm), lambda mi, ki: (ki, mi))
    else:
        a_spec = pl.BlockSpec((None, tk, tm), lambda mi, ki, _l=a_lead: (_l, ki, mi))
    return pl.pallas_call(
        body, name=name, grid=(lead * per, t // tk),
        in_specs=[a_spec, pl.BlockSpec((tk, n), lambda mi, ki: (ki, 0))],
        out_specs=pl.BlockSpec((tm, n), lambda mi, ki: (mi, 0)), out_shape=SDS((lead * m, n), BF16),
        scratch_shapes=[pltpu.VMEM((tm, n), F32)],
        compiler_params=_params("parallel", "arbitrary"))(a, b)


def _head_consts():
    iq = np.arange(ATTN_DIM)
    ik = np.arange(KV_DIM)
    ones_q = (iq[:, None] // HEAD_DIM == iq[None, :] // HEAD_DIM).astype(np.float32)
    ones_k = (ik[:, None] // HEAD_DIM == ik[None, :] // HEAD_DIM).astype(np.float32)
    dup = (ik[:, None] == (HEAD_DIM * (iq[None, :] // 128) + iq[None, :] % HEAD_DIM)).astype(np.float32)
    return jnp.asarray(ones_q, BF16), jnp.asarray(ones_k, BF16), jnp.asarray(dup, BF16), jnp.asarray(dup.T, BF16)


def _attn_prep(proj, gq, gk, ones_q, ones_k, dup, tm=512):
    t = proj.shape[0]

    def body(p_ref, gq_ref, gk_ref, oq_ref, ok_ref, dup_ref, qn_ref, kd_ref, vd_ref):
        q = p_ref[:, 0:ATTN_DIM]
        k = p_ref[:, ATTN_DIM:ATTN_DIM + KV_DIM]
        v = p_ref[:, ATTN_DIM + KV_DIM:]
        rq = lax.rsqrt(_hdot(q * q, oq_ref[...]) * (1.0 / HEAD_DIM) + EPS)
        qn_ref[...] = (q * rq * gq_ref[...]) * (HEAD_DIM ** -0.5)
        rk = lax.rsqrt(_hdot(k * k, ok_ref[...]) * (1.0 / HEAD_DIM) + EPS)
        kn = k * rk * gk_ref[...]
        kd_ref[...] = _dot(kn.astype(BF16), dup_ref[...])
        vd_ref[...] = _dot(v.astype(BF16), dup_ref[...])

    full = lambda a: pl.BlockSpec(a.shape, lambda i: (0, 0))
    o_spec = pl.BlockSpec((tm, ATTN_DIM), lambda i: (i, 0))
    return pl.pallas_call(
        body, name="attn_prep", grid=(t // tm,),
        in_specs=[pl.BlockSpec((tm, 1024), lambda i: (i, 0)), full(gq), full(gk), full(ones_q), full(ones_k), full(dup)],
        out_specs=[o_spec, o_spec, o_spec], out_shape=[SDS((t, ATTN_DIM), F32)] * 3,
        compiler_params=_params("parallel"))(proj, gq, gk, ones_q, ones_k, dup)


def _attn_prep_bwd(proj, dqn, dkc, dkp, dvc, dvp, gq, gk, ones_q, ones_k, dup_t, tm=512):
    t = proj.shape[0]
    nblk = t // tm
    off = SUPER // tm

    def body(p_ref, dqn_ref, dkc_ref, dkp_ref, dvc_ref, dvp_ref, gq_ref, gk_ref, oq_ref, ok_ref, dt_ref,
             o_ref, dgq_ref, dgk_ref):
        i = pl.program_id(0)
        has_next = (i + off < nblk).astype(F32)
        q = p_ref[:, 0:ATTN_DIM]
        k = p_ref[:, ATTN_DIM:ATTN_DIM + KV_DIM]
        dkn = _hdot(dkc_ref[...] + has_next * dkp_ref[...], dt_ref[...])
        dv = _hdot(dvc_ref[...] + has_next * dvp_ref[...], dt_ref[...])

        @pl.when(i == 0)
        def _():
            dgq_ref[...] = jnp.zeros_like(dgq_ref)
            dgk_ref[...] = jnp.zeros_like(dgk_ref)

        rq = lax.rsqrt(_hdot(q * q, oq_ref[...]) * (1.0 / HEAD_DIM) + EPS)
        xh = q * rq
        dy = dqn_ref[...] * (HEAD_DIM ** -0.5)
        dgq_ref[...] += jnp.sum(dy * xh, axis=0, keepdims=True)
        gd = dy * gq_ref[...]
        dq = rq * (gd - xh * (_hdot(gd * xh, oq_ref[...]) * (1.0 / HEAD_DIM)))
        rk = lax.rsqrt(_hdot(k * k, ok_ref[...]) * (1.0 / HEAD_DIM) + EPS)
        kh = k * rk
        dgk_ref[...] += jnp.sum(dkn * kh, axis=0, keepdims=True)
        gdk = dkn * gk_ref[...]
        dk = rk * (gdk - kh * (_hdot(gdk * kh, ok_ref[...]) * (1.0 / HEAD_DIM)))
        o_ref[:, 0:ATTN_DIM] = dq.astype(BF16)
        o_ref[:, ATTN_DIM:ATTN_DIM + KV_DIM] = dk.astype(BF16)
        o_ref[:, ATTN_DIM + KV_DIM:] = dv.astype(BF16)

    full = lambda a: pl.BlockSpec(a.shape, lambda i: (0, 0))
    cur = pl.BlockSpec((tm, ATTN_DIM), lambda i: (i, 0))
    nxt = pl.BlockSpec((tm, ATTN_DIM), lambda i: (jnp.minimum(i + off, nblk - 1), 0))
    return pl.pallas_call(
        body, name="attn_prep_bwd", grid=(nblk,),
        in_specs=[pl.BlockSpec((tm, 1024), lambda i: (i, 0)), cur, cur, nxt, cur, nxt,
                  full(gq), full(gk), full(ones_q), full(ones_k), full(dup_t)],
        out_specs=[pl.BlockSpec((tm, 1024), lambda i: (i, 0)), pl.BlockSpec((1, ATTN_DIM), lambda i: (0, 0)),
                   pl.BlockSpec((1, KV_DIM), lambda i: (0, 0))],
        out_shape=[SDS((t, 1024), BF16), SDS((1, ATTN_DIM), F32), SDS((1, KV_DIM), F32)],
        compiler_params=_params("arbitrary"))(proj, dqn, dkc, dkp, dvc, dvp, gq, gk, ones_q, ones_k, dup_t)


def _tile_masks():
    qi = lax.broadcasted_iota(jnp.int32, (2 * CHUNK, 2 * CHUNK), 0) & (CHUNK - 1)
    kj = lax.broadcasted_iota(jnp.int32, (2 * CHUNK, 2 * CHUNK), 1)
    delta = CHUNK + qi - kj
    band = (delta >= 0) & (delta <= CHUNK)
    return band, kj


def _deinterleave(dst, src, n_rows, d):
    per = n_rows // d
    for r in range(d):
        dst[r * per:(r + 1) * per, :] = src[pl.ds(r, per, stride=d), :]


def _attn_specs(t):
    blk = lambda f: pl.BlockSpec((SUPER, 128), f)
    cur = blk(lambda h, s: (s, h))
    prev = blk(lambda h, s: (jnp.maximum(s - 1, 0), h))
    return cur, prev


def _attn_fwd(qn, kd, vd):
    t = qn.shape[0]
    cur, prev = _attn_specs(t)

    def body(q_ref, kp_ref, kc_ref, vp_ref, vc_ref, o_ref, lse_ref, kk, vv, qd, kdd, vdd, po, pm, pll, acc, mm, ll):
        s = pl.program_id(1)
        kk[0:SUPER, :] = kp_ref[...]
        kk[SUPER:, :] = kc_ref[...]
        vv[0:SUPER, :] = vp_ref[...]
        vv[SUPER:, :] = vc_ref[...]
        m0 = lax.broadcasted_iota(jnp.int32, (CHUNK, 128), 1) < HEAD_DIM
        band, kj = _tile_masks()
        for d in DILATIONS:
            lq = SUPER // d
            if d == 1:
                qs_ref, ks_ref, vs_ref = q_ref, kk, vv
            else:
                _deinterleave(qd, q_ref, SUPER, d)
                _deinterleave(kdd, kk, 2 * SUPER, d)
                _deinterleave(vdd, vv, 2 * SUPER, d)
                qs_ref, ks_ref, vs_ref = qd, kdd, vdd

            nblk = lq // CHUNK

            def key_rows(ti):
                return pl.ds((ti // nblk) * 2 * lq + lq + (ti % nblk - 1) * CHUNK, 2 * CHUNK)

            def scores(ti):
                qt = qs_ref[pl.ds(ti * CHUNK, CHUNK), :]
                qs = jnp.concatenate([jnp.where(m0, qt, 0.0), jnp.where(m0, 0.0, qt)], axis=0).astype(BF16)
                return _dot(qs, ks_ref[key_rows(ti), :].astype(BF16), NT)

            def softmax_pv(ti, sc):
                ok = band if ti % nblk > 0 else band & (kj >= jnp.where(s > 0, 0, CHUNK))
                sc = jnp.where(ok, sc, -jnp.inf)
                mt = jnp.max(sc, axis=-1, keepdims=True)
                p = jnp.exp(sc - mt)
                lt = jnp.sum(p, axis=-1, keepdims=True)
                ot = _dot(p.astype(BF16), vs_ref[key_rows(ti), :].astype(BF16))
                qrows = pl.ds(ti * CHUNK, CHUNK)
                po[qrows, :] = jnp.where(m0, ot[:CHUNK], ot[CHUNK:])
                pm[qrows, :] = jnp.where(m0, mt[:CHUNK], mt[CHUNK:])
                pll[qrows, :] = jnp.where(m0, lt[:CHUNK], lt[CHUNK:])

            for ti in range(SUPER // CHUNK):
                softmax_pv(ti, scores(ti))
            if d == 1:
                acc[...] = po[...]
                mm[...] = pm[...]
                ll[...] = pll[...]
            else:
                for r in range(d):
                    rows = pl.ds(r, lq, stride=d)
                    seg = slice(r * lq, (r + 1) * lq)
                    m_old, m_new = mm[rows, :], pm[seg, :]
                    m_all = jnp.maximum(m_old, m_new)
                    a, b = jnp.exp(m_old - m_all), jnp.exp(m_new - m_all)
                    acc[rows, :] = acc[rows, :] * a + po[seg, :] * b
                    ll[rows, :] = ll[rows, :] * a + pll[seg, :] * b
                    mm[rows, :] = m_all
        o_ref[...] = acc[...] / ll[...]
        lse_ref[...] = mm[...] + jnp.log(ll[...])

    big = pltpu.VMEM((2 * SUPER, 128), F32)
    one = pltpu.VMEM((SUPER, 128), F32)
    return pl.pallas_call(
        body, name="attn_fwd", grid=(4, t // SUPER),
        in_specs=[cur, prev, cur, prev, cur], out_specs=[cur, cur],
        out_shape=[SDS((t, ATTN_DIM), F32)] * 2,
        scratch_shapes=[big, big, one, big, big, one, one, one, one, one, one],
        compiler_params=_params("parallel", "arbitrary"))(qn, kd, kd, vd, vd)


def _attn_bwd(qn, kd, vd, out, lse, dout, ones_pair):
    t = qn.shape[0]
    cur, prev = _attn_specs(t)

    def body(q_ref, kp_ref, kc_ref, vp_ref, vc_ref, o_ref, lse_ref, do_ref, ones_ref,
             dq_ref, dkc_ref, dkp_ref, dvc_ref, dvp_ref,
             kk, vv, od, ld, kb, vb, qsb, dosb, tk, tv, pdq, delta):
        s = pl.program_id(1)
        delta[...] = _hdot(do_ref[...] * o_ref[...], ones_ref[...])

        def per_row(a):
            ar = pltpu.roll(a, HEAD_DIM, 1)
            rows = jnp.concatenate([jnp.where(m0, a, ar), jnp.where(m0, ar, a)], axis=0)
            return jnp.concatenate([rows, rows], axis=1)

        kk[0:SUPER, :] = kp_ref[...]
        kk[SUPER:, :] = kc_ref[...]
        vv[0:SUPER, :] = vp_ref[...]
        vv[SUPER:, :] = vc_ref[...]
        for ref in (dq_ref, dkc_ref, dkp_ref, dvc_ref, dvp_ref):
            ref[...] = jnp.zeros_like(ref)
        m0 = lax.broadcasted_iota(jnp.int32, (CHUNK, 128), 1) < HEAD_DIM
        band, kj = _tile_masks()
        ninf = -jnp.inf
        for d in DILATIONS:
            lq = SUPER // d
            nblk = lq // CHUNK
            for r in range(d):
                seg = slice(r * 2 * lq, (r + 1) * 2 * lq)
                kb[seg, :] = kk[pl.ds(r, 2 * lq, stride=d), :].astype(BF16)
                vb[seg, :] = vv[pl.ds(r, 2 * lq, stride=d), :].astype(BF16)
            for ti in range(SUPER // CHUNK):
                rows = pl.ds(ti // nblk + d * CHUNK * (ti % nblk), CHUNK, stride=d)
                for src, dst in ((q_ref, qsb), (do_ref, dosb)):
                    a = src[rows, :]
                    dst[ti * 2 * CHUNK:(ti + 1) * 2 * CHUNK, :] = jnp.concatenate(
                        [jnp.where(m0, a, 0.0), jnp.where(m0, 0.0, a)], axis=0).astype(BF16)
                ld[ti * CHUNK:(ti + 1) * CHUNK, :] = lse_ref[rows, :]
                od[ti * CHUNK:(ti + 1) * CHUNK, :] = delta[rows, :]

            def operands(ti):
                r, nb = ti // nblk, ti % nblk
                stacked = slice(ti * 2 * CHUNK, (ti + 1) * 2 * CHUNK)
                krows = pl.ds(r * 2 * lq + lq + (nb - 1) * CHUNK, 2 * CHUNK)
                return stacked, krows

            def scores(ti):
                stacked, krows = operands(ti)
                kt = kb[krows, :]
                return dict(ti=ti, sc=_dot(qsb[stacked, :], kt, NT), dp=_dot(dosb[stacked, :], vb[krows, :], NT))

            def softmax_grad(c):
                qrows = slice(c["ti"] * CHUNK, (c["ti"] + 1) * CHUNK)
                ok = band if c["ti"] % nblk > 0 else band & (kj >= jnp.where(s > 0, 0, CHUNK))
                p = jnp.exp(jnp.where(ok, c.pop("sc"), ninf) - per_row(ld[qrows, :]))
                ds = p * (c.pop("dp") - per_row(od[qrows, :]))
                c.update(p=p.astype(BF16), ds=ds.astype(BF16))
                return c

            def grads(c):
                ti = c["ti"]
                stacked, krows = operands(ti)
                dqs = _dot(c["ds"], kb[krows, :])
                pdq[ti * CHUNK:(ti + 1) * CHUNK, :] = jnp.where(m0, dqs[:CHUNK], dqs[CHUNK:])
                tk[stacked, :] = _dot(c["ds"], qsb[stacked, :], TN)
                tv[stacked, :] = _dot(c["p"], dosb[stacked, :], TN)

            n_tiles = SUPER // CHUNK
            stage_a = scores(0)
            for ti in range(n_tiles):
                ahead = scores(ti + 1) if ti + 1 < n_tiles else None
                grads(softmax_grad(stage_a))
                stage_a = ahead

            for r in range(d):
                dq_ref[pl.ds(r, lq, stride=d), :] += pdq[r * lq:(r + 1) * lq, :]
                for tile_out, cur_ref, prev_ref in ((tk, dkc_ref, dkp_ref), (tv, dvc_ref, dvp_ref)):
                    first = r * nblk * 2 * CHUNK
                    prev_ref[pl.ds(SUPER - CHUNK * d + r, CHUNK, stride=d), :] += tile_out[first:first + CHUNK, :]
                    for nb in range(nblk):
                        at = (r * nblk + nb) * 2 * CHUNK
                        part = tile_out[at + CHUNK:at + 2 * CHUNK, :]
                        if nb + 1 < nblk:
                            part = part + tile_out[at + 2 * CHUNK:at + 3 * CHUNK, :]
                        cur_ref[pl.ds(r + d * nb * CHUNK, CHUNK, stride=d), :] += part

    big = pltpu.VMEM((2 * SUPER, 128), F32)
    one = pltpu.VMEM((SUPER, 128), F32)
    half = pltpu.VMEM((2 * SUPER, 128), BF16)
    return pl.pallas_call(
        body, name="attn_bwd", grid=(4, t // SUPER),
        in_specs=[cur, prev, cur, prev, cur, cur, cur, cur, pl.BlockSpec((128, 128), lambda h, s: (0, 0))],
        out_specs=[cur] * 5, out_shape=[SDS((t, ATTN_DIM), F32)] * 5,
        scratch_shapes=[big, big, one, one, half, half, half, half, big, big, one, one],
        compiler_params=_params("parallel", "arbitrary"))(qn, kd, kd, vd, vd, out, lse, dout, ones_pair)


def _ssd_consts():
    tri = np.tril(np.ones((CHUNK, CHUNK), np.float32))
    expand = np.zeros((128, SSM_INNER), np.float32)
    for h in range(SSM_HEADS):
        expand[h, h * HEAD_DIM:(h + 1) * HEAD_DIM] = 1.0
    return jnp.asarray(tri, BF16), jnp.asarray(tri.T, BF16), jnp.asarray(expand, BF16), jnp.asarray(expand.T, BF16)


def _conv4(x, halo, w_ref, b_ref):
    acc = b_ref[...] + w_ref[3:4, :] * x
    for k in range(3):
        acc = acc + w_ref[k:k + 1, :] * _shift_down(x, halo, 3 - k)
    return acc


def _softplus(x):
    return jnp.maximum(x, 0.0) + jnp.log(1.0 + jnp.exp(-jnp.abs(x)))


def _ssd_common(pre_x, pre_b, dt_ref, dtb_ref, alog_ref, tri_ref, exp_ref):
    xa = pre_x * _sigmoid(pre_x)
    ba = pre_b * _sigmoid(pre_b)
    dtv = _softplus(dt_ref[...] + dtb_ref[...])
    a_neg = -jnp.exp(alog_ref[...])
    acum = _hdot(tri_ref[...], dtv * a_neg, parts=3)
    lam = jnp.exp(acum)
    gam = jnp.exp(acum[CHUNK - 1:CHUNK, :] - acum)
    dt_e = _hdot(dtv, exp_ref[...])
    lam_e = _hdot(lam, exp_ref[...])
    gam_e = _hdot(gam, exp_ref[...])
    return dict(pre_x=pre_x, pre_b=pre_b, xa=xa, ba=ba, dtv=dtv, a_neg=a_neg, acum=acum,
                dt_e=dt_e, lam_e=lam_e, gam_e=gam_e, xdt=xa * dt_e)


def _decay(acum_t, h, transposed):
    rb = jnp.broadcast_to(acum_t[h:h + 1, :], (CHUNK, CHUNK))
    ri = lax.broadcasted_iota(jnp.int32, (CHUNK, CHUNK), 0)
    ci = lax.broadcasted_iota(jnp.int32, (CHUNK, CHUNK), 1)
    if transposed:
        return jnp.exp(jnp.where(ci >= ri, rb - rb.T, -jnp.inf))
    return jnp.exp(jnp.where(ri >= ci, rb.T - rb, -jnp.inf))


SSD_STEP = 4 * CHUNK


def _ssd_specs(t, rev):
    nc = t // SSD_STEP
    ch = (lambda c: nc - 1 - c) if rev else (lambda c: c)
    col = lambda w, j: pl.BlockSpec((SSD_STEP, w), lambda c: (ch(c), j))
    halo = lambda w, j: pl.BlockSpec((8, w), lambda c: (jnp.maximum(ch(c) * (SSD_STEP // 8) - 1, 0), j))
    return nc, ch, col, halo


def _ssd_fwd(proj, cwx, cbx, cwb, cbb, dtb, alog, dsk_e, norm_g, tri, expand):
    t = proj.shape[0]
    nc, _, col, halo = _ssd_specs(t, False)

    def body(z_all, xs_all, bc_all, dt_all, hx_ref, hb_ref, cwx_ref, cbx_ref, cwb_ref, cbb_ref, dtb_ref, alog_ref,
             dsk_ref, g_ref, tri_ref, exp_ref, y_all, hs_all, o_all, px_all, pb_all, state):
        @pl.when(pl.program_id(0) == 0)
        def _():
            state[...] = jnp.zeros_like(state)

        keep = (pl.program_id(0) > 0).astype(F32)
        for sc in range(SSD_STEP // CHUNK):
            rows = pl.ds(sc * CHUNK, CHUNK)
            before = pl.ds(sc * CHUNK - 8, 8)
            hx = hx_ref[...] * keep if sc == 0 else xs_all[before, :]
            hb = hb_ref[...] * keep if sc == 0 else bc_all[before, :]
            chunk(z_all.at[rows], xs_all.at[rows], bc_all.at[rows], dt_all.at[rows], hx, hb, cwx_ref, cbx_ref, cwb_ref,
                  cbb_ref, dtb_ref, alog_ref, dsk_ref, g_ref, tri_ref, exp_ref, y_all.at[rows],
                  hs_all.at[pl.ds(sc, 1)], o_all.at[rows], px_all.at[rows], pb_all.at[rows], state)

    def chunk(z_ref, xs_ref, bc_ref, dt_ref, hx, hb, cwx_ref, cbx_ref, cwb_ref, cbb_ref, dtb_ref, alog_ref,
              dsk_ref, g_ref, tri_ref, exp_ref, y_ref, hs_ref, o_ref, px_ref, pb_ref, state):
        pre_x = _conv4(xs_ref[...], hx, cwx_ref, cbx_ref)
        pre_b = _conv4(bc_ref[...], hb, cwb_ref, cbb_ref)
        px_ref[...] = pre_x.astype(BF16)
        pb_ref[...] = pre_b.astype(BF16)
        v = _ssd_common(pre_x, pre_b, dt_ref, dtb_ref, alog_ref, tri_ref, exp_ref)
        acum_t = v["acum"].T
        xdt, ba = v["xdt"], v["ba"]
        h_in = state[...]
        hs_ref[0] = h_in
        xg = xdt * v["gam_e"]
        m0 = lax.broadcasted_iota(jnp.int32, (CHUNK, 128), 1) < HEAD_DIM
        for g in range(2):
            bg = ba[:, g * 128:(g + 1) * 128].astype(BF16)
            cg = ba[:, 256 + g * 128:256 + (g + 1) * 128].astype(BF16)
            gl = slice(g * 512, (g + 1) * 512)
            cb = _dot(cg, bg, NT)
            y_off = _dot(cg, h_in[:, gl].astype(BF16)) * v["lam_e"][:, gl]
            s_new = _dot(bg.T, xg[:, gl].astype(BF16))
            state[:, gl] = h_in[:, gl] * v["lam_e"][CHUNK - 1:CHUNK, gl] + s_new
            for j in range(4):
                h0 = 8 * g + 2 * j
                ln = slice(g * 512 + j * 128, g * 512 + (j + 1) * 128)
                xp = xdt[:, ln].astype(BF16)
                y0 = _dot((cb * _decay(acum_t, h0, False)).astype(BF16), xp)
                y1 = _dot((cb * _decay(acum_t, h0 + 1, False)).astype(BF16), xp)
                y_ref[:, ln] = jnp.where(m0, y0, y1) + y_off[:, j * 128:(j + 1) * 128]
        z = z_ref[...]
        yg = (y_ref[...] + dsk_ref[...] * v["xa"]) * (z * _sigmoid(z))
        r = lax.rsqrt(jnp.mean(yg * yg, axis=-1, keepdims=True) + EPS)
        o_ref[...] = (yg * r * g_ref[...]).astype(BF16)

    full = lambda a: pl.BlockSpec(a.shape, lambda c: (0,) * a.ndim)
    return pl.pallas_call(
        body, name="ssd_fwd", grid=(nc,),
        in_specs=[col(1024, 1), col(1024, 2), col(512, 6), col(128, 28), halo(1024, 2), halo(512, 6),
                  full(cwx), full(cbx), full(cwb), full(cbb), full(dtb), full(alog), full(dsk_e), full(norm_g),
                  full(tri), full(expand)],
        out_specs=[pl.BlockSpec((SSD_STEP, SSM_INNER), lambda c: (c, 0)),
                   pl.BlockSpec((SSD_STEP // CHUNK, 128, SSM_INNER), lambda c: (c, 0, 0)),
                   pl.BlockSpec((SSD_STEP, SSM_INNER), lambda c: (c, 0)),
                   pl.BlockSpec((SSD_STEP, SSM_INNER), lambda c: (c, 0)), pl.BlockSpec((SSD_STEP, 512), lambda c: (c, 0))],
        out_shape=[SDS((t, SSM_INNER), F32), SDS((t // CHUNK, 128, SSM_INNER), F32), SDS((t, SSM_INNER), BF16),
                   SDS((t, SSM_INNER), BF16), SDS((t, 512), BF16)],
        scratch_shapes=[pltpu.VMEM((128, SSM_INNER), F32)],
        compiler_params=_params("arbitrary"))(proj, proj, proj, proj, proj, proj, cwx, cbx, cwb, cbb, dtb, alog,
                                              dsk_e, norm_g, tri, expand)


def _ssd_bwd(proj, pre_x, pre_b, y_ssd, hs, dout, cwx, cwb, dtb, alog, dsk_e, norm_g, tri, triu, expand, expand_t):
    t = proj.shape[0]
    nc, ch, col, halo = _ssd_specs(t, True)

    def body(z_all, xs_all, bc_all, dt_all, px_all, pb_all, y_all, hin_all, do_all,
             cwx_ref, cwb_ref, dtb_ref, alog_ref, dsk_ref, g_ref, tri_ref, triu_ref, exp_ref, expt_ref,
             dz_all, dxs_all, dbc_all, ddt_all, dg_ref, ddsk_ref, dalog_ref, ddtb_ref, dcwx_ref, dcbx_ref, dcwb_ref,
             dcbb_ref, gstate, nx_x, nx_b, dact_b, dxdt_s):
        @pl.when(pl.program_id(0) == 0)
        def _():
            gstate[...] = jnp.zeros_like(gstate)
            nx_x[...] = jnp.zeros_like(nx_x)
            nx_b[...] = jnp.zeros_like(nx_b)
            for ref in (dg_ref, ddsk_ref, dalog_ref, ddtb_ref, dcwx_ref, dcbx_ref, dcwb_ref, dcbb_ref):
                ref[...] = jnp.zeros_like(ref)

        for sc in reversed(range(SSD_STEP // CHUNK)):
            rows = pl.ds(sc * CHUNK, CHUNK)
            by_rows = [r.at[rows] for r in (z_all, xs_all, bc_all, dt_all, px_all, pb_all, y_all)]
            outs = [r.at[rows] for r in (dz_all, dxs_all, dbc_all, ddt_all)]
            chunk(*by_rows, hin_all.at[pl.ds(sc, 1)], do_all.at[rows],
                  cwx_ref, cwb_ref, dtb_ref, alog_ref, dsk_ref, g_ref, tri_ref, triu_ref, exp_ref, expt_ref,
                  *outs, dg_ref, ddsk_ref, dalog_ref, ddtb_ref, dcwx_ref, dcbx_ref, dcwb_ref, dcbb_ref,
                  gstate, nx_x, nx_b, dact_b, dxdt_s)

    def chunk(z_ref, xs_ref, bc_ref, dt_ref, px_ref, pb_ref, y_ref, hin_ref, do_ref,
              cwx_ref, cwb_ref, dtb_ref, alog_ref, dsk_ref, g_ref, tri_ref, triu_ref, exp_ref, expt_ref,
              dz_ref, dxs_ref, dbc_ref, ddt_ref, dg_ref, ddsk_ref, dalog_ref, ddtb_ref, dcwx_ref, dcbx_ref, dcwb_ref,
              dcbb_ref, gstate, nx_x, nx_b, dact_b, dxdt_s):
        v = _ssd_common(px_ref[...].astype(F32), pb_ref[...].astype(F32), dt_ref, dtb_ref, alog_ref, tri_ref, exp_ref)
        acum_t = v["acum"].T
        xa, ba, xdt, dtv = v["xa"], v["ba"], v["xdt"], v["dtv"]
        lam_e, gam_e, dt_e = v["lam_e"], v["gam_e"], v["dt_e"]
        z = z_ref[...]
        y = y_ref[...]
        sz = _sigmoid(z)
        zs = z * sz
        y_tot = y + dsk_ref[...] * xa
        yg = y_tot * zs
        r = lax.rsqrt(jnp.mean(yg * yg, axis=-1, keepdims=True) + EPS)
        yh = yg * r
        do = do_ref[...]
        dg_ref[...] += jnp.sum(do * yh, axis=0, keepdims=True)
        gd = do * g_ref[...]
        dyg = r * (gd - yh * jnp.mean(gd * yh, axis=-1, keepdims=True))
        dz_ref[...] = (dyg * y_tot * (sz * (1.0 + z * (1.0 - sz)))).astype(BF16)
        dy = dyg * zs
        ddsk_ref[...] += jnp.sum(dy * xa, axis=0, keepdims=True)
        g_out = gstate[...]
        h_in = hin_ref[0]
        lam_dy = lam_e * dy
        gam_x = gam_e * xdt
        m0 = lax.broadcasted_iota(jnp.int32, (CHUNK, 128), 1) < HEAD_DIM
        lane = lax.broadcasted_iota(jnp.int32, (CHUNK, 128), 1)
        below = (lax.broadcasted_iota(jnp.int32, (CHUNK, CHUNK), 0) >
                 lax.broadcasted_iota(jnp.int32, (CHUNK, CHUNK), 1))
        da_in = jnp.zeros((CHUNK, 128), F32)
        off_y, off_x = [], []
        for g in range(2):
            bg = ba[:, g * 128:(g + 1) * 128].astype(BF16)
            cg = ba[:, 256 + g * 128:256 + (g + 1) * 128].astype(BF16)
            gl = slice(g * 512, (g + 1) * 512)
            gg = g_out[:, gl].astype(BF16)
            bc_t = _dot(bg, cg, NT)
            cb = _dot(cg, bg, NT)
            dxdt_off = _dot(bg, gg) * gam_e[:, gl]
            off_x.append(xdt[:, gl] * dxdt_off)
            off_y.append(dy[:, gl] * (_dot(cg, h_in[:, gl].astype(BF16)) * lam_e[:, gl]))
            q_sum = jnp.zeros((CHUNK, CHUNK), F32)
            for j in range(4):
                h0 = 8 * g + 2 * j
                ln = slice(g * 512 + j * 128, g * 512 + (j + 1) * 128)
                dyp = dy[:, ln]
                dyb = dyp.astype(BF16)
                xpb = xdt[:, ln].astype(BF16)
                d0 = _dot((bc_t * _decay(acum_t, h0, True)).astype(BF16), dyb)
                d1 = _dot((bc_t * _decay(acum_t, h0 + 1, True)).astype(BF16), dyb)
                dxdt_s[:, ln] = jnp.where(m0, d0, d1) + dxdt_off[:, j * 128:(j + 1) * 128]
                for hh, dym in ((h0, jnp.where(m0, dyp, 0.0)), (h0 + 1, jnp.where(m0, 0.0, dyp))):
                    qd = _dot(dym.astype(BF16), xpb, NT) * _decay(acum_t, hh, False)
                    q_sum = q_sum + qd
                    reach = jnp.where(below, _hdot(triu_ref[...], qd * cb), 0.0)
                    da_in = jnp.where(lane == hh, jnp.sum(reach, axis=-1, keepdims=True), da_in)
            gstate[:, gl] = g_out[:, gl] * lam_e[CHUNK - 1:CHUNK, gl] + _dot(cg.T, lam_dy[:, gl].astype(BF16))
            qb = q_sum.astype(BF16)
            dact_b[:, 256 + g * 128:256 + (g + 1) * 128] = (
                _dot(qb, bg) + _dot(lam_dy[:, gl].astype(BF16), h_in[:, gl].astype(BF16), NT))
            dact_b[:, g * 128:(g + 1) * 128] = _dot(qb.T, cg) + _dot(gam_x[:, gl].astype(BF16), gg, NT)
        dxdt = dxdt_s[...]
        seg_y = _hdot(jnp.concatenate(off_y, axis=1), expt_ref[...])
        seg_x = _hdot(jnp.concatenate(off_x, axis=1), expt_ref[...])
        e_col = jnp.sum(g_out * h_in * lam_e[CHUNK - 1:CHUNK, :], axis=0, keepdims=True)
        e_seg = _hdot(jnp.broadcast_to(e_col, (8, SSM_INNER)), expt_ref[...])[0:1, :]
        da = da_in + _hdot(triu_ref[...], seg_y) + (_hdot(tri_ref[...], seg_x) - seg_x) + e_seg
        a_neg = v["a_neg"]
        ddtv = da * a_neg + _hdot(dxdt * xa, expt_ref[...])
        dalog_ref[...] += jnp.sum(da * dtv, axis=0, keepdims=True) * a_neg
        lane16 = lax.broadcasted_iota(jnp.int32, (CHUNK, 128), 1) < SSM_HEADS
        draw = jnp.where(lane16, ddtv * _sigmoid(dt_ref[...] + dtb_ref[...]), 0.0)
        ddtb_ref[...] += jnp.sum(draw, axis=0, keepdims=True)
        ddt_ref[...] = draw.astype(BF16)
        dxa = dxdt * dt_e + dy * dsk_ref[...]
        for (dact, pre, x_ref, nx, cw_ref, dcw_ref, dcb_ref, dx_ref) in (
                (dxa, v["pre_x"], xs_ref, nx_x, cwx_ref, dcwx_ref, dcbx_ref, dxs_ref),
                (dact_b[...], v["pre_b"], bc_ref, nx_b, cwb_ref, dcwb_ref, dcbb_ref, dbc_ref)):
            sp = _sigmoid(pre)
            dpre = dact * (sp * (1.0 + pre * (1.0 - sp)))
            dcb_ref[...] += jnp.sum(dpre, axis=0, keepdims=True)
            xv = x_ref[...]
            nxt = nx[...]
            dx = cw_ref[3:4, :] * dpre
            dcw_ref[3:4, :] += jnp.sum(dpre * xv, axis=0, keepdims=True)
            for k in range(3):
                d_up = _shift_up(dpre, nxt, 3 - k)
                dcw_ref[k:k + 1, :] += jnp.sum(xv * d_up, axis=0, keepdims=True)
                dx = dx + cw_ref[k:k + 1, :] * d_up
            nx[...] = dpre[0:8, :]
            dx_ref[...] = dx.astype(dx_ref.dtype)

    full = lambda a: pl.BlockSpec(a.shape, lambda c: (0,) * a.ndim)
    rowblk = lambda w: pl.BlockSpec((SSD_STEP, w), lambda c: (ch(c), 0))
    acc = lambda a, b: pl.BlockSpec((a, b), lambda c: (0, 0))
    return pl.pallas_call(
        body, name="ssd_bwd", grid=(nc,),
        in_specs=[col(1024, 1), col(1024, 2), col(512, 6), col(128, 28), rowblk(SSM_INNER), rowblk(512),
                  rowblk(SSM_INNER),
                  pl.BlockSpec((SSD_STEP // CHUNK, 128, SSM_INNER), lambda c: (ch(c), 0, 0)),
                  rowblk(SSM_INNER),
                  full(cwx), full(cwb), full(dtb), full(alog), full(dsk_e), full(norm_g),
                  full(tri), full(triu), full(expand), full(expand_t)],
        out_specs=[rowblk(SSM_INNER), rowblk(SSM_INNER), rowblk(512), rowblk(128),
                   acc(1, 1024), acc(1, 1024), acc(1, 128), acc(1, 128), acc(4, 1024), acc(1, 1024), acc(4, 512),
                   acc(1, 512)],
        out_shape=[SDS((t, SSM_INNER), BF16), SDS((t, SSM_INNER), BF16), SDS((t, 512), BF16), SDS((t, 128), BF16),
                   SDS((1, 1024), F32), SDS((1, 1024), F32), SDS((1, 128), F32), SDS((1, 128), F32),
                   SDS((4, 1024), F32), SDS((1, 1024), F32), SDS((4, 512), F32), SDS((1, 512), F32)],
        scratch_shapes=[pltpu.VMEM((128, SSM_INNER), F32), pltpu.VMEM((8, 1024), F32), pltpu.VMEM((8, 512), F32),
                        pltpu.VMEM((CHUNK, 512), F32), pltpu.VMEM((CHUNK, SSM_INNER), F32)],
        compiler_params=_params("arbitrary"))(proj, proj, proj, proj, pre_x, pre_b, y_ssd, hs, dout,
                                              cwx, cwb, dtb, alog, dsk_e, norm_g, tri, triu, expand, expand_t)


def _conv3(x, halo, w_ref, b_ref, part):
    acc = b_ref[part] + w_ref[2, part] * x
    for k in range(2):
        acc = acc + w_ref[k, part] * _shift_down(x, halo, 2 - k)
    return acc


def _up_act(x, g, w_up_t, cw, cb, tm=2048, tn=256, tr=512):
    t, k = x.shape
    nj = D_FF // tn

    def body(x_ref, g_ref, wg_ref, wv_ref, w_ref, b_ref, u_ref, c_ref, h_ref, f_ref, halo):
        i, j = pl.program_id(0), pl.program_id(1)

        @pl.when(j == 0)
        def _():
            xv = x_ref[...]
            r = lax.rsqrt(jnp.mean(xv * xv, axis=-1, keepdims=True) + EPS)
            h_ref[...] = (xv * r * g_ref[...]).astype(BF16)

        @pl.when(i == 0)
        def _():
            halo[j] = jnp.zeros((2, 8, tn), F32)

        def matmuls(r):
            rows = slice(r * tr, (r + 1) * tr)
            return [_dot(h_ref[rows, :], wt_ref[...], NT) for wt_ref in (wg_ref, wv_ref)]

        def epilogue(r, us, before):
            rows = slice(r * tr, (r + 1) * tr)
            parts = []
            for part, u in enumerate(us):
                u_ref[part, rows, :] = u.astype(BF16)
                parts.append(_conv3(u, before[part], w_ref, b_ref, part))
                c_ref[part, rows, :] = parts[-1].astype(BF16)
            gate, val = parts
            f_ref[rows, :] = (gate * _sigmoid(gate) * val).astype(BF16)
            return [u[tr - 8:, :] for u in us]

        before = [halo[j, 0], halo[j, 1]]
        pending = matmuls(0)
        for r in range(tm // tr):
            ahead = matmuls(r + 1) if r + 1 < tm // tr else None
            before = epilogue(r, pending, before)
            pending = ahead
        halo[j, 0], halo[j, 1] = before

    return pl.pallas_call(
        body, name="up_proj", grid=(t // tm, nj),
        in_specs=[pl.BlockSpec((tm, k), lambda i, j: (i, 0)), pl.BlockSpec((1, k), lambda i, j: (0, 0)),
                  pl.BlockSpec((tn, k), lambda i, j: (j, 0)), pl.BlockSpec((tn, k), lambda i, j: (j + nj, 0)),
                  pl.BlockSpec((3, 2, 1, tn), lambda i, j: (0, 0, 0, j)), pl.BlockSpec((2, 1, tn), lambda i, j: (0, 0, j))],
        out_specs=[pl.BlockSpec((2, tm, tn), lambda i, j: (0, i, j)), pl.BlockSpec((2, tm, tn), lambda i, j: (0, i, j)),
                   pl.BlockSpec((tm, k), lambda i, j: (i, 0)), pl.BlockSpec((tm, tn), lambda i, j: (i, j))],
        out_shape=[SDS((2, t, D_FF), BF16), SDS((2, t, D_FF), BF16), SDS((t, k), BF16), SDS((t, D_FF), BF16)],
        scratch_shapes=[pltpu.VMEM((nj, 2, 8, tn), F32)],
        compiler_params=_params("arbitrary", "arbitrary"))(x, g, w_up_t, w_up_t, cw, cb)


def _ffn_bwd(dx2, w_down, u, c, cw, tm=512, tn=1408):
    t = u.shape[1]
    nj, ni = D_FF // tn, t // tm
    rev = lambda i: ni - 1 - i

    def body(dx_ref, wd_ref, u_ref, c_ref, w_ref, du_ref, dcw_ref, dcb_ref, nxt):
        i = pl.program_id(1)

        @pl.when(i == 0)
        def _():
            nxt[...] = jnp.zeros_like(nxt)
            dcw_ref[...] = jnp.zeros_like(dcw_ref)
            dcb_ref[...] = jnp.zeros_like(dcb_ref)

        df = _dot(dx_ref[...].astype(BF16), wd_ref[...], NT)
        gate, val = c_ref[0].astype(F32), c_ref[1].astype(F32)
        sg = _sigmoid(gate)
        dgate = df * val * (sg * (1.0 + gate * (1.0 - sg)))
        dval = df * (gate * sg)
        for part, d in enumerate((dgate, dval)):
            uu = u_ref[part].astype(F32)
            dcb_ref[part] += jnp.sum(d, axis=0, keepdims=True)
            ahead = nxt[part]
            acc = w_ref[2, part] * d
            dcw_ref[2, part] += jnp.sum(d * uu, axis=0, keepdims=True)
            for k in range(2):
                d_up = _shift_up(d, ahead, 2 - k)
                dcw_ref[k, part] += jnp.sum(uu * d_up, axis=0, keepdims=True)
                acc = acc + w_ref[k, part] * d_up
            nxt[part] = d[0:8, :]
            du_ref[part] = acc.astype(BF16)

    w_spec = pl.BlockSpec((3, 2, 1, tn), lambda j, i: (0, 0, 0, j))
    b_spec = pl.BlockSpec((2, 1, tn), lambda j, i: (0, 0, j))
    tile = pl.BlockSpec((2, tm, tn), lambda j, i: (0, rev(i), j))
    return pl.pallas_call(
        body, name="ffn_bwd", grid=(nj, ni),
        in_specs=[pl.BlockSpec((tm, D_MODEL), lambda j, i: (rev(i), 0)), pl.BlockSpec((tn, D_MODEL), lambda j, i: (j, 0)),
                  tile, tile, w_spec],
        out_specs=[tile, w_spec, b_spec],
        out_shape=[SDS((2, t, D_FF), BF16), SDS((3, 2, 1, D_FF), F32), SDS((2, 1, D_FF), F32)],
        scratch_shapes=[pltpu.VMEM((2, 8, tn), F32)],
        compiler_params=_params("parallel", "arbitrary"))(dx2, w_down, u, c, cw)


def _ple_loss(x2, g, w_gate, p, w_proj_t, target, tm=512):
    t = x2.shape[0]

    def body(x_ref, g_ref, wg_ref, p_ref, wp_ref, tg_ref, dx_ref, dpre_ref, dpp_ref, h_ref, loss_ref, dg_ref):
        i = pl.program_id(0)
        xv = x_ref[...]
        r = lax.rsqrt(jnp.mean(xv * xv, axis=-1, keepdims=True) + EPS)
        xh = xv * r
        h = (xh * g_ref[...]).astype(BF16)
        h_ref[...] = h
        gate = _sigmoid(_dot(h, wg_ref[...]))
        pp = _dot(p_ref[...].astype(BF16), wp_ref[...], NT)
        err = (xv + gate * pp) - tg_ref[...]

        @pl.when(i == 0)
        def _():
            loss_ref[...] = jnp.zeros_like(loss_ref)
            dg_ref[...] = jnp.zeros_like(dg_ref)

        loss_ref[...] += 0.5 * jnp.sum(jnp.mean(err * err, axis=-1, keepdims=True), axis=0, keepdims=True)
        dy = err * (1.0 / D_MODEL)
        dpre = (dy * pp * gate * (1.0 - gate)).astype(BF16)
        dpre_ref[...] = dpre
        dpp_ref[...] = (dy * gate).astype(BF16)
        dh = _dot(dpre, wg_ref[...], NT)
        dg_ref[...] += jnp.sum(dh * xh, axis=0, keepdims=True)
        gd = dh * g_ref[...]
        dx_ref[...] = dy + r * (gd - xh * jnp.mean(gd * xh, axis=-1, keepdims=True))

    row = lambda w: pl.BlockSpec((tm, w), lambda i: (i, 0))
    full = lambda a: pl.BlockSpec(a.shape, lambda i: (0, 0))
    return pl.pallas_call(
        body, name="ple_loss", grid=(t // tm,),
        in_specs=[row(D_MODEL), full(g), full(w_gate), row(PLE_DIM), full(w_proj_t), row(D_MODEL)],
        out_specs=[row(D_MODEL), row(D_MODEL), row(D_MODEL), row(D_MODEL),
                   pl.BlockSpec((1, 128), lambda i: (0, 0)), pl.BlockSpec((1, D_MODEL), lambda i: (0, 0))],
        out_shape=[SDS((t, D_MODEL), F32), SDS((t, D_MODEL), BF16), SDS((t, D_MODEL), BF16), SDS((t, D_MODEL), BF16),
                   SDS((1, 128), F32), SDS((1, D_MODEL), F32)],
        compiler_params=_params("arbitrary"))(x2, g, w_gate, p, w_proj_t, target)


def _all_gather(arrays, name):
    n_a = len(arrays)

    def body(*refs):
        src, dst = refs[:n_a], refs[n_a:2 * n_a]
        send_sems, recv_sems, local_sems = refs[2 * n_a:]
        x, y, c = lax.axis_index("x"), lax.axis_index("y"), lax.axis_index("c")
        slot = lambda px, py, pc: 4 * px + 2 * py + pc
        me, sibling = (x, y, c), (x, y, 1 - c)
        chips = [(1 - x, y), (x, 1 - y), (1 - x, 1 - y)]

        def copy(a, k, block, to, own=False):
            return pltpu.make_async_remote_copy(
                src_ref=src[a] if own else dst[a].at[slot(*block)], dst_ref=dst[a].at[slot(*block)],
                send_sem=send_sems.at[a, k], recv_sem=recv_sems.at[a, k], device_id=to,
                device_id_type=pl.DeviceIdType.MESH)

        local = [pltpu.make_async_copy(src[a], dst[a].at[slot(*me)], local_sems.at[a]) for a in range(n_a)]
        for cp in local:
            cp.start()
        sends = []
        for a in range(n_a):
            sends.append(copy(a, 0, me, sibling, own=True))
            sends += [copy(a, 1 + j, me, (*chip, c), own=True) for j, chip in enumerate(chips)]
        for cp in sends:
            cp.start()
        for j, chip in enumerate(chips):
            for a in range(n_a):
                copy(a, 1 + j, (*chip, c), me).wait_recv()
                passed = copy(a, 4 + j, (*chip, c), sibling)
                passed.start()
                sends.append(passed)
        for a in range(n_a):
            copy(a, 0, sibling, me).wait_recv()
            for j, chip in enumerate(chips):
                copy(a, 4 + j, (*chip, 1 - c), me).wait_recv()
        for cp in sends:
            cp.wait_send()
        for cp in local:
            cp.wait()

    hbm = pl.BlockSpec(memory_space=pl.ANY)
    return pl.pallas_call(
        body, name=name, in_specs=[hbm] * n_a, out_specs=[hbm] * n_a,
        out_shape=[SDS((N_DEV,) + a.shape, a.dtype) for a in arrays],
        scratch_shapes=[pltpu.SemaphoreType.DMA((n_a, N_DEV - 1)), pltpu.SemaphoreType.DMA((n_a, N_DEV - 1)),
                        pltpu.SemaphoreType.DMA((n_a,))],
        )(*arrays)


def _peer(k):
    x, y, c = lax.axis_index("x"), lax.axis_index("y"), lax.axis_index("c")
    px = 1 - x if k & 4 else x
    py = 1 - y if k & 2 else y
    pc = 1 - c if k & 1 else c
    return (px, py, pc), 4 * px + 2 * py + pc


_HBM = pl.BlockSpec(memory_space=pltpu.HBM)
_SEM = pl.BlockSpec(memory_space=pltpu.SEMAPHORE)


def _split_copies(src, land, send_sems, recv_sems, scatter, arrivals):
    _, me = _peer(0)
    out = []
    for k in range(1, N_DEV):
        coords, peer = _peer(k)
        for a in range(len(src)):
            sem = a * (N_DEV - 1) + k - 1
            if scatter[a]:
                s, d = src[a].at[peer], land[a].at[k]
            else:
                s, d = src[a], land[a].at[peer if arrivals else me]
            out.append(pltpu.make_async_remote_copy(
                src_ref=s, dst_ref=d, send_sem=send_sems.at[sem], recv_sem=recv_sems.at[sem], device_id=coords,
                device_id_type=pl.DeviceIdType.MESH))
    return out


def _exchange_start(srcs, lands, scatter, name):
    n = len(srcs)

    def body(*refs):
        src, land = refs[:n], refs[n:2 * n]
        send_sems, recv_sems = refs[2 * n], refs[2 * n + 1]
        token = refs[-1]
        for cp in _split_copies(src, land, send_sems, recv_sems, scatter, False):
            cp.start()
        token[...] = jnp.zeros_like(token)

    hbm_shape = lambda a: pltpu.HBM(a.shape, a.dtype)
    sem_shape = pltpu.SemaphoreType.DMA((n * (N_DEV - 1),))
    outs = pl.pallas_call(
        body, name=name,
        out_shape=(sem_shape, sem_shape, *[hbm_shape(a) for a in srcs], *[hbm_shape(a) for a in lands],
                   SDS((8, 128), F32)),
        in_specs=[_HBM] * (2 * n), out_specs=(_SEM, _SEM, *[_HBM] * (2 * n), pl.BlockSpec(memory_space=pltpu.VMEM)),
        input_output_aliases={a: 2 + a for a in range(2 * n)},
        compiler_params=pltpu.CompilerParams(has_side_effects=pltpu.SideEffectType.DATAFLOW_SIDE_EFFECTING),
    )(*[pltpu.with_memory_space_constraint(a, pltpu.HBM) for a in list(srcs) + list(lands)])
    return outs[0], outs[1], outs[2:2 + n], outs[2 + n:2 + 2 * n], outs[-1]


def _exchange_wait(send_sems, recv_sems, srcs, lands, scatter, after, name):
    n = len(srcs)

    def body(*refs):
        src, land = refs[:n], refs[n:2 * n]
        for cp in _split_copies(src, land, refs[2 * n], refs[2 * n + 1], scatter, False):
            cp.wait_send()
        for cp in _split_copies(src, land, refs[2 * n], refs[2 * n + 1], scatter, True):
            cp.wait_recv()

    hbm_shape = lambda a: pltpu.HBM(a.shape, a.dtype)
    outs = pl.pallas_call(
        body, name=name, out_shape=tuple(hbm_shape(a) for a in list(srcs) + list(lands)),
        in_specs=[_HBM] * (2 * n) + [_SEM, _SEM, pl.BlockSpec(memory_space=pl.ANY)], out_specs=(_HBM,) * (2 * n),
        input_output_aliases={a: a for a in range(2 * n)},
        compiler_params=pltpu.CompilerParams(has_side_effects=pltpu.SideEffectType.DATAFLOW_SIDE_EFFECTING),
    )(*srcs, *lands, send_sems, recv_sems, after)
    return outs[:n], outs[n:]


def _reduce8(a, tr, name):
    _, rows, cols = a.shape

    def body(a_ref, o_ref):
        acc = a_ref[0]
        for j in range(1, N_DEV):
            acc = acc + a_ref[j]
        o_ref[...] = acc

    return pl.pallas_call(
        body, name=name, grid=(rows // tr,),
        in_specs=[pl.BlockSpec((N_DEV, tr, cols), lambda i: (0, i, 0))],
        out_specs=pl.BlockSpec((tr, cols), lambda i: (i, 0)), out_shape=SDS((rows, cols), F32),
        compiler_params=_params("parallel"))(a)


def _reduce_landed(own, land, name, tc=256):
    rows, cols = own.shape

    def body(own_ref, land_ref, o_ref):
        acc = own_ref[...].astype(F32)
        for k in range(1, N_DEV):
            acc = acc + land_ref[k].astype(F32)
        o_ref[...] = acc

    return pl.pallas_call(
        body, name=name, grid=(cols // tc,),
        in_specs=[pl.BlockSpec((rows, tc), lambda j: (0, j)), pl.BlockSpec((N_DEV, rows, tc), lambda j: (0, 0, j))],
        out_specs=pl.BlockSpec((rows, tc), lambda j: (0, j)), out_shape=SDS((rows, cols), F32),
        compiler_params=_params("parallel"))(own, land)


def _adamw(w, g, m, v, name, tr=None):
    rows, cols = w.shape
    tr = rows if tr is None else tr

    def body(w_ref, g_ref, m_ref, v_ref, d_ref, mo_ref, vo_ref):
        d_ref[...], mo_ref[...], vo_ref[...] = _adam_update(w_ref[...], g_ref[...], m_ref[...], v_ref[...])

    blk = pl.BlockSpec((tr, cols), lambda i: (i, 0))
    return pl.pallas_call(
        body, name=name, grid=(rows // tr,), in_specs=[blk] * 4, out_specs=[blk] * 3,
        out_shape=[SDS((rows, cols), F32)] * 3, compiler_params=_params("parallel"))(w, g, m, v)


def _pad_rows(a, rows):
    return jnp.pad(a, ((0, rows - a.shape[0]),) + ((0, 0),) * (a.ndim - 1))


def _local_step(x, p, target, sm, wts, fetch_rest, send, tok):
    ones_q, ones_k, dup, dup_t = _head_consts()
    tri, triu, expand, expand_t = _ssd_consts()
    w_in_t = wts["in_t"]
    cwx, cwb = wts["ssm_cw"][:, :SSM_INNER], wts["ssm_cw"][:, SSM_INNER:]
    cbx, cbb = sm["ssm_conv_b"][:, :SSM_INNER], sm["ssm_conv_b"][:, SSM_INNER:]
    pad128 = lambda a: jnp.pad(a, ((0, 0), (0, 128 - a.shape[1])))
    dtb, alog = pad128(sm["dt_bias"]), pad128(sm["a_log"])
    dsk_e = jnp.repeat(sm["d_skip"], HEAD_DIM, axis=1)
    gq = jnp.tile(sm["q_norm_g"], (1, ATTN_DIM // HEAD_DIM))
    gk = jnp.tile(sm["k_norm_g"], (1, KV_DIM // HEAD_DIM))
    ffn_cw = wts["ffn_cw"].reshape(3, 2, 1, D_FF)
    ffn_cb = sm["ffn_conv_b"].reshape(2, 1, D_FF)

    proj, h1 = _norm_matmul(x, sm["attn_norm_g"] + tok, w_in_t, 1024, 768, "in_proj")
    qn, kd, vd = _attn_prep(proj, gq, gk, ones_q, ones_k, dup)
    attn_out, lse = _attn_fwd(qn, kd, vd)
    y_ssd, hs, ssm_out, pre_x, pre_b = _ssd_fwd(proj, cwx, cbx, cwb, cbb, dtb, alog, dsk_e, sm["ssm_norm_g"], tri,
                                                expand)
    rest = fetch_rest(ssm_out)
    w_out, w_up_t, w_down, w_gate, w_proj_t = (rest[k] for k in ("out", "up_t", "down", "gate", "proj_t"))
    x1 = _mm_resid([(attn_out, None, w_out[:ATTN_DIM]), (ssm_out, None, w_out[ATTN_DIM:])], x, None, 512, F32,
                   "out_proj")
    u, uc, h2, f = _up_act(x1, sm["ffn_norm_g"], w_up_t, ffn_cw, ffn_cb)
    x2 =_mm_resid([(f, None, w_down)], x1, None, 512, F32, "down_proj")
    dx2, dpre, dpp, h3, loss, dg_ple = _ple_loss(x2, sm["ple_norm_g"], w_gate, p, w_proj_t, target)

    g_gate = _wgrad(h3, None, dpre, "wg_gate")
    g_proj_t = _wgrad(dpp, None, p, "wg_proj")
    g_down = _wgrad(f, None, dx2, "wg_down")
    du, d_ffn_cw, d_ffn_cb = _ffn_bwd(dx2, w_down, u, uc, ffn_cw)
    dx1, dg_ffn = _mm_normbwd([(du, 0, w_up_t, D_FF, 0), (du, 1, w_up_t, D_FF, 1)], x1, sm["ffn_norm_g"], dx2, 512,
                              "up_proj_bwd")
    g_up_t = _wgrad(du, "all", h2, "wg_up")
    tok = send(dict(gate=g_gate, proj_t=g_proj_t, down=g_down, up_t=g_up_t)).astype(BF16)
    d_attn, d_ssm = _out_proj_bwd(dx1, w_out + tok)
    g_out = _wgrad_multi([attn_out, ssm_out], dx1, "wg_out")
    tok = send(dict(out=g_out))
    (dz, dxs, dbc, ddt, dg_ssm, d_dsk_e, d_alog, d_dtb, d_cwx, d_cbx, d_cwb, d_cbb) = _ssd_bwd(
        proj, pre_x, pre_b, y_ssd, hs, d_ssm, cwx, cwb, dtb + tok, alog, dsk_e, sm["ssm_norm_g"], tri, triu, expand,
        expand_t)
    dqn, dkc, dkp, dvc, dvp = _attn_bwd(qn, kd, vd, attn_out, lse, d_attn, ones_k[:128, :128])
    dqkv, dgq, dgk = _attn_prep_bwd(proj, dqn, dkc, dkp, dvc, dvp, gq + tok, gk, ones_q, ones_k, dup_t)
    pieces = [(dqkv, 0, 1024), (dz, 1024, 2048), (dxs, 2048, 3072), (dbc, 3072, 3584), (ddt, 3584, 3712)]
    g_in_t = jnp.concatenate([_wgrad_multi([dqkv, dz], h1, "wg_in_qkvz"),
                              _wgrad_multi([dxs, dbc, ddt], h1, "wg_in_xbcdt")], axis=0)[:IN_PROJ]
    tok = send(dict(in_t=g_in_t))
    grad_x, dg_attn = _mm_normbwd([(a, None, w_in_t, hi - lo, lo // (hi - lo)) for a, lo, hi in pieces], x,
                                  sm["attn_norm_g"] + tok, dx1, 512, "in_proj_bwd")

    small = dict(
        attn_norm_g=dg_attn, q_norm_g=dgq.reshape(-1, HEAD_DIM).sum(0, keepdims=True),
        k_norm_g=dgk.reshape(-1, HEAD_DIM).sum(0, keepdims=True),
        ssm_conv_w=jnp.concatenate([d_cwx, d_cwb], axis=1), ssm_conv_b=jnp.concatenate([d_cbx, d_cbb], axis=1),
        dt_bias=d_dtb[:, :SSM_HEADS], a_log=d_alog[:, :SSM_HEADS],
        d_skip=d_dsk_e.reshape(SSM_HEADS, HEAD_DIM).sum(1)[None, :], ssm_norm_g=dg_ssm, ffn_norm_g=dg_ffn,
        ffn_conv_w=d_ffn_cw.reshape(3, 2 * D_FF), ffn_conv_b=d_ffn_cb.reshape(1, 2 * D_FF), ple_norm_g=dg_ple)
    return loss[0, 0], grad_x, small


_SMALL = (("attn_norm_g", 1, 1024), ("q_norm_g", 1, 64), ("k_norm_g", 1, 64), ("ssm_conv_w", 4, XBC_DIM),
          ("ssm_conv_b", 1, XBC_DIM), ("dt_bias", 1, 16), ("a_log", 1, 16), ("d_skip", 1, 16), ("ssm_norm_g", 1, 1024),
          ("ffn_norm_g", 1, 1024), ("ffn_conv_w", 3, 2 * D_FF), ("ffn_conv_b", 1, 2 * D_FF), ("ple_norm_g", 1, 1024))
_SMALL_ROWS, _SMALL_COLS = 32, XBC_DIM
_SHARDED_SMALL = ("ssm_conv_w", "ffn_conv_w")


def _small_chunks(n):
    return 1 if n <= _SMALL_COLS else 4


def _pack_small(parts, loss):
    rows = []
    for k, r, n in _SMALL:
        c = _small_chunks(n)
        rows.append(jnp.pad(parts[k].reshape(r * c, n // c), ((0, 0), (0, _SMALL_COLS - n // c))))
    packed = _pad_rows(jnp.concatenate(rows, axis=0), _SMALL_ROWS)
    at_loss = ((lax.broadcasted_iota(jnp.int32, packed.shape, 0) == _SMALL_ROWS - 1) &
               (lax.broadcasted_iota(jnp.int32, packed.shape, 1) == 0))
    return jnp.where(at_loss, loss, packed)


def _adam_update(w, g, m, v):
    c1 = 1.0 - ADAM_B1 ** ADAM_STEP
    c2 = 1.0 - ADAM_B2 ** ADAM_STEP
    mn = ADAM_B1 * m + (1.0 - ADAM_B1) * g
    vn = ADAM_B2 * v + (1.0 - ADAM_B2) * (g * g)
    return -ADAM_LR * ((mn / c1) / (jnp.sqrt(vn / c2) + ADAM_EPS) + ADAM_WD * w), mn, vn


def _adamw_small(g_all, g_shard, w, m, v):
    ins, shapes = [g_all], []
    for k, _, _ in _SMALL:
        shape2 = w[k].shape if w[k].ndim == 2 else (1, w[k].shape[0])
        shapes.append(shape2)
        ins += ([g_shard[k]] if k in _SHARDED_SMALL else []) + [a.reshape(shape2) for a in (w[k], m[k], v[k])]

    def body(*refs):
        g_ref, pos, row = refs[0], 1, 0
        outs = refs[len(ins):]
        for i, (k, r, n) in enumerate(_SMALL):
            c = _small_chunks(n)
            if k in _SHARDED_SMALL:
                g = refs[pos][...]
                pos += 1
            elif c == 1:
                g = g_ref[row:row + r, 0:n]
            else:
                g = jnp.concatenate([g_ref[row + j:row + j + 1, 0:n // c] for j in range(c)], axis=1)
            row += r * c
            d, mn, vn = _adam_update(refs[pos][...], g, refs[pos + 1][...], refs[pos + 2][...])
            pos += 3
            for o_ref, val in zip(outs[4 * i:4 * i + 4], (g, d, mn, vn)):
                o_ref[...] = val

    res = pl.pallas_call(body, name="adamw_small",
                         out_shape=[SDS(s, F32) for s in shapes for _ in range(4)])(*ins)
    return {k: tuple(a.reshape(w[k].shape) for a in res[4 * i:4 * i + 4]) for i, (k, _, _) in enumerate(_SMALL)}


def kernel(x, p, attn_norm_g, w_in, q_norm_g, k_norm_g, ssm_conv_w, ssm_conv_b, dt_bias, a_log, d_skip, ssm_norm_g, w_out, ffn_norm_g, w_up, ffn_conv_w, ffn_conv_b, w_down, ple_norm_g, w_ple_gate, w_ple_proj, loss_target, m_attn_norm_g, m_w_in, m_q_norm_g, m_k_norm_g, m_ssm_conv_w, m_ssm_conv_b, m_dt_bias, m_a_log, m_d_skip, m_ssm_norm_g, m_w_out, m_ffn_norm_g, m_w_up, m_ffn_conv_w, m_ffn_conv_b, m_w_down, m_ple_norm_g, m_w_ple_gate, m_w_ple_proj, v_attn_norm_g, v_w_in, v_q_norm_g, v_k_norm_g, v_ssm_conv_w, v_ssm_conv_b, v_dt_bias, v_a_log, v_d_skip, v_ssm_norm_g, v_w_out, v_ffn_norm_g, v_w_up, v_ffn_conv_w, v_ffn_conv_b, v_w_down, v_ple_norm_g, v_w_ple_gate, v_w_ple_proj):
    names = ("attn_norm_g", "w_in", "q_norm_g", "k_norm_g", "ssm_conv_w", "ssm_conv_b", "dt_bias", "a_log", "d_skip",
             "ssm_norm_g", "w_out", "ffn_norm_g", "w_up", "ffn_conv_w", "ffn_conv_b", "w_down", "ple_norm_g",
             "w_ple_gate", "w_ple_proj")
    w = dict(zip(names, (attn_norm_g, w_in, q_norm_g, k_norm_g, ssm_conv_w, ssm_conv_b, dt_bias, a_log, d_skip,
                         ssm_norm_g, w_out, ffn_norm_g, w_up, ffn_conv_w, ffn_conv_b, w_down, ple_norm_g, w_ple_gate,
                         w_ple_proj)))
    m = dict(zip(names, (m_attn_norm_g, m_w_in, m_q_norm_g, m_k_norm_g, m_ssm_conv_w, m_ssm_conv_b, m_dt_bias,
                         m_a_log, m_d_skip, m_ssm_norm_g, m_w_out, m_ffn_norm_g, m_w_up, m_ffn_conv_w, m_ffn_conv_b,
                         m_w_down, m_ple_norm_g, m_w_ple_gate, m_w_ple_proj)))
    v = dict(zip(names, (v_attn_norm_g, v_w_in, v_q_norm_g, v_k_norm_g, v_ssm_conv_w, v_ssm_conv_b, v_dt_bias,
                         v_a_log, v_d_skip, v_ssm_norm_g, v_w_out, v_ffn_norm_g, v_w_up, v_ffn_conv_w, v_ffn_conv_b,
                         v_w_down, v_ple_norm_g, v_w_ple_gate, v_w_ple_proj)))
    w, m, v = ({k: a[0] for k, a in d.items()} for d in (w, m, v))
    me = 4 * lax.axis_index("x") + 2 * lax.axis_index("y") + lax.axis_index("c")

    mine = dict(in_t=w["w_in"].T, out=w["w_out"], up_t=w["w_up"].T, down=w["w_down"], gate=w["w_ple_gate"],
                proj_t=w["w_ple_proj"].T)
    mine = {k: a.astype(BF16) for k, a in mine.items()}
    conv_pack = jnp.pad(jnp.concatenate([w["ssm_conv_w"].reshape(-1), w["ffn_conv_w"].reshape(-1)]),
                        (0, 3072 - 2880)).reshape(8, 384)
    all_in, all_conv = _all_gather([mine["in_t"], conv_pack], "gather_first")
    later = ("out", "up_t", "down", "gate", "proj_t")
    zones = [lax.dynamic_update_slice(lax.empty((N_DEV,) + mine[k].shape, BF16), mine[k][None], (me, 0, 0))
             for k in later]
    zones, all_in, all_conv = lax.optimization_barrier((zones, all_in, all_conv))
    rest_state = _exchange_start([mine[k] for k in later], zones, [False] * len(later), "gather_rest_start")

    def fetch_rest(after):
        _, landed = _exchange_wait(*rest_state[:4], [False] * len(later), after, "gather_rest_wait")
        return {k: a.reshape(N_DEV * a.shape[1], a.shape[2]) for k, a in zip(later, landed)}

    wts = dict(in_t=_pad_rows(all_in.reshape(IN_PROJ, D_MODEL), IN_PROJ_PAD))
    conv_flat = all_conv.reshape(N_DEV, 3072)
    wts["ssm_cw"] = conv_flat[:, :768].reshape(N_DEV, 4, XBC_DIM // N_DEV).transpose(1, 0, 2).reshape(4, XBC_DIM)
    wts["ffn_cw"] = conv_flat[:, 768:2880].reshape(N_DEV, 3, 2 * D_FF // N_DEV).transpose(1, 0, 2).reshape(3, 2 * D_FF)
    sm = {k: w[k].reshape(1, -1) for k, _, _ in _SMALL if k not in _SHARDED_SMALL}

    in_flight = []

    def send(grads):
        keys = sorted(grads)
        srcs = [grads[k].reshape(N_DEV, grads[k].shape[0] // N_DEV, grads[k].shape[1]) for k in keys]
        state = _exchange_start(srcs, [lax.empty(a.shape, BF16) for a in srcs], [True] * len(keys),
                                "send_" + "_".join(keys))
        in_flight.append((keys, state))
        return state[4][0:1, 0:1]

    loss, grad_x, small = _local_step(x[0], p[0, 0], loss_target[0], sm, wts, fetch_rest, send,
                                      rest_state[4][0:1, 0:1])

    (got_small,) = _all_gather([_pack_small(small, loss)], "gather_small_grads")
    g_small = _reduce8(got_small, _SMALL_ROWS, "reduce_small")
    loss = g_small[_SMALL_ROWS - 1, 0]
    grads = {}
    for keys, state in in_flight:
        sent, landed = _exchange_wait(*state[:4], [True] * len(keys), grad_x, "wait_" + "_".join(keys))
        for k, shares, land in zip(keys, sent, landed):
            grads[k] = _reduce_landed(lax.dynamic_index_in_dim(shares, me, 0, keepdims=False), land, "reduce_" + k)
    gw = {"w_in": grads["in_t"].T, "w_out": grads["out"], "w_up": grads["up_t"].T, "w_down": grads["down"],
          "w_ple_gate": grads["gate"], "w_ple_proj": grads["proj_t"].T}
    n_ssm, n_ffn = XBC_DIM // N_DEV, 2 * D_FF // N_DEV
    g_shard = {"ssm_conv_w": lax.dynamic_slice(g_small, (3, me * n_ssm), (4, n_ssm)),
               "ffn_conv_w": lax.dynamic_slice(g_small[13:25, :2 * D_FF // 4].reshape(3, 2 * D_FF), (0, me * n_ffn),
                                               (3, n_ffn))}

    delta, new_m, new_v = {}, {}, {}
    for k, tr in (("w_in", 256), ("w_out", None), ("w_up", 256), ("w_down", None), ("w_ple_gate", None),
                  ("w_ple_proj", None)):
        delta[k], new_m[k], new_v[k] = _adamw(w[k], gw[k], m[k], v[k], "adamw_" + k, tr)
    for k, (g_k, d_k, m_k, v_k) in _adamw_small(g_small, g_shard, w, m, v).items():
        gw[k], delta[k], new_m[k], new_v[k] = g_k, d_k, m_k, v_k

    outs = [loss, grad_x[None]]
    for d in (gw, delta, new_m, new_v):
        outs += [d[k][None] for k in names]
    return tuple(outs)
```

```python
import functools

import numpy as np
import jax
import jax.numpy as jnp
from jax import lax
from jax.experimental import pallas as pl
from jax.experimental.pallas import tpu as pltpu

F32 = jnp.float32
BF16 = jnp.bfloat16
SDS = jax.ShapeDtypeStruct
EPS = 1e-6
N_DEV = 8
D_MODEL = 1024
HEAD_DIM = 64
ATTN_DIM = 512
KV_DIM = 256
SSM_INNER = 1024
SSM_HEADS = 16
BC_DIM = 256
XBC_DIM = SSM_INNER + 2 * BC_DIM
MIX_DIM = ATTN_DIM + SSM_INNER
IN_PROJ = 3600
IN_PROJ_PAD = 3840
D_FF = 2816
PLE_DIM = 256
CHUNK = 128
SUPER = 2048
DILATIONS = (1, 4, 16)
TILE_UNROLL = 8
VMEM_LIMIT = 56 * 1024 * 1024
ADAM_LR, ADAM_B1, ADAM_B2, ADAM_EPS, ADAM_WD, ADAM_STEP = 0.001, 0.9, 0.999, 1e-08, 0.01, 10

NT = (((1,), (1,)), ((), ()))
TN = (((0,), (0,)), ((), ()))


def _params(*sem):
    return pltpu.CompilerParams(dimension_semantics=sem if sem else None, vmem_limit_bytes=VMEM_LIMIT)


def _dot(a, b, dims=None):
    if dims is None:
        return jnp.dot(a, b, preferred_element_type=F32)
    return lax.dot_general(a, b, dims, preferred_element_type=F32)


def _hdot(a, b, parts=2):
    a_exact = a.dtype == BF16
    x = b if a_exact else a
    acc = None
    for _ in range(parts):
        piece = x.astype(BF16)
        x = x - piece.astype(F32)
        d = _dot(a, piece) if a_exact else _dot(piece, b)
        acc = d if acc is None else acc + d
    return acc


def _sigmoid(x):
    return 0.5 * jnp.tanh(0.5 * x) + 0.5


def _shift_down(x, halo8, s):
    xr = pltpu.roll(x, s, 0)
    row = lax.broadcasted_iota(jnp.int32, halo8.shape, 0)
    first = jnp.where(row < s, pltpu.roll(halo8, s, 0), xr[0:8])
    return jnp.concatenate([first, xr[8:]], axis=0)


def _shift_up(x, halo8, s):
    n = x.shape[0]
    xr = pltpu.roll(x, n - s, 0)
    row = lax.broadcasted_iota(jnp.int32, halo8.shape, 0)
    last = jnp.where(row >= 8 - s, pltpu.roll(halo8, 8 - s, 0), xr[n - 8:])
    return jnp.concatenate([xr[:n - 8], last], axis=0)


def _norm_matmul(x, g, wt, tm, tn, name):
    t, k = x.shape
    n = wt.shape[0]

    def body(x_ref, g_ref, w_ref, o_ref, h_ref):
        @pl.when(pl.program_id(1) == 0)
        def _():
            xv = x_ref[...]
            r = lax.rsqrt(jnp.mean(xv * xv, axis=-1, keepdims=True) + EPS)
            h_ref[...] = (xv * r * g_ref[...]).astype(BF16)
        o_ref[...] = _dot(h_ref[...], w_ref[...], NT)

    return pl.pallas_call(
        body, name=name, grid=(t // tm, n // tn),
        in_specs=[pl.BlockSpec((tm, k), lambda i, j: (i, 0)), pl.BlockSpec((1, k), lambda i, j: (0, 0)),
                  pl.BlockSpec((tn, k), lambda i, j: (j, 0))],
        out_specs=[pl.BlockSpec((tm, tn), lambda i, j: (i, j)), pl.BlockSpec((tm, k), lambda i, j: (i, 0))],
        out_shape=[SDS((t, n), F32), SDS((t, k), BF16)],
        compiler_params=_params("parallel", "arbitrary"))(x, g, wt)


def _a_spec(a, lead, tm):
    if lead is None:
        return pl.BlockSpec((tm, a.shape[-1]), lambda i: (i, 0))
    return pl.BlockSpec((None, tm, a.shape[-1]), lambda i, _l=lead: (_l, i, 0))


def _mm_resid(pairs, res, dims, tm, out_dtype, name):
    t = pairs[0][0].shape[-2]
    n = pairs[0][2].shape[1] if dims is None else pairs[0][2].shape[0]
    np_ = len(pairs)

    def body(*refs):
        o_ref = refs[-1]
        acc = refs[2 * np_][...] if res is not None else None
        for q in range(np_):
            d = _dot(refs[q][...].astype(BF16), refs[np_ + q][...], dims)
            acc = d if acc is None else acc + d
        o_ref[...] = acc.astype(out_dtype)

    in_specs = [_a_spec(a, lead, tm) for a, lead, _ in pairs]
    in_specs += [pl.BlockSpec(b.shape, lambda i: (0, 0)) for _, _, b in pairs]
    args = [a for a, _, _ in pairs] + [b for _, _, b in pairs]
    if res is not None:
        in_specs.append(pl.BlockSpec((tm, n), lambda i: (i, 0)))
        args.append(res)
    return pl.pallas_call(
        body, name=name, grid=(t // tm,), in_specs=in_specs,
        out_specs=pl.BlockSpec((tm, n), lambda i: (i, 0)), out_shape=SDS((t, n), out_dtype),
        compiler_params=_params("parallel"))(*args)


def _wgrad_multi(parts, b, name, tk=2048):
    t, n = b.shape
    widths = [a.shape[1] for a in parts]
    m = sum(widths)

    def body(*refs):
        b_ref, o_ref, acc = refs[len(parts):]

        @pl.when(pl.program_id(0) == 0)
        def _():
            acc[...] = jnp.zeros_like(acc)
        bv = b_ref[...].astype(BF16)
        row = 0
        for a_ref, w in zip(refs, widths):
            acc[row:row + w, :] += _dot(a_ref[...].astype(BF16), bv, TN)
            row += w

        @pl.when(pl.program_id(0) == pl.num_programs(0) - 1)
        def _():
            o_ref[...] = acc[...].astype(BF16)

    return pl.pallas_call(
        body, name=name, grid=(t // tk,),
        in_specs=[pl.BlockSpec((tk, w), lambda k: (k, 0)) for w in widths] + [pl.BlockSpec((tk, n), lambda k: (k, 0))],
        out_specs=pl.BlockSpec((m, n), lambda k: (0, 0)), out_shape=SDS((m, n), BF16),
        scratch_shapes=[pltpu.VMEM((m, n), F32)],
        compiler_params=_params("arbitrary"))(*parts, b)


def _out_proj_bwd(dx, w_out, tm=1024):
    t = dx.shape[0]

    def body(dx_ref, w_ref, da_ref, ds_ref):
        a = dx_ref[...].astype(BF16)
        da_ref[...] = _dot(a, w_ref[0:ATTN_DIM, :], NT)
        ds_ref[...] = _dot(a, w_ref[ATTN_DIM:, :], NT)

    return pl.pallas_call(
        body, name="out_proj_bwd", grid=(t // tm,),
        in_specs=[pl.BlockSpec((tm, D_MODEL), lambda i: (i, 0)), pl.BlockSpec(w_out.shape, lambda i: (0, 0))],
        out_specs=[pl.BlockSpec((tm, ATTN_DIM), lambda i: (i, 0)), pl.BlockSpec((tm, SSM_INNER), lambda i: (i, 0))],
        out_shape=[SDS((t, ATTN_DIM), F32), SDS((t, SSM_INNER), F32)],
        compiler_params=_params("parallel"))(dx, w_out)


def _mm_normbwd(pairs, x, g, dres, tm, name):
    t, k = x.shape
    np_ = len(pairs)
    b_specs = [pl.BlockSpec((rows, b.shape[1]), lambda i, _b=blk: (_b, 0)) for _, _, b, rows, blk in pairs]
    pairs = [(a, lead, b) for a, lead, b, _, _ in pairs]

    def body(*refs):
        x_ref, g_ref, dres_ref, dx_ref, dg_ref = refs[2 * np_:]
        dh = None
        for q in range(np_):
            d = _dot(refs[q][...], refs[np_ + q][...])
            dh = d if dh is None else dh + d
        xv = x_ref[...]
        r = lax.rsqrt(jnp.mean(xv * xv, axis=-1, keepdims=True) + EPS)
        xh = xv * r

        @pl.when(pl.program_id(0) == 0)
        def _():
            dg_ref[...] = jnp.zeros_like(dg_ref)
        dg_ref[...] += jnp.sum(dh * xh, axis=0, keepdims=True)
        gd = dh * g_ref[...]
        dx_ref[...] = dres_ref[...] + r * (gd - xh * jnp.mean(gd * xh, axis=-1, keepdims=True))

    in_specs = [_a_spec(a, lead, tm) for a, lead, _ in pairs] + b_specs
    in_specs += [pl.BlockSpec((tm, k), lambda i: (i, 0)), pl.BlockSpec((1, k), lambda i: (0, 0)),
                 pl.BlockSpec((tm, k), lambda i: (i, 0))]
    args = [a for a, _, _ in pairs] + [b for _, _, b in pairs] + [x, g, dres]
    return pl.pallas_call(
        body, name=name, grid=(t // tm,), in_specs=in_specs,
        out_specs=[pl.BlockSpec((tm, k), lambda i: (i, 0)), pl.BlockSpec((1, k), lambda i: (0, 0))],
        out_shape=[SDS((t, k), F32), SDS((1, k), F32)],
        compiler_params=_params("arbitrary"))(*args)


def _wgrad(a, a_lead, b, name, tk=2048):
    t, m = a.shape[-2:]
    n = b.shape[1]
    tm = m if m <= 1024 else 1408
    assert m % tm == 0

    def body(a_ref, b_ref, o_ref, acc):
        @pl.when(pl.program_id(1) == 0)
        def _():
            acc[...] = jnp.zeros_like(acc)
        acc[...] += _dot(a_ref[...].astype(BF16), b_ref[...].astype(BF16), TN)

        @pl.when(pl.program_id(1) == pl.num_programs(1) - 1)
        def _():
            o_ref[...] = acc[...].astype(BF16)

    per, lead = m // tm, 1
    if a_lead == "all":
        lead = a.shape[0]
        a_spec = pl.BlockSpec((None, tk, tm), lambda mi, ki: (mi // per, ki, mi % per))
    elif a_lead is None:
        a_spec = pl.BlockSpec((tk, tm), lambda mi, ki: (ki, mi))
    else:
        a_spec = pl.BlockSpec((None, tk, tm), lambda mi, ki, _l=a_lead: (_l, ki, mi))
    return pl.pallas_call(
        body, name=name, grid=(lead * per, t // tk),
        in_specs=[a_spec, pl.BlockSpec((tk, n), lambda mi, ki: (ki, 0))],
        out_specs=pl.BlockSpec((tm, n), lambda mi, ki: (mi, 0)), out_shape=SDS((lead * m, n), BF16),
        scratch_shapes=[pltpu.VMEM((tm, n), F32)],
        compiler_params=_params("parallel", "arbitrary"))(a, b)


def _head_consts():
    iq = np.arange(ATTN_DIM)
    ik = np.arange(KV_DIM)
    ones_q = (iq[:, None] // HEAD_DIM == iq[None, :] // HEAD_DIM).astype(np.float32)
    ones_k = (ik[:, None] // HEAD_DIM == ik[None, :] // HEAD_DIM).astype(np.float32)
    dup = (ik[:, None] == (HEAD_DIM * (iq[None, :] // 128) + iq[None, :] % HEAD_DIM)).astype(np.float32)
    return jnp.asarray(ones_q, BF16), jnp.asarray(ones_k, BF16), jnp.asarray(dup, BF16), jnp.asarray(dup.T, BF16)


def _attn_prep(proj, gq, gk, ones_q, ones_k, dup, tm=512):
    t = proj.shape[0]

    def body(p_ref, gq_ref, gk_ref, oq_ref, ok_ref, dup_ref, qn_ref, kd_ref, vd_ref):
        q = p_ref[:, 0:ATTN_DIM]
        k = p_ref[:, ATTN_DIM:ATTN_DIM + KV_DIM]
        v = p_ref[:, ATTN_DIM + KV_DIM:]
        rq = lax.rsqrt(_hdot(q * q, oq_ref[...]) * (1.0 / HEAD_DIM) + EPS)
        qn_ref[...] = (q * rq * gq_ref[...]) * (HEAD_DIM ** -0.5)
        rk = lax.rsqrt(_hdot(k * k, ok_ref[...]) * (1.0 / HEAD_DIM) + EPS)
        kn = k * rk * gk_ref[...]
        kd_ref[...] = _dot(kn.astype(BF16), dup_ref[...])
        vd_ref[...] = _dot(v.astype(BF16), dup_ref[...])

    full = lambda a: pl.BlockSpec(a.shape, lambda i: (0, 0))
    o_spec = pl.BlockSpec((tm, ATTN_DIM), lambda i: (i, 0))
    return pl.pallas_call(
        body, name="attn_prep", grid=(t // tm,),
        in_specs=[pl.BlockSpec((tm, 1024), lambda i: (i, 0)), full(gq), full(gk), full(ones_q), full(ones_k), full(dup)],
        out_specs=[o_spec, o_spec, o_spec], out_shape=[SDS((t, ATTN_DIM), F32)] * 3,
        compiler_params=_params("parallel"))(proj, gq, gk, ones_q, ones_k, dup)


def _attn_prep_bwd(proj, dqn, dkc, dkp, dvc, dvp, gq, gk, ones_q, ones_k, dup_t, tm=512):
    t = proj.shape[0]
    nblk = t // tm
    off = SUPER // tm

    def body(p_ref, dqn_ref, dkc_ref, dkp_ref, dvc_ref, dvp_ref, gq_ref, gk_ref, oq_ref, ok_ref, dt_ref,
             o_ref, dgq_ref, dgk_ref):
        i = pl.program_id(0)
        has_next = (i + off < nblk).astype(F32)
        q = p_ref[:, 0:ATTN_DIM]
        k = p_ref[:, ATTN_DIM:ATTN_DIM + KV_DIM]
        dkn = _hdot(dkc_ref[...] + has_next * dkp_ref[...], dt_ref[...])
        dv = _hdot(dvc_ref[...] + has_next * dvp_ref[...], dt_ref[...])

        @pl.when(i == 0)
        def _():
            dgq_ref[...] = jnp.zeros_like(dgq_ref)
            dgk_ref[...] = jnp.zeros_like(dgk_ref)

        rq = lax.rsqrt(_hdot(q * q, oq_ref[...]) * (1.0 / HEAD_DIM) + EPS)
        xh = q * rq
        dy = dqn_ref[...] * (HEAD_DIM ** -0.5)
        dgq_ref[...] += jnp.sum(dy * xh, axis=0, keepdims=True)
        gd = dy * gq_ref[...]
        dq = rq * (gd - xh * (_hdot(gd * xh, oq_ref[...]) * (1.0 / HEAD_DIM)))
        rk = lax.rsqrt(_hdot(k * k, ok_ref[...]) * (1.0 / HEAD_DIM) + EPS)
        kh = k * rk
        dgk_ref[...] += jnp.sum(dkn * kh, axis=0, keepdims=True)
        gdk = dkn * gk_ref[...]
        dk = rk * (gdk - kh * (_hdot(gdk * kh, ok_ref[...]) * (1.0 / HEAD_DIM)))
        o_ref[:, 0:ATTN_DIM] = dq.astype(BF16)
        o_ref[:, ATTN_DIM:ATTN_DIM + KV_DIM] = dk.astype(BF16)
        o_ref[:, ATTN_DIM + KV_DIM:] = dv.astype(BF16)

    full = lambda a: pl.BlockSpec(a.shape, lambda i: (0, 0))
    cur = pl.BlockSpec((tm, ATTN_DIM), lambda i: (i, 0))
    nxt = pl.BlockSpec((tm, ATTN_DIM), lambda i: (jnp.minimum(i + off, nblk - 1), 0))
    return pl.pallas_call(
        body, name="attn_prep_bwd", grid=(nblk,),
        in_specs=[pl.BlockSpec((tm, 1024), lambda i: (i, 0)), cur, cur, nxt, cur, nxt,
                  full(gq), full(gk), full(ones_q), full(ones_k), full(dup_t)],
        out_specs=[pl.BlockSpec((tm, 1024), lambda i: (i, 0)), pl.BlockSpec((1, ATTN_DIM), lambda i: (0, 0)),
                   pl.BlockSpec((1, KV_DIM), lambda i: (0, 0))],
        out_shape=[SDS((t, 1024), BF16), SDS((1, ATTN_DIM), F32), SDS((1, KV_DIM), F32)],
        compiler_params=_params("arbitrary"))(proj, dqn, dkc, dkp, dvc, dvp, gq, gk, ones_q, ones_k, dup_t)


def _tile_masks():
    qi = lax.broadcasted_iota(jnp.int32, (2 * CHUNK, 2 * CHUNK), 0) & (CHUNK - 1)
    kj = lax.broadcasted_iota(jnp.int32, (2 * CHUNK, 2 * CHUNK), 1)
    delta = CHUNK + qi - kj
    band = (delta >= 0) & (delta <= CHUNK)
    return band, kj


def _deinterleave(dst, src, n_rows, d):
    per = n_rows // d
    for r in range(d):
        dst[r * per:(r + 1) * per, :] = src[pl.ds(r, per, stride=d), :]


def _attn_specs(t):
    blk = lambda f: pl.BlockSpec((SUPER, 128), f)
    cur = blk(lambda h, s: (s, h))
    prev = blk(lambda h, s: (jnp.maximum(s - 1, 0), h))
    return cur, prev


def _attn_fwd(qn, kd, vd):
    t = qn.shape[0]
    cur, prev = _attn_specs(t)

    def body(q_ref, kp_ref, kc_ref, vp_ref, vc_ref, o_ref, lse_ref, kk, vv, qd, kdd, vdd, po, pm, pll, acc, mm, ll):
        s = pl.program_id(1)
        kk[0:SUPER, :] = kp_ref[...]
        kk[SUPER:, :] = kc_ref[...]
        vv[0:SUPER, :] = vp_ref[...]
        vv[SUPER:, :] = vc_ref[...]
        m0 = lax.broadcasted_iota(jnp.int32, (CHUNK, 128), 1) < HEAD_DIM
        band, kj = _tile_masks()
        for d in DILATIONS:
            lq = SUPER // d
            if d == 1:
                qs_ref, ks_ref, vs_ref = q_ref, kk, vv
            else:
                _deinterleave(qd, q_ref, SUPER, d)
                _deinterleave(kdd, kk, 2 * SUPER, d)
                _deinterleave(vdd, vv, 2 * SUPER, d)
                qs_ref, ks_ref, vs_ref = qd, kdd, vdd

            nblk = lq // CHUNK

            def key_rows(ti):
                return pl.ds((ti // nblk) * 2 * lq + lq + (ti % nblk - 1) * CHUNK, 2 * CHUNK)

            def scores(ti):
                qt = qs_ref[pl.ds(ti * CHUNK, CHUNK), :]
                qs = jnp.concatenate([jnp.where(m0, qt, 0.0), jnp.where(m0, 0.0, qt)], axis=0).astype(BF16)
                return _dot(qs, ks_ref[key_rows(ti), :].astype(BF16), NT)

            def softmax_pv(ti, sc):
                ok = band if ti % nblk > 0 else band & (kj >= jnp.where(s > 0, 0, CHUNK))
                sc = jnp.where(ok, sc, -jnp.inf)
                mt = jnp.max(sc, axis=-1, keepdims=True)
                p = jnp.exp(sc - mt)
                lt = jnp.sum(p, axis=-1, keepdims=True)
                ot = _dot(p.astype(BF16), vs_ref[key_rows(ti), :].astype(BF16))
                qrows = pl.ds(ti * CHUNK, CHUNK)
                po[qrows, :] = jnp.where(m0, ot[:CHUNK], ot[CHUNK:])
                pm[qrows, :] = jnp.where(m0, mt[:CHUNK], mt[CHUNK:])
                pll[qrows, :] = jnp.where(m0, lt[:CHUNK], lt[CHUNK:])

            for ti in range(SUPER // CHUNK):
                softmax_pv(ti, scores(ti))
            if d == 1:
                acc[...] = po[...]
                mm[...] = pm[...]
                ll[...] = pll[...]
            else:
                for r in range(d):
                    rows = pl.ds(r, lq, stride=d)
                    seg = slice(r * lq, (r + 1) * lq)
                    m_old, m_new = mm[rows, :], pm[seg, :]
                    m_all = jnp.maximum(m_old, m_new)
                    a, b = jnp.exp(m_old - m_all), jnp.exp(m_new - m_all)
                    acc[rows, :] = acc[rows, :] * a + po[seg, :] * b
                    ll[rows, :] = ll[rows, :] * a + pll[seg, :] * b
                    mm[rows, :] = m_all
        o_ref[...] = acc[...] / ll[...]
        lse_ref[...] = mm[...] + jnp.log(ll[...])

    big = pltpu.VMEM((2 * SUPER, 128), F32)
    one = pltpu.VMEM((SUPER, 128), F32)
    return pl.pallas_call(
        body, name="attn_fwd", grid=(4, t // SUPER),
        in_specs=[cur, prev, cur, prev, cur], out_specs=[cur, cur],
        out_shape=[SDS((t, ATTN_DIM), F32)] * 2,
        scratch_shapes=[big, big, one, big, big, one, one, one, one, one, one],
        compiler_params=_params("parallel", "arbitrary"))(qn, kd, kd, vd, vd)


def _attn_bwd(qn, kd, vd, out, lse, dout, ones_pair):
    t = qn.shape[0]
    cur, prev = _attn_specs(t)

    def body(q_ref, kp_ref, kc_ref, vp_ref, vc_ref, o_ref, lse_ref, do_ref, ones_ref,
             dq_ref, dkc_ref, dkp_ref, dvc_ref, dvp_ref,
             kk, vv, od, ld, kb, vb, qsb, dosb, tk, tv, pdq, delta):
        s = pl.program_id(1)
        delta[...] = _hdot(do_ref[...] * o_ref[...], ones_ref[...])

        def per_row(a):
            ar = pltpu.roll(a, HEAD_DIM, 1)
            rows = jnp.concatenate([jnp.where(m0, a, ar), jnp.where(m0, ar, a)], axis=0)
            return jnp.concatenate([rows, rows], axis=1)

        kk[0:SUPER, :] = kp_ref[...]
        kk[SUPER:, :] = kc_ref[...]
        vv[0:SUPER, :] = vp_ref[...]
        vv[SUPER:, :] = vc_ref[...]
        for ref in (dq_ref, dkc_ref, dkp_ref, dvc_ref, dvp_ref):
            ref[...] = jnp.zeros_like(ref)
        m0 = lax.broadcasted_iota(jnp.int32, (CHUNK, 128), 1) < HEAD_DIM
        band, kj = _tile_masks()
        ninf = -jnp.inf
        for d in DILATIONS:
            lq = SUPER // d
            nblk = lq // CHUNK
            for r in range(d):
                seg = slice(r * 2 * lq, (r + 1) * 2 * lq)
                kb[seg, :] = kk[pl.ds(r, 2 * lq, stride=d), :].astype(BF16)
                vb[seg, :] = vv[pl.ds(r, 2 * lq, stride=d), :].astype(BF16)
            for ti in range(SUPER // CHUNK):
                rows = pl.ds(ti // nblk + d * CHUNK * (ti % nblk), CHUNK, stride=d)
                for src, dst in ((q_ref, qsb), (do_ref, dosb)):
                    a = src[rows, :]
                    dst[ti * 2 * CHUNK:(ti + 1) * 2 * CHUNK, :] = jnp.concatenate(
                        [jnp.where(m0, a, 0.0), jnp.where(m0, 0.0, a)], axis=0).astype(BF16)
                ld[ti * CHUNK:(ti + 1) * CHUNK, :] = lse_ref[rows, :]
                od[ti * CHUNK:(ti + 1) * CHUNK, :] = delta[rows, :]

            def operands(ti):
                r, nb = ti // nblk, ti % nblk
                stacked = slice(ti * 2 * CHUNK, (ti + 1) * 2 * CHUNK)
                krows = pl.ds(r * 2 * lq + lq + (nb - 1) * CHUNK, 2 * CHUNK)
                return stacked, krows

            def scores(ti):
                stacked, krows = operands(ti)
                kt = kb[krows, :]
                return dict(ti=ti, sc=_dot(qsb[stacked, :], kt, NT), dp=_dot(dosb[stacked, :], vb[krows, :], NT))

            def softmax_grad(c):
                qrows = slice(c["ti"] * CHUNK, (c["ti"] + 1) * CHUNK)
                ok = band if c["ti"] % nblk > 0 else band & (kj >= jnp.where(s > 0, 0, CHUNK))
                p = jnp.exp(jnp.where(ok, c.pop("sc"), ninf) - per_row(ld[qrows, :]))
                ds = p * (c.pop("dp") - per_row(od[qrows, :]))
                c.update(p=p.astype(BF16), ds=ds.astype(BF16))
                return c

            def grads(c):
                ti = c["ti"]
                stacked, krows = operands(ti)
                dqs = _dot(c["ds"], kb[krows, :])
                pdq[ti * CHUNK:(ti + 1) * CHUNK, :] = jnp.where(m0, dqs[:CHUNK], dqs[CHUNK:])
                tk[stacked, :] = _dot(c["ds"], qsb[stacked, :], TN)
                tv[stacked, :] = _dot(c["p"], dosb[stacked, :], TN)

            n_tiles = SUPER // CHUNK
            stage_a = scores(0)
            for ti in range(n_tiles):
                ahead = scores(ti + 1) if ti + 1 < n_tiles else None
                grads(softmax_grad(stage_a))
                stage_a = ahead

            for r in range(d):
                dq_ref[pl.ds(r, lq, stride=d), :] += pdq[r * lq:(r + 1) * lq, :]
                for tile_out, cur_ref, prev_ref in ((tk, dkc_ref, dkp_ref), (tv, dvc_ref, dvp_ref)):
                    first = r * nblk * 2 * CHUNK
                    prev_ref[pl.ds(SUPER - CHUNK * d + r, CHUNK, stride=d), :] += tile_out[first:first + CHUNK, :]
                    for nb in range(nblk):
                        at = (r * nblk + nb) * 2 * CHUNK
                        part = tile_out[at + CHUNK:at + 2 * CHUNK, :]
                        if nb + 1 < nblk:
                            part = part + tile_out[at + 2 * CHUNK:at + 3 * CHUNK, :]
                        cur_ref[pl.ds(r + d * nb * CHUNK, CHUNK, stride=d), :] += part

    big = pltpu.VMEM((2 * SUPER, 128), F32)
    one = pltpu.VMEM((SUPER, 128), F32)
    half = pltpu.VMEM((2 * SUPER, 128), BF16)
    return pl.pallas_call(
        body, name="attn_bwd", grid=(4, t // SUPER),
        in_specs=[cur, prev, cur, prev, cur, cur, cur, cur, pl.BlockSpec((128, 128), lambda h, s: (0, 0))],
        out_specs=[cur] * 5, out_shape=[SDS((t, ATTN_DIM), F32)] * 5,
        scratch_shapes=[big, big, one, one, half, half, half, half, big, big, one, one],
        compiler_params=_params("parallel", "arbitrary"))(qn, kd, kd, vd, vd, out, lse, dout, ones_pair)


def _ssd_consts():
    tri = np.tril(np.ones((CHUNK, CHUNK), np.float32))
    expand = np.zeros((128, SSM_INNER), np.float32)
    for h in range(SSM_HEADS):
        expand[h, h * HEAD_DIM:(h + 1) * HEAD_DIM] = 1.0
    return jnp.asarray(tri, BF16), jnp.asarray(tri.T, BF16), jnp.asarray(expand, BF16), jnp.asarray(expand.T, BF16)


def _conv4(x, halo, w_ref, b_ref):
    acc = b_ref[...] + w_ref[3:4, :] * x
    for k in range(3):
        acc = acc + w_ref[k:k + 1, :] * _shift_down(x, halo, 3 - k)
    return acc


def _softplus(x):
    return jnp.maximum(x, 0.0) + jnp.log(1.0 + jnp.exp(-jnp.abs(x)))


def _ssd_common(pre_x, pre_b, dt_ref, dtb_ref, alog_ref, tri_ref, exp_ref):
    xa = pre_x * _sigmoid(pre_x)
    ba = pre_b * _sigmoid(pre_b)
    dtv = _softplus(dt_ref[...] + dtb_ref[...])
    a_neg = -jnp.exp(alog_ref[...])
    acum = _hdot(tri_ref[...], dtv * a_neg, parts=3)
    lam = jnp.exp(acum)
    gam = jnp.exp(acum[CHUNK - 1:CHUNK, :] - acum)
    dt_e = _hdot(dtv, exp_ref[...])
    lam_e = _hdot(lam, exp_ref[...])
    gam_e = _hdot(gam, exp_ref[...])
    return dict(pre_x=pre_x, pre_b=pre_b, xa=xa, ba=ba, dtv=dtv, a_neg=a_neg, acum=acum,
                dt_e=dt_e, lam_e=lam_e, gam_e=gam_e, xdt=xa * dt_e)


def _decay(acum_t, h, transposed):
    rb = jnp.broadcast_to(acum_t[h:h + 1, :], (CHUNK, CHUNK))
    ri = lax.broadcasted_iota(jnp.int32, (CHUNK, CHUNK), 0)
    ci = lax.broadcasted_iota(jnp.int32, (CHUNK, CHUNK), 1)
    if transposed:
        return jnp.exp(jnp.where(ci >= ri, rb - rb.T, -jnp.inf))
    return jnp.exp(jnp.where(ri >= ci, rb.T - rb, -jnp.inf))


SSD_STEP = 4 * CHUNK


def _ssd_specs(t, rev):
    nc = t // SSD_STEP
    ch = (lambda c: nc - 1 - c) if rev else (lambda c: c)
    col = lambda w, j: pl.BlockSpec((SSD_STEP, w), lambda c: (ch(c), j))
    halo = lambda w, j: pl.BlockSpec((8, w), lambda c: (jnp.maximum(ch(c) * (SSD_STEP // 8) - 1, 0), j))
    return nc, ch, col, halo


def _ssd_fwd(proj, cwx, cbx, cwb, cbb, dtb, alog, dsk_e, norm_g, tri, expand):
    t = proj.shape[0]
    nc, _, col, halo = _ssd_specs(t, False)

    def body(z_all, xs_all, bc_all, dt_all, hx_ref, hb_ref, cwx_ref, cbx_ref, cwb_ref, cbb_ref, dtb_ref, alog_ref,
             dsk_ref, g_ref, tri_ref, exp_ref, y_all, hs_all, o_all, px_all, pb_all, state):
        @pl.when(pl.program_id(0) == 0)
        def _():
            state[...] = jnp.zeros_like(state)

        keep = (pl.program_id(0) > 0).astype(F32)
        for sc in range(SSD_STEP // CHUNK):
            rows = pl.ds(sc * CHUNK, CHUNK)
            before = pl.ds(sc * CHUNK - 8, 8)
            hx = hx_ref[...] * keep if sc == 0 else xs_all[before, :]
            hb = hb_ref[...] * keep if sc == 0 else bc_all[before, :]
            chunk(z_all.at[rows], xs_all.at[rows], bc_all.at[rows], dt_all.at[rows], hx, hb, cwx_ref, cbx_ref, cwb_ref,
                  cbb_ref, dtb_ref, alog_ref, dsk_ref, g_ref, tri_ref, exp_ref, y_all.at[rows],
                  hs_all.at[pl.ds(sc, 1)], o_all.at[rows], px_all.at[rows], pb_all.at[rows], state)

    def chunk(z_ref, xs_ref, bc_ref, dt_ref, hx, hb, cwx_ref, cbx_ref, cwb_ref, cbb_ref, dtb_ref, alog_ref,
              dsk_ref, g_ref, tri_ref, exp_ref, y_ref, hs_ref, o_ref, px_ref, pb_ref, state):
        pre_x = _conv4(xs_ref[...], hx, cwx_ref, cbx_ref)
        pre_b = _conv4(bc_ref[...], hb, cwb_ref, cbb_ref)
        px_ref[...] = pre_x.astype(BF16)
        pb_ref[...] = pre_b.astype(BF16)
        v = _ssd_common(pre_x, pre_b, dt_ref, dtb_ref, alog_ref, tri_ref, exp_ref)
        acum_t = v["acum"].T
        xdt, ba = v["xdt"], v["ba"]
        h_in = state[...]
        hs_ref[0] = h_in
        xg = xdt * v["gam_e"]
        m0 = lax.broadcasted_iota(jnp.int32, (CHUNK, 128), 1) < HEAD_DIM
        for g in range(2):
            bg = ba[:, g * 128:(g + 1) * 128].astype(BF16)
            cg = ba[:, 256 + g * 128:256 + (g + 1) * 128].astype(BF16)
            gl = slice(g * 512, (g + 1) * 512)
            cb = _dot(cg, bg, NT)
            y_off = _dot(cg, h_in[:, gl].astype(BF16)) * v["lam_e"][:, gl]
            s_new = _dot(bg.T, xg[:, gl].astype(BF16))
            state[:, gl] = h_in[:, gl] * v["lam_e"][CHUNK - 1:CHUNK, gl] + s_new
            for j in range(4):
                h0 = 8 * g + 2 * j
                ln = slice(g * 512 + j * 128, g * 512 + (j + 1) * 128)
                xp = xdt[:, ln].astype(BF16)
                y0 = _dot((cb * _decay(acum_t, h0, False)).astype(BF16), xp)
                y1 = _dot((cb * _decay(acum_t, h0 + 1, False)).astype(BF16), xp)
                y_ref[:, ln] = jnp.where(m0, y0, y1) + y_off[:, j * 128:(j + 1) * 128]
        z = z_ref[...]
        yg = (y_ref[...] + dsk_ref[...] * v["xa"]) * (z * _sigmoid(z))
        r = lax.rsqrt(jnp.mean(yg * yg, axis=-1, keepdims=True) + EPS)
        o_ref[...] = (yg * r * g_ref[...]).astype(BF16)

    full = lambda a: pl.BlockSpec(a.shape, lambda c: (0,) * a.ndim)
    return pl.pallas_call(
        body, name="ssd_fwd", grid=(nc,),
        in_specs=[col(1024, 1), col(1024, 2), col(512, 6), col(128, 28), halo(1024, 2), halo(512, 6),
                  full(cwx), full(cbx), full(cwb), full(cbb), full(dtb), full(alog), full(dsk_e), full(norm_g),
                  full(tri), full(expand)],
        out_specs=[pl.BlockSpec((SSD_STEP, SSM_INNER), lambda c: (c, 0)),
                   pl.BlockSpec((SSD_STEP // CHUNK, 128, SSM_INNER), lambda c: (c, 0, 0)),
                   pl.BlockSpec((SSD_STEP, SSM_INNER), lambda c: (c, 0)),
                   pl.BlockSpec((SSD_STEP, SSM_INNER), lambda c: (c, 0)), pl.BlockSpec((SSD_STEP, 512), lambda c: (c, 0))],
        out_shape=[SDS((t, SSM_INNER), F32), SDS((t // CHUNK, 128, SSM_INNER), F32), SDS((t, SSM_INNER), BF16),
                   SDS((t, SSM_INNER), BF16), SDS((t, 512), BF16)],
        scratch_shapes=[pltpu.VMEM((128, SSM_INNER), F32)],
        compiler_params=_params("arbitrary"))(proj, proj, proj, proj, proj, proj, cwx, cbx, cwb, cbb, dtb, alog,
                                              dsk_e, norm_g, tri, expand)


def _ssd_bwd(proj, pre_x, pre_b, y_ssd, hs, dout, cwx, cwb, dtb, alog, dsk_e, norm_g, tri, triu, expand, expand_t):
    t = proj.shape[0]
    nc, ch, col, halo = _ssd_specs(t, True)

    def body(z_all, xs_all, bc_all, dt_all, px_all, pb_all, y_all, hin_all, do_all,
             cwx_ref, cwb_ref, dtb_ref, alog_ref, dsk_ref, g_ref, tri_ref, triu_ref, exp_ref, expt_ref,
             dz_all, dxs_all, dbc_all, ddt_all, dg_ref, ddsk_ref, dalog_ref, ddtb_ref, dcwx_ref, dcbx_ref, dcwb_ref,
             dcbb_ref, gstate, nx_x, nx_b, dact_b, dxdt_s):
        @pl.when(pl.program_id(0) == 0)
        def _():
            gstate[...] = jnp.zeros_like(gstate)
            nx_x[...] = jnp.zeros_like(nx_x)
            nx_b[...] = jnp.zeros_like(nx_b)
            for ref in (dg_ref, ddsk_ref, dalog_ref, ddtb_ref, dcwx_ref, dcbx_ref, dcwb_ref, dcbb_ref):
                ref[...] = jnp.zeros_like(ref)

        for sc in reversed(range(SSD_STEP // CHUNK)):
            rows = pl.ds(sc * CHUNK, CHUNK)
            by_rows = [r.at[rows] for r in (z_all, xs_all, bc_all, dt_all, px_all, pb_all, y_all)]
            outs = [r.at[rows] for r in (dz_all, dxs_all, dbc_all, ddt_all)]
            chunk(*by_rows, hin_all.at[pl.ds(sc, 1)], do_all.at[rows],
                  cwx_ref, cwb_ref, dtb_ref, alog_ref, dsk_ref, g_ref, tri_ref, triu_ref, exp_ref, expt_ref,
                  *outs, dg_ref, ddsk_ref, dalog_ref, ddtb_ref, dcwx_ref, dcbx_ref, dcwb_ref, dcbb_ref,
                  gstate, nx_x, nx_b, dact_b, dxdt_s)

    def chunk(z_ref, xs_ref, bc_ref, dt_ref, px_ref, pb_ref, y_ref, hin_ref, do_ref,
              cwx_ref, cwb_ref, dtb_ref, alog_ref, dsk_ref, g_ref, tri_ref, triu_ref, exp_ref, expt_ref,
              dz_ref, dxs_ref, dbc_ref, ddt_ref, dg_ref, ddsk_ref, dalog_ref, ddtb_ref, dcwx_ref, dcbx_ref, dcwb_ref,
              dcbb_ref, gstate, nx_x, nx_b, dact_b, dxdt_s):
        v = _ssd_common(px_ref[...].astype(F32), pb_ref[...].astype(F32), dt_ref, dtb_ref, alog_ref, tri_ref, exp_ref)
        acum_t = v["acum"].T
        xa, ba, xdt, dtv = v["xa"], v["ba"], v["xdt"], v["dtv"]
        lam_e, gam_e, dt_e = v["lam_e"], v["gam_e"], v["dt_e"]
        z = z_ref[...]
        y = y_ref[...]
        sz = _sigmoid(z)
        zs = z * sz
        y_tot = y + dsk_ref[...] * xa
        yg = y_tot * zs
        r = lax.rsqrt(jnp.mean(yg * yg, axis=-1, keepdims=True) + EPS)
        yh = yg * r
        do = do_ref[...]
        dg_ref[...] += jnp.sum(do * yh, axis=0, keepdims=True)
        gd = do * g_ref[...]
        dyg = r * (gd - yh * jnp.mean(gd * yh, axis=-1, keepdims=True))
        dz_ref[...] = (dyg * y_tot * (sz * (1.0 + z * (1.0 - sz)))).astype(BF16)
        dy = dyg * zs
        ddsk_ref[...] += jnp.sum(dy * xa, axis=0, keepdims=True)
        g_out = gstate[...]
        h_in = hin_ref[0]
        lam_dy = lam_e * dy
        gam_x = gam_e * xdt
        m0 = lax.broadcasted_iota(jnp.int32, (CHUNK, 128), 1) < HEAD_DIM
        lane = lax.broadcasted_iota(jnp.int32, (CHUNK, 128), 1)
        below = (lax.broadcasted_iota(jnp.int32, (CHUNK, CHUNK), 0) >
                 lax.broadcasted_iota(jnp.int32, (CHUNK, CHUNK), 1))
        da_in = jnp.zeros((CHUNK, 128), F32)
        off_y, off_x = [], []
        for g in range(2):
            bg = ba[:, g * 128:(g + 1) * 128].astype(BF16)
            cg = ba[:, 256 + g * 128:256 + (g + 1) * 128].astype(BF16)
            gl = slice(g * 512, (g + 1) * 512)
            gg = g_out[:, gl].astype(BF16)
            bc_t = _dot(bg, cg, NT)
            cb = _dot(cg, bg, NT)
            dxdt_off = _dot(bg, gg) * gam_e[:, gl]
            off_x.append(xdt[:, gl] * dxdt_off)
            off_y.append(dy[:, gl] * (_dot(cg, h_in[:, gl].astype(BF16)) * lam_e[:, gl]))
            q_sum = jnp.zeros((CHUNK, CHUNK), F32)
            for j in range(4):
                h0 = 8 * g + 2 * j
                ln = slice(g * 512 + j * 128, g * 512 + (j + 1) * 128)
                dyp = dy[:, ln]
                dyb = dyp.astype(BF16)
                xpb = xdt[:, ln].astype(BF16)
                d0 = _dot((bc_t * _decay(acum_t, h0, True)).astype(BF16), dyb)
                d1 = _dot((bc_t * _decay(acum_t, h0 + 1, True)).astype(BF16), dyb)
                dxdt_s[:, ln] = jnp.where(m0, d0, d1) + dxdt_off[:, j * 128:(j + 1) * 128]
                for hh, dym in ((h0, jnp.where(m0, dyp, 0.0)), (h0 + 1, jnp.where(m0, 0.0, dyp))):
                    qd = _dot(dym.astype(BF16), xpb, NT) * _decay(acum_t, hh, False)
                    q_sum = q_sum + qd
                    reach = jnp.where(below, _hdot(triu_ref[...], qd * cb), 0.0)
                    da_in = jnp.where(lane == hh, jnp.sum(reach, axis=-1, keepdims=True), da_in)
            gstate[:, gl] = g_out[:, gl] * lam_e[CHUNK - 1:CHUNK, gl] + _dot(cg.T, lam_dy[:, gl].astype(BF16))
            qb = q_sum.astype(BF16)
            dact_b[:, 256 + g * 128:256 + (g + 1) * 128] = (
                _dot(qb, bg) + _dot(lam_dy[:, gl].astype(BF16), h_in[:, gl].astype(BF16), NT))
            dact_b[:, g * 128:(g + 1) * 128] = _dot(qb.T, cg) + _dot(gam_x[:, gl].astype(BF16), gg, NT)
        dxdt = dxdt_s[...]
        seg_y = _hdot(jnp.concatenate(off_y, axis=1), expt_ref[...])
        seg_x = _hdot(jnp.concatenate(off_x, axis=1), expt_ref[...])
        e_col = jnp.sum(g_out * h_in * lam_e[CHUNK - 1:CHUNK, :], axis=0, keepdims=True)
        e_seg = _hdot(jnp.broadcast_to(e_col, (8, SSM_INNER)), expt_ref[...])[0:1, :]
        da = da_in + _hdot(triu_ref[...], seg_y) + (_hdot(tri_ref[...], seg_x) - seg_x) + e_seg
        a_neg = v["a_neg"]
        ddtv = da * a_neg + _hdot(dxdt * xa, expt_ref[...])
        dalog_ref[...] += jnp.sum(da * dtv, axis=0, keepdims=True) * a_neg
        lane16 = lax.broadcasted_iota(jnp.int32, (CHUNK, 128), 1) < SSM_HEADS
        draw = jnp.where(lane16, ddtv * _sigmoid(dt_ref[...] + dtb_ref[...]), 0.0)
        ddtb_ref[...] += jnp.sum(draw, axis=0, keepdims=True)
        ddt_ref[...] = draw.astype(BF16)
        dxa = dxdt * dt_e + dy * dsk_ref[...]
        for (dact, pre, x_ref, nx, cw_ref, dcw_ref, dcb_ref, dx_ref) in (
                (dxa, v["pre_x"], xs_ref, nx_x, cwx_ref, dcwx_ref, dcbx_ref, dxs_ref),
                (dact_b[...], v["pre_b"], bc_ref, nx_b, cwb_ref, dcwb_ref, dcbb_ref, dbc_ref)):
            sp = _sigmoid(pre)
            dpre = dact * (sp * (1.0 + pre * (1.0 - sp)))
            dcb_ref[...] += jnp.sum(dpre, axis=0, keepdims=True)
            xv = x_ref[...]
            nxt = nx[...]
            dx = cw_ref[3:4, :] * dpre
            dcw_ref[3:4, :] += jnp.sum(dpre * xv, axis=0, keepdims=True)
            for k in range(3):
                d_up = _shift_up(dpre, nxt, 3 - k)
                dcw_ref[k:k + 1, :] += jnp.sum(xv * d_up, axis=0, keepdims=True)
                dx = dx + cw_ref[k:k + 1, :] * d_up
            nx[...] = dpre[0:8, :]
            dx_ref[...] = dx.astype(dx_ref.dtype)

    full = lambda a: pl.BlockSpec(a.shape, lambda c: (0,) * a.ndim)
    rowblk = lambda w: pl.BlockSpec((SSD_STEP, w), lambda c: (ch(c), 0))
    acc = lambda a, b: pl.BlockSpec((a, b), lambda c: (0, 0))
    return pl.pallas_call(
        body, name="ssd_bwd", grid=(nc,),
        in_specs=[col(1024, 1), col(1024, 2), col(512, 6), col(128, 28), rowblk(SSM_INNER), rowblk(512),
                  rowblk(SSM_INNER),
                  pl.BlockSpec((SSD_STEP // CHUNK, 128, SSM_INNER), lambda c: (ch(c), 0, 0)),
                  rowblk(SSM_INNER),
                  full(cwx), full(cwb), full(dtb), full(alog), full(dsk_e), full(norm_g),
                  full(tri), full(triu), full(expand), full(expand_t)],
        out_specs=[rowblk(SSM_INNER), rowblk(SSM_INNER), rowblk(512), rowblk(128),
                   acc(1, 1024), acc(1, 1024), acc(1, 128), acc(1, 128), acc(4, 1024), acc(1, 1024), acc(4, 512),
                   acc(1, 512)],
        out_shape=[SDS((t, SSM_INNER), BF16), SDS((t, SSM_INNER), BF16), SDS((t, 512), BF16), SDS((t, 128), BF16),
                   SDS((1, 1024), F32), SDS((1, 1024), F32), SDS((1, 128), F32), SDS((1, 128), F32),
                   SDS((4, 1024), F32), SDS((1, 1024), F32), SDS((4, 512), F32), SDS((1, 512), F32)],
        scratch_shapes=[pltpu.VMEM((128, SSM_INNER), F32), pltpu.VMEM((8, 1024), F32), pltpu.VMEM((8, 512), F32),
                        pltpu.VMEM((CHUNK, 512), F32), pltpu.VMEM((CHUNK, SSM_INNER), F32)],
        compiler_params=_params("arbitrary"))(proj, proj, proj, proj, pre_x, pre_b, y_ssd, hs, dout,
                                              cwx, cwb, dtb, alog, dsk_e, norm_g, tri, triu, expand, expand_t)


def _conv3(x, halo, w_ref, b_ref, part):
    acc = b_ref[part] + w_ref[2, part] * x
    for k in range(2):
        acc = acc + w_ref[k, part] * _shift_down(x, halo, 2 - k)
    return acc


def _up_act(x, g, w_up_t, cw, cb, tm=2048, tn=256, tr=512):
    t, k = x.shape
    nj = D_FF // tn

    def body(x_ref, g_ref, wg_ref, wv_ref, w_ref, b_ref, u_ref, c_ref, h_ref, f_ref, halo):
        i, j = pl.program_id(0), pl.program_id(1)

        @pl.when(j == 0)
        def _():
            xv = x_ref[...]
            r = lax.rsqrt(jnp.mean(xv * xv, axis=-1, keepdims=True) + EPS)
            h_ref[...] = (xv * r * g_ref[...]).astype(BF16)

        @pl.when(i == 0)
        def _():
            halo[j] = jnp.zeros((2, 8, tn), F32)

        def matmuls(r):
            rows = slice(r * tr, (r + 1) * tr)
            return [_dot(h_ref[rows, :], wt_ref[...], NT) for wt_ref in (wg_ref, wv_ref)]

        def epilogue(r, us, before):
            rows = slice(r * tr, (r + 1) * tr)
            parts = []
            for part, u in enumerate(us):
                u_ref[part, rows, :] = u.astype(BF16)
                parts.append(_conv3(u, before[part], w_ref, b_ref, part))
                c_ref[part, rows, :] = parts[-1].astype(BF16)
            gate, val = parts
            f_ref[rows, :] = (gate * _sigmoid(gate) * val).astype(BF16)
            return [u[tr - 8:, :] for u in us]

        before = [halo[j, 0], halo[j, 1]]
        pending = matmuls(0)
        for r in range(tm // tr):
            ahead = matmuls(r + 1) if r + 1 < tm // tr else None
            before = epilogue(r, pending, before)
            pending = ahead
        halo[j, 0], halo[j, 1] = before

    return pl.pallas_call(
        body, name="up_proj", grid=(t // tm, nj),
        in_specs=[pl.BlockSpec((tm, k), lambda i, j: (i, 0)), pl.BlockSpec((1, k), lambda i, j: (0, 0)),
                  pl.BlockSpec((tn, k), lambda i, j: (j, 0)), pl.BlockSpec((tn, k), lambda i, j: (j + nj, 0)),
                  pl.BlockSpec((3, 2, 1, tn), lambda i, j: (0, 0, 0, j)), pl.BlockSpec((2, 1, tn), lambda i, j: (0, 0, j))],
        out_specs=[pl.BlockSpec((2, tm, tn), lambda i, j: (0, i, j)), pl.BlockSpec((2, tm, tn), lambda i, j: (0, i, j)),
                   pl.BlockSpec((tm, k), lambda i, j: (i, 0)), pl.BlockSpec((tm, tn), lambda i, j: (i, j))],
        out_shape=[SDS((2, t, D_FF), BF16), SDS((2, t, D_FF), BF16), SDS((t, k), BF16), SDS((t, D_FF), BF16)],
        scratch_shapes=[pltpu.VMEM((nj, 2, 8, tn), F32)],
        compiler_params=_params("arbitrary", "arbitrary"))(x, g, w_up_t, w_up_t, cw, cb)


def _ffn_bwd(dx2, w_down, u, c, cw, tm=512, tn=1408):
    t = u.shape[1]
    nj, ni = D_FF // tn, t // tm
    rev = lambda i: ni - 1 - i

    def body(dx_ref, wd_ref, u_ref, c_ref, w_ref, du_ref, dcw_ref, dcb_ref, nxt):
        i = pl.program_id(1)

        @pl.when(i == 0)
        def _():
            nxt[...] = jnp.zeros_like(nxt)
            dcw_ref[...] = jnp.zeros_like(dcw_ref)
            dcb_ref[...] = jnp.zeros_like(dcb_ref)

        df = _dot(dx_ref[...].astype(BF16), wd_ref[...], NT)
        gate, val = c_ref[0].astype(F32), c_ref[1].astype(F32)
        sg = _sigmoid(gate)
        dgate = df * val * (sg * (1.0 + gate * (1.0 - sg)))
        dval = df * (gate * sg)
        for part, d in enumerate((dgate, dval)):
            uu = u_ref[part].astype(F32)
            dcb_ref[part] += jnp.sum(d, axis=0, keepdims=True)
            ahead = nxt[part]
            acc = w_ref[2, part] * d
            dcw_ref[2, part] += jnp.sum(d * uu, axis=0, keepdims=True)
            for k in range(2):
                d_up = _shift_up(d, ahead, 2 - k)
                dcw_ref[k, part] += jnp.sum(uu * d_up, axis=0, keepdims=True)
                acc = acc + w_ref[k, part] * d_up
            nxt[part] = d[0:8, :]
            du_ref[part] = acc.astype(BF16)

    w_spec = pl.BlockSpec((3, 2, 1, tn), lambda j, i: (0, 0, 0, j))
    b_spec = pl.BlockSpec((2, 1, tn), lambda j, i: (0, 0, j))
    tile = pl.BlockSpec((2, tm, tn), lambda j, i: (0, rev(i), j))
    return pl.pallas_call(
        body, name="ffn_bwd", grid=(nj, ni),
        in_specs=[pl.BlockSpec((tm, D_MODEL), lambda j, i: (rev(i), 0)), pl.BlockSpec((tn, D_MODEL), lambda j, i: (j, 0)),
                  tile, tile, w_spec],
        out_specs=[tile, w_spec, b_spec],
        out_shape=[SDS((2, t, D_FF), BF16), SDS((3, 2, 1, D_FF), F32), SDS((2, 1, D_FF), F32)],
        scratch_shapes=[pltpu.VMEM((2, 8, tn), F32)],
        compiler_params=_params("parallel", "arbitrary"))(dx2, w_down, u, c, cw)


def _ple_loss(x2, g, w_gate, p, w_proj_t, target, tm=512):
    t = x2.shape[0]

    def body(x_ref, g_ref, wg_ref, p_ref, wp_ref, tg_ref, dx_ref, dpre_ref, dpp_ref, h_ref, loss_ref, dg_ref):
        i = pl.program_id(0)
        xv = x_ref[...]
        r = lax.rsqrt(jnp.mean(xv * xv, axis=-1, keepdims=True) + EPS)
        xh = xv * r
        h = (xh * g_ref[...]).astype(BF16)
        h_ref[...] = h
        gate = _sigmoid(_dot(h, wg_ref[...]))
        pp = _dot(p_ref[...].astype(BF16), wp_ref[...], NT)
        err = (xv + gate * pp) - tg_ref[...]

        @pl.when(i == 0)
        def _():
            loss_ref[...] = jnp.zeros_like(loss_ref)
            dg_ref[...] = jnp.zeros_like(dg_ref)

        loss_ref[...] += 0.5 * jnp.sum(jnp.mean(err * err, axis=-1, keepdims=True), axis=0, keepdims=True)
        dy = err * (1.0 / D_MODEL)
        dpre = (dy * pp * gate * (1.0 - gate)).astype(BF16)
        dpre_ref[...] = dpre
        dpp_ref[...] = (dy * gate).astype(BF16)
        dh = _dot(dpre, wg_ref[...], NT)
        dg_ref[...] += jnp.sum(dh * xh, axis=0, keepdims=True)
        gd = dh * g_ref[...]
        dx_ref[...] = dy + r * (gd - xh * jnp.mean(gd * xh, axis=-1, keepdims=True))

    row = lambda w: pl.BlockSpec((tm, w), lambda i: (i, 0))
    full = lambda a: pl.BlockSpec(a.shape, lambda i: (0, 0))
    return pl.pallas_call(
        body, name="ple_loss", grid=(t // tm,),
        in_specs=[row(D_MODEL), full(g), full(w_gate), row(PLE_DIM), full(w_proj_t), row(D_MODEL)],
        out_specs=[row(D_MODEL), row(D_MODEL), row(D_MODEL), row(D_MODEL),
                   pl.BlockSpec((1, 128), lambda i: (0, 0)), pl.BlockSpec((1, D_MODEL), lambda i: (0, 0))],
        out_shape=[SDS((t, D_MODEL), F32), SDS((t, D_MODEL), BF16), SDS((t, D_MODEL), BF16), SDS((t, D_MODEL), BF16),
                   SDS((1, 128), F32), SDS((1, D_MODEL), F32)],
        compiler_params=_params("arbitrary"))(x2, g, w_gate, p, w_proj_t, target)


def _all_gather(arrays, name):
    n_a = len(arrays)

    def body(*refs):
        src, dst = refs[:n_a], refs[n_a:2 * n_a]
        send_sems, recv_sems, local_sems = refs[2 * n_a:]
        x, y, c = lax.axis_index("x"), lax.axis_index("y"), lax.axis_index("c")
        slot = lambda px, py, pc: 4 * px + 2 * py + pc
        me, sibling = (x, y, c), (x, y, 1 - c)
        chips = [(1 - x, y), (x, 1 - y), (1 - x, 1 - y)]

        def copy(a, k, block, to, own=False):
            return pltpu.make_async_remote_copy(
                src_ref=src[a] if own else dst[a].at[slot(*block)], dst_ref=dst[a].at[slot(*block)],
                send_sem=send_sems.at[a, k], recv_sem=recv_sems.at[a, k], device_id=to,
                device_id_type=pl.DeviceIdType.MESH)

        local = [pltpu.make_async_copy(src[a], dst[a].at[slot(*me)], local_sems.at[a]) for a in range(n_a)]
        for cp in local:
            cp.start()
        sends = []
        for a in range(n_a):
            sends.append(copy(a, 0, me, sibling, own=True))
            sends += [copy(a, 1 + j, me, (*chip, c), own=True) for j, chip in enumerate(chips)]
        for cp in sends:
            cp.start()
        for j, chip in enumerate(chips):
            for a in range(n_a):
                copy(a, 1 + j, (*chip, c), me).wait_recv()
                passed = copy(a, 4 + j, (*chip, c), sibling)
                passed.start()
                sends.append(passed)
        for a in range(n_a):
            copy(a, 0, sibling, me).wait_recv()
            for j, chip in enumerate(chips):
                copy(a, 4 + j, (*chip, 1 - c), me).wait_recv()
        for cp in sends:
            cp.wait_send()
        for cp in local:
            cp.wait()

    hbm = pl.BlockSpec(memory_space=pl.ANY)
    return pl.pallas_call(
        body, name=name, in_specs=[hbm] * n_a, out_specs=[hbm] * n_a,
        out_shape=[SDS((N_DEV,) + a.shape, a.dtype) for a in arrays],
        scratch_shapes=[pltpu.SemaphoreType.DMA((n_a, N_DEV - 1)), pltpu.SemaphoreType.DMA((n_a, N_DEV - 1)),
                        pltpu.SemaphoreType.DMA((n_a,))],
        )(*arrays)


def _peer(k):
    x, y, c = lax.axis_index("x"), lax.axis_index("y"), lax.axis_index("c")
    px = 1 - x if k & 4 else x
    py = 1 - y if k & 2 else y
    pc = 1 - c if k & 1 else c
    return (px, py, pc), 4 * px + 2 * py + pc


_HBM = pl.BlockSpec(memory_space=pltpu.HBM)
_SEM = pl.BlockSpec(memory_space=pltpu.SEMAPHORE)


def _split_copies(src, land, send_sems, recv_sems, scatter, arrivals):
    _, me = _peer(0)
    out = []
    for k in range(1, N_DEV):
        coords, peer = _peer(k)
        for a in range(len(src)):
            sem = a * (N_DEV - 1) + k - 1
            if scatter[a]:
                s, d = src[a].at[peer], land[a].at[k]
            else:
                s, d = src[a], land[a].at[peer if arrivals else me]
            out.append(pltpu.make_async_remote_copy(
                src_ref=s, dst_ref=d, send_sem=send_sems.at[sem], recv_sem=recv_sems.at[sem], device_id=coords,
                device_id_type=pl.DeviceIdType.MESH))
    return out


def _exchange_start(srcs, lands, scatter, name):
    n = len(srcs)

    def body(*refs):
        src, land = refs[:n], refs[n:2 * n]
        send_sems, recv_sems = refs[2 * n], refs[2 * n + 1]
        token = refs[-1]
        for cp in _split_copies(src, land, send_sems, recv_sems, scatter, False):
            cp.start()
        token[...] = jnp.zeros_like(token)

    hbm_shape = lambda a: pltpu.HBM(a.shape, a.dtype)
    sem_shape = pltpu.SemaphoreType.DMA((n * (N_DEV - 1),))
    outs = pl.pallas_call(
        body, name=name,
        out_shape=(sem_shape, sem_shape, *[hbm_shape(a) for a in srcs], *[hbm_shape(a) for a in lands],
                   SDS((8, 128), F32)),
        in_specs=[_HBM] * (2 * n), out_specs=(_SEM, _SEM, *[_HBM] * (2 * n), pl.BlockSpec(memory_space=pltpu.VMEM)),
        input_output_aliases={a: 2 + a for a in range(2 * n)},
        compiler_params=pltpu.CompilerParams(has_side_effects=pltpu.SideEffectType.DATAFLOW_SIDE_EFFECTING),
    )(*[pltpu.with_memory_space_constraint(a, pltpu.HBM) for a in list(srcs) + list(lands)])
    return outs[0], outs[1], outs[2:2 + n], outs[2 + n:2 + 2 * n], outs[-1]


def _exchange_wait(send_sems, recv_sems, srcs, lands, scatter, after, name):
    n = len(srcs)

    def body(*refs):
        src, land = refs[:n], refs[n:2 * n]
        for cp in _split_copies(src, land, refs[2 * n], refs[2 * n + 1], scatter, False):
            cp.wait_send()
        for cp in _split_copies(src, land, refs[2 * n], refs[2 * n + 1], scatter, True):
            cp.wait_recv()

    hbm_shape = lambda a: pltpu.HBM(a.shape, a.dtype)
    outs = pl.pallas_call(
        body, name=name, out_shape=tuple(hbm_shape(a) for a in list(srcs) + list(lands)),
        in_specs=[_HBM] * (2 * n) + [_SEM, _SEM, pl.BlockSpec(memory_space=pl.ANY)], out_specs=(_HBM,) * (2 * n),
        input_output_aliases={a: a for a in range(2 * n)},
        compiler_params=pltpu.CompilerParams(has_side_effects=pltpu.SideEffectType.DATAFLOW_SIDE_EFFECTING),
    )(*srcs, *lands, send_sems, recv_sems, after)
    return outs[:n], outs[n:]


def _reduce8(a, tr, name):
    _, rows, cols = a.shape

    def body(a_ref, o_ref):
        acc = a_ref[0]
        for j in range(1, N_DEV):
            acc = acc + a_ref[j]
        o_ref[...] = acc

    return pl.pallas_call(
        body, name=name, grid=(rows // tr,),
        in_specs=[pl.BlockSpec((N_DEV, tr, cols), lambda i: (0, i, 0))],
        out_specs=pl.BlockSpec((tr, cols), lambda i: (i, 0)), out_shape=SDS((rows, cols), F32),
        compiler_params=_params("parallel"))(a)


def _reduce_landed(own, land, name, tc=256):
    rows, cols = own.shape

    def body(own_ref, land_ref, o_ref):
        acc = own_ref[...].astype(F32)
        for k in range(1, N_DEV):
            acc = acc + land_ref[k].astype(F32)
        o_ref[...] = acc

    return pl.pallas_call(
        body, name=name, grid=(cols // tc,),
        in_specs=[pl.BlockSpec((rows, tc), lambda j: (0, j)), pl.BlockSpec((N_DEV, rows, tc), lambda j: (0, 0, j))],
        out_specs=pl.BlockSpec((rows, tc), lambda j: (0, j)), out_shape=SDS((rows, cols), F32),
        compiler_params=_params("parallel"))(own, land)


def _reduce_adamw(own, land, w, m, v, name, tc=256):
    rows, cols = own.shape

    def body(own_ref, land_ref, w_ref, m_ref, v_ref, g_ref, d_ref, mo_ref, vo_ref):
        g = own_ref[...].astype(F32)
        for k in range(1, N_DEV):
            g = g + land_ref[k].astype(F32)
        g_ref[...] = g
        d_ref[...], mo_ref[...], vo_ref[...] = _adam_update(w_ref[...], g, m_ref[...], v_ref[...])

    blk = pl.BlockSpec((rows, tc), lambda j: (0, j))
    return pl.pallas_call(
        body, name=name, grid=(cols // tc,),
        in_specs=[blk, pl.BlockSpec((N_DEV, rows, tc), lambda j: (0, 0, j)), blk, blk, blk], out_specs=[blk] * 4,
        out_shape=[SDS((rows, cols), F32)] * 4, compiler_params=_params("parallel"))(own, land, w, m, v)


def _adamw(w, g, m, v, name, tr=None):
    rows, cols = w.shape
    tr = rows if tr is None else tr

    def body(w_ref, g_ref, m_ref, v_ref, d_ref, mo_ref, vo_ref):
        d_ref[...], mo_ref[...], vo_ref[...] = _adam_update(w_ref[...], g_ref[...], m_ref[...], v_ref[...])

    blk = pl.BlockSpec((tr, cols), lambda i: (i, 0))
    return pl.pallas_call(
        body, name=name, grid=(rows // tr,), in_specs=[blk] * 4, out_specs=[blk] * 3,
        out_shape=[SDS((rows, cols), F32)] * 3, compiler_params=_params("parallel"))(w, g, m, v)


def _pad_rows(a, rows):
    return jnp.pad(a, ((0, rows - a.shape[0]),) + ((0, 0),) * (a.ndim - 1))


def _local_step(x, p, target, sm, wts, fetch_rest, send, tok):
    ones_q, ones_k, dup, dup_t = _head_consts()
    tri, triu, expand, expand_t = _ssd_consts()
    w_in_t = wts["in_t"]
    cwx, cwb = wts["ssm_cw"][:, :SSM_INNER], wts["ssm_cw"][:, SSM_INNER:]
    cbx, cbb = sm["ssm_conv_b"][:, :SSM_INNER], sm["ssm_conv_b"][:, SSM_INNER:]
    pad128 = lambda a: jnp.pad(a, ((0, 0), (0, 128 - a.shape[1])))
    dtb, alog = pad128(sm["dt_bias"]), pad128(sm["a_log"])
    dsk_e = jnp.repeat(sm["d_skip"], HEAD_DIM, axis=1)
    gq = jnp.tile(sm["q_norm_g"], (1, ATTN_DIM // HEAD_DIM))
    gk = jnp.tile(sm["k_norm_g"], (1, KV_DIM // HEAD_DIM))
    ffn_cw = wts["ffn_cw"].reshape(3, 2, 1, D_FF)
    ffn_cb = sm["ffn_conv_b"].reshape(2, 1, D_FF)

    proj, h1 = _norm_matmul(x, sm["attn_norm_g"] + tok, w_in_t, 1024, 768, "in_proj")
    qn, kd, vd = _attn_prep(proj, gq, gk, ones_q, ones_k, dup)
    attn_out, lse = _attn_fwd(qn, kd, vd)
    y_ssd, hs, ssm_out, pre_x, pre_b = _ssd_fwd(proj, cwx, cbx, cwb, cbb, dtb, alog, dsk_e, sm["ssm_norm_g"], tri,
                                                expand)
    rest = fetch_rest(ssm_out)
    w_out, w_up_t, w_down, w_gate, w_proj_t = (rest[k] for k in ("out", "up_t", "down", "gate", "proj_t"))
    x1 = _mm_resid([(attn_out, None, w_out[:ATTN_DIM]), (ssm_out, None, w_out[ATTN_DIM:])], x, None, 1024, F32,
                   "out_proj")
    u, uc, h2, f = _up_act(x1, sm["ffn_norm_g"], w_up_t, ffn_cw, ffn_cb)
    x2 = _mm_resid([(f, None, w_down)], x1, None, 1024, F32, "down_proj")
    dx2, dpre, dpp, h3, loss, dg_ple = _ple_loss(x2, sm["ple_norm_g"], w_gate, p, w_proj_t, target)

    g_gate = _wgrad(h3, None, dpre, "wg_gate")
    g_proj_t = _wgrad(dpp, None, p, "wg_proj")
    g_down = _wgrad(f, None, dx2, "wg_down")
    du, d_ffn_cw, d_ffn_cb = _ffn_bwd(dx2, w_down, u, uc, ffn_cw)
    dx1, dg_ffn = _mm_normbwd([(du, 0, w_up_t, D_FF, 0), (du, 1, w_up_t, D_FF, 1)], x1, sm["ffn_norm_g"], dx2, 512,
                              "up_proj_bwd")
    g_up_t = _wgrad(du, "all", h2, "wg_up")
    tok = send(dict(gate=g_gate, proj_t=g_proj_t, down=g_down, up_t=g_up_t)).astype(BF16)
    d_attn, d_ssm = _out_proj_bwd(dx1, w_out + tok)
    g_out = _wgrad_multi([attn_out, ssm_out], dx1, "wg_out")
    tok = send(dict(out=g_out))
    (dz, dxs, dbc, ddt, dg_ssm, d_dsk_e, d_alog, d_dtb, d_cwx, d_cbx, d_cwb, d_cbb) = _ssd_bwd(
        proj, pre_x, pre_b, y_ssd, hs, d_ssm, cwx, cwb, dtb + tok, alog, dsk_e, sm["ssm_norm_g"], tri, triu, expand,
        expand_t)
    dqn, dkc, dkp, dvc, dvp = _attn_bwd(qn, kd, vd, attn_out, lse, d_attn, ones_k[:128, :128])
    dqkv, dgq, dgk = _attn_prep_bwd(proj, dqn, dkc, dkp, dvc, dvp, gq + tok, gk, ones_q, ones_k, dup_t)
    pieces = [(dqkv, 0, 1024), (dz, 1024, 2048), (dxs, 2048, 3072), (dbc, 3072, 3584), (ddt, 3584, 3712)]
    g_in_t = jnp.concatenate([_wgrad_multi([dqkv, dz], h1, "wg_in_qkvz"),
                              _wgrad_multi([dxs, dbc, ddt], h1, "wg_in_xbcdt")], axis=0)[:IN_PROJ]
    tok = send(dict(in_t=g_in_t))
    grad_x, dg_attn = _mm_normbwd([(a, None, w_in_t, hi - lo, lo // (hi - lo)) for a, lo, hi in pieces], x,
                                  sm["attn_norm_g"] + tok, dx1, 512, "in_proj_bwd")

    small = dict(
        attn_norm_g=dg_attn, q_norm_g=dgq.reshape(-1, HEAD_DIM).sum(0, keepdims=True),
        k_norm_g=dgk.reshape(-1, HEAD_DIM).sum(0, keepdims=True),
        ssm_conv_w=jnp.concatenate([d_cwx, d_cwb], axis=1), ssm_conv_b=jnp.concatenate([d_cbx, d_cbb], axis=1),
        dt_bias=d_dtb[:, :SSM_HEADS], a_log=d_alog[:, :SSM_HEADS],
        d_skip=d_dsk_e.reshape(SSM_HEADS, HEAD_DIM).sum(1)[None, :], ssm_norm_g=dg_ssm, ffn_norm_g=dg_ffn,
        ffn_conv_w=d_ffn_cw.reshape(3, 2 * D_FF), ffn_conv_b=d_ffn_cb.reshape(1, 2 * D_FF), ple_norm_g=dg_ple)
    return loss[0, 0], grad_x, small


_SMALL = (("attn_norm_g", 1, 1024), ("q_norm_g", 1, 64), ("k_norm_g", 1, 64), ("ssm_conv_w", 4, XBC_DIM),
          ("ssm_conv_b", 1, XBC_DIM), ("dt_bias", 1, 16), ("a_log", 1, 16), ("d_skip", 1, 16), ("ssm_norm_g", 1, 1024),
          ("ffn_norm_g", 1, 1024), ("ffn_conv_w", 3, 2 * D_FF), ("ffn_conv_b", 1, 2 * D_FF), ("ple_norm_g", 1, 1024))
_SMALL_ROWS, _SMALL_COLS = 32, XBC_DIM
_SHARDED_SMALL = ("ssm_conv_w", "ffn_conv_w")


def _small_chunks(n):
    return 1 if n <= _SMALL_COLS else 4


def _pack_small(parts, loss):
    rows = []
    for k, r, n in _SMALL:
        c = _small_chunks(n)
        rows.append(jnp.pad(parts[k].reshape(r * c, n // c), ((0, 0), (0, _SMALL_COLS - n // c))))
    packed = _pad_rows(jnp.concatenate(rows, axis=0), _SMALL_ROWS)
    at_loss = ((lax.broadcasted_iota(jnp.int32, packed.shape, 0) == _SMALL_ROWS - 1) &
               (lax.broadcasted_iota(jnp.int32, packed.shape, 1) == 0))
    return jnp.where(at_loss, loss, packed)


def _adam_update(w, g, m, v):
    c1 = 1.0 - ADAM_B1 ** ADAM_STEP
    c2 = 1.0 - ADAM_B2 ** ADAM_STEP
    mn = ADAM_B1 * m + (1.0 - ADAM_B1) * g
    vn = ADAM_B2 * v + (1.0 - ADAM_B2) * (g * g)
    return -ADAM_LR * ((mn / c1) / (jnp.sqrt(vn / c2) + ADAM_EPS) + ADAM_WD * w), mn, vn


def _adamw_small(g_all, g_shard, w, m, v):
    ins, shapes = [g_all], []
    for k, _, _ in _SMALL:
        shape2 = w[k].shape if w[k].ndim == 2 else (1, w[k].shape[0])
        shapes.append(shape2)
        ins += ([g_shard[k]] if k in _SHARDED_SMALL else []) + [a.reshape(shape2) for a in (w[k], m[k], v[k])]

    def body(*refs):
        g_ref, pos, row = refs[0], 1, 0
        outs = refs[len(ins):]
        for i, (k, r, n) in enumerate(_SMALL):
            c = _small_chunks(n)
            if k in _SHARDED_SMALL:
                g = refs[pos][...]
                pos += 1
            elif c == 1:
                g = g_ref[row:row + r, 0:n]
            else:
                g = jnp.concatenate([g_ref[row + j:row + j + 1, 0:n // c] for j in range(c)], axis=1)
            row += r * c
            d, mn, vn = _adam_update(refs[pos][...], g, refs[pos + 1][...], refs[pos + 2][...])
            pos += 3
            for o_ref, val in zip(outs[4 * i:4 * i + 4], (g, d, mn, vn)):
                o_ref[...] = val

    res = pl.pallas_call(body, name="adamw_small",
                         out_shape=[SDS(s, F32) for s in shapes for _ in range(4)])(*ins)
    return {k: tuple(a.reshape(w[k].shape) for a in res[4 * i:4 * i + 4]) for i, (k, _, _) in enumerate(_SMALL)}


def kernel(x, p, attn_norm_g, w_in, q_norm_g, k_norm_g, ssm_conv_w, ssm_conv_b, dt_bias, a_log, d_skip, ssm_norm_g, w_out, ffn_norm_g, w_up, ffn_conv_w, ffn_conv_b, w_down, ple_norm_g, w_ple_gate, w_ple_proj, loss_target, m_attn_norm_g, m_w_in, m_q_norm_g, m_k_norm_g, m_ssm_conv_w, m_ssm_conv_b, m_dt_bias, m_a_log, m_d_skip, m_ssm_norm_g, m_w_out, m_ffn_norm_g, m_w_up, m_ffn_conv_w, m_ffn_conv_b, m_w_down, m_ple_norm_g, m_w_ple_gate, m_w_ple_proj, v_attn_norm_g, v_w_in, v_q_norm_g, v_k_norm_g, v_ssm_conv_w, v_ssm_conv_b, v_dt_bias, v_a_log, v_d_skip, v_ssm_norm_g, v_w_out, v_ffn_norm_g, v_w_up, v_ffn_conv_w, v_ffn_conv_b, v_w_down, v_ple_norm_g, v_w_ple_gate, v_w_ple_proj):
    names = ("attn_norm_g", "w_in", "q_norm_g", "k_norm_g", "ssm_conv_w", "ssm_conv_b", "dt_bias", "a_log", "d_skip",
             "ssm_norm_g", "w_out", "ffn_norm_g", "w_up", "ffn_conv_w", "ffn_conv_b", "w_down", "ple_norm_g",
             "w_ple_gate", "w_ple_proj")
    w = dict(zip(names, (attn_norm_g, w_in, q_norm_g, k_norm_g, ssm_conv_w, ssm_conv_b, dt_bias, a_log, d_skip,
                         ssm_norm_g, w_out, ffn_norm_g, w_up, ffn_conv_w, ffn_conv_b, w_down, ple_norm_g, w_ple_gate,
                         w_ple_proj)))
    m = dict(zip(names, (m_attn_norm_g, m_w_in, m_q_norm_g, m_k_norm_g, m_ssm_conv_w, m_ssm_conv_b, m_dt_bias,
                         m_a_log, m_d_skip, m_ssm_norm_g, m_w_out, m_ffn_norm_g, m_w_up, m_ffn_conv_w, m_ffn_conv_b,
                         m_w_down, m_ple_norm_g, m_w_ple_gate, m_w_ple_proj)))
    v = dict(zip(names, (v_attn_norm_g, v_w_in, v_q_norm_g, v_k_norm_g, v_ssm_conv_w, v_ssm_conv_b, v_dt_bias,
                         v_a_log, v_d_skip, v_ssm_norm_g, v_w_out, v_ffn_norm_g, v_w_up, v_ffn_conv_w, v_ffn_conv_b,
                         v_w_down, v_ple_norm_g, v_w_ple_gate, v_w_ple_proj)))
    w, m, v = ({k: a[0] for k, a in d.items()} for d in (w, m, v))
    me = 4 * lax.axis_index("x") + 2 * lax.axis_index("y") + lax.axis_index("c")

    mine = dict(in_t=w["w_in"].T, out=w["w_out"], up_t=w["w_up"].T, down=w["w_down"], gate=w["w_ple_gate"],
                proj_t=w["w_ple_proj"].T)
    mine = {k: a.astype(BF16) for k, a in mine.items()}
    conv_pack = jnp.pad(jnp.concatenate([w["ssm_conv_w"].reshape(-1), w["ffn_conv_w"].reshape(-1)]),
                        (0, 3072 - 2880)).reshape(8, 384)
    all_in, all_conv = _all_gather([mine["in_t"], conv_pack], "gather_first")
    later = ("out", "up_t", "down", "gate", "proj_t")
    zones = [lax.dynamic_update_slice(lax.empty((N_DEV,) + mine[k].shape, BF16), mine[k][None], (me, 0, 0))
             for k in later]
    zones, all_in, all_conv = lax.optimization_barrier((zones, all_in, all_conv))
    rest_state = _exchange_start([mine[k] for k in later], zones, [False] * len(later), "gather_rest_start")

    def fetch_rest(after):
        _, landed = _exchange_wait(*rest_state[:4], [False] * len(later), after, "gather_rest_wait")
        return {k: a.reshape(N_DEV * a.shape[1], a.shape[2]) for k, a in zip(later, landed)}

    wts = dict(in_t=_pad_rows(all_in.reshape(IN_PROJ, D_MODEL), IN_PROJ_PAD))
    conv_flat = all_conv.reshape(N_DEV, 3072)
    wts["ssm_cw"] = conv_flat[:, :768].reshape(N_DEV, 4, XBC_DIM // N_DEV).transpose(1, 0, 2).reshape(4, XBC_DIM)
    wts["ffn_cw"] = conv_flat[:, 768:2880].reshape(N_DEV, 3, 2 * D_FF // N_DEV).transpose(1, 0, 2).reshape(3, 2 * D_FF)
    sm = {k: w[k].reshape(1, -1) for k, _, _ in _SMALL if k not in _SHARDED_SMALL}

    in_flight = []

    def send(grads):
        keys = sorted(grads)
        srcs = [grads[k].reshape(N_DEV, grads[k].shape[0] // N_DEV, grads[k].shape[1]) for k in keys]
        state = _exchange_start(srcs, [lax.empty(a.shape, BF16) for a in srcs], [True] * len(keys),
                                "send_" + "_".join(keys))
        in_flight.append((keys, state))
        return state[4][0:1, 0:1]

    loss, grad_x, small = _local_step(x[0], p[0, 0], loss_target[0], sm, wts, fetch_rest, send,
                                      rest_state[4][0:1, 0:1])

    (got_small,) = _all_gather([_pack_small(small, loss)], "gather_small_grads")
    g_small = _reduce8(got_small, _SMALL_ROWS, "reduce_small")
    loss = g_small[_SMALL_ROWS - 1, 0]
    grads, gw, delta, new_m, new_v = {}, {}, {}, {}, {}
    row_sharded = {"out": "w_out", "down": "w_down", "gate": "w_ple_gate"}
    for keys, state in in_flight:
        sent, landed = _exchange_wait(*state[:4], [True] * len(keys), grad_x, "wait_" + "_".join(keys))
        for k, shares, land in zip(keys, sent, landed):
            own = lax.dynamic_index_in_dim(shares, me, 0, keepdims=False)
            if k in row_sharded:
                n = row_sharded[k]
                gw[n], delta[n], new_m[n], new_v[n] = _reduce_adamw(own, land, w[n], m[n], v[n], "update_" + n)
            else:
                grads[k] = _reduce_landed(own, land, "reduce_" + k)
    gw.update({"w_in": grads["in_t"].T, "w_up": grads["up_t"].T, "w_ple_proj": grads["proj_t"].T})
    n_ssm, n_ffn = XBC_DIM // N_DEV, 2 * D_FF // N_DEV
    g_shard = {"ssm_conv_w": lax.dynamic_slice(g_small, (3, me * n_ssm), (4, n_ssm)),
               "ffn_conv_w": lax.dynamic_slice(g_small[13:25, :2 * D_FF // 4].reshape(3, 2 * D_FF), (0, me * n_ffn),
                                               (3, n_ffn))}

    for k, tr in (("w_in", 256), ("w_up", 256), ("w_ple_proj", None)):
        delta[k], new_m[k], new_v[k] = _adamw(w[k], gw[k], m[k], v[k], "adamw_" + k, tr)
    for k, (g_k, d_k, m_k, v_k) in _adamw_small(g_small, g_shard, w, m, v).items():
        gw[k], delta[k], new_m[k], new_v[k] = g_k, d_k, m_k, v_k

    outs = [loss, grad_x[None]]
    for d in (gw, delta, new_m, new_v):
        outs += [d[k][None] for k in names]
    return tuple(outs)
```

```python
import functools

import numpy as np
import jax
import jax.numpy as jnp
from jax import lax
from jax.experimental import pallas as pl
from jax.experimental.pallas import tpu as pltpu

F32 = jnp.float32
BF16 = jnp.bfloat16
SDS = jax.ShapeDtypeStruct
EPS = 1e-6
N_DEV = 8
D_MODEL = 1024
HEAD_DIM = 64
ATTN_DIM = 512
KV_DIM = 256
SSM_INNER = 1024
SSM_HEADS = 16
BC_DIM = 256
XBC_DIM = SSM_INNER + 2 * BC_DIM
MIX_DIM = ATTN_DIM + SSM_INNER
IN_PROJ = 3600
IN_PROJ_PAD = 3840
D_FF = 2816
PLE_DIM = 256
CHUNK = 128
SUPER = 2048
DILATIONS = (1, 4, 16)
TILE_UNROLL = 8
VMEM_LIMIT = 56 * 1024 * 1024
ADAM_LR, ADAM_B1, ADAM_B2, ADAM_EPS, ADAM_WD, ADAM_STEP = 0.001, 0.9, 0.999, 1e-08, 0.01, 10

NT = (((1,), (1,)), ((), ()))
TN = (((0,), (0,)), ((), ()))


def _params(*sem):
    return pltpu.CompilerParams(dimension_semantics=sem if sem else None, vmem_limit_bytes=VMEM_LIMIT)


def _dot(a, b, dims=None):
    if dims is None:
        return jnp.dot(a, b, preferred_element_type=F32)
    return lax.dot_general(a, b, dims, preferred_element_type=F32)


def _hdot(a, b, parts=2):
    a_exact = a.dtype == BF16
    x = b if a_exact else a
    acc = None
    for _ in range(parts):
        piece = x.astype(BF16)
        x = x - piece.astype(F32)
        d = _dot(a, piece) if a_exact else _dot(piece, b)
        acc = d if acc is None else acc + d
    return acc


def _sigmoid(x):
    return 0.5 * jnp.tanh(0.5 * x) + 0.5


def _shift_down(x, halo8, s):
    xr = pltpu.roll(x, s, 0)
    row = lax.broadcasted_iota(jnp.int32, halo8.shape, 0)
    first = jnp.where(row < s, pltpu.roll(halo8, s, 0), xr[0:8])
    return jnp.concatenate([first, xr[8:]], axis=0)


def _shift_up(x, halo8, s):
    n = x.shape[0]
    xr = pltpu.roll(x, n - s, 0)
    row = lax.broadcasted_iota(jnp.int32, halo8.shape, 0)
    last = jnp.where(row >= 8 - s, pltpu.roll(halo8, 8 - s, 0), xr[n - 8:])
    return jnp.concatenate([xr[:n - 8], last], axis=0)


def _norm_matmul(x, g, wt, tm, tn, name):
    t, k = x.shape
    n = wt.shape[0]

    def body(x_ref, g_ref, w_ref, o_ref, h_ref):
        @pl.when(pl.program_id(1) == 0)
        def _():
            xv = x_ref[...]
            r = lax.rsqrt(jnp.mean(xv * xv, axis=-1, keepdims=True) + EPS)
            h_ref[...] = (xv * r * g_ref[...]).astype(BF16)
        o_ref[...] = _dot(h_ref[...], w_ref[...], NT)

    return pl.pallas_call(
        body, name=name, grid=(t // tm, n // tn),
        in_specs=[pl.BlockSpec((tm, k), lambda i, j: (i, 0)), pl.BlockSpec((1, k), lambda i, j: (0, 0)),
                  pl.BlockSpec((tn, k), lambda i, j: (j, 0))],
        out_specs=[pl.BlockSpec((tm, tn), lambda i, j: (i, j)), pl.BlockSpec((tm, k), lambda i, j: (i, 0))],
        out_shape=[SDS((t, n), F32), SDS((t, k), BF16)],
        compiler_params=_params("parallel", "arbitrary"))(x, g, wt)


def _a_spec(a, lead, tm):
    if lead is None:
        return pl.BlockSpec((tm, a.shape[-1]), lambda i: (i, 0))
    return pl.BlockSpec((None, tm, a.shape[-1]), lambda i, _l=lead: (_l, i, 0))


def _mm_resid(pairs, res, dims, tm, out_dtype, name):
    t = pairs[0][0].shape[-2]
    n = pairs[0][2].shape[1] if dims is None else pairs[0][2].shape[0]
    np_ = len(pairs)

    def body(*refs):
        o_ref = refs[-1]
        acc = refs[2 * np_][...] if res is not None else None
        for q in range(np_):
            d = _dot(refs[q][...].astype(BF16), refs[np_ + q][...], dims)
            acc = d if acc is None else acc + d
        o_ref[...] = acc.astype(out_dtype)

    in_specs = [_a_spec(a, lead, tm) for a, lead, _ in pairs]
    in_specs += [pl.BlockSpec(b.shape, lambda i: (0, 0)) for _, _, b in pairs]
    args = [a for a, _, _ in pairs] + [b for _, _, b in pairs]
    if res is not None:
        in_specs.append(pl.BlockSpec((tm, n), lambda i: (i, 0)))
        args.append(res)
    return pl.pallas_call(
        body, name=name, grid=(t // tm,), in_specs=in_specs,
        out_specs=pl.BlockSpec((tm, n), lambda i: (i, 0)), out_shape=SDS((t, n), out_dtype),
        compiler_params=_params("parallel"))(*args)


def _wgrad_multi(parts, b, name, tk=2048):
    t, n = b.shape
    widths = [a.shape[1] for a in parts]
    m = sum(widths)

    def body(*refs):
        b_ref, o_ref, acc = refs[len(parts):]

        @pl.when(pl.program_id(0) == 0)
        def _():
            acc[...] = jnp.zeros_like(acc)
        bv = b_ref[...].astype(BF16)
        row = 0
        for a_ref, w in zip(refs, widths):
            acc[row:row + w, :] += _dot(a_ref[...].astype(BF16), bv, TN)
            row += w

        @pl.when(pl.program_id(0) == pl.num_programs(0) - 1)
        def _():
            o_ref[...] = acc[...].astype(BF16)

    return pl.pallas_call(
        body, name=name, grid=(t // tk,),
        in_specs=[pl.BlockSpec((tk, w), lambda k: (k, 0)) for w in widths] + [pl.BlockSpec((tk, n), lambda k: (k, 0))],
        out_specs=pl.BlockSpec((m, n), lambda k: (0, 0)), out_shape=SDS((m, n), BF16),
        scratch_shapes=[pltpu.VMEM((m, n), F32)],
        compiler_params=_params("arbitrary"))(*parts, b)


def _out_proj_bwd(dx, w_out, tm=1024):
    t = dx.shape[0]

    def body(dx_ref, w_ref, da_ref, ds_ref):
        a = dx_ref[...].astype(BF16)
        da_ref[...] = _dot(a, w_ref[0:ATTN_DIM, :], NT)
        ds_ref[...] = _dot(a, w_ref[ATTN_DIM:, :], NT)

    return pl.pallas_call(
        body, name="out_proj_bwd", grid=(t // tm,),
        in_specs=[pl.BlockSpec((tm, D_MODEL), lambda i: (i, 0)), pl.BlockSpec(w_out.shape, lambda i: (0, 0))],
        out_specs=[pl.BlockSpec((tm, ATTN_DIM), lambda i: (i, 0)), pl.BlockSpec((tm, SSM_INNER), lambda i: (i, 0))],
        out_shape=[SDS((t, ATTN_DIM), F32), SDS((t, SSM_INNER), F32)],
        compiler_params=_params("parallel"))(dx, w_out)


def _mm_normbwd(pairs, x, g, dres, tm, name):
    t, k = x.shape
    np_ = len(pairs)
    b_specs = [pl.BlockSpec((rows, b.shape[1]), lambda i, _b=blk: (_b, 0)) for _, _, b, rows, blk in pairs]
    pairs = [(a, lead, b) for a, lead, b, _, _ in pairs]

    def body(*refs):
        x_ref, g_ref, dres_ref, dx_ref, dg_ref = refs[2 * np_:]
        dh = None
        for q in range(np_):
            d = _dot(refs[q][...], refs[np_ + q][...])
            dh = d if dh is None else dh + d
        xv = x_ref[...]
        r = lax.rsqrt(jnp.mean(xv * xv, axis=-1, keepdims=True) + EPS)
        xh = xv * r

        @pl.when(pl.program_id(0) == 0)
        def _():
            dg_ref[...] = jnp.zeros_like(dg_ref)
        dg_ref[...] += jnp.sum(dh * xh, axis=0, keepdims=True)
        gd = dh * g_ref[...]
        dx_ref[...] = dres_ref[...] + r * (gd - xh * jnp.mean(gd * xh, axis=-1, keepdims=True))

    in_specs = [_a_spec(a, lead, tm) for a, lead, _ in pairs] + b_specs
    in_specs += [pl.BlockSpec((tm, k), lambda i: (i, 0)), pl.BlockSpec((1, k), lambda i: (0, 0)),
                 pl.BlockSpec((tm, k), lambda i: (i, 0))]
    args = [a for a, _, _ in pairs] + [b for _, _, b in pairs] + [x, g, dres]
    return pl.pallas_call(
        body, name=name, grid=(t // tm,), in_specs=in_specs,
        out_specs=[pl.BlockSpec((tm, k), lambda i: (i, 0)), pl.BlockSpec((1, k), lambda i: (0, 0))],
        out_shape=[SDS((t, k), F32), SDS((1, k), F32)],
        compiler_params=_params("arbitrary"))(*args)


def _wgrad(a, a_lead, b, name, tk=2048):
    t, m = a.shape[-2:]
    n = b.shape[1]
    tm = m if m <= 1024 else 1408
    assert m % tm == 0

    def body(a_ref, b_ref, o_ref, acc):
        @pl.when(pl.program_id(1) == 0)
        def _():
            acc[...] = jnp.zeros_like(acc)
        acc[...] += _dot(a_ref[...].astype(BF16), b_ref[...].astype(BF16), TN)

        @pl.when(pl.program_id(1) == pl.num_programs(1) - 1)
        def _():
            o_ref[...] = acc[...].astype(BF16)

    per, lead = m // tm, 1
    if a_lead == "all":
        lead = a.shape[0]
        a_spec = pl.BlockSpec((None, tk, tm), lambda mi, ki: (mi // per, ki, mi % per))
    elif a_lead is None:
        a_spec = pl.BlockSpec((tk, tm), lambda mi, ki: (ki, mi))
    else:
        a_spec = pl.BlockSpec((None, tk, tm), lambda mi, ki, _l=a_lead: (_l, ki, mi))
    return pl.pallas_call(
        body, name=name, grid=(lead * per, t // tk),
        in_specs=[a_spec, pl.BlockSpec((tk, n), lambda mi, ki: (ki, 0))],
        out_specs=pl.BlockSpec((tm, n), lambda mi, ki: (mi, 0)), out_shape=SDS((lead * m, n), BF16),
        scratch_shapes=[pltpu.VMEM((tm, n), F32)],
        compiler_params=_params("parallel", "arbitrary"))(a, b)


def _head_consts():
    iq = np.arange(ATTN_DIM)
    ik = np.arange(KV_DIM)
    ones_q = (iq[:, None] // HEAD_DIM == iq[None, :] // HEAD_DIM).astype(np.float32)
    ones_k = (ik[:, None] // HEAD_DIM == ik[None, :] // HEAD_DIM).astype(np.float32)
    dup = (ik[:, None] == (HEAD_DIM * (iq[None, :] // 128) + iq[None, :] % HEAD_DIM)).astype(np.float32)
    return jnp.asarray(ones_q, BF16), jnp.asarray(ones_k, BF16), jnp.asarray(dup, BF16), jnp.asarray(dup.T, BF16)


def _attn_prep(proj, gq, gk, ones_q, ones_k, dup, tm=512):
    t = proj.shape[0]

    def body(p_ref, gq_ref, gk_ref, oq_ref, ok_ref, dup_ref, qn_ref, kd_ref, vd_ref):
        q = p_ref[:, 0:ATTN_DIM]
        k = p_ref[:, ATTN_DIM:ATTN_DIM + KV_DIM]
        v = p_ref[:, ATTN_DIM + KV_DIM:]
        rq = lax.rsqrt(_hdot(q * q, oq_ref[...]) * (1.0 / HEAD_DIM) + EPS)
        qn_ref[...] = (q * rq * gq_ref[...]) * (HEAD_DIM ** -0.5)
        rk = lax.rsqrt(_hdot(k * k, ok_ref[...]) * (1.0 / HEAD_DIM) + EPS)
        kn = k * rk * gk_ref[...]
        kd_ref[...] = _dot(kn.astype(BF16), dup_ref[...])
        vd_ref[...] = _dot(v.astype(BF16), dup_ref[...])

    full = lambda a: pl.BlockSpec(a.shape, lambda i: (0, 0))
    o_spec = pl.BlockSpec((tm, ATTN_DIM), lambda i: (i, 0))
    return pl.pallas_call(
        body, name="attn_prep", grid=(t // tm,),
        in_specs=[pl.BlockSpec((tm, 1024), lambda i: (i, 0)), full(gq), full(gk), full(ones_q), full(ones_k), full(dup)],
        out_specs=[o_spec, o_spec, o_spec], out_shape=[SDS((t, ATTN_DIM), F32)] * 3,
        compiler_params=_params("parallel"))(proj, gq, gk, ones_q, ones_k, dup)


def _attn_prep_bwd(proj, dqn, dkc, dkp, dvc, dvp, gq, gk, ones_q, ones_k, dup_t, tm=512):
    t = proj.shape[0]
    nblk = t // tm
    off = SUPER // tm

    def body(p_ref, dqn_ref, dkc_ref, dkp_ref, dvc_ref, dvp_ref, gq_ref, gk_ref, oq_ref, ok_ref, dt_ref,
             o_ref, dgq_ref, dgk_ref):
        i = pl.program_id(0)
        has_next = (i + off < nblk).astype(F32)
        q = p_ref[:, 0:ATTN_DIM]
        k = p_ref[:, ATTN_DIM:ATTN_DIM + KV_DIM]
        dkn = _hdot(dkc_ref[...] + has_next * dkp_ref[...], dt_ref[...])
        dv = _hdot(dvc_ref[...] + has_next * dvp_ref[...], dt_ref[...])

        @pl.when(i == 0)
        def _():
            dgq_ref[...] = jnp.zeros_like(dgq_ref)
            dgk_ref[...] = jnp.zeros_like(dgk_ref)

        rq = lax.rsqrt(_hdot(q * q, oq_ref[...]) * (1.0 / HEAD_DIM) + EPS)
        xh = q * rq
        dy = dqn_ref[...] * (HEAD_DIM ** -0.5)
        dgq_ref[...] += jnp.sum(dy * xh, axis=0, keepdims=True)
        gd = dy * gq_ref[...]
        dq = rq * (gd - xh * (_hdot(gd * xh, oq_ref[...]) * (1.0 / HEAD_DIM)))
        rk = lax.rsqrt(_hdot(k * k, ok_ref[...]) * (1.0 / HEAD_DIM) + EPS)
        kh = k * rk
        dgk_ref[...] += jnp.sum(dkn * kh, axis=0, keepdims=True)
        gdk = dkn * gk_ref[...]
        dk = rk * (gdk - kh * (_hdot(gdk * kh, ok_ref[...]) * (1.0 / HEAD_DIM)))
        o_ref[:, 0:ATTN_DIM] = dq.astype(BF16)
        o_ref[:, ATTN_DIM:ATTN_DIM + KV_DIM] = dk.astype(BF16)
        o_ref[:, ATTN_DIM + KV_DIM:] = dv.astype(BF16)

    full = lambda a: pl.BlockSpec(a.shape, lambda i: (0, 0))
    cur = pl.BlockSpec((tm, ATTN_DIM), lambda i: (i, 0))
    nxt = pl.BlockSpec((tm, ATTN_DIM), lambda i: (jnp.minimum(i + off, nblk - 1), 0))
    return pl.pallas_call(
        body, name="attn_prep_bwd", grid=(nblk,),
        in_specs=[pl.BlockSpec((tm, 1024), lambda i: (i, 0)), cur, cur, nxt, cur, nxt,
                  full(gq), full(gk), full(ones_q), full(ones_k), full(dup_t)],
        out_specs=[pl.BlockSpec((tm, 1024), lambda i: (i, 0)), pl.BlockSpec((1, ATTN_DIM), lambda i: (0, 0)),
                   pl.BlockSpec((1, KV_DIM), lambda i: (0, 0))],
        out_shape=[SDS((t, 1024), BF16), SDS((1, ATTN_DIM), F32), SDS((1, KV_DIM), F32)],
        compiler_params=_params("arbitrary"))(proj, dqn, dkc, dkp, dvc, dvp, gq, gk, ones_q, ones_k, dup_t)


def _tile_masks():
    qi = lax.broadcasted_iota(jnp.int32, (2 * CHUNK, 2 * CHUNK), 0) & (CHUNK - 1)
    kj = lax.broadcasted_iota(jnp.int32, (2 * CHUNK, 2 * CHUNK), 1)
    delta = CHUNK + qi - kj
    band = (delta >= 0) & (delta <= CHUNK)
    return band, kj


def _deinterleave(dst, src, n_rows, d):
    per = n_rows // d
    for r in range(d):
        dst[r * per:(r + 1) * per, :] = src[pl.ds(r, per, stride=d), :]


def _attn_specs(t):
    blk = lambda f: pl.BlockSpec((SUPER, 128), f)
    cur = blk(lambda h, s: (s, h))
    prev = blk(lambda h, s: (jnp.maximum(s - 1, 0), h))
    return cur, prev


def _attn_fwd(qn, kd, vd):
    t = qn.shape[0]
    cur, prev = _attn_specs(t)

    def body(q_ref, kp_ref, kc_ref, vp_ref, vc_ref, o_ref, lse_ref, kk, vv, qd, kdd, vdd, po, pm, pll, acc, mm, ll):
        s = pl.program_id(1)
        kk[0:SUPER, :] = kp_ref[...]
        kk[SUPER:, :] = kc_ref[...]
        vv[0:SUPER, :] = vp_ref[...]
        vv[SUPER:, :] = vc_ref[...]
        m0 = lax.broadcasted_iota(jnp.int32, (CHUNK, 128), 1) < HEAD_DIM
        band, kj = _tile_masks()
        for d in DILATIONS:
            lq = SUPER // d
            if d == 1:
                qs_ref, ks_ref, vs_ref = q_ref, kk, vv
            else:
                _deinterleave(qd, q_ref, SUPER, d)
                _deinterleave(kdd, kk, 2 * SUPER, d)
                _deinterleave(vdd, vv, 2 * SUPER, d)
                qs_ref, ks_ref, vs_ref = qd, kdd, vdd

            nblk = lq // CHUNK

            def key_rows(ti):
                return pl.ds((ti // nblk) * 2 * lq + lq + (ti % nblk - 1) * CHUNK, 2 * CHUNK)

            def scores(ti):
                qt = qs_ref[pl.ds(ti * CHUNK, CHUNK), :]
                qs = jnp.concatenate([jnp.where(m0, qt, 0.0), jnp.where(m0, 0.0, qt)], axis=0).astype(BF16)
                return _dot(qs, ks_ref[key_rows(ti), :].astype(BF16), NT)

            def softmax_pv(ti, sc):
                ok = band if ti % nblk > 0 else band & (kj >= jnp.where(s > 0, 0, CHUNK))
                sc = jnp.where(ok, sc, -jnp.inf)
                mt = jnp.max(sc, axis=-1, keepdims=True)
                p = jnp.exp(sc - mt)
                lt = jnp.sum(p, axis=-1, keepdims=True)
                ot = _dot(p.astype(BF16), vs_ref[key_rows(ti), :].astype(BF16))
                qrows = pl.ds(ti * CHUNK, CHUNK)
                po[qrows, :] = jnp.where(m0, ot[:CHUNK], ot[CHUNK:])
                pm[qrows, :] = jnp.where(m0, mt[:CHUNK], mt[CHUNK:])
                pll[qrows, :] = jnp.where(m0, lt[:CHUNK], lt[CHUNK:])

            for ti in range(SUPER // CHUNK):
                softmax_pv(ti, scores(ti))
            if d == 1:
                acc[...] = po[...]
                mm[...] = pm[...]
                ll[...] = pll[...]
            else:
                for r in range(d):
                    rows = pl.ds(r, lq, stride=d)
                    seg = slice(r * lq, (r + 1) * lq)
                    m_old, m_new = mm[rows, :], pm[seg, :]
                    m_all = jnp.maximum(m_old, m_new)
                    a, b = jnp.exp(m_old - m_all), jnp.exp(m_new - m_all)
                    acc[rows, :] = acc[rows, :] * a + po[seg, :] * b
                    ll[rows, :] = ll[rows, :] * a + pll[seg, :] * b
                    mm[rows, :] = m_all
        o_ref[...] = acc[...] / ll[...]
        lse_ref[...] = mm[...] + jnp.log(ll[...])

    big = pltpu.VMEM((2 * SUPER, 128), F32)
    one = pltpu.VMEM((SUPER, 128), F32)
    return pl.pallas_call(
        body, name="attn_fwd", grid=(4, t // SUPER),
        in_specs=[cur, prev, cur, prev, cur], out_specs=[cur, cur],
        out_shape=[SDS((t, ATTN_DIM), F32)] * 2,
        scratch_shapes=[big, big, one, big, big, one, one, one, one, one, one],
        compiler_params=_params("parallel", "arbitrary"))(qn, kd, kd, vd, vd)


def _attn_bwd(qn, kd, vd, out, lse, dout, ones_pair):
    t = qn.shape[0]
    cur, prev = _attn_specs(t)

    def body(q_ref, kp_ref, kc_ref, vp_ref, vc_ref, o_ref, lse_ref, do_ref, ones_ref,
             dq_ref, dkc_ref, dkp_ref, dvc_ref, dvp_ref,
             kk, vv, od, ld, kb, vb, qsb, dosb, tk, tv, pdq, delta):
        s = pl.program_id(1)
        delta[...] = _hdot(do_ref[...] * o_ref[...], ones_ref[...])

        def per_row(a):
            ar = pltpu.roll(a, HEAD_DIM, 1)
            rows = jnp.concatenate([jnp.where(m0, a, ar), jnp.where(m0, ar, a)], axis=0)
            return jnp.concatenate([rows, rows], axis=1)

        kk[0:SUPER, :] = kp_ref[...]
        kk[SUPER:, :] = kc_ref[...]
        vv[0:SUPER, :] = vp_ref[...]
        vv[SUPER:, :] = vc_ref[...]
        for ref in (dq_ref, dkc_ref, dkp_ref, dvc_ref, dvp_ref):
            ref[...] = jnp.zeros_like(ref)
        m0 = lax.broadcasted_iota(jnp.int32, (CHUNK, 128), 1) < HEAD_DIM
        band, kj = _tile_masks()
        ninf = -jnp.inf
        for d in DILATIONS:
            lq = SUPER // d
            nblk = lq // CHUNK
            for r in range(d):
                seg = slice(r * 2 * lq, (r + 1) * 2 * lq)
                kb[seg, :] = kk[pl.ds(r, 2 * lq, stride=d), :].astype(BF16)
                vb[seg, :] = vv[pl.ds(r, 2 * lq, stride=d), :].astype(BF16)
            for ti in range(SUPER // CHUNK):
                rows = pl.ds(ti // nblk + d * CHUNK * (ti % nblk), CHUNK, stride=d)
                for src, dst in ((q_ref, qsb), (do_ref, dosb)):
                    a = src[rows, :]
                    dst[ti * 2 * CHUNK:(ti + 1) * 2 * CHUNK, :] = jnp.concatenate(
                        [jnp.where(m0, a, 0.0), jnp.where(m0, 0.0, a)], axis=0).astype(BF16)
                ld[ti * CHUNK:(ti + 1) * CHUNK, :] = lse_ref[rows, :]
                od[ti * CHUNK:(ti + 1) * CHUNK, :] = delta[rows, :]

            def operands(ti):
                r, nb = ti // nblk, ti % nblk
                stacked = slice(ti * 2 * CHUNK, (ti + 1) * 2 * CHUNK)
                krows = pl.ds(r * 2 * lq + lq + (nb - 1) * CHUNK, 2 * CHUNK)
                return stacked, krows

            def scores(ti):
                stacked, krows = operands(ti)
                kt = kb[krows, :]
                return dict(ti=ti, sc=_dot(qsb[stacked, :], kt, NT), dp=_dot(dosb[stacked, :], vb[krows, :], NT))

            def softmax_grad(c):
                qrows = slice(c["ti"] * CHUNK, (c["ti"] + 1) * CHUNK)
                ok = band if c["ti"] % nblk > 0 else band & (kj >= jnp.where(s > 0, 0, CHUNK))
                p = jnp.exp(jnp.where(ok, c.pop("sc"), ninf) - per_row(ld[qrows, :]))
                ds = p * (c.pop("dp") - per_row(od[qrows, :]))
                c.update(p=p.astype(BF16), ds=ds.astype(BF16))
                return c

            def grads(c):
                ti = c["ti"]
                stacked, krows = operands(ti)
                dqs = _dot(c["ds"], kb[krows, :])
                pdq[ti * CHUNK:(ti + 1) * CHUNK, :] = jnp.where(m0, dqs[:CHUNK], dqs[CHUNK:])
                tk[stacked, :] = _dot(c["ds"], qsb[stacked, :], TN)
                tv[stacked, :] = _dot(c["p"], dosb[stacked, :], TN)

            n_tiles = SUPER // CHUNK
            stage_a = scores(0)
            for ti in range(n_tiles):
                ahead = scores(ti + 1) if ti + 1 < n_tiles else None
                grads(softmax_grad(stage_a))
                stage_a = ahead

            for r in range(d):
                dq_ref[pl.ds(r, lq, stride=d), :] += pdq[r * lq:(r + 1) * lq, :]
                for tile_out, cur_ref, prev_ref in ((tk, dkc_ref, dkp_ref), (tv, dvc_ref, dvp_ref)):
                    first = r * nblk * 2 * CHUNK
                    prev_ref[pl.ds(SUPER - CHUNK * d + r, CHUNK, stride=d), :] += tile_out[first:first + CHUNK, :]
                    for nb in range(nblk):
                        at = (r * nblk + nb) * 2 * CHUNK
                        part = tile_out[at + CHUNK:at + 2 * CHUNK, :]
                        if nb + 1 < nblk:
                            part = part + tile_out[at + 2 * CHUNK:at + 3 * CHUNK, :]
                        cur_ref[pl.ds(r + d * nb * CHUNK, CHUNK, stride=d), :] += part

    big = pltpu.VMEM((2 * SUPER, 128), F32)
    one = pltpu.VMEM((SUPER, 128), F32)
    half = pltpu.VMEM((2 * SUPER, 128), BF16)
    return pl.pallas_call(
        body, name="attn_bwd", grid=(4, t // SUPER),
        in_specs=[cur, prev, cur, prev, cur, cur, cur, cur, pl.BlockSpec((128, 128), lambda h, s: (0, 0))],
        out_specs=[cur] * 5, out_shape=[SDS((t, ATTN_DIM), F32)] * 5,
        scratch_shapes=[big, big, one, one, half, half, half, half, big, big, one, one],
        compiler_params=_params("parallel", "arbitrary"))(qn, kd, kd, vd, vd, out, lse, dout, ones_pair)


def _ssd_consts():
    tri = np.tril(np.ones((CHUNK, CHUNK), np.float32))
    expand = np.zeros((128, SSM_INNER), np.float32)
    for h in range(SSM_HEADS):
        expand[h, h * HEAD_DIM:(h + 1) * HEAD_DIM] = 1.0
    return jnp.asarray(tri, BF16), jnp.asarray(tri.T, BF16), jnp.asarray(expand, BF16), jnp.asarray(expand.T, BF16)


def _conv4(x, halo, w_ref, b_ref):
    acc = b_ref[...] + w_ref[3:4, :] * x
    for k in range(3):
        acc = acc + w_ref[k:k + 1, :] * _shift_down(x, halo, 3 - k)
    return acc


def _softplus(x):
    return jnp.maximum(x, 0.0) + jnp.log(1.0 + jnp.exp(-jnp.abs(x)))


def _ssd_common(pre_x, pre_b, dt_ref, dtb_ref, alog_ref, tri_ref, exp_ref):
    xa = pre_x * _sigmoid(pre_x)
    ba = pre_b * _sigmoid(pre_b)
    dtv = _softplus(dt_ref[...] + dtb_ref[...])
    a_neg = -jnp.exp(alog_ref[...])
    acum = _hdot(tri_ref[...], dtv * a_neg, parts=3)
    lam = jnp.exp(acum)
    gam = jnp.exp(acum[CHUNK - 1:CHUNK, :] - acum)
    dt_e = _hdot(dtv, exp_ref[...])
    lam_e = _hdot(lam, exp_ref[...])
    gam_e = _hdot(gam, exp_ref[...])
    return dict(pre_x=pre_x, pre_b=pre_b, xa=xa, ba=ba, dtv=dtv, a_neg=a_neg, acum=acum,
                dt_e=dt_e, lam_e=lam_e, gam_e=gam_e, xdt=xa * dt_e)


def _decay(acum_t, h, transposed):
    rb = jnp.broadcast_to(acum_t[h:h + 1, :], (CHUNK, CHUNK))
    ri = lax.broadcasted_iota(jnp.int32, (CHUNK, CHUNK), 0)
    ci = lax.broadcasted_iota(jnp.int32, (CHUNK, CHUNK), 1)
    if transposed:
        return jnp.exp(jnp.where(ci >= ri, rb - rb.T, -jnp.inf))
    return jnp.exp(jnp.where(ri >= ci, rb.T - rb, -jnp.inf))


SSD_STEP = 4 * CHUNK


def _ssd_specs(t, rev):
    nc = t // SSD_STEP
    ch = (lambda c: nc - 1 - c) if rev else (lambda c: c)
    col = lambda w, j: pl.BlockSpec((SSD_STEP, w), lambda c: (ch(c), j))
    halo = lambda w, j: pl.BlockSpec((8, w), lambda c: (jnp.maximum(ch(c) * (SSD_STEP // 8) - 1, 0), j))
    return nc, ch, col, halo


def _ssd_fwd(proj, cwx, cbx, cwb, cbb, dtb, alog, dsk_e, norm_g, tri, expand):
    t = proj.shape[0]
    nc, _, col, halo = _ssd_specs(t, False)

    def body(z_all, xs_all, bc_all, dt_all, hx_ref, hb_ref, cwx_ref, cbx_ref, cwb_ref, cbb_ref, dtb_ref, alog_ref,
             dsk_ref, g_ref, tri_ref, exp_ref, y_all, hs_all, o_all, px_all, pb_all, state):
        @pl.when(pl.program_id(0) == 0)
        def _():
            state[...] = jnp.zeros_like(state)

        keep = (pl.program_id(0) > 0).astype(F32)
        for sc in range(SSD_STEP // CHUNK):
            rows = pl.ds(sc * CHUNK, CHUNK)
            before = pl.ds(sc * CHUNK - 8, 8)
            hx = hx_ref[...] * keep if sc == 0 else xs_all[before, :]
            hb = hb_ref[...] * keep if sc == 0 else bc_all[before, :]
            chunk(z_all.at[rows], xs_all.at[rows], bc_all.at[rows], dt_all.at[rows], hx, hb, cwx_ref, cbx_ref, cwb_ref,
                  cbb_ref, dtb_ref, alog_ref, dsk_ref, g_ref, tri_ref, exp_ref, y_all.at[rows],
                  hs_all.at[pl.ds(sc, 1)], o_all.at[rows], px_all.at[rows], pb_all.at[rows], state)

    def chunk(z_ref, xs_ref, bc_ref, dt_ref, hx, hb, cwx_ref, cbx_ref, cwb_ref, cbb_ref, dtb_ref, alog_ref,
              dsk_ref, g_ref, tri_ref, exp_ref, y_ref, hs_ref, o_ref, px_ref, pb_ref, state):
        pre_x = _conv4(xs_ref[...], hx, cwx_ref, cbx_ref)
        pre_b = _conv4(bc_ref[...], hb, cwb_ref, cbb_ref)
        px_ref[...] = pre_x.astype(BF16)
        pb_ref[...] = pre_b.astype(BF16)
        v = _ssd_common(pre_x, pre_b, dt_ref, dtb_ref, alog_ref, tri_ref, exp_ref)
        acum_t = v["acum"].T
        xdt, ba = v["xdt"], v["ba"]
        h_in = state[...]
        hs_ref[0] = h_in
        xg = xdt * v["gam_e"]
        m0 = lax.broadcasted_iota(jnp.int32, (CHUNK, 128), 1) < HEAD_DIM
        for g in range(2):
            bg = ba[:, g * 128:(g + 1) * 128].astype(BF16)
            cg = ba[:, 256 + g * 128:256 + (g + 1) * 128].astype(BF16)
            gl = slice(g * 512, (g + 1) * 512)
            cb = _dot(cg, bg, NT)
            y_off = _dot(cg, h_in[:, gl].astype(BF16)) * v["lam_e"][:, gl]
            s_new = _dot(bg.T, xg[:, gl].astype(BF16))
            state[:, gl] = h_in[:, gl] * v["lam_e"][CHUNK - 1:CHUNK, gl] + s_new
            for j in range(4):
                h0 = 8 * g + 2 * j
                ln = slice(g * 512 + j * 128, g * 512 + (j + 1) * 128)
                xp = xdt[:, ln].astype(BF16)
                y0 = _dot((cb * _decay(acum_t, h0, False)).astype(BF16), xp)
                y1 = _dot((cb * _decay(acum_t, h0 + 1, False)).astype(BF16), xp)
                y_ref[:, ln] = jnp.where(m0, y0, y1) + y_off[:, j * 128:(j + 1) * 128]
        z = z_ref[...]
        yg = (y_ref[...] + dsk_ref[...] * v["xa"]) * (z * _sigmoid(z))
        r = lax.rsqrt(jnp.mean(yg * yg, axis=-1, keepdims=True) + EPS)
        o_ref[...] = (yg * r * g_ref[...]).astype(BF16)

    full = lambda a: pl.BlockSpec(a.shape, lambda c: (0,) * a.ndim)
    return pl.pallas_call(
        body, name="ssd_fwd", grid=(nc,),
        in_specs=[col(1024, 1), col(1024, 2), col(512, 6), col(128, 28), halo(1024, 2), halo(512, 6),
                  full(cwx), full(cbx), full(cwb), full(cbb), full(dtb), full(alog), full(dsk_e), full(norm_g),
                  full(tri), full(expand)],
        out_specs=[pl.BlockSpec((SSD_STEP, SSM_INNER), lambda c: (c, 0)),
                   pl.BlockSpec((SSD_STEP // CHUNK, 128, SSM_INNER), lambda c: (c, 0, 0)),
                   pl.BlockSpec((SSD_STEP, SSM_INNER), lambda c: (c, 0)),
                   pl.BlockSpec((SSD_STEP, SSM_INNER), lambda c: (c, 0)), pl.BlockSpec((SSD_STEP, 512), lambda c: (c, 0))],
        out_shape=[SDS((t, SSM_INNER), F32), SDS((t // CHUNK, 128, SSM_INNER), F32), SDS((t, SSM_INNER), BF16),
                   SDS((t, SSM_INNER), BF16), SDS((t, 512), BF16)],
        scratch_shapes=[pltpu.VMEM((128, SSM_INNER), F32)],
        compiler_params=_params("arbitrary"))(proj, proj, proj, proj, proj, proj, cwx, cbx, cwb, cbb, dtb, alog,
                                              dsk_e, norm_g, tri, expand)


def _ssd_bwd(proj, pre_x, pre_b, y_ssd, hs, dout, cwx, cwb, dtb, alog, dsk_e, norm_g, tri, triu, expand, expand_t):
    t = proj.shape[0]
    nc, ch, col, halo = _ssd_specs(t, True)

    def body(z_all, xs_all, bc_all, dt_all, px_all, pb_all, y_all, hin_all, do_all,
             cwx_ref, cwb_ref, dtb_ref, alog_ref, dsk_ref, g_ref, tri_ref, triu_ref, exp_ref, expt_ref,
             dz_all, dxs_all, dbc_all, ddt_all, dg_ref, ddsk_ref, dalog_ref, ddtb_ref, dcwx_ref, dcbx_ref, dcwb_ref,
             dcbb_ref, gstate, nx_x, nx_b, dact_b, dxdt_s):
        @pl.when(pl.program_id(0) == 0)
        def _():
            gstate[...] = jnp.zeros_like(gstate)
            nx_x[...] = jnp.zeros_like(nx_x)
            nx_b[...] = jnp.zeros_like(nx_b)
            for ref in (dg_ref, ddsk_ref, dalog_ref, ddtb_ref, dcwx_ref, dcbx_ref, dcwb_ref, dcbb_ref):
                ref[...] = jnp.zeros_like(ref)

        for sc in reversed(range(SSD_STEP // CHUNK)):
            rows = pl.ds(sc * CHUNK, CHUNK)
            by_rows = [r.at[rows] for r in (z_all, xs_all, bc_all, dt_all, px_all, pb_all, y_all)]
            outs = [r.at[rows] for r in (dz_all, dxs_all, dbc_all, ddt_all)]
            chunk(*by_rows, hin_all.at[pl.ds(sc, 1)], do_all.at[rows],
                  cwx_ref, cwb_ref, dtb_ref, alog_ref, dsk_ref, g_ref, tri_ref, triu_ref, exp_ref, expt_ref,
                  *outs, dg_ref, ddsk_ref, dalog_ref, ddtb_ref, dcwx_ref, dcbx_ref, dcwb_ref, dcbb_ref,
                  gstate, nx_x, nx_b, dact_b, dxdt_s)

    def chunk(z_ref, xs_ref, bc_ref, dt_ref, px_ref, pb_ref, y_ref, hin_ref, do_ref,
              cwx_ref, cwb_ref, dtb_ref, alog_ref, dsk_ref, g_ref, tri_ref, triu_ref, exp_ref, expt_ref,
              dz_ref, dxs_ref, dbc_ref, ddt_ref, dg_ref, ddsk_ref, dalog_ref, ddtb_ref, dcwx_ref, dcbx_ref, dcwb_ref,
              dcbb_ref, gstate, nx_x, nx_b, dact_b, dxdt_s):
        v = _ssd_common(px_ref[...].astype(F32), pb_ref[...].astype(F32), dt_ref, dtb_ref, alog_ref, tri_ref, exp_ref)
        acum_t = v["acum"].T
        xa, ba, xdt, dtv = v["xa"], v["ba"], v["xdt"], v["dtv"]
        lam_e, gam_e, dt_e = v["lam_e"], v["gam_e"], v["dt_e"]
        z = z_ref[...]
        y = y_ref[...]
        sz = _sigmoid(z)
        zs = z * sz
        y_tot = y + dsk_ref[...] * xa
        yg = y_tot * zs
        r = lax.rsqrt(jnp.mean(yg * yg, axis=-1, keepdims=True) + EPS)
        yh = yg * r
        do = do_ref[...]
        dg_ref[...] += jnp.sum(do * yh, axis=0, keepdims=True)
        gd = do * g_ref[...]
        dyg = r * (gd - yh * jnp.mean(gd * yh, axis=-1, keepdims=True))
        dz_ref[...] = (dyg * y_tot * (sz * (1.0 + z * (1.0 - sz)))).astype(BF16)
        dy = dyg * zs
        ddsk_ref[...] += jnp.sum(dy * xa, axis=0, keepdims=True)
        g_out = gstate[...]
        h_in = hin_ref[0]
        lam_dy = lam_e * dy
        gam_x = gam_e * xdt
        m0 = lax.broadcasted_iota(jnp.int32, (CHUNK, 128), 1) < HEAD_DIM
        lane = lax.broadcasted_iota(jnp.int32, (CHUNK, 128), 1)
        below = (lax.broadcasted_iota(jnp.int32, (CHUNK, CHUNK), 0) >
                 lax.broadcasted_iota(jnp.int32, (CHUNK, CHUNK), 1))
        da_in = jnp.zeros((CHUNK, 128), F32)
        off_y, off_x = [], []
        for g in range(2):
            bg = ba[:, g * 128:(g + 1) * 128].astype(BF16)
            cg = ba[:, 256 + g * 128:256 + (g + 1) * 128].astype(BF16)
            gl = slice(g * 512, (g + 1) * 512)
            gg = g_out[:, gl].astype(BF16)
            bc_t = _dot(bg, cg, NT)
            cb = _dot(cg, bg, NT)
            dxdt_off = _dot(bg, gg) * gam_e[:, gl]
            off_x.append(xdt[:, gl] * dxdt_off)
            off_y.append(dy[:, gl] * (_dot(cg, h_in[:, gl].astype(BF16)) * lam_e[:, gl]))
            q_sum = jnp.zeros((CHUNK, CHUNK), F32)
            for j in range(4):
                h0 = 8 * g + 2 * j
                ln = slice(g * 512 + j * 128, g * 512 + (j + 1) * 128)
                dyp = dy[:, ln]
                dyb = dyp.astype(BF16)
                xpb = xdt[:, ln].astype(BF16)
                d0 = _dot((bc_t * _decay(acum_t, h0, True)).astype(BF16), dyb)
                d1 = _dot((bc_t * _decay(acum_t, h0 + 1, True)).astype(BF16), dyb)
                dxdt_s[:, ln] = jnp.where(m0, d0, d1) + dxdt_off[:, j * 128:(j + 1) * 128]
                for hh, dym in ((h0, jnp.where(m0, dyp, 0.0)), (h0 + 1, jnp.where(m0, 0.0, dyp))):
                    qd = _dot(dym.astype(BF16), xpb, NT) * _decay(acum_t, hh, False)
                    q_sum = q_sum + qd
                    reach = jnp.where(below, _hdot(triu_ref[...], qd * cb), 0.0)
                    da_in = jnp.where(lane == hh, jnp.sum(reach, axis=-1, keepdims=True), da_in)
            gstate[:, gl] = g_out[:, gl] * lam_e[CHUNK - 1:CHUNK, gl] + _dot(cg.T, lam_dy[:, gl].astype(BF16))
            qb = q_sum.astype(BF16)
            dact_b[:, 256 + g * 128:256 + (g + 1) * 128] = (
                _dot(qb, bg) + _dot(lam_dy[:, gl].astype(BF16), h_in[:, gl].astype(BF16), NT))
            dact_b[:, g * 128:(g + 1) * 128] = _dot(qb.T, cg) + _dot(gam_x[:, gl].astype(BF16), gg, NT)
        dxdt = dxdt_s[...]
        seg_y = _hdot(jnp.concatenate(off_y, axis=1), expt_ref[...])
        seg_x = _hdot(jnp.concatenate(off_x, axis=1), expt_ref[...])
        e_col = jnp.sum(g_out * h_in * lam_e[CHUNK - 1:CHUNK, :], axis=0, keepdims=True)
        e_seg = _hdot(jnp.broadcast_to(e_col, (8, SSM_INNER)), expt_ref[...])[0:1, :]
        da = da_in + _hdot(triu_ref[...], seg_y) + (_hdot(tri_ref[...], seg_x) - seg_x) + e_seg
        a_neg = v["a_neg"]
        ddtv = da * a_neg + _hdot(dxdt * xa, expt_ref[...])
        dalog_ref[...] += jnp.sum(da * dtv, axis=0, keepdims=True) * a_neg
        lane16 = lax.broadcasted_iota(jnp.int32, (CHUNK, 128), 1) < SSM_HEADS
        draw = jnp.where(lane16, ddtv * _sigmoid(dt_ref[...] + dtb_ref[...]), 0.0)
        ddtb_ref[...] += jnp.sum(draw, axis=0, keepdims=True)
        ddt_ref[...] = draw.astype(BF16)
        dxa = dxdt * dt_e + dy * dsk_ref[...]
        for (dact, pre, x_ref, nx, cw_ref, dcw_ref, dcb_ref, dx_ref) in (
                (dxa, v["pre_x"], xs_ref, nx_x, cwx_ref, dcwx_ref, dcbx_ref, dxs_ref),
                (dact_b[...], v["pre_b"], bc_ref, nx_b, cwb_ref, dcwb_ref, dcbb_ref, dbc_ref)):
            sp = _sigmoid(pre)
            dpre = dact * (sp * (1.0 + pre * (1.0 - sp)))
            dcb_ref[...] += jnp.sum(dpre, axis=0, keepdims=True)
            xv = x_ref[...]
            nxt = nx[...]
            dx = cw_ref[3:4, :] * dpre
            dcw_ref[3:4, :] += jnp.sum(dpre * xv, axis=0, keepdims=True)
            for k in range(3):
                d_up = _shift_up(dpre, nxt, 3 - k)
                dcw_ref[k:k + 1, :] += jnp.sum(xv * d_up, axis=0, keepdims=True)
                dx = dx + cw_ref[k:k + 1, :] * d_up
            nx[...] = dpre[0:8, :]
            dx_ref[...] = dx.astype(dx_ref.dtype)

    full = lambda a: pl.BlockSpec(a.shape, lambda c: (0,) * a.ndim)
    rowblk = lambda w: pl.BlockSpec((SSD_STEP, w), lambda c: (ch(c), 0))
    acc = lambda a, b: pl.BlockSpec((a, b), lambda c: (0, 0))
    return pl.pallas_call(
        body, name="ssd_bwd", grid=(nc,),
        in_specs=[col(1024, 1), col(1024, 2), col(512, 6), col(128, 28), rowblk(SSM_INNER), rowblk(512),
                  rowblk(SSM_INNER),
                  pl.BlockSpec((SSD_STEP // CHUNK, 128, SSM_INNER), lambda c: (ch(c), 0, 0)),
                  rowblk(SSM_INNER),
                  full(cwx), full(cwb), full(dtb), full(alog), full(dsk_e), full(norm_g),
                  full(tri), full(triu), full(expand), full(expand_t)],
        out_specs=[rowblk(SSM_INNER), rowblk(SSM_INNER), rowblk(512), rowblk(128),
                   acc(1, 1024), acc(1, 1024), acc(1, 128), acc(1, 128), acc(4, 1024), acc(1, 1024), acc(4, 512),
                   acc(1, 512)],
        out_shape=[SDS((t, SSM_INNER), BF16), SDS((t, SSM_INNER), BF16), SDS((t, 512), BF16), SDS((t, 128), BF16),
                   SDS((1, 1024), F32), SDS((1, 1024), F32), SDS((1, 128), F32), SDS((1, 128), F32),
                   SDS((4, 1024), F32), SDS((1, 1024), F32), SDS((4, 512), F32), SDS((1, 512), F32)],
        scratch_shapes=[pltpu.VMEM((128, SSM_INNER), F32), pltpu.VMEM((8, 1024), F32), pltpu.VMEM((8, 512), F32),
                        pltpu.VMEM((CHUNK, 512), F32), pltpu.VMEM((CHUNK, SSM_INNER), F32)],
        compiler_params=_params("arbitrary"))(proj, proj, proj, proj, pre_x, pre_b, y_ssd, hs, dout,
                                              cwx, cwb, dtb, alog, dsk_e, norm_g, tri, triu, expand, expand_t)


def _conv3(x, halo, w_ref, b_ref, part):
    acc = b_ref[part] + w_ref[2, part] * x
    for k in range(2):
        acc = acc + w_ref[k, part] * _shift_down(x, halo, 2 - k)
    return acc


def _up_act(x, g, w_up_t, cw, cb, tm=2048, tn=256, tr=512):
    t, k = x.shape
    nj = D_FF // tn

    def body(x_ref, g_ref, wg_ref, wv_ref, w_ref, b_ref, u_ref, c_ref, h_ref, f_ref, halo):
        i, j = pl.program_id(0), pl.program_id(1)

        @pl.when(j == 0)
        def _():
            xv = x_ref[...]
            r = lax.rsqrt(jnp.mean(xv * xv, axis=-1, keepdims=True) + EPS)
            h_ref[...] = (xv * r * g_ref[...]).astype(BF16)

        @pl.when(i == 0)
        def _():
            halo[j] = jnp.zeros((2, 8, tn), F32)

        def matmuls(r):
            rows = slice(r * tr, (r + 1) * tr)
            return [_dot(h_ref[rows, :], wt_ref[...], NT) for wt_ref in (wg_ref, wv_ref)]

        def epilogue(r, us, before):
            rows = slice(r * tr, (r + 1) * tr)
            parts = []
            for part, u in enumerate(us):
                u_ref[part, rows, :] = u.astype(BF16)
                parts.append(_conv3(u, before[part], w_ref, b_ref, part))
                c_ref[part, rows, :] = parts[-1].astype(BF16)
            gate, val = parts
            f_ref[rows, :] = (gate * _sigmoid(gate) * val).astype(BF16)
            return [u[tr - 8:, :] for u in us]

        before = [halo[j, 0], halo[j, 1]]
        pending = matmuls(0)
        for r in range(tm // tr):
            ahead = matmuls(r + 1) if r + 1 < tm // tr else None
            before = epilogue(r, pending, before)
            pending = ahead
        halo[j, 0], halo[j, 1] = before

    return pl.pallas_call(
        body, name="up_proj", grid=(t // tm, nj),
        in_specs=[pl.BlockSpec((tm, k), lambda i, j: (i, 0)), pl.BlockSpec((1, k), lambda i, j: (0, 0)),
                  pl.BlockSpec((tn, k), lambda i, j: (j, 0)), pl.BlockSpec((tn, k), lambda i, j: (j + nj, 0)),
                  pl.BlockSpec((3, 2, 1, tn), lambda i, j: (0, 0, 0, j)), pl.BlockSpec((2, 1, tn), lambda i, j: (0, 0, j))],
        out_specs=[pl.BlockSpec((2, tm, tn), lambda i, j: (0, i, j)), pl.BlockSpec((2, tm, tn), lambda i, j: (0, i, j)),
                   pl.BlockSpec((tm, k), lambda i, j: (i, 0)), pl.BlockSpec((tm, tn), lambda i, j: (i, j))],
        out_shape=[SDS((2, t, D_FF), BF16), SDS((2, t, D_FF), BF16), SDS((t, k), BF16), SDS((t, D_FF), BF16)],
        scratch_shapes=[pltpu.VMEM((nj, 2, 8, tn), F32)],
        compiler_params=_params("arbitrary", "arbitrary"))(x, g, w_up_t, w_up_t, cw, cb)


def _ffn_bwd(dx2, w_down, u, c, cw, tm=512, tn=1408):
    t = u.shape[1]
    nj, ni = D_FF // tn, t // tm
    rev = lambda i: ni - 1 - i

    def body(dx_ref, wd_ref, u_ref, c_ref, w_ref, du_ref, dcw_ref, dcb_ref, nxt):
        i = pl.program_id(1)

        @pl.when(i == 0)
        def _():
            nxt[...] = jnp.zeros_like(nxt)
            dcw_ref[...] = jnp.zeros_like(dcw_ref)
            dcb_ref[...] = jnp.zeros_like(dcb_ref)

        df = _dot(dx_ref[...].astype(BF16), wd_ref[...], NT)
        gate, val = c_ref[0].astype(F32), c_ref[1].astype(F32)
        sg = _sigmoid(gate)
        dgate = df * val * (sg * (1.0 + gate * (1.0 - sg)))
        dval = df * (gate * sg)
        for part, d in enumerate((dgate, dval)):
            uu = u_ref[part].astype(F32)
            dcb_ref[part] += jnp.sum(d, axis=0, keepdims=True)
            ahead = nxt[part]
            acc = w_ref[2, part] * d
            dcw_ref[2, part] += jnp.sum(d * uu, axis=0, keepdims=True)
            for k in range(2):
                d_up = _shift_up(d, ahead, 2 - k)
                dcw_ref[k, part] += jnp.sum(uu * d_up, axis=0, keepdims=True)
                acc = acc + w_ref[k, part] * d_up
            nxt[part] = d[0:8, :]
            du_ref[part] = acc.astype(BF16)

    w_spec = pl.BlockSpec((3, 2, 1, tn), lambda j, i: (0, 0, 0, j))
    b_spec = pl.BlockSpec((2, 1, tn), lambda j, i: (0, 0, j))
    tile = pl.BlockSpec((2, tm, tn), lambda j, i: (0, rev(i), j))
    return pl.pallas_call(
        body, name="ffn_bwd", grid=(nj, ni),
        in_specs=[pl.BlockSpec((tm, D_MODEL), lambda j, i: (rev(i), 0)), pl.BlockSpec((tn, D_MODEL), lambda j, i: (j, 0)),
                  tile, tile, w_spec],
        out_specs=[tile, w_spec, b_spec],
        out_shape=[SDS((2, t, D_FF), BF16), SDS((3, 2, 1, D_FF), F32), SDS((2, 1, D_FF), F32)],
        scratch_shapes=[pltpu.VMEM((2, 8, tn), F32)],
        compiler_params=_params("parallel", "arbitrary"))(dx2, w_down, u, c, cw)


def _down_ple_loss(x1, f, w_down, g, w_gate, p, w_proj_t, target, tm=512):
    t = x1.shape[0]

    def body(x_ref, f_ref, wd_ref, g_ref, wg_ref, p_ref, wp_ref, tg_ref, dx_ref, dpre_ref, dpp_ref, h_ref, loss_ref,
             dg_ref):
        i = pl.program_id(0)
        xv = x_ref[...] + _dot(f_ref[...], wd_ref[...])
        r = lax.rsqrt(jnp.mean(xv * xv, axis=-1, keepdims=True) + EPS)
        xh = xv * r
        h = (xh * g_ref[...]).astype(BF16)
        h_ref[...] = h
        gate = _sigmoid(_dot(h, wg_ref[...]))
        pp = _dot(p_ref[...].astype(BF16), wp_ref[...], NT)
        err = (xv + gate * pp) - tg_ref[...]

        @pl.when(i == 0)
        def _():
            loss_ref[...] = jnp.zeros_like(loss_ref)
            dg_ref[...] = jnp.zeros_like(dg_ref)

        loss_ref[...] += 0.5 * jnp.sum(jnp.mean(err * err, axis=-1, keepdims=True), axis=0, keepdims=True)
        dy = err * (1.0 / D_MODEL)
        dpre = (dy * pp * gate * (1.0 - gate)).astype(BF16)
        dpre_ref[...] = dpre
        dpp_ref[...] = (dy * gate).astype(BF16)
        dh = _dot(dpre, wg_ref[...], NT)
        dg_ref[...] += jnp.sum(dh * xh, axis=0, keepdims=True)
        gd = dh * g_ref[...]
        dx_ref[...] = dy + r * (gd - xh * jnp.mean(gd * xh, axis=-1, keepdims=True))

    row = lambda w: pl.BlockSpec((tm, w), lambda i: (i, 0))
    full = lambda a: pl.BlockSpec(a.shape, lambda i: (0, 0))
    return pl.pallas_call(
        body, name="down_ple_loss", grid=(t // tm,),
        in_specs=[row(D_MODEL), row(D_FF), full(w_down), full(g), full(w_gate), row(PLE_DIM), full(w_proj_t),
                  row(D_MODEL)],
        out_specs=[row(D_MODEL), row(D_MODEL), row(D_MODEL), row(D_MODEL),
                   pl.BlockSpec((1, 128), lambda i: (0, 0)), pl.BlockSpec((1, D_MODEL), lambda i: (0, 0))],
        out_shape=[SDS((t, D_MODEL), F32), SDS((t, D_MODEL), BF16), SDS((t, D_MODEL), BF16), SDS((t, D_MODEL), BF16),
                   SDS((1, 128), F32), SDS((1, D_MODEL), F32)],
        compiler_params=_params("arbitrary"))(x1, f, w_down, g, w_gate, p, w_proj_t, target)


def _all_gather(arrays, name):
    n_a = len(arrays)

    def body(*refs):
        src, dst = refs[:n_a], refs[n_a:2 * n_a]
        send_sems, recv_sems, local_sems = refs[2 * n_a:]
        x, y, c = lax.axis_index("x"), lax.axis_index("y"), lax.axis_index("c")
        slot = lambda px, py, pc: 4 * px + 2 * py + pc
        me, sibling = (x, y, c), (x, y, 1 - c)
        chips = [(1 - x, y), (x, 1 - y), (1 - x, 1 - y)]

        def copy(a, k, block, to, own=False):
            return pltpu.make_async_remote_copy(
                src_ref=src[a] if own else dst[a].at[slot(*block)], dst_ref=dst[a].at[slot(*block)],
                send_sem=send_sems.at[a, k], recv_sem=recv_sems.at[a, k], device_id=to,
                device_id_type=pl.DeviceIdType.MESH)

        local = [pltpu.make_async_copy(src[a], dst[a].at[slot(*me)], local_sems.at[a]) for a in range(n_a)]
        for cp in local:
            cp.start()
        sends = []
        for a in range(n_a):
            sends.append(copy(a, 0, me, sibling, own=True))
            sends += [copy(a, 1 + j, me, (*chip, c), own=True) for j, chip in enumerate(chips)]
        for cp in sends:
            cp.start()
        for j, chip in enumerate(chips):
            for a in range(n_a):
                copy(a, 1 + j, (*chip, c), me).wait_recv()
                passed = copy(a, 4 + j, (*chip, c), sibling)
                passed.start()
                sends.append(passed)
        for a in range(n_a):
            copy(a, 0, sibling, me).wait_recv()
            for j, chip in enumerate(chips):
                copy(a, 4 + j, (*chip, 1 - c), me).wait_recv()
        for cp in sends:
            cp.wait_send()
        for cp in local:
            cp.wait()

    hbm = pl.BlockSpec(memory_space=pl.ANY)
    return pl.pallas_call(
        body, name=name, in_specs=[hbm] * n_a, out_specs=[hbm] * n_a,
        out_shape=[SDS((N_DEV,) + a.shape, a.dtype) for a in arrays],
        scratch_shapes=[pltpu.SemaphoreType.DMA((n_a, N_DEV - 1)), pltpu.SemaphoreType.DMA((n_a, N_DEV - 1)),
                        pltpu.SemaphoreType.DMA((n_a,))],
        )(*arrays)


def _peer(k):
    x, y, c = lax.axis_index("x"), lax.axis_index("y"), lax.axis_index("c")
    px = 1 - x if k & 4 else x
    py = 1 - y if k & 2 else y
    pc = 1 - c if k & 1 else c
    return (px, py, pc), 4 * px + 2 * py + pc


_HBM = pl.BlockSpec(memory_space=pltpu.HBM)
_SEM = pl.BlockSpec(memory_space=pltpu.SEMAPHORE)


def _split_copies(src, land, send_sems, recv_sems, scatter, arrivals):
    _, me = _peer(0)
    out = []
    for k in range(1, N_DEV):
        coords, peer = _peer(k)
        for a in range(len(src)):
            sem = a * (N_DEV - 1) + k - 1
            if scatter[a]:
                s, d = src[a].at[peer], land[a].at[k]
            else:
                s, d = src[a], land[a].at[peer if arrivals else me]
            out.append(pltpu.make_async_remote_copy(
                src_ref=s, dst_ref=d, send_sem=send_sems.at[sem], recv_sem=recv_sems.at[sem], device_id=coords,
                device_id_type=pl.DeviceIdType.MESH))
    return out


def _exchange_start(srcs, lands, scatter, name):
    n = len(srcs)

    def body(*refs):
        src, land = refs[:n], refs[n:2 * n]
        send_sems, recv_sems = refs[2 * n], refs[2 * n + 1]
        token = refs[-1]
        for cp in _split_copies(src, land, send_sems, recv_sems, scatter, False):
            cp.start()
        token[...] = jnp.zeros_like(token)

    hbm_shape = lambda a: pltpu.HBM(a.shape, a.dtype)
    sem_shape = pltpu.SemaphoreType.DMA((n * (N_DEV - 1),))
    outs = pl.pallas_call(
        body, name=name,
        out_shape=(sem_shape, sem_shape, *[hbm_shape(a) for a in srcs], *[hbm_shape(a) for a in lands],
                   SDS((8, 128), F32)),
        in_specs=[_HBM] * (2 * n), out_specs=(_SEM, _SEM, *[_HBM] * (2 * n), pl.BlockSpec(memory_space=pltpu.VMEM)),
        input_output_aliases={a: 2 + a for a in range(2 * n)},
        compiler_params=pltpu.CompilerParams(has_side_effects=pltpu.SideEffectType.DATAFLOW_SIDE_EFFECTING),
    )(*[pltpu.with_memory_space_constraint(a, pltpu.HBM) for a in list(srcs) + list(lands)])
    return outs[0], outs[1], outs[2:2 + n], outs[2 + n:2 + 2 * n], outs[-1]


def _exchange_wait(send_sems, recv_sems, srcs, lands, scatter, after, name):
    n = len(srcs)

    def body(*refs):
        src, land = refs[:n], refs[n:2 * n]
        for cp in _split_copies(src, land, refs[2 * n], refs[2 * n + 1], scatter, False):
            cp.wait_send()
        for cp in _split_copies(src, land, refs[2 * n], refs[2 * n + 1], scatter, True):
            cp.wait_recv()

    hbm_shape = lambda a: pltpu.HBM(a.shape, a.dtype)
    outs = pl.pallas_call(
        body, name=name, out_shape=tuple(hbm_shape(a) for a in list(srcs) + list(lands)),
        in_specs=[_HBM] * (2 * n) + [_SEM, _SEM, pl.BlockSpec(memory_space=pl.ANY)], out_specs=(_HBM,) * (2 * n),
        input_output_aliases={a: a for a in range(2 * n)},
        compiler_params=pltpu.CompilerParams(has_side_effects=pltpu.SideEffectType.DATAFLOW_SIDE_EFFECTING),
    )(*srcs, *lands, send_sems, recv_sems, after)
    return outs[:n], outs[n:]


def _reduce8(a, tr, name):
    _, rows, cols = a.shape

    def body(a_ref, o_ref):
        acc = a_ref[0]
        for j in range(1, N_DEV):
            acc = acc + a_ref[j]
        o_ref[...] = acc

    return pl.pallas_call(
        body, name=name, grid=(rows // tr,),
        in_specs=[pl.BlockSpec((N_DEV, tr, cols), lambda i: (0, i, 0))],
        out_specs=pl.BlockSpec((tr, cols), lambda i: (i, 0)), out_shape=SDS((rows, cols), F32),
        compiler_params=_params("parallel"))(a)


def _reduce_landed(own, land, name, tc=256):
    rows, cols = own.shape

    def body(own_ref, land_ref, o_ref):
        acc = own_ref[...].astype(F32)
        for k in range(1, N_DEV):
            acc = acc + land_ref[k].astype(F32)
        o_ref[...] = acc

    return pl.pallas_call(
        body, name=name, grid=(cols // tc,),
        in_specs=[pl.BlockSpec((rows, tc), lambda j: (0, j)), pl.BlockSpec((N_DEV, rows, tc), lambda j: (0, 0, j))],
        out_specs=pl.BlockSpec((rows, tc), lambda j: (0, j)), out_shape=SDS((rows, cols), F32),
        compiler_params=_params("parallel"))(own, land)


def _reduce_adamw(own, land, w, m, v, name, tc=256):
    rows, cols = own.shape

    def body(own_ref, land_ref, w_ref, m_ref, v_ref, g_ref, d_ref, mo_ref, vo_ref):
        g = own_ref[...].astype(F32)
        for k in range(1, N_DEV):
            g = g + land_ref[k].astype(F32)
        g_ref[...] = g
        d_ref[...], mo_ref[...], vo_ref[...] = _adam_update(w_ref[...], g, m_ref[...], v_ref[...])

    blk = pl.BlockSpec((rows, tc), lambda j: (0, j))
    return pl.pallas_call(
        body, name=name, grid=(cols // tc,),
        in_specs=[blk, pl.BlockSpec((N_DEV, rows, tc), lambda j: (0, 0, j)), blk, blk, blk], out_specs=[blk] * 4,
        out_shape=[SDS((rows, cols), F32)] * 4, compiler_params=_params("parallel"))(own, land, w, m, v)


def _adamw(w, g, m, v, name, tr=None):
    rows, cols = w.shape
    tr = rows if tr is None else tr

    def body(w_ref, g_ref, m_ref, v_ref, d_ref, mo_ref, vo_ref):
        d_ref[...], mo_ref[...], vo_ref[...] = _adam_update(w_ref[...], g_ref[...], m_ref[...], v_ref[...])

    blk = pl.BlockSpec((tr, cols), lambda i: (i, 0))
    return pl.pallas_call(
        body, name=name, grid=(rows // tr,), in_specs=[blk] * 4, out_specs=[blk] * 3,
        out_shape=[SDS((rows, cols), F32)] * 3, compiler_params=_params("parallel"))(w, g, m, v)


def _pad_rows(a, rows):
    return jnp.pad(a, ((0, rows - a.shape[0]),) + ((0, 0),) * (a.ndim - 1))


def _local_step(x, p, target, sm, wts, fetch_rest, send, tok):
    ones_q, ones_k, dup, dup_t = _head_consts()
    tri, triu, expand, expand_t = _ssd_consts()
    w_in_t = wts["in_t"]
    cwx, cwb = wts["ssm_cw"][:, :SSM_INNER], wts["ssm_cw"][:, SSM_INNER:]
    cbx, cbb = sm["ssm_conv_b"][:, :SSM_INNER], sm["ssm_conv_b"][:, SSM_INNER:]
    pad128 = lambda a: jnp.pad(a, ((0, 0), (0, 128 - a.shape[1])))
    dtb, alog = pad128(sm["dt_bias"]), pad128(sm["a_log"])
    dsk_e = jnp.repeat(sm["d_skip"], HEAD_DIM, axis=1)
    gq = jnp.tile(sm["q_norm_g"], (1, ATTN_DIM // HEAD_DIM))
    gk = jnp.tile(sm["k_norm_g"], (1, KV_DIM // HEAD_DIM))
    ffn_cw = wts["ffn_cw"].reshape(3, 2, 1, D_FF)
    ffn_cb = sm["ffn_conv_b"].reshape(2, 1, D_FF)

    proj, h1 = _norm_matmul(x, sm["attn_norm_g"] + tok, w_in_t, 1024, 1280, "in_proj")
    qn, kd, vd = _attn_prep(proj, gq, gk, ones_q, ones_k, dup)
    attn_out, lse = _attn_fwd(qn, kd, vd)
    y_ssd, hs, ssm_out, pre_x, pre_b = _ssd_fwd(proj, cwx, cbx, cwb, cbb, dtb, alog, dsk_e, sm["ssm_norm_g"], tri,
                                                expand)
    rest = fetch_rest(ssm_out)
    w_out, w_up_t, w_down, w_gate, w_proj_t = (rest[k] for k in ("out", "up_t", "down", "gate", "proj_t"))
    x1 = _mm_resid([(attn_out, None, w_out[:ATTN_DIM]), (ssm_out, None, w_out[ATTN_DIM:])], x, None, 1024, F32,
                   "out_proj")
    u, uc, h2, f = _up_act(x1, sm["ffn_norm_g"], w_up_t, ffn_cw, ffn_cb)
    dx2, dpre, dpp, h3, loss, dg_ple = _down_ple_loss(x1, f, w_down, sm["ple_norm_g"], w_gate, p, w_proj_t, target)

    g_gate = _wgrad(h3, None, dpre, "wg_gate")
    g_proj_t = _wgrad(dpp, None, p, "wg_proj")
    g_down = _wgrad(f, None, dx2, "wg_down")
    du, d_ffn_cw, d_ffn_cb = _ffn_bwd(dx2, w_down, u, uc, ffn_cw)
    dx1, dg_ffn = _mm_normbwd([(du, 0, w_up_t, D_FF, 0), (du, 1, w_up_t, D_FF, 1)], x1, sm["ffn_norm_g"], dx2, 512,
                              "up_proj_bwd")
    g_up_t = _wgrad(du, "all", h2, "wg_up")
    tok = send(dict(gate=g_gate, proj_t=g_proj_t, down=g_down, up_t=g_up_t)).astype(BF16)
    d_attn, d_ssm = _out_proj_bwd(dx1, w_out + tok)
    g_out = _wgrad_multi([attn_out, ssm_out], dx1, "wg_out")
    tok = send(dict(out=g_out))
    (dz, dxs, dbc, ddt, dg_ssm, d_dsk_e, d_alog, d_dtb, d_cwx, d_cbx, d_cwb, d_cbb) = _ssd_bwd(
        proj, pre_x, pre_b, y_ssd, hs, d_ssm, cwx, cwb, dtb + tok, alog, dsk_e, sm["ssm_norm_g"], tri, triu, expand,
        expand_t)
    dqn, dkc, dkp, dvc, dvp = _attn_bwd(qn, kd, vd, attn_out, lse, d_attn, ones_k[:128, :128])
    dqkv, dgq, dgk = _attn_prep_bwd(proj, dqn, dkc, dkp, dvc, dvp, gq + tok, gk, ones_q, ones_k, dup_t)
    pieces = [(dqkv, 0, 1024), (dz, 1024, 2048), (dxs, 2048, 3072), (dbc, 3072, 3584), (ddt, 3584, 3712)]
    g_in_t = jnp.concatenate([_wgrad_multi([dqkv, dz], h1, "wg_in_qkvz"),
                              _wgrad_multi([dxs, dbc, ddt], h1, "wg_in_xbcdt")], axis=0)[:IN_PROJ]
    tok = send(dict(in_t=g_in_t))
    grad_x, dg_attn = _mm_normbwd([(a, None, w_in_t, hi - lo, lo // (hi - lo)) for a, lo, hi in pieces], x,
                                  sm["attn_norm_g"] + tok, dx1, 512, "in_proj_bwd")

    small = dict(
        attn_norm_g=dg_attn, q_norm_g=dgq.reshape(-1, HEAD_DIM).sum(0, keepdims=True),
        k_norm_g=dgk.reshape(-1, HEAD_DIM).sum(0, keepdims=True),
        ssm_conv_w=jnp.concatenate([d_cwx, d_cwb], axis=1), ssm_conv_b=jnp.concatenate([d_cbx, d_cbb], axis=1),
        dt_bias=d_dtb[:, :SSM_HEADS], a_log=d_alog[:, :SSM_HEADS],
        d_skip=d_dsk_e.reshape(SSM_HEADS, HEAD_DIM).sum(1)[None, :], ssm_norm_g=dg_ssm, ffn_norm_g=dg_ffn,
        ffn_conv_w=d_ffn_cw.reshape(3, 2 * D_FF), ffn_conv_b=d_ffn_cb.reshape(1, 2 * D_FF), ple_norm_g=dg_ple)
    return loss[0, 0], grad_x, small


_SMALL = (("attn_norm_g", 1, 1024), ("q_norm_g", 1, 64), ("k_norm_g", 1, 64), ("ssm_conv_w", 4, XBC_DIM),
          ("ssm_conv_b", 1, XBC_DIM), ("dt_bias", 1, 16), ("a_log", 1, 16), ("d_skip", 1, 16), ("ssm_norm_g", 1, 1024),
          ("ffn_norm_g", 1, 1024), ("ffn_conv_w", 3, 2 * D_FF), ("ffn_conv_b", 1, 2 * D_FF), ("ple_norm_g", 1, 1024))
_SMALL_ROWS, _SMALL_COLS = 32, XBC_DIM
_SHARDED_SMALL = ("ssm_conv_w", "ffn_conv_w")


def _small_chunks(n):
    return 1 if n <= _SMALL_COLS else 4


def _pack_small(parts, loss):
    rows = []
    for k, r, n in _SMALL:
        c = _small_chunks(n)
        rows.append(jnp.pad(parts[k].reshape(r * c, n // c), ((0, 0), (0, _SMALL_COLS - n // c))))
    packed = _pad_rows(jnp.concatenate(rows, axis=0), _SMALL_ROWS)
    at_loss = ((lax.broadcasted_iota(jnp.int32, packed.shape, 0) == _SMALL_ROWS - 1) &
               (lax.broadcasted_iota(jnp.int32, packed.shape, 1) == 0))
    return jnp.where(at_loss, loss, packed)


def _adam_update(w, g, m, v):
    c1 = 1.0 - ADAM_B1 ** ADAM_STEP
    c2 = 1.0 - ADAM_B2 ** ADAM_STEP
    mn = ADAM_B1 * m + (1.0 - ADAM_B1) * g
    vn = ADAM_B2 * v + (1.0 - ADAM_B2) * (g * g)
    return -ADAM_LR * ((mn / c1) / (jnp.sqrt(vn / c2) + ADAM_EPS) + ADAM_WD * w), mn, vn


def _adamw_small(g_all, g_shard, w, m, v):
    ins, shapes = [g_all], []
    for k, _, _ in _SMALL:
        shape2 = w[k].shape if w[k].ndim == 2 else (1, w[k].shape[0])
        shapes.append(shape2)
        ins += ([g_shard[k]] if k in _SHARDED_SMALL else []) + [a.reshape(shape2) for a in (w[k], m[k], v[k])]

    def body(*refs):
        g_ref, pos, row = refs[0], 1, 0
        outs = refs[len(ins):]
        for i, (k, r, n) in enumerate(_SMALL):
            c = _small_chunks(n)
            if k in _SHARDED_SMALL:
                g = refs[pos][...]
                pos += 1
            elif c == 1:
                g = g_ref[row:row + r, 0:n]
            else:
                g = jnp.concatenate([g_ref[row + j:row + j + 1, 0:n // c] for j in range(c)], axis=1)
            row += r * c
            d, mn, vn = _adam_update(refs[pos][...], g, refs[pos + 1][...], refs[pos + 2][...])
            pos += 3
            for o_ref, val in zip(outs[4 * i:4 * i + 4], (g, d, mn, vn)):
                o_ref[...] = val

    res = pl.pallas_call(body, name="adamw_small",
                         out_shape=[SDS(s, F32) for s in shapes for _ in range(4)])(*ins)
    return {k: tuple(a.reshape(w[k].shape) for a in res[4 * i:4 * i + 4]) for i, (k, _, _) in enumerate(_SMALL)}


def kernel(x, p, attn_norm_g, w_in, q_norm_g, k_norm_g, ssm_conv_w, ssm_conv_b, dt_bias, a_log, d_skip, ssm_norm_g, w_out, ffn_norm_g, w_up, ffn_conv_w, ffn_conv_b, w_down, ple_norm_g, w_ple_gate, w_ple_proj, loss_target, m_attn_norm_g, m_w_in, m_q_norm_g, m_k_norm_g, m_ssm_conv_w, m_ssm_conv_b, m_dt_bias, m_a_log, m_d_skip, m_ssm_norm_g, m_w_out, m_ffn_norm_g, m_w_up, m_ffn_conv_w, m_ffn_conv_b, m_w_down, m_ple_norm_g, m_w_ple_gate, m_w_ple_proj, v_attn_norm_g, v_w_in, v_q_norm_g, v_k_norm_g, v_ssm_conv_w, v_ssm_conv_b, v_dt_bias, v_a_log, v_d_skip, v_ssm_norm_g, v_w_out, v_ffn_norm_g, v_w_up, v_ffn_conv_w, v_ffn_conv_b, v_w_down, v_ple_norm_g, v_w_ple_gate, v_w_ple_proj):
    names = ("attn_norm_g", "w_in", "q_norm_g", "k_norm_g", "ssm_conv_w", "ssm_conv_b", "dt_bias", "a_log", "d_skip",
             "ssm_norm_g", "w_out", "ffn_norm_g", "w_up", "ffn_conv_w", "ffn_conv_b", "w_down", "ple_norm_g",
             "w_ple_gate", "w_ple_proj")
    w = dict(zip(names, (attn_norm_g, w_in, q_norm_g, k_norm_g, ssm_conv_w, ssm_conv_b, dt_bias, a_log, d_skip,
                         ssm_norm_g, w_out, ffn_norm_g, w_up, ffn_conv_w, ffn_conv_b, w_down, ple_norm_g, w_ple_gate,
                         w_ple_proj)))
    m = dict(zip(names, (m_attn_norm_g, m_w_in, m_q_norm_g, m_k_norm_g, m_ssm_conv_w, m_ssm_conv_b, m_dt_bias,
                         m_a_log, m_d_skip, m_ssm_norm_g, m_w_out, m_ffn_norm_g, m_w_up, m_ffn_conv_w, m_ffn_conv_b,
                         m_w_down, m_ple_norm_g, m_w_ple_gate, m_w_ple_proj)))
    v = dict(zip(names, (v_attn_norm_g, v_w_in, v_q_norm_g, v_k_norm_g, v_ssm_conv_w, v_ssm_conv_b, v_dt_bias,
                         v_a_log, v_d_skip, v_ssm_norm_g, v_w_out, v_ffn_norm_g, v_w_up, v_ffn_conv_w, v_ffn_conv_b,
                         v_w_down, v_ple_norm_g, v_w_ple_gate, v_w_ple_proj)))
    w, m, v = ({k: a[0] for k, a in d.items()} for d in (w, m, v))
    me = 4 * lax.axis_index("x") + 2 * lax.axis_index("y") + lax.axis_index("c")

    mine = dict(in_t=w["w_in"].T, out=w["w_out"], up_t=w["w_up"].T, down=w["w_down"], gate=w["w_ple_gate"],
                proj_t=w["w_ple_proj"].T)
    mine = {k: a.astype(BF16) for k, a in mine.items()}
    conv_pack = jnp.pad(jnp.concatenate([w["ssm_conv_w"].reshape(-1), w["ffn_conv_w"].reshape(-1)]),
                        (0, 3072 - 2880)).reshape(8, 384)
    all_in, all_conv = _all_gather([mine["in_t"], conv_pack], "gather_first")
    later = ("out", "up_t", "down", "gate", "proj_t")
    zones = [lax.dynamic_update_slice(lax.empty((N_DEV,) + mine[k].shape, BF16), mine[k][None], (me, 0, 0))
             for k in later]
    zones, all_in, all_conv = lax.optimization_barrier((zones, all_in, all_conv))
    rest_state = _exchange_start([mine[k] for k in later], zones, [False] * len(later), "gather_rest_start")

    def fetch_rest(after):
        _, landed = _exchange_wait(*rest_state[:4], [False] * len(later), after, "gather_rest_wait")
        return {k: a.reshape(N_DEV * a.shape[1], a.shape[2]) for k, a in zip(later, landed)}

    wts = dict(in_t=_pad_rows(all_in.reshape(IN_PROJ, D_MODEL), IN_PROJ_PAD))
    conv_flat = all_conv.reshape(N_DEV, 3072)
    wts["ssm_cw"] = conv_flat[:, :768].reshape(N_DEV, 4, XBC_DIM // N_DEV).transpose(1, 0, 2).reshape(4, XBC_DIM)
    wts["ffn_cw"] = conv_flat[:, 768:2880].reshape(N_DEV, 3, 2 * D_FF // N_DEV).transpose(1, 0, 2).reshape(3, 2 * D_FF)
    sm = {k: w[k].reshape(1, -1) for k, _, _ in _SMALL if k not in _SHARDED_SMALL}

    in_flight = []

    def send(grads):
        keys = sorted(grads)
        srcs = [grads[k].reshape(N_DEV, grads[k].shape[0] // N_DEV, grads[k].shape[1]) for k in keys]
        state = _exchange_start(srcs, [lax.empty(a.shape, BF16) for a in srcs], [True] * len(keys),
                                "send_" + "_".join(keys))
        in_flight.append((keys, state))
        return state[4][0:1, 0:1]

    loss, grad_x, small = _local_step(x[0], p[0, 0], loss_target[0], sm, wts, fetch_rest, send,
                                      rest_state[4][0:1, 0:1])

    (got_small,) = _all_gather([_pack_small(small, loss)], "gather_small_grads")
    g_small = _reduce8(got_small, _SMALL_ROWS, "reduce_small")
    loss = g_small[_SMALL_ROWS - 1, 0]
    grads, gw, delta, new_m, new_v = {}, {}, {}, {}, {}
    row_sharded = {"out": "w_out", "down": "w_down", "gate": "w_ple_gate"}
    for keys, state in in_flight:
        sent, landed = _exchange_wait(*state[:4], [True] * len(keys), grad_x, "wait_" + "_".join(keys))
        for k, shares, land in zip(keys, sent, landed):
            own = lax.dynamic_index_in_dim(shares, me, 0, keepdims=False)
            if k in row_sharded:
                n = row_sharded[k]
                gw[n], delta[n], new_m[n], new_v[n] = _reduce_adamw(own, land, w[n], m[n], v[n], "update_" + n)
            else:
                grads[k] = _reduce_landed(own, land, "reduce_" + k)
    gw.update({"w_in": grads["in_t"].T, "w_up": grads["up_t"].T, "w_ple_proj": grads["proj_t"].T})
    n_ssm, n_ffn = XBC_DIM // N_DEV, 2 * D_FF // N_DEV
    g_shard = {"ssm_conv_w": lax.dynamic_slice(g_small, (3, me * n_ssm), (4, n_ssm)),
               "ffn_conv_w": lax.dynamic_slice(g_small[13:25, :2 * D_FF // 4].reshape(3, 2 * D_FF), (0, me * n_ffn),
                                               (3, n_ffn))}

    for k, tr in (("w_in", 256), ("w_up", 256), ("w_ple_proj", None)):
        delta[k], new_m[k], new_v[k] = _adamw(w[k], gw[k], m[k], v[k], "adamw_" + k, tr)
    for k, (g_k, d_k, m_k, v_k) in _adamw_small(g_small, g_shard, w, m, v).items():
        gw[k], delta[k], new_m[k], new_v[k] = g_k, d_k, m_k, v_k

    outs = [loss, grad_x[None]]
    for d in (gw, delta, new_m, new_v):
        outs += [d[k][None] for k in names]
    return tuple(outs)
```

```python
import functools

import numpy as np
import jax
import jax.numpy as jnp
from jax import lax
from jax.experimental import pallas as pl
from jax.experimental.pallas import tpu as pltpu

F32 = jnp.float32
BF16 = jnp.bfloat16
SDS = jax.ShapeDtypeStruct
EPS = 1e-6
N_DEV = 8
D_MODEL = 1024
HEAD_DIM = 64
ATTN_DIM = 512
KV_DIM = 256
SSM_INNER = 1024
SSM_HEADS = 16
BC_DIM = 256
XBC_DIM = SSM_INNER + 2 * BC_DIM
MIX_DIM = ATTN_DIM + SSM_INNER
IN_PROJ = 3600
IN_PROJ_PAD = 3840
D_FF = 2816
PLE_DIM = 256
CHUNK = 128
SUPER = 2048
DILATIONS = (1, 4, 16)
TILE_UNROLL = 8
VMEM_LIMIT = 56 * 1024 * 1024
ADAM_LR, ADAM_B1, ADAM_B2, ADAM_EPS, ADAM_WD, ADAM_STEP = 0.001, 0.9, 0.999, 1e-08, 0.01, 10

NT = (((1,), (1,)), ((), ()))
TN = (((0,), (0,)), ((), ()))


def _params(*sem):
    return pltpu.CompilerParams(dimension_semantics=sem if sem else None, vmem_limit_bytes=VMEM_LIMIT)


def _dot(a, b, dims=None):
    if dims is None:
        return jnp.dot(a, b, preferred_element_type=F32)
    return lax.dot_general(a, b, dims, preferred_element_type=F32)


def _hdot(a, b, parts=2):
    a_exact = a.dtype == BF16
    x = b if a_exact else a
    acc = None
    for _ in range(parts):
        piece = x.astype(BF16)
        x = x - piece.astype(F32)
        d = _dot(a, piece) if a_exact else _dot(piece, b)
        acc = d if acc is None else acc + d
    return acc


def _sigmoid(x):
    return 0.5 * jnp.tanh(0.5 * x) + 0.5


def _shift_down(x, halo8, s):
    xr = pltpu.roll(x, s, 0)
    row = lax.broadcasted_iota(jnp.int32, halo8.shape, 0)
    first = jnp.where(row < s, pltpu.roll(halo8, s, 0), xr[0:8])
    return jnp.concatenate([first, xr[8:]], axis=0)


def _shift_up(x, halo8, s):
    n = x.shape[0]
    xr = pltpu.roll(x, n - s, 0)
    row = lax.broadcasted_iota(jnp.int32, halo8.shape, 0)
    last = jnp.where(row >= 8 - s, pltpu.roll(halo8, 8 - s, 0), xr[n - 8:])
    return jnp.concatenate([xr[:n - 8], last], axis=0)


def _norm_matmul(x, g, wt, tm, tn, name):
    t, k = x.shape
    n = wt.shape[0]

    def body(x_ref, g_ref, w_ref, o_ref, h_ref):
        @pl.when(pl.program_id(1) == 0)
        def _():
            xv = x_ref[...]
            r = lax.rsqrt(jnp.mean(xv * xv, axis=-1, keepdims=True) + EPS)
            h_ref[...] = (xv * r * g_ref[...]).astype(BF16)
        o_ref[...] = _dot(h_ref[...], w_ref[...], NT)

    return pl.pallas_call(
        body, name=name, grid=(t // tm, n // tn),
        in_specs=[pl.BlockSpec((tm, k), lambda i, j: (i, 0)), pl.BlockSpec((1, k), lambda i, j: (0, 0)),
                  pl.BlockSpec((tn, k), lambda i, j: (j, 0))],
        out_specs=[pl.BlockSpec((tm, tn), lambda i, j: (i, j)), pl.BlockSpec((tm, k), lambda i, j: (i, 0))],
        out_shape=[SDS((t, n), F32), SDS((t, k), BF16)],
        compiler_params=_params("parallel", "arbitrary"))(x, g, wt)


def _a_spec(a, lead, tm):
    if lead is None:
        return pl.BlockSpec((tm, a.shape[-1]), lambda i: (i, 0))
    return pl.BlockSpec((None, tm, a.shape[-1]), lambda i, _l=lead: (_l, i, 0))


def _mm_resid(pairs, res, dims, tm, out_dtype, name):
    t = pairs[0][0].shape[-2]
    n = pairs[0][2].shape[1] if dims is None else pairs[0][2].shape[0]
    np_ = len(pairs)

    def body(*refs):
        o_ref = refs[-1]
        acc = refs[2 * np_][...] if res is not None else None
        for q in range(np_):
            d = _dot(refs[q][...].astype(BF16), refs[np_ + q][...], dims)
            acc = d if acc is None else acc + d
        o_ref[...] = acc.astype(out_dtype)

    in_specs = [_a_spec(a, lead, tm) for a, lead, _ in pairs]
    in_specs += [pl.BlockSpec(b.shape, lambda i: (0, 0)) for _, _, b in pairs]
    args = [a for a, _, _ in pairs] + [b for _, _, b in pairs]
    if res is not None:
        in_specs.append(pl.BlockSpec((tm, n), lambda i: (i, 0)))
        args.append(res)
    return pl.pallas_call(
        body, name=name, grid=(t // tm,), in_specs=in_specs,
        out_specs=pl.BlockSpec((tm, n), lambda i: (i, 0)), out_shape=SDS((t, n), out_dtype),
        compiler_params=_params("parallel"))(*args)


def _wgrad_multi(parts, b, name, tk=2048):
    t, n = b.shape
    widths = [a.shape[1] for a in parts]
    m = sum(widths)

    def body(*refs):
        b_ref, o_ref, acc = refs[len(parts):]

        @pl.when(pl.program_id(0) == 0)
        def _():
            acc[...] = jnp.zeros_like(acc)
        bv = b_ref[...].astype(BF16)
        row = 0
        for a_ref, w in zip(refs, widths):
            acc[row:row + w, :] += _dot(a_ref[...].astype(BF16), bv, TN)
            row += w

        @pl.when(pl.program_id(0) == pl.num_programs(0) - 1)
        def _():
            o_ref[...] = acc[...].astype(BF16)

    return pl.pallas_call(
        body, name=name, grid=(t // tk,),
        in_specs=[pl.BlockSpec((tk, w), lambda k: (k, 0)) for w in widths] + [pl.BlockSpec((tk, n), lambda k: (k, 0))],
        out_specs=pl.BlockSpec((m, n), lambda k: (0, 0)), out_shape=SDS((m, n), BF16),
        scratch_shapes=[pltpu.VMEM((m, n), F32)],
        compiler_params=_params("arbitrary"))(*parts, b)


def _out_proj_bwd(dx, w_out, tm=1024):
    t = dx.shape[0]

    def body(dx_ref, w_ref, da_ref, ds_ref):
        a = dx_ref[...].astype(BF16)
        da_ref[...] = _dot(a, w_ref[0:ATTN_DIM, :], NT)
        ds_ref[...] = _dot(a, w_ref[ATTN_DIM:, :], NT)

    return pl.pallas_call(
        body, name="out_proj_bwd", grid=(t // tm,),
        in_specs=[pl.BlockSpec((tm, D_MODEL), lambda i: (i, 0)), pl.BlockSpec(w_out.shape, lambda i: (0, 0))],
        out_specs=[pl.BlockSpec((tm, ATTN_DIM), lambda i: (i, 0)), pl.BlockSpec((tm, SSM_INNER), lambda i: (i, 0))],
        out_shape=[SDS((t, ATTN_DIM), F32), SDS((t, SSM_INNER), F32)],
        compiler_params=_params("parallel"))(dx, w_out)


def _mm_normbwd(pairs, x, g, dres, tm, name):
    t, k = x.shape
    np_ = len(pairs)
    b_specs = [pl.BlockSpec((rows, b.shape[1]), lambda i, _b=blk: (_b, 0)) for _, _, b, rows, blk in pairs]
    pairs = [(a, lead, b) for a, lead, b, _, _ in pairs]

    def body(*refs):
        x_ref, g_ref, dres_ref, dx_ref, dg_ref = refs[2 * np_:]
        dh = None
        for q in range(np_):
            d = _dot(refs[q][...], refs[np_ + q][...])
            dh = d if dh is None else dh + d
        xv = x_ref[...]
        r = lax.rsqrt(jnp.mean(xv * xv, axis=-1, keepdims=True) + EPS)
        xh = xv * r

        @pl.when(pl.program_id(0) == 0)
        def _():
            dg_ref[...] = jnp.zeros_like(dg_ref)
        dg_ref[...] += jnp.sum(dh * xh, axis=0, keepdims=True)
        gd = dh * g_ref[...]
        dx_ref[...] = dres_ref[...] + r * (gd - xh * jnp.mean(gd * xh, axis=-1, keepdims=True))

    in_specs = [_a_spec(a, lead, tm) for a, lead, _ in pairs] + b_specs
    in_specs += [pl.BlockSpec((tm, k), lambda i: (i, 0)), pl.BlockSpec((1, k), lambda i: (0, 0)),
                 pl.BlockSpec((tm, k), lambda i: (i, 0))]
    args = [a for a, _, _ in pairs] + [b for _, _, b in pairs] + [x, g, dres]
    return pl.pallas_call(
        body, name=name, grid=(t // tm,), in_specs=in_specs,
        out_specs=[pl.BlockSpec((tm, k), lambda i: (i, 0)), pl.BlockSpec((1, k), lambda i: (0, 0))],
        out_shape=[SDS((t, k), F32), SDS((1, k), F32)],
        compiler_params=_params("arbitrary"))(*args)


def _wgrad(a, a_lead, b, name, tk=2048):
    t, m = a.shape[-2:]
    n = b.shape[1]
    tm = m if m <= 1024 else 1408
    assert m % tm == 0

    def body(a_ref, b_ref, o_ref, acc):
        @pl.when(pl.program_id(1) == 0)
        def _():
            acc[...] = jnp.zeros_like(acc)
        acc[...] += _dot(a_ref[...].astype(BF16), b_ref[...].astype(BF16), TN)

        @pl.when(pl.program_id(1) == pl.num_programs(1) - 1)
        def _():
            o_ref[...] = acc[...].astype(BF16)

    per, lead = m // tm, 1
    if a_lead == "all":
        lead = a.shape[0]
        a_spec = pl.BlockSpec((None, tk, tm), lambda mi, ki: (mi // per, ki, mi % per))
    elif a_lead is None:
        a_spec = pl.BlockSpec((tk, tm), lambda mi, ki: (ki, mi))
    else:
        a_spec = pl.BlockSpec((None, tk, tm), lambda mi, ki, _l=a_lead: (_l, ki, mi))
    return pl.pallas_call(
        body, name=name, grid=(lead * per, t // tk),
        in_specs=[a_spec, pl.BlockSpec((tk, n), lambda mi, ki: (ki, 0))],
        out_specs=pl.BlockSpec((tm, n), lambda mi, ki: (mi, 0)), out_shape=SDS((lead * m, n), BF16),
        scratch_shapes=[pltpu.VMEM((tm, n), F32)],
        compiler_params=_params("parallel", "arbitrary"))(a, b)


def _head_consts():
    iq = np.arange(ATTN_DIM)
    ik = np.arange(KV_DIM)
    ones_q = (iq[:, None] // HEAD_DIM == iq[None, :] // HEAD_DIM).astype(np.float32)
    ones_k = (ik[:, None] // HEAD_DIM == ik[None, :] // HEAD_DIM).astype(np.float32)
    dup = (ik[:, None] == (HEAD_DIM * (iq[None, :] // 128) + iq[None, :] % HEAD_DIM)).astype(np.float32)
    return jnp.asarray(ones_q, BF16), jnp.asarray(ones_k, BF16), jnp.asarray(dup, BF16), jnp.asarray(dup.T, BF16)


def _attn_prep(proj, gq, gk, ones_q, ones_k, dup, tm=1024):
    t = proj.shape[0]

    def body(p_ref, gq_ref, gk_ref, oq_ref, ok_ref, dup_ref, qn_ref, kd_ref, vd_ref):
        q = p_ref[:, 0:ATTN_DIM]
        k = p_ref[:, ATTN_DIM:ATTN_DIM + KV_DIM]
        v = p_ref[:, ATTN_DIM + KV_DIM:]
        rq = lax.rsqrt(_hdot(q * q, oq_ref[...]) * (1.0 / HEAD_DIM) + EPS)
        qn_ref[...] = (q * rq * gq_ref[...]) * (HEAD_DIM ** -0.5)
        rk = lax.rsqrt(_hdot(k * k, ok_ref[...]) * (1.0 / HEAD_DIM) + EPS)
        kn = k * rk * gk_ref[...]
        kd_ref[...] = _dot(kn.astype(BF16), dup_ref[...])
        vd_ref[...] = _dot(v.astype(BF16), dup_ref[...])

    full = lambda a: pl.BlockSpec(a.shape, lambda i: (0, 0))
    o_spec = pl.BlockSpec((tm, ATTN_DIM), lambda i: (i, 0))
    return pl.pallas_call(
        body, name="attn_prep", grid=(t // tm,),
        in_specs=[pl.BlockSpec((tm, 1024), lambda i: (i, 0)), full(gq), full(gk), full(ones_q), full(ones_k), full(dup)],
        out_specs=[o_spec, o_spec, o_spec], out_shape=[SDS((t, ATTN_DIM), F32)] * 3,
        compiler_params=_params("parallel"))(proj, gq, gk, ones_q, ones_k, dup)


def _attn_prep_bwd(proj, dqn, dkc, dkp, dvc, dvp, gq, gk, ones_q, ones_k, dup_t, tm=1024):
    t = proj.shape[0]
    nblk = t // tm
    off = SUPER // tm

    def body(p_ref, dqn_ref, dkc_ref, dkp_ref, dvc_ref, dvp_ref, gq_ref, gk_ref, oq_ref, ok_ref, dt_ref,
             o_ref, dgq_ref, dgk_ref):
        i = pl.program_id(0)
        has_next = (i + off < nblk).astype(F32)
        q = p_ref[:, 0:ATTN_DIM]
        k = p_ref[:, ATTN_DIM:ATTN_DIM + KV_DIM]
        dkn = _hdot(dkc_ref[...] + has_next * dkp_ref[...], dt_ref[...])
        dv = _hdot(dvc_ref[...] + has_next * dvp_ref[...], dt_ref[...])

        @pl.when(i == 0)
        def _():
            dgq_ref[...] = jnp.zeros_like(dgq_ref)
            dgk_ref[...] = jnp.zeros_like(dgk_ref)

        rq = lax.rsqrt(_hdot(q * q, oq_ref[...]) * (1.0 / HEAD_DIM) + EPS)
        xh = q * rq
        dy = dqn_ref[...] * (HEAD_DIM ** -0.5)
        dgq_ref[...] += jnp.sum(dy * xh, axis=0, keepdims=True)
        gd = dy * gq_ref[...]
        dq = rq * (gd - xh * (_hdot(gd * xh, oq_ref[...]) * (1.0 / HEAD_DIM)))
        rk = lax.rsqrt(_hdot(k * k, ok_ref[...]) * (1.0 / HEAD_DIM) + EPS)
        kh = k * rk
        dgk_ref[...] += jnp.sum(dkn * kh, axis=0, keepdims=True)
        gdk = dkn * gk_ref[...]
        dk = rk * (gdk - kh * (_hdot(gdk * kh, ok_ref[...]) * (1.0 / HEAD_DIM)))
        o_ref[:, 0:ATTN_DIM] = dq.astype(BF16)
        o_ref[:, ATTN_DIM:ATTN_DIM + KV_DIM] = dk.astype(BF16)
        o_ref[:, ATTN_DIM + KV_DIM:] = dv.astype(BF16)

    full = lambda a: pl.BlockSpec(a.shape, lambda i: (0, 0))
    cur = pl.BlockSpec((tm, ATTN_DIM), lambda i: (i, 0))
    nxt = pl.BlockSpec((tm, ATTN_DIM), lambda i: (jnp.minimum(i + off, nblk - 1), 0))
    return pl.pallas_call(
        body, name="attn_prep_bwd", grid=(nblk,),
        in_specs=[pl.BlockSpec((tm, 1024), lambda i: (i, 0)), cur, cur, nxt, cur, nxt,
                  full(gq), full(gk), full(ones_q), full(ones_k), full(dup_t)],
        out_specs=[pl.BlockSpec((tm, 1024), lambda i: (i, 0)), pl.BlockSpec((1, ATTN_DIM), lambda i: (0, 0)),
                   pl.BlockSpec((1, KV_DIM), lambda i: (0, 0))],
        out_shape=[SDS((t, 1024), BF16), SDS((1, ATTN_DIM), F32), SDS((1, KV_DIM), F32)],
        compiler_params=_params("arbitrary"))(proj, dqn, dkc, dkp, dvc, dvp, gq, gk, ones_q, ones_k, dup_t)


def _tile_masks():
    qi = lax.broadcasted_iota(jnp.int32, (2 * CHUNK, 2 * CHUNK), 0) & (CHUNK - 1)
    kj = lax.broadcasted_iota(jnp.int32, (2 * CHUNK, 2 * CHUNK), 1)
    delta = CHUNK + qi - kj
    band = (delta >= 0) & (delta <= CHUNK)
    return band, kj


def _deinterleave(dst, src, n_rows, d):
    per = n_rows // d
    for r in range(d):
        dst[r * per:(r + 1) * per, :] = src[pl.ds(r, per, stride=d), :]


def _attn_specs(t):
    blk = lambda f: pl.BlockSpec((SUPER, 128), f)
    cur = blk(lambda h, s: (s, h))
    prev = blk(lambda h, s: (jnp.maximum(s - 1, 0), h))
    return cur, prev


def _attn_fwd(qn, kd, vd):
    t = qn.shape[0]
    cur, prev = _attn_specs(t)

    def body(q_ref, kp_ref, kc_ref, vp_ref, vc_ref, o_ref, lse_ref, kk, vv, qd, kdd, vdd, po, pm, pll, acc, mm, ll):
        s = pl.program_id(1)
        kk[0:SUPER, :] = kp_ref[...]
        kk[SUPER:, :] = kc_ref[...]
        vv[0:SUPER, :] = vp_ref[...]
        vv[SUPER:, :] = vc_ref[...]
        m0 = lax.broadcasted_iota(jnp.int32, (CHUNK, 128), 1) < HEAD_DIM
        band, kj = _tile_masks()
        for d in DILATIONS:
            lq = SUPER // d
            if d == 1:
                qs_ref, ks_ref, vs_ref = q_ref, kk, vv
            else:
                _deinterleave(qd, q_ref, SUPER, d)
                _deinterleave(kdd, kk, 2 * SUPER, d)
                _deinterleave(vdd, vv, 2 * SUPER, d)
                qs_ref, ks_ref, vs_ref = qd, kdd, vdd

            nblk = lq // CHUNK

            def key_rows(ti):
                return pl.ds((ti // nblk) * 2 * lq + lq + (ti % nblk - 1) * CHUNK, 2 * CHUNK)

            def scores(ti):
                qt = qs_ref[pl.ds(ti * CHUNK, CHUNK), :]
                qs = jnp.concatenate([jnp.where(m0, qt, 0.0), jnp.where(m0, 0.0, qt)], axis=0).astype(BF16)
                return _dot(qs, ks_ref[key_rows(ti), :].astype(BF16), NT)

            def softmax_pv(ti, sc):
                ok = band if ti % nblk > 0 else band & (kj >= jnp.where(s > 0, 0, CHUNK))
                sc = jnp.where(ok, sc, -jnp.inf)
                mt = jnp.max(sc, axis=-1, keepdims=True)
                p = jnp.exp(sc - mt)
                lt = jnp.sum(p, axis=-1, keepdims=True)
                ot = _dot(p.astype(BF16), vs_ref[key_rows(ti), :].astype(BF16))
                qrows = pl.ds(ti * CHUNK, CHUNK)
                po[qrows, :] = jnp.where(m0, ot[:CHUNK], ot[CHUNK:])
                pm[qrows, :] = jnp.where(m0, mt[:CHUNK], mt[CHUNK:])
                pll[qrows, :] = jnp.where(m0, lt[:CHUNK], lt[CHUNK:])

            for ti in range(SUPER // CHUNK):
                softmax_pv(ti, scores(ti))
            if d == 1:
                acc[...] = po[...]
                mm[...] = pm[...]
                ll[...] = pll[...]
            else:
                for r in range(d):
                    rows = pl.ds(r, lq, stride=d)
                    seg = slice(r * lq, (r + 1) * lq)
                    m_old, m_new = mm[rows, :], pm[seg, :]
                    m_all = jnp.maximum(m_old, m_new)
                    a, b = jnp.exp(m_old - m_all), jnp.exp(m_new - m_all)
                    acc[rows, :] = acc[rows, :] * a + po[seg, :] * b
                    ll[rows, :] = ll[rows, :] * a + pll[seg, :] * b
                    mm[rows, :] = m_all
        o_ref[...] = acc[...] / ll[...]
        lse_ref[...] = mm[...] + jnp.log(ll[...])

    big = pltpu.VMEM((2 * SUPER, 128), F32)
    one = pltpu.VMEM((SUPER, 128), F32)
    return pl.pallas_call(
        body, name="attn_fwd", grid=(4, t // SUPER),
        in_specs=[cur, prev, cur, prev, cur], out_specs=[cur, cur],
        out_shape=[SDS((t, ATTN_DIM), F32)] * 2,
        scratch_shapes=[big, big, one, big, big, one, one, one, one, one, one],
        compiler_params=_params("parallel", "arbitrary"))(qn, kd, kd, vd, vd)


def _attn_bwd(qn, kd, vd, out, lse, dout, ones_pair):
    t = qn.shape[0]
    cur, prev = _attn_specs(t)

    def body(q_ref, kp_ref, kc_ref, vp_ref, vc_ref, o_ref, lse_ref, do_ref, ones_ref,
             dq_ref, dkc_ref, dkp_ref, dvc_ref, dvp_ref,
             kk, vv, od, ld, kb, vb, qsb, dosb, tk, tv, pdq, delta):
        s = pl.program_id(1)
        delta[...] = _hdot(do_ref[...] * o_ref[...], ones_ref[...])

        def per_row(a):
            ar = pltpu.roll(a, HEAD_DIM, 1)
            rows = jnp.concatenate([jnp.where(m0, a, ar), jnp.where(m0, ar, a)], axis=0)
            return jnp.concatenate([rows, rows], axis=1)

        kk[0:SUPER, :] = kp_ref[...]
        kk[SUPER:, :] = kc_ref[...]
        vv[0:SUPER, :] = vp_ref[...]
        vv[SUPER:, :] = vc_ref[...]
        for ref in (dq_ref, dkc_ref, dkp_ref, dvc_ref, dvp_ref):
            ref[...] = jnp.zeros_like(ref)
        m0 = lax.broadcasted_iota(jnp.int32, (CHUNK, 128), 1) < HEAD_DIM
        band, kj = _tile_masks()
        ninf = -jnp.inf
        for d in DILATIONS:
            lq = SUPER // d
            nblk = lq // CHUNK
            for r in range(d):
                seg = slice(r * 2 * lq, (r + 1) * 2 * lq)
                kb[seg, :] = kk[pl.ds(r, 2 * lq, stride=d), :].astype(BF16)
                vb[seg, :] = vv[pl.ds(r, 2 * lq, stride=d), :].astype(BF16)
            for ti in range(SUPER // CHUNK):
                rows = pl.ds(ti // nblk + d * CHUNK * (ti % nblk), CHUNK, stride=d)
                for src, dst in ((q_ref, qsb), (do_ref, dosb)):
                    a = src[rows, :]
                    dst[ti * 2 * CHUNK:(ti + 1) * 2 * CHUNK, :] = jnp.concatenate(
                        [jnp.where(m0, a, 0.0), jnp.where(m0, 0.0, a)], axis=0).astype(BF16)
                ld[ti * CHUNK:(ti + 1) * CHUNK, :] = lse_ref[rows, :]
                od[ti * CHUNK:(ti + 1) * CHUNK, :] = delta[rows, :]

            def operands(ti):
                r, nb = ti // nblk, ti % nblk
                stacked = slice(ti * 2 * CHUNK, (ti + 1) * 2 * CHUNK)
                krows = pl.ds(r * 2 * lq + lq + (nb - 1) * CHUNK, 2 * CHUNK)
                return stacked, krows

            def scores(ti):
                stacked, krows = operands(ti)
                kt = kb[krows, :]
                return dict(ti=ti, sc=_dot(qsb[stacked, :], kt, NT), dp=_dot(dosb[stacked, :], vb[krows, :], NT))

            def softmax_grad(c):
                qrows = slice(c["ti"] * CHUNK, (c["ti"] + 1) * CHUNK)
                ok = band if c["ti"] % nblk > 0 else band & (kj >= jnp.where(s > 0, 0, CHUNK))
                p = jnp.exp(jnp.where(ok, c.pop("sc"), ninf) - per_row(ld[qrows, :]))
                ds = p * (c.pop("dp") - per_row(od[qrows, :]))
                c.update(p=p.astype(BF16), ds=ds.astype(BF16))
                return c

            def grads(c):
                ti = c["ti"]
                stacked, krows = operands(ti)
                dqs = _dot(c["ds"], kb[krows, :])
                pdq[ti * CHUNK:(ti + 1) * CHUNK, :] = jnp.where(m0, dqs[:CHUNK], dqs[CHUNK:])
                tk[stacked, :] = _dot(c["ds"], qsb[stacked, :], TN)
                tv[stacked, :] = _dot(c["p"], dosb[stacked, :], TN)

            n_tiles = SUPER // CHUNK
            stage_a = scores(0)
            for ti in range(n_tiles):
                ahead = scores(ti + 1) if ti + 1 < n_tiles else None
                grads(softmax_grad(stage_a))
                stage_a = ahead

            for r in range(d):
                dq_ref[pl.ds(r, lq, stride=d), :] += pdq[r * lq:(r + 1) * lq, :]
                for tile_out, cur_ref, prev_ref in ((tk, dkc_ref, dkp_ref), (tv, dvc_ref, dvp_ref)):
                    first = r * nblk * 2 * CHUNK
                    prev_ref[pl.ds(SUPER - CHUNK * d + r, CHUNK, stride=d), :] += tile_out[first:first + CHUNK, :]
                    for nb in range(nblk):
                        at = (r * nblk + nb) * 2 * CHUNK
                        part = tile_out[at + CHUNK:at + 2 * CHUNK, :]
                        if nb + 1 < nblk:
                            part = part + tile_out[at + 2 * CHUNK:at + 3 * CHUNK, :]
                        cur_ref[pl.ds(r + d * nb * CHUNK, CHUNK, stride=d), :] += part

    big = pltpu.VMEM((2 * SUPER, 128), F32)
    one = pltpu.VMEM((SUPER, 128), F32)
    half = pltpu.VMEM((2 * SUPER, 128), BF16)
    return pl.pallas_call(
        body, name="attn_bwd", grid=(4, t // SUPER),
        in_specs=[cur, prev, cur, prev, cur, cur, cur, cur, pl.BlockSpec((128, 128), lambda h, s: (0, 0))],
        out_specs=[cur] * 5, out_shape=[SDS((t, ATTN_DIM), F32)] * 5,
        scratch_shapes=[big, big, one, one, half, half, half, half, big, big, one, one],
        compiler_params=_params("parallel", "arbitrary"))(qn, kd, kd, vd, vd, out, lse, dout, ones_pair)


def _ssd_consts():
    tri = np.tril(np.ones((CHUNK, CHUNK), np.float32))
    expand = np.zeros((128, SSM_INNER), np.float32)
    for h in range(SSM_HEADS):
        expand[h, h * HEAD_DIM:(h + 1) * HEAD_DIM] = 1.0
    return jnp.asarray(tri, BF16), jnp.asarray(tri.T, BF16), jnp.asarray(expand, BF16), jnp.asarray(expand.T, BF16)


def _conv4(x, halo, w_ref, b_ref):
    acc = b_ref[...] + w_ref[3:4, :] * x
    for k in range(3):
        acc = acc + w_ref[k:k + 1, :] * _shift_down(x, halo, 3 - k)
    return acc


def _softplus(x):
    return jnp.maximum(x, 0.0) + jnp.log(1.0 + jnp.exp(-jnp.abs(x)))


def _ssd_common(pre_x, pre_b, dt_ref, dtb_ref, alog_ref, tri_ref, exp_ref):
    xa = pre_x * _sigmoid(pre_x)
    ba = pre_b * _sigmoid(pre_b)
    dtv = _softplus(dt_ref[...] + dtb_ref[...])
    a_neg = -jnp.exp(alog_ref[...])
    acum = _hdot(tri_ref[...], dtv * a_neg, parts=3)
    lam = jnp.exp(acum)
    gam = jnp.exp(acum[CHUNK - 1:CHUNK, :] - acum)
    dt_e = _hdot(dtv, exp_ref[...])
    lam_e = _hdot(lam, exp_ref[...])
    gam_e = _hdot(gam, exp_ref[...])
    return dict(pre_x=pre_x, pre_b=pre_b, xa=xa, ba=ba, dtv=dtv, a_neg=a_neg, acum=acum,
                dt_e=dt_e, lam_e=lam_e, gam_e=gam_e, xdt=xa * dt_e)


def _decay(acum_t, h, transposed):
    rb = jnp.broadcast_to(acum_t[h:h + 1, :], (CHUNK, CHUNK))
    ri = lax.broadcasted_iota(jnp.int32, (CHUNK, CHUNK), 0)
    ci = lax.broadcasted_iota(jnp.int32, (CHUNK, CHUNK), 1)
    if transposed:
        return jnp.exp(jnp.where(ci >= ri, rb - rb.T, -jnp.inf))
    return jnp.exp(jnp.where(ri >= ci, rb.T - rb, -jnp.inf))


SSD_STEP = 4 * CHUNK


def _ssd_specs(t, rev):
    nc = t // SSD_STEP
    ch = (lambda c: nc - 1 - c) if rev else (lambda c: c)
    col = lambda w, j: pl.BlockSpec((SSD_STEP, w), lambda c: (ch(c), j))
    halo = lambda w, j: pl.BlockSpec((8, w), lambda c: (jnp.maximum(ch(c) * (SSD_STEP // 8) - 1, 0), j))
    return nc, ch, col, halo


def _ssd_fwd(proj, cwx, cbx, cwb, cbb, dtb, alog, dsk_e, norm_g, tri, expand):
    t = proj.shape[0]
    nc, _, col, halo = _ssd_specs(t, False)

    def body(z_all, xs_all, bc_all, dt_all, hx_ref, hb_ref, cwx_ref, cbx_ref, cwb_ref, cbb_ref, dtb_ref, alog_ref,
             dsk_ref, g_ref, tri_ref, exp_ref, y_all, hs_all, o_all, px_all, pb_all, state):
        @pl.when(pl.program_id(0) == 0)
        def _():
            state[...] = jnp.zeros_like(state)

        keep = (pl.program_id(0) > 0).astype(F32)
        for sc in range(SSD_STEP // CHUNK):
            rows = pl.ds(sc * CHUNK, CHUNK)
            before = pl.ds(sc * CHUNK - 8, 8)
            hx = hx_ref[...] * keep if sc == 0 else xs_all[before, :]
            hb = hb_ref[...] * keep if sc == 0 else bc_all[before, :]
            chunk(z_all.at[rows], xs_all.at[rows], bc_all.at[rows], dt_all.at[rows], hx, hb, cwx_ref, cbx_ref, cwb_ref,
                  cbb_ref, dtb_ref, alog_ref, dsk_ref, g_ref, tri_ref, exp_ref, y_all.at[rows],
                  hs_all.at[pl.ds(sc, 1)], o_all.at[rows], px_all.at[rows], pb_all.at[rows], state)

    def chunk(z_ref, xs_ref, bc_ref, dt_ref, hx, hb, cwx_ref, cbx_ref, cwb_ref, cbb_ref, dtb_ref, alog_ref,
              dsk_ref, g_ref, tri_ref, exp_ref, y_ref, hs_ref, o_ref, px_ref, pb_ref, state):
        pre_x = _conv4(xs_ref[...], hx, cwx_ref, cbx_ref)
        pre_b = _conv4(bc_ref[...], hb, cwb_ref, cbb_ref)
        px_ref[...] = pre_x.astype(BF16)
        pb_ref[...] = pre_b.astype(BF16)
        v = _ssd_common(pre_x, pre_b, dt_ref, dtb_ref, alog_ref, tri_ref, exp_ref)
        acum_t = v["acum"].T
        xdt, ba = v["xdt"], v["ba"]
        h_in = state[...]
        hs_ref[0] = h_in
        xg = xdt * v["gam_e"]
        m0 = lax.broadcasted_iota(jnp.int32, (CHUNK, 128), 1) < HEAD_DIM
        for g in range(2):
            bg = ba[:, g * 128:(g + 1) * 128].astype(BF16)
            cg = ba[:, 256 + g * 128:256 + (g + 1) * 128].astype(BF16)
            gl = slice(g * 512, (g + 1) * 512)
            cb = _dot(cg, bg, NT)
            y_off = _dot(cg, h_in[:, gl].astype(BF16)) * v["lam_e"][:, gl]
            s_new = _dot(bg.T, xg[:, gl].astype(BF16))
            state[:, gl] = h_in[:, gl] * v["lam_e"][CHUNK - 1:CHUNK, gl] + s_new
            for j in range(4):
                h0 = 8 * g + 2 * j
                ln = slice(g * 512 + j * 128, g * 512 + (j + 1) * 128)
                xp = xdt[:, ln].astype(BF16)
                y0 = _dot((cb * _decay(acum_t, h0, False)).astype(BF16), xp)
                y1 = _dot((cb * _decay(acum_t, h0 + 1, False)).astype(BF16), xp)
                y_ref[:, ln] = jnp.where(m0, y0, y1) + y_off[:, j * 128:(j + 1) * 128]
        z = z_ref[...]
        yg = (y_ref[...] + dsk_ref[...] * v["xa"]) * (z * _sigmoid(z))
        r = lax.rsqrt(jnp.mean(yg * yg, axis=-1, keepdims=True) + EPS)
        o_ref[...] = (yg * r * g_ref[...]).astype(BF16)

    full = lambda a: pl.BlockSpec(a.shape, lambda c: (0,) * a.ndim)
    return pl.pallas_call(
        body, name="ssd_fwd", grid=(nc,),
        in_specs=[col(1024, 1), col(1024, 2), col(512, 6), col(128, 28), halo(1024, 2), halo(512, 6),
                  full(cwx), full(cbx), full(cwb), full(cbb), full(dtb), full(alog), full(dsk_e), full(norm_g),
                  full(tri), full(expand)],
        out_specs=[pl.BlockSpec((SSD_STEP, SSM_INNER), lambda c: (c, 0)),
                   pl.BlockSpec((SSD_STEP // CHUNK, 128, SSM_INNER), lambda c: (c, 0, 0)),
                   pl.BlockSpec((SSD_STEP, SSM_INNER), lambda c: (c, 0)),
                   pl.BlockSpec((SSD_STEP, SSM_INNER), lambda c: (c, 0)), pl.BlockSpec((SSD_STEP, 512), lambda c: (c, 0))],
        out_shape=[SDS((t, SSM_INNER), F32), SDS((t // CHUNK, 128, SSM_INNER), F32), SDS((t, SSM_INNER), BF16),
                   SDS((t, SSM_INNER), BF16), SDS((t, 512), BF16)],
        scratch_shapes=[pltpu.VMEM((128, SSM_INNER), F32)],
        compiler_params=_params("arbitrary"))(proj, proj, proj, proj, proj, proj, cwx, cbx, cwb, cbb, dtb, alog,
                                              dsk_e, norm_g, tri, expand)


def _ssd_bwd(proj, pre_x, pre_b, y_ssd, hs, dout, cwx, cwb, dtb, alog, dsk_e, norm_g, tri, triu, expand, expand_t):
    t = proj.shape[0]
    nc, ch, col, halo = _ssd_specs(t, True)

    def body(z_all, xs_all, bc_all, dt_all, px_all, pb_all, y_all, hin_all, do_all,
             cwx_ref, cwb_ref, dtb_ref, alog_ref, dsk_ref, g_ref, tri_ref, triu_ref, exp_ref, expt_ref,
             dz_all, dxs_all, dbc_all, ddt_all, dg_ref, ddsk_ref, dalog_ref, ddtb_ref, dcwx_ref, dcbx_ref, dcwb_ref,
             dcbb_ref, gstate, nx_x, nx_b, dact_b, dxdt_s):
        @pl.when(pl.program_id(0) == 0)
        def _():
            gstate[...] = jnp.zeros_like(gstate)
            nx_x[...] = jnp.zeros_like(nx_x)
            nx_b[...] = jnp.zeros_like(nx_b)
            for ref in (dg_ref, ddsk_ref, dalog_ref, ddtb_ref, dcwx_ref, dcbx_ref, dcwb_ref, dcbb_ref):
                ref[...] = jnp.zeros_like(ref)

        for sc in reversed(range(SSD_STEP // CHUNK)):
            rows = pl.ds(sc * CHUNK, CHUNK)
            by_rows = [r.at[rows] for r in (z_all, xs_all, bc_all, dt_all, px_all, pb_all, y_all)]
            outs = [r.at[rows] for r in (dz_all, dxs_all, dbc_all, ddt_all)]
            chunk(*by_rows, hin_all.at[pl.ds(sc, 1)], do_all.at[rows],
                  cwx_ref, cwb_ref, dtb_ref, alog_ref, dsk_ref, g_ref, tri_ref, triu_ref, exp_ref, expt_ref,
                  *outs, dg_ref, ddsk_ref, dalog_ref, ddtb_ref, dcwx_ref, dcbx_ref, dcwb_ref, dcbb_ref,
                  gstate, nx_x, nx_b, dact_b, dxdt_s)

    def chunk(z_ref, xs_ref, bc_ref, dt_ref, px_ref, pb_ref, y_ref, hin_ref, do_ref,
              cwx_ref, cwb_ref, dtb_ref, alog_ref, dsk_ref, g_ref, tri_ref, triu_ref, exp_ref, expt_ref,
              dz_ref, dxs_ref, dbc_ref, ddt_ref, dg_ref, ddsk_ref, dalog_ref, ddtb_ref, dcwx_ref, dcbx_ref, dcwb_ref,
              dcbb_ref, gstate, nx_x, nx_b, dact_b, dxdt_s):
        v = _ssd_common(px_ref[...].astype(F32), pb_ref[...].astype(F32), dt_ref, dtb_ref, alog_ref, tri_ref, exp_ref)
        acum_t = v["acum"].T
        xa, ba, xdt, dtv = v["xa"], v["ba"], v["xdt"], v["dtv"]
        lam_e, gam_e, dt_e = v["lam_e"], v["gam_e"], v["dt_e"]
        z = z_ref[...]
        y = y_ref[...]
        sz = _sigmoid(z)
        zs = z * sz
        y_tot = y + dsk_ref[...] * xa
        yg = y_tot * zs
        r = lax.rsqrt(jnp.mean(yg * yg, axis=-1, keepdims=True) + EPS)
        yh = yg * r
        do = do_ref[...]
        dg_ref[...] += jnp.sum(do * yh, axis=0, keepdims=True)
        gd = do * g_ref[...]
        dyg = r * (gd - yh * jnp.mean(gd * yh, axis=-1, keepdims=True))
        dz_ref[...] = (dyg * y_tot * (sz * (1.0 + z * (1.0 - sz)))).astype(BF16)
        dy = dyg * zs
        ddsk_ref[...] += jnp.sum(dy * xa, axis=0, keepdims=True)
        g_out = gstate[...]
        h_in = hin_ref[0]
        lam_dy = lam_e * dy
        gam_x = gam_e * xdt
        m0 = lax.broadcasted_iota(jnp.int32, (CHUNK, 128), 1) < HEAD_DIM
        lane = lax.broadcasted_iota(jnp.int32, (CHUNK, 128), 1)
        below = (lax.broadcasted_iota(jnp.int32, (CHUNK, CHUNK), 0) >
                 lax.broadcasted_iota(jnp.int32, (CHUNK, CHUNK), 1))
        da_in = jnp.zeros((CHUNK, 128), F32)
        off_y, off_x = [], []
        for g in range(2):
            bg = ba[:, g * 128:(g + 1) * 128].astype(BF16)
            cg = ba[:, 256 + g * 128:256 + (g + 1) * 128].astype(BF16)
            gl = slice(g * 512, (g + 1) * 512)
            gg = g_out[:, gl].astype(BF16)
            cb = _dot(cg, bg, NT)
            dxdt_off = _dot(bg, gg) * gam_e[:, gl]
            off_x.append(xdt[:, gl] * dxdt_off)
            off_y.append(dy[:, gl] * (_dot(cg, h_in[:, gl].astype(BF16)) * lam_e[:, gl]))
            q_sum = jnp.zeros((CHUNK, CHUNK), F32)
            for j in range(4):
                h0 = 8 * g + 2 * j
                ln = slice(g * 512 + j * 128, g * 512 + (j + 1) * 128)
                dyp = dy[:, ln]
                dyb = dyp.astype(BF16)
                xpb = xdt[:, ln].astype(BF16)
                dec = [_decay(acum_t, h0, False), _decay(acum_t, h0 + 1, False)]
                mix = [cb * dec[0], cb * dec[1]]
                d0 = _dot(mix[0].T.astype(BF16), dyb)
                d1 = _dot(mix[1].T.astype(BF16), dyb)
                dxdt_s[:, ln] = jnp.where(m0, d0, d1) + dxdt_off[:, j * 128:(j + 1) * 128]
                for e, (hh, dym) in enumerate(((h0, jnp.where(m0, dyp, 0.0)), (h0 + 1, jnp.where(m0, 0.0, dyp)))):
                    dyx = _dot(dym.astype(BF16), xpb, NT)
                    q_sum = q_sum + dyx * dec[e]
                    reach = jnp.where(below, _hdot(triu_ref[...], dyx * mix[e]), 0.0)
                    da_in = jnp.where(lane == hh, jnp.sum(reach, axis=-1, keepdims=True), da_in)
            gstate[:, gl] = g_out[:, gl] * lam_e[CHUNK - 1:CHUNK, gl] + _dot(cg.T, lam_dy[:, gl].astype(BF16))
            qb = q_sum.astype(BF16)
            dact_b[:, 256 + g * 128:256 + (g + 1) * 128] = (
                _dot(qb, bg) + _dot(lam_dy[:, gl].astype(BF16), h_in[:, gl].astype(BF16), NT))
            dact_b[:, g * 128:(g + 1) * 128] = _dot(qb.T, cg) + _dot(gam_x[:, gl].astype(BF16), gg, NT)
        dxdt = dxdt_s[...]
        seg_y = _hdot(jnp.concatenate(off_y, axis=1), expt_ref[...])
        seg_x = _hdot(jnp.concatenate(off_x, axis=1), expt_ref[...])
        e_col = jnp.sum(g_out * h_in * lam_e[CHUNK - 1:CHUNK, :], axis=0, keepdims=True)
        e_seg = _hdot(jnp.broadcast_to(e_col, (8, SSM_INNER)), expt_ref[...])[0:1, :]
        da = da_in + _hdot(triu_ref[...], seg_y) + (_hdot(tri_ref[...], seg_x) - seg_x) + e_seg
        a_neg = v["a_neg"]
        ddtv = da * a_neg + _hdot(dxdt * xa, expt_ref[...])
        dalog_ref[...] += jnp.sum(da * dtv, axis=0, keepdims=True) * a_neg
        lane16 = lax.broadcasted_iota(jnp.int32, (CHUNK, 128), 1) < SSM_HEADS
        draw = jnp.where(lane16, ddtv * _sigmoid(dt_ref[...] + dtb_ref[...]), 0.0)
        ddtb_ref[...] += jnp.sum(draw, axis=0, keepdims=True)
        ddt_ref[...] = draw.astype(BF16)
        dxa = dxdt * dt_e + dy * dsk_ref[...]
        for (dact, pre, x_ref, nx, cw_ref, dcw_ref, dcb_ref, dx_ref) in (
                (dxa, v["pre_x"], xs_ref, nx_x, cwx_ref, dcwx_ref, dcbx_ref, dxs_ref),
                (dact_b[...], v["pre_b"], bc_ref, nx_b, cwb_ref, dcwb_ref, dcbb_ref, dbc_ref)):
            sp = _sigmoid(pre)
            dpre = dact * (sp * (1.0 + pre * (1.0 - sp)))
            dcb_ref[...] += jnp.sum(dpre, axis=0, keepdims=True)
            xv = x_ref[...]
            nxt = nx[...]
            dx = cw_ref[3:4, :] * dpre
            dcw_ref[3:4, :] += jnp.sum(dpre * xv, axis=0, keepdims=True)
            for k in range(3):
                d_up = _shift_up(dpre, nxt, 3 - k)
                dcw_ref[k:k + 1, :] += jnp.sum(xv * d_up, axis=0, keepdims=True)
                dx = dx + cw_ref[k:k + 1, :] * d_up
            nx[...] = dpre[0:8, :]
            dx_ref[...] = dx.astype(dx_ref.dtype)

    full = lambda a: pl.BlockSpec(a.shape, lambda c: (0,) * a.ndim)
    rowblk = lambda w: pl.BlockSpec((SSD_STEP, w), lambda c: (ch(c), 0))
    acc = lambda a, b: pl.BlockSpec((a, b), lambda c: (0, 0))
    return pl.pallas_call(
        body, name="ssd_bwd", grid=(nc,),
        in_specs=[col(1024, 1), col(1024, 2), col(512, 6), col(128, 28), rowblk(SSM_INNER), rowblk(512),
                  rowblk(SSM_INNER),
                  pl.BlockSpec((SSD_STEP // CHUNK, 128, SSM_INNER), lambda c: (ch(c), 0, 0)),
                  rowblk(SSM_INNER),
                  full(cwx), full(cwb), full(dtb), full(alog), full(dsk_e), full(norm_g),
                  full(tri), full(triu), full(expand), full(expand_t)],
        out_specs=[rowblk(SSM_INNER), rowblk(SSM_INNER), rowblk(512), rowblk(128),
                   acc(1, 1024), acc(1, 1024), acc(1, 128), acc(1, 128), acc(4, 1024), acc(1, 1024), acc(4, 512),
                   acc(1, 512)],
        out_shape=[SDS((t, SSM_INNER), BF16), SDS((t, SSM_INNER), BF16), SDS((t, 512), BF16), SDS((t, 128), BF16),
                   SDS((1, 1024), F32), SDS((1, 1024), F32), SDS((1, 128), F32), SDS((1, 128), F32),
                   SDS((4, 1024), F32), SDS((1, 1024), F32), SDS((4, 512), F32), SDS((1, 512), F32)],
        scratch_shapes=[pltpu.VMEM((128, SSM_INNER), F32), pltpu.VMEM((8, 1024), F32), pltpu.VMEM((8, 512), F32),
                        pltpu.VMEM((CHUNK, 512), F32), pltpu.VMEM((CHUNK, SSM_INNER), F32)],
        compiler_params=_params("arbitrary"))(proj, proj, proj, proj, pre_x, pre_b, y_ssd, hs, dout,
                                              cwx, cwb, dtb, alog, dsk_e, norm_g, tri, triu, expand, expand_t)


def _conv3(x, halo, w_ref, b_ref, part):
    acc = b_ref[part] + w_ref[2, part] * x
    for k in range(2):
        acc = acc + w_ref[k, part] * _shift_down(x, halo, 2 - k)
    return acc


def _up_act(x, g, w_up_t, cw, cb, tm=2048, tn=256, tr=512):
    t, k = x.shape
    nj = D_FF // tn

    def body(x_ref, g_ref, wg_ref, wv_ref, w_ref, b_ref, u_ref, c_ref, h_ref, f_ref, halo):
        i, j = pl.program_id(0), pl.program_id(1)

        @pl.when(j == 0)
        def _():
            xv = x_ref[...]
            r = lax.rsqrt(jnp.mean(xv * xv, axis=-1, keepdims=True) + EPS)
            h_ref[...] = (xv * r * g_ref[...]).astype(BF16)

        @pl.when(i == 0)
        def _():
            halo[j] = jnp.zeros((2, 8, tn), F32)

        def matmuls(r):
            rows = slice(r * tr, (r + 1) * tr)
            return [_dot(h_ref[rows, :], wt_ref[...], NT) for wt_ref in (wg_ref, wv_ref)]

        def epilogue(r, us, before):
            rows = slice(r * tr, (r + 1) * tr)
            parts = []
            for part, u in enumerate(us):
                u_ref[part, rows, :] = u.astype(BF16)
                parts.append(_conv3(u, before[part], w_ref, b_ref, part))
                c_ref[part, rows, :] = parts[-1].astype(BF16)
            gate, val = parts
            f_ref[rows, :] = (gate * _sigmoid(gate) * val).astype(BF16)
            return [u[tr - 8:, :] for u in us]

        before = [halo[j, 0], halo[j, 1]]
        pending = matmuls(0)
        for r in range(tm // tr):
            ahead = matmuls(r + 1) if r + 1 < tm // tr else None
            before = epilogue(r, pending, before)
            pending = ahead
        halo[j, 0], halo[j, 1] = before

    return pl.pallas_call(
        body, name="up_proj", grid=(t // tm, nj),
        in_specs=[pl.BlockSpec((tm, k), lambda i, j: (i, 0)), pl.BlockSpec((1, k), lambda i, j: (0, 0)),
                  pl.BlockSpec((tn, k), lambda i, j: (j, 0)), pl.BlockSpec((tn, k), lambda i, j: (j + nj, 0)),
                  pl.BlockSpec((3, 2, 1, tn), lambda i, j: (0, 0, 0, j)), pl.BlockSpec((2, 1, tn), lambda i, j: (0, 0, j))],
        out_specs=[pl.BlockSpec((2, tm, tn), lambda i, j: (0, i, j)), pl.BlockSpec((2, tm, tn), lambda i, j: (0, i, j)),
                   pl.BlockSpec((tm, k), lambda i, j: (i, 0)), pl.BlockSpec((tm, tn), lambda i, j: (i, j))],
        out_shape=[SDS((2, t, D_FF), BF16), SDS((2, t, D_FF), BF16), SDS((t, k), BF16), SDS((t, D_FF), BF16)],
        scratch_shapes=[pltpu.VMEM((nj, 2, 8, tn), F32)],
        compiler_params=_params("arbitrary", "arbitrary"))(x, g, w_up_t, w_up_t, cw, cb)


def _ffn_bwd(dx2, w_down, u, c, cw, tm=512, tn=1408):
    t = u.shape[1]
    nj, ni = D_FF // tn, t // tm
    rev = lambda i: ni - 1 - i

    def body(dx_ref, wd_ref, u_ref, c_ref, w_ref, du_ref, dcw_ref, dcb_ref, nxt):
        i = pl.program_id(1)

        @pl.when(i == 0)
        def _():
            nxt[...] = jnp.zeros_like(nxt)
            dcw_ref[...] = jnp.zeros_like(dcw_ref)
            dcb_ref[...] = jnp.zeros_like(dcb_ref)

        df = _dot(dx_ref[...].astype(BF16), wd_ref[...], NT)
        gate, val = c_ref[0].astype(F32), c_ref[1].astype(F32)
        sg = _sigmoid(gate)
        dgate = df * val * (sg * (1.0 + gate * (1.0 - sg)))
        dval = df * (gate * sg)
        for part, d in enumerate((dgate, dval)):
            uu = u_ref[part].astype(F32)
            dcb_ref[part] += jnp.sum(d, axis=0, keepdims=True)
            ahead = nxt[part]
            acc = w_ref[2, part] * d
            dcw_ref[2, part] += jnp.sum(d * uu, axis=0, keepdims=True)
            for k in range(2):
                d_up = _shift_up(d, ahead, 2 - k)
                dcw_ref[k, part] += jnp.sum(uu * d_up, axis=0, keepdims=True)
                acc = acc + w_ref[k, part] * d_up
            nxt[part] = d[0:8, :]
            du_ref[part] = acc.astype(BF16)

    w_spec = pl.BlockSpec((3, 2, 1, tn), lambda j, i: (0, 0, 0, j))
    b_spec = pl.BlockSpec((2, 1, tn), lambda j, i: (0, 0, j))
    tile = pl.BlockSpec((2, tm, tn), lambda j, i: (0, rev(i), j))
    return pl.pallas_call(
        body, name="ffn_bwd", grid=(nj, ni),
        in_specs=[pl.BlockSpec((tm, D_MODEL), lambda j, i: (rev(i), 0)), pl.BlockSpec((tn, D_MODEL), lambda j, i: (j, 0)),
                  tile, tile, w_spec],
        out_specs=[tile, w_spec, b_spec],
        out_shape=[SDS((2, t, D_FF), BF16), SDS((3, 2, 1, D_FF), F32), SDS((2, 1, D_FF), F32)],
        scratch_shapes=[pltpu.VMEM((2, 8, tn), F32)],
        compiler_params=_params("parallel", "arbitrary"))(dx2, w_down, u, c, cw)


def _down_ple_loss(x1, f, w_down, g, w_gate, p, w_proj_t, target, tm=512):
    t = x1.shape[0]

    def body(x_ref, f_ref, wd_ref, g_ref, wg_ref, p_ref, wp_ref, tg_ref, dx_ref, dpre_ref, dpp_ref, h_ref, loss_ref,
             dg_ref):
        i = pl.program_id(0)
        xv = x_ref[...] + _dot(f_ref[...], wd_ref[...])
        r = lax.rsqrt(jnp.mean(xv * xv, axis=-1, keepdims=True) + EPS)
        xh = xv * r
        h = (xh * g_ref[...]).astype(BF16)
        h_ref[...] = h
        gate = _sigmoid(_dot(h, wg_ref[...]))
        pp = _dot(p_ref[...].astype(BF16), wp_ref[...], NT)
        err = (xv + gate * pp) - tg_ref[...]

        @pl.when(i == 0)
        def _():
            loss_ref[...] = jnp.zeros_like(loss_ref)
            dg_ref[...] = jnp.zeros_like(dg_ref)

        loss_ref[...] += 0.5 * jnp.sum(jnp.mean(err * err, axis=-1, keepdims=True), axis=0, keepdims=True)
        dy = err * (1.0 / D_MODEL)
        dpre = (dy * pp * gate * (1.0 - gate)).astype(BF16)
        dpre_ref[...] = dpre
        dpp_ref[...] = (dy * gate).astype(BF16)
        dh = _dot(dpre, wg_ref[...], NT)
        dg_ref[...] += jnp.sum(dh * xh, axis=0, keepdims=True)
        gd = dh * g_ref[...]
        dx_ref[...] = dy + r * (gd - xh * jnp.mean(gd * xh, axis=-1, keepdims=True))

    row = lambda w: pl.BlockSpec((tm, w), lambda i: (i, 0))
    full = lambda a: pl.BlockSpec(a.shape, lambda i: (0, 0))
    return pl.pallas_call(
        body, name="down_ple_loss", grid=(t // tm,),
        in_specs=[row(D_MODEL), row(D_FF), full(w_down), full(g), full(w_gate), row(PLE_DIM), full(w_proj_t),
                  row(D_MODEL)],
        out_specs=[row(D_MODEL), row(D_MODEL), row(D_MODEL), row(D_MODEL),
                   pl.BlockSpec((1, 128), lambda i: (0, 0)), pl.BlockSpec((1, D_MODEL), lambda i: (0, 0))],
        out_shape=[SDS((t, D_MODEL), F32), SDS((t, D_MODEL), BF16), SDS((t, D_MODEL), BF16), SDS((t, D_MODEL), BF16),
                   SDS((1, 128), F32), SDS((1, D_MODEL), F32)],
        compiler_params=_params("arbitrary"))(x1, f, w_down, g, w_gate, p, w_proj_t, target)


def _all_gather(arrays, name):
    n_a = len(arrays)

    def body(*refs):
        src, dst = refs[:n_a], refs[n_a:2 * n_a]
        send_sems, recv_sems, local_sems = refs[2 * n_a:]
        x, y, c = lax.axis_index("x"), lax.axis_index("y"), lax.axis_index("c")
        slot = lambda px, py, pc: 4 * px + 2 * py + pc
        me, sibling = (x, y, c), (x, y, 1 - c)
        chips = [(1 - x, y), (x, 1 - y), (1 - x, 1 - y)]

        def copy(a, k, block, to, own=False):
            return pltpu.make_async_remote_copy(
                src_ref=src[a] if own else dst[a].at[slot(*block)], dst_ref=dst[a].at[slot(*block)],
                send_sem=send_sems.at[a, k], recv_sem=recv_sems.at[a, k], device_id=to,
                device_id_type=pl.DeviceIdType.MESH)

        local = [pltpu.make_async_copy(src[a], dst[a].at[slot(*me)], local_sems.at[a]) for a in range(n_a)]
        for cp in local:
            cp.start()
        sends = []
        for a in range(n_a):
            sends.append(copy(a, 0, me, sibling, own=True))
            sends += [copy(a, 1 + j, me, (*chip, c), own=True) for j, chip in enumerate(chips)]
        for cp in sends:
            cp.start()
        for j, chip in enumerate(chips):
            for a in range(n_a):
                copy(a, 1 + j, (*chip, c), me).wait_recv()
                passed = copy(a, 4 + j, (*chip, c), sibling)
                passed.start()
                sends.append(passed)
        for a in range(n_a):
            copy(a, 0, sibling, me).wait_recv()
            for j, chip in enumerate(chips):
                copy(a, 4 + j, (*chip, 1 - c), me).wait_recv()
        for cp in sends:
            cp.wait_send()
        for cp in local:
            cp.wait()

    hbm = pl.BlockSpec(memory_space=pl.ANY)
    return pl.pallas_call(
        body, name=name, in_specs=[hbm] * n_a, out_specs=[hbm] * n_a,
        out_shape=[SDS((N_DEV,) + a.shape, a.dtype) for a in arrays],
        scratch_shapes=[pltpu.SemaphoreType.DMA((n_a, N_DEV - 1)), pltpu.SemaphoreType.DMA((n_a, N_DEV - 1)),
                        pltpu.SemaphoreType.DMA((n_a,))],
        )(*arrays)


def _peer(k):
    x, y, c = lax.axis_index("x"), lax.axis_index("y"), lax.axis_index("c")
    px = 1 - x if k & 4 else x
    py = 1 - y if k & 2 else y
    pc = 1 - c if k & 1 else c
    return (px, py, pc), 4 * px + 2 * py + pc


_HBM = pl.BlockSpec(memory_space=pltpu.HBM)
_SEM = pl.BlockSpec(memory_space=pltpu.SEMAPHORE)


def _split_copies(src, land, send_sems, recv_sems, scatter, arrivals):
    _, me = _peer(0)
    out = []
    for k in range(1, N_DEV):
        coords, peer = _peer(k)
        for a in range(len(src)):
            sem = a * (N_DEV - 1) + k - 1
            if scatter[a]:
                s, d = src[a].at[peer], land[a].at[k]
            else:
                s, d = src[a], land[a].at[peer if arrivals else me]
            out.append(pltpu.make_async_remote_copy(
                src_ref=s, dst_ref=d, send_sem=send_sems.at[sem], recv_sem=recv_sems.at[sem], device_id=coords,
                device_id_type=pl.DeviceIdType.MESH))
    return out


def _exchange_start(srcs, lands, scatter, name):
    n = len(srcs)

    def body(*refs):
        src, land = refs[:n], refs[n:2 * n]
        send_sems, recv_sems = refs[2 * n], refs[2 * n + 1]
        token = refs[-1]
        for cp in _split_copies(src, land, send_sems, recv_sems, scatter, False):
            cp.start()
        token[...] = jnp.zeros_like(token)

    hbm_shape = lambda a: pltpu.HBM(a.shape, a.dtype)
    sem_shape = pltpu.SemaphoreType.DMA((n * (N_DEV - 1),))
    outs = pl.pallas_call(
        body, name=name,
        out_shape=(sem_shape, sem_shape, *[hbm_shape(a) for a in srcs], *[hbm_shape(a) for a in lands],
                   SDS((8, 128), F32)),
        in_specs=[_HBM] * (2 * n), out_specs=(_SEM, _SEM, *[_HBM] * (2 * n), pl.BlockSpec(memory_space=pltpu.VMEM)),
        input_output_aliases={a: 2 + a for a in range(2 * n)},
        compiler_params=pltpu.CompilerParams(has_side_effects=pltpu.SideEffectType.DATAFLOW_SIDE_EFFECTING),
    )(*[pltpu.with_memory_space_constraint(a, pltpu.HBM) for a in list(srcs) + list(lands)])
    return outs[0], outs[1], outs[2:2 + n], outs[2 + n:2 + 2 * n], outs[-1]


def _exchange_wait(send_sems, recv_sems, srcs, lands, scatter, after, name):
    n = len(srcs)

    def body(*refs):
        src, land = refs[:n], refs[n:2 * n]
        for cp in _split_copies(src, land, refs[2 * n], refs[2 * n + 1], scatter, False):
            cp.wait_send()
        for cp in _split_copies(src, land, refs[2 * n], refs[2 * n + 1], scatter, True):
            cp.wait_recv()

    hbm_shape = lambda a: pltpu.HBM(a.shape, a.dtype)
    outs = pl.pallas_call(
        body, name=name, out_shape=tuple(hbm_shape(a) for a in list(srcs) + list(lands)),
        in_specs=[_HBM] * (2 * n) + [_SEM, _SEM, pl.BlockSpec(memory_space=pl.ANY)], out_specs=(_HBM,) * (2 * n),
        input_output_aliases={a: a for a in range(2 * n)},
        compiler_params=pltpu.CompilerParams(has_side_effects=pltpu.SideEffectType.DATAFLOW_SIDE_EFFECTING),
    )(*srcs, *lands, send_sems, recv_sems, after)
    return outs[:n], outs[n:]


def _reduce8(a, tr, name):
    _, rows, cols = a.shape

    def body(a_ref, o_ref):
        acc = a_ref[0]
        for j in range(1, N_DEV):
            acc = acc + a_ref[j]
        o_ref[...] = acc

    return pl.pallas_call(
        body, name=name, grid=(rows // tr,),
        in_specs=[pl.BlockSpec((N_DEV, tr, cols), lambda i: (0, i, 0))],
        out_specs=pl.BlockSpec((tr, cols), lambda i: (i, 0)), out_shape=SDS((rows, cols), F32),
        compiler_params=_params("parallel"))(a)


def _reduce_landed(own, land, name, tc=256):
    rows, cols = own.shape

    def body(own_ref, land_ref, o_ref):
        acc = own_ref[...].astype(F32)
        for k in range(1, N_DEV):
            acc = acc + land_ref[k].astype(F32)
        o_ref[...] = acc

    return pl.pallas_call(
        body, name=name, grid=(cols // tc,),
        in_specs=[pl.BlockSpec((rows, tc), lambda j: (0, j)), pl.BlockSpec((N_DEV, rows, tc), lambda j: (0, 0, j))],
        out_specs=pl.BlockSpec((rows, tc), lambda j: (0, j)), out_shape=SDS((rows, cols), F32),
        compiler_params=_params("parallel"))(own, land)


def _reduce_adamw(own, land, w, m, v, name, tc=256):
    rows, cols = own.shape

    def body(own_ref, land_ref, w_ref, m_ref, v_ref, g_ref, d_ref, mo_ref, vo_ref):
        g = own_ref[...].astype(F32)
        for k in range(1, N_DEV):
            g = g + land_ref[k].astype(F32)
        g_ref[...] = g
        d_ref[...], mo_ref[...], vo_ref[...] = _adam_update(w_ref[...], g, m_ref[...], v_ref[...])

    blk = pl.BlockSpec((rows, tc), lambda j: (0, j))
    return pl.pallas_call(
        body, name=name, grid=(cols // tc,),
        in_specs=[blk, pl.BlockSpec((N_DEV, rows, tc), lambda j: (0, 0, j)), blk, blk, blk], out_specs=[blk] * 4,
        out_shape=[SDS((rows, cols), F32)] * 4, compiler_params=_params("parallel"))(own, land, w, m, v)


def _adamw(w, g, m, v, name, tr=None):
    rows, cols = w.shape
    tr = rows if tr is None else tr

    def body(w_ref, g_ref, m_ref, v_ref, d_ref, mo_ref, vo_ref):
        d_ref[...], mo_ref[...], vo_ref[...] = _adam_update(w_ref[...], g_ref[...], m_ref[...], v_ref[...])

    blk = pl.BlockSpec((tr, cols), lambda i: (i, 0))
    return pl.pallas_call(
        body, name=name, grid=(rows // tr,), in_specs=[blk] * 4, out_specs=[blk] * 3,
        out_shape=[SDS((rows, cols), F32)] * 3, compiler_params=_params("parallel"))(w, g, m, v)


def _pad_rows(a, rows):
    return jnp.pad(a, ((0, rows - a.shape[0]),) + ((0, 0),) * (a.ndim - 1))


def _local_step(x, p, target, sm, wts, fetch_rest, send, tok):
    ones_q, ones_k, dup, dup_t = _head_consts()
    tri, triu, expand, expand_t = _ssd_consts()
    w_in_t = wts["in_t"]
    cwx, cwb = wts["ssm_cw"][:, :SSM_INNER], wts["ssm_cw"][:, SSM_INNER:]
    cbx, cbb = sm["ssm_conv_b"][:, :SSM_INNER], sm["ssm_conv_b"][:, SSM_INNER:]
    pad128 = lambda a: jnp.pad(a, ((0, 0), (0, 128 - a.shape[1])))
    dtb, alog = pad128(sm["dt_bias"]), pad128(sm["a_log"])
    dsk_e = jnp.repeat(sm["d_skip"], HEAD_DIM, axis=1)
    gq = jnp.tile(sm["q_norm_g"], (1, ATTN_DIM // HEAD_DIM))
    gk = jnp.tile(sm["k_norm_g"], (1, KV_DIM // HEAD_DIM))
    ffn_cw = wts["ffn_cw"].reshape(3, 2, 1, D_FF)
    ffn_cb = sm["ffn_conv_b"].reshape(2, 1, D_FF)

    proj, h1 = _norm_matmul(x, sm["attn_norm_g"] + tok, w_in_t, 1024, 1920, "in_proj")
    qn, kd, vd = _attn_prep(proj, gq, gk, ones_q, ones_k, dup)
    attn_out, lse = _attn_fwd(qn, kd, vd)
    y_ssd, hs, ssm_out, pre_x, pre_b = _ssd_fwd(proj, cwx, cbx, cwb, cbb, dtb, alog, dsk_e, sm["ssm_norm_g"], tri,
                                                expand)
    rest = fetch_rest(ssm_out)
    w_out, w_up_t, w_down, w_gate, w_proj_t = (rest[k] for k in ("out", "up_t", "down", "gate", "proj_t"))
    x1 = _mm_resid([(attn_out, None, w_out[:ATTN_DIM]), (ssm_out, None, w_out[ATTN_DIM:])], x, None, 1024, F32,
                   "out_proj")
    u, uc, h2, f = _up_act(x1, sm["ffn_norm_g"], w_up_t, ffn_cw, ffn_cb)
    dx2, dpre, dpp, h3, loss, dg_ple = _down_ple_loss(x1, f, w_down, sm["ple_norm_g"], w_gate, p, w_proj_t, target)

    g_gate = _wgrad(h3, None, dpre, "wg_gate")
    g_proj_t = _wgrad(dpp, None, p, "wg_proj")
    g_down = _wgrad(f, None, dx2, "wg_down")
    du, d_ffn_cw, d_ffn_cb = _ffn_bwd(dx2, w_down, u, uc, ffn_cw)
    dx1, dg_ffn = _mm_normbwd([(du, 0, w_up_t, D_FF, 0), (du, 1, w_up_t, D_FF, 1)], x1, sm["ffn_norm_g"], dx2, 512,
                              "up_proj_bwd")
    g_up_t = _wgrad(du, "all", h2, "wg_up")
    tok = send(dict(gate=g_gate, proj_t=g_proj_t, down=g_down, up_t=g_up_t)).astype(BF16)
    d_attn, d_ssm = _out_proj_bwd(dx1, w_out + tok)
    g_out = _wgrad_multi([attn_out, ssm_out], dx1, "wg_out")
    tok = send(dict(out=g_out))
    (dz, dxs, dbc, ddt, dg_ssm, d_dsk_e, d_alog, d_dtb, d_cwx, d_cbx, d_cwb, d_cbb) = _ssd_bwd(
        proj, pre_x, pre_b, y_ssd, hs, d_ssm, cwx, cwb, dtb + tok, alog, dsk_e, sm["ssm_norm_g"], tri, triu, expand,
        expand_t)
    dqn, dkc, dkp, dvc, dvp = _attn_bwd(qn, kd, vd, attn_out, lse, d_attn, ones_k[:128, :128])
    dqkv, dgq, dgk = _attn_prep_bwd(proj, dqn, dkc, dkp, dvc, dvp, gq + tok, gk, ones_q, ones_k, dup_t)
    pieces = [(dqkv, 0, 1024), (dz, 1024, 2048), (dxs, 2048, 3072), (dbc, 3072, 3584), (ddt, 3584, 3712)]
    g_in_t = jnp.concatenate([_wgrad_multi([dqkv, dz], h1, "wg_in_qkvz"),
                              _wgrad_multi([dxs, dbc, ddt], h1, "wg_in_xbcdt")], axis=0)[:IN_PROJ]
    tok = send(dict(in_t=g_in_t))
    grad_x, dg_attn = _mm_normbwd([(a, None, w_in_t, hi - lo, lo // (hi - lo)) for a, lo, hi in pieces], x,
                                  sm["attn_norm_g"] + tok, dx1, 512, "in_proj_bwd")

    small = dict(
        attn_norm_g=dg_attn, q_norm_g=dgq.reshape(-1, HEAD_DIM).sum(0, keepdims=True),
        k_norm_g=dgk.reshape(-1, HEAD_DIM).sum(0, keepdims=True),
        ssm_conv_w=jnp.concatenate([d_cwx, d_cwb], axis=1), ssm_conv_b=jnp.concatenate([d_cbx, d_cbb], axis=1),
        dt_bias=d_dtb[:, :SSM_HEADS], a_log=d_alog[:, :SSM_HEADS],
        d_skip=d_dsk_e.reshape(SSM_HEADS, HEAD_DIM).sum(1)[None, :], ssm_norm_g=dg_ssm, ffn_norm_g=dg_ffn,
        ffn_conv_w=d_ffn_cw.reshape(3, 2 * D_FF), ffn_conv_b=d_ffn_cb.reshape(1, 2 * D_FF), ple_norm_g=dg_ple)
    return loss[0, 0], grad_x, small


_SMALL = (("attn_norm_g", 1, 1024), ("q_norm_g", 1, 64), ("k_norm_g", 1, 64), ("ssm_conv_w", 4, XBC_DIM),
          ("ssm_conv_b", 1, XBC_DIM), ("dt_bias", 1, 16), ("a_log", 1, 16), ("d_skip", 1, 16), ("ssm_norm_g", 1, 1024),
          ("ffn_norm_g", 1, 1024), ("ffn_conv_w", 3, 2 * D_FF), ("ffn_conv_b", 1, 2 * D_FF), ("ple_norm_g", 1, 1024))
_SMALL_ROWS, _SMALL_COLS = 32, XBC_DIM
_SHARDED_SMALL = ("ssm_conv_w", "ffn_conv_w")


def _small_chunks(n):
    return 1 if n <= _SMALL_COLS else 4


def _pack_small(parts, loss):
    rows = []
    for k, r, n in _SMALL:
        c = _small_chunks(n)
        rows.append(jnp.pad(parts[k].reshape(r * c, n // c), ((0, 0), (0, _SMALL_COLS - n // c))))
    packed = _pad_rows(jnp.concatenate(rows, axis=0), _SMALL_ROWS)
    at_loss = ((lax.broadcasted_iota(jnp.int32, packed.shape, 0) == _SMALL_ROWS - 1) &
               (lax.broadcasted_iota(jnp.int32, packed.shape, 1) == 0))
    return jnp.where(at_loss, loss, packed)


def _adam_update(w, g, m, v):
    c1 = 1.0 - ADAM_B1 ** ADAM_STEP
    c2 = 1.0 - ADAM_B2 ** ADAM_STEP
    mn = ADAM_B1 * m + (1.0 - ADAM_B1) * g
    vn = ADAM_B2 * v + (1.0 - ADAM_B2) * (g * g)
    return -ADAM_LR * ((mn / c1) / (jnp.sqrt(vn / c2) + ADAM_EPS) + ADAM_WD * w), mn, vn


def _adamw_small(g_all, g_shard, w, m, v):
    ins, shapes = [g_all], []
    for k, _, _ in _SMALL:
        shape2 = w[k].shape if w[k].ndim == 2 else (1, w[k].shape[0])
        shapes.append(shape2)
        ins += ([g_shard[k]] if k in _SHARDED_SMALL else []) + [a.reshape(shape2) for a in (w[k], m[k], v[k])]

    def body(*refs):
        g_ref, pos, row = refs[0], 1, 0
        outs = refs[len(ins):]
        for i, (k, r, n) in enumerate(_SMALL):
            c = _small_chunks(n)
            if k in _SHARDED_SMALL:
                g = refs[pos][...]
                pos += 1
            elif c == 1:
                g = g_ref[row:row + r, 0:n]
            else:
                g = jnp.concatenate([g_ref[row + j:row + j + 1, 0:n // c] for j in range(c)], axis=1)
            row += r * c
            d, mn, vn = _adam_update(refs[pos][...], g, refs[pos + 1][...], refs[pos + 2][...])
            pos += 3
            for o_ref, val in zip(outs[4 * i:4 * i + 4], (g, d, mn, vn)):
                o_ref[...] = val

    res = pl.pallas_call(body, name="adamw_small",
                         out_shape=[SDS(s, F32) for s in shapes for _ in range(4)])(*ins)
    return {k: tuple(a.reshape(w[k].shape) for a in res[4 * i:4 * i + 4]) for i, (k, _, _) in enumerate(_SMALL)}


def kernel(x, p, attn_norm_g, w_in, q_norm_g, k_norm_g, ssm_conv_w, ssm_conv_b, dt_bias, a_log, d_skip, ssm_norm_g, w_out, ffn_norm_g, w_up, ffn_conv_w, ffn_conv_b, w_down, ple_norm_g, w_ple_gate, w_ple_proj, loss_target, m_attn_norm_g, m_w_in, m_q_norm_g, m_k_norm_g, m_ssm_conv_w, m_ssm_conv_b, m_dt_bias, m_a_log, m_d_skip, m_ssm_norm_g, m_w_out, m_ffn_norm_g, m_w_up, m_ffn_conv_w, m_ffn_conv_b, m_w_down, m_ple_norm_g, m_w_ple_gate, m_w_ple_proj, v_attn_norm_g, v_w_in, v_q_norm_g, v_k_norm_g, v_ssm_conv_w, v_ssm_conv_b, v_dt_bias, v_a_log, v_d_skip, v_ssm_norm_g, v_w_out, v_ffn_norm_g, v_w_up, v_ffn_conv_w, v_ffn_conv_b, v_w_down, v_ple_norm_g, v_w_ple_gate, v_w_ple_proj):
    names = ("attn_norm_g", "w_in", "q_norm_g", "k_norm_g", "ssm_conv_w", "ssm_conv_b", "dt_bias", "a_log", "d_skip",
             "ssm_norm_g", "w_out", "ffn_norm_g", "w_up", "ffn_conv_w", "ffn_conv_b", "w_down", "ple_norm_g",
             "w_ple_gate", "w_ple_proj")
    w = dict(zip(names, (attn_norm_g, w_in, q_norm_g, k_norm_g, ssm_conv_w, ssm_conv_b, dt_bias, a_log, d_skip,
                         ssm_norm_g, w_out, ffn_norm_g, w_up, ffn_conv_w, ffn_conv_b, w_down, ple_norm_g, w_ple_gate,
                         w_ple_proj)))
    m = dict(zip(names, (m_attn_norm_g, m_w_in, m_q_norm_g, m_k_norm_g, m_ssm_conv_w, m_ssm_conv_b, m_dt_bias,
                         m_a_log, m_d_skip, m_ssm_norm_g, m_w_out, m_ffn_norm_g, m_w_up, m_ffn_conv_w, m_ffn_conv_b,
                         m_w_down, m_ple_norm_g, m_w_ple_gate, m_w_ple_proj)))
    v = dict(zip(names, (v_attn_norm_g, v_w_in, v_q_norm_g, v_k_norm_g, v_ssm_conv_w, v_ssm_conv_b, v_dt_bias,
                         v_a_log, v_d_skip, v_ssm_norm_g, v_w_out, v_ffn_norm_g, v_w_up, v_ffn_conv_w, v_ffn_conv_b,
                         v_w_down, v_ple_norm_g, v_w_ple_gate, v_w_ple_proj)))
    w, m, v = ({k: a[0] for k, a in d.items()} for d in (w, m, v))
    me = 4 * lax.axis_index("x") + 2 * lax.axis_index("y") + lax.axis_index("c")

    mine = dict(in_t=w["w_in"].T, out=w["w_out"], up_t=w["w_up"].T, down=w["w_down"], gate=w["w_ple_gate"],
                proj_t=w["w_ple_proj"].T)
    mine = {k: a.astype(BF16) for k, a in mine.items()}
    conv_pack = jnp.pad(jnp.concatenate([w["ssm_conv_w"].reshape(-1), w["ffn_conv_w"].reshape(-1)]),
                        (0, 3072 - 2880)).reshape(8, 384)
    all_in, all_conv = _all_gather([mine["in_t"], conv_pack], "gather_first")
    later = ("out", "up_t", "down", "gate", "proj_t")
    zones = [lax.dynamic_update_slice(lax.empty((N_DEV,) + mine[k].shape, BF16), mine[k][None], (me, 0, 0))
             for k in later]
    zones, all_in, all_conv = lax.optimization_barrier((zones, all_in, all_conv))
    rest_state = _exchange_start([mine[k] for k in later], zones, [False] * len(later), "gather_rest_start")

    def fetch_rest(after):
        _, landed = _exchange_wait(*rest_state[:4], [False] * len(later), after, "gather_rest_wait")
        return {k: a.reshape(N_DEV * a.shape[1], a.shape[2]) for k, a in zip(later, landed)}

    wts = dict(in_t=_pad_rows(all_in.reshape(IN_PROJ, D_MODEL), IN_PROJ_PAD))
    conv_flat = all_conv.reshape(N_DEV, 3072)
    wts["ssm_cw"] = conv_flat[:, :768].reshape(N_DEV, 4, XBC_DIM // N_DEV).transpose(1, 0, 2).reshape(4, XBC_DIM)
    wts["ffn_cw"] = conv_flat[:, 768:2880].reshape(N_DEV, 3, 2 * D_FF // N_DEV).transpose(1, 0, 2).reshape(3, 2 * D_FF)
    sm = {k: w[k].reshape(1, -1) for k, _, _ in _SMALL if k not in _SHARDED_SMALL}

    in_flight = []

    def send(grads):
        keys = sorted(grads)
        srcs = [grads[k].reshape(N_DEV, grads[k].shape[0] // N_DEV, grads[k].shape[1]) for k in keys]
        state = _exchange_start(srcs, [lax.empty(a.shape, BF16) for a in srcs], [True] * len(keys),
                                "send_" + "_".join(keys))
        in_flight.append((keys, state))
        return state[4][0:1, 0:1]

    loss, grad_x, small = _local_step(x[0], p[0, 0], loss_target[0], sm, wts, fetch_rest, send,
                                      rest_state[4][0:1, 0:1])

    (got_small,) = _all_gather([_pack_small(small, loss)], "gather_small_grads")
    g_small = _reduce8(got_small, _SMALL_ROWS, "reduce_small")
    loss = g_small[_SMALL_ROWS - 1, 0]
    grads, gw, delta, new_m, new_v = {}, {}, {}, {}, {}
    row_sharded = {"out": "w_out", "down": "w_down", "gate": "w_ple_gate"}
    for keys, state in in_flight:
        sent, landed = _exchange_wait(*state[:4], [True] * len(keys), grad_x, "wait_" + "_".join(keys))
        for k, shares, land in zip(keys, sent, landed):
            own = lax.dynamic_index_in_dim(shares, me, 0, keepdims=False)
            if k in row_sharded:
                n = row_sharded[k]
                gw[n], delta[n], new_m[n], new_v[n] = _reduce_adamw(own, land, w[n], m[n], v[n], "update_" + n)
            else:
                grads[k] = _reduce_landed(own, land, "reduce_" + k)
    gw.update({"w_in": grads["in_t"].T, "w_up": grads["up_t"].T, "w_ple_proj": grads["proj_t"].T})
    n_ssm, n_ffn = XBC_DIM // N_DEV, 2 * D_FF // N_DEV
    g_shard = {"ssm_conv_w": lax.dynamic_slice(g_small, (3, me * n_ssm), (4, n_ssm)),
               "ffn_conv_w": lax.dynamic_slice(g_small[13:25, :2 * D_FF // 4].reshape(3, 2 * D_FF), (0, me * n_ffn),
                                               (3, n_ffn))}

    for k, tr in (("w_in", 256), ("w_up", 256), ("w_ple_proj", None)):
        delta[k], new_m[k], new_v[k] = _adamw(w[k], gw[k], m[k], v[k], "adamw_" + k, tr)
    for k, (g_k, d_k, m_k, v_k) in _adamw_small(g_small, g_shard, w, m, v).items():
        gw[k], delta[k], new_m[k], new_v[k] = g_k, d_k, m_k, v_k

    outs = [loss, grad_x[None]]
    for d in (gw, delta, new_m, new_v):
        outs += [d[k][None] for k in names]
    return tuple(outs)
```

```python
import functools

import numpy as np
import jax
import jax.numpy as jnp
from jax import lax
from jax.experimental import pallas as pl
from jax.experimental.pallas import tpu as pltpu

F32 = jnp.float32
BF16 = jnp.bfloat16
SDS = jax.ShapeDtypeStruct
EPS = 1e-6
N_DEV = 8
D_MODEL = 1024
HEAD_DIM = 64
ATTN_DIM = 512
KV_DIM = 256
SSM_INNER = 1024
SSM_HEADS = 16
BC_DIM = 256
XBC_DIM = SSM_INNER + 2 * BC_DIM
MIX_DIM = ATTN_DIM + SSM_INNER
IN_PROJ = 3600
IN_PROJ_PAD = 3840
D_FF = 2816
PLE_DIM = 256
CHUNK = 128
SUPER = 2048
DILATIONS = (1, 4, 16)
TILE_UNROLL = 8
VMEM_LIMIT = 56 * 1024 * 1024
ADAM_LR, ADAM_B1, ADAM_B2, ADAM_EPS, ADAM_WD, ADAM_STEP = 0.001, 0.9, 0.999, 1e-08, 0.01, 10

NT = (((1,), (1,)), ((), ()))
TN = (((0,), (0,)), ((), ()))


def _params(*sem):
    return pltpu.CompilerParams(dimension_semantics=sem if sem else None, vmem_limit_bytes=VMEM_LIMIT)


def _dot(a, b, dims=None):
    if dims is None:
        return jnp.dot(a, b, preferred_element_type=F32)
    return lax.dot_general(a, b, dims, preferred_element_type=F32)


def _hdot(a, b, parts=2):
    a_exact = a.dtype == BF16
    x = b if a_exact else a
    acc = None
    for _ in range(parts):
        piece = x.astype(BF16)
        x = x - piece.astype(F32)
        d = _dot(a, piece) if a_exact else _dot(piece, b)
        acc = d if acc is None else acc + d
    return acc


def _sigmoid(x):
    return 0.5 * jnp.tanh(0.5 * x) + 0.5


def _shift_down(x, halo8, s):
    xr = pltpu.roll(x, s, 0)
    row = lax.broadcasted_iota(jnp.int32, halo8.shape, 0)
    first = jnp.where(row < s, pltpu.roll(halo8, s, 0), xr[0:8])
    return jnp.concatenate([first, xr[8:]], axis=0)


def _shift_up(x, halo8, s):
    n = x.shape[0]
    xr = pltpu.roll(x, n - s, 0)
    row = lax.broadcasted_iota(jnp.int32, halo8.shape, 0)
    last = jnp.where(row >= 8 - s, pltpu.roll(halo8, 8 - s, 0), xr[n - 8:])
    return jnp.concatenate([xr[:n - 8], last], axis=0)


def _norm_matmul(x, g, wt, tm, tn, name):
    t, k = x.shape
    n = wt.shape[0]

    def body(x_ref, g_ref, w_ref, o_ref, h_ref):
        @pl.when(pl.program_id(1) == 0)
        def _():
            xv = x_ref[...]
            r = lax.rsqrt(jnp.mean(xv * xv, axis=-1, keepdims=True) + EPS)
            h_ref[...] = (xv * r * g_ref[...]).astype(BF16)
        o_ref[...] = _dot(h_ref[...], w_ref[...], NT)

    return pl.pallas_call(
        body, name=name, grid=(t // tm, n // tn),
        in_specs=[pl.BlockSpec((tm, k), lambda i, j: (i, 0)), pl.BlockSpec((1, k), lambda i, j: (0, 0)),
                  pl.BlockSpec((tn, k), lambda i, j: (j, 0))],
        out_specs=[pl.BlockSpec((tm, tn), lambda i, j: (i, j)), pl.BlockSpec((tm, k), lambda i, j: (i, 0))],
        out_shape=[SDS((t, n), F32), SDS((t, k), BF16)],
        compiler_params=_params("parallel", "arbitrary"))(x, g, wt)


def _a_spec(a, lead, tm):
    if lead is None:
        return pl.BlockSpec((tm, a.shape[-1]), lambda i: (i, 0))
    return pl.BlockSpec((None, tm, a.shape[-1]), lambda i, _l=lead: (_l, i, 0))


def _mm_resid(pairs, res, dims, tm, out_dtype, name):
    t = pairs[0][0].shape[-2]
    n = pairs[0][2].shape[1] if dims is None else pairs[0][2].shape[0]
    np_ = len(pairs)

    def body(*refs):
        o_ref = refs[-1]
        acc = refs[2 * np_][...] if res is not None else None
        for q in range(np_):
            d = _dot(refs[q][...].astype(BF16), refs[np_ + q][...], dims)
            acc = d if acc is None else acc + d
        o_ref[...] = acc.astype(out_dtype)

    in_specs = [_a_spec(a, lead, tm) for a, lead, _ in pairs]
    in_specs += [pl.BlockSpec(b.shape, lambda i: (0, 0)) for _, _, b in pairs]
    args = [a for a, _, _ in pairs] + [b for _, _, b in pairs]
    if res is not None:
        in_specs.append(pl.BlockSpec((tm, n), lambda i: (i, 0)))
        args.append(res)
    return pl.pallas_call(
        body, name=name, grid=(t // tm,), in_specs=in_specs,
        out_specs=pl.BlockSpec((tm, n), lambda i: (i, 0)), out_shape=SDS((t, n), out_dtype),
        compiler_params=_params("parallel"))(*args)


def _wgrad_multi(parts, b, name, tk=2048):
    t, n = b.shape
    widths = [a.shape[1] for a in parts]
    m = sum(widths)

    def body(*refs):
        b_ref, o_ref, acc = refs[len(parts):]

        @pl.when(pl.program_id(0) == 0)
        def _():
            acc[...] = jnp.zeros_like(acc)
        bv = b_ref[...].astype(BF16)
        row = 0
        for a_ref, w in zip(refs, widths):
            acc[row:row + w, :] += _dot(a_ref[...].astype(BF16), bv, TN)
            row += w

        @pl.when(pl.program_id(0) == pl.num_programs(0) - 1)
        def _():
            o_ref[...] = acc[...].astype(BF16)

    return pl.pallas_call(
        body, name=name, grid=(t // tk,),
        in_specs=[pl.BlockSpec((tk, w), lambda k: (k, 0)) for w in widths] + [pl.BlockSpec((tk, n), lambda k: (k, 0))],
        out_specs=pl.BlockSpec((m, n), lambda k: (0, 0)), out_shape=SDS((m, n), BF16),
        scratch_shapes=[pltpu.VMEM((m, n), F32)],
        compiler_params=_params("arbitrary"))(*parts, b)


def _out_proj_bwd(dx, w_out, tm=1024):
    t = dx.shape[0]

    def body(dx_ref, w_ref, da_ref, ds_ref):
        a = dx_ref[...].astype(BF16)
        da_ref[...] = _dot(a, w_ref[0:ATTN_DIM, :], NT)
        ds_ref[...] = _dot(a, w_ref[ATTN_DIM:, :], NT)

    return pl.pallas_call(
        body, name="out_proj_bwd", grid=(t // tm,),
        in_specs=[pl.BlockSpec((tm, D_MODEL), lambda i: (i, 0)), pl.BlockSpec(w_out.shape, lambda i: (0, 0))],
        out_specs=[pl.BlockSpec((tm, ATTN_DIM), lambda i: (i, 0)), pl.BlockSpec((tm, SSM_INNER), lambda i: (i, 0))],
        out_shape=[SDS((t, ATTN_DIM), F32), SDS((t, SSM_INNER), F32)],
        compiler_params=_params("parallel"))(dx, w_out)


def _mm_normbwd(pairs, x, g, dres, tm, name):
    t, k = x.shape
    np_ = len(pairs)
    b_specs = [pl.BlockSpec((rows, b.shape[1]), lambda i, _b=blk: (_b, 0)) for _, _, b, rows, blk in pairs]
    pairs = [(a, lead, b) for a, lead, b, _, _ in pairs]

    def body(*refs):
        x_ref, g_ref, dres_ref, dx_ref, dg_ref = refs[2 * np_:]
        dh = None
        for q in range(np_):
            d = _dot(refs[q][...], refs[np_ + q][...])
            dh = d if dh is None else dh + d
        xv = x_ref[...]
        r = lax.rsqrt(jnp.mean(xv * xv, axis=-1, keepdims=True) + EPS)
        xh = xv * r

        @pl.when(pl.program_id(0) == 0)
        def _():
            dg_ref[...] = jnp.zeros_like(dg_ref)
        dg_ref[...] += jnp.sum(dh * xh, axis=0, keepdims=True)
        gd = dh * g_ref[...]
        dx_ref[...] = dres_ref[...] + r * (gd - xh * jnp.mean(gd * xh, axis=-1, keepdims=True))

    in_specs = [_a_spec(a, lead, tm) for a, lead, _ in pairs] + b_specs
    in_specs += [pl.BlockSpec((tm, k), lambda i: (i, 0)), pl.BlockSpec((1, k), lambda i: (0, 0)),
                 pl.BlockSpec((tm, k), lambda i: (i, 0))]
    args = [a for a, _, _ in pairs] + [b for _, _, b in pairs] + [x, g, dres]
    return pl.pallas_call(
        body, name=name, grid=(t // tm,), in_specs=in_specs,
        out_specs=[pl.BlockSpec((tm, k), lambda i: (i, 0)), pl.BlockSpec((1, k), lambda i: (0, 0))],
        out_shape=[SDS((t, k), F32), SDS((1, k), F32)],
        compiler_params=_params("arbitrary"))(*args)


def _wgrad(a, a_lead, b, name, tk=2048):
    t, m = a.shape[-2:]
    n = b.shape[1]
    tm = m if m <= 1024 else 1408
    assert m % tm == 0

    def body(a_ref, b_ref, o_ref, acc):
        @pl.when(pl.program_id(1) == 0)
        def _():
            acc[...] = jnp.zeros_like(acc)
        acc[...] += _dot(a_ref[...].astype(BF16), b_ref[...].astype(BF16), TN)

        @pl.when(pl.program_id(1) == pl.num_programs(1) - 1)
        def _():
            o_ref[...] = acc[...].astype(BF16)

    per, lead = m // tm, 1
    if a_lead == "all":
        lead = a.shape[0]
        a_spec = pl.BlockSpec((None, tk, tm), lambda mi, ki: (mi // per, ki, mi % per))
    elif a_lead is None:
        a_spec = pl.BlockSpec((tk, tm), lambda mi, ki: (ki, mi))
    else:
        a_spec = pl.BlockSpec((None, tk, tm), lambda mi, ki, _l=a_lead: (_l, ki, mi))
    return pl.pallas_call(
        body, name=name, grid=(lead * per, t // tk),
        in_specs=[a_spec, pl.BlockSpec((tk, n), lambda mi, ki: (ki, 0))],
        out_specs=pl.BlockSpec((tm, n), lambda mi, ki: (mi, 0)), out_shape=SDS((lead * m, n), BF16),
        scratch_shapes=[pltpu.VMEM((tm, n), F32)],
        compiler_params=_params("parallel", "arbitrary"))(a, b)


def _head_consts():
    iq = np.arange(ATTN_DIM)
    ik = np.arange(KV_DIM)
    ones_q = (iq[:, None] // HEAD_DIM == iq[None, :] // HEAD_DIM).astype(np.float32)
    ones_k = (ik[:, None] // HEAD_DIM == ik[None, :] // HEAD_DIM).astype(np.float32)
    dup = (ik[:, None] == (HEAD_DIM * (iq[None, :] // 128) + iq[None, :] % HEAD_DIM)).astype(np.float32)
    return jnp.asarray(ones_q, BF16), jnp.asarray(ones_k, BF16), jnp.asarray(dup, BF16), jnp.asarray(dup.T, BF16)


def _attn_prep(proj, gq, gk, ones_q, ones_k, dup, tm=1024):
    t = proj.shape[0]

    def body(p_ref, gq_ref, gk_ref, oq_ref, ok_ref, dup_ref, qn_ref, kd_ref, vd_ref):
        q = p_ref[:, 0:ATTN_DIM]
        k = p_ref[:, ATTN_DIM:ATTN_DIM + KV_DIM]
        v = p_ref[:, ATTN_DIM + KV_DIM:]
        rq = lax.rsqrt(_hdot(q * q, oq_ref[...]) * (1.0 / HEAD_DIM) + EPS)
        qn_ref[...] = (q * rq * gq_ref[...]) * (HEAD_DIM ** -0.5)
        rk = lax.rsqrt(_hdot(k * k, ok_ref[...]) * (1.0 / HEAD_DIM) + EPS)
        kn = k * rk * gk_ref[...]
        kd_ref[...] = _dot(kn.astype(BF16), dup_ref[...])
        vd_ref[...] = _dot(v.astype(BF16), dup_ref[...])

    full = lambda a: pl.BlockSpec(a.shape, lambda i: (0, 0))
    o_spec = pl.BlockSpec((tm, ATTN_DIM), lambda i: (i, 0))
    return pl.pallas_call(
        body, name="attn_prep", grid=(t // tm,),
        in_specs=[pl.BlockSpec((tm, 1024), lambda i: (i, 0)), full(gq), full(gk), full(ones_q), full(ones_k), full(dup)],
        out_specs=[o_spec, o_spec, o_spec], out_shape=[SDS((t, ATTN_DIM), F32)] * 3,
        compiler_params=_params("parallel"))(proj, gq, gk, ones_q, ones_k, dup)


def _attn_prep_bwd(proj, dqn, dkc, dkp, dvc, dvp, gq, gk, ones_q, ones_k, dup_t, tm=1024):
    t = proj.shape[0]
    nblk = t // tm
    off = SUPER // tm

    def body(p_ref, dqn_ref, dkc_ref, dkp_ref, dvc_ref, dvp_ref, gq_ref, gk_ref, oq_ref, ok_ref, dt_ref,
             o_ref, dgq_ref, dgk_ref):
        i = pl.program_id(0)
        has_next = (i + off < nblk).astype(F32)
        q = p_ref[:, 0:ATTN_DIM]
        k = p_ref[:, ATTN_DIM:ATTN_DIM + KV_DIM]
        dkn = _hdot(dkc_ref[...] + has_next * dkp_ref[...], dt_ref[...])
        dv = _hdot(dvc_ref[...] + has_next * dvp_ref[...], dt_ref[...])

        @pl.when(i == 0)
        def _():
            dgq_ref[...] = jnp.zeros_like(dgq_ref)
            dgk_ref[...] = jnp.zeros_like(dgk_ref)

        rq = lax.rsqrt(_hdot(q * q, oq_ref[...]) * (1.0 / HEAD_DIM) + EPS)
        xh = q * rq
        dy = dqn_ref[...] * (HEAD_DIM ** -0.5)
        dgq_ref[...] += jnp.sum(dy * xh, axis=0, keepdims=True)
        gd = dy * gq_ref[...]
        dq = rq * (gd - xh * (_hdot(gd * xh, oq_ref[...]) * (1.0 / HEAD_DIM)))
        rk = lax.rsqrt(_hdot(k * k, ok_ref[...]) * (1.0 / HEAD_DIM) + EPS)
        kh = k * rk
        dgk_ref[...] += jnp.sum(dkn * kh, axis=0, keepdims=True)
        gdk = dkn * gk_ref[...]
        dk = rk * (gdk - kh * (_hdot(gdk * kh, ok_ref[...]) * (1.0 / HEAD_DIM)))
        o_ref[:, 0:ATTN_DIM] = dq.astype(BF16)
        o_ref[:, ATTN_DIM:ATTN_DIM + KV_DIM] = dk.astype(BF16)
        o_ref[:, ATTN_DIM + KV_DIM:] = dv.astype(BF16)

    full = lambda a: pl.BlockSpec(a.shape, lambda i: (0, 0))
    cur = pl.BlockSpec((tm, ATTN_DIM), lambda i: (i, 0))
    nxt = pl.BlockSpec((tm, ATTN_DIM), lambda i: (jnp.minimum(i + off, nblk - 1), 0))
    return pl.pallas_call(
        body, name="attn_prep_bwd", grid=(nblk,),
        in_specs=[pl.BlockSpec((tm, 1024), lambda i: (i, 0)), cur, cur, nxt, cur, nxt,
                  full(gq), full(gk), full(ones_q), full(ones_k), full(dup_t)],
        out_specs=[pl.BlockSpec((tm, 1024), lambda i: (i, 0)), pl.BlockSpec((1, ATTN_DIM), lambda i: (0, 0)),
                   pl.BlockSpec((1, KV_DIM), lambda i: (0, 0))],
        out_shape=[SDS((t, 1024), BF16), SDS((1, ATTN_DIM), F32), SDS((1, KV_DIM), F32)],
        compiler_params=_params("arbitrary"))(proj, dqn, dkc, dkp, dvc, dvp, gq, gk, ones_q, ones_k, dup_t)


def _tile_masks():
    qi = lax.broadcasted_iota(jnp.int32, (2 * CHUNK, 2 * CHUNK), 0) & (CHUNK - 1)
    kj = lax.broadcasted_iota(jnp.int32, (2 * CHUNK, 2 * CHUNK), 1)
    delta = CHUNK + qi - kj
    band = (delta >= 0) & (delta <= CHUNK)
    return band, kj


def _attn_specs(t):
    blk = lambda f: pl.BlockSpec((SUPER, 128), f)
    cur = blk(lambda h, s: (s, h))
    prev = blk(lambda h, s: (jnp.maximum(s - 1, 0), h))
    return cur, prev


def _attn_fwd(qn, kd, vd):
    t = qn.shape[0]
    cur, prev = _attn_specs(t)

    def body(q_ref, kp_ref, kc_ref, vp_ref, vc_ref, o_ref, lse_ref, kk, vv, qsb, kb, vb, po, pm, pll, acc, mm, ll):
        s = pl.program_id(1)
        kk[0:SUPER, :] = kp_ref[...]
        kk[SUPER:, :] = kc_ref[...]
        vv[0:SUPER, :] = vp_ref[...]
        vv[SUPER:, :] = vc_ref[...]
        m0 = lax.broadcasted_iota(jnp.int32, (CHUNK, 128), 1) < HEAD_DIM
        band, kj = _tile_masks()
        for d in DILATIONS:
            lq = SUPER // d
            nblk = lq // CHUNK
            for r in range(d):
                seg = slice(r * 2 * lq, (r + 1) * 2 * lq)
                kb[seg, :] = kk[pl.ds(r, 2 * lq, stride=d), :].astype(BF16)
                vb[seg, :] = vv[pl.ds(r, 2 * lq, stride=d), :].astype(BF16)
            for ti in range(SUPER // CHUNK):
                a = q_ref[pl.ds(ti // nblk + d * CHUNK * (ti % nblk), CHUNK, stride=d), :]
                qsb[ti * 2 * CHUNK:(ti + 1) * 2 * CHUNK, :] = jnp.concatenate(
                    [jnp.where(m0, a, 0.0), jnp.where(m0, 0.0, a)], axis=0).astype(BF16)

            def key_rows(ti):
                return pl.ds((ti // nblk) * 2 * lq + lq + (ti % nblk - 1) * CHUNK, 2 * CHUNK)

            def scores(ti):
                return _dot(qsb[ti * 2 * CHUNK:(ti + 1) * 2 * CHUNK, :], kb[key_rows(ti), :], NT)

            def softmax_pv(ti, sc):
                ok = band if ti % nblk > 0 else band & (kj >= jnp.where(s > 0, 0, CHUNK))
                sc = jnp.where(ok, sc, -jnp.inf)
                mt = jnp.max(sc, axis=-1, keepdims=True)
                p = jnp.exp(sc - mt)
                lt = jnp.sum(p, axis=-1, keepdims=True)
                ot = _dot(p.astype(BF16), vb[key_rows(ti), :])
                qrows = pl.ds(ti * CHUNK, CHUNK)
                po[qrows, :] = jnp.where(m0, ot[:CHUNK], ot[CHUNK:])
                pm[qrows, :] = jnp.where(m0, mt[:CHUNK], mt[CHUNK:])
                pll[qrows, :] = jnp.where(m0, lt[:CHUNK], lt[CHUNK:])

            for ti in range(SUPER // CHUNK):
                softmax_pv(ti, scores(ti))
            if d == 1:
                acc[...] = po[...]
                mm[...] = pm[...]
                ll[...] = pll[...]
            else:
                for r in range(d):
                    rows = pl.ds(r, lq, stride=d)
                    seg = slice(r * lq, (r + 1) * lq)
                    m_old, m_new = mm[rows, :], pm[seg, :]
                    m_all = jnp.maximum(m_old, m_new)
                    a, b = jnp.exp(m_old - m_all), jnp.exp(m_new - m_all)
                    acc[rows, :] = acc[rows, :] * a + po[seg, :] * b
                    ll[rows, :] = ll[rows, :] * a + pll[seg, :] * b
                    mm[rows, :] = m_all
        o_ref[...] = acc[...] / ll[...]
        lse_ref[...] = mm[...] + jnp.log(ll[...])

    big = pltpu.VMEM((2 * SUPER, 128), F32)
    one = pltpu.VMEM((SUPER, 128), F32)
    half = pltpu.VMEM((2 * SUPER, 128), BF16)
    return pl.pallas_call(
        body, name="attn_fwd", grid=(4, t // SUPER),
        in_specs=[cur, prev, cur, prev, cur], out_specs=[cur, cur],
        out_shape=[SDS((t, ATTN_DIM), F32)] * 2,
        scratch_shapes=[big, big, half, half, half, one, one, one, one, one, one],
        compiler_params=_params("parallel", "arbitrary"))(qn, kd, kd, vd, vd)


def _attn_bwd(qn, kd, vd, out, lse, dout, ones_pair):
    t = qn.shape[0]
    cur, prev = _attn_specs(t)

    def body(q_ref, kp_ref, kc_ref, vp_ref, vc_ref, o_ref, lse_ref, do_ref, ones_ref,
             dq_ref, dkc_ref, dkp_ref, dvc_ref, dvp_ref,
             kk, vv, od, ld, kb, vb, qsb, dosb, tk, tv, pdq, delta):
        s = pl.program_id(1)
        delta[...] = _hdot(do_ref[...] * o_ref[...], ones_ref[...])

        def per_row(a):
            ar = pltpu.roll(a, HEAD_DIM, 1)
            rows = jnp.concatenate([jnp.where(m0, a, ar), jnp.where(m0, ar, a)], axis=0)
            return jnp.concatenate([rows, rows], axis=1)

        kk[0:SUPER, :] = kp_ref[...]
        kk[SUPER:, :] = kc_ref[...]
        vv[0:SUPER, :] = vp_ref[...]
        vv[SUPER:, :] = vc_ref[...]
        for ref in (dq_ref, dkc_ref, dkp_ref, dvc_ref, dvp_ref):
            ref[...] = jnp.zeros_like(ref)
        m0 = lax.broadcasted_iota(jnp.int32, (CHUNK, 128), 1) < HEAD_DIM
        band, kj = _tile_masks()
        ninf = -jnp.inf
        for d in DILATIONS:
            lq = SUPER // d
            nblk = lq // CHUNK
            for r in range(d):
                seg = slice(r * 2 * lq, (r + 1) * 2 * lq)
                kb[seg, :] = kk[pl.ds(r, 2 * lq, stride=d), :].astype(BF16)
                vb[seg, :] = vv[pl.ds(r, 2 * lq, stride=d), :].astype(BF16)
            for ti in range(SUPER // CHUNK):
                rows = pl.ds(ti // nblk + d * CHUNK * (ti % nblk), CHUNK, stride=d)
                for src, dst in ((q_ref, qsb), (do_ref, dosb)):
                    a = src[rows, :]
                    dst[ti * 2 * CHUNK:(ti + 1) * 2 * CHUNK, :] = jnp.concatenate(
                        [jnp.where(m0, a, 0.0), jnp.where(m0, 0.0, a)], axis=0).astype(BF16)
                ld[ti * CHUNK:(ti + 1) * CHUNK, :] = lse_ref[rows, :]
                od[ti * CHUNK:(ti + 1) * CHUNK, :] = delta[rows, :]

            def operands(ti):
                r, nb = ti // nblk, ti % nblk
                stacked = slice(ti * 2 * CHUNK, (ti + 1) * 2 * CHUNK)
                krows = pl.ds(r * 2 * lq + lq + (nb - 1) * CHUNK, 2 * CHUNK)
                return stacked, krows

            def scores(ti):
                stacked, krows = operands(ti)
                kt = kb[krows, :]
                return dict(ti=ti, sc=_dot(qsb[stacked, :], kt, NT), dp=_dot(dosb[stacked, :], vb[krows, :], NT))

            def softmax_grad(c):
                qrows = slice(c["ti"] * CHUNK, (c["ti"] + 1) * CHUNK)
                ok = band if c["ti"] % nblk > 0 else band & (kj >= jnp.where(s > 0, 0, CHUNK))
                p = jnp.exp(jnp.where(ok, c.pop("sc"), ninf) - per_row(ld[qrows, :]))
                ds = p * (c.pop("dp") - per_row(od[qrows, :]))
                c.update(p=p.astype(BF16), ds=ds.astype(BF16))
                return c

            def grads(c):
                ti = c["ti"]
                stacked, krows = operands(ti)
                dqs = _dot(c["ds"], kb[krows, :])
                pdq[ti * CHUNK:(ti + 1) * CHUNK, :] = jnp.where(m0, dqs[:CHUNK], dqs[CHUNK:])
                tk[stacked, :] = _dot(c["ds"], qsb[stacked, :], TN)
                tv[stacked, :] = _dot(c["p"], dosb[stacked, :], TN)

            n_tiles = SUPER // CHUNK
            stage_a = scores(0)
            for ti in range(n_tiles):
                ahead = scores(ti + 1) if ti + 1 < n_tiles else None
                grads(softmax_grad(stage_a))
                stage_a = ahead

            for r in range(d):
                dq_ref[pl.ds(r, lq, stride=d), :] += pdq[r * lq:(r + 1) * lq, :]
                for tile_out, cur_ref, prev_ref in ((tk, dkc_ref, dkp_ref), (tv, dvc_ref, dvp_ref)):
                    first = r * nblk * 2 * CHUNK
                    prev_ref[pl.ds(SUPER - CHUNK * d + r, CHUNK, stride=d), :] += tile_out[first:first + CHUNK, :]
                    for nb in range(nblk):
                        at = (r * nblk + nb) * 2 * CHUNK
                        part = tile_out[at + CHUNK:at + 2 * CHUNK, :]
                        if nb + 1 < nblk:
                            part = part + tile_out[at + 2 * CHUNK:at + 3 * CHUNK, :]
                        cur_ref[pl.ds(r + d * nb * CHUNK, CHUNK, stride=d), :] += part

    big = pltpu.VMEM((2 * SUPER, 128), F32)
    one = pltpu.VMEM((SUPER, 128), F32)
    half = pltpu.VMEM((2 * SUPER, 128), BF16)
    return pl.pallas_call(
        body, name="attn_bwd", grid=(4, t // SUPER),
        in_specs=[cur, prev, cur, prev, cur, cur, cur, cur, pl.BlockSpec((128, 128), lambda h, s: (0, 0))],
        out_specs=[cur] * 5, out_shape=[SDS((t, ATTN_DIM), F32)] * 5,
        scratch_shapes=[big, big, one, one, half, half, half, half, big, big, one, one],
        compiler_params=_params("parallel", "arbitrary"))(qn, kd, kd, vd, vd, out, lse, dout, ones_pair)


def _ssd_consts():
    tri = np.tril(np.ones((CHUNK, CHUNK), np.float32))
    expand = np.zeros((128, SSM_INNER), np.float32)
    for h in range(SSM_HEADS):
        expand[h, h * HEAD_DIM:(h + 1) * HEAD_DIM] = 1.0
    return jnp.asarray(tri, BF16), jnp.asarray(tri.T, BF16), jnp.asarray(expand, BF16), jnp.asarray(expand.T, BF16)


def _conv4(x, halo, w_ref, b_ref):
    acc = b_ref[...] + w_ref[3:4, :] * x
    for k in range(3):
        acc = acc + w_ref[k:k + 1, :] * _shift_down(x, halo, 3 - k)
    return acc


def _softplus(x):
    return jnp.maximum(x, 0.0) + jnp.log(1.0 + jnp.exp(-jnp.abs(x)))


def _ssd_common(pre_x, pre_b, dt_ref, dtb_ref, alog_ref, tri_ref, exp_ref):
    xa = pre_x * _sigmoid(pre_x)
    ba = pre_b * _sigmoid(pre_b)
    dtv = _softplus(dt_ref[...] + dtb_ref[...])
    a_neg = -jnp.exp(alog_ref[...])
    acum = _hdot(tri_ref[...], dtv * a_neg, parts=3)
    lam = jnp.exp(acum)
    gam = jnp.exp(acum[CHUNK - 1:CHUNK, :] - acum)
    dt_e = _hdot(dtv, exp_ref[...])
    lam_e = _hdot(lam, exp_ref[...])
    gam_e = _hdot(gam, exp_ref[...])
    return dict(pre_x=pre_x, pre_b=pre_b, xa=xa, ba=ba, dtv=dtv, a_neg=a_neg, acum=acum,
                dt_e=dt_e, lam_e=lam_e, gam_e=gam_e, xdt=xa * dt_e)


def _decay(acum_t, h, transposed):
    rb = jnp.broadcast_to(acum_t[h:h + 1, :], (CHUNK, CHUNK))
    ri = lax.broadcasted_iota(jnp.int32, (CHUNK, CHUNK), 0)
    ci = lax.broadcasted_iota(jnp.int32, (CHUNK, CHUNK), 1)
    if transposed:
        return jnp.exp(jnp.where(ci >= ri, rb - rb.T, -jnp.inf))
    return jnp.exp(jnp.where(ri >= ci, rb.T - rb, -jnp.inf))


SSD_STEP = 4 * CHUNK


def _ssd_specs(t, rev):
    nc = t // SSD_STEP
    ch = (lambda c: nc - 1 - c) if rev else (lambda c: c)
    col = lambda w, j: pl.BlockSpec((SSD_STEP, w), lambda c: (ch(c), j))
    halo = lambda w, j: pl.BlockSpec((8, w), lambda c: (jnp.maximum(ch(c) * (SSD_STEP // 8) - 1, 0), j))
    return nc, ch, col, halo


def _ssd_fwd(proj, cwx, cbx, cwb, cbb, dtb, alog, dsk_e, norm_g, tri, expand):
    t = proj.shape[0]
    nc, _, col, halo = _ssd_specs(t, False)

    def body(z_all, xs_all, bc_all, dt_all, hx_ref, hb_ref, cwx_ref, cbx_ref, cwb_ref, cbb_ref, dtb_ref, alog_ref,
             dsk_ref, g_ref, tri_ref, exp_ref, y_all, hs_all, o_all, px_all, pb_all, state):
        @pl.when(pl.program_id(0) == 0)
        def _():
            state[...] = jnp.zeros_like(state)

        keep = (pl.program_id(0) > 0).astype(F32)
        for sc in range(SSD_STEP // CHUNK):
            rows = pl.ds(sc * CHUNK, CHUNK)
            before = pl.ds(sc * CHUNK - 8, 8)
            hx = hx_ref[...] * keep if sc == 0 else xs_all[before, :]
            hb = hb_ref[...] * keep if sc == 0 else bc_all[before, :]
            chunk(z_all.at[rows], xs_all.at[rows], bc_all.at[rows], dt_all.at[rows], hx, hb, cwx_ref, cbx_ref, cwb_ref,
                  cbb_ref, dtb_ref, alog_ref, dsk_ref, g_ref, tri_ref, exp_ref, y_all.at[rows],
                  hs_all.at[pl.ds(sc, 1)], o_all.at[rows], px_all.at[rows], pb_all.at[rows], state)

    def chunk(z_ref, xs_ref, bc_ref, dt_ref, hx, hb, cwx_ref, cbx_ref, cwb_ref, cbb_ref, dtb_ref, alog_ref,
              dsk_ref, g_ref, tri_ref, exp_ref, y_ref, hs_ref, o_ref, px_ref, pb_ref, state):
        pre_x = _conv4(xs_ref[...], hx, cwx_ref, cbx_ref)
        pre_b = _conv4(bc_ref[...], hb, cwb_ref, cbb_ref)
        px_ref[...] = pre_x.astype(BF16)
        pb_ref[...] = pre_b.astype(BF16)
        v = _ssd_common(pre_x, pre_b, dt_ref, dtb_ref, alog_ref, tri_ref, exp_ref)
        acum_t = v["acum"].T
        xdt, ba = v["xdt"], v["ba"]
        h_in = state[...]
        hs_ref[0] = h_in
        xg = xdt * v["gam_e"]
        m0 = lax.broadcasted_iota(jnp.int32, (CHUNK, 128), 1) < HEAD_DIM
        for g in range(2):
            bg = ba[:, g * 128:(g + 1) * 128].astype(BF16)
            cg = ba[:, 256 + g * 128:256 + (g + 1) * 128].astype(BF16)
            gl = slice(g * 512, (g + 1) * 512)
            cb = _dot(cg, bg, NT)
            y_off = _dot(cg, h_in[:, gl].astype(BF16)) * v["lam_e"][:, gl]
            s_new = _dot(bg.T, xg[:, gl].astype(BF16))
            state[:, gl] = h_in[:, gl] * v["lam_e"][CHUNK - 1:CHUNK, gl] + s_new
            for j in range(4):
                h0 = 8 * g + 2 * j
                ln = slice(g * 512 + j * 128, g * 512 + (j + 1) * 128)
                xp = xdt[:, ln].astype(BF16)
                y0 = _dot((cb * _decay(acum_t, h0, False)).astype(BF16), xp)
                y1 = _dot((cb * _decay(acum_t, h0 + 1, False)).astype(BF16), xp)
                y_ref[:, ln] = jnp.where(m0, y0, y1) + y_off[:, j * 128:(j + 1) * 128]
        z = z_ref[...]
        yg = (y_ref[...] + dsk_ref[...] * v["xa"]) * (z * _sigmoid(z))
        r = lax.rsqrt(jnp.mean(yg * yg, axis=-1, keepdims=True) + EPS)
        o_ref[...] = (yg * r * g_ref[...]).astype(BF16)

    full = lambda a: pl.BlockSpec(a.shape, lambda c: (0,) * a.ndim)
    return pl.pallas_call(
        body, name="ssd_fwd", grid=(nc,),
        in_specs=[col(1024, 1), col(1024, 2), col(512, 6), col(128, 28), halo(1024, 2), halo(512, 6),
                  full(cwx), full(cbx), full(cwb), full(cbb), full(dtb), full(alog), full(dsk_e), full(norm_g),
                  full(tri), full(expand)],
        out_specs=[pl.BlockSpec((SSD_STEP, SSM_INNER), lambda c: (c, 0)),
                   pl.BlockSpec((SSD_STEP // CHUNK, 128, SSM_INNER), lambda c: (c, 0, 0)),
                   pl.BlockSpec((SSD_STEP, SSM_INNER), lambda c: (c, 0)),
                   pl.BlockSpec((SSD_STEP, SSM_INNER), lambda c: (c, 0)), pl.BlockSpec((SSD_STEP, 512), lambda c: (c, 0))],
        out_shape=[SDS((t, SSM_INNER), F32), SDS((t // CHUNK, 128, SSM_INNER), F32), SDS((t, SSM_INNER), BF16),
                   SDS((t, SSM_INNER), BF16), SDS((t, 512), BF16)],
        scratch_shapes=[pltpu.VMEM((128, SSM_INNER), F32)],
        compiler_params=_params("arbitrary"))(proj, proj, proj, proj, proj, proj, cwx, cbx, cwb, cbb, dtb, alog,
                                              dsk_e, norm_g, tri, expand)


def _ssd_bwd(proj, pre_x, pre_b, y_ssd, hs, dout, cwx, cwb, dtb, alog, dsk_e, norm_g, tri, triu, expand, expand_t):
    t = proj.shape[0]
    nc, ch, col, halo = _ssd_specs(t, True)

    def body(z_all, xs_all, bc_all, dt_all, px_all, pb_all, y_all, hin_all, do_all,
             cwx_ref, cwb_ref, dtb_ref, alog_ref, dsk_ref, g_ref, tri_ref, triu_ref, exp_ref, expt_ref,
             dz_all, dxs_all, dbc_all, ddt_all, dg_ref, ddsk_ref, dalog_ref, ddtb_ref, dcwx_ref, dcbx_ref, dcwb_ref,
             dcbb_ref, gstate, nx_x, nx_b, dact_b, dxdt_s):
        @pl.when(pl.program_id(0) == 0)
        def _():
            gstate[...] = jnp.zeros_like(gstate)
            nx_x[...] = jnp.zeros_like(nx_x)
            nx_b[...] = jnp.zeros_like(nx_b)
            for ref in (dg_ref, ddsk_ref, dalog_ref, ddtb_ref, dcwx_ref, dcbx_ref, dcwb_ref, dcbb_ref):
                ref[...] = jnp.zeros_like(ref)

        for sc in reversed(range(SSD_STEP // CHUNK)):
            rows = pl.ds(sc * CHUNK, CHUNK)
            by_rows = [r.at[rows] for r in (z_all, xs_all, bc_all, dt_all, px_all, pb_all, y_all)]
            outs = [r.at[rows] for r in (dz_all, dxs_all, dbc_all, ddt_all)]
            chunk(*by_rows, hin_all.at[pl.ds(sc, 1)], do_all.at[rows],
                  cwx_ref, cwb_ref, dtb_ref, alog_ref, dsk_ref, g_ref, tri_ref, triu_ref, exp_ref, expt_ref,
                  *outs, dg_ref, ddsk_ref, dalog_ref, ddtb_ref, dcwx_ref, dcbx_ref, dcwb_ref, dcbb_ref,
                  gstate, nx_x, nx_b, dact_b, dxdt_s)

    def chunk(z_ref, xs_ref, bc_ref, dt_ref, px_ref, pb_ref, y_ref, hin_ref, do_ref,
              cwx_ref, cwb_ref, dtb_ref, alog_ref, dsk_ref, g_ref, tri_ref, triu_ref, exp_ref, expt_ref,
              dz_ref, dxs_ref, dbc_ref, ddt_ref, dg_ref, ddsk_ref, dalog_ref, ddtb_ref, dcwx_ref, dcbx_ref, dcwb_ref,
              dcbb_ref, gstate, nx_x, nx_b, dact_b, dxdt_s):
        v = _ssd_common(px_ref[...].astype(F32), pb_ref[...].astype(F32), dt_ref, dtb_ref, alog_ref, tri_ref, exp_ref)
        acum_t = v["acum"].T
        xa, ba, xdt, dtv = v["xa"], v["ba"], v["xdt"], v["dtv"]
        lam_e, gam_e, dt_e = v["lam_e"], v["gam_e"], v["dt_e"]
        z = z_ref[...]
        y = y_ref[...]
        sz = _sigmoid(z)
        zs = z * sz
        y_tot = y + dsk_ref[...] * xa
        yg = y_tot * zs
        r = lax.rsqrt(jnp.mean(yg * yg, axis=-1, keepdims=True) + EPS)
        yh = yg * r
        do = do_ref[...]
        dg_ref[...] += jnp.sum(do * yh, axis=0, keepdims=True)
        gd = do * g_ref[...]
        dyg = r * (gd - yh * jnp.mean(gd * yh, axis=-1, keepdims=True))
        dz_ref[...] = (dyg * y_tot * (sz * (1.0 + z * (1.0 - sz)))).astype(BF16)
        dy = dyg * zs
        ddsk_ref[...] += jnp.sum(dy * xa, axis=0, keepdims=True)
        g_out = gstate[...]
        h_in = hin_ref[0]
        lam_dy = lam_e * dy
        gam_x = gam_e * xdt
        m0 = lax.broadcasted_iota(jnp.int32, (CHUNK, 128), 1) < HEAD_DIM
        lane = lax.broadcasted_iota(jnp.int32, (CHUNK, 128), 1)
        below = (lax.broadcasted_iota(jnp.int32, (CHUNK, CHUNK), 0) >
                 lax.broadcasted_iota(jnp.int32, (CHUNK, CHUNK), 1))
        da_in = jnp.zeros((CHUNK, 128), F32)
        off_y, off_x = [], []
        for g in range(2):
            bg = ba[:, g * 128:(g + 1) * 128].astype(BF16)
            cg = ba[:, 256 + g * 128:256 + (g + 1) * 128].astype(BF16)
            gl = slice(g * 512, (g + 1) * 512)
            gg = g_out[:, gl].astype(BF16)
            cb = _dot(cg, bg, NT)
            dxdt_off = _dot(bg, gg) * gam_e[:, gl]
            off_x.append(xdt[:, gl] * dxdt_off)
            off_y.append(dy[:, gl] * (_dot(cg, h_in[:, gl].astype(BF16)) * lam_e[:, gl]))
            q_sum = jnp.zeros((CHUNK, CHUNK), F32)
            for j in range(4):
                h0 = 8 * g + 2 * j
                ln = slice(g * 512 + j * 128, g * 512 + (j + 1) * 128)
                dyp = dy[:, ln]
                dyb = dyp.astype(BF16)
                xpb = xdt[:, ln].astype(BF16)
                dec = [_decay(acum_t, h0, False), _decay(acum_t, h0 + 1, False)]
                mix = [cb * dec[0], cb * dec[1]]
                d0 = _dot(mix[0].T.astype(BF16), dyb)
                d1 = _dot(mix[1].T.astype(BF16), dyb)
                dxdt_s[:, ln] = jnp.where(m0, d0, d1) + dxdt_off[:, j * 128:(j + 1) * 128]
                for e, (hh, dym) in enumerate(((h0, jnp.where(m0, dyp, 0.0)), (h0 + 1, jnp.where(m0, 0.0, dyp)))):
                    dyx = _dot(dym.astype(BF16), xpb, NT)
                    q_sum = q_sum + dyx * dec[e]
                    reach = jnp.where(below, _hdot(triu_ref[...], dyx * mix[e]), 0.0)
                    da_in = jnp.where(lane == hh, jnp.sum(reach, axis=-1, keepdims=True), da_in)
            gstate[:, gl] = g_out[:, gl] * lam_e[CHUNK - 1:CHUNK, gl] + _dot(cg.T, lam_dy[:, gl].astype(BF16))
            qb = q_sum.astype(BF16)
            dact_b[:, 256 + g * 128:256 + (g + 1) * 128] = (
                _dot(qb, bg) + _dot(lam_dy[:, gl].astype(BF16), h_in[:, gl].astype(BF16), NT))
            dact_b[:, g * 128:(g + 1) * 128] = _dot(qb.T, cg) + _dot(gam_x[:, gl].astype(BF16), gg, NT)
        dxdt = dxdt_s[...]
        seg_y = _hdot(jnp.concatenate(off_y, axis=1), expt_ref[...])
        seg_x = _hdot(jnp.concatenate(off_x, axis=1), expt_ref[...])
        e_col = jnp.sum(g_out * h_in * lam_e[CHUNK - 1:CHUNK, :], axis=0, keepdims=True)
        e_seg = _hdot(jnp.broadcast_to(e_col, (8, SSM_INNER)), expt_ref[...])[0:1, :]
        da = da_in + _hdot(triu_ref[...], seg_y) + (_hdot(tri_ref[...], seg_x) - seg_x) + e_seg
        a_neg = v["a_neg"]
        ddtv = da * a_neg + _hdot(dxdt * xa, expt_ref[...])
        dalog_ref[...] += jnp.sum(da * dtv, axis=0, keepdims=True) * a_neg
        lane16 = lax.broadcasted_iota(jnp.int32, (CHUNK, 128), 1) < SSM_HEADS
        draw = jnp.where(lane16, ddtv * _sigmoid(dt_ref[...] + dtb_ref[...]), 0.0)
        ddtb_ref[...] += jnp.sum(draw, axis=0, keepdims=True)
        ddt_ref[...] = draw.astype(BF16)
        dxa = dxdt * dt_e + dy * dsk_ref[...]
        for (dact, pre, x_ref, nx, cw_ref, dcw_ref, dcb_ref, dx_ref) in (
                (dxa, v["pre_x"], xs_ref, nx_x, cwx_ref, dcwx_ref, dcbx_ref, dxs_ref),
                (dact_b[...], v["pre_b"], bc_ref, nx_b, cwb_ref, dcwb_ref, dcbb_ref, dbc_ref)):
            sp = _sigmoid(pre)
            dpre = dact * (sp * (1.0 + pre * (1.0 - sp)))
            dcb_ref[...] += jnp.sum(dpre, axis=0, keepdims=True)
            xv = x_ref[...]
            nxt = nx[...]
            dx = cw_ref[3:4, :] * dpre
            dcw_ref[3:4, :] += jnp.sum(dpre * xv, axis=0, keepdims=True)
            for k in range(3):
                d_up = _shift_up(dpre, nxt, 3 - k)
                dcw_ref[k:k + 1, :] += jnp.sum(xv * d_up, axis=0, keepdims=True)
                dx = dx + cw_ref[k:k + 1, :] * d_up
            nx[...] = dpre[0:8, :]
            dx_ref[...] = dx.astype(dx_ref.dtype)

    full = lambda a: pl.BlockSpec(a.shape, lambda c: (0,) * a.ndim)
    rowblk = lambda w: pl.BlockSpec((SSD_STEP, w), lambda c: (ch(c), 0))
    acc = lambda a, b: pl.BlockSpec((a, b), lambda c: (0, 0))
    return pl.pallas_call(
        body, name="ssd_bwd", grid=(nc,),
        in_specs=[col(1024, 1), col(1024, 2), col(512, 6), col(128, 28), rowblk(SSM_INNER), rowblk(512),
                  rowblk(SSM_INNER),
                  pl.BlockSpec((SSD_STEP // CHUNK, 128, SSM_INNER), lambda c: (ch(c), 0, 0)),
                  rowblk(SSM_INNER),
                  full(cwx), full(cwb), full(dtb), full(alog), full(dsk_e), full(norm_g),
                  full(tri), full(triu), full(expand), full(expand_t)],
        out_specs=[rowblk(SSM_INNER), rowblk(SSM_INNER), rowblk(512), rowblk(128),
                   acc(1, 1024), acc(1, 1024), acc(1, 128), acc(1, 128), acc(4, 1024), acc(1, 1024), acc(4, 512),
                   acc(1, 512)],
        out_shape=[SDS((t, SSM_INNER), BF16), SDS((t, SSM_INNER), BF16), SDS((t, 512), BF16), SDS((t, 128), BF16),
                   SDS((1, 1024), F32), SDS((1, 1024), F32), SDS((1, 128), F32), SDS((1, 128), F32),
                   SDS((4, 1024), F32), SDS((1, 1024), F32), SDS((4, 512), F32), SDS((1, 512), F32)],
        scratch_shapes=[pltpu.VMEM((128, SSM_INNER), F32), pltpu.VMEM((8, 1024), F32), pltpu.VMEM((8, 512), F32),
                        pltpu.VMEM((CHUNK, 512), F32), pltpu.VMEM((CHUNK, SSM_INNER), F32)],
        compiler_params=_params("arbitrary"))(proj, proj, proj, proj, pre_x, pre_b, y_ssd, hs, dout,
                                              cwx, cwb, dtb, alog, dsk_e, norm_g, tri, triu, expand, expand_t)


def _conv3(x, halo, w_ref, b_ref, part):
    acc = b_ref[part] + w_ref[2, part] * x
    for k in range(2):
        acc = acc + w_ref[k, part] * _shift_down(x, halo, 2 - k)
    return acc


def _up_act(x, g, w_up_t, cw, cb, tm=2048, tn=256, tr=512):
    t, k = x.shape
    nj = D_FF // tn

    def body(x_ref, g_ref, wg_ref, wv_ref, w_ref, b_ref, u_ref, c_ref, h_ref, f_ref, halo):
        i, j = pl.program_id(0), pl.program_id(1)

        @pl.when(j == 0)
        def _():
            xv = x_ref[...]
            r = lax.rsqrt(jnp.mean(xv * xv, axis=-1, keepdims=True) + EPS)
            h_ref[...] = (xv * r * g_ref[...]).astype(BF16)

        @pl.when(i == 0)
        def _():
            halo[j] = jnp.zeros((2, 8, tn), F32)

        def matmuls(r):
            rows = slice(r * tr, (r + 1) * tr)
            return [_dot(h_ref[rows, :], wt_ref[...], NT) for wt_ref in (wg_ref, wv_ref)]

        def epilogue(r, us, before):
            rows = slice(r * tr, (r + 1) * tr)
            parts = []
            for part, u in enumerate(us):
                u_ref[part, rows, :] = u.astype(BF16)
                parts.append(_conv3(u, before[part], w_ref, b_ref, part))
                c_ref[part, rows, :] = parts[-1].astype(BF16)
            gate, val = parts
            f_ref[rows, :] = (gate * _sigmoid(gate) * val).astype(BF16)
            return [u[tr - 8:, :] for u in us]

        before = [halo[j, 0], halo[j, 1]]
        pending = matmuls(0)
        for r in range(tm // tr):
            ahead = matmuls(r + 1) if r + 1 < tm // tr else None
            before = epilogue(r, pending, before)
            pending = ahead
        halo[j, 0], halo[j, 1] = before

    return pl.pallas_call(
        body, name="up_proj", grid=(t // tm, nj),
        in_specs=[pl.BlockSpec((tm, k), lambda i, j: (i, 0)), pl.BlockSpec((1, k), lambda i, j: (0, 0)),
                  pl.BlockSpec((tn, k), lambda i, j: (j, 0)), pl.BlockSpec((tn, k), lambda i, j: (j + nj, 0)),
                  pl.BlockSpec((3, 2, 1, tn), lambda i, j: (0, 0, 0, j)), pl.BlockSpec((2, 1, tn), lambda i, j: (0, 0, j))],
        out_specs=[pl.BlockSpec((2, tm, tn), lambda i, j: (0, i, j)), pl.BlockSpec((2, tm, tn), lambda i, j: (0, i, j)),
                   pl.BlockSpec((tm, k), lambda i, j: (i, 0)), pl.BlockSpec((tm, tn), lambda i, j: (i, j))],
        out_shape=[SDS((2, t, D_FF), BF16), SDS((2, t, D_FF), BF16), SDS((t, k), BF16), SDS((t, D_FF), BF16)],
        scratch_shapes=[pltpu.VMEM((nj, 2, 8, tn), F32)],
        compiler_params=_params("arbitrary", "arbitrary"))(x, g, w_up_t, w_up_t, cw, cb)


def _ffn_bwd(dx2, w_down, u, c, cw, tm=512, tn=1408):
    t = u.shape[1]
    nj, ni = D_FF // tn, t // tm
    rev = lambda i: ni - 1 - i

    def body(dx_ref, wd_ref, u_ref, c_ref, w_ref, du_ref, dcw_ref, dcb_ref, nxt):
        i = pl.program_id(1)

        @pl.when(i == 0)
        def _():
            nxt[...] = jnp.zeros_like(nxt)
            dcw_ref[...] = jnp.zeros_like(dcw_ref)
            dcb_ref[...] = jnp.zeros_like(dcb_ref)

        df = _dot(dx_ref[...].astype(BF16), wd_ref[...], NT)
        gate, val = c_ref[0].astype(F32), c_ref[1].astype(F32)
        sg = _sigmoid(gate)
        dgate = df * val * (sg * (1.0 + gate * (1.0 - sg)))
        dval = df * (gate * sg)
        for part, d in enumerate((dgate, dval)):
            uu = u_ref[part].astype(F32)
            dcb_ref[part] += jnp.sum(d, axis=0, keepdims=True)
            ahead = nxt[part]
            acc = w_ref[2, part] * d
            dcw_ref[2, part] += jnp.sum(d * uu, axis=0, keepdims=True)
            for k in range(2):
                d_up = _shift_up(d, ahead, 2 - k)
                dcw_ref[k, part] += jnp.sum(uu * d_up, axis=0, keepdims=True)
                acc = acc + w_ref[k, part] * d_up
            nxt[part] = d[0:8, :]
            du_ref[part] = acc.astype(BF16)

    w_spec = pl.BlockSpec((3, 2, 1, tn), lambda j, i: (0, 0, 0, j))
    b_spec = pl.BlockSpec((2, 1, tn), lambda j, i: (0, 0, j))
    tile = pl.BlockSpec((2, tm, tn), lambda j, i: (0, rev(i), j))
    return pl.pallas_call(
        body, name="ffn_bwd", grid=(nj, ni),
        in_specs=[pl.BlockSpec((tm, D_MODEL), lambda j, i: (rev(i), 0)), pl.BlockSpec((tn, D_MODEL), lambda j, i: (j, 0)),
                  tile, tile, w_spec],
        out_specs=[tile, w_spec, b_spec],
        out_shape=[SDS((2, t, D_FF), BF16), SDS((3, 2, 1, D_FF), F32), SDS((2, 1, D_FF), F32)],
        scratch_shapes=[pltpu.VMEM((2, 8, tn), F32)],
        compiler_params=_params("parallel", "arbitrary"))(dx2, w_down, u, c, cw)


def _down_ple_loss(x1, f, w_down, g, w_gate, p, w_proj_t, target, tm=512):
    t = x1.shape[0]

    def body(x_ref, f_ref, wd_ref, g_ref, wg_ref, p_ref, wp_ref, tg_ref, dx_ref, dpre_ref, dpp_ref, h_ref, loss_ref,
             dg_ref):
        i = pl.program_id(0)
        xv = x_ref[...] + _dot(f_ref[...], wd_ref[...])
        r = lax.rsqrt(jnp.mean(xv * xv, axis=-1, keepdims=True) + EPS)
        xh = xv * r
        h = (xh * g_ref[...]).astype(BF16)
        h_ref[...] = h
        gate = _sigmoid(_dot(h, wg_ref[...]))
        pp = _dot(p_ref[...].astype(BF16), wp_ref[...], NT)
        err = (xv + gate * pp) - tg_ref[...]

        @pl.when(i == 0)
        def _():
            loss_ref[...] = jnp.zeros_like(loss_ref)
            dg_ref[...] = jnp.zeros_like(dg_ref)

        loss_ref[...] += 0.5 * jnp.sum(jnp.mean(err * err, axis=-1, keepdims=True), axis=0, keepdims=True)
        dy = err * (1.0 / D_MODEL)
        dpre = (dy * pp * gate * (1.0 - gate)).astype(BF16)
        dpre_ref[...] = dpre
        dpp_ref[...] = (dy * gate).astype(BF16)
        dh = _dot(dpre, wg_ref[...], NT)
        dg_ref[...] += jnp.sum(dh * xh, axis=0, keepdims=True)
        gd = dh * g_ref[...]
        dx_ref[...] = dy + r * (gd - xh * jnp.mean(gd * xh, axis=-1, keepdims=True))

    row = lambda w: pl.BlockSpec((tm, w), lambda i: (i, 0))
    full = lambda a: pl.BlockSpec(a.shape, lambda i: (0, 0))
    return pl.pallas_call(
        body, name="down_ple_loss", grid=(t // tm,),
        in_specs=[row(D_MODEL), row(D_FF), full(w_down), full(g), full(w_gate), row(PLE_DIM), full(w_proj_t),
                  row(D_MODEL)],
        out_specs=[row(D_MODEL), row(D_MODEL), row(D_MODEL), row(D_MODEL),
                   pl.BlockSpec((1, 128), lambda i: (0, 0)), pl.BlockSpec((1, D_MODEL), lambda i: (0, 0))],
        out_shape=[SDS((t, D_MODEL), F32), SDS((t, D_MODEL), BF16), SDS((t, D_MODEL), BF16), SDS((t, D_MODEL), BF16),
                   SDS((1, 128), F32), SDS((1, D_MODEL), F32)],
        compiler_params=_params("arbitrary"))(x1, f, w_down, g, w_gate, p, w_proj_t, target)


def _all_gather(arrays, name):
    n_a = len(arrays)

    def body(*refs):
        src, dst = refs[:n_a], refs[n_a:2 * n_a]
        send_sems, recv_sems, local_sems = refs[2 * n_a:]
        x, y, c = lax.axis_index("x"), lax.axis_index("y"), lax.axis_index("c")
        slot = lambda px, py, pc: 4 * px + 2 * py + pc
        me, sibling = (x, y, c), (x, y, 1 - c)
        chips = [(1 - x, y), (x, 1 - y), (1 - x, 1 - y)]

        def copy(a, k, block, to, own=False):
            return pltpu.make_async_remote_copy(
                src_ref=src[a] if own else dst[a].at[slot(*block)], dst_ref=dst[a].at[slot(*block)],
                send_sem=send_sems.at[a, k], recv_sem=recv_sems.at[a, k], device_id=to,
                device_id_type=pl.DeviceIdType.MESH)

        local = [pltpu.make_async_copy(src[a], dst[a].at[slot(*me)], local_sems.at[a]) for a in range(n_a)]
        for cp in local:
            cp.start()
        sends = []
        for a in range(n_a):
            sends.append(copy(a, 0, me, sibling, own=True))
            sends += [copy(a, 1 + j, me, (*chip, c), own=True) for j, chip in enumerate(chips)]
        for cp in sends:
            cp.start()
        for j, chip in enumerate(chips):
            for a in range(n_a):
                copy(a, 1 + j, (*chip, c), me).wait_recv()
                passed = copy(a, 4 + j, (*chip, c), sibling)
                passed.start()
                sends.append(passed)
        for a in range(n_a):
            copy(a, 0, sibling, me).wait_recv()
            for j, chip in enumerate(chips):
                copy(a, 4 + j, (*chip, 1 - c), me).wait_recv()
        for cp in sends:
            cp.wait_send()
        for cp in local:
            cp.wait()

    hbm = pl.BlockSpec(memory_space=pl.ANY)
    return pl.pallas_call(
        body, name=name, in_specs=[hbm] * n_a, out_specs=[hbm] * n_a,
        out_shape=[SDS((N_DEV,) + a.shape, a.dtype) for a in arrays],
        scratch_shapes=[pltpu.SemaphoreType.DMA((n_a, N_DEV - 1)), pltpu.SemaphoreType.DMA((n_a, N_DEV - 1)),
                        pltpu.SemaphoreType.DMA((n_a,))],
        )(*arrays)


def _peer(k):
    x, y, c = lax.axis_index("x"), lax.axis_index("y"), lax.axis_index("c")
    px = 1 - x if k & 4 else x
    py = 1 - y if k & 2 else y
    pc = 1 - c if k & 1 else c
    return (px, py, pc), 4 * px + 2 * py + pc


_HBM = pl.BlockSpec(memory_space=pltpu.HBM)
_SEM = pl.BlockSpec(memory_space=pltpu.SEMAPHORE)


def _split_copies(src, land, send_sems, recv_sems, scatter, arrivals):
    _, me = _peer(0)
    out = []
    for k in range(1, N_DEV):
        coords, peer = _peer(k)
        for a in range(len(src)):
            sem = a * (N_DEV - 1) + k - 1
            if scatter[a]:
                s, d = src[a].at[peer], land[a].at[k]
            else:
                s, d = src[a], land[a].at[peer if arrivals else me]
            out.append(pltpu.make_async_remote_copy(
                src_ref=s, dst_ref=d, send_sem=send_sems.at[sem], recv_sem=recv_sems.at[sem], device_id=coords,
                device_id_type=pl.DeviceIdType.MESH))
    return out


def _exchange_start(srcs, lands, scatter, name):
    n = len(srcs)

    def body(*refs):
        src, land = refs[:n], refs[n:2 * n]
        send_sems, recv_sems = refs[2 * n], refs[2 * n + 1]
        token = refs[-1]
        for cp in _split_copies(src, land, send_sems, recv_sems, scatter, False):
            cp.start()
        token[...] = jnp.zeros_like(token)

    hbm_shape = lambda a: pltpu.HBM(a.shape, a.dtype)
    sem_shape = pltpu.SemaphoreType.DMA((n * (N_DEV - 1),))
    outs = pl.pallas_call(
        body, name=name,
        out_shape=(sem_shape, sem_shape, *[hbm_shape(a) for a in srcs], *[hbm_shape(a) for a in lands],
                   SDS((8, 128), F32)),
        in_specs=[_HBM] * (2 * n), out_specs=(_SEM, _SEM, *[_HBM] * (2 * n), pl.BlockSpec(memory_space=pltpu.VMEM)),
        input_output_aliases={a: 2 + a for a in range(2 * n)},
        compiler_params=pltpu.CompilerParams(has_side_effects=pltpu.SideEffectType.DATAFLOW_SIDE_EFFECTING),
    )(*[pltpu.with_memory_space_constraint(a, pltpu.HBM) for a in list(srcs) + list(lands)])
    return outs[0], outs[1], outs[2:2 + n], outs[2 + n:2 + 2 * n], outs[-1]


def _exchange_wait(send_sems, recv_sems, srcs, lands, scatter, after, name):
    n = len(srcs)

    def body(*refs):
        src, land = refs[:n], refs[n:2 * n]
        for cp in _split_copies(src, land, refs[2 * n], refs[2 * n + 1], scatter, False):
            cp.wait_send()
        for cp in _split_copies(src, land, refs[2 * n], refs[2 * n + 1], scatter, True):
            cp.wait_recv()

    hbm_shape = lambda a: pltpu.HBM(a.shape, a.dtype)
    outs = pl.pallas_call(
        body, name=name, out_shape=tuple(hbm_shape(a) for a in list(srcs) + list(lands)),
        in_specs=[_HBM] * (2 * n) + [_SEM, _SEM, pl.BlockSpec(memory_space=pl.ANY)], out_specs=(_HBM,) * (2 * n),
        input_output_aliases={a: a for a in range(2 * n)},
        compiler_params=pltpu.CompilerParams(has_side_effects=pltpu.SideEffectType.DATAFLOW_SIDE_EFFECTING),
    )(*srcs, *lands, send_sems, recv_sems, after)
    return outs[:n], outs[n:]


def _reduce8(a, tr, name):
    _, rows, cols = a.shape

    def body(a_ref, o_ref):
        acc = a_ref[0]
        for j in range(1, N_DEV):
            acc = acc + a_ref[j]
        o_ref[...] = acc

    return pl.pallas_call(
        body, name=name, grid=(rows // tr,),
        in_specs=[pl.BlockSpec((N_DEV, tr, cols), lambda i: (0, i, 0))],
        out_specs=pl.BlockSpec((tr, cols), lambda i: (i, 0)), out_shape=SDS((rows, cols), F32),
        compiler_params=_params("parallel"))(a)


def _reduce_landed(own, land, name, tc=256):
    rows, cols = own.shape

    def body(own_ref, land_ref, o_ref):
        acc = own_ref[...].astype(F32)
        for k in range(1, N_DEV):
            acc = acc + land_ref[k].astype(F32)
        o_ref[...] = acc

    return pl.pallas_call(
        body, name=name, grid=(cols // tc,),
        in_specs=[pl.BlockSpec((rows, tc), lambda j: (0, j)), pl.BlockSpec((N_DEV, rows, tc), lambda j: (0, 0, j))],
        out_specs=pl.BlockSpec((rows, tc), lambda j: (0, j)), out_shape=SDS((rows, cols), F32),
        compiler_params=_params("parallel"))(own, land)


def _reduce_adamw(own, land, w, m, v, name, tc=256):
    rows, cols = own.shape

    def body(own_ref, land_ref, w_ref, m_ref, v_ref, g_ref, d_ref, mo_ref, vo_ref):
        g = own_ref[...].astype(F32)
        for k in range(1, N_DEV):
            g = g + land_ref[k].astype(F32)
        g_ref[...] = g
        d_ref[...], mo_ref[...], vo_ref[...] = _adam_update(w_ref[...], g, m_ref[...], v_ref[...])

    blk = pl.BlockSpec((rows, tc), lambda j: (0, j))
    return pl.pallas_call(
        body, name=name, grid=(cols // tc,),
        in_specs=[blk, pl.BlockSpec((N_DEV, rows, tc), lambda j: (0, 0, j)), blk, blk, blk], out_specs=[blk] * 4,
        out_shape=[SDS((rows, cols), F32)] * 4, compiler_params=_params("parallel"))(own, land, w, m, v)


def _adamw(w, g, m, v, name, tr=None):
    rows, cols = w.shape
    tr = rows if tr is None else tr

    def body(w_ref, g_ref, m_ref, v_ref, d_ref, mo_ref, vo_ref):
        d_ref[...], mo_ref[...], vo_ref[...] = _adam_update(w_ref[...], g_ref[...], m_ref[...], v_ref[...])

    blk = pl.BlockSpec((tr, cols), lambda i: (i, 0))
    return pl.pallas_call(
        body, name=name, grid=(rows // tr,), in_specs=[blk] * 4, out_specs=[blk] * 3,
        out_shape=[SDS((rows, cols), F32)] * 3, compiler_params=_params("parallel"))(w, g, m, v)


def _pad_rows(a, rows):
    return jnp.pad(a, ((0, rows - a.shape[0]),) + ((0, 0),) * (a.ndim - 1))


def _local_step(x, p, target, sm, wts, fetch_rest, send, tok):
    ones_q, ones_k, dup, dup_t = _head_consts()
    tri, triu, expand, expand_t = _ssd_consts()
    w_in_t = wts["in_t"]
    cwx, cwb = wts["ssm_cw"][:, :SSM_INNER], wts["ssm_cw"][:, SSM_INNER:]
    cbx, cbb = sm["ssm_conv_b"][:, :SSM_INNER], sm["ssm_conv_b"][:, SSM_INNER:]
    pad128 = lambda a: jnp.pad(a, ((0, 0), (0, 128 - a.shape[1])))
    dtb, alog = pad128(sm["dt_bias"]), pad128(sm["a_log"])
    dsk_e = jnp.repeat(sm["d_skip"], HEAD_DIM, axis=1)
    gq = jnp.tile(sm["q_norm_g"], (1, ATTN_DIM // HEAD_DIM))
    gk = jnp.tile(sm["k_norm_g"], (1, KV_DIM // HEAD_DIM))
    ffn_cw = wts["ffn_cw"].reshape(3, 2, 1, D_FF)
    ffn_cb = sm["ffn_conv_b"].reshape(2, 1, D_FF)

    proj, h1 = _norm_matmul(x, sm["attn_norm_g"] + tok, w_in_t, 512, 3840, "in_proj")
    qn, kd, vd = _attn_prep(proj, gq, gk, ones_q, ones_k, dup)
    attn_out, lse = _attn_fwd(qn, kd, vd)
    y_ssd, hs, ssm_out, pre_x, pre_b = _ssd_fwd(proj, cwx, cbx, cwb, cbb, dtb, alog, dsk_e, sm["ssm_norm_g"], tri,
                                                expand)
    rest = fetch_rest(ssm_out)
    w_out, w_up_t, w_down, w_gate, w_proj_t = (rest[k] for k in ("out", "up_t", "down", "gate", "proj_t"))
    x1 = _mm_resid([(attn_out, None, w_out[:ATTN_DIM]), (ssm_out, None, w_out[ATTN_DIM:])], x, None, 1024, F32,
                   "out_proj")
    u, uc, h2, f = _up_act(x1, sm["ffn_norm_g"], w_up_t, ffn_cw, ffn_cb)
    dx2, dpre, dpp, h3, loss, dg_ple = _down_ple_loss(x1, f, w_down, sm["ple_norm_g"], w_gate, p, w_proj_t, target)

    g_gate = _wgrad(h3, None, dpre, "wg_gate")
    g_proj_t = _wgrad(dpp, None, p, "wg_proj")
    g_down = _wgrad(f, None, dx2, "wg_down")
    du, d_ffn_cw, d_ffn_cb = _ffn_bwd(dx2, w_down, u, uc, ffn_cw)
    dx1, dg_ffn = _mm_normbwd([(du, 0, w_up_t, D_FF, 0), (du, 1, w_up_t, D_FF, 1)], x1, sm["ffn_norm_g"], dx2, 512,
                              "up_proj_bwd")
    g_up_t = _wgrad(du, "all", h2, "wg_up")
    tok = send(dict(gate=g_gate, proj_t=g_proj_t, down=g_down, up_t=g_up_t)).astype(BF16)
    d_attn, d_ssm = _out_proj_bwd(dx1, w_out + tok)
    g_out = _wgrad_multi([attn_out, ssm_out], dx1, "wg_out")
    tok = send(dict(out=g_out))
    (dz, dxs, dbc, ddt, dg_ssm, d_dsk_e, d_alog, d_dtb, d_cwx, d_cbx, d_cwb, d_cbb) = _ssd_bwd(
        proj, pre_x, pre_b, y_ssd, hs, d_ssm, cwx, cwb, dtb + tok, alog, dsk_e, sm["ssm_norm_g"], tri, triu, expand,
        expand_t)
    dqn, dkc, dkp, dvc, dvp = _attn_bwd(qn, kd, vd, attn_out, lse, d_attn, ones_k[:128, :128])
    dqkv, dgq, dgk = _attn_prep_bwd(proj, dqn, dkc, dkp, dvc, dvp, gq + tok, gk, ones_q, ones_k, dup_t)
    pieces = [(dqkv, 0, 1024), (dz, 1024, 2048), (dxs, 2048, 3072), (dbc, 3072, 3584), (ddt, 3584, 3712)]
    g_in_t = jnp.concatenate([_wgrad_multi([dqkv, dz], h1, "wg_in_qkvz"),
                              _wgrad_multi([dxs, dbc, ddt], h1, "wg_in_xbcdt")], axis=0)[:IN_PROJ]
    tok = send(dict(in_t=g_in_t))
    grad_x, dg_attn = _mm_normbwd([(a, None, w_in_t, hi - lo, lo // (hi - lo)) for a, lo, hi in pieces], x,
                                  sm["attn_norm_g"] + tok, dx1, 512, "in_proj_bwd")

    small = dict(
        attn_norm_g=dg_attn, q_norm_g=dgq.reshape(-1, HEAD_DIM).sum(0, keepdims=True),
        k_norm_g=dgk.reshape(-1, HEAD_DIM).sum(0, keepdims=True),
        ssm_conv_w=jnp.concatenate([d_cwx, d_cwb], axis=1), ssm_conv_b=jnp.concatenate([d_cbx, d_cbb], axis=1),
        dt_bias=d_dtb[:, :SSM_HEADS], a_log=d_alog[:, :SSM_HEADS],
        d_skip=d_dsk_e.reshape(SSM_HEADS, HEAD_DIM).sum(1)[None, :], ssm_norm_g=dg_ssm, ffn_norm_g=dg_ffn,
        ffn_conv_w=d_ffn_cw.reshape(3, 2 * D_FF), ffn_conv_b=d_ffn_cb.reshape(1, 2 * D_FF), ple_norm_g=dg_ple)
    return loss[0, 0], grad_x, small


_SMALL = (("attn_norm_g", 1, 1024), ("q_norm_g", 1, 64), ("k_norm_g", 1, 64), ("ssm_conv_w", 4, XBC_DIM),
          ("ssm_conv_b", 1, XBC_DIM), ("dt_bias", 1, 16), ("a_log", 1, 16), ("d_skip", 1, 16), ("ssm_norm_g", 1, 1024),
          ("ffn_norm_g", 1, 1024), ("ffn_conv_w", 3, 2 * D_FF), ("ffn_conv_b", 1, 2 * D_FF), ("ple_norm_g", 1, 1024))
_SMALL_ROWS, _SMALL_COLS = 32, XBC_DIM
_SHARDED_SMALL = ("ssm_conv_w", "ffn_conv_w")


def _small_chunks(n):
    return 1 if n <= _SMALL_COLS else 4


def _pack_small(parts, loss):
    rows = []
    for k, r, n in _SMALL:
        c = _small_chunks(n)
        rows.append(jnp.pad(parts[k].reshape(r * c, n // c), ((0, 0), (0, _SMALL_COLS - n // c))))
    packed = _pad_rows(jnp.concatenate(rows, axis=0), _SMALL_ROWS)
    at_loss = ((lax.broadcasted_iota(jnp.int32, packed.shape, 0) == _SMALL_ROWS - 1) &
               (lax.broadcasted_iota(jnp.int32, packed.shape, 1) == 0))
    return jnp.where(at_loss, loss, packed)


def _adam_update(w, g, m, v):
    c1 = 1.0 - ADAM_B1 ** ADAM_STEP
    c2 = 1.0 - ADAM_B2 ** ADAM_STEP
    mn = ADAM_B1 * m + (1.0 - ADAM_B1) * g
    vn = ADAM_B2 * v + (1.0 - ADAM_B2) * (g * g)
    return -ADAM_LR * ((mn / c1) / (jnp.sqrt(vn / c2) + ADAM_EPS) + ADAM_WD * w), mn, vn


def _adamw_small(g_all, g_shard, w, m, v):
    ins, shapes = [g_all], []
    for k, _, _ in _SMALL:
        shape2 = w[k].shape if w[k].ndim == 2 else (1, w[k].shape[0])
        shapes.append(shape2)
        ins += ([g_shard[k]] if k in _SHARDED_SMALL else []) + [a.reshape(shape2) for a in (w[k], m[k], v[k])]

    def body(*refs):
        g_ref, pos, row = refs[0], 1, 0
        outs = refs[len(ins):]
        for i, (k, r, n) in enumerate(_SMALL):
            c = _small_chunks(n)
            if k in _SHARDED_SMALL:
                g = refs[pos][...]
                pos += 1
            elif c == 1:
                g = g_ref[row:row + r, 0:n]
            else:
                g = jnp.concatenate([g_ref[row + j:row + j + 1, 0:n // c] for j in range(c)], axis=1)
            row += r * c
            d, mn, vn = _adam_update(refs[pos][...], g, refs[pos + 1][...], refs[pos + 2][...])
            pos += 3
            for o_ref, val in zip(outs[4 * i:4 * i + 4], (g, d, mn, vn)):
                o_ref[...] = val

    res = pl.pallas_call(body, name="adamw_small",
                         out_shape=[SDS(s, F32) for s in shapes for _ in range(4)])(*ins)
    return {k: tuple(a.reshape(w[k].shape) for a in res[4 * i:4 * i + 4]) for i, (k, _, _) in enumerate(_SMALL)}


def kernel(x, p, attn_norm_g, w_in, q_norm_g, k_norm_g, ssm_conv_w, ssm_conv_b, dt_bias, a_log, d_skip, ssm_norm_g, w_out, ffn_norm_g, w_up, ffn_conv_w, ffn_conv_b, w_down, ple_norm_g, w_ple_gate, w_ple_proj, loss_target, m_attn_norm_g, m_w_in, m_q_norm_g, m_k_norm_g, m_ssm_conv_w, m_ssm_conv_b, m_dt_bias, m_a_log, m_d_skip, m_ssm_norm_g, m_w_out, m_ffn_norm_g, m_w_up, m_ffn_conv_w, m_ffn_conv_b, m_w_down, m_ple_norm_g, m_w_ple_gate, m_w_ple_proj, v_attn_norm_g, v_w_in, v_q_norm_g, v_k_norm_g, v_ssm_conv_w, v_ssm_conv_b, v_dt_bias, v_a_log, v_d_skip, v_ssm_norm_g, v_w_out, v_ffn_norm_g, v_w_up, v_ffn_conv_w, v_ffn_conv_b, v_w_down, v_ple_norm_g, v_w_ple_gate, v_w_ple_proj):
    names = ("attn_norm_g", "w_in", "q_norm_g", "k_norm_g", "ssm_conv_w", "ssm_conv_b", "dt_bias", "a_log", "d_skip",
             "ssm_norm_g", "w_out", "ffn_norm_g", "w_up", "ffn_conv_w", "ffn_conv_b", "w_down", "ple_norm_g",
             "w_ple_gate", "w_ple_proj")
    w = dict(zip(names, (attn_norm_g, w_in, q_norm_g, k_norm_g, ssm_conv_w, ssm_conv_b, dt_bias, a_log, d_skip,
                         ssm_norm_g, w_out, ffn_norm_g, w_up, ffn_conv_w, ffn_conv_b, w_down, ple_norm_g, w_ple_gate,
                         w_ple_proj)))
    m = dict(zip(names, (m_attn_norm_g, m_w_in, m_q_norm_g, m_k_norm_g, m_ssm_conv_w, m_ssm_conv_b, m_dt_bias,
                         m_a_log, m_d_skip, m_ssm_norm_g, m_w_out, m_ffn_norm_g, m_w_up, m_ffn_conv_w, m_ffn_conv_b,
                         m_w_down, m_ple_norm_g, m_w_ple_gate, m_w_ple_proj)))
    v = dict(zip(names, (v_attn_norm_g, v_w_in, v_q_norm_g, v_k_norm_g, v_ssm_conv_w, v_ssm_conv_b, v_dt_bias,
                         v_a_log, v_d_skip, v_ssm_norm_g, v_w_out, v_ffn_norm_g, v_w_up, v_ffn_conv_w, v_ffn_conv_b,
                         v_w_down, v_ple_norm_g, v_w_ple_gate, v_w_ple_proj)))
    w, m, v = ({k: a[0] for k, a in d.items()} for d in (w, m, v))
    me = 4 * lax.axis_index("x") + 2 * lax.axis_index("y") + lax.axis_index("c")

    mine = dict(in_t=w["w_in"].T, out=w["w_out"], up_t=w["w_up"].T, down=w["w_down"], gate=w["w_ple_gate"],
                proj_t=w["w_ple_proj"].T)
    mine = {k: a.astype(BF16) for k, a in mine.items()}
    conv_pack = jnp.pad(jnp.concatenate([w["ssm_conv_w"].reshape(-1), w["ffn_conv_w"].reshape(-1)]),
                        (0, 3072 - 2880)).reshape(8, 384)
    all_in, all_conv = _all_gather([mine["in_t"], conv_pack], "gather_first")
    later = ("out", "up_t", "down", "gate", "proj_t")
    zones = [lax.dynamic_update_slice(lax.empty((N_DEV,) + mine[k].shape, BF16), mine[k][None], (me, 0, 0))
             for k in later]
    zones, all_in, all_conv = lax.optimization_barrier((zones, all_in, all_conv))
    rest_state = _exchange_start([mine[k] for k in later], zones, [False] * len(later), "gather_rest_start")

    def fetch_rest(after):
        _, landed = _exchange_wait(*rest_state[:4], [False] * len(later), after, "gather_rest_wait")
        return {k: a.reshape(N_DEV * a.shape[1], a.shape[2]) for k, a in zip(later, landed)}

    wts = dict(in_t=_pad_rows(all_in.reshape(IN_PROJ, D_MODEL), IN_PROJ_PAD))
    conv_flat = all_conv.reshape(N_DEV, 3072)
    wts["ssm_cw"] = conv_flat[:, :768].reshape(N_DEV, 4, XBC_DIM // N_DEV).transpose(1, 0, 2).reshape(4, XBC_DIM)
    wts["ffn_cw"] = conv_flat[:, 768:2880].reshape(N_DEV, 3, 2 * D_FF // N_DEV).transpose(1, 0, 2).reshape(3, 2 * D_FF)
    sm = {k: w[k].reshape(1, -1) for k, _, _ in _SMALL if k not in _SHARDED_SMALL}

    in_flight = []

    def send(grads):
        keys = sorted(grads)
        srcs = [grads[k].reshape(N_DEV, grads[k].shape[0] // N_DEV, grads[k].shape[1]) for k in keys]
        state = _exchange_start(srcs, [lax.empty(a.shape, BF16) for a in srcs], [True] * len(keys),
                                "send_" + "_".join(keys))
        in_flight.append((keys, state))
        return state[4][0:1, 0:1]

    loss, grad_x, small = _local_step(x[0], p[0, 0], loss_target[0], sm, wts, fetch_rest, send,
                                      rest_state[4][0:1, 0:1])

    (got_small,) = _all_gather([_pack_small(small, loss)], "gather_small_grads")
    g_small = _reduce8(got_small, _SMALL_ROWS, "reduce_small")
    loss = g_small[_SMALL_ROWS - 1, 0]
    grads, gw, delta, new_m, new_v = {}, {}, {}, {}, {}
    row_sharded = {"out": "w_out", "down": "w_down", "gate": "w_ple_gate"}
    for keys, state in in_flight:
        sent, landed = _exchange_wait(*state[:4], [True] * len(keys), grad_x, "wait_" + "_".join(keys))
        for k, shares, land in zip(keys, sent, landed):
            own = lax.dynamic_index_in_dim(shares, me, 0, keepdims=False)
            if k in row_sharded:
                n = row_sharded[k]
                gw[n], delta[n], new_m[n], new_v[n] = _reduce_adamw(own, land, w[n], m[n], v[n], "update_" + n)
            else:
                grads[k] = _reduce_landed(own, land, "reduce_" + k)
    gw.update({"w_in": grads["in_t"].T, "w_up": grads["up_t"].T, "w_ple_proj": grads["proj_t"].T})
    n_ssm, n_ffn = XBC_DIM // N_DEV, 2 * D_FF // N_DEV
    g_shard = {"ssm_conv_w": lax.dynamic_slice(g_small, (3, me * n_ssm), (4, n_ssm)),
               "ffn_conv_w": lax.dynamic_slice(g_small[13:25, :2 * D_FF // 4].reshape(3, 2 * D_FF), (0, me * n_ffn),
                                               (3, n_ffn))}

    for k, tr in (("w_in", 256), ("w_up", 256), ("w_ple_proj", None)):
        delta[k], new_m[k], new_v[k] = _adamw(w[k], gw[k], m[k], v[k], "adamw_" + k, tr)
    for k, (g_k, d_k, m_k, v_k) in _adamw_small(g_small, g_shard, w, m, v).items():
        gw[k], delta[k], new_m[k], new_v[k] = g_k, d_k, m_k, v_k

    outs = [loss, grad_x[None]]
    for d in (gw, delta, new_m, new_v):
        outs += [d[k][None] for k in names]
    return tuple(outs)
```

```python
import functools

import numpy as np
import jax
import jax.numpy as jnp
from jax import lax
from jax.experimental import pallas as pl
from jax.experimental.pallas import tpu as pltpu

F32 = jnp.float32
BF16 = jnp.bfloat16
SDS = jax.ShapeDtypeStruct
EPS = 1e-6
N_DEV = 8
D_MODEL = 1024
HEAD_DIM = 64
ATTN_DIM = 512
KV_DIM = 256
SSM_INNER = 1024
SSM_HEADS = 16
BC_DIM = 256
XBC_DIM = SSM_INNER + 2 * BC_DIM
MIX_DIM = ATTN_DIM + SSM_INNER
IN_PROJ = 3600
IN_PROJ_PAD = 3840
D_FF = 2816
PLE_DIM = 256
CHUNK = 128
SUPER = 2048
DILATIONS = (1, 4, 16)
TILE_UNROLL = 8
VMEM_LIMIT = 56 * 1024 * 1024
ADAM_LR, ADAM_B1, ADAM_B2, ADAM_EPS, ADAM_WD, ADAM_STEP = 0.001, 0.9, 0.999, 1e-08, 0.01, 10

NT = (((1,), (1,)), ((), ()))
TN = (((0,), (0,)), ((), ()))


def _params(*sem):
    return pltpu.CompilerParams(dimension_semantics=sem if sem else None, vmem_limit_bytes=VMEM_LIMIT)


def _dot(a, b, dims=None):
    if dims is None:
        return jnp.dot(a, b, preferred_element_type=F32)
    return lax.dot_general(a, b, dims, preferred_element_type=F32)


def _hdot(a, b, parts=2):
    a_exact = a.dtype == BF16
    x = b if a_exact else a
    acc = None
    for _ in range(parts):
        piece = x.astype(BF16)
        x = x - piece.astype(F32)
        d = _dot(a, piece) if a_exact else _dot(piece, b)
        acc = d if acc is None else acc + d
    return acc


def _sigmoid(x):
    return 0.5 * jnp.tanh(0.5 * x) + 0.5


def _shift_down(x, halo8, s):
    xr = pltpu.roll(x, s, 0)
    row = lax.broadcasted_iota(jnp.int32, halo8.shape, 0)
    first = jnp.where(row < s, pltpu.roll(halo8, s, 0), xr[0:8])
    return jnp.concatenate([first, xr[8:]], axis=0)


def _shift_up(x, halo8, s):
    n = x.shape[0]
    xr = pltpu.roll(x, n - s, 0)
    row = lax.broadcasted_iota(jnp.int32, halo8.shape, 0)
    last = jnp.where(row >= 8 - s, pltpu.roll(halo8, 8 - s, 0), xr[n - 8:])
    return jnp.concatenate([xr[:n - 8], last], axis=0)


def _norm_matmul(x, g, wt, tm, tn, name):
    t, k = x.shape
    n = wt.shape[0]

    def body(x_ref, g_ref, w_ref, o_ref, h_ref):
        @pl.when(pl.program_id(1) == 0)
        def _():
            xv = x_ref[...]
            r = lax.rsqrt(jnp.mean(xv * xv, axis=-1, keepdims=True) + EPS)
            h_ref[...] = (xv * r * g_ref[...]).astype(BF16)
        o_ref[...] = _dot(h_ref[...], w_ref[...], NT)

    return pl.pallas_call(
        body, name=name, grid=(t // tm, n // tn),
        in_specs=[pl.BlockSpec((tm, k), lambda i, j: (i, 0)), pl.BlockSpec((1, k), lambda i, j: (0, 0)),
                  pl.BlockSpec((tn, k), lambda i, j: (j, 0))],
        out_specs=[pl.BlockSpec((tm, tn), lambda i, j: (i, j)), pl.BlockSpec((tm, k), lambda i, j: (i, 0))],
        out_shape=[SDS((t, n), F32), SDS((t, k), BF16)],
        compiler_params=_params("parallel", "arbitrary"))(x, g, wt)


def _a_spec(a, lead, tm):
    if lead is None:
        return pl.BlockSpec((tm, a.shape[-1]), lambda i: (i, 0))
    return pl.BlockSpec((None, tm, a.shape[-1]), lambda i, _l=lead: (_l, i, 0))


def _mm_resid(pairs, res, dims, tm, out_dtype, name):
    t = pairs[0][0].shape[-2]
    n = pairs[0][2].shape[1] if dims is None else pairs[0][2].shape[0]
    np_ = len(pairs)

    def body(*refs):
        o_ref = refs[-1]
        acc = refs[2 * np_][...] if res is not None else None
        for q in range(np_):
            d = _dot(refs[q][...].astype(BF16), refs[np_ + q][...], dims)
            acc = d if acc is None else acc + d
        o_ref[...] = acc.astype(out_dtype)

    in_specs = [_a_spec(a, lead, tm) for a, lead, _ in pairs]
    in_specs += [pl.BlockSpec(b.shape, lambda i: (0, 0)) for _, _, b in pairs]
    args = [a for a, _, _ in pairs] + [b for _, _, b in pairs]
    if res is not None:
        in_specs.append(pl.BlockSpec((tm, n), lambda i: (i, 0)))
        args.append(res)
    return pl.pallas_call(
        body, name=name, grid=(t // tm,), in_specs=in_specs,
        out_specs=pl.BlockSpec((tm, n), lambda i: (i, 0)), out_shape=SDS((t, n), out_dtype),
        compiler_params=_params("parallel"))(*args)


def _wgrad_multi(parts, b, name, tk=2048):
    t, n = b.shape
    widths = [a.shape[1] for a in parts]
    m = sum(widths)

    def body(*refs):
        b_ref, o_ref, acc = refs[len(parts):]

        @pl.when(pl.program_id(0) == 0)
        def _():
            acc[...] = jnp.zeros_like(acc)
        bv = b_ref[...].astype(BF16)
        row = 0
        for a_ref, w in zip(refs, widths):
            acc[row:row + w, :] += _dot(a_ref[...].astype(BF16), bv, TN)
            row += w

        @pl.when(pl.program_id(0) == pl.num_programs(0) - 1)
        def _():
            o_ref[...] = acc[...].astype(BF16)

    return pl.pallas_call(
        body, name=name, grid=(t // tk,),
        in_specs=[pl.BlockSpec((tk, w), lambda k: (k, 0)) for w in widths] + [pl.BlockSpec((tk, n), lambda k: (k, 0))],
        out_specs=pl.BlockSpec((m, n), lambda k: (0, 0)), out_shape=SDS((m, n), BF16),
        scratch_shapes=[pltpu.VMEM((m, n), F32)],
        compiler_params=_params("arbitrary"))(*parts, b)


def _out_proj_bwd(dx, w_out, tm=1024):
    t = dx.shape[0]

    def body(dx_ref, w_ref, da_ref, ds_ref):
        a = dx_ref[...].astype(BF16)
        da_ref[...] = _dot(a, w_ref[0:ATTN_DIM, :], NT)
        ds_ref[...] = _dot(a, w_ref[ATTN_DIM:, :], NT)

    return pl.pallas_call(
        body, name="out_proj_bwd", grid=(t // tm,),
        in_specs=[pl.BlockSpec((tm, D_MODEL), lambda i: (i, 0)), pl.BlockSpec(w_out.shape, lambda i: (0, 0))],
        out_specs=[pl.BlockSpec((tm, ATTN_DIM), lambda i: (i, 0)), pl.BlockSpec((tm, SSM_INNER), lambda i: (i, 0))],
        out_shape=[SDS((t, ATTN_DIM), F32), SDS((t, SSM_INNER), F32)],
        compiler_params=_params("parallel"))(dx, w_out)


def _mm_normbwd(pairs, x, g, dres, tm, name):
    t, k = x.shape
    np_ = len(pairs)
    b_specs = [pl.BlockSpec((rows, b.shape[1]), lambda i, _b=blk: (_b, 0)) for _, _, b, rows, blk in pairs]
    pairs = [(a, lead, b) for a, lead, b, _, _ in pairs]

    def body(*refs):
        x_ref, g_ref, dres_ref, dx_ref, dg_ref = refs[2 * np_:]
        dh = None
        for q in range(np_):
            d = _dot(refs[q][...], refs[np_ + q][...])
            dh = d if dh is None else dh + d
        xv = x_ref[...]
        r = lax.rsqrt(jnp.mean(xv * xv, axis=-1, keepdims=True) + EPS)
        xh = xv * r

        @pl.when(pl.program_id(0) == 0)
        def _():
            dg_ref[...] = jnp.zeros_like(dg_ref)
        dg_ref[...] += jnp.sum(dh * xh, axis=0, keepdims=True)
        gd = dh * g_ref[...]
        dx_ref[...] = dres_ref[...] + r * (gd - xh * jnp.mean(gd * xh, axis=-1, keepdims=True))

    in_specs = [_a_spec(a, lead, tm) for a, lead, _ in pairs] + b_specs
    in_specs += [pl.BlockSpec((tm, k), lambda i: (i, 0)), pl.BlockSpec((1, k), lambda i: (0, 0)),
                 pl.BlockSpec((tm, k), lambda i: (i, 0))]
    args = [a for a, _, _ in pairs] + [b for _, _, b in pairs] + [x, g, dres]
    return pl.pallas_call(
        body, name=name, grid=(t // tm,), in_specs=in_specs,
        out_specs=[pl.BlockSpec((tm, k), lambda i: (i, 0)), pl.BlockSpec((1, k), lambda i: (0, 0))],
        out_shape=[SDS((t, k), F32), SDS((1, k), F32)],
        compiler_params=_params("arbitrary"))(*args)


def _wgrad(a, a_lead, b, name, tk=2048):
    t, m = a.shape[-2:]
    n = b.shape[1]
    tm = m if m <= 1024 else 1408
    assert m % tm == 0

    def body(a_ref, b_ref, o_ref, acc):
        @pl.when(pl.program_id(1) == 0)
        def _():
            acc[...] = jnp.zeros_like(acc)
        acc[...] += _dot(a_ref[...].astype(BF16), b_ref[...].astype(BF16), TN)

        @pl.when(pl.program_id(1) == pl.num_programs(1) - 1)
        def _():
            o_ref[...] = acc[...].astype(BF16)

    per, lead = m // tm, 1
    if a_lead == "all":
        lead = a.shape[0]
        a_spec = pl.BlockSpec((None, tk, tm), lambda mi, ki: (mi // per, ki, mi % per))
    elif a_lead is None:
        a_spec = pl.BlockSpec((tk, tm), lambda mi, ki: (ki, mi))
    else:
        a_spec = pl.BlockSpec((None, tk, tm), lambda mi, ki, _l=a_lead: (_l, ki, mi))
    return pl.pallas_call(
        body, name=name, grid=(lead * per, t // tk),
        in_specs=[a_spec, pl.BlockSpec((tk, n), lambda mi, ki: (ki, 0))],
        out_specs=pl.BlockSpec((tm, n), lambda mi, ki: (mi, 0)), out_shape=SDS((lead * m, n), BF16),
        scratch_shapes=[pltpu.VMEM((tm, n), F32)],
        compiler_params=_params("parallel", "arbitrary"))(a, b)


def _head_consts():
    iq = np.arange(ATTN_DIM)
    ik = np.arange(KV_DIM)
    ones_q = (iq[:, None] // HEAD_DIM == iq[None, :] // HEAD_DIM).astype(np.float32)
    ones_k = (ik[:, None] // HEAD_DIM == ik[None, :] // HEAD_DIM).astype(np.float32)
    dup = (ik[:, None] == (HEAD_DIM * (iq[None, :] // 128) + iq[None, :] % HEAD_DIM)).astype(np.float32)
    return jnp.asarray(ones_q, BF16), jnp.asarray(ones_k, BF16), jnp.asarray(dup, BF16), jnp.asarray(dup.T, BF16)


def _attn_prep(proj, gq, gk, ones_q, ones_k, dup, tm=1024):
    t = proj.shape[0]

    def body(p_ref, gq_ref, gk_ref, oq_ref, ok_ref, dup_ref, qn_ref, kd_ref, vd_ref):
        q = p_ref[:, 0:ATTN_DIM]
        k = p_ref[:, ATTN_DIM:ATTN_DIM + KV_DIM]
        v = p_ref[:, ATTN_DIM + KV_DIM:]
        rq = lax.rsqrt(_hdot(q * q, oq_ref[...]) * (1.0 / HEAD_DIM) + EPS)
        qn_ref[...] = (q * rq * gq_ref[...]) * (HEAD_DIM ** -0.5)
        rk = lax.rsqrt(_hdot(k * k, ok_ref[...]) * (1.0 / HEAD_DIM) + EPS)
        kn = k * rk * gk_ref[...]
        kd_ref[...] = _dot(kn.astype(BF16), dup_ref[...])
        vd_ref[...] = _dot(v.astype(BF16), dup_ref[...])

    full = lambda a: pl.BlockSpec(a.shape, lambda i: (0, 0))
    o_spec = pl.BlockSpec((tm, ATTN_DIM), lambda i: (i, 0))
    return pl.pallas_call(
        body, name="attn_prep", grid=(t // tm,),
        in_specs=[pl.BlockSpec((tm, 1024), lambda i: (i, 0)), full(gq), full(gk), full(ones_q), full(ones_k), full(dup)],
        out_specs=[o_spec, o_spec, o_spec], out_shape=[SDS((t, ATTN_DIM), F32)] * 3,
        compiler_params=_params("parallel"))(proj, gq, gk, ones_q, ones_k, dup)


def _attn_prep_bwd(proj, dqn, dkc, dkp, dvc, dvp, gq, gk, ones_q, ones_k, dup_t, tm=1024):
    t = proj.shape[0]
    nblk = t // tm
    off = SUPER // tm

    def body(p_ref, dqn_ref, dkc_ref, dkp_ref, dvc_ref, dvp_ref, gq_ref, gk_ref, oq_ref, ok_ref, dt_ref,
             o_ref, dgq_ref, dgk_ref):
        i = pl.program_id(0)
        has_next = (i + off < nblk).astype(F32)
        q = p_ref[:, 0:ATTN_DIM]
        k = p_ref[:, ATTN_DIM:ATTN_DIM + KV_DIM]
        dkn = _hdot(dkc_ref[...] + has_next * dkp_ref[...], dt_ref[...])
        dv = _hdot(dvc_ref[...] + has_next * dvp_ref[...], dt_ref[...])

        @pl.when(i == 0)
        def _():
            dgq_ref[...] = jnp.zeros_like(dgq_ref)
            dgk_ref[...] = jnp.zeros_like(dgk_ref)

        rq = lax.rsqrt(_hdot(q * q, oq_ref[...]) * (1.0 / HEAD_DIM) + EPS)
        xh = q * rq
        dy = dqn_ref[...] * (HEAD_DIM ** -0.5)
        dgq_ref[...] += jnp.sum(dy * xh, axis=0, keepdims=True)
        gd = dy * gq_ref[...]
        dq = rq * (gd - xh * (_hdot(gd * xh, oq_ref[...]) * (1.0 / HEAD_DIM)))
        rk = lax.rsqrt(_hdot(k * k, ok_ref[...]) * (1.0 / HEAD_DIM) + EPS)
        kh = k * rk
        dgk_ref[...] += jnp.sum(dkn * kh, axis=0, keepdims=True)
        gdk = dkn * gk_ref[...]
        dk = rk * (gdk - kh * (_hdot(gdk * kh, ok_ref[...]) * (1.0 / HEAD_DIM)))
        o_ref[:, 0:ATTN_DIM] = dq.astype(BF16)
        o_ref[:, ATTN_DIM:ATTN_DIM + KV_DIM] = dk.astype(BF16)
        o_ref[:, ATTN_DIM + KV_DIM:] = dv.astype(BF16)

    full = lambda a: pl.BlockSpec(a.shape, lambda i: (0, 0))
    cur = pl.BlockSpec((tm, ATTN_DIM), lambda i: (i, 0))
    nxt = pl.BlockSpec((tm, ATTN_DIM), lambda i: (jnp.minimum(i + off, nblk - 1), 0))
    return pl.pallas_call(
        body, name="attn_prep_bwd", grid=(nblk,),
        in_specs=[pl.BlockSpec((tm, 1024), lambda i: (i, 0)), cur, cur, nxt, cur, nxt,
                  full(gq), full(gk), full(ones_q), full(ones_k), full(dup_t)],
        out_specs=[pl.BlockSpec((tm, 1024), lambda i: (i, 0)), pl.BlockSpec((1, ATTN_DIM), lambda i: (0, 0)),
                   pl.BlockSpec((1, KV_DIM), lambda i: (0, 0))],
        out_shape=[SDS((t, 1024), BF16), SDS((1, ATTN_DIM), F32), SDS((1, KV_DIM), F32)],
        compiler_params=_params("arbitrary"))(proj, dqn, dkc, dkp, dvc, dvp, gq, gk, ones_q, ones_k, dup_t)


def _tile_masks():
    qi = lax.broadcasted_iota(jnp.int32, (2 * CHUNK, 2 * CHUNK), 0) & (CHUNK - 1)
    kj = lax.broadcasted_iota(jnp.int32, (2 * CHUNK, 2 * CHUNK), 1)
    delta = CHUNK + qi - kj
    band = (delta >= 0) & (delta <= CHUNK)
    return band, kj


def _deinterleave(dst, src, n_rows, d):
    per = n_rows // d
    for r in range(d):
        dst[r * per:(r + 1) * per, :] = src[pl.ds(r, per, stride=d), :]


def _attn_specs(t):
    blk = lambda f: pl.BlockSpec((SUPER, 128), f)
    cur = blk(lambda h, s: (s, h))
    prev = blk(lambda h, s: (jnp.maximum(s - 1, 0), h))
    return cur, prev


def _attn_fwd(qn, kd, vd):
    t = qn.shape[0]
    cur, prev = _attn_specs(t)

    def body(q_ref, kp_ref, kc_ref, vp_ref, vc_ref, o_ref, lse_ref, kk, vv, qd, kdd, vdd, po, pm, pll, acc, mm, ll):
        s = pl.program_id(1)
        kk[0:SUPER, :] = kp_ref[...]
        kk[SUPER:, :] = kc_ref[...]
        vv[0:SUPER, :] = vp_ref[...]
        vv[SUPER:, :] = vc_ref[...]
        m0 = lax.broadcasted_iota(jnp.int32, (CHUNK, 128), 1) < HEAD_DIM
        band, kj = _tile_masks()
        for d in DILATIONS:
            lq = SUPER // d
            if d == 1:
                qs_ref, ks_ref, vs_ref = q_ref, kk, vv
            else:
                _deinterleave(qd, q_ref, SUPER, d)
                _deinterleave(kdd, kk, 2 * SUPER, d)
                _deinterleave(vdd, vv, 2 * SUPER, d)
                qs_ref, ks_ref, vs_ref = qd, kdd, vdd

            nblk = lq // CHUNK

            def key_rows(ti):
                return pl.ds((ti // nblk) * 2 * lq + lq + (ti % nblk - 1) * CHUNK, 2 * CHUNK)

            def scores(ti):
                qt = qs_ref[pl.ds(ti * CHUNK, CHUNK), :]
                qs = jnp.concatenate([jnp.where(m0, qt, 0.0), jnp.where(m0, 0.0, qt)], axis=0).astype(BF16)
                return _dot(qs, ks_ref[key_rows(ti), :].astype(BF16), NT)

            def softmax_pv(ti, sc):
                ok = band if ti % nblk > 0 else band & (kj >= jnp.where(s > 0, 0, CHUNK))
                sc = jnp.where(ok, sc, -jnp.inf)
                mt = jnp.max(sc, axis=-1, keepdims=True)
                p = jnp.exp(sc - mt)
                lt = jnp.sum(p, axis=-1, keepdims=True)
                ot = _dot(p.astype(BF16), vs_ref[key_rows(ti), :].astype(BF16))
                qrows = pl.ds(ti * CHUNK, CHUNK)
                po[qrows, :] = jnp.where(m0, ot[:CHUNK], ot[CHUNK:])
                pm[qrows, :] = jnp.where(m0, mt[:CHUNK], mt[CHUNK:])
                pll[qrows, :] = jnp.where(m0, lt[:CHUNK], lt[CHUNK:])

            for ti in range(SUPER // CHUNK):
                softmax_pv(ti, scores(ti))
            if d == 1:
                acc[...] = po[...]
                mm[...] = pm[...]
                ll[...] = pll[...]
            else:
                for r in range(d):
                    rows = pl.ds(r, lq, stride=d)
                    seg = slice(r * lq, (r + 1) * lq)
                    m_old, m_new = mm[rows, :], pm[seg, :]
                    m_all = jnp.maximum(m_old, m_new)
                    a, b = jnp.exp(m_old - m_all), jnp.exp(m_new - m_all)
                    acc[rows, :] = acc[rows, :] * a + po[seg, :] * b
                    ll[rows, :] = ll[rows, :] * a + pll[seg, :] * b
                    mm[rows, :] = m_all
        o_ref[...] = acc[...] / ll[...]
        lse_ref[...] = mm[...] + jnp.log(ll[...])

    big = pltpu.VMEM((2 * SUPER, 128), F32)
    one = pltpu.VMEM((SUPER, 128), F32)
    return pl.pallas_call(
        body, name="attn_fwd", grid=(4, t // SUPER),
        in_specs=[cur, prev, cur, prev, cur], out_specs=[cur, cur],
        out_shape=[SDS((t, ATTN_DIM), F32)] * 2,
        scratch_shapes=[big, big, one, big, big, one, one, one, one, one, one],
        compiler_params=_params("parallel", "arbitrary"))(qn, kd, kd, vd, vd)


def _attn_bwd(qn, kd, vd, out, lse, dout, ones_pair):
    t = qn.shape[0]
    cur, prev = _attn_specs(t)

    def body(q_ref, kp_ref, kc_ref, vp_ref, vc_ref, o_ref, lse_ref, do_ref, ones_ref,
             dq_ref, dkc_ref, dkp_ref, dvc_ref, dvp_ref,
             kk, vv, od, ld, kb, vb, qsb, dosb, tk, tv, pdq, delta):
        s = pl.program_id(1)
        delta[...] = _hdot(do_ref[...] * o_ref[...], ones_ref[...])

        def per_row(a):
            ar = pltpu.roll(a, HEAD_DIM, 1)
            rows = jnp.concatenate([jnp.where(m0, a, ar), jnp.where(m0, ar, a)], axis=0)
            return jnp.concatenate([rows, rows], axis=1)

        kk[0:SUPER, :] = kp_ref[...]
        kk[SUPER:, :] = kc_ref[...]
        vv[0:SUPER, :] = vp_ref[...]
        vv[SUPER:, :] = vc_ref[...]
        for ref in (dq_ref, dkc_ref, dkp_ref, dvc_ref, dvp_ref):
            ref[...] = jnp.zeros_like(ref)
        m0 = lax.broadcasted_iota(jnp.int32, (CHUNK, 128), 1) < HEAD_DIM
        band, kj = _tile_masks()
        ninf = -jnp.inf
        for d in DILATIONS:
            lq = SUPER // d
            nblk = lq // CHUNK
            for r in range(d):
                seg = slice(r * 2 * lq, (r + 1) * 2 * lq)
                kb[seg, :] = kk[pl.ds(r, 2 * lq, stride=d), :].astype(BF16)
                vb[seg, :] = vv[pl.ds(r, 2 * lq, stride=d), :].astype(BF16)
            for ti in range(SUPER // CHUNK):
                rows = pl.ds(ti // nblk + d * CHUNK * (ti % nblk), CHUNK, stride=d)
                for src, dst in ((q_ref, qsb), (do_ref, dosb)):
                    a = src[rows, :]
                    dst[ti * 2 * CHUNK:(ti + 1) * 2 * CHUNK, :] = jnp.concatenate(
                        [jnp.where(m0, a, 0.0), jnp.where(m0, 0.0, a)], axis=0).astype(BF16)
                ld[ti * CHUNK:(ti + 1) * CHUNK, :] = lse_ref[rows, :]
                od[ti * CHUNK:(ti + 1) * CHUNK, :] = delta[rows, :]

            def operands(ti):
                r, nb = ti // nblk, ti % nblk
                stacked = slice(ti * 2 * CHUNK, (ti + 1) * 2 * CHUNK)
                krows = pl.ds(r * 2 * lq + lq + (nb - 1) * CHUNK, 2 * CHUNK)
                return stacked, krows

            def scores(ti):
                stacked, krows = operands(ti)
                kt = kb[krows, :]
                return dict(ti=ti, sc=_dot(qsb[stacked, :], kt, NT), dp=_dot(dosb[stacked, :], vb[krows, :], NT))

            def softmax_grad(c):
                qrows = slice(c["ti"] * CHUNK, (c["ti"] + 1) * CHUNK)
                ok = band if c["ti"] % nblk > 0 else band & (kj >= jnp.where(s > 0, 0, CHUNK))
                p = jnp.exp(jnp.where(ok, c.pop("sc"), ninf) - per_row(ld[qrows, :]))
                ds = p * (c.pop("dp") - per_row(od[qrows, :]))
                c.update(p=p.astype(BF16), ds=ds.astype(BF16))
                return c

            def grads(c):
                ti = c["ti"]
                stacked, krows = operands(ti)
                dqs = _dot(c["ds"], kb[krows, :])
                pdq[ti * CHUNK:(ti + 1) * CHUNK, :] = jnp.where(m0, dqs[:CHUNK], dqs[CHUNK:])
                tk[stacked, :] = _dot(c["ds"], qsb[stacked, :], TN)
                tv[stacked, :] = _dot(c["p"], dosb[stacked, :], TN)

            n_tiles = SUPER // CHUNK
            stage_a = scores(0)
            for ti in range(n_tiles):
                ahead = scores(ti + 1) if ti + 1 < n_tiles else None
                grads(softmax_grad(stage_a))
                stage_a = ahead

            for r in range(d):
                dq_ref[pl.ds(r, lq, stride=d), :] += pdq[r * lq:(r + 1) * lq, :]
                for tile_out, cur_ref, prev_ref in ((tk, dkc_ref, dkp_ref), (tv, dvc_ref, dvp_ref)):
                    first = r * nblk * 2 * CHUNK
                    prev_ref[pl.ds(SUPER - CHUNK * d + r, CHUNK, stride=d), :] += tile_out[first:first + CHUNK, :]
                    for nb in range(nblk):
                        at = (r * nblk + nb) * 2 * CHUNK
                        part = tile_out[at + CHUNK:at + 2 * CHUNK, :]
                        if nb + 1 < nblk:
                            part = part + tile_out[at + 2 * CHUNK:at + 3 * CHUNK, :]
                        cur_ref[pl.ds(r + d * nb * CHUNK, CHUNK, stride=d), :] += part

    big = pltpu.VMEM((2 * SUPER, 128), F32)
    one = pltpu.VMEM((SUPER, 128), F32)
    half = pltpu.VMEM((2 * SUPER, 128), BF16)
    return pl.pallas_call(
        body, name="attn_bwd", grid=(4, t // SUPER),
        in_specs=[cur, prev, cur, prev, cur, cur, cur, cur, pl.BlockSpec((128, 128), lambda h, s: (0, 0))],
        out_specs=[cur] * 5, out_shape=[SDS((t, ATTN_DIM), F32)] * 5,
        scratch_shapes=[big, big, one, one, half, half, half, half, big, big, one, one],
        compiler_params=_params("parallel", "arbitrary"))(qn, kd, kd, vd, vd, out, lse, dout, ones_pair)


def _ssd_consts():
    tri = np.tril(np.ones((CHUNK, CHUNK), np.float32))
    expand = np.zeros((128, SSM_INNER), np.float32)
    for h in range(SSM_HEADS):
        expand[h, h * HEAD_DIM:(h + 1) * HEAD_DIM] = 1.0
    return jnp.asarray(tri, BF16), jnp.asarray(tri.T, BF16), jnp.asarray(expand, BF16), jnp.asarray(expand.T, BF16)


def _conv4(x, halo, w_ref, b_ref):
    acc = b_ref[...] + w_ref[3:4, :] * x
    for k in range(3):
        acc = acc + w_ref[k:k + 1, :] * _shift_down(x, halo, 3 - k)
    return acc


def _softplus(x):
    return jnp.maximum(x, 0.0) + jnp.log(1.0 + jnp.exp(-jnp.abs(x)))


def _ssd_common(pre_x, pre_b, dt_ref, dtb_ref, alog_ref, tri_ref, exp_ref):
    xa = pre_x * _sigmoid(pre_x)
    ba = pre_b * _sigmoid(pre_b)
    dtv = _softplus(dt_ref[...] + dtb_ref[...])
    a_neg = -jnp.exp(alog_ref[...])
    acum = _hdot(tri_ref[...], dtv * a_neg, parts=3)
    lam = jnp.exp(acum)
    gam = jnp.exp(acum[CHUNK - 1:CHUNK, :] - acum)
    dt_e = _hdot(dtv, exp_ref[...])
    lam_e = _hdot(lam, exp_ref[...])
    gam_e = _hdot(gam, exp_ref[...])
    return dict(pre_x=pre_x, pre_b=pre_b, xa=xa, ba=ba, dtv=dtv, a_neg=a_neg, acum=acum,
                dt_e=dt_e, lam_e=lam_e, gam_e=gam_e, xdt=xa * dt_e)


def _decay(acum_t, h, transposed):
    rb = jnp.broadcast_to(acum_t[h:h + 1, :], (CHUNK, CHUNK))
    ri = lax.broadcasted_iota(jnp.int32, (CHUNK, CHUNK), 0)
    ci = lax.broadcasted_iota(jnp.int32, (CHUNK, CHUNK), 1)
    if transposed:
        return jnp.exp(jnp.where(ci >= ri, rb - rb.T, -jnp.inf))
    return jnp.exp(jnp.where(ri >= ci, rb.T - rb, -jnp.inf))


SSD_STEP = 4 * CHUNK


def _ssd_specs(t, rev):
    nc = t // SSD_STEP
    ch = (lambda c: nc - 1 - c) if rev else (lambda c: c)
    col = lambda w, j: pl.BlockSpec((SSD_STEP, w), lambda c: (ch(c), j))
    halo = lambda w, j: pl.BlockSpec((8, w), lambda c: (jnp.maximum(ch(c) * (SSD_STEP // 8) - 1, 0), j))
    return nc, ch, col, halo


def _ssd_fwd(proj, cwx, cbx, cwb, cbb, dtb, alog, dsk_e, norm_g, tri, expand):
    t = proj.shape[0]
    nc, _, col, halo = _ssd_specs(t, False)

    def body(z_all, xs_all, bc_all, dt_all, hx_ref, hb_ref, cwx_ref, cbx_ref, cwb_ref, cbb_ref, dtb_ref, alog_ref,
             dsk_ref, g_ref, tri_ref, exp_ref, y_all, hs_all, o_all, px_all, pb_all, state):
        @pl.when(pl.program_id(0) == 0)
        def _():
            state[...] = jnp.zeros_like(state)

        keep = (pl.program_id(0) > 0).astype(F32)
        for sc in range(SSD_STEP // CHUNK):
            rows = pl.ds(sc * CHUNK, CHUNK)
            before = pl.ds(sc * CHUNK - 8, 8)
            hx = hx_ref[...] * keep if sc == 0 else xs_all[before, :]
            hb = hb_ref[...] * keep if sc == 0 else bc_all[before, :]
            chunk(z_all.at[rows], xs_all.at[rows], bc_all.at[rows], dt_all.at[rows], hx, hb, cwx_ref, cbx_ref, cwb_ref,
                  cbb_ref, dtb_ref, alog_ref, dsk_ref, g_ref, tri_ref, exp_ref, y_all.at[rows],
                  hs_all.at[pl.ds(sc, 1)], o_all.at[rows], px_all.at[rows], pb_all.at[rows], state)

    def chunk(z_ref, xs_ref, bc_ref, dt_ref, hx, hb, cwx_ref, cbx_ref, cwb_ref, cbb_ref, dtb_ref, alog_ref,
              dsk_ref, g_ref, tri_ref, exp_ref, y_ref, hs_ref, o_ref, px_ref, pb_ref, state):
        pre_x = _conv4(xs_ref[...], hx, cwx_ref, cbx_ref)
        pre_b = _conv4(bc_ref[...], hb, cwb_ref, cbb_ref)
        px_ref[...] = pre_x.astype(BF16)
        pb_ref[...] = pre_b.astype(BF16)
        v = _ssd_common(pre_x, pre_b, dt_ref, dtb_ref, alog_ref, tri_ref, exp_ref)
        acum_t = v["acum"].T
        xdt, ba = v["xdt"], v["ba"]
        h_in = state[...]
        hs_ref[0] = h_in
        xg = xdt * v["gam_e"]
        m0 = lax.broadcasted_iota(jnp.int32, (CHUNK, 128), 1) < HEAD_DIM
        for g in range(2):
            bg = ba[:, g * 128:(g + 1) * 128].astype(BF16)
            cg = ba[:, 256 + g * 128:256 + (g + 1) * 128].astype(BF16)
            gl = slice(g * 512, (g + 1) * 512)
            cb = _dot(cg, bg, NT)
            y_off = _dot(cg, h_in[:, gl].astype(BF16)) * v["lam_e"][:, gl]
            s_new = _dot(bg.T, xg[:, gl].astype(BF16))
            state[:, gl] = h_in[:, gl] * v["lam_e"][CHUNK - 1:CHUNK, gl] + s_new
            for j in range(4):
                h0 = 8 * g + 2 * j
                ln = slice(g * 512 + j * 128, g * 512 + (j + 1) * 128)
                xp = xdt[:, ln].astype(BF16)
                y0 = _dot((cb * _decay(acum_t, h0, False)).astype(BF16), xp)
                y1 = _dot((cb * _decay(acum_t, h0 + 1, False)).astype(BF16), xp)
                y_ref[:, ln] = jnp.where(m0, y0, y1) + y_off[:, j * 128:(j + 1) * 128]
        z = z_ref[...]
        yg = (y_ref[...] + dsk_ref[...] * v["xa"]) * (z * _sigmoid(z))
        r = lax.rsqrt(jnp.mean(yg * yg, axis=-1, keepdims=True) + EPS)
        o_ref[...] = (yg * r * g_ref[...]).astype(BF16)

    full = lambda a: pl.BlockSpec(a.shape, lambda c: (0,) * a.ndim)
    return pl.pallas_call(
        body, name="ssd_fwd", grid=(nc,),
        in_specs=[col(1024, 1), col(1024, 2), col(512, 6), col(128, 28), halo(1024, 2), halo(512, 6),
                  full(cwx), full(cbx), full(cwb), full(cbb), full(dtb), full(alog), full(dsk_e), full(norm_g),
                  full(tri), full(expand)],
        out_specs=[pl.BlockSpec((SSD_STEP, SSM_INNER), lambda c: (c, 0)),
                   pl.BlockSpec((SSD_STEP // CHUNK, 128, SSM_INNER), lambda c: (c, 0, 0)),
                   pl.BlockSpec((SSD_STEP, SSM_INNER), lambda c: (c, 0)),
                   pl.BlockSpec((SSD_STEP, SSM_INNER), lambda c: (c, 0)), pl.BlockSpec((SSD_STEP, 512), lambda c: (c, 0))],
        out_shape=[SDS((t, SSM_INNER), F32), SDS((t // CHUNK, 128, SSM_INNER), F32), SDS((t, SSM_INNER), BF16),
                   SDS((t, SSM_INNER), BF16), SDS((t, 512), BF16)],
        scratch_shapes=[pltpu.VMEM((128, SSM_INNER), F32)],
        compiler_params=_params("arbitrary"))(proj, proj, proj, proj, proj, proj, cwx, cbx, cwb, cbb, dtb, alog,
                                              dsk_e, norm_g, tri, expand)


def _ssd_bwd(proj, pre_x, pre_b, y_ssd, hs, dout, cwx, cwb, dtb, alog, dsk_e, norm_g, tri, triu, expand, expand_t):
    t = proj.shape[0]
    nc, ch, col, halo = _ssd_specs(t, True)

    def body(z_all, xs_all, bc_all, dt_all, px_all, pb_all, y_all, hin_all, do_all,
             cwx_ref, cwb_ref, dtb_ref, alog_ref, dsk_ref, g_ref, tri_ref, triu_ref, exp_ref, expt_ref,
             dz_all, dxs_all, dbc_all, ddt_all, dg_ref, ddsk_ref, dalog_ref, ddtb_ref, dcwx_ref, dcbx_ref, dcwb_ref,
             dcbb_ref, gstate, nx_x, nx_b, dact_b, dxdt_s):
        @pl.when(pl.program_id(0) == 0)
        def _():
            gstate[...] = jnp.zeros_like(gstate)
            nx_x[...] = jnp.zeros_like(nx_x)
            nx_b[...] = jnp.zeros_like(nx_b)
            for ref in (dg_ref, ddsk_ref, dalog_ref, ddtb_ref, dcwx_ref, dcbx_ref, dcwb_ref, dcbb_ref):
                ref[...] = jnp.zeros_like(ref)

        for sc in reversed(range(SSD_STEP // CHUNK)):
            rows = pl.ds(sc * CHUNK, CHUNK)
            by_rows = [r.at[rows] for r in (z_all, xs_all, bc_all, dt_all, px_all, pb_all, y_all)]
            outs = [r.at[rows] for r in (dz_all, dxs_all, dbc_all, ddt_all)]
            chunk(*by_rows, hin_all.at[pl.ds(sc, 1)], do_all.at[rows],
                  cwx_ref, cwb_ref, dtb_ref, alog_ref, dsk_ref, g_ref, tri_ref, triu_ref, exp_ref, expt_ref,
                  *outs, dg_ref, ddsk_ref, dalog_ref, ddtb_ref, dcwx_ref, dcbx_ref, dcwb_ref, dcbb_ref,
                  gstate, nx_x, nx_b, dact_b, dxdt_s)

    def chunk(z_ref, xs_ref, bc_ref, dt_ref, px_ref, pb_ref, y_ref, hin_ref, do_ref,
              cwx_ref, cwb_ref, dtb_ref, alog_ref, dsk_ref, g_ref, tri_ref, triu_ref, exp_ref, expt_ref,
              dz_ref, dxs_ref, dbc_ref, ddt_ref, dg_ref, ddsk_ref, dalog_ref, ddtb_ref, dcwx_ref, dcbx_ref, dcwb_ref,
              dcbb_ref, gstate, nx_x, nx_b, dact_b, dxdt_s):
        v = _ssd_common(px_ref[...].astype(F32), pb_ref[...].astype(F32), dt_ref, dtb_ref, alog_ref, tri_ref, exp_ref)
        acum_t = v["acum"].T
        xa, ba, xdt, dtv = v["xa"], v["ba"], v["xdt"], v["dtv"]
        lam_e, gam_e, dt_e = v["lam_e"], v["gam_e"], v["dt_e"]
        z = z_ref[...]
        y = y_ref[...]
        sz = _sigmoid(z)
        zs = z * sz
        y_tot = y + dsk_ref[...] * xa
        yg = y_tot * zs
        r = lax.rsqrt(jnp.mean(yg * yg, axis=-1, keepdims=True) + EPS)
        yh = yg * r
        do = do_ref[...]
        dg_ref[...] += jnp.sum(do * yh, axis=0, keepdims=True)
        gd = do * g_ref[...]
        dyg = r * (gd - yh * jnp.mean(gd * yh, axis=-1, keepdims=True))
        dz_ref[...] = (dyg * y_tot * (sz * (1.0 + z * (1.0 - sz)))).astype(BF16)
        dy = dyg * zs
        ddsk_ref[...] += jnp.sum(dy * xa, axis=0, keepdims=True)
        g_out = gstate[...]
        h_in = hin_ref[0]
        lam_dy = lam_e * dy
        gam_x = gam_e * xdt
        m0 = lax.broadcasted_iota(jnp.int32, (CHUNK, 128), 1) < HEAD_DIM
        lane = lax.broadcasted_iota(jnp.int32, (CHUNK, 128), 1)
        below = (lax.broadcasted_iota(jnp.int32, (CHUNK, CHUNK), 0) >
                 lax.broadcasted_iota(jnp.int32, (CHUNK, CHUNK), 1))
        da_in = jnp.zeros((CHUNK, 128), F32)
        off_y, off_x = [], []
        for g in range(2):
            bg = ba[:, g * 128:(g + 1) * 128].astype(BF16)
            cg = ba[:, 256 + g * 128:256 + (g + 1) * 128].astype(BF16)
            gl = slice(g * 512, (g + 1) * 512)
            gg = g_out[:, gl].astype(BF16)
            cb = _dot(cg, bg, NT)
            dxdt_off = _dot(bg, gg) * gam_e[:, gl]
            off_x.append(xdt[:, gl] * dxdt_off)
            off_y.append(dy[:, gl] * (_dot(cg, h_in[:, gl].astype(BF16)) * lam_e[:, gl]))
            q_sum = jnp.zeros((CHUNK, CHUNK), F32)
            for j in range(4):
                h0 = 8 * g + 2 * j
                ln = slice(g * 512 + j * 128, g * 512 + (j + 1) * 128)
                dyp = dy[:, ln]
                dyb = dyp.astype(BF16)
                xpb = xdt[:, ln].astype(BF16)
                dec = [_decay(acum_t, h0, False), _decay(acum_t, h0 + 1, False)]
                mix = [cb * dec[0], cb * dec[1]]
                d0 = _dot(mix[0].T.astype(BF16), dyb)
                d1 = _dot(mix[1].T.astype(BF16), dyb)
                dxdt_s[:, ln] = jnp.where(m0, d0, d1) + dxdt_off[:, j * 128:(j + 1) * 128]
                for e, (hh, dym) in enumerate(((h0, jnp.where(m0, dyp, 0.0)), (h0 + 1, jnp.where(m0, 0.0, dyp)))):
                    dyx = _dot(dym.astype(BF16), xpb, NT)
                    q_sum = q_sum + dyx * dec[e]
                    reach = jnp.where(below, _hdot(triu_ref[...], dyx * mix[e]), 0.0)
                    da_in = jnp.where(lane == hh, jnp.sum(reach, axis=-1, keepdims=True), da_in)
            gstate[:, gl] = g_out[:, gl] * lam_e[CHUNK - 1:CHUNK, gl] + _dot(cg.T, lam_dy[:, gl].astype(BF16))
            qb = q_sum.astype(BF16)
            dact_b[:, 256 + g * 128:256 + (g + 1) * 128] = (
                _dot(qb, bg) + _dot(lam_dy[:, gl].astype(BF16), h_in[:, gl].astype(BF16), NT))
            dact_b[:, g * 128:(g + 1) * 128] = _dot(qb.T, cg) + _dot(gam_x[:, gl].astype(BF16), gg, NT)
        dxdt = dxdt_s[...]
        seg_y = _hdot(jnp.concatenate(off_y, axis=1), expt_ref[...])
        seg_x = _hdot(jnp.concatenate(off_x, axis=1), expt_ref[...])
        e_col = jnp.sum(g_out * h_in * lam_e[CHUNK - 1:CHUNK, :], axis=0, keepdims=True)
        e_seg = _hdot(jnp.broadcast_to(e_col, (8, SSM_INNER)), expt_ref[...])[0:1, :]
        da = da_in + _hdot(triu_ref[...], seg_y) + (_hdot(tri_ref[...], seg_x) - seg_x) + e_seg
        a_neg = v["a_neg"]
        ddtv = da * a_neg + _hdot(dxdt * xa, expt_ref[...])
        dalog_ref[...] += jnp.sum(da * dtv, axis=0, keepdims=True) * a_neg
        lane16 = lax.broadcasted_iota(jnp.int32, (CHUNK, 128), 1) < SSM_HEADS
        draw = jnp.where(lane16, ddtv * _sigmoid(dt_ref[...] + dtb_ref[...]), 0.0)
        ddtb_ref[...] += jnp.sum(draw, axis=0, keepdims=True)
        ddt_ref[...] = draw.astype(BF16)
        dxa = dxdt * dt_e + dy * dsk_ref[...]
        for (dact, pre, x_ref, nx, cw_ref, dcw_ref, dcb_ref, dx_ref) in (
                (dxa, v["pre_x"], xs_ref, nx_x, cwx_ref, dcwx_ref, dcbx_ref, dxs_ref),
                (dact_b[...], v["pre_b"], bc_ref, nx_b, cwb_ref, dcwb_ref, dcbb_ref, dbc_ref)):
            sp = _sigmoid(pre)
            dpre = dact * (sp * (1.0 + pre * (1.0 - sp)))
            dcb_ref[...] += jnp.sum(dpre, axis=0, keepdims=True)
            xv = x_ref[...]
            nxt = nx[...]
            dx = cw_ref[3:4, :] * dpre
            dcw_ref[3:4, :] += jnp.sum(dpre * xv, axis=0, keepdims=True)
            for k in range(3):
                d_up = _shift_up(dpre, nxt, 3 - k)
                dcw_ref[k:k + 1, :] += jnp.sum(xv * d_up, axis=0, keepdims=True)
                dx = dx + cw_ref[k:k + 1, :] * d_up
            nx[...] = dpre[0:8, :]
            dx_ref[...] = dx.astype(dx_ref.dtype)

    full = lambda a: pl.BlockSpec(a.shape, lambda c: (0,) * a.ndim)
    rowblk = lambda w: pl.BlockSpec((SSD_STEP, w), lambda c: (ch(c), 0))
    acc = lambda a, b: pl.BlockSpec((a, b), lambda c: (0, 0))
    return pl.pallas_call(
        body, name="ssd_bwd", grid=(nc,),
        in_specs=[col(1024, 1), col(1024, 2), col(512, 6), col(128, 28), rowblk(SSM_INNER), rowblk(512),
                  rowblk(SSM_INNER),
                  pl.BlockSpec((SSD_STEP // CHUNK, 128, SSM_INNER), lambda c: (ch(c), 0, 0)),
                  rowblk(SSM_INNER),
                  full(cwx), full(cwb), full(dtb), full(alog), full(dsk_e), full(norm_g),
                  full(tri), full(triu), full(expand), full(expand_t)],
        out_specs=[rowblk(SSM_INNER), rowblk(SSM_INNER), rowblk(512), rowblk(128),
                   acc(1, 1024), acc(1, 1024), acc(1, 128), acc(1, 128), acc(4, 1024), acc(1, 1024), acc(4, 512),
                   acc(1, 512)],
        out_shape=[SDS((t, SSM_INNER), BF16), SDS((t, SSM_INNER), BF16), SDS((t, 512), BF16), SDS((t, 128), BF16),
                   SDS((1, 1024), F32), SDS((1, 1024), F32), SDS((1, 128), F32), SDS((1, 128), F32),
                   SDS((4, 1024), F32), SDS((1, 1024), F32), SDS((4, 512), F32), SDS((1, 512), F32)],
        scratch_shapes=[pltpu.VMEM((128, SSM_INNER), F32), pltpu.VMEM((8, 1024), F32), pltpu.VMEM((8, 512), F32),
                        pltpu.VMEM((CHUNK, 512), F32), pltpu.VMEM((CHUNK, SSM_INNER), F32)],
        compiler_params=_params("arbitrary"))(proj, proj, proj, proj, pre_x, pre_b, y_ssd, hs, dout,
                                              cwx, cwb, dtb, alog, dsk_e, norm_g, tri, triu, expand, expand_t)


def _conv3(x, halo, w_ref, b_ref, part):
    acc = b_ref[part] + w_ref[2, part] * x
    for k in range(2):
        acc = acc + w_ref[k, part] * _shift_down(x, halo, 2 - k)
    return acc


def _up_act(x, g, w_up_t, cw, cb, tm=2048, tn=256, tr=512):
    t, k = x.shape
    nj = D_FF // tn

    def body(x_ref, g_ref, wg_ref, wv_ref, w_ref, b_ref, u_ref, c_ref, h_ref, f_ref, halo):
        i, j = pl.program_id(0), pl.program_id(1)

        @pl.when(j == 0)
        def _():
            xv = x_ref[...]
            r = lax.rsqrt(jnp.mean(xv * xv, axis=-1, keepdims=True) + EPS)
            h_ref[...] = (xv * r * g_ref[...]).astype(BF16)

        @pl.when(i == 0)
        def _():
            halo[j] = jnp.zeros((2, 8, tn), F32)

        def matmuls(r):
            rows = slice(r * tr, (r + 1) * tr)
            return [_dot(h_ref[rows, :], wt_ref[...], NT) for wt_ref in (wg_ref, wv_ref)]

        def epilogue(r, us, before):
            rows = slice(r * tr, (r + 1) * tr)
            parts = []
            for part, u in enumerate(us):
                u_ref[part, rows, :] = u.astype(BF16)
                parts.append(_conv3(u, before[part], w_ref, b_ref, part))
                c_ref[part, rows, :] = parts[-1].astype(BF16)
            gate, val = parts
            f_ref[rows, :] = (gate * _sigmoid(gate) * val).astype(BF16)
            return [u[tr - 8:, :] for u in us]

        before = [halo[j, 0], halo[j, 1]]
        pending = matmuls(0)
        for r in range(tm // tr):
            ahead = matmuls(r + 1) if r + 1 < tm // tr else None
            before = epilogue(r, pending, before)
            pending = ahead
        halo[j, 0], halo[j, 1] = before

    return pl.pallas_call(
        body, name="up_proj", grid=(t // tm, nj),
        in_specs=[pl.BlockSpec((tm, k), lambda i, j: (i, 0)), pl.BlockSpec((1, k), lambda i, j: (0, 0)),
                  pl.BlockSpec((tn, k), lambda i, j: (j, 0)), pl.BlockSpec((tn, k), lambda i, j: (j + nj, 0)),
                  pl.BlockSpec((3, 2, 1, tn), lambda i, j: (0, 0, 0, j)), pl.BlockSpec((2, 1, tn), lambda i, j: (0, 0, j))],
        out_specs=[pl.BlockSpec((2, tm, tn), lambda i, j: (0, i, j)), pl.BlockSpec((2, tm, tn), lambda i, j: (0, i, j)),
                   pl.BlockSpec((tm, k), lambda i, j: (i, 0)), pl.BlockSpec((tm, tn), lambda i, j: (i, j))],
        out_shape=[SDS((2, t, D_FF), BF16), SDS((2, t, D_FF), BF16), SDS((t, k), BF16), SDS((t, D_FF), BF16)],
        scratch_shapes=[pltpu.VMEM((nj, 2, 8, tn), F32)],
        compiler_params=_params("arbitrary", "arbitrary"))(x, g, w_up_t, w_up_t, cw, cb)


def _ffn_bwd(dx2, w_down, u, c, cw, tm=512, tn=1408):
    t = u.shape[1]
    nj, ni = D_FF // tn, t // tm
    rev = lambda i: ni - 1 - i

    def body(dx_ref, wd_ref, u_ref, c_ref, w_ref, du_ref, dcw_ref, dcb_ref, nxt):
        i = pl.program_id(1)

        @pl.when(i == 0)
        def _():
            nxt[...] = jnp.zeros_like(nxt)
            dcw_ref[...] = jnp.zeros_like(dcw_ref)
            dcb_ref[...] = jnp.zeros_like(dcb_ref)

        df = _dot(dx_ref[...].astype(BF16), wd_ref[...], NT)
        gate, val = c_ref[0].astype(F32), c_ref[1].astype(F32)
        sg = _sigmoid(gate)
        dgate = df * val * (sg * (1.0 + gate * (1.0 - sg)))
        dval = df * (gate * sg)
        for part, d in enumerate((dgate, dval)):
            uu = u_ref[part].astype(F32)
            dcb_ref[part] += jnp.sum(d, axis=0, keepdims=True)
            ahead = nxt[part]
            acc = w_ref[2, part] * d
            dcw_ref[2, part] += jnp.sum(d * uu, axis=0, keepdims=True)
            for k in range(2):
                d_up = _shift_up(d, ahead, 2 - k)
                dcw_ref[k, part] += jnp.sum(uu * d_up, axis=0, keepdims=True)
                acc = acc + w_ref[k, part] * d_up
            nxt[part] = d[0:8, :]
            du_ref[part] = acc.astype(BF16)

    w_spec = pl.BlockSpec((3, 2, 1, tn), lambda j, i: (0, 0, 0, j))
    b_spec = pl.BlockSpec((2, 1, tn), lambda j, i: (0, 0, j))
    tile = pl.BlockSpec((2, tm, tn), lambda j, i: (0, rev(i), j))
    return pl.pallas_call(
        body, name="ffn_bwd", grid=(nj, ni),
        in_specs=[pl.BlockSpec((tm, D_MODEL), lambda j, i: (rev(i), 0)), pl.BlockSpec((tn, D_MODEL), lambda j, i: (j, 0)),
                  tile, tile, w_spec],
        out_specs=[tile, w_spec, b_spec],
        out_shape=[SDS((2, t, D_FF), BF16), SDS((3, 2, 1, D_FF), F32), SDS((2, 1, D_FF), F32)],
        scratch_shapes=[pltpu.VMEM((2, 8, tn), F32)],
        compiler_params=_params("parallel", "arbitrary"))(dx2, w_down, u, c, cw)


def _down_ple_loss(x1, f, w_down, g, w_gate, p, w_proj_t, target, tm=512):
    t = x1.shape[0]

    def body(x_ref, f_ref, wd_ref, g_ref, wg_ref, p_ref, wp_ref, tg_ref, dx_ref, dpre_ref, dpp_ref, h_ref, loss_ref,
             dg_ref):
        i = pl.program_id(0)
        xv = x_ref[...] + _dot(f_ref[...], wd_ref[...])
        r = lax.rsqrt(jnp.mean(xv * xv, axis=-1, keepdims=True) + EPS)
        xh = xv * r
        h = (xh * g_ref[...]).astype(BF16)
        h_ref[...] = h
        gate = _sigmoid(_dot(h, wg_ref[...]))
        pp = _dot(p_ref[...].astype(BF16), wp_ref[...], NT)
        err = (xv + gate * pp) - tg_ref[...]

        @pl.when(i == 0)
        def _():
            loss_ref[...] = jnp.zeros_like(loss_ref)
            dg_ref[...] = jnp.zeros_like(dg_ref)

        loss_ref[...] += 0.5 * jnp.sum(jnp.mean(err * err, axis=-1, keepdims=True), axis=0, keepdims=True)
        dy = err * (1.0 / D_MODEL)
        dpre = (dy * pp * gate * (1.0 - gate)).astype(BF16)
        dpre_ref[...] = dpre
        dpp_ref[...] = (dy * gate).astype(BF16)
        dh = _dot(dpre, wg_ref[...], NT)
        dg_ref[...] += jnp.sum(dh * xh, axis=0, keepdims=True)
        gd = dh * g_ref[...]
        dx_ref[...] = dy + r * (gd - xh * jnp.mean(gd * xh, axis=-1, keepdims=True))

    row = lambda w: pl.BlockSpec((tm, w), lambda i: (i, 0))
    full = lambda a: pl.BlockSpec(a.shape, lambda i: (0, 0))
    return pl.pallas_call(
        body, name="down_ple_loss", grid=(t // tm,),
        in_specs=[row(D_MODEL), row(D_FF), full(w_down), full(g), full(w_gate), row(PLE_DIM), full(w_proj_t),
                  row(D_MODEL)],
        out_specs=[row(D_MODEL), row(D_MODEL), row(D_MODEL), row(D_MODEL),
                   pl.BlockSpec((1, 128), lambda i: (0, 0)), pl.BlockSpec((1, D_MODEL), lambda i: (0, 0))],
        out_shape=[SDS((t, D_MODEL), F32), SDS((t, D_MODEL), BF16), SDS((t, D_MODEL), BF16), SDS((t, D_MODEL), BF16),
                   SDS((1, 128), F32), SDS((1, D_MODEL), F32)],
        compiler_params=_params("arbitrary"))(x1, f, w_down, g, w_gate, p, w_proj_t, target)


def _all_gather(arrays, name):
    n_a = len(arrays)

    def body(*refs):
        src, dst = refs[:n_a], refs[n_a:2 * n_a]
        send_sems, recv_sems, local_sems = refs[2 * n_a:]
        x, y, c = lax.axis_index("x"), lax.axis_index("y"), lax.axis_index("c")
        slot = lambda px, py, pc: 4 * px + 2 * py + pc
        me, sibling = (x, y, c), (x, y, 1 - c)
        chips = [(1 - x, y), (x, 1 - y), (1 - x, 1 - y)]

        def copy(a, k, block, to, own=False):
            return pltpu.make_async_remote_copy(
                src_ref=src[a] if own else dst[a].at[slot(*block)], dst_ref=dst[a].at[slot(*block)],
                send_sem=send_sems.at[a, k], recv_sem=recv_sems.at[a, k], device_id=to,
                device_id_type=pl.DeviceIdType.MESH)

        local = [pltpu.make_async_copy(src[a], dst[a].at[slot(*me)], local_sems.at[a]) for a in range(n_a)]
        for cp in local:
            cp.start()
        sends = []
        for a in range(n_a):
            sends.append(copy(a, 0, me, sibling, own=True))
            sends += [copy(a, 1 + j, me, (*chip, c), own=True) for j, chip in enumerate(chips)]
        for cp in sends:
            cp.start()
        for j, chip in enumerate(chips):
            for a in range(n_a):
                copy(a, 1 + j, (*chip, c), me).wait_recv()
                passed = copy(a, 4 + j, (*chip, c), sibling)
                passed.start()
                sends.append(passed)
        for a in range(n_a):
            copy(a, 0, sibling, me).wait_recv()
            for j, chip in enumerate(chips):
                copy(a, 4 + j, (*chip, 1 - c), me).wait_recv()
        for cp in sends:
            cp.wait_send()
        for cp in local:
            cp.wait()

    hbm = pl.BlockSpec(memory_space=pl.ANY)
    return pl.pallas_call(
        body, name=name, in_specs=[hbm] * n_a, out_specs=[hbm] * n_a,
        out_shape=[SDS((N_DEV,) + a.shape, a.dtype) for a in arrays],
        scratch_shapes=[pltpu.SemaphoreType.DMA((n_a, N_DEV - 1)), pltpu.SemaphoreType.DMA((n_a, N_DEV - 1)),
                        pltpu.SemaphoreType.DMA((n_a,))],
        )(*arrays)


def _peer(k):
    x, y, c = lax.axis_index("x"), lax.axis_index("y"), lax.axis_index("c")
    px = 1 - x if k & 4 else x
    py = 1 - y if k & 2 else y
    pc = 1 - c if k & 1 else c
    return (px, py, pc), 4 * px + 2 * py + pc


_HBM = pl.BlockSpec(memory_space=pltpu.HBM)
_SEM = pl.BlockSpec(memory_space=pltpu.SEMAPHORE)


def _split_copies(src, land, send_sems, recv_sems, scatter, arrivals):
    _, me = _peer(0)
    out = []
    for k in range(1, N_DEV):
        coords, peer = _peer(k)
        for a in range(len(src)):
            sem = a * (N_DEV - 1) + k - 1
            if scatter[a]:
                s, d = src[a].at[peer], land[a].at[k]
            else:
                s, d = src[a], land[a].at[peer if arrivals else me]
            out.append(pltpu.make_async_remote_copy(
                src_ref=s, dst_ref=d, send_sem=send_sems.at[sem], recv_sem=recv_sems.at[sem], device_id=coords,
                device_id_type=pl.DeviceIdType.MESH))
    return out


def _exchange_start(srcs, lands, scatter, name):
    n = len(srcs)

    def body(*refs):
        src, land = refs[:n], refs[n:2 * n]
        send_sems, recv_sems = refs[2 * n], refs[2 * n + 1]
        token = refs[-1]
        for cp in _split_copies(src, land, send_sems, recv_sems, scatter, False):
            cp.start()
        token[...] = jnp.zeros_like(token)

    hbm_shape = lambda a: pltpu.HBM(a.shape, a.dtype)
    sem_shape = pltpu.SemaphoreType.DMA((n * (N_DEV - 1),))
    outs = pl.pallas_call(
        body, name=name,
        out_shape=(sem_shape, sem_shape, *[hbm_shape(a) for a in srcs], *[hbm_shape(a) for a in lands],
                   SDS((8, 128), F32)),
        in_specs=[_HBM] * (2 * n), out_specs=(_SEM, _SEM, *[_HBM] * (2 * n), pl.BlockSpec(memory_space=pltpu.VMEM)),
        input_output_aliases={a: 2 + a for a in range(2 * n)},
        compiler_params=pltpu.CompilerParams(has_side_effects=pltpu.SideEffectType.DATAFLOW_SIDE_EFFECTING),
    )(*[pltpu.with_memory_space_constraint(a, pltpu.HBM) for a in list(srcs) + list(lands)])
    return outs[0], outs[1], outs[2:2 + n], outs[2 + n:2 + 2 * n], outs[-1]


def _exchange_wait(send_sems, recv_sems, srcs, lands, scatter, after, name):
    n = len(srcs)

    def body(*refs):
        src, land = refs[:n], refs[n:2 * n]
        for cp in _split_copies(src, land, refs[2 * n], refs[2 * n + 1], scatter, False):
            cp.wait_send()
        for cp in _split_copies(src, land, refs[2 * n], refs[2 * n + 1], scatter, True):
            cp.wait_recv()

    hbm_shape = lambda a: pltpu.HBM(a.shape, a.dtype)
    outs = pl.pallas_call(
        body, name=name, out_shape=tuple(hbm_shape(a) for a in list(srcs) + list(lands)),
        in_specs=[_HBM] * (2 * n) + [_SEM, _SEM, pl.BlockSpec(memory_space=pl.ANY)], out_specs=(_HBM,) * (2 * n),
        input_output_aliases={a: a for a in range(2 * n)},
        compiler_params=pltpu.CompilerParams(has_side_effects=pltpu.SideEffectType.DATAFLOW_SIDE_EFFECTING),
    )(*srcs, *lands, send_sems, recv_sems, after)
    return outs[:n], outs[n:]


def _reduce8(a, tr, name):
    _, rows, cols = a.shape

    def body(a_ref, o_ref):
        acc = a_ref[0]
        for j in range(1, N_DEV):
            acc = acc + a_ref[j]
        o_ref[...] = acc

    return pl.pallas_call(
        body, name=name, grid=(rows // tr,),
        in_specs=[pl.BlockSpec((N_DEV, tr, cols), lambda i: (0, i, 0))],
        out_specs=pl.BlockSpec((tr, cols), lambda i: (i, 0)), out_shape=SDS((rows, cols), F32),
        compiler_params=_params("parallel"))(a)


def _reduce_landed(own, land, name, tc=256):
    rows, cols = own.shape

    def body(own_ref, land_ref, o_ref):
        acc = own_ref[...].astype(F32)
        for k in range(1, N_DEV):
            acc = acc + land_ref[k].astype(F32)
        o_ref[...] = acc

    return pl.pallas_call(
        body, name=name, grid=(cols // tc,),
        in_specs=[pl.BlockSpec((rows, tc), lambda j: (0, j)), pl.BlockSpec((N_DEV, rows, tc), lambda j: (0, 0, j))],
        out_specs=pl.BlockSpec((rows, tc), lambda j: (0, j)), out_shape=SDS((rows, cols), F32),
        compiler_params=_params("parallel"))(own, land)


def _reduce_adamw(own, land, w, m, v, name, tc=256):
    rows, cols = own.shape

    def body(own_ref, land_ref, w_ref, m_ref, v_ref, g_ref, d_ref, mo_ref, vo_ref):
        g = own_ref[...].astype(F32)
        for k in range(1, N_DEV):
            g = g + land_ref[k].astype(F32)
        g_ref[...] = g
        d_ref[...], mo_ref[...], vo_ref[...] = _adam_update(w_ref[...], g, m_ref[...], v_ref[...])

    blk = pl.BlockSpec((rows, tc), lambda j: (0, j))
    return pl.pallas_call(
        body, name=name, grid=(cols // tc,),
        in_specs=[blk, pl.BlockSpec((N_DEV, rows, tc), lambda j: (0, 0, j)), blk, blk, blk], out_specs=[blk] * 4,
        out_shape=[SDS((rows, cols), F32)] * 4, compiler_params=_params("parallel"))(own, land, w, m, v)


def _adamw(w, g, m, v, name, tr=None):
    rows, cols = w.shape
    tr = rows if tr is None else tr

    def body(w_ref, g_ref, m_ref, v_ref, d_ref, mo_ref, vo_ref):
        d_ref[...], mo_ref[...], vo_ref[...] = _adam_update(w_ref[...], g_ref[...], m_ref[...], v_ref[...])

    blk = pl.BlockSpec((tr, cols), lambda i: (i, 0))
    return pl.pallas_call(
        body, name=name, grid=(rows // tr,), in_specs=[blk] * 4, out_specs=[blk] * 3,
        out_shape=[SDS((rows, cols), F32)] * 3, compiler_params=_params("parallel"))(w, g, m, v)


def _pad_rows(a, rows):
    return jnp.pad(a, ((0, rows - a.shape[0]),) + ((0, 0),) * (a.ndim - 1))


def _local_step(x, p, target, sm, wts, fetch_rest, send, tok):
    ones_q, ones_k, dup, dup_t = _head_consts()
    tri, triu, expand, expand_t = _ssd_consts()
    w_in_t = wts["in_t"]
    cwx, cwb = wts["ssm_cw"][:, :SSM_INNER], wts["ssm_cw"][:, SSM_INNER:]
    cbx, cbb = sm["ssm_conv_b"][:, :SSM_INNER], sm["ssm_conv_b"][:, SSM_INNER:]
    pad128 = lambda a: jnp.pad(a, ((0, 0), (0, 128 - a.shape[1])))
    dtb, alog = pad128(sm["dt_bias"]), pad128(sm["a_log"])
    dsk_e = jnp.repeat(sm["d_skip"], HEAD_DIM, axis=1)
    gq = jnp.tile(sm["q_norm_g"], (1, ATTN_DIM // HEAD_DIM))
    gk = jnp.tile(sm["k_norm_g"], (1, KV_DIM // HEAD_DIM))
    ffn_cw = wts["ffn_cw"].reshape(3, 2, 1, D_FF)
    ffn_cb = sm["ffn_conv_b"].reshape(2, 1, D_FF)

    proj, h1 = _norm_matmul(x, sm["attn_norm_g"] + tok, w_in_t, 512, 3840, "in_proj")
    qn, kd, vd = _attn_prep(proj, gq, gk, ones_q, ones_k, dup)
    attn_out, lse = _attn_fwd(qn, kd, vd)
    y_ssd, hs, ssm_out, pre_x, pre_b = _ssd_fwd(proj, cwx, cbx, cwb, cbb, dtb, alog, dsk_e, sm["ssm_norm_g"], tri,
                                                expand)
    rest = fetch_rest(ssm_out)
    w_out, w_up_t, w_down, w_gate, w_proj_t = (rest[k] for k in ("out", "up_t", "down", "gate", "proj_t"))
    x1 = _mm_resid([(attn_out, None, w_out[:ATTN_DIM]), (ssm_out, None, w_out[ATTN_DIM:])], x, None, 1024, F32,
                   "out_proj")
    u, uc, h2, f = _up_act(x1, sm["ffn_norm_g"], w_up_t, ffn_cw, ffn_cb)
    dx2, dpre, dpp, h3, loss, dg_ple = _down_ple_loss(x1, f, w_down, sm["ple_norm_g"], w_gate, p, w_proj_t, target)

    g_gate = _wgrad(h3, None, dpre, "wg_gate")
    g_proj_t = _wgrad(dpp, None, p, "wg_proj")
    g_down = _wgrad(f, None, dx2, "wg_down")
    du, d_ffn_cw, d_ffn_cb = _ffn_bwd(dx2, w_down, u, uc, ffn_cw)
    dx1, dg_ffn = _mm_normbwd([(du, 0, w_up_t, D_FF, 0), (du, 1, w_up_t, D_FF, 1)], x1, sm["ffn_norm_g"], dx2, 512,
                              "up_proj_bwd")
    g_up_t = _wgrad(du, "all", h2, "wg_up")
    tok = send(dict(gate=g_gate, proj_t=g_proj_t, down=g_down, up_t=g_up_t)).astype(BF16)
    d_attn, d_ssm = _out_proj_bwd(dx1, w_out + tok)
    g_out = _wgrad_multi([attn_out, ssm_out], dx1, "wg_out")
    tok = send(dict(out=g_out))
    (dz, dxs, dbc, ddt, dg_ssm, d_dsk_e, d_alog, d_dtb, d_cwx, d_cbx, d_cwb, d_cbb) = _ssd_bwd(
        proj, pre_x, pre_b, y_ssd, hs, d_ssm, cwx, cwb, dtb + tok, alog, dsk_e, sm["ssm_norm_g"], tri, triu, expand,
        expand_t)
    dqn, dkc, dkp, dvc, dvp = _attn_bwd(qn, kd, vd, attn_out, lse, d_attn, ones_k[:128, :128])
    dqkv, dgq, dgk = _attn_prep_bwd(proj, dqn, dkc, dkp, dvc, dvp, gq + tok, gk, ones_q, ones_k, dup_t)
    pieces = [(dqkv, 0, 1024), (dz, 1024, 2048), (dxs, 2048, 3072), (dbc, 3072, 3584), (ddt, 3584, 3712)]
    g_in_t = jnp.concatenate([_wgrad_multi([dqkv, dz], h1, "wg_in_qkvz"),
                              _wgrad_multi([dxs, dbc, ddt], h1, "wg_in_xbcdt")], axis=0)[:IN_PROJ]
    tok = send(dict(in_t=g_in_t))
    grad_x, dg_attn = _mm_normbwd([(a, None, w_in_t, hi - lo, lo // (hi - lo)) for a, lo, hi in pieces], x,
                                  sm["attn_norm_g"] + tok, dx1, 512, "in_proj_bwd")

    small = dict(
        attn_norm_g=dg_attn, q_norm_g=dgq.reshape(-1, HEAD_DIM).sum(0, keepdims=True),
        k_norm_g=dgk.reshape(-1, HEAD_DIM).sum(0, keepdims=True),
        ssm_conv_w=jnp.concatenate([d_cwx, d_cwb], axis=1), ssm_conv_b=jnp.concatenate([d_cbx, d_cbb], axis=1),
        dt_bias=d_dtb[:, :SSM_HEADS], a_log=d_alog[:, :SSM_HEADS],
        d_skip=d_dsk_e.reshape(SSM_HEADS, HEAD_DIM).sum(1)[None, :], ssm_norm_g=dg_ssm, ffn_norm_g=dg_ffn,
        ffn_conv_w=d_ffn_cw.reshape(3, 2 * D_FF), ffn_conv_b=d_ffn_cb.reshape(1, 2 * D_FF), ple_norm_g=dg_ple)
    return loss[0, 0], grad_x, small


_SMALL = (("attn_norm_g", 1, 1024), ("q_norm_g", 1, 64), ("k_norm_g", 1, 64), ("ssm_conv_w", 4, XBC_DIM),
          ("ssm_conv_b", 1, XBC_DIM), ("dt_bias", 1, 16), ("a_log", 1, 16), ("d_skip", 1, 16), ("ssm_norm_g", 1, 1024),
          ("ffn_norm_g", 1, 1024), ("ffn_conv_w", 3, 2 * D_FF), ("ffn_conv_b", 1, 2 * D_FF), ("ple_norm_g", 1, 1024))
_SMALL_ROWS, _SMALL_COLS = 32, XBC_DIM
_SHARDED_SMALL = ("ssm_conv_w", "ffn_conv_w")


def _small_chunks(n):
    return 1 if n <= _SMALL_COLS else 4


def _pack_small(parts, loss):
    rows = []
    for k, r, n in _SMALL:
        c = _small_chunks(n)
        rows.append(jnp.pad(parts[k].reshape(r * c, n // c), ((0, 0), (0, _SMALL_COLS - n // c))))
    packed = _pad_rows(jnp.concatenate(rows, axis=0), _SMALL_ROWS)
    at_loss = ((lax.broadcasted_iota(jnp.int32, packed.shape, 0) == _SMALL_ROWS - 1) &
               (lax.broadcasted_iota(jnp.int32, packed.shape, 1) == 0))
    return jnp.where(at_loss, loss, packed)


def _adam_update(w, g, m, v):
    c1 = 1.0 - ADAM_B1 ** ADAM_STEP
    c2 = 1.0 - ADAM_B2 ** ADAM_STEP
    mn = ADAM_B1 * m + (1.0 - ADAM_B1) * g
    vn = ADAM_B2 * v + (1.0 - ADAM_B2) * (g * g)
    return -ADAM_LR * ((mn / c1) / (jnp.sqrt(vn / c2) + ADAM_EPS) + ADAM_WD * w), mn, vn


def _adamw_small(g_all, g_shard, w, m, v):
    ins, shapes = [g_all], []
    for k, _, _ in _SMALL:
        shape2 = w[k].shape if w[k].ndim == 2 else (1, w[k].shape[0])
        shapes.append(shape2)
        ins += ([g_shard[k]] if k in _SHARDED_SMALL else []) + [a.reshape(shape2) for a in (w[k], m[k], v[k])]

    def body(*refs):
        g_ref, pos, row = refs[0], 1, 0
        outs = refs[len(ins):]
        for i, (k, r, n) in enumerate(_SMALL):
            c = _small_chunks(n)
            if k in _SHARDED_SMALL:
                g = refs[pos][...]
                pos += 1
            elif c == 1:
                g = g_ref[row:row + r, 0:n]
            else:
                g = jnp.concatenate([g_ref[row + j:row + j + 1, 0:n // c] for j in range(c)], axis=1)
            row += r * c
            d, mn, vn = _adam_update(refs[pos][...], g, refs[pos + 1][...], refs[pos + 2][...])
            pos += 3
            for o_ref, val in zip(outs[4 * i:4 * i + 4], (g, d, mn, vn)):
                o_ref[...] = val

    res = pl.pallas_call(body, name="adamw_small",
                         out_shape=[SDS(s, F32) for s in shapes for _ in range(4)])(*ins)
    return {k: tuple(a.reshape(w[k].shape) for a in res[4 * i:4 * i + 4]) for i, (k, _, _) in enumerate(_SMALL)}


def kernel(x, p, attn_norm_g, w_in, q_norm_g, k_norm_g, ssm_conv_w, ssm_conv_b, dt_bias, a_log, d_skip, ssm_norm_g, w_out, ffn_norm_g, w_up, ffn_conv_w, ffn_conv_b, w_down, ple_norm_g, w_ple_gate, w_ple_proj, loss_target, m_attn_norm_g, m_w_in, m_q_norm_g, m_k_norm_g, m_ssm_conv_w, m_ssm_conv_b, m_dt_bias, m_a_log, m_d_skip, m_ssm_norm_g, m_w_out, m_ffn_norm_g, m_w_up, m_ffn_conv_w, m_ffn_conv_b, m_w_down, m_ple_norm_g, m_w_ple_gate, m_w_ple_proj, v_attn_norm_g, v_w_in, v_q_norm_g, v_k_norm_g, v_ssm_conv_w, v_ssm_conv_b, v_dt_bias, v_a_log, v_d_skip, v_ssm_norm_g, v_w_out, v_ffn_norm_g, v_w_up, v_ffn_conv_w, v_ffn_conv_b, v_w_down, v_ple_norm_g, v_w_ple_gate, v_w_ple_proj):
    names = ("attn_norm_g", "w_in", "q_norm_g", "k_norm_g", "ssm_conv_w", "ssm_conv_b", "dt_bias", "a_log", "d_skip",
             "ssm_norm_g", "w_out", "ffn_norm_g", "w_up", "ffn_conv_w", "ffn_conv_b", "w_down", "ple_norm_g",
             "w_ple_gate", "w_ple_proj")
    w = dict(zip(names, (attn_norm_g, w_in, q_norm_g, k_norm_g, ssm_conv_w, ssm_conv_b, dt_bias, a_log, d_skip,
                         ssm_norm_g, w_out, ffn_norm_g, w_up, ffn_conv_w, ffn_conv_b, w_down, ple_norm_g, w_ple_gate,
                         w_ple_proj)))
    m = dict(zip(names, (m_attn_norm_g, m_w_in, m_q_norm_g, m_k_norm_g, m_ssm_conv_w, m_ssm_conv_b, m_dt_bias,
                         m_a_log, m_d_skip, m_ssm_norm_g, m_w_out, m_ffn_norm_g, m_w_up, m_ffn_conv_w, m_ffn_conv_b,
                         m_w_down, m_ple_norm_g, m_w_ple_gate, m_w_ple_proj)))
    v = dict(zip(names, (v_attn_norm_g, v_w_in, v_q_norm_g, v_k_norm_g, v_ssm_conv_w, v_ssm_conv_b, v_dt_bias,
                         v_a_log, v_d_skip, v_ssm_norm_g, v_w_out, v_ffn_norm_g, v_w_up, v_ffn_conv_w, v_ffn_conv_b,
                         v_w_down, v_ple_norm_g, v_w_ple_gate, v_w_ple_proj)))
    w, m, v = ({k: a[0] for k, a in d.items()} for d in (w, m, v))
    me = 4 * lax.axis_index("x") + 2 * lax.axis_index("y") + lax.axis_index("c")

    mine = dict(in_t=w["w_in"].T, out=w["w_out"], up_t=w["w_up"].T, down=w["w_down"], gate=w["w_ple_gate"],
                proj_t=w["w_ple_proj"].T)
    mine = {k: a.astype(BF16) for k, a in mine.items()}
    conv_pack = jnp.pad(jnp.concatenate([w["ssm_conv_w"].reshape(-1), w["ffn_conv_w"].reshape(-1)]),
                        (0, 3072 - 2880)).reshape(8, 384)
    all_in, all_conv = _all_gather([mine["in_t"], conv_pack], "gather_first")
    later = ("out", "up_t", "down", "gate", "proj_t")
    zones = [lax.dynamic_update_slice(lax.empty((N_DEV,) + mine[k].shape, BF16), mine[k][None], (me, 0, 0))
             for k in later]
    zones, all_in, all_conv = lax.optimization_barrier((zones, all_in, all_conv))
    rest_state = _exchange_start([mine[k] for k in later], zones, [False] * len(later), "gather_rest_start")

    def fetch_rest(after):
        _, landed = _exchange_wait(*rest_state[:4], [False] * len(later), after, "gather_rest_wait")
        return {k: a.reshape(N_DEV * a.shape[1], a.shape[2]) for k, a in zip(later, landed)}

    wts = dict(in_t=_pad_rows(all_in.reshape(IN_PROJ, D_MODEL), IN_PROJ_PAD))
    conv_flat = all_conv.reshape(N_DEV, 3072)
    wts["ssm_cw"] = conv_flat[:, :768].reshape(N_DEV, 4, XBC_DIM // N_DEV).transpose(1, 0, 2).reshape(4, XBC_DIM)
    wts["ffn_cw"] = conv_flat[:, 768:2880].reshape(N_DEV, 3, 2 * D_FF // N_DEV).transpose(1, 0, 2).reshape(3, 2 * D_FF)
    sm = {k: w[k].reshape(1, -1) for k, _, _ in _SMALL if k not in _SHARDED_SMALL}

    in_flight = []

    def send(grads):
        keys = sorted(grads)
        srcs = [grads[k].reshape(N_DEV, grads[k].shape[0] // N_DEV, grads[k].shape[1]) for k in keys]
        state = _exchange_start(srcs, [lax.empty(a.shape, BF16) for a in srcs], [True] * len(keys),
                                "send_" + "_".join(keys))
        in_flight.append((keys, state))
        return state[4][0:1, 0:1]

    loss, grad_x, small = _local_step(x[0], p[0, 0], loss_target[0], sm, wts, fetch_rest, send,
                                      rest_state[4][0:1, 0:1])

    (got_small,) = _all_gather([_pack_small(small, loss)], "gather_small_grads")
    g_small = _reduce8(got_small, _SMALL_ROWS, "reduce_small")
    loss = g_small[_SMALL_ROWS - 1, 0]
    grads, gw, delta, new_m, new_v = {}, {}, {}, {}, {}
    row_sharded = {"out": "w_out", "down": "w_down", "gate": "w_ple_gate"}
    for keys, state in in_flight:
        sent, landed = _exchange_wait(*state[:4], [True] * len(keys), grad_x, "wait_" + "_".join(keys))
        for k, shares, land in zip(keys, sent, landed):
            own = lax.dynamic_index_in_dim(shares, me, 0, keepdims=False)
            if k in row_sharded:
                n = row_sharded[k]
                gw[n], delta[n], new_m[n], new_v[n] = _reduce_adamw(own, land, w[n], m[n], v[n], "update_" + n)
            else:
                grads[k] = _reduce_landed(own, land, "reduce_" + k)
    gw.update({"w_in": grads["in_t"].T, "w_up": grads["up_t"].T, "w_ple_proj": grads["proj_t"].T})
    n_ssm, n_ffn = XBC_DIM // N_DEV, 2 * D_FF // N_DEV
    g_shard = {"ssm_conv_w": lax.dynamic_slice(g_small, (3, me * n_ssm), (4, n_ssm)),
               "ffn_conv_w": lax.dynamic_slice(g_small[13:25, :2 * D_FF // 4].reshape(3, 2 * D_FF), (0, me * n_ffn),
                                               (3, n_ffn))}

    for k, tr in (("w_in", 256), ("w_up", 256), ("w_ple_proj", None)):
        delta[k], new_m[k], new_v[k] = _adamw(w[k], gw[k], m[k], v[k], "adamw_" + k, tr)
    for k, (g_k, d_k, m_k, v_k) in _adamw_small(g_small, g_shard, w, m, v).items():
        gw[k], delta[k], new_m[k], new_v[k] = g_k, d_k, m_k, v_k

    outs = [loss, grad_x[None]]
    for d in (gw, delta, new_m, new_v):
        outs += [d[k][None] for k in names]
    return tuple(outs)
```

```python
import numpy as np
import jax
import jax.numpy as jnp
from jax import lax
from jax.experimental import pallas as pl
from jax.experimental.pallas import tpu as pltpu

F32 = jnp.float32
BF16 = jnp.bfloat16
SDS = jax.ShapeDtypeStruct
EPS = 1e-6
N_DEV = 8
D_MODEL = 1024
HEAD_DIM = 64
ATTN_DIM = 512
KV_DIM = 256
SSM_INNER = 1024
SSM_HEADS = 16
BC_DIM = 256
XBC_DIM = SSM_INNER + 2 * BC_DIM
MIX_DIM = ATTN_DIM + SSM_INNER
IN_PROJ = 3600
IN_PROJ_PAD = 3840
D_FF = 2816
PLE_DIM = 256
CHUNK = 128
SUPER = 2048
DILATIONS = (1, 4, 16)
VMEM_LIMIT = 56 * 1024 * 1024
ADAM_LR, ADAM_B1, ADAM_B2, ADAM_EPS, ADAM_WD, ADAM_STEP = 0.001, 0.9, 0.999, 1e-08, 0.01, 10

NT = (((1,), (1,)), ((), ()))
TN = (((0,), (0,)), ((), ()))


def _params(*sem):
    return pltpu.CompilerParams(dimension_semantics=sem if sem else None, vmem_limit_bytes=VMEM_LIMIT)


def _dot(a, b, dims=None):
    if dims is None:
        return jnp.dot(a, b, preferred_element_type=F32)
    return lax.dot_general(a, b, dims, preferred_element_type=F32)


def _hdot(a, b, parts=2):
    a_exact = a.dtype == BF16
    x = b if a_exact else a
    acc = None
    for _ in range(parts):
        piece = x.astype(BF16)
        x = x - piece.astype(F32)
        d = _dot(a, piece) if a_exact else _dot(piece, b)
        acc = d if acc is None else acc + d
    return acc


def _sigmoid(x):
    return 0.5 * jnp.tanh(0.5 * x) + 0.5


def _shift_down(x, halo8, s):
    xr = pltpu.roll(x, s, 0)
    row = lax.broadcasted_iota(jnp.int32, halo8.shape, 0)
    first = jnp.where(row < s, pltpu.roll(halo8, s, 0), xr[0:8])
    return jnp.concatenate([first, xr[8:]], axis=0)


def _shift_up(x, halo8, s):
    n = x.shape[0]
    xr = pltpu.roll(x, n - s, 0)
    row = lax.broadcasted_iota(jnp.int32, halo8.shape, 0)
    last = jnp.where(row >= 8 - s, pltpu.roll(halo8, 8 - s, 0), xr[n - 8:])
    return jnp.concatenate([xr[:n - 8], last], axis=0)


def _norm_matmul(x, g, wt, tm, tn, name):
    t, k = x.shape
    n = wt.shape[0]

    def body(x_ref, g_ref, w_ref, o_ref, h_ref):
        @pl.when(pl.program_id(1) == 0)
        def _():
            xv = x_ref[...]
            r = lax.rsqrt(jnp.mean(xv * xv, axis=-1, keepdims=True) + EPS)
            h_ref[...] = (xv * r * g_ref[...]).astype(BF16)
        o_ref[...] = _dot(h_ref[...], w_ref[...], NT)

    return pl.pallas_call(
        body, name=name, grid=(t // tm, n // tn),
        in_specs=[pl.BlockSpec((tm, k), lambda i, j: (i, 0)), pl.BlockSpec((1, k), lambda i, j: (0, 0)),
                  pl.BlockSpec((tn, k), lambda i, j: (j, 0))],
        out_specs=[pl.BlockSpec((tm, tn), lambda i, j: (i, j)), pl.BlockSpec((tm, k), lambda i, j: (i, 0))],
        out_shape=[SDS((t, n), F32), SDS((t, k), BF16)],
        compiler_params=_params("parallel", "arbitrary"))(x, g, wt)


def _a_spec(a, lead, tm):
    if lead is None:
        return pl.BlockSpec((tm, a.shape[-1]), lambda i: (i, 0))
    return pl.BlockSpec((None, tm, a.shape[-1]), lambda i, _l=lead: (_l, i, 0))


def _mm_resid_norm(pairs, res, g, tm, name):
    t = pairs[0][0].shape[-2]
    n = pairs[0][2].shape[1]
    np_ = len(pairs)

    def body(*refs):
        res_ref, g_ref, o_ref, h_ref = refs[2 * np_:]
        acc = res_ref[...]
        for q in range(np_):
            acc = acc + _dot(refs[q][...].astype(BF16), refs[np_ + q][...])
        o_ref[...] = acc
        r = lax.rsqrt(jnp.mean(acc * acc, axis=-1, keepdims=True) + EPS)
        h_ref[...] = (acc * r * g_ref[...]).astype(BF16)

    in_specs = [_a_spec(a, lead, tm) for a, lead, _ in pairs]
    in_specs += [pl.BlockSpec(b.shape, lambda i: (0, 0)) for _, _, b in pairs]
    in_specs += [pl.BlockSpec((tm, n), lambda i: (i, 0)), pl.BlockSpec((1, n), lambda i: (0, 0))]
    args = [a for a, _, _ in pairs] + [b for _, _, b in pairs] + [res, g]
    row = pl.BlockSpec((tm, n), lambda i: (i, 0))
    return pl.pallas_call(
        body, name=name, grid=(t // tm,), in_specs=in_specs, out_specs=[row, row],
        out_shape=[SDS((t, n), F32), SDS((t, n), BF16)], compiler_params=_params("parallel"))(*args)


def _wgrad_multi(parts, b, name, tk=2048):
    t, n = b.shape
    widths = [a.shape[1] for a in parts]
    m = sum(widths)

    def body(*refs):
        b_ref, o_ref, acc = refs[len(parts):]

        @pl.when(pl.program_id(0) == 0)
        def _():
            acc[...] = jnp.zeros_like(acc)
        bv = b_ref[...].astype(BF16)
        row = 0
        for a_ref, w in zip(refs, widths):
            acc[row:row + w, :] += _dot(a_ref[...].astype(BF16), bv, TN)
            row += w

        @pl.when(pl.program_id(0) == pl.num_programs(0) - 1)
        def _():
            o_ref[...] = acc[...].astype(BF16)

    return pl.pallas_call(
        body, name=name, grid=(t // tk,),
        in_specs=[pl.BlockSpec((tk, w), lambda k: (k, 0)) for w in widths] + [pl.BlockSpec((tk, n), lambda k: (k, 0))],
        out_specs=pl.BlockSpec((m, n), lambda k: (0, 0)), out_shape=SDS((m, n), BF16),
        scratch_shapes=[pltpu.VMEM((m, n), F32)],
        compiler_params=_params("arbitrary"))(*parts, b)


def _out_proj_bwd(dx, w_out, tm=1024):
    t = dx.shape[0]

    def body(dx_ref, w_ref, da_ref, ds_ref):
        a = dx_ref[...].astype(BF16)
        da_ref[...] = _dot(a, w_ref[0:ATTN_DIM, :], NT)
        ds_ref[...] = _dot(a, w_ref[ATTN_DIM:, :], NT)

    return pl.pallas_call(
        body, name="out_proj_bwd", grid=(t // tm,),
        in_specs=[pl.BlockSpec((tm, D_MODEL), lambda i: (i, 0)), pl.BlockSpec(w_out.shape, lambda i: (0, 0))],
        out_specs=[pl.BlockSpec((tm, ATTN_DIM), lambda i: (i, 0)), pl.BlockSpec((tm, SSM_INNER), lambda i: (i, 0))],
        out_shape=[SDS((t, ATTN_DIM), F32), SDS((t, SSM_INNER), F32)],
        compiler_params=_params("parallel"))(dx, w_out)


def _mm_normbwd(pairs, x, g, dres, tm, name):
    t, k = x.shape
    np_ = len(pairs)
    b_specs = [pl.BlockSpec((rows, b.shape[1]), lambda i, _b=blk: (_b, 0)) for _, _, b, rows, blk in pairs]
    pairs = [(a, lead, b) for a, lead, b, _, _ in pairs]

    def body(*refs):
        x_ref, g_ref, dres_ref, dx_ref, dg_ref = refs[2 * np_:]
        dh = None
        for q in range(np_):
            d = _dot(refs[q][...], refs[np_ + q][...])
            dh = d if dh is None else dh + d
        xv = x_ref[...]
        r = lax.rsqrt(jnp.mean(xv * xv, axis=-1, keepdims=True) + EPS)
        xh = xv * r

        @pl.when(pl.program_id(0) == 0)
        def _():
            dg_ref[...] = jnp.zeros_like(dg_ref)
        dg_ref[...] += jnp.sum(dh * xh, axis=0, keepdims=True)
        gd = dh * g_ref[...]
        dx_ref[...] = dres_ref[...] + r * (gd - xh * jnp.mean(gd * xh, axis=-1, keepdims=True))

    in_specs = [_a_spec(a, lead, tm) for a, lead, _ in pairs] + b_specs
    in_specs += [pl.BlockSpec((tm, k), lambda i: (i, 0)), pl.BlockSpec((1, k), lambda i: (0, 0)),
                 pl.BlockSpec((tm, k), lambda i: (i, 0))]
    args = [a for a, _, _ in pairs] + [b for _, _, b in pairs] + [x, g, dres]
    return pl.pallas_call(
        body, name=name, grid=(t // tm,), in_specs=in_specs,
        out_specs=[pl.BlockSpec((tm, k), lambda i: (i, 0)), pl.BlockSpec((1, k), lambda i: (0, 0))],
        out_shape=[SDS((t, k), F32), SDS((1, k), F32)],
        compiler_params=_params("arbitrary"))(*args)


def _wgrad(a, a_lead, b, name, tk=2048):
    t, m = a.shape[-2:]
    n = b.shape[1]
    tm = m if m <= 1024 else 1408
    assert m % tm == 0

    def body(a_ref, b_ref, o_ref, acc):
        @pl.when(pl.program_id(1) == 0)
        def _():
            acc[...] = jnp.zeros_like(acc)
        acc[...] += _dot(a_ref[...].astype(BF16), b_ref[...].astype(BF16), TN)

        @pl.when(pl.program_id(1) == pl.num_programs(1) - 1)
        def _():
            o_ref[...] = acc[...].astype(BF16)

    per, lead = m // tm, 1
    if a_lead == "all":
        lead = a.shape[0]
        a_spec = pl.BlockSpec((None, tk, tm), lambda mi, ki: (mi // per, ki, mi % per))
    elif a_lead is None:
        a_spec = pl.BlockSpec((tk, tm), lambda mi, ki: (ki, mi))
    else:
        a_spec = pl.BlockSpec((None, tk, tm), lambda mi, ki, _l=a_lead: (_l, ki, mi))
    return pl.pallas_call(
        body, name=name, grid=(lead * per, t // tk),
        in_specs=[a_spec, pl.BlockSpec((tk, n), lambda mi, ki: (ki, 0))],
        out_specs=pl.BlockSpec((tm, n), lambda mi, ki: (mi, 0)), out_shape=SDS((lead * m, n), BF16),
        scratch_shapes=[pltpu.VMEM((tm, n), F32)],
        compiler_params=_params("parallel", "arbitrary"))(a, b)


def _head_consts():
    iq = np.arange(ATTN_DIM)
    ik = np.arange(KV_DIM)
    ones_q = (iq[:, None] // HEAD_DIM == iq[None, :] // HEAD_DIM).astype(np.float32)
    ones_k = (ik[:, None] // HEAD_DIM == ik[None, :] // HEAD_DIM).astype(np.float32)
    dup = (ik[:, None] == (HEAD_DIM * (iq[None, :] // 128) + iq[None, :] % HEAD_DIM)).astype(np.float32)
    return jnp.asarray(ones_q, BF16), jnp.asarray(ones_k, BF16), jnp.asarray(dup, BF16), jnp.asarray(dup.T, BF16)


def _attn_prep(proj, gq, gk, ones_q, ones_k, dup, tm=1024):
    t = proj.shape[0]

    def body(p_ref, gq_ref, gk_ref, oq_ref, ok_ref, dup_ref, qn_ref, kd_ref, vd_ref):
        q = p_ref[:, 0:ATTN_DIM]
        k = p_ref[:, ATTN_DIM:ATTN_DIM + KV_DIM]
        v = p_ref[:, ATTN_DIM + KV_DIM:]
        rq = lax.rsqrt(_hdot(q * q, oq_ref[...]) * (1.0 / HEAD_DIM) + EPS)
        qn_ref[...] = (q * rq * gq_ref[...]) * (HEAD_DIM ** -0.5)
        rk = lax.rsqrt(_hdot(k * k, ok_ref[...]) * (1.0 / HEAD_DIM) + EPS)
        kn = k * rk * gk_ref[...]
        kd_ref[...] = _dot(kn.astype(BF16), dup_ref[...])
        vd_ref[...] = _dot(v.astype(BF16), dup_ref[...])

    full = lambda a: pl.BlockSpec(a.shape, lambda i: (0, 0))
    o_spec = pl.BlockSpec((tm, ATTN_DIM), lambda i: (i, 0))
    return pl.pallas_call(
        body, name="attn_prep", grid=(t // tm,),
        in_specs=[pl.BlockSpec((tm, 1024), lambda i: (i, 0)), full(gq), full(gk), full(ones_q), full(ones_k), full(dup)],
        out_specs=[o_spec, o_spec, o_spec], out_shape=[SDS((t, ATTN_DIM), F32)] * 3,
        compiler_params=_params("parallel"))(proj, gq, gk, ones_q, ones_k, dup)


def _attn_prep_bwd(proj, dqn, dkc, dkp, dvc, dvp, gq, gk, ones_q, ones_k, dup_t, tm=1024):
    t = proj.shape[0]
    nblk = t // tm
    off = SUPER // tm

    def body(p_ref, dqn_ref, dkc_ref, dkp_ref, dvc_ref, dvp_ref, gq_ref, gk_ref, oq_ref, ok_ref, dt_ref,
             o_ref, dgq_ref, dgk_ref):
        i = pl.program_id(0)
        has_next = (i + off < nblk).astype(F32)
        q = p_ref[:, 0:ATTN_DIM]
        k = p_ref[:, ATTN_DIM:ATTN_DIM + KV_DIM]
        dkn = _hdot(dkc_ref[...] + has_next * dkp_ref[...], dt_ref[...])
        dv = _hdot(dvc_ref[...] + has_next * dvp_ref[...], dt_ref[...])

        @pl.when(i == 0)
        def _():
            dgq_ref[...] = jnp.zeros_like(dgq_ref)
            dgk_ref[...] = jnp.zeros_like(dgk_ref)

        rq = lax.rsqrt(_hdot(q * q, oq_ref[...]) * (1.0 / HEAD_DIM) + EPS)
        xh = q * rq
        dy = dqn_ref[...] * (HEAD_DIM ** -0.5)
        dgq_ref[...] += jnp.sum(dy * xh, axis=0, keepdims=True)
        gd = dy * gq_ref[...]
        dq = rq * (gd - xh * (_hdot(gd * xh, oq_ref[...]) * (1.0 / HEAD_DIM)))
        rk = lax.rsqrt(_hdot(k * k, ok_ref[...]) * (1.0 / HEAD_DIM) + EPS)
        kh = k * rk
        dgk_ref[...] += jnp.sum(dkn * kh, axis=0, keepdims=True)
        gdk = dkn * gk_ref[...]
        dk = rk * (gdk - kh * (_hdot(gdk * kh, ok_ref[...]) * (1.0 / HEAD_DIM)))
        o_ref[:, 0:ATTN_DIM] = dq.astype(BF16)
        o_ref[:, ATTN_DIM:ATTN_DIM + KV_DIM] = dk.astype(BF16)
        o_ref[:, ATTN_DIM + KV_DIM:] = dv.astype(BF16)

    full = lambda a: pl.BlockSpec(a.shape, lambda i: (0, 0))
    cur = pl.BlockSpec((tm, ATTN_DIM), lambda i: (i, 0))
    nxt = pl.BlockSpec((tm, ATTN_DIM), lambda i: (jnp.minimum(i + off, nblk - 1), 0))
    return pl.pallas_call(
        body, name="attn_prep_bwd", grid=(nblk,),
        in_specs=[pl.BlockSpec((tm, 1024), lambda i: (i, 0)), cur, cur, nxt, cur, nxt,
                  full(gq), full(gk), full(ones_q), full(ones_k), full(dup_t)],
        out_specs=[pl.BlockSpec((tm, 1024), lambda i: (i, 0)), pl.BlockSpec((1, ATTN_DIM), lambda i: (0, 0)),
                   pl.BlockSpec((1, KV_DIM), lambda i: (0, 0))],
        out_shape=[SDS((t, 1024), BF16), SDS((1, ATTN_DIM), F32), SDS((1, KV_DIM), F32)],
        compiler_params=_params("arbitrary"))(proj, dqn, dkc, dkp, dvc, dvp, gq, gk, ones_q, ones_k, dup_t)


def _tile_masks():
    qi = lax.broadcasted_iota(jnp.int32, (2 * CHUNK, 2 * CHUNK), 0) & (CHUNK - 1)
    kj = lax.broadcasted_iota(jnp.int32, (2 * CHUNK, 2 * CHUNK), 1)
    delta = CHUNK + qi - kj
    band = (delta >= 0) & (delta <= CHUNK)
    return band, kj


def _deinterleave(dst, src, n_rows, d):
    per = n_rows // d
    for r in range(d):
        dst[r * per:(r + 1) * per, :] = src[pl.ds(r, per, stride=d), :]


def _attn_specs(t):
    blk = lambda f: pl.BlockSpec((SUPER, 128), f)
    cur = blk(lambda h, s: (s, h))
    prev = blk(lambda h, s: (jnp.maximum(s - 1, 0), h))
    return cur, prev


def _attn_fwd(qn, kd, vd):
    t = qn.shape[0]
    cur, prev = _attn_specs(t)

    def body(q_ref, kp_ref, kc_ref, vp_ref, vc_ref, o_ref, lse_ref, kk, vv, qd, kdd, vdd, po, pm, pll, acc, mm, ll):
        s = pl.program_id(1)
        kk[0:SUPER, :] = kp_ref[...]
        kk[SUPER:, :] = kc_ref[...]
        vv[0:SUPER, :] = vp_ref[...]
        vv[SUPER:, :] = vc_ref[...]
        m0 = lax.broadcasted_iota(jnp.int32, (CHUNK, 128), 1) < HEAD_DIM
        band, kj = _tile_masks()
        for d in DILATIONS:
            lq = SUPER // d
            if d == 1:
                qs_ref, ks_ref, vs_ref = q_ref, kk, vv
            else:
                _deinterleave(qd, q_ref, SUPER, d)
                _deinterleave(kdd, kk, 2 * SUPER, d)
                _deinterleave(vdd, vv, 2 * SUPER, d)
                qs_ref, ks_ref, vs_ref = qd, kdd, vdd

            nblk = lq // CHUNK

            def key_rows(ti):
                return pl.ds((ti // nblk) * 2 * lq + lq + (ti % nblk - 1) * CHUNK, 2 * CHUNK)

            def scores(ti):
                qt = qs_ref[pl.ds(ti * CHUNK, CHUNK), :]
                qs = jnp.concatenate([jnp.where(m0, qt, 0.0), jnp.where(m0, 0.0, qt)], axis=0).astype(BF16)
                return _dot(qs, ks_ref[key_rows(ti), :].astype(BF16), NT)

            def softmax_pv(ti, sc):
                ok = band if ti % nblk > 0 else band & (kj >= jnp.where(s > 0, 0, CHUNK))
                sc = jnp.where(ok, sc, -jnp.inf)
                mt = jnp.max(sc, axis=-1, keepdims=True)
                p = jnp.exp(sc - mt)
                lt = jnp.sum(p, axis=-1, keepdims=True)
                ot = _dot(p.astype(BF16), vs_ref[key_rows(ti), :].astype(BF16))
                qrows = pl.ds(ti * CHUNK, CHUNK)
                po[qrows, :] = jnp.where(m0, ot[:CHUNK], ot[CHUNK:])
                pm[qrows, :] = jnp.where(m0, mt[:CHUNK], mt[CHUNK:])
                pll[qrows, :] = jnp.where(m0, lt[:CHUNK], lt[CHUNK:])

            for ti in range(SUPER // CHUNK):
                softmax_pv(ti, scores(ti))
            if d == 1:
                acc[...] = po[...]
                mm[...] = pm[...]
                ll[...] = pll[...]
            else:
                for r in range(d):
                    rows = pl.ds(r, lq, stride=d)
                    seg = slice(r * lq, (r + 1) * lq)
                    m_old, m_new = mm[rows, :], pm[seg, :]
                    m_all = jnp.maximum(m_old, m_new)
                    a, b = jnp.exp(m_old - m_all), jnp.exp(m_new - m_all)
                    acc[rows, :] = acc[rows, :] * a + po[seg, :] * b
                    ll[rows, :] = ll[rows, :] * a + pll[seg, :] * b
                    mm[rows, :] = m_all
        o_ref[...] = acc[...] / ll[...]
        lse_ref[...] = mm[...] + jnp.log(ll[...])

    big = pltpu.VMEM((2 * SUPER, 128), F32)
    one = pltpu.VMEM((SUPER, 128), F32)
    return pl.pallas_call(
        body, name="attn_fwd", grid=(4, t // SUPER),
        in_specs=[cur, prev, cur, prev, cur], out_specs=[cur, cur],
        out_shape=[SDS((t, ATTN_DIM), F32)] * 2,
        scratch_shapes=[big, big, one, big, big, one, one, one, one, one, one],
        compiler_params=_params("parallel", "arbitrary"))(qn, kd, kd, vd, vd)


def _attn_bwd(qn, kd, vd, out, lse, dout, ones_pair):
    t = qn.shape[0]
    cur, prev = _attn_specs(t)

    def body(q_ref, kp_ref, kc_ref, vp_ref, vc_ref, o_ref, lse_ref, do_ref, ones_ref,
             dq_ref, dkc_ref, dkp_ref, dvc_ref, dvp_ref,
             kk, vv, od, ld, kb, vb, qsb, dosb, tk, tv, pdq, delta):
        s = pl.program_id(1)
        delta[...] = _hdot(do_ref[...] * o_ref[...], ones_ref[...])

        def per_row(a):
            ar = pltpu.roll(a, HEAD_DIM, 1)
            rows = jnp.concatenate([jnp.where(m0, a, ar), jnp.where(m0, ar, a)], axis=0)
            return jnp.concatenate([rows, rows], axis=1)

        kk[0:SUPER, :] = kp_ref[...]
        kk[SUPER:, :] = kc_ref[...]
        vv[0:SUPER, :] = vp_ref[...]
        vv[SUPER:, :] = vc_ref[...]
        for ref in (dq_ref, dkc_ref, dkp_ref, dvc_ref, dvp_ref):
            ref[...] = jnp.zeros_like(ref)
        m0 = lax.broadcasted_iota(jnp.int32, (CHUNK, 128), 1) < HEAD_DIM
        band, kj = _tile_masks()
        ninf = -jnp.inf
        for d in DILATIONS:
            lq = SUPER // d
            nblk = lq // CHUNK
            for r in range(d):
                seg = slice(r * 2 * lq, (r + 1) * 2 * lq)
                kb[seg, :] = kk[pl.ds(r, 2 * lq, stride=d), :].astype(BF16)
                vb[seg, :] = vv[pl.ds(r, 2 * lq, stride=d), :].astype(BF16)
            for ti in range(SUPER // CHUNK):
                rows = pl.ds(ti // nblk + d * CHUNK * (ti % nblk), CHUNK, stride=d)
                for src, dst in ((q_ref, qsb), (do_ref, dosb)):
                    a = src[rows, :]
                    dst[ti * 2 * CHUNK:(ti + 1) * 2 * CHUNK, :] = jnp.concatenate(
                        [jnp.where(m0, a, 0.0), jnp.where(m0, 0.0, a)], axis=0).astype(BF16)
                ld[ti * CHUNK:(ti + 1) * CHUNK, :] = lse_ref[rows, :]
                od[ti * CHUNK:(ti + 1) * CHUNK, :] = delta[rows, :]

            def operands(ti):
                r, nb = ti // nblk, ti % nblk
                stacked = slice(ti * 2 * CHUNK, (ti + 1) * 2 * CHUNK)
                krows = pl.ds(r * 2 * lq + lq + (nb - 1) * CHUNK, 2 * CHUNK)
                return stacked, krows

            def scores(ti):
                stacked, krows = operands(ti)
                kt = kb[krows, :]
                return dict(ti=ti, sc=_dot(qsb[stacked, :], kt, NT), dp=_dot(dosb[stacked, :], vb[krows, :], NT))

            def softmax_grad(c):
                qrows = slice(c["ti"] * CHUNK, (c["ti"] + 1) * CHUNK)
                ok = band if c["ti"] % nblk > 0 else band & (kj >= jnp.where(s > 0, 0, CHUNK))
                p = jnp.exp(jnp.where(ok, c.pop("sc"), ninf) - per_row(ld[qrows, :]))
                ds = p * (c.pop("dp") - per_row(od[qrows, :]))
                c.update(p=p.astype(BF16), ds=ds.astype(BF16))
                return c

            def grads(c):
                ti = c["ti"]
                stacked, krows = operands(ti)
                dqs = _dot(c["ds"], kb[krows, :])
                pdq[ti * CHUNK:(ti + 1) * CHUNK, :] = jnp.where(m0, dqs[:CHUNK], dqs[CHUNK:])
                tk[stacked, :] = _dot(c["ds"], qsb[stacked, :], TN)
                tv[stacked, :] = _dot(c["p"], dosb[stacked, :], TN)

            n_tiles = SUPER // CHUNK
            stage_a = scores(0)
            for ti in range(n_tiles):
                ahead = scores(ti + 1) if ti + 1 < n_tiles else None
                grads(softmax_grad(stage_a))
                stage_a = ahead

            for r in range(d):
                dq_ref[pl.ds(r, lq, stride=d), :] += pdq[r * lq:(r + 1) * lq, :]
                for tile_out, cur_ref, prev_ref in ((tk, dkc_ref, dkp_ref), (tv, dvc_ref, dvp_ref)):
                    first = r * nblk * 2 * CHUNK
                    prev_ref[pl.ds(SUPER - CHUNK * d + r, CHUNK, stride=d), :] += tile_out[first:first + CHUNK, :]
                    for nb in range(nblk):
                        at = (r * nblk + nb) * 2 * CHUNK
                        part = tile_out[at + CHUNK:at + 2 * CHUNK, :]
                        if nb + 1 < nblk:
                            part = part + tile_out[at + 2 * CHUNK:at + 3 * CHUNK, :]
                        cur_ref[pl.ds(r + d * nb * CHUNK, CHUNK, stride=d), :] += part

    big = pltpu.VMEM((2 * SUPER, 128), F32)
    one = pltpu.VMEM((SUPER, 128), F32)
    half = pltpu.VMEM((2 * SUPER, 128), BF16)
    return pl.pallas_call(
        body, name="attn_bwd", grid=(4, t // SUPER),
        in_specs=[cur, prev, cur, prev, cur, cur, cur, cur, pl.BlockSpec((128, 128), lambda h, s: (0, 0))],
        out_specs=[cur] * 5, out_shape=[SDS((t, ATTN_DIM), F32)] * 5,
        scratch_shapes=[big, big, one, one, half, half, half, half, big, big, one, one],
        compiler_params=_params("parallel", "arbitrary"))(qn, kd, kd, vd, vd, out, lse, dout, ones_pair)


def _ssd_consts():
    tri = np.tril(np.ones((CHUNK, CHUNK), np.float32))
    expand = np.zeros((128, SSM_INNER), np.float32)
    for h in range(SSM_HEADS):
        expand[h, h * HEAD_DIM:(h + 1) * HEAD_DIM] = 1.0
    return jnp.asarray(tri, BF16), jnp.asarray(tri.T, BF16), jnp.asarray(expand, BF16), jnp.asarray(expand.T, BF16)


def _conv4(x, halo, w_ref, b_ref):
    acc = b_ref[...] + w_ref[3:4, :] * x
    for k in range(3):
        acc = acc + w_ref[k:k + 1, :] * _shift_down(x, halo, 3 - k)
    return acc


def _softplus(x):
    return jnp.maximum(x, 0.0) + jnp.log(1.0 + jnp.exp(-jnp.abs(x)))


def _ssd_common(pre_x, pre_b, dt_ref, dtb_ref, alog_ref, tri_ref, exp_ref):
    xa = pre_x * _sigmoid(pre_x)
    ba = pre_b * _sigmoid(pre_b)
    dtv = _softplus(dt_ref[...] + dtb_ref[...])
    a_neg = -jnp.exp(alog_ref[...])
    acum = _hdot(tri_ref[...], dtv * a_neg, parts=3)
    lam = jnp.exp(acum)
    gam = jnp.exp(acum[CHUNK - 1:CHUNK, :] - acum)
    dt_e = _hdot(dtv, exp_ref[...])
    lam_e = _hdot(lam, exp_ref[...])
    gam_e = _hdot(gam, exp_ref[...])
    return dict(pre_x=pre_x, pre_b=pre_b, xa=xa, ba=ba, dtv=dtv, a_neg=a_neg, acum=acum,
                dt_e=dt_e, lam_e=lam_e, gam_e=gam_e, xdt=xa * dt_e)


def _decay(acum_t, h, transposed):
    rb = jnp.broadcast_to(acum_t[h:h + 1, :], (CHUNK, CHUNK))
    ri = lax.broadcasted_iota(jnp.int32, (CHUNK, CHUNK), 0)
    ci = lax.broadcasted_iota(jnp.int32, (CHUNK, CHUNK), 1)
    if transposed:
        return jnp.exp(jnp.where(ci >= ri, rb - rb.T, -jnp.inf))
    return jnp.exp(jnp.where(ri >= ci, rb.T - rb, -jnp.inf))


SSD_STEP = 4 * CHUNK


def _ssd_specs(t, rev):
    nc = t // SSD_STEP
    ch = (lambda c: nc - 1 - c) if rev else (lambda c: c)
    col = lambda w, j: pl.BlockSpec((SSD_STEP, w), lambda c: (ch(c), j))
    halo = lambda w, j: pl.BlockSpec((8, w), lambda c: (jnp.maximum(ch(c) * (SSD_STEP // 8) - 1, 0), j))
    return nc, ch, col, halo


def _ssd_fwd(proj, cwx, cbx, cwb, cbb, dtb, alog, dsk_e, norm_g, tri, expand):
    t = proj.shape[0]
    nc, _, col, halo = _ssd_specs(t, False)

    def body(z_all, xs_all, bc_all, dt_all, hx_ref, hb_ref, cwx_ref, cbx_ref, cwb_ref, cbb_ref, dtb_ref, alog_ref,
             dsk_ref, g_ref, tri_ref, exp_ref, y_all, hs_all, o_all, px_all, pb_all, state):
        @pl.when(pl.program_id(0) == 0)
        def _():
            state[...] = jnp.zeros_like(state)

        keep = (pl.program_id(0) > 0).astype(F32)
        for sc in range(SSD_STEP // CHUNK):
            rows = pl.ds(sc * CHUNK, CHUNK)
            before = pl.ds(sc * CHUNK - 8, 8)
            hx = hx_ref[...] * keep if sc == 0 else xs_all[before, :]
            hb = hb_ref[...] * keep if sc == 0 else bc_all[before, :]
            chunk(z_all.at[rows], xs_all.at[rows], bc_all.at[rows], dt_all.at[rows], hx, hb, cwx_ref, cbx_ref, cwb_ref,
                  cbb_ref, dtb_ref, alog_ref, dsk_ref, g_ref, tri_ref, exp_ref, y_all.at[rows],
                  hs_all.at[pl.ds(sc, 1)], o_all.at[rows], px_all.at[rows], pb_all.at[rows], state)

    def chunk(z_ref, xs_ref, bc_ref, dt_ref, hx, hb, cwx_ref, cbx_ref, cwb_ref, cbb_ref, dtb_ref, alog_ref,
              dsk_ref, g_ref, tri_ref, exp_ref, y_ref, hs_ref, o_ref, px_ref, pb_ref, state):
        pre_x = _conv4(xs_ref[...], hx, cwx_ref, cbx_ref)
        pre_b = _conv4(bc_ref[...], hb, cwb_ref, cbb_ref)
        px_ref[...] = pre_x.astype(BF16)
        pb_ref[...] = pre_b.astype(BF16)
        v = _ssd_common(pre_x, pre_b, dt_ref, dtb_ref, alog_ref, tri_ref, exp_ref)
        acum_t = v["acum"].T
        xdt, ba = v["xdt"], v["ba"]
        h_in = state[...]
        hs_ref[0] = h_in
        xg = xdt * v["gam_e"]
        m0 = lax.broadcasted_iota(jnp.int32, (CHUNK, 128), 1) < HEAD_DIM
        for g in range(2):
            bg = ba[:, g * 128:(g + 1) * 128].astype(BF16)
            cg = ba[:, 256 + g * 128:256 + (g + 1) * 128].astype(BF16)
            gl = slice(g * 512, (g + 1) * 512)
            cb = _dot(cg, bg, NT)
            y_off = _dot(cg, h_in[:, gl].astype(BF16)) * v["lam_e"][:, gl]
            s_new = _dot(bg.T, xg[:, gl].astype(BF16))
            state[:, gl] = h_in[:, gl] * v["lam_e"][CHUNK - 1:CHUNK, gl] + s_new
            for j in range(4):
                h0 = 8 * g + 2 * j
                ln = slice(g * 512 + j * 128, g * 512 + (j + 1) * 128)
                xp = xdt[:, ln].astype(BF16)
                y0 = _dot((cb * _decay(acum_t, h0, False)).astype(BF16), xp)
                y1 = _dot((cb * _decay(acum_t, h0 + 1, False)).astype(BF16), xp)
                y_ref[:, ln] = jnp.where(m0, y0, y1) + y_off[:, j * 128:(j + 1) * 128]
        z = z_ref[...]
        yg = (y_ref[...] + dsk_ref[...] * v["xa"]) * (z * _sigmoid(z))
        r = lax.rsqrt(jnp.mean(yg * yg, axis=-1, keepdims=True) + EPS)
        o_ref[...] = (yg * r * g_ref[...]).astype(BF16)

    full = lambda a: pl.BlockSpec(a.shape, lambda c: (0,) * a.ndim)
    return pl.pallas_call(
        body, name="ssd_fwd", grid=(nc,),
        in_specs=[col(1024, 1), col(1024, 2), col(512, 6), col(128, 28), halo(1024, 2), halo(512, 6),
                  full(cwx), full(cbx), full(cwb), full(cbb), full(dtb), full(alog), full(dsk_e), full(norm_g),
                  full(tri), full(expand)],
        out_specs=[pl.BlockSpec((SSD_STEP, SSM_INNER), lambda c: (c, 0)),
                   pl.BlockSpec((SSD_STEP // CHUNK, 128, SSM_INNER), lambda c: (c, 0, 0)),
                   pl.BlockSpec((SSD_STEP, SSM_INNER), lambda c: (c, 0)),
                   pl.BlockSpec((SSD_STEP, SSM_INNER), lambda c: (c, 0)), pl.BlockSpec((SSD_STEP, 512), lambda c: (c, 0))],
        out_shape=[SDS((t, SSM_INNER), F32), SDS((t // CHUNK, 128, SSM_INNER), F32), SDS((t, SSM_INNER), BF16),
                   SDS((t, SSM_INNER), BF16), SDS((t, 512), BF16)],
        scratch_shapes=[pltpu.VMEM((128, SSM_INNER), F32)],
        compiler_params=_params("arbitrary"))(proj, proj, proj, proj, proj, proj, cwx, cbx, cwb, cbb, dtb, alog,
                                              dsk_e, norm_g, tri, expand)


def _ssd_bwd(proj, pre_x, pre_b, y_ssd, hs, dout, cwx, cwb, dtb, alog, dsk_e, norm_g, tri, triu, expand, expand_t):
    t = proj.shape[0]
    nc, ch, col, halo = _ssd_specs(t, True)

    def body(z_all, xs_all, bc_all, dt_all, px_all, pb_all, y_all, hin_all, do_all,
             cwx_ref, cwb_ref, dtb_ref, alog_ref, dsk_ref, g_ref, tri_ref, triu_ref, exp_ref, expt_ref,
             dz_all, dxs_all, dbc_all, ddt_all, dg_ref, ddsk_ref, dalog_ref, ddtb_ref, dcwx_ref, dcbx_ref, dcwb_ref,
             dcbb_ref, gstate, nx_x, nx_b, dact_b, dxdt_s):
        @pl.when(pl.program_id(0) == 0)
        def _():
            gstate[...] = jnp.zeros_like(gstate)
            nx_x[...] = jnp.zeros_like(nx_x)
            nx_b[...] = jnp.zeros_like(nx_b)
            for ref in (dg_ref, ddsk_ref, dalog_ref, ddtb_ref, dcwx_ref, dcbx_ref, dcwb_ref, dcbb_ref):
                ref[...] = jnp.zeros_like(ref)

        for sc in reversed(range(SSD_STEP // CHUNK)):
            rows = pl.ds(sc * CHUNK, CHUNK)
            by_rows = [r.at[rows] for r in (z_all, xs_all, bc_all, dt_all, px_all, pb_all, y_all)]
            outs = [r.at[rows] for r in (dz_all, dxs_all, dbc_all, ddt_all)]
            chunk(*by_rows, hin_all.at[pl.ds(sc, 1)], do_all.at[rows],
                  cwx_ref, cwb_ref, dtb_ref, alog_ref, dsk_ref, g_ref, tri_ref, triu_ref, exp_ref, expt_ref,
                  *outs, dg_ref, ddsk_ref, dalog_ref, ddtb_ref, dcwx_ref, dcbx_ref, dcwb_ref, dcbb_ref,
                  gstate, nx_x, nx_b, dact_b, dxdt_s)

    def chunk(z_ref, xs_ref, bc_ref, dt_ref, px_ref, pb_ref, y_ref, hin_ref, do_ref,
              cwx_ref, cwb_ref, dtb_ref, alog_ref, dsk_ref, g_ref, tri_ref, triu_ref, exp_ref, expt_ref,
              dz_ref, dxs_ref, dbc_ref, ddt_ref, dg_ref, ddsk_ref, dalog_ref, ddtb_ref, dcwx_ref, dcbx_ref, dcwb_ref,
              dcbb_ref, gstate, nx_x, nx_b, dact_b, dxdt_s):
        v = _ssd_common(px_ref[...].astype(F32), pb_ref[...].astype(F32), dt_ref, dtb_ref, alog_ref, tri_ref, exp_ref)
        acum_t = v["acum"].T
        xa, ba, xdt, dtv = v["xa"], v["ba"], v["xdt"], v["dtv"]
        lam_e, gam_e, dt_e = v["lam_e"], v["gam_e"], v["dt_e"]
        z = z_ref[...]
        y = y_ref[...]
        sz = _sigmoid(z)
        zs = z * sz
        y_tot = y + dsk_ref[...] * xa
        yg = y_tot * zs
        r = lax.rsqrt(jnp.mean(yg * yg, axis=-1, keepdims=True) + EPS)
        yh = yg * r
        do = do_ref[...]
        dg_ref[...] += jnp.sum(do * yh, axis=0, keepdims=True)
        gd = do * g_ref[...]
        dyg = r * (gd - yh * jnp.mean(gd * yh, axis=-1, keepdims=True))
        dz_ref[...] = (dyg * y_tot * (sz * (1.0 + z * (1.0 - sz)))).astype(BF16)
        dy = dyg * zs
        ddsk_ref[...] += jnp.sum(dy * xa, axis=0, keepdims=True)
        g_out = gstate[...]
        h_in = hin_ref[0]
        lam_dy = lam_e * dy
        gam_x = gam_e * xdt
        m0 = lax.broadcasted_iota(jnp.int32, (CHUNK, 128), 1) < HEAD_DIM
        lane = lax.broadcasted_iota(jnp.int32, (CHUNK, 128), 1)
        below = (lax.broadcasted_iota(jnp.int32, (CHUNK, CHUNK), 0) >
                 lax.broadcasted_iota(jnp.int32, (CHUNK, CHUNK), 1))
        da_in = jnp.zeros((CHUNK, 128), F32)
        off_y, off_x = [], []
        for g in range(2):
            bg = ba[:, g * 128:(g + 1) * 128].astype(BF16)
            cg = ba[:, 256 + g * 128:256 + (g + 1) * 128].astype(BF16)
            gl = slice(g * 512, (g + 1) * 512)
            gg = g_out[:, gl].astype(BF16)
            cb = _dot(cg, bg, NT)
            dxdt_off = _dot(bg, gg) * gam_e[:, gl]
            off_x.append(xdt[:, gl] * dxdt_off)
            off_y.append(dy[:, gl] * (_dot(cg, h_in[:, gl].astype(BF16)) * lam_e[:, gl]))
            q_sum = jnp.zeros((CHUNK, CHUNK), F32)
            for j in range(4):
                h0 = 8 * g + 2 * j
                ln = slice(g * 512 + j * 128, g * 512 + (j + 1) * 128)
                dyp = dy[:, ln]
                dyb = dyp.astype(BF16)
                xpb = xdt[:, ln].astype(BF16)
                dec = [_decay(acum_t, h0, False), _decay(acum_t, h0 + 1, False)]
                mix = [cb * dec[0], cb * dec[1]]
                d0 = _dot(mix[0].T.astype(BF16), dyb)
                d1 = _dot(mix[1].T.astype(BF16), dyb)
                dxdt_s[:, ln] = jnp.where(m0, d0, d1) + dxdt_off[:, j * 128:(j + 1) * 128]
                for e, (hh, dym) in enumerate(((h0, jnp.where(m0, dyp, 0.0)), (h0 + 1, jnp.where(m0, 0.0, dyp)))):
                    dyx = _dot(dym.astype(BF16), xpb, NT)
                    q_sum = q_sum + dyx * dec[e]
                    reach = jnp.where(below, _hdot(triu_ref[...], dyx * mix[e]), 0.0)
                    da_in = jnp.where(lane == hh, jnp.sum(reach, axis=-1, keepdims=True), da_in)
            gstate[:, gl] = g_out[:, gl] * lam_e[CHUNK - 1:CHUNK, gl] + _dot(cg.T, lam_dy[:, gl].astype(BF16))
            qb = q_sum.astype(BF16)
            dact_b[:, 256 + g * 128:256 + (g + 1) * 128] = (
                _dot(qb, bg) + _dot(lam_dy[:, gl].astype(BF16), h_in[:, gl].astype(BF16), NT))
            dact_b[:, g * 128:(g + 1) * 128] = _dot(qb.T, cg) + _dot(gam_x[:, gl].astype(BF16), gg, NT)
        dxdt = dxdt_s[...]
        seg_y = _hdot(jnp.concatenate(off_y, axis=1), expt_ref[...])
        seg_x = _hdot(jnp.concatenate(off_x, axis=1), expt_ref[...])
        e_col = jnp.sum(g_out * h_in * lam_e[CHUNK - 1:CHUNK, :], axis=0, keepdims=True)
        e_seg = _hdot(jnp.broadcast_to(e_col, (8, SSM_INNER)), expt_ref[...])[0:1, :]
        da = da_in + _hdot(triu_ref[...], seg_y) + (_hdot(tri_ref[...], seg_x) - seg_x) + e_seg
        a_neg = v["a_neg"]
        ddtv = da * a_neg + _hdot(dxdt * xa, expt_ref[...])
        dalog_ref[...] += jnp.sum(da * dtv, axis=0, keepdims=True) * a_neg
        lane16 = lax.broadcasted_iota(jnp.int32, (CHUNK, 128), 1) < SSM_HEADS
        draw = jnp.where(lane16, ddtv * _sigmoid(dt_ref[...] + dtb_ref[...]), 0.0)
        ddtb_ref[...] += jnp.sum(draw, axis=0, keepdims=True)
        ddt_ref[...] = draw.astype(BF16)
        dxa = dxdt * dt_e + dy * dsk_ref[...]
        for (dact, pre, x_ref, nx, cw_ref, dcw_ref, dcb_ref, dx_ref) in (
                (dxa, v["pre_x"], xs_ref, nx_x, cwx_ref, dcwx_ref, dcbx_ref, dxs_ref),
                (dact_b[...], v["pre_b"], bc_ref, nx_b, cwb_ref, dcwb_ref, dcbb_ref, dbc_ref)):
            sp = _sigmoid(pre)
            dpre = dact * (sp * (1.0 + pre * (1.0 - sp)))
            dcb_ref[...] += jnp.sum(dpre, axis=0, keepdims=True)
            xv = x_ref[...]
            nxt = nx[...]
            dx = cw_ref[3:4, :] * dpre
            dcw_ref[3:4, :] += jnp.sum(dpre * xv, axis=0, keepdims=True)
            for k in range(3):
                d_up = _shift_up(dpre, nxt, 3 - k)
                dcw_ref[k:k + 1, :] += jnp.sum(xv * d_up, axis=0, keepdims=True)
                dx = dx + cw_ref[k:k + 1, :] * d_up
            nx[...] = dpre[0:8, :]
            dx_ref[...] = dx.astype(dx_ref.dtype)

    full = lambda a: pl.BlockSpec(a.shape, lambda c: (0,) * a.ndim)
    rowblk = lambda w: pl.BlockSpec((SSD_STEP, w), lambda c: (ch(c), 0))
    acc = lambda a, b: pl.BlockSpec((a, b), lambda c: (0, 0))
    return pl.pallas_call(
        body, name="ssd_bwd", grid=(nc,),
        in_specs=[col(1024, 1), col(1024, 2), col(512, 6), col(128, 28), rowblk(SSM_INNER), rowblk(512),
                  rowblk(SSM_INNER),
                  pl.BlockSpec((SSD_STEP // CHUNK, 128, SSM_INNER), lambda c: (ch(c), 0, 0)),
                  rowblk(SSM_INNER),
                  full(cwx), full(cwb), full(dtb), full(alog), full(dsk_e), full(norm_g),
                  full(tri), full(triu), full(expand), full(expand_t)],
        out_specs=[rowblk(SSM_INNER), rowblk(SSM_INNER), rowblk(512), rowblk(128),
                   acc(1, 1024), acc(1, 1024), acc(1, 128), acc(1, 128), acc(4, 1024), acc(1, 1024), acc(4, 512),
                   acc(1, 512)],
        out_shape=[SDS((t, SSM_INNER), BF16), SDS((t, SSM_INNER), BF16), SDS((t, 512), BF16), SDS((t, 128), BF16),
                   SDS((1, 1024), F32), SDS((1, 1024), F32), SDS((1, 128), F32), SDS((1, 128), F32),
                   SDS((4, 1024), F32), SDS((1, 1024), F32), SDS((4, 512), F32), SDS((1, 512), F32)],
        scratch_shapes=[pltpu.VMEM((128, SSM_INNER), F32), pltpu.VMEM((8, 1024), F32), pltpu.VMEM((8, 512), F32),
                        pltpu.VMEM((CHUNK, 512), F32), pltpu.VMEM((CHUNK, SSM_INNER), F32)],
        compiler_params=_params("arbitrary"))(proj, proj, proj, proj, pre_x, pre_b, y_ssd, hs, dout,
                                              cwx, cwb, dtb, alog, dsk_e, norm_g, tri, triu, expand, expand_t)


def _conv3(x, halo, w_ref, b_ref, part):
    acc = b_ref[part] + w_ref[2, part] * x
    for k in range(2):
        acc = acc + w_ref[k, part] * _shift_down(x, halo, 2 - k)
    return acc


def _up_act(h, w_up_t, cw, cb, tm=2048, tn=256, tr=512):
    t, k = h.shape
    nj = D_FF // tn

    def body(h_ref, wg_ref, wv_ref, w_ref, b_ref, u_ref, c_ref, f_ref, halo):
        i, j = pl.program_id(0), pl.program_id(1)

        @pl.when(i == 0)
        def _():
            halo[j] = jnp.zeros((2, 8, tn), F32)

        def matmuls(r):
            rows = slice(r * tr, (r + 1) * tr)
            return [_dot(h_ref[rows, :], wt_ref[...], NT) for wt_ref in (wg_ref, wv_ref)]

        def epilogue(r, us, before):
            rows = slice(r * tr, (r + 1) * tr)
            parts = []
            for part, u in enumerate(us):
                u_ref[part, rows, :] = u.astype(BF16)
                parts.append(_conv3(u, before[part], w_ref, b_ref, part))
                c_ref[part, rows, :] = parts[-1].astype(BF16)
            gate, val = parts
            f_ref[rows, :] = (gate * _sigmoid(gate) * val).astype(BF16)
            return [u[tr - 8:, :] for u in us]

        before = [halo[j, 0], halo[j, 1]]
        pending = matmuls(0)
        for r in range(tm // tr):
            ahead = matmuls(r + 1) if r + 1 < tm // tr else None
            before = epilogue(r, pending, before)
            pending = ahead
        halo[j, 0], halo[j, 1] = before

    return pl.pallas_call(
        body, name="up_proj", grid=(t // tm, nj),
        in_specs=[pl.BlockSpec((tm, k), lambda i, j: (i, 0)),
                  pl.BlockSpec((tn, k), lambda i, j: (j, 0)), pl.BlockSpec((tn, k), lambda i, j: (j + nj, 0)),
                  pl.BlockSpec((3, 2, 1, tn), lambda i, j: (0, 0, 0, j)), pl.BlockSpec((2, 1, tn), lambda i, j: (0, 0, j))],
        out_specs=[pl.BlockSpec((2, tm, tn), lambda i, j: (0, i, j)), pl.BlockSpec((2, tm, tn), lambda i, j: (0, i, j)),
                   pl.BlockSpec((tm, tn), lambda i, j: (i, j))],
        out_shape=[SDS((2, t, D_FF), BF16), SDS((2, t, D_FF), BF16), SDS((t, D_FF), BF16)],
        scratch_shapes=[pltpu.VMEM((nj, 2, 8, tn), F32)],
        compiler_params=_params("arbitrary", "arbitrary"))(h, w_up_t, w_up_t, cw, cb)


def _ffn_bwd(dx2, w_down, u, c, cw, tm=512, tn=1408):
    t = u.shape[1]
    nj, ni = D_FF // tn, t // tm
    rev = lambda i: ni - 1 - i

    def body(dx_ref, wd_ref, u_ref, c_ref, w_ref, du_ref, dcw_ref, dcb_ref, nxt):
        i = pl.program_id(1)

        @pl.when(i == 0)
        def _():
            nxt[...] = jnp.zeros_like(nxt)
            dcw_ref[...] = jnp.zeros_like(dcw_ref)
            dcb_ref[...] = jnp.zeros_like(dcb_ref)

        df = _dot(dx_ref[...].astype(BF16), wd_ref[...], NT)
        gate, val = c_ref[0].astype(F32), c_ref[1].astype(F32)
        sg = _sigmoid(gate)
        dgate = df * val * (sg * (1.0 + gate * (1.0 - sg)))
        dval = df * (gate * sg)
        for part, d in enumerate((dgate, dval)):
            uu = u_ref[part].astype(F32)
            dcb_ref[part] += jnp.sum(d, axis=0, keepdims=True)
            ahead = nxt[part]
            acc = w_ref[2, part] * d
            dcw_ref[2, part] += jnp.sum(d * uu, axis=0, keepdims=True)
            for k in range(2):
                d_up = _shift_up(d, ahead, 2 - k)
                dcw_ref[k, part] += jnp.sum(uu * d_up, axis=0, keepdims=True)
                acc = acc + w_ref[k, part] * d_up
            nxt[part] = d[0:8, :]
            du_ref[part] = acc.astype(BF16)

    w_spec = pl.BlockSpec((3, 2, 1, tn), lambda j, i: (0, 0, 0, j))
    b_spec = pl.BlockSpec((2, 1, tn), lambda j, i: (0, 0, j))
    tile = pl.BlockSpec((2, tm, tn), lambda j, i: (0, rev(i), j))
    return pl.pallas_call(
        body, name="ffn_bwd", grid=(nj, ni),
        in_specs=[pl.BlockSpec((tm, D_MODEL), lambda j, i: (rev(i), 0)), pl.BlockSpec((tn, D_MODEL), lambda j, i: (j, 0)),
                  tile, tile, w_spec],
        out_specs=[tile, w_spec, b_spec],
        out_shape=[SDS((2, t, D_FF), BF16), SDS((3, 2, 1, D_FF), F32), SDS((2, 1, D_FF), F32)],
        scratch_shapes=[pltpu.VMEM((2, 8, tn), F32)],
        compiler_params=_params("parallel", "arbitrary"))(dx2, w_down, u, c, cw)


def _down_ple_loss(x1, f, w_down, g, w_gate, p, w_proj_t, target, tm=512):
    t = x1.shape[0]

    def body(x_ref, f_ref, wd_ref, g_ref, wg_ref, p_ref, wp_ref, tg_ref, dx_ref, dpre_ref, dpp_ref, h_ref, loss_ref,
             dg_ref):
        i = pl.program_id(0)
        xv = x_ref[...] + _dot(f_ref[...], wd_ref[...])
        r = lax.rsqrt(jnp.mean(xv * xv, axis=-1, keepdims=True) + EPS)
        xh = xv * r
        h = (xh * g_ref[...]).astype(BF16)
        h_ref[...] = h
        gate = _sigmoid(_dot(h, wg_ref[...]))
        pp = _dot(p_ref[...].astype(BF16), wp_ref[...], NT)
        err = (xv + gate * pp) - tg_ref[...]

        @pl.when(i == 0)
        def _():
            loss_ref[...] = jnp.zeros_like(loss_ref)
            dg_ref[...] = jnp.zeros_like(dg_ref)

        loss_ref[...] += 0.5 * jnp.sum(jnp.mean(err * err, axis=-1, keepdims=True), axis=0, keepdims=True)
        dy = err * (1.0 / D_MODEL)
        dpre = (dy * pp * gate * (1.0 - gate)).astype(BF16)
        dpre_ref[...] = dpre
        dpp_ref[...] = (dy * gate).astype(BF16)
        dh = _dot(dpre, wg_ref[...], NT)
        dg_ref[...] += jnp.sum(dh * xh, axis=0, keepdims=True)
        gd = dh * g_ref[...]
        dx_ref[...] = dy + r * (gd - xh * jnp.mean(gd * xh, axis=-1, keepdims=True))

    row = lambda w: pl.BlockSpec((tm, w), lambda i: (i, 0))
    full = lambda a: pl.BlockSpec(a.shape, lambda i: (0, 0))
    return pl.pallas_call(
        body, name="down_ple_loss", grid=(t // tm,),
        in_specs=[row(D_MODEL), row(D_FF), full(w_down), full(g), full(w_gate), row(PLE_DIM), full(w_proj_t),
                  row(D_MODEL)],
        out_specs=[row(D_MODEL), row(D_MODEL), row(D_MODEL), row(D_MODEL),
                   pl.BlockSpec((1, 128), lambda i: (0, 0)), pl.BlockSpec((1, D_MODEL), lambda i: (0, 0))],
        out_shape=[SDS((t, D_MODEL), F32), SDS((t, D_MODEL), BF16), SDS((t, D_MODEL), BF16), SDS((t, D_MODEL), BF16),
                   SDS((1, 128), F32), SDS((1, D_MODEL), F32)],
        compiler_params=_params("arbitrary"))(x1, f, w_down, g, w_gate, p, w_proj_t, target)


def _all_gather(arrays, name):
    n_a = len(arrays)

    def body(*refs):
        src, dst = refs[:n_a], refs[n_a:2 * n_a]
        send_sems, recv_sems, local_sems = refs[2 * n_a:]
        x, y, c = lax.axis_index("x"), lax.axis_index("y"), lax.axis_index("c")
        slot = lambda px, py, pc: 4 * px + 2 * py + pc
        me, sibling = (x, y, c), (x, y, 1 - c)
        chips = [(1 - x, y), (x, 1 - y), (1 - x, 1 - y)]

        def copy(a, k, block, to, own=False):
            return pltpu.make_async_remote_copy(
                src_ref=src[a] if own else dst[a].at[slot(*block)], dst_ref=dst[a].at[slot(*block)],
                send_sem=send_sems.at[a, k], recv_sem=recv_sems.at[a, k], device_id=to,
                device_id_type=pl.DeviceIdType.MESH)

        local = [pltpu.make_async_copy(src[a], dst[a].at[slot(*me)], local_sems.at[a]) for a in range(n_a)]
        for cp in local:
            cp.start()
        sends = []
        for a in range(n_a):
            sends.append(copy(a, 0, me, sibling, own=True))
            sends += [copy(a, 1 + j, me, (*chip, c), own=True) for j, chip in enumerate(chips)]
        for cp in sends:
            cp.start()
        for j, chip in enumerate(chips):
            for a in range(n_a):
                copy(a, 1 + j, (*chip, c), me).wait_recv()
                passed = copy(a, 4 + j, (*chip, c), sibling)
                passed.start()
                sends.append(passed)
        for a in range(n_a):
            copy(a, 0, sibling, me).wait_recv()
            for j, chip in enumerate(chips):
                copy(a, 4 + j, (*chip, 1 - c), me).wait_recv()
        for cp in sends:
            cp.wait_send()
        for cp in local:
            cp.wait()

    hbm = pl.BlockSpec(memory_space=pl.ANY)
    return pl.pallas_call(
        body, name=name, in_specs=[hbm] * n_a, out_specs=[hbm] * n_a,
        out_shape=[SDS((N_DEV,) + a.shape, a.dtype) for a in arrays],
        scratch_shapes=[pltpu.SemaphoreType.DMA((n_a, N_DEV - 1)), pltpu.SemaphoreType.DMA((n_a, N_DEV - 1)),
                        pltpu.SemaphoreType.DMA((n_a,))],
        )(*arrays)


def _peer(k):
    x, y, c = lax.axis_index("x"), lax.axis_index("y"), lax.axis_index("c")
    px = 1 - x if k & 4 else x
    py = 1 - y if k & 2 else y
    pc = 1 - c if k & 1 else c
    return (px, py, pc), 4 * px + 2 * py + pc


_HBM = pl.BlockSpec(memory_space=pltpu.HBM)
_SEM = pl.BlockSpec(memory_space=pltpu.SEMAPHORE)


def _split_copies(src, land, send_sems, recv_sems, scatter, arrivals):
    _, me = _peer(0)
    out = []
    for k in range(1, N_DEV):
        coords, peer = _peer(k)
        for a in range(len(src)):
            sem = a * (N_DEV - 1) + k - 1
            if scatter[a]:
                s, d = src[a].at[peer], land[a].at[k]
            else:
                s, d = src[a], land[a].at[peer if arrivals else me]
            out.append(pltpu.make_async_remote_copy(
                src_ref=s, dst_ref=d, send_sem=send_sems.at[sem], recv_sem=recv_sems.at[sem], device_id=coords,
                device_id_type=pl.DeviceIdType.MESH))
    return out


def _exchange_start(srcs, lands, scatter, name):
    n = len(srcs)

    def body(*refs):
        src, land = refs[:n], refs[n:2 * n]
        send_sems, recv_sems = refs[2 * n], refs[2 * n + 1]
        token = refs[-1]
        for cp in _split_copies(src, land, send_sems, recv_sems, scatter, False):
            cp.start()
        token[...] = jnp.zeros_like(token)

    hbm_shape = lambda a: pltpu.HBM(a.shape, a.dtype)
    sem_shape = pltpu.SemaphoreType.DMA((n * (N_DEV - 1),))
    outs = pl.pallas_call(
        body, name=name,
        out_shape=(sem_shape, sem_shape, *[hbm_shape(a) for a in srcs], *[hbm_shape(a) for a in lands],
                   SDS((8, 128), F32)),
        in_specs=[_HBM] * (2 * n), out_specs=(_SEM, _SEM, *[_HBM] * (2 * n), pl.BlockSpec(memory_space=pltpu.VMEM)),
        input_output_aliases={a: 2 + a for a in range(2 * n)},
        compiler_params=pltpu.CompilerParams(has_side_effects=pltpu.SideEffectType.DATAFLOW_SIDE_EFFECTING),
    )(*[pltpu.with_memory_space_constraint(a, pltpu.HBM) for a in list(srcs) + list(lands)])
    return outs[0], outs[1], outs[2:2 + n], outs[2 + n:2 + 2 * n], outs[-1]


def _exchange_wait(send_sems, recv_sems, srcs, lands, scatter, after, name):
    n = len(srcs)

    def body(*refs):
        src, land = refs[:n], refs[n:2 * n]
        for cp in _split_copies(src, land, refs[2 * n], refs[2 * n + 1], scatter, False):
            cp.wait_send()
        for cp in _split_copies(src, land, refs[2 * n], refs[2 * n + 1], scatter, True):
            cp.wait_recv()

    hbm_shape = lambda a: pltpu.HBM(a.shape, a.dtype)
    outs = pl.pallas_call(
        body, name=name, out_shape=tuple(hbm_shape(a) for a in list(srcs) + list(lands)),
        in_specs=[_HBM] * (2 * n) + [_SEM, _SEM, pl.BlockSpec(memory_space=pl.ANY)], out_specs=(_HBM,) * (2 * n),
        input_output_aliases={a: a for a in range(2 * n)},
        compiler_params=pltpu.CompilerParams(has_side_effects=pltpu.SideEffectType.DATAFLOW_SIDE_EFFECTING),
    )(*srcs, *lands, send_sems, recv_sems, after)
    return outs[:n], outs[n:]


def _reduce8(a, tr, name):
    _, rows, cols = a.shape

    def body(a_ref, o_ref):
        acc = a_ref[0]
        for j in range(1, N_DEV):
            acc = acc + a_ref[j]
        o_ref[...] = acc

    return pl.pallas_call(
        body, name=name, grid=(rows // tr,),
        in_specs=[pl.BlockSpec((N_DEV, tr, cols), lambda i: (0, i, 0))],
        out_specs=pl.BlockSpec((tr, cols), lambda i: (i, 0)), out_shape=SDS((rows, cols), F32),
        compiler_params=_params("parallel"))(a)


def _reduce_landed(own, land, name, tc=256):
    rows, cols = own.shape

    def body(own_ref, land_ref, o_ref):
        acc = own_ref[...].astype(F32)
        for k in range(1, N_DEV):
            acc = acc + land_ref[k].astype(F32)
        o_ref[...] = acc

    return pl.pallas_call(
        body, name=name, grid=(cols // tc,),
        in_specs=[pl.BlockSpec((rows, tc), lambda j: (0, j)), pl.BlockSpec((N_DEV, rows, tc), lambda j: (0, 0, j))],
        out_specs=pl.BlockSpec((rows, tc), lambda j: (0, j)), out_shape=SDS((rows, cols), F32),
        compiler_params=_params("parallel"))(own, land)


def _reduce_adamw(own, land, w, m, v, name, tc=256):
    rows, cols = own.shape

    def body(own_ref, land_ref, w_ref, m_ref, v_ref, g_ref, d_ref, mo_ref, vo_ref):
        g = own_ref[...].astype(F32)
        for k in range(1, N_DEV):
            g = g + land_ref[k].astype(F32)
        g_ref[...] = g
        d_ref[...], mo_ref[...], vo_ref[...] = _adam_update(w_ref[...], g, m_ref[...], v_ref[...])

    blk = pl.BlockSpec((rows, tc), lambda j: (0, j))
    return pl.pallas_call(
        body, name=name, grid=(cols // tc,),
        in_specs=[blk, pl.BlockSpec((N_DEV, rows, tc), lambda j: (0, 0, j)), blk, blk, blk], out_specs=[blk] * 4,
        out_shape=[SDS((rows, cols), F32)] * 4, compiler_params=_params("parallel"))(own, land, w, m, v)


def _adamw(w, g, m, v, name, tr=None):
    rows, cols = w.shape
    tr = rows if tr is None else tr

    def body(w_ref, g_ref, m_ref, v_ref, d_ref, mo_ref, vo_ref):
        d_ref[...], mo_ref[...], vo_ref[...] = _adam_update(w_ref[...], g_ref[...], m_ref[...], v_ref[...])

    blk = pl.BlockSpec((tr, cols), lambda i: (i, 0))
    return pl.pallas_call(
        body, name=name, grid=(rows // tr,), in_specs=[blk] * 4, out_specs=[blk] * 3,
        out_shape=[SDS((rows, cols), F32)] * 3, compiler_params=_params("parallel"))(w, g, m, v)


def _pad_rows(a, rows):
    return jnp.pad(a, ((0, rows - a.shape[0]),) + ((0, 0),) * (a.ndim - 1))


def _local_step(x, p, target, sm, wts, fetch_rest, send, tok):
    ones_q, ones_k, dup, dup_t = _head_consts()
    tri, triu, expand, expand_t = _ssd_consts()
    w_in_t = wts["in_t"]
    cwx, cwb = wts["ssm_cw"][:, :SSM_INNER], wts["ssm_cw"][:, SSM_INNER:]
    cbx, cbb = sm["ssm_conv_b"][:, :SSM_INNER], sm["ssm_conv_b"][:, SSM_INNER:]
    pad128 = lambda a: jnp.pad(a, ((0, 0), (0, 128 - a.shape[1])))
    dtb, alog = pad128(sm["dt_bias"]), pad128(sm["a_log"])
    dsk_e = jnp.repeat(sm["d_skip"], HEAD_DIM, axis=1)
    gq = jnp.tile(sm["q_norm_g"], (1, ATTN_DIM // HEAD_DIM))
    gk = jnp.tile(sm["k_norm_g"], (1, KV_DIM // HEAD_DIM))
    ffn_cw = wts["ffn_cw"].reshape(3, 2, 1, D_FF)
    ffn_cb = sm["ffn_conv_b"].reshape(2, 1, D_FF)

    proj, h1 = _norm_matmul(x, sm["attn_norm_g"] + tok, w_in_t, 512, 3840, "in_proj")
    qn, kd, vd = _attn_prep(proj, gq, gk, ones_q, ones_k, dup)
    attn_out, lse = _attn_fwd(qn, kd, vd)
    y_ssd, hs, ssm_out, pre_x, pre_b = _ssd_fwd(proj, cwx, cbx, cwb, cbb, dtb, alog, dsk_e, sm["ssm_norm_g"], tri,
                                                expand)
    rest = fetch_rest(ssm_out)
    w_out, w_up_t, w_down, w_gate, w_proj_t = (rest[k] for k in ("out", "up_t", "down", "gate", "proj_t"))
    x1, h2 = _mm_resid_norm([(attn_out, None, w_out[:ATTN_DIM]), (ssm_out, None, w_out[ATTN_DIM:])], x,
                            sm["ffn_norm_g"], 1024, "out_proj")
    u, uc, f = _up_act(h2, w_up_t, ffn_cw, ffn_cb)
    dx2, dpre, dpp, h3, loss, dg_ple = _down_ple_loss(x1, f, w_down, sm["ple_norm_g"], w_gate, p, w_proj_t, target)

    g_gate = _wgrad(h3, None, dpre, "wg_gate")
    g_proj_t = _wgrad(dpp, None, p, "wg_proj")
    g_down = _wgrad(f, None, dx2, "wg_down")
    du, d_ffn_cw, d_ffn_cb = _ffn_bwd(dx2, w_down, u, uc, ffn_cw)
    dx1, dg_ffn = _mm_normbwd([(du, 0, w_up_t, D_FF, 0), (du, 1, w_up_t, D_FF, 1)], x1, sm["ffn_norm_g"], dx2, 512,
                              "up_proj_bwd")
    g_up_t = _wgrad(du, "all", h2, "wg_up")
    tok = send(dict(gate=g_gate, proj_t=g_proj_t, down=g_down, up_t=g_up_t)).astype(BF16)
    d_attn, d_ssm = _out_proj_bwd(dx1, w_out + tok)
    g_out = _wgrad_multi([attn_out, ssm_out], dx1, "wg_out")
    tok = send(dict(out=g_out))
    (dz, dxs, dbc, ddt, dg_ssm, d_dsk_e, d_alog, d_dtb, d_cwx, d_cbx, d_cwb, d_cbb) = _ssd_bwd(
        proj, pre_x, pre_b, y_ssd, hs, d_ssm, cwx, cwb, dtb + tok, alog, dsk_e, sm["ssm_norm_g"], tri, triu, expand,
        expand_t)
    dqn, dkc, dkp, dvc, dvp = _attn_bwd(qn, kd, vd, attn_out, lse, d_attn, ones_k[:128, :128])
    dqkv, dgq, dgk = _attn_prep_bwd(proj, dqn, dkc, dkp, dvc, dvp, gq + tok, gk, ones_q, ones_k, dup_t)
    pieces = [(dqkv, 0, 1024), (dz, 1024, 2048), (dxs, 2048, 3072), (dbc, 3072, 3584), (ddt, 3584, 3712)]
    g_in_t = jnp.concatenate([_wgrad_multi([dqkv, dz], h1, "wg_in_qkvz"),
                              _wgrad_multi([dxs, dbc, ddt], h1, "wg_in_xbcdt")], axis=0)[:IN_PROJ]
    tok = send(dict(in_t=g_in_t))
    grad_x, dg_attn = _mm_normbwd([(a, None, w_in_t, hi - lo, lo // (hi - lo)) for a, lo, hi in pieces], x,
                                  sm["attn_norm_g"] + tok, dx1, 512, "in_proj_bwd")

    small = dict(
        attn_norm_g=dg_attn, q_norm_g=dgq.reshape(-1, HEAD_DIM).sum(0, keepdims=True),
        k_norm_g=dgk.reshape(-1, HEAD_DIM).sum(0, keepdims=True),
        ssm_conv_w=jnp.concatenate([d_cwx, d_cwb], axis=1), ssm_conv_b=jnp.concatenate([d_cbx, d_cbb], axis=1),
        dt_bias=d_dtb[:, :SSM_HEADS], a_log=d_alog[:, :SSM_HEADS],
        d_skip=d_dsk_e.reshape(SSM_HEADS, HEAD_DIM).sum(1)[None, :], ssm_norm_g=dg_ssm, ffn_norm_g=dg_ffn,
        ffn_conv_w=d_ffn_cw.reshape(3, 2 * D_FF), ffn_conv_b=d_ffn_cb.reshape(1, 2 * D_FF), ple_norm_g=dg_ple)
    return loss[0, 0], grad_x, small


_SMALL = (("attn_norm_g", 1, 1024), ("q_norm_g", 1, 64), ("k_norm_g", 1, 64), ("ssm_conv_w", 4, XBC_DIM),
          ("ssm_conv_b", 1, XBC_DIM), ("dt_bias", 1, 16), ("a_log", 1, 16), ("d_skip", 1, 16), ("ssm_norm_g", 1, 1024),
          ("ffn_norm_g", 1, 1024), ("ffn_conv_w", 3, 2 * D_FF), ("ffn_conv_b", 1, 2 * D_FF), ("ple_norm_g", 1, 1024))
_SMALL_ROWS, _SMALL_COLS = 32, XBC_DIM
_SHARDED_SMALL = ("ssm_conv_w", "ffn_conv_w")


def _small_chunks(n):
    return 1 if n <= _SMALL_COLS else 4


def _pack_small(parts, loss):
    rows = []
    for k, r, n in _SMALL:
        c = _small_chunks(n)
        rows.append(jnp.pad(parts[k].reshape(r * c, n // c), ((0, 0), (0, _SMALL_COLS - n // c))))
    packed = _pad_rows(jnp.concatenate(rows, axis=0), _SMALL_ROWS)
    at_loss = ((lax.broadcasted_iota(jnp.int32, packed.shape, 0) == _SMALL_ROWS - 1) &
               (lax.broadcasted_iota(jnp.int32, packed.shape, 1) == 0))
    return jnp.where(at_loss, loss, packed)


def _adam_update(w, g, m, v):
    c1 = 1.0 - ADAM_B1 ** ADAM_STEP
    c2 = 1.0 - ADAM_B2 ** ADAM_STEP
    mn = ADAM_B1 * m + (1.0 - ADAM_B1) * g
    vn = ADAM_B2 * v + (1.0 - ADAM_B2) * (g * g)
    return -ADAM_LR * ((mn / c1) / (jnp.sqrt(vn / c2) + ADAM_EPS) + ADAM_WD * w), mn, vn


def _adamw_small(g_all, g_shard, w, m, v):
    ins, shapes = [g_all], []
    for k, _, _ in _SMALL:
        shape2 = w[k].shape if w[k].ndim == 2 else (1, w[k].shape[0])
        shapes.append(shape2)
        ins += ([g_shard[k]] if k in _SHARDED_SMALL else []) + [a.reshape(shape2) for a in (w[k], m[k], v[k])]

    def body(*refs):
        g_ref, pos, row = refs[0], 1, 0
        outs = refs[len(ins):]
        for i, (k, r, n) in enumerate(_SMALL):
            c = _small_chunks(n)
            if k in _SHARDED_SMALL:
                g = refs[pos][...]
                pos += 1
            elif c == 1:
                g = g_ref[row:row + r, 0:n]
            else:
                g = jnp.concatenate([g_ref[row + j:row + j + 1, 0:n // c] for j in range(c)], axis=1)
            row += r * c
            d, mn, vn = _adam_update(refs[pos][...], g, refs[pos + 1][...], refs[pos + 2][...])
            pos += 3
            for o_ref, val in zip(outs[4 * i:4 * i + 4], (g, d, mn, vn)):
                o_ref[...] = val

    res = pl.pallas_call(body, name="adamw_small",
                         out_shape=[SDS(s, F32) for s in shapes for _ in range(4)])(*ins)
    return {k: tuple(a.reshape(w[k].shape) for a in res[4 * i:4 * i + 4]) for i, (k, _, _) in enumerate(_SMALL)}


def kernel(x, p, attn_norm_g, w_in, q_norm_g, k_norm_g, ssm_conv_w, ssm_conv_b, dt_bias, a_log, d_skip, ssm_norm_g, w_out, ffn_norm_g, w_up, ffn_conv_w, ffn_conv_b, w_down, ple_norm_g, w_ple_gate, w_ple_proj, loss_target, m_attn_norm_g, m_w_in, m_q_norm_g, m_k_norm_g, m_ssm_conv_w, m_ssm_conv_b, m_dt_bias, m_a_log, m_d_skip, m_ssm_norm_g, m_w_out, m_ffn_norm_g, m_w_up, m_ffn_conv_w, m_ffn_conv_b, m_w_down, m_ple_norm_g, m_w_ple_gate, m_w_ple_proj, v_attn_norm_g, v_w_in, v_q_norm_g, v_k_norm_g, v_ssm_conv_w, v_ssm_conv_b, v_dt_bias, v_a_log, v_d_skip, v_ssm_norm_g, v_w_out, v_ffn_norm_g, v_w_up, v_ffn_conv_w, v_ffn_conv_b, v_w_down, v_ple_norm_g, v_w_ple_gate, v_w_ple_proj):
    names = ("attn_norm_g", "w_in", "q_norm_g", "k_norm_g", "ssm_conv_w", "ssm_conv_b", "dt_bias", "a_log", "d_skip",
             "ssm_norm_g", "w_out", "ffn_norm_g", "w_up", "ffn_conv_w", "ffn_conv_b", "w_down", "ple_norm_g",
             "w_ple_gate", "w_ple_proj")
    w = dict(zip(names, (attn_norm_g, w_in, q_norm_g, k_norm_g, ssm_conv_w, ssm_conv_b, dt_bias, a_log, d_skip,
                         ssm_norm_g, w_out, ffn_norm_g, w_up, ffn_conv_w, ffn_conv_b, w_down, ple_norm_g, w_ple_gate,
                         w_ple_proj)))
    m = dict(zip(names, (m_attn_norm_g, m_w_in, m_q_norm_g, m_k_norm_g, m_ssm_conv_w, m_ssm_conv_b, m_dt_bias,
                         m_a_log, m_d_skip, m_ssm_norm_g, m_w_out, m_ffn_norm_g, m_w_up, m_ffn_conv_w, m_ffn_conv_b,
                         m_w_down, m_ple_norm_g, m_w_ple_gate, m_w_ple_proj)))
    v = dict(zip(names, (v_attn_norm_g, v_w_in, v_q_norm_g, v_k_norm_g, v_ssm_conv_w, v_ssm_conv_b, v_dt_bias,
                         v_a_log, v_d_skip, v_ssm_norm_g, v_w_out, v_ffn_norm_g, v_w_up, v_ffn_conv_w, v_ffn_conv_b,
                         v_w_down, v_ple_norm_g, v_w_ple_gate, v_w_ple_proj)))
    w, m, v = ({k: a[0] for k, a in d.items()} for d in (w, m, v))
    me = 4 * lax.axis_index("x") + 2 * lax.axis_index("y") + lax.axis_index("c")

    mine = dict(in_t=w["w_in"].T, out=w["w_out"], up_t=w["w_up"].T, down=w["w_down"], gate=w["w_ple_gate"],
                proj_t=w["w_ple_proj"].T)
    mine = {k: a.astype(BF16) for k, a in mine.items()}
    conv_pack = jnp.pad(jnp.concatenate([w["ssm_conv_w"].reshape(-1), w["ffn_conv_w"].reshape(-1)]),
                        (0, 3072 - 2880)).reshape(8, 384)
    all_in, all_conv = _all_gather([mine["in_t"], conv_pack], "gather_first")
    later = ("out", "up_t", "down", "gate", "proj_t")
    zones = [lax.dynamic_update_slice(lax.empty((N_DEV,) + mine[k].shape, BF16), mine[k][None], (me, 0, 0))
             for k in later]
    zones, all_in, all_conv = lax.optimization_barrier((zones, all_in, all_conv))
    rest_state = _exchange_start([mine[k] for k in later], zones, [False] * len(later), "gather_rest_start")

    def fetch_rest(after):
        _, landed = _exchange_wait(*rest_state[:4], [False] * len(later), after, "gather_rest_wait")
        return {k: a.reshape(N_DEV * a.shape[1], a.shape[2]) for k, a in zip(later, landed)}

    wts = dict(in_t=_pad_rows(all_in.reshape(IN_PROJ, D_MODEL), IN_PROJ_PAD))
    conv_flat = all_conv.reshape(N_DEV, 3072)
    wts["ssm_cw"] = conv_flat[:, :768].reshape(N_DEV, 4, XBC_DIM // N_DEV).transpose(1, 0, 2).reshape(4, XBC_DIM)
    wts["ffn_cw"] = conv_flat[:, 768:2880].reshape(N_DEV, 3, 2 * D_FF // N_DEV).transpose(1, 0, 2).reshape(3, 2 * D_FF)
    sm = {k: w[k].reshape(1, -1) for k, _, _ in _SMALL if k not in _SHARDED_SMALL}

    in_flight = []

    def send(grads):
        keys = sorted(grads)
        srcs = [grads[k].reshape(N_DEV, grads[k].shape[0] // N_DEV, grads[k].shape[1]) for k in keys]
        state = _exchange_start(srcs, [lax.empty(a.shape, BF16) for a in srcs], [True] * len(keys),
                                "send_" + "_".join(keys))
        in_flight.append((keys, state))
        return state[4][0:1, 0:1]

    loss, grad_x, small = _local_step(x[0], p[0, 0], loss_target[0], sm, wts, fetch_rest, send,
                                      rest_state[4][0:1, 0:1])

    (got_small,) = _all_gather([_pack_small(small, loss)], "gather_small_grads")
    g_small = _reduce8(got_small, _SMALL_ROWS, "reduce_small")
    loss = g_small[_SMALL_ROWS - 1, 0]
    grads, gw, delta, new_m, new_v = {}, {}, {}, {}, {}
    row_sharded = {"out": "w_out", "down": "w_down", "gate": "w_ple_gate"}
    for keys, state in in_flight:
        sent, landed = _exchange_wait(*state[:4], [True] * len(keys), grad_x, "wait_" + "_".join(keys))
        for k, shares, land in zip(keys, sent, landed):
            own = lax.dynamic_index_in_dim(shares, me, 0, keepdims=False)
            if k in row_sharded:
                n = row_sharded[k]
                gw[n], delta[n], new_m[n], new_v[n] = _reduce_adamw(own, land, w[n], m[n], v[n], "update_" + n)
            else:
                grads[k] = _reduce_landed(own, land, "reduce_" + k)
    gw.update({"w_in": grads["in_t"].T, "w_up": grads["up_t"].T, "w_ple_proj": grads["proj_t"].T})
    n_ssm, n_ffn = XBC_DIM // N_DEV, 2 * D_FF // N_DEV
    g_shard = {"ssm_conv_w": lax.dynamic_slice(g_small, (3, me * n_ssm), (4, n_ssm)),
               "ffn_conv_w": lax.dynamic_slice(g_small[13:25, :2 * D_FF // 4].reshape(3, 2 * D_FF), (0, me * n_ffn),
                                               (3, n_ffn))}

    for k, tr in (("w_in", 256), ("w_up", 256), ("w_ple_proj", None)):
        delta[k], new_m[k], new_v[k] = _adamw(w[k], gw[k], m[k], v[k], "adamw_" + k, tr)
    for k, (g_k, d_k, m_k, v_k) in _adamw_small(g_small, g_shard, w, m, v).items():
        gw[k], delta[k], new_m[k], new_v[k] = g_k, d_k, m_k, v_k

    outs = [loss, grad_x[None]]
    for d in (gw, delta, new_m, new_v):
        outs += [d[k][None] for k in names]
    return tuple(outs)
```
